```python
import jax, jax.numpy as jnp
from jax import lax
import numpy as np

D_MODEL = 1024
BATCH = 8
SEQ = 2048
DEPTH = 1

N_META = 16
CONV_WIDTH = 512
CONV_K = 31
N_HEADS = 8
HEAD_DIM = 64
ATTN_WIDTH = N_HEADS * HEAD_DIM
Q_BLOCK = 128
N_GROUPS = 8
EXPERTS_PER_GROUP = 8
N_EXPERTS = N_GROUPS * EXPERTS_PER_GROUP
TOP_K = 2
D_EXPERT = 256
ROW_BLOCK = 256
RMS_EPS = 1e-6
LN_EPS = 1e-5

IN_SPLITS = (2 * CONV_WIDTH, ATTN_WIDTH, ATTN_WIDTH, ATTN_WIDTH, N_HEADS, D_MODEL, D_MODEL)
IN_COLS = sum(IN_SPLITS)

kernel_name = "hybrid_conformer_fox_hmoe_block"


def rmsnorm(x, g):
    xf = x.astype(jnp.float32)
    y = xf * lax.rsqrt(jnp.mean(xf * xf, axis=-1, keepdims=True) + RMS_EPS)
    return (y * g.astype(jnp.float32)).astype(x.dtype)


def layernorm(x, g, b):
    xf = x.astype(jnp.float32)
    mu = jnp.mean(xf, axis=-1, keepdims=True)
    var = jnp.mean(jnp.square(xf - mu), axis=-1, keepdims=True)
    y = (xf - mu) * lax.rsqrt(var + LN_EPS)
    return (y * g.astype(jnp.float32) + b.astype(jnp.float32)).astype(x.dtype)


def conformer_conv(u, dw_w, dw_b, ln_g, ln_b, w_pw, b_pw):
    a = u[..., :CONV_WIDTH] * jax.nn.sigmoid(u[..., CONV_WIDTH:])
    a = lax.conv_general_dilated(a, dw_w[:, None, :], window_strides=(1,),
                                 padding=[(CONV_K - 1, 0)],
                                 dimension_numbers=('NWC', 'WIO', 'NWC'),
                                 feature_group_count=CONV_WIDTH) + dw_b
    a = jax.nn.silu(layernorm(a, ln_g, ln_b))
    return a @ w_pw + b_pw


def forgetting_attention(q, k, v, logf):
    L = q.shape[1]
    c = jnp.cumsum(logf, axis=1).transpose(0, 2, 1)
    scale = HEAD_DIM ** -0.5

    def block(q0, q1):
        t = jnp.arange(q0, q1)
        s = jnp.arange(q1)
        logits = jnp.einsum('bqhd,bkhd->bhqk', q[:, q0:q1], k[:, :q1]).astype(jnp.float32) * scale
        logits = logits + c[:, :, q0:q1, None] - c[:, :, None, :q1]
        logits = jnp.where(s[None, :] <= t[:, None], logits, -jnp.inf)
        p = jax.nn.softmax(logits, axis=-1)
        return jnp.einsum('bhqk,bkhd->bqhd', p.astype(v.dtype), v[:, :q1])

    n_blocks = (L - N_META) // Q_BLOCK
    bounds = [(0, N_META)] + [(N_META + i * Q_BLOCK, N_META + (i + 1) * Q_BLOCK) for i in range(n_blocks)]
    return jnp.concatenate([block(q0, q1) for (q0, q1) in bounds], axis=1)


def hierarchical_moe(xt, w_group, b_group, w_router, b_router, w_gate, w_up, w_down):
    N = xt.shape[0]
    g_logits = (xt @ w_group).astype(jnp.float32) + b_group.astype(jnp.float32)
    g_prob = jax.nn.softmax(g_logits, axis=-1)
    g_idx = jnp.argmax(g_logits, axis=-1).astype(jnp.int32)
    g_w = jnp.take_along_axis(g_prob, g_idx[:, None], axis=1)
    e_logits = ((xt @ w_router).astype(jnp.float32) + b_router.astype(jnp.float32)).reshape(N, N_GROUPS, EXPERTS_PER_GROUP)
    e_logits = jnp.take_along_axis(e_logits, g_idx[:, None, None], axis=1)[:, 0]
    top_l, top_i = lax.top_k(e_logits, TOP_K)
    w = jax.nn.softmax(top_l, axis=-1) * g_w

    A = N * TOP_K
    e_id = (g_idx[:, None] * EXPERTS_PER_GROUP + top_i.astype(jnp.int32)).reshape(A)
    tok = jnp.repeat(jnp.arange(N, dtype=jnp.int32), TOP_K)
    wa = w.reshape(A)

    order = jnp.argsort(e_id)
    se, stok, sw = e_id[order], tok[order], wa[order]
    counts = jnp.zeros((N_EXPERTS,), jnp.int32).at[e_id].add(1)
    start = jnp.cumsum(counts) - counts
    padded = (counts + ROW_BLOCK - 1) // ROW_BLOCK * ROW_BLOCK
    pend = jnp.cumsum(padded)
    pstart = pend - padded
    dest = pstart[se] + jnp.arange(A, dtype=jnp.int32) - start[se]

    n_blocks = -(-A // ROW_BLOCK) + N_EXPERTS
    P = n_blocks * ROW_BLOCK
    buf_tok = jnp.full((P,), N, jnp.int32).at[dest].set(stok)
    buf_w = jnp.zeros((P,), jnp.float32).at[dest].set(sw)
    blk_e = jnp.minimum(jnp.searchsorted(pend, jnp.arange(n_blocks, dtype=jnp.int32) * ROW_BLOCK, side='right'),
                        N_EXPERTS - 1).astype(jnp.int32)

    x_pad = jnp.concatenate([xt, jnp.zeros((1, xt.shape[1]), xt.dtype)], axis=0)
    xin = x_pad[buf_tok].reshape(n_blocks, ROW_BLOCK, xt.shape[1])

    def expert_block(args):
        xb, e = args
        hb = jax.nn.silu(xb @ w_gate[e]) * (xb @ w_up[e])
        return hb @ w_down[e]

    yb = lax.map(expert_block, (xin, blk_e)).reshape(P, xt.shape[1])
    y = yb.astype(jnp.float32) * buf_w[:, None]
    out = jax.ops.segment_sum(y, buf_tok, num_segments=N + 1)[:N]
    return out.astype(xt.dtype)


def setup_inputs(seed: int = 0) -> dict:
    key = jax.random.key(seed)
    ks = jax.random.split(key, 24)
    f32 = jnp.float32
    nrm = lambda k, shape, scale: jax.random.normal(k, shape, f32) * scale
    return {
        "x": nrm(ks[0], (BATCH, SEQ, D_MODEL), 1.0),
        "meta": nrm(ks[1], (N_META, D_MODEL), 1.0),
        "norm1_g": 1.0 + nrm(ks[2], (DEPTH, D_MODEL), 0.02),
        "w_in": nrm(ks[3], (DEPTH, D_MODEL, IN_COLS), D_MODEL ** -0.5),
        "b_in": nrm(ks[4], (DEPTH, IN_COLS), 0.02),
        "dw_w": nrm(ks[5], (DEPTH, CONV_K, CONV_WIDTH), CONV_K ** -0.5),
        "dw_b": nrm(ks[6], (DEPTH, CONV_WIDTH), 0.02),
        "conv_ln_g": 1.0 + nrm(ks[7], (DEPTH, CONV_WIDTH), 0.02),
        "conv_ln_b": nrm(ks[8], (DEPTH, CONV_WIDTH), 0.02),
        "w_conv_out": nrm(ks[9], (DEPTH, CONV_WIDTH, D_MODEL), CONV_WIDTH ** -0.5),
        "b_conv_out": nrm(ks[10], (DEPTH, D_MODEL), 0.02),
        "q_norm_g": 1.0 + nrm(ks[11], (DEPTH, HEAD_DIM), 0.02),
        "k_norm_g": 1.0 + nrm(ks[12], (DEPTH, HEAD_DIM), 0.02),
        "b_forget": jax.random.uniform(ks[13], (DEPTH, N_HEADS), f32, 1.0, 3.0),
        "w_attn_out": nrm(ks[14], (DEPTH, ATTN_WIDTH, D_MODEL), ATTN_WIDTH ** -0.5),
        "w_out": nrm(ks[15], (DEPTH, D_MODEL, D_MODEL), D_MODEL ** -0.5),
        "norm2_g": 1.0 + nrm(ks[16], (DEPTH, D_MODEL), 0.02),
        "w_group": nrm(ks[17], (DEPTH, D_MODEL, N_GROUPS), D_MODEL ** -0.5),
        "b_group": nrm(ks[18], (DEPTH, N_GROUPS), 0.01),
        "w_router": nrm(ks[19], (DEPTH, D_MODEL, N_EXPERTS), D_MODEL ** -0.5),
        "b_router": nrm(ks[20], (DEPTH, N_EXPERTS), 0.01),
        "w_gate": nrm(ks[21], (DEPTH, N_EXPERTS, D_MODEL, D_EXPERT), D_MODEL ** -0.5),
        "w_up": nrm(ks[22], (DEPTH, N_EXPERTS, D_MODEL, D_EXPERT), D_MODEL ** -0.5),
        "w_down": nrm(ks[23], (DEPTH, N_EXPERTS, D_EXPERT, D_MODEL), D_EXPERT ** -0.5),
    }


def reference(x, meta, norm1_g, w_in, b_in, dw_w, dw_b, conv_ln_g, conv_ln_b, w_conv_out, b_conv_out,
              q_norm_g, k_norm_g, b_forget, w_attn_out, w_out, norm2_g, w_group, b_group, w_router,
              b_router, w_gate, w_up, w_down):
    B = x.shape[0]
    h = jnp.concatenate([jnp.broadcast_to(meta[None].astype(x.dtype), (B, N_META, D_MODEL)), x], axis=1)
    L = h.shape[1]
    split_pts = list(np.cumsum(IN_SPLITS)[:-1])
    for l in range(DEPTH):
        xn = rmsnorm(h, norm1_g[l])
        proj = xn @ w_in[l] + b_in[l]
        u_conv, q, k, v, f_logit, gate_c, gate_a = jnp.split(proj, split_pts, axis=-1)

        y_conv = conformer_conv(u_conv, dw_w[l], dw_b[l], conv_ln_g[l], conv_ln_b[l],
                                w_conv_out[l], b_conv_out[l])

        q = rmsnorm(q.reshape(B, L, N_HEADS, HEAD_DIM), q_norm_g[l])
        k = rmsnorm(k.reshape(B, L, N_HEADS, HEAD_DIM), k_norm_g[l])
        v = v.reshape(B, L, N_HEADS, HEAD_DIM)
        logf = jax.nn.log_sigmoid(f_logit.astype(jnp.float32) + b_forget[l].astype(jnp.float32))
        o = forgetting_attention(q, k, v, logf).reshape(B, L, ATTN_WIDTH)
        y_attn = o @ w_attn_out[l]

        merged = jax.nn.sigmoid(gate_c) * y_conv + jax.nn.sigmoid(gate_a) * y_attn
        h = h + merged @ w_out[l]

        hn = rmsnorm(h, norm2_g[l]).reshape(B * L, D_MODEL)
        y_moe = hierarchical_moe(hn, w_group[l], b_group[l], w_router[l], b_router[l],
                                 w_gate[l], w_up[l], w_down[l])
        h = h + y_moe.reshape(B, L, D_MODEL)
    return h[:, N_META:]
```

```python
import functools

import jax
import jax.numpy as jnp
from jax import lax
from jax.experimental import pallas as pl
from jax.experimental.pallas import tpu as pltpu

D_MODEL = 1024
BATCH = 8
SEQ = 2048
N_META = 16
CONV_WIDTH = 512
CONV_K = 31
N_HEADS = 8
HEAD_DIM = 64
ATTN_WIDTH = N_HEADS * HEAD_DIM
N_GROUPS = 8
EXPERTS_PER_GROUP = 8
N_EXPERTS = N_GROUPS * EXPERTS_PER_GROUP
D_EXPERT = 256
RMS_EPS = 1e-6
LN_EPS = 1e-5

N_TOK = BATCH * SEQ
LANES = 128
META_ROWS = 128
ROW_BLOCK = 256
N_ASSIGN = 2 * N_TOK
N_BLOCKS = N_ASSIGN // ROW_BLOCK + N_EXPERTS
P_ROWS = N_BLOCKS * ROW_BLOCK
VMEM_LIMIT = 56 * 1024 * 1024

F32 = jnp.float32
BF16 = jnp.bfloat16
NEG_INF = float("-inf")


def _dot(a, b):
    return jnp.dot(a, b, preferred_element_type=F32)


def _dot_nt(a, b):
    return lax.dot_general(a, b, (((1,), (1,)), ((), ())), preferred_element_type=F32)


def _sigmoid(x):
    return 1.0 / (1.0 + jnp.exp(-x))


def _const_spec(shape):
    nd = len(shape)
    return pl.BlockSpec(shape, lambda *_: (0,) * nd)


def _head_rms(t, hsum_ref, hexp_ref, gain):
    ss = _dot((t * t).astype(BF16), hsum_ref[...])
    inv = lax.rsqrt(ss * (1.0 / HEAD_DIM) + RMS_EPS)
    hi = inv.astype(BF16)
    lo = (inv - hi.astype(F32)).astype(BF16)
    invb = _dot(jnp.concatenate([hi, lo], axis=1), hexp_ref[...])
    return t * invb * gain


def _proj_kernel(x_ref, g1_ref, wu_ref, bu_ref, wqkv_ref, bqkv_ref, wft_ref, bft_ref, wgc_ref, bgc_ref,
                 wga_ref, bga_ref, qg_ref, kg_ref, hsum_ref, hexp_ref,
                 a_ref, q_ref, k_ref, v_ref, lf_ref, sgc_ref, sga_ref):
    x = x_ref[...]
    ms = jnp.mean(x * x, axis=-1, keepdims=True)
    xn = (x * lax.rsqrt(ms + RMS_EPS) * g1_ref[...]).astype(BF16)

    u = _dot(xn, wu_ref[...]) + bu_ref[...]
    a_ref[...] = (u[:, :CONV_WIDTH] * _sigmoid(u[:, CONV_WIDTH:])).astype(BF16)

    qkv = _dot(xn, wqkv_ref[...]) + bqkv_ref[...]
    q_ref[...] = _head_rms(qkv[:, :ATTN_WIDTH], hsum_ref, hexp_ref, qg_ref[...]).astype(BF16)
    k_ref[...] = _head_rms(qkv[:, ATTN_WIDTH:2 * ATTN_WIDTH], hsum_ref, hexp_ref, kg_ref[...]).astype(BF16)
    v_ref[...] = qkv[:, 2 * ATTN_WIDTH:].astype(BF16)

    f = _dot_nt(wft_ref[...], xn) + bft_ref[...]
    lf_ref[...] = jnp.minimum(f, 0.0) - jnp.log(1.0 + jnp.exp(-jnp.abs(f)))

    sgc_ref[...] = _sigmoid(_dot(xn, wgc_ref[...]) + bgc_ref[...]).astype(BF16)
    sga_ref[...] = _sigmoid(_dot(xn, wga_ref[...]) + bga_ref[...]).astype(BF16)


def _proj(x2d, consts, tm):
    n = x2d.shape[0]
    row = lambda w: pl.BlockSpec((tm, w), lambda i: (i, 0))
    in_specs = [row(D_MODEL)] + [_const_spec(c.shape) for c in consts]
    out_shape = [
        jax.ShapeDtypeStruct((n, CONV_WIDTH), BF16),
        jax.ShapeDtypeStruct((n, ATTN_WIDTH), BF16),
        jax.ShapeDtypeStruct((n, ATTN_WIDTH), BF16),
        jax.ShapeDtypeStruct((n, ATTN_WIDTH), BF16),
        jax.ShapeDtypeStruct((16, n), F32),
        jax.ShapeDtypeStruct((n, D_MODEL), BF16),
        jax.ShapeDtypeStruct((n, D_MODEL), BF16),
    ]
    out_specs = [row(CONV_WIDTH), row(ATTN_WIDTH), row(ATTN_WIDTH), row(ATTN_WIDTH),
                 pl.BlockSpec((16, tm), lambda i: (0, i)), row(D_MODEL), row(D_MODEL)]
    return pl.pallas_call(
        _proj_kernel,
        grid=(n // tm,),
        in_specs=in_specs,
        out_specs=out_specs,
        out_shape=out_shape,
        compiler_params=pltpu.CompilerParams(dimension_semantics=("arbitrary",), vmem_limit_bytes=VMEM_LIMIT),
        name="proj",
    )(x2d, *consts)


CONV_CHUNK = 32
CONV_PAD = 32


def _conv_kernel(a_ref, am_ref, dw_ref, dwb_ref, lng_ref, lnb_ref, o_ref, pad_scr):
    pad_scr[0:N_META, :] = jnp.zeros((N_META, CONV_WIDTH), F32)
    pad_scr[N_META:CONV_PAD, :] = am_ref[0:N_META, :].astype(F32)
    pad_scr[CONV_PAD:CONV_PAD + SEQ, :] = a_ref[0].astype(F32)

    def chunk(ci, carry):
        r0 = pl.multiple_of(ci * CONV_CHUNK, CONV_CHUNK)
        acc = jnp.zeros((CONV_CHUNK, CONV_WIDTH), F32) + dwb_ref[...]
        win = pad_scr[pl.ds(r0, 2 * CONV_CHUNK), :]
        for b in range(8):
            shifted = win if b == 0 else win[b:b + 2 * CONV_CHUNK - 8]
            for j in range(CONV_K):
                off = j + 2
                if off % 8 == b:
                    a0 = off - b
                    acc = acc + shifted[a0:a0 + CONV_CHUNK] * dw_ref[j:j + 1, :]
        mu = jnp.mean(acc, axis=-1, keepdims=True)
        d = acc - mu
        var = jnp.mean(d * d, axis=-1, keepdims=True)
        y = d * lax.rsqrt(var + LN_EPS) * lng_ref[...] + lnb_ref[...]
        o_ref[0, pl.ds(r0, CONV_CHUNK), :] = (y * _sigmoid(y)).astype(BF16)
        return carry

    lax.fori_loop(0, SEQ // CONV_CHUNK, chunk, 0)


def _conv(a3, a_meta, dw_w, dw_b, ln_g, ln_b):
    return pl.pallas_call(
        _conv_kernel,
        grid=(BATCH,),
        in_specs=[pl.BlockSpec((1, SEQ, CONV_WIDTH), lambda b: (b, 0, 0)),
                  _const_spec(a_meta.shape), _const_spec(dw_w.shape), _const_spec(dw_b.shape),
                  _const_spec(ln_g.shape), _const_spec(ln_b.shape)],
        out_specs=pl.BlockSpec((1, SEQ, CONV_WIDTH), lambda b: (b, 0, 0)),
        out_shape=jax.ShapeDtypeStruct((BATCH, SEQ, CONV_WIDTH), BF16),
        scratch_shapes=[pltpu.VMEM((CONV_PAD + SEQ, CONV_WIDTH), F32)],
        compiler_params=pltpu.CompilerParams(dimension_semantics=("arbitrary",), vmem_limit_bytes=VMEM_LIMIT),
        name="conv",
    )(a3, a_meta, dw_w, dw_b, ln_g, ln_b)


ATT_TQ = 512
ATT_TK = 256
N_PAIRS = N_HEADS // 2


def _split3(x):
    hi = x.astype(BF16)
    r1 = x - hi.astype(F32)
    mid = r1.astype(BF16)
    lo = (r1 - mid.astype(F32)).astype(BF16)
    return jnp.concatenate([hi, mid, lo], axis=0)


def _attn_kernel(q_ref, k_ref, v_ref, lf_ref, km_ref, vm_ref, lfm_ref, o_ref,
                 c_scr, cm_scr, m_scr, l_scr, acc_scr):
    p = pl.program_id(1)
    qi = pl.program_id(2)
    nk = SEQ // ATT_TK

    @pl.when((p == 0) & (qi == 0))
    def _():
        r = lax.broadcasted_iota(jnp.int32, (ATT_TK, ATT_TK), 0)
        c = lax.broadcasted_iota(jnp.int32, (ATT_TK, ATT_TK), 1)
        tri = (r <= c).astype(BF16)
        carry = jnp.zeros((N_HEADS, 1), F32)
        for j in range(nk):
            parts = _dot(_split3(lf_ref[0:N_HEADS, j * ATT_TK:(j + 1) * ATT_TK]), tri)
            cj = parts[0:8] + parts[8:16] + parts[16:24] + carry
            c_scr[j] = cj
            carry = cj[:, ATT_TK - 1:ATT_TK]
        rm = lax.broadcasted_iota(jnp.int32, (LANES, LANES), 0)
        cmx = lax.broadcasted_iota(jnp.int32, (LANES, LANES), 1)
        trim = ((rm > cmx) & (rm < N_META)).astype(BF16)
        pm = _dot(_split3(lfm_ref[0:N_HEADS, :]), trim)
        cm_scr[...] = -(pm[0:8] + pm[8:16] + pm[16:24])

    lane = lax.broadcasted_iota(jnp.int32, (ATT_TQ, LANES), 1)
    q_all = q_ref[...]

    def block(hl, qm, kb, vb, crow, mask):
        s = _dot_nt(qm, kb) - crow
        if mask is not None:
            s = jnp.where(mask, s, NEG_INF)
        m_prev = m_scr[hl]
        m_next = jnp.maximum(m_prev, jnp.max(s, axis=1, keepdims=True))
        reps = s.shape[1] // LANES
        m_wide = m_next if reps == 1 else jnp.concatenate([m_next] * reps, axis=1)
        pr = jnp.exp(s - m_wide)
        alpha = jnp.exp(m_prev - m_next)
        l_scr[hl] = alpha * l_scr[hl] + jnp.sum(pr, axis=1, keepdims=True)
        acc_scr[hl] = acc_scr[hl] * alpha + _dot(pr.astype(BF16), vb)
        m_scr[hl] = m_next

    for hl in range(2):
        h = 2 * p + hl
        head_lanes = (lane < HEAD_DIM) if hl == 0 else (lane >= HEAD_DIM)
        qm = jnp.where(head_lanes, q_all, jnp.zeros_like(q_all))
        m_scr[hl] = jnp.full((ATT_TQ, LANES), NEG_INF, F32)
        l_scr[hl] = jnp.zeros((ATT_TQ, LANES), F32)
        acc_scr[hl] = jnp.zeros((ATT_TQ, LANES), F32)

        mcol = lax.broadcasted_iota(jnp.int32, (ATT_TQ, LANES), 1)
        block(hl, qm, km_ref[...], vm_ref[...], cm_scr[pl.ds(h, 1), :], mcol < N_META)

        def full(j, carry):
            r0 = pl.multiple_of(j * ATT_TK, ATT_TK)
            block(hl, qm, k_ref[pl.ds(r0, ATT_TK), :], v_ref[pl.ds(r0, ATT_TK), :],
                  c_scr[j, pl.ds(h, 1), :], None)
            return carry

        nfull = qi * (ATT_TQ // ATT_TK)
        lax.fori_loop(0, nfull, full, 0)

        rows = lax.broadcasted_iota(jnp.int32, (ATT_TQ, ATT_TK), 0)
        cols = lax.broadcasted_iota(jnp.int32, (ATT_TQ, ATT_TK), 1)
        for dj in range(ATT_TQ // ATT_TK):
            j = nfull + dj
            r0 = pl.multiple_of(j * ATT_TK, ATT_TK)
            block(hl, qm, k_ref[pl.ds(r0, ATT_TK), :], v_ref[pl.ds(r0, ATT_TK), :],
                  c_scr[j, pl.ds(h, 1), :], cols + dj * ATT_TK <= rows)

    o0 = acc_scr[0] / l_scr[0]
    o1 = acc_scr[1] / l_scr[1]
    o_ref[...] = jnp.where(lane < HEAD_DIM, o0, o1).astype(BF16)


def _attention(q, k, v, lf, km, vm, lfm):
    nq = SEQ // ATT_TQ
    return pl.pallas_call(
        _attn_kernel,
        grid=(BATCH, N_PAIRS, nq),
        in_specs=[
            pl.BlockSpec((ATT_TQ, LANES), lambda b, p, i: (b * nq + i, p)),
            pl.BlockSpec((SEQ, LANES), lambda b, p, i: (b, p)),
            pl.BlockSpec((SEQ, LANES), lambda b, p, i: (b, p)),
            pl.BlockSpec((16, SEQ), lambda b, p, i: (0, b)),
            pl.BlockSpec((META_ROWS, LANES), lambda b, p, i: (0, p)),
            pl.BlockSpec((META_ROWS, LANES), lambda b, p, i: (0, p)),
            pl.BlockSpec((16, META_ROWS), lambda b, p, i: (0, 0)),
        ],
        out_specs=pl.BlockSpec((ATT_TQ, LANES), lambda b, p, i: (b * nq + i, p)),
        out_shape=jax.ShapeDtypeStruct((N_TOK, ATTN_WIDTH), BF16),
        scratch_shapes=[
            pltpu.VMEM((SEQ // ATT_TK, N_HEADS, ATT_TK), F32),
            pltpu.VMEM((N_HEADS, LANES), F32),
            pltpu.VMEM((2, ATT_TQ, LANES), F32),
            pltpu.VMEM((2, ATT_TQ, LANES), F32),
            pltpu.VMEM((2, ATT_TQ, LANES), F32),
        ],
        compiler_params=pltpu.CompilerParams(
            dimension_semantics=("arbitrary", "arbitrary", "arbitrary"), vmem_limit_bytes=VMEM_LIMIT),
        name="attn",
    )(q, k, v, lf, km, vm, lfm)


MERGE_TM = 512


def _merge_kernel(act_ref, o_ref, sgc_ref, sga_ref, x_ref, wpw_ref, bpw_ref, wao_ref, wout_ref, g2_ref,
                  wrt_ref, brt_ref, tri_ref,
                  h1_ref, hn_ref, route_ref, cnt_ref, base_scr):
    i = pl.program_id(0)

    @pl.when(i == 0)
    def _():
        base_scr[...] = jnp.zeros_like(base_scr)

    y_conv = _dot(act_ref[...], wpw_ref[...]) + bpw_ref[...]
    y_attn = _dot(o_ref[...], wao_ref[...])
    merged = sgc_ref[...].astype(F32) * y_conv + sga_ref[...].astype(F32) * y_attn
    h1 = x_ref[...] + _dot(merged.astype(BF16), wout_ref[...])
    h1_ref[...] = h1
    ms = jnp.mean(h1 * h1, axis=-1, keepdims=True)
    hn = h1 * lax.rsqrt(ms + RMS_EPS) * g2_ref[...]
    hn_ref[...] = hn

    logits = _dot(hn.astype(BF16), wrt_ref[...]) + brt_ref[...]
    lane = lax.broadcasted_iota(jnp.int32, logits.shape, 1).astype(F32)
    big = float(LANES)
    is_g = lane < N_GROUPS
    gl = jnp.where(is_g, logits, NEG_INF)
    gmax = jnp.max(gl, axis=1, keepdims=True)
    gidx = jnp.min(jnp.where(gl == gmax, lane, big), axis=1, keepdims=True)
    gsum = jnp.sum(jnp.where(is_g, jnp.exp(gl - gmax), 0.0), axis=1, keepdims=True)
    g_w = 1.0 / gsum
    lo = N_GROUPS + EXPERTS_PER_GROUP * gidx
    el = jnp.where((lane >= lo) & (lane < lo + EXPERTS_PER_GROUP), logits, NEG_INF)
    t1 = jnp.max(el, axis=1, keepdims=True)
    i1 = jnp.min(jnp.where(el == t1, lane, big), axis=1, keepdims=True)
    el2 = jnp.where(lane == i1, NEG_INF, el)
    t2 = jnp.max(el2, axis=1, keepdims=True)
    i2 = jnp.min(jnp.where(el2 == t2, lane, big), axis=1, keepdims=True)
    e2 = jnp.exp(t2 - t1)
    den = 1.0 + e2
    w1 = (1.0 / den) * g_w
    w2 = (e2 / den) * g_w
    ea = i1 - N_GROUPS
    eb = i2 - N_GROUPS

    oha = (lane == ea).astype(F32)
    ohb = (lane == eb).astype(F32)
    ohs = oha + ohb
    before = _dot(tri_ref[...], ohs.astype(BF16)) + base_scr[0:1, :]
    ra = jnp.sum(before * oha, axis=1, keepdims=True)
    rb = jnp.sum(before * ohb, axis=1, keepdims=True)
    base_scr[...] = base_scr[...] + jnp.sum(ohs, axis=0, keepdims=True)
    cnt_ref[...] = base_scr[...]

    route = jnp.where(lane == 0, ea, 0.0)
    route = jnp.where(lane == 1, eb, route)
    route = jnp.where(lane == 2, w1, route)
    route = jnp.where(lane == 3, w2, route)
    route = jnp.where(lane == 4, ra, route)
    route = jnp.where(lane == 5, rb, route)
    route_ref[...] = route


def _merge(act, o, sgc, sga, x2d, consts):
    tm = MERGE_TM
    row = lambda w: pl.BlockSpec((tm, w), lambda i: (i, 0))
    return pl.pallas_call(
        _merge_kernel,
        grid=(N_TOK // tm,),
        in_specs=[row(CONV_WIDTH), row(ATTN_WIDTH), row(D_MODEL), row(D_MODEL), row(D_MODEL)]
        + [_const_spec(c.shape) for c in consts],
        out_specs=[row(D_MODEL), row(D_MODEL), row(LANES), _const_spec((8, LANES))],
        out_shape=[jax.ShapeDtypeStruct((N_TOK, D_MODEL), F32),
                   jax.ShapeDtypeStruct((N_TOK, D_MODEL), F32),
                   jax.ShapeDtypeStruct((N_TOK, LANES), F32),
                   jax.ShapeDtypeStruct((8, LANES), F32)],
        scratch_shapes=[pltpu.VMEM((8, LANES), F32)],
        compiler_params=pltpu.CompilerParams(dimension_semantics=("arbitrary",), vmem_limit_bytes=VMEM_LIMIT),
        name="merge",
    )(act, o, sgc, sga, x2d, *consts)


DISP_TM = 512


def _dispatch_kernel(d0_ref, d1_ref, hn_ref, xin_in_ref, xin_ref, sem):
    del xin_in_ref
    base = pl.program_id(0) * DISP_TM

    def row_copy(r, d_ref, s):
        return pltpu.make_async_copy(hn_ref.at[pl.ds(r, 1)], xin_ref.at[pl.ds(d_ref[base + r], 1)], sem.at[s])

    def start(r, carry):
        row_copy(r, d0_ref, 0).start()
        row_copy(r, d1_ref, 1).start()
        return carry

    lax.fori_loop(0, DISP_TM, start, 0)

    def wait(r, carry):
        row_copy(r, d0_ref, 0).wait()
        row_copy(r, d1_ref, 1).wait()
        return carry

    lax.fori_loop(0, DISP_TM, wait, 0)


def _dispatch(d0, d1, hn, xin_zero):
    grid_spec = pltpu.PrefetchScalarGridSpec(
        num_scalar_prefetch=2,
        grid=(N_TOK // DISP_TM,),
        in_specs=[pl.BlockSpec((DISP_TM, D_MODEL), lambda i, d0, d1: (i, 0)),
                  pl.BlockSpec(memory_space=pl.ANY)],
        out_specs=pl.BlockSpec(memory_space=pl.ANY),
        scratch_shapes=[pltpu.SemaphoreType.DMA((2,))],
    )
    return pl.pallas_call(
        _dispatch_kernel,
        grid_spec=grid_spec,
        out_shape=jax.ShapeDtypeStruct((P_ROWS, D_MODEL), F32),
        input_output_aliases={3: 0},
        compiler_params=pltpu.CompilerParams(dimension_semantics=("arbitrary",), vmem_limit_bytes=VMEM_LIMIT),
        name="dispatch",
    )(d0, d1, hn, xin_zero)


def _expert_kernel(be_ref, nu_ref, x_ref, wg_ref, wu_ref, wd_ref, y_ref):
    i = pl.program_id(0)

    @pl.when(i < nu_ref[0])
    def _():
        x = x_ref[...].astype(BF16)
        g = _dot(x, wg_ref[0].astype(BF16))
        u = _dot(x, wu_ref[0].astype(BF16))
        hmid = (g * _sigmoid(g) * u).astype(BF16)
        y_ref[...] = _dot(hmid, wd_ref[0].astype(BF16))

    @pl.when(i >= nu_ref[0])
    def _():
        y_ref[...] = jnp.zeros_like(y_ref)


def _experts(blk_e, n_used, xin, w_gate, w_up, w_down):
    def xmap(i, be, nu):
        return (jnp.minimum(i, nu[0] - 1), 0)

    def wmap(i, be, nu):
        return (be[i], 0, 0)

    grid_spec = pltpu.PrefetchScalarGridSpec(
        num_scalar_prefetch=2,
        grid=(N_BLOCKS,),
        in_specs=[pl.BlockSpec((ROW_BLOCK, D_MODEL), xmap),
                  pl.BlockSpec((1, D_MODEL, D_EXPERT), wmap),
                  pl.BlockSpec((1, D_MODEL, D_EXPERT), wmap),
                  pl.BlockSpec((1, D_EXPERT, D_MODEL), wmap)],
        out_specs=pl.BlockSpec((ROW_BLOCK, D_MODEL), lambda i, be, nu: (i, 0)),
    )
    return pl.pallas_call(
        _expert_kernel,
        grid_spec=grid_spec,
        out_shape=jax.ShapeDtypeStruct((P_ROWS, D_MODEL), F32),
        compiler_params=pltpu.CompilerParams(dimension_semantics=("arbitrary",), vmem_limit_bytes=VMEM_LIMIT),
        name="experts",
    )(blk_e, n_used, xin, w_gate, w_up, w_down)


COMB_TM = 256


def _combine_kernel(d0_ref, d1_ref, h1_ref, route_ref, yb_ref, out_ref, g0_scr, g1_scr, sem):
    base = pl.program_id(0) * COMB_TM

    def row_copy(r, d_ref, buf, s):
        return pltpu.make_async_copy(yb_ref.at[pl.ds(d_ref[base + r], 1)], buf.at[pl.ds(r, 1)], sem.at[s])

    def start(r, carry):
        row_copy(r, d0_ref, g0_scr, 0).start()
        row_copy(r, d1_ref, g1_scr, 1).start()
        return carry

    lax.fori_loop(0, COMB_TM, start, 0)

    def wait(r, carry):
        row_copy(r, d0_ref, g0_scr, 0).wait()
        row_copy(r, d1_ref, g1_scr, 1).wait()
        return carry

    lax.fori_loop(0, COMB_TM, wait, 0)

    route = route_ref[...]
    w0 = route[:, 2:3]
    w1 = route[:, 3:4]
    out_ref[...] = h1_ref[...] + (g0_scr[...] * w0 + g1_scr[...] * w1)


def _combine(d0, d1, h1, route, yb):
    grid_spec = pltpu.PrefetchScalarGridSpec(
        num_scalar_prefetch=2,
        grid=(N_TOK // COMB_TM,),
        in_specs=[pl.BlockSpec((COMB_TM, D_MODEL), lambda i, d0, d1: (i, 0)),
                  pl.BlockSpec((COMB_TM, LANES), lambda i, d0, d1: (i, 0)),
                  pl.BlockSpec(memory_space=pl.ANY)],
        out_specs=pl.BlockSpec((COMB_TM, D_MODEL), lambda i, d0, d1: (i, 0)),
        scratch_shapes=[pltpu.VMEM((COMB_TM, D_MODEL), F32), pltpu.VMEM((COMB_TM, D_MODEL), F32),
                        pltpu.SemaphoreType.DMA((2,))],
    )
    return pl.pallas_call(
        _combine_kernel,
        grid_spec=grid_spec,
        out_shape=jax.ShapeDtypeStruct((N_TOK, D_MODEL), F32),
        compiler_params=pltpu.CompilerParams(dimension_semantics=("arbitrary",), vmem_limit_bytes=VMEM_LIMIT),
        name="combine",
    )(d0, d1, h1, route, yb)


def kernel(x, meta, norm1_g, w_in, b_in, dw_w, dw_b, conv_ln_g, conv_ln_b, w_conv_out, b_conv_out,
           q_norm_g, k_norm_g, b_forget, w_attn_out, w_out, norm2_g, w_group, b_group, w_router,
           b_router, w_gate, w_up, w_down):
    l = 0
    row = lambda v: v.reshape(1, -1).astype(F32)

    c_u, c_q, c_f = 2 * CONV_WIDTH, 2 * CONV_WIDTH, 2 * CONV_WIDTH + 3 * ATTN_WIDTH
    c_gc = c_f + N_HEADS
    c_ga = c_gc + D_MODEL
    wi, bi = w_in[l], b_in[l]
    wu, bu = wi[:, :c_u].astype(BF16), row(bi[:c_u])
    wqkv, bqkv = wi[:, c_q:c_f].astype(BF16), row(bi[c_q:c_f])
    wft = jnp.zeros((16, D_MODEL), F32).at[:N_HEADS].set(wi[:, c_f:c_gc].T).astype(BF16)
    bft = jnp.zeros((16, 1), F32).at[:N_HEADS, 0].set(bi[c_f:c_gc] + b_forget[l])
    wgc, bgc = wi[:, c_gc:c_ga].astype(BF16), row(bi[c_gc:c_ga])
    wga, bga = wi[:, c_ga:].astype(BF16), row(bi[c_ga:])
    qg = row(jnp.tile(q_norm_g[l], N_HEADS) * (HEAD_DIM ** -0.5))
    kg = row(jnp.tile(k_norm_g[l], N_HEADS))
    head_of_col = jnp.arange(ATTN_WIDTH) // HEAD_DIM
    hsum = (head_of_col[:, None] == jnp.arange(LANES)[None, :]).astype(BF16)
    hexp_half = (jnp.arange(LANES)[:, None] == head_of_col[None, :]).astype(BF16)
    hexp = jnp.concatenate([hexp_half, hexp_half], axis=0)
    proj_consts = (row(norm1_g[l]), wu, bu, wqkv, bqkv, wft, bft, wgc, bgc, wga, bga, qg, kg, hsum, hexp)

    x2d = x.reshape(N_TOK, D_MODEL)
    meta_pad = jnp.zeros((META_ROWS, D_MODEL), F32).at[:N_META].set(meta.astype(F32))
    a, q, k, v, lf, sgc, sga = _proj(x2d, proj_consts, 512)
    a_m, _, k_m, v_m, lf_m, _, _ = _proj(meta_pad, proj_consts, META_ROWS)

    act = _conv(a.reshape(BATCH, SEQ, CONV_WIDTH), a_m, dw_w[l].astype(F32), row(dw_b[l]),
                row(conv_ln_g[l]), row(conv_ln_b[l])).reshape(N_TOK, CONV_WIDTH)

    o = _attention(q, k, v, lf, k_m, v_m, lf_m)

    wrt = jnp.zeros((D_MODEL, LANES), F32).at[:, :N_GROUPS].set(w_group[l])
    wrt = wrt.at[:, N_GROUPS:N_GROUPS + N_EXPERTS].set(w_router[l]).astype(BF16)
    brt = jnp.zeros((1, LANES), F32).at[0, :N_GROUPS].set(b_group[l])
    brt = brt.at[0, N_GROUPS:N_GROUPS + N_EXPERTS].set(b_router[l])
    tri = jnp.tril(jnp.ones((MERGE_TM, MERGE_TM), F32), -1).astype(BF16)
    merge_consts = (w_conv_out[l].astype(BF16), row(b_conv_out[l]), w_attn_out[l].astype(BF16),
                    w_out[l].astype(BF16), row(norm2_g[l]), wrt, brt, tri)
    h1, hn, route, cnt = _merge(act, o, sgc, sga, x2d, merge_consts)

    e0 = route[:, 0].astype(jnp.int32)
    e1 = route[:, 1].astype(jnp.int32)
    r0 = route[:, 4].astype(jnp.int32)
    r1 = route[:, 5].astype(jnp.int32)
    counts = cnt[0, :N_EXPERTS].astype(jnp.int32)
    padded = (counts + ROW_BLOCK - 1) // ROW_BLOCK * ROW_BLOCK
    pend = jnp.cumsum(padded)
    pstart = pend - padded
    d0 = pstart[e0] + r0
    d1 = pstart[e1] + r1
    blk_e = jnp.minimum(jnp.searchsorted(pend, jnp.arange(N_BLOCKS, dtype=jnp.int32) * ROW_BLOCK, side="right"),
                        N_EXPERTS - 1).astype(jnp.int32)
    n_used = (pend[-1:] // ROW_BLOCK).astype(jnp.int32)

    xin = _dispatch(d0, d1, hn, jnp.zeros((P_ROWS, D_MODEL), F32))
    yb = _experts(blk_e, n_used, xin, w_gate[l], w_up[l], w_down[l])
    out = _combine(d0, d1, h1, route, yb)
    return out.reshape(BATCH, SEQ, D_MODEL)
```

```python
import functools

import jax
import jax.numpy as jnp
from jax import lax
from jax.experimental import pallas as pl
from jax.experimental.pallas import tpu as pltpu

D_MODEL = 1024
BATCH = 8
SEQ = 2048
N_META = 16
CONV_WIDTH = 512
CONV_K = 31
N_HEADS = 8
HEAD_DIM = 64
ATTN_WIDTH = N_HEADS * HEAD_DIM
N_GROUPS = 8
EXPERTS_PER_GROUP = 8
N_EXPERTS = N_GROUPS * EXPERTS_PER_GROUP
D_EXPERT = 256
RMS_EPS = 1e-6
LN_EPS = 1e-5

N_TOK = BATCH * SEQ
LANES = 128
META_ROWS = 128
ROW_BLOCK = 256
N_ASSIGN = 2 * N_TOK
N_BLOCKS = N_ASSIGN // ROW_BLOCK + N_EXPERTS
P_ROWS = N_BLOCKS * ROW_BLOCK
TOK_SUB = D_MODEL // LANES
RANK_BITS = 16
VMEM_LIMIT = 56 * 1024 * 1024

F32 = jnp.float32
BF16 = jnp.bfloat16
NEG_INF = float("-inf")


def _dot(a, b):
    return jnp.dot(a, b, preferred_element_type=F32)


def _dot_nt(a, b):
    return lax.dot_general(a, b, (((1,), (1,)), ((), ())), preferred_element_type=F32)


def _sigmoid(x):
    return 1.0 / (1.0 + jnp.exp(-x))


def _const_spec(shape):
    nd = len(shape)
    return pl.BlockSpec(shape, lambda *_: (0,) * nd)


def _head_rms(t, hsum_ref, hexp_ref, gain):
    ss = _dot((t * t).astype(BF16), hsum_ref[...])
    inv = lax.rsqrt(ss * (1.0 / HEAD_DIM) + RMS_EPS)
    hi = inv.astype(BF16)
    lo = (inv - hi.astype(F32)).astype(BF16)
    invb = _dot(jnp.concatenate([hi, lo], axis=1), hexp_ref[...])
    return t * invb * gain


def _proj_kernel(x_ref, g1_ref, wu_ref, bu_ref, wqkv_ref, bqkv_ref, wft_ref, bft_ref, wgc_ref, bgc_ref,
                 wga_ref, bga_ref, qg_ref, kg_ref, hsum_ref, hexp_ref,
                 a_ref, q_ref, k_ref, v_ref, lf_ref, sgc_ref, sga_ref):
    x = x_ref[...]
    ms = jnp.mean(x * x, axis=-1, keepdims=True)
    xn = (x * lax.rsqrt(ms + RMS_EPS) * g1_ref[...]).astype(BF16)

    u = _dot(xn, wu_ref[...]) + bu_ref[...]
    a_ref[...] = (u[:, :CONV_WIDTH] * _sigmoid(u[:, CONV_WIDTH:])).astype(BF16)

    qkv = _dot(xn, wqkv_ref[...]) + bqkv_ref[...]
    q_ref[...] = _head_rms(qkv[:, :ATTN_WIDTH], hsum_ref, hexp_ref, qg_ref[...]).astype(BF16)
    k_ref[...] = _head_rms(qkv[:, ATTN_WIDTH:2 * ATTN_WIDTH], hsum_ref, hexp_ref, kg_ref[...]).astype(BF16)
    v_ref[...] = qkv[:, 2 * ATTN_WIDTH:].astype(BF16)

    f = _dot_nt(wft_ref[...], xn) + bft_ref[...]
    lf_ref[...] = jnp.minimum(f, 0.0) - jnp.log(1.0 + jnp.exp(-jnp.abs(f)))

    sgc_ref[...] = _sigmoid(_dot(xn, wgc_ref[...]) + bgc_ref[...]).astype(BF16)
    sga_ref[...] = _sigmoid(_dot(xn, wga_ref[...]) + bga_ref[...]).astype(BF16)


def _proj(x2d, consts, tm):
    n = x2d.shape[0]
    row = lambda w: pl.BlockSpec((tm, w), lambda i: (i, 0))
    in_specs = [row(D_MODEL)] + [_const_spec(c.shape) for c in consts]
    out_shape = [
        jax.ShapeDtypeStruct((n, CONV_WIDTH), BF16),
        jax.ShapeDtypeStruct((n, ATTN_WIDTH), BF16),
        jax.ShapeDtypeStruct((n, ATTN_WIDTH), BF16),
        jax.ShapeDtypeStruct((n, ATTN_WIDTH), BF16),
        jax.ShapeDtypeStruct((16, n), F32),
        jax.ShapeDtypeStruct((n, D_MODEL), BF16),
        jax.ShapeDtypeStruct((n, D_MODEL), BF16),
    ]
    out_specs = [row(CONV_WIDTH), row(ATTN_WIDTH), row(ATTN_WIDTH), row(ATTN_WIDTH),
                 pl.BlockSpec((16, tm), lambda i: (0, i)), row(D_MODEL), row(D_MODEL)]
    return pl.pallas_call(
        _proj_kernel,
        grid=(n // tm,),
        in_specs=in_specs,
        out_specs=out_specs,
        out_shape=out_shape,
        compiler_params=pltpu.CompilerParams(dimension_semantics=("arbitrary",), vmem_limit_bytes=VMEM_LIMIT),
        name="proj",
    )(x2d, *consts)


CONV_CHUNK = 32
CONV_PAD = 32
CONV_WIN = 2 * CONV_CHUNK
CONV_SHROWS = CONV_WIN


def _conv_kernel(a_ref, am_ref, shift_ref, dw_ref, dwb_ref, lng_ref, lnb_ref, o_ref, pad_scr, sha_scr, shb_scr):
    pad_scr[0:N_META, :] = jnp.zeros((N_META, CONV_WIDTH), BF16)
    pad_scr[N_META:CONV_PAD, :] = am_ref[0:N_META, :]
    pad_scr[CONV_PAD:CONV_PAD + SEQ, :] = a_ref[0]

    def shifted_copies(ci):
        r0 = pl.multiple_of(ci * CONV_CHUNK, CONV_CHUNK)
        return _dot(shift_ref[...], pad_scr[pl.ds(r0, CONV_WIN), :])

    n_chunks = SEQ // CONV_CHUNK
    sha_scr[...] = shifted_copies(0)

    def finish(ci, cur_scr):
        r0 = pl.multiple_of(ci * CONV_CHUNK, CONV_CHUNK)
        acc = jnp.zeros((CONV_CHUNK, CONV_WIDTH), F32) + dwb_ref[...]
        for j in range(CONV_K):
            off = j + 2
            b = off % 8
            a0 = b * CONV_SHROWS + (off - b)
            acc = acc + cur_scr[a0:a0 + CONV_CHUNK, :] * dw_ref[j:j + 1, :]
        mu = jnp.mean(acc, axis=-1, keepdims=True)
        d = acc - mu
        var = jnp.mean(d * d, axis=-1, keepdims=True)
        y = d * lax.rsqrt(var + LN_EPS) * lng_ref[...] + lnb_ref[...]
        o_ref[0, pl.ds(r0, CONV_CHUNK), :] = (y * _sigmoid(y)).astype(BF16)

    def chunk_pair(i, carry):
        ci = 2 * i
        shb_scr[...] = shifted_copies(ci + 1)
        finish(ci, sha_scr)
        sha_scr[...] = shifted_copies(jnp.minimum(ci + 2, n_chunks - 1))
        finish(ci + 1, shb_scr)
        return carry

    lax.fori_loop(0, n_chunks // 2, chunk_pair, 0)


def _conv(a3, a_meta, dw_w, dw_b, ln_g, ln_b):
    rr = jnp.arange(8 * CONV_SHROWS)
    shift = ((rr % CONV_SHROWS + rr // CONV_SHROWS)[:, None] == jnp.arange(CONV_WIN)[None, :]).astype(BF16)
    return pl.pallas_call(
        _conv_kernel,
        grid=(BATCH,),
        in_specs=[pl.BlockSpec((1, SEQ, CONV_WIDTH), lambda b: (b, 0, 0)),
                  _const_spec(a_meta.shape), _const_spec(shift.shape), _const_spec(dw_w.shape),
                  _const_spec(dw_b.shape), _const_spec(ln_g.shape), _const_spec(ln_b.shape)],
        out_specs=pl.BlockSpec((1, SEQ, CONV_WIDTH), lambda b: (b, 0, 0)),
        out_shape=jax.ShapeDtypeStruct((BATCH, SEQ, CONV_WIDTH), BF16),
        scratch_shapes=[pltpu.VMEM((CONV_PAD + SEQ, CONV_WIDTH), BF16),
                        pltpu.VMEM((8 * CONV_SHROWS, CONV_WIDTH), F32),
                        pltpu.VMEM((8 * CONV_SHROWS, CONV_WIDTH), F32)],
        compiler_params=pltpu.CompilerParams(dimension_semantics=("arbitrary",), vmem_limit_bytes=VMEM_LIMIT),
        name="conv",
    )(a3, a_meta, shift, dw_w, dw_b, ln_g, ln_b)


ATT_TQ = 512
ATT_TK = 256
N_PAIRS = N_HEADS // 2
assert (ATT_TQ // ATT_TK) % 2 == 0


def _split3(x):
    hi = x.astype(BF16)
    r1 = x - hi.astype(F32)
    mid = r1.astype(BF16)
    lo = (r1 - mid.astype(F32)).astype(BF16)
    return jnp.concatenate([hi, mid, lo], axis=0)


def _attn_kernel(q_ref, k_ref, v_ref, lf_ref, km_ref, vm_ref, lfm_ref, o_ref,
                 c_scr, cm_scr, m_scr, acc_scr, sa_scr, sb_scr):
    p = pl.program_id(1)
    qi = pl.program_id(2)
    nk = SEQ // ATT_TK

    @pl.when((p == 0) & (qi == 0))
    def _():
        r = lax.broadcasted_iota(jnp.int32, (ATT_TK, ATT_TK), 0)
        c = lax.broadcasted_iota(jnp.int32, (ATT_TK, ATT_TK), 1)
        tri = (r <= c).astype(BF16)
        carry = jnp.zeros((N_HEADS, 1), F32)
        for j in range(nk):
            parts = _dot(_split3(lf_ref[0:N_HEADS, j * ATT_TK:(j + 1) * ATT_TK]), tri)
            cj = parts[0:8] + parts[8:16] + parts[16:24] + carry
            c_scr[j] = cj
            carry = cj[:, ATT_TK - 1:ATT_TK]
        rm = lax.broadcasted_iota(jnp.int32, (LANES, LANES), 0)
        cmx = lax.broadcasted_iota(jnp.int32, (LANES, LANES), 1)
        trim = ((rm > cmx) & (rm < N_META)).astype(BF16)
        pm = _dot(_split3(lfm_ref[0:N_HEADS, :]), trim)
        cm_scr[...] = -(pm[0:8] + pm[8:16] + pm[16:24])

    lane = lax.broadcasted_iota(jnp.int32, (ATT_TQ, LANES), 1)
    q_all = q_ref[...]
    zq = jnp.zeros_like(q_all)
    qm = (jnp.where(lane < HEAD_DIM, q_all, zq), jnp.where(lane >= HEAD_DIM, q_all, zq))
    for hl in range(2):
        m_scr[hl] = jnp.full((ATT_TQ, LANES), NEG_INF, F32)
        acc_scr[hl] = jnp.zeros((ATT_TQ, LANES), F32)

    def scores(kb, crows):
        return tuple(_dot_nt(qm[hl], kb) - crows[hl] for hl in range(2))

    def update(s2, vb, mask):
        vlane = lax.broadcasted_iota(jnp.int32, vb.shape, 1)
        ones = jnp.ones_like(vb)
        for hl in range(2):
            s = s2[hl]
            if mask is not None:
                s = jnp.where(mask, s, NEG_INF)
            m_prev = m_scr[hl]
            m_next = jnp.maximum(m_prev, jnp.max(s, axis=1, keepdims=True))
            reps = s.shape[1] // LANES
            m_wide = m_next if reps == 1 else jnp.concatenate([m_next] * reps, axis=1)
            pr = jnp.exp(s - m_wide).astype(BF16)
            alpha = jnp.exp(m_prev - m_next)
            own = (vlane < HEAD_DIM) if hl == 0 else (vlane >= HEAD_DIM)
            acc_scr[hl] = acc_scr[hl] * alpha + _dot(pr, jnp.where(own, vb, ones))
            m_scr[hl] = m_next

    def crows_at(j):
        return (c_scr[j, pl.ds(2 * p, 1), :], c_scr[j, pl.ds(2 * p + 1, 1), :])

    def kblock(j):
        return k_ref[pl.ds(pl.multiple_of(j * ATT_TK, ATT_TK), ATT_TK), :]

    def vblock(j):
        return v_ref[pl.ds(pl.multiple_of(j * ATT_TK, ATT_TK), ATT_TK), :]

    update(scores(km_ref[...], (cm_scr[pl.ds(2 * p, 1), :], cm_scr[pl.ds(2 * p + 1, 1), :])), vm_ref[...],
           lane < N_META)

    nfull = qi * (ATT_TQ // ATT_TK)

    def put(scr, s2):
        scr[0] = s2[0]
        scr[1] = s2[1]

    put(sa_scr, scores(kblock(0), crows_at(0)))

    def full2(i, carry):
        j = 2 * i
        put(sb_scr, scores(kblock(j + 1), crows_at(j + 1)))
        update((sa_scr[0], sa_scr[1]), vblock(j), None)
        put(sa_scr, scores(kblock(j + 2), crows_at(j + 2)))
        update((sb_scr[0], sb_scr[1]), vblock(j + 1), None)
        return carry

    lax.fori_loop(0, nfull // 2, full2, 0)

    rows = lax.broadcasted_iota(jnp.int32, (ATT_TQ, ATT_TK), 0)
    cols = lax.broadcasted_iota(jnp.int32, (ATT_TQ, ATT_TK), 1)
    s_diag = [(sa_scr[0], sa_scr[1])]
    for dj in range(1, ATT_TQ // ATT_TK):
        s_diag.append(scores(kblock(nfull + dj), crows_at(nfull + dj)))
    for dj in range(ATT_TQ // ATT_TK):
        update(s_diag[dj], vblock(nfull + dj), cols + dj * ATT_TK <= rows)

    a0 = acc_scr[0]
    a1 = acc_scr[1]
    o0 = a0 / pltpu.roll(a0, HEAD_DIM, axis=1)
    o1 = a1 / pltpu.roll(a1, HEAD_DIM, axis=1)
    o_ref[...] = jnp.where(lane < HEAD_DIM, o0, o1).astype(BF16)


def _attention(q, k, v, lf, km, vm, lfm):
    nq = SEQ // ATT_TQ
    return pl.pallas_call(
        _attn_kernel,
        grid=(BATCH, N_PAIRS, nq),
        in_specs=[
            pl.BlockSpec((ATT_TQ, LANES), lambda b, p, i: (b * nq + i, p)),
            pl.BlockSpec((SEQ, LANES), lambda b, p, i: (b, p)),
            pl.BlockSpec((SEQ, LANES), lambda b, p, i: (b, p)),
            pl.BlockSpec((16, SEQ), lambda b, p, i: (0, b)),
            pl.BlockSpec((META_ROWS, LANES), lambda b, p, i: (0, p)),
            pl.BlockSpec((META_ROWS, LANES), lambda b, p, i: (0, p)),
            pl.BlockSpec((16, META_ROWS), lambda b, p, i: (0, 0)),
        ],
        out_specs=pl.BlockSpec((ATT_TQ, LANES), lambda b, p, i: (b * nq + i, p)),
        out_shape=jax.ShapeDtypeStruct((N_TOK, ATTN_WIDTH), BF16),
        scratch_shapes=[
            pltpu.VMEM((SEQ // ATT_TK, N_HEADS, ATT_TK), F32),
            pltpu.VMEM((N_HEADS, LANES), F32),
            pltpu.VMEM((2, ATT_TQ, LANES), F32),
            pltpu.VMEM((2, ATT_TQ, LANES), F32),
            pltpu.VMEM((2, ATT_TQ, ATT_TK), F32),
            pltpu.VMEM((2, ATT_TQ, ATT_TK), F32),
        ],
        compiler_params=pltpu.CompilerParams(
            dimension_semantics=("arbitrary", "arbitrary", "arbitrary"), vmem_limit_bytes=VMEM_LIMIT),
        name="attn",
    )(q, k, v, lf, km, vm, lfm)


MERGE_TM = 512


def _merge_kernel(act_ref, o_ref, sgc_ref, sga_ref, x_ref, wpw_ref, bpw_ref, wao_ref, wout_ref, g2_ref,
                  wrt_ref, brt_ref, tri_ref,
                  h1_ref, hn_ref, route_ref, routet_ref, cnt_ref, base_scr):
    i = pl.program_id(0)

    @pl.when(i == 0)
    def _():
        base_scr[...] = jnp.zeros_like(base_scr)

    y_conv = _dot(act_ref[...], wpw_ref[...]) + bpw_ref[...]
    y_attn = _dot(o_ref[...], wao_ref[...])
    merged = sgc_ref[...].astype(F32) * y_conv + sga_ref[...].astype(F32) * y_attn
    h1 = x_ref[...] + _dot(merged.astype(BF16), wout_ref[...])
    h1_ref[...] = h1
    ms = jnp.mean(h1 * h1, axis=-1, keepdims=True)
    hn = (h1 * lax.rsqrt(ms + RMS_EPS) * g2_ref[...]).astype(BF16)
    hn_ref[...] = hn.reshape(hn.shape[0], TOK_SUB, LANES)

    logits = _dot(hn, wrt_ref[...]) + brt_ref[...]
    lane = lax.broadcasted_iota(jnp.int32, logits.shape, 1).astype(F32)
    big = float(LANES)
    is_g = lane < N_GROUPS
    gl = jnp.where(is_g, logits, NEG_INF)
    gmax = jnp.max(gl, axis=1, keepdims=True)
    gidx = jnp.min(jnp.where(gl == gmax, lane, big), axis=1, keepdims=True)
    gsum = jnp.sum(jnp.where(is_g, jnp.exp(gl - gmax), 0.0), axis=1, keepdims=True)
    g_w = 1.0 / gsum
    lo = N_GROUPS + EXPERTS_PER_GROUP * gidx
    el = jnp.where((lane >= lo) & (lane < lo + EXPERTS_PER_GROUP), logits, NEG_INF)
    t1 = jnp.max(el, axis=1, keepdims=True)
    i1 = jnp.min(jnp.where(el == t1, lane, big), axis=1, keepdims=True)
    el2 = jnp.where(lane == i1, NEG_INF, el)
    t2 = jnp.max(el2, axis=1, keepdims=True)
    i2 = jnp.min(jnp.where(el2 == t2, lane, big), axis=1, keepdims=True)
    e2 = jnp.exp(t2 - t1)
    den = 1.0 + e2
    w1 = (1.0 / den) * g_w
    w2 = (e2 / den) * g_w
    ea = i1 - N_GROUPS
    eb = i2 - N_GROUPS

    oha = (lane == ea).astype(F32)
    ohb = (lane == eb).astype(F32)
    ohs = oha + ohb
    before = _dot(tri_ref[...], ohs.astype(BF16)) + base_scr[0:1, :]
    ra = jnp.sum(before * oha, axis=1, keepdims=True)
    rb = jnp.sum(before * ohb, axis=1, keepdims=True)
    base_scr[...] = base_scr[...] + jnp.sum(ohs, axis=0, keepdims=True)
    cnt_ref[...] = base_scr[...]

    route = jnp.where(lane == 0, ea, 0.0)
    route = jnp.where(lane == 1, eb, route)
    route = jnp.where(lane == 2, w1, route)
    route = jnp.where(lane == 3, w2, route)
    route = jnp.where(lane == 4, ra, route)
    route = jnp.where(lane == 5, rb, route)
    route_ref[...] = route
    routet_ref[...] = route.T[0:8]


def _merge(act, o, sgc, sga, x2d, consts):
    tm = MERGE_TM
    row = lambda w: pl.BlockSpec((tm, w), lambda i: (i, 0))
    return pl.pallas_call(
        _merge_kernel,
        grid=(N_TOK // tm,),
        in_specs=[row(CONV_WIDTH), row(ATTN_WIDTH), row(D_MODEL), row(D_MODEL), row(D_MODEL)]
        + [_const_spec(c.shape) for c in consts],
        out_specs=[row(D_MODEL), pl.BlockSpec((tm, TOK_SUB, LANES), lambda i: (i, 0, 0)), row(LANES),
                   pl.BlockSpec((8, tm), lambda i: (0, i)), _const_spec((8, LANES))],
        out_shape=[jax.ShapeDtypeStruct((N_TOK, D_MODEL), F32),
                   jax.ShapeDtypeStruct((N_TOK, TOK_SUB, LANES), BF16),
                   jax.ShapeDtypeStruct((N_TOK, LANES), F32),
                   jax.ShapeDtypeStruct((8, N_TOK), F32),
                   jax.ShapeDtypeStruct((8, LANES), F32)],
        scratch_shapes=[pltpu.VMEM((8, LANES), F32)],
        compiler_params=pltpu.CompilerParams(dimension_semantics=("arbitrary",), vmem_limit_bytes=VMEM_LIMIT),
        name="merge",
    )(act, o, sgc, sga, x2d, *consts)


DISP_TM = 1024
DMA_UNROLL = 8


def _dest_row(code_ref, pstart_ref, idx):
    c = code_ref[idx]
    return pstart_ref[lax.shift_right_logical(c, RANK_BITS)] + (c & ((1 << RANK_BITS) - 1))


def _dispatch_kernel(c0_ref, c1_ref, pstart_ref, pend_ref, hn_ref, xin_ref, zero_scr, sem, zsem):
    i = pl.program_id(0)
    base = i * DISP_TM

    @pl.when(i == 0)
    def _():
        zero_scr[...] = jnp.zeros_like(zero_scr)

        def zcopy(e):
            start = pl.multiple_of(pend_ref[e] - ROW_BLOCK, ROW_BLOCK)
            return pltpu.make_async_copy(zero_scr, xin_ref.at[pl.ds(start, ROW_BLOCK)], zsem.at[0])

        def zstart(e, carry):
            @pl.when(pend_ref[e] > pstart_ref[e])
            def _():
                zcopy(e).start()
            return carry

        def zwait(e, carry):
            @pl.when(pend_ref[e] > pstart_ref[e])
            def _():
                zcopy(e).wait()
            return carry

        n_used = pend_ref[N_EXPERTS - 1] // ROW_BLOCK

        def tcopy(b):
            return pltpu.make_async_copy(
                zero_scr, xin_ref.at[pl.ds(pl.multiple_of(b * ROW_BLOCK, ROW_BLOCK), ROW_BLOCK)], zsem.at[0])

        def tstart(b, carry):
            tcopy(b).start()
            return carry

        def twait(b, carry):
            tcopy(b).wait()
            return carry

        lax.fori_loop(0, N_EXPERTS, zstart, 0)
        lax.fori_loop(n_used, N_BLOCKS, tstart, 0)
        lax.fori_loop(0, N_EXPERTS, zwait, 0)
        lax.fori_loop(n_used, N_BLOCKS, twait, 0)

    def start(r, carry):
        pltpu.make_async_copy(hn_ref.at[r], xin_ref.at[_dest_row(c0_ref, pstart_ref, base + r)], sem.at[0]).start()
        pltpu.make_async_copy(hn_ref.at[r], xin_ref.at[_dest_row(c1_ref, pstart_ref, base + r)], sem.at[1]).start()
        return carry

    lax.fori_loop(0, DISP_TM, start, 0, unroll=DMA_UNROLL)
    pltpu.make_async_copy(hn_ref, xin_ref.at[pl.ds(0, DISP_TM)], sem.at[0]).wait()
    pltpu.make_async_copy(hn_ref, xin_ref.at[pl.ds(0, DISP_TM)], sem.at[1]).wait()


def _dispatch(code0, code1, pstart, pend, hn):
    grid_spec = pltpu.PrefetchScalarGridSpec(
        num_scalar_prefetch=4,
        grid=(N_TOK // DISP_TM,),
        in_specs=[pl.BlockSpec((DISP_TM, TOK_SUB, LANES), lambda i, *_: (i, 0, 0))],
        out_specs=pl.BlockSpec(memory_space=pl.ANY),
        scratch_shapes=[pltpu.VMEM((ROW_BLOCK, TOK_SUB, LANES), BF16),
                        pltpu.SemaphoreType.DMA((2,)), pltpu.SemaphoreType.DMA((1,))],
    )
    return pl.pallas_call(
        _dispatch_kernel,
        grid_spec=grid_spec,
        out_shape=jax.ShapeDtypeStruct((P_ROWS, TOK_SUB, LANES), BF16),
        compiler_params=pltpu.CompilerParams(dimension_semantics=("arbitrary",), vmem_limit_bytes=VMEM_LIMIT),
        name="dispatch",
    )(code0, code1, pstart, pend, hn)


def _expert_kernel(be_ref, nu_ref, x_ref, wg_ref, wu_ref, wd_ref, y_ref):
    i = pl.program_id(0)

    @pl.when(i < nu_ref[0])
    def _():
        x = x_ref[...].reshape(ROW_BLOCK, D_MODEL)
        g = _dot(x, wg_ref[0].astype(BF16))
        u = _dot(x, wu_ref[0].astype(BF16))
        hmid = (g * _sigmoid(g) * u).astype(BF16)
        y = _dot(hmid, wd_ref[0].astype(BF16)).astype(BF16)
        y_ref[...] = y.reshape(ROW_BLOCK, TOK_SUB, LANES)

    @pl.when(i >= nu_ref[0])
    def _():
        y_ref[...] = jnp.zeros_like(y_ref)


def _experts(blk_e, n_used, xin, w_gate, w_up, w_down):
    def xmap(i, be, nu):
        return (jnp.minimum(i, nu[0] - 1), 0, 0)

    def wmap(i, be, nu):
        return (be[i], 0, 0)

    grid_spec = pltpu.PrefetchScalarGridSpec(
        num_scalar_prefetch=2,
        grid=(N_BLOCKS,),
        in_specs=[pl.BlockSpec((ROW_BLOCK, TOK_SUB, LANES), xmap),
                  pl.BlockSpec((1, D_MODEL, D_EXPERT), wmap),
                  pl.BlockSpec((1, D_MODEL, D_EXPERT), wmap),
                  pl.BlockSpec((1, D_EXPERT, D_MODEL), wmap)],
        out_specs=pl.BlockSpec((ROW_BLOCK, TOK_SUB, LANES), lambda i, be, nu: (i, 0, 0)),
    )
    return pl.pallas_call(
        _expert_kernel,
        grid_spec=grid_spec,
        out_shape=jax.ShapeDtypeStruct((P_ROWS, TOK_SUB, LANES), BF16),
        compiler_params=pltpu.CompilerParams(dimension_semantics=("arbitrary",), vmem_limit_bytes=VMEM_LIMIT),
        name="experts",
    )(blk_e, n_used, xin, w_gate, w_up, w_down)


COMB_TM = 256


def _combine_kernel(c0_ref, c1_ref, pstart_ref, h1_ref, route_ref, yb_ref, out_ref, g_scr, sem):
    i = pl.program_id(0)
    n = pl.num_programs(0)

    def issue(tile, slot):
        base = tile * COMB_TM

        def start(r, carry):
            pltpu.make_async_copy(yb_ref.at[_dest_row(c0_ref, pstart_ref, base + r)], g_scr.at[slot, 0, r],
                                  sem.at[slot, 0]).start()
            pltpu.make_async_copy(yb_ref.at[_dest_row(c1_ref, pstart_ref, base + r)], g_scr.at[slot, 1, r],
                                  sem.at[slot, 1]).start()
            return carry

        lax.fori_loop(0, COMB_TM, start, 0, unroll=DMA_UNROLL)

    @pl.when(i == 0)
    def _():
        issue(0, 0)

    slot = i % 2

    @pl.when(i + 1 < n)
    def _():
        issue(i + 1, 1 - slot)

    for k in range(2):
        pltpu.make_async_copy(yb_ref.at[pl.ds(0, COMB_TM)], g_scr.at[slot, k], sem.at[slot, k]).wait()

    route = route_ref[...]
    w0 = route[:, 2:3]
    w1 = route[:, 3:4]
    g0 = g_scr[slot, 0].reshape(COMB_TM, D_MODEL).astype(F32)
    g1 = g_scr[slot, 1].reshape(COMB_TM, D_MODEL).astype(F32)
    out_ref[...] = h1_ref[...] + (g0 * w0 + g1 * w1)


def _combine(code0, code1, pstart, h1, route, yb):
    grid_spec = pltpu.PrefetchScalarGridSpec(
        num_scalar_prefetch=3,
        grid=(N_TOK // COMB_TM,),
        in_specs=[pl.BlockSpec((COMB_TM, D_MODEL), lambda i, *_: (i, 0)),
                  pl.BlockSpec((COMB_TM, LANES), lambda i, *_: (i, 0)),
                  pl.BlockSpec(memory_space=pl.ANY)],
        out_specs=pl.BlockSpec((COMB_TM, D_MODEL), lambda i, *_: (i, 0)),
        scratch_shapes=[pltpu.VMEM((2, 2, COMB_TM, TOK_SUB, LANES), BF16),
                        pltpu.SemaphoreType.DMA((2, 2))],
    )
    return pl.pallas_call(
        _combine_kernel,
        grid_spec=grid_spec,
        out_shape=jax.ShapeDtypeStruct((N_TOK, D_MODEL), F32),
        compiler_params=pltpu.CompilerParams(dimension_semantics=("arbitrary",), vmem_limit_bytes=VMEM_LIMIT),
        name="combine",
    )(code0, code1, pstart, h1, route, yb)


def kernel(x, meta, norm1_g, w_in, b_in, dw_w, dw_b, conv_ln_g, conv_ln_b, w_conv_out, b_conv_out,
           q_norm_g, k_norm_g, b_forget, w_attn_out, w_out, norm2_g, w_group, b_group, w_router,
           b_router, w_gate, w_up, w_down):
    l = 0
    row = lambda v: v.reshape(1, -1).astype(F32)

    c_u, c_q, c_f = 2 * CONV_WIDTH, 2 * CONV_WIDTH, 2 * CONV_WIDTH + 3 * ATTN_WIDTH
    c_gc = c_f + N_HEADS
    c_ga = c_gc + D_MODEL
    wi, bi = w_in[l], b_in[l]
    wu, bu = wi[:, :c_u].astype(BF16), row(bi[:c_u])
    wqkv, bqkv = wi[:, c_q:c_f].astype(BF16), row(bi[c_q:c_f])
    wft = jnp.zeros((16, D_MODEL), F32).at[:N_HEADS].set(wi[:, c_f:c_gc].T).astype(BF16)
    bft = jnp.zeros((16, 1), F32).at[:N_HEADS, 0].set(bi[c_f:c_gc] + b_forget[l])
    wgc, bgc = wi[:, c_gc:c_ga].astype(BF16), row(bi[c_gc:c_ga])
    wga, bga = wi[:, c_ga:].astype(BF16), row(bi[c_ga:])
    qg = row(jnp.tile(q_norm_g[l], N_HEADS) * (HEAD_DIM ** -0.5))
    kg = row(jnp.tile(k_norm_g[l], N_HEADS))
    head_of_col = jnp.arange(ATTN_WIDTH) // HEAD_DIM
    hsum = (head_of_col[:, None] == jnp.arange(LANES)[None, :]).astype(BF16)
    hexp_half = (jnp.arange(LANES)[:, None] == head_of_col[None, :]).astype(BF16)
    hexp = jnp.concatenate([hexp_half, hexp_half], axis=0)
    proj_consts = (row(norm1_g[l]), wu, bu, wqkv, bqkv, wft, bft, wgc, bgc, wga, bga, qg, kg, hsum, hexp)

    x2d = x.reshape(N_TOK, D_MODEL)
    meta_pad = jnp.zeros((META_ROWS, D_MODEL), F32).at[:N_META].set(meta.astype(F32))
    a, q, k, v, lf, sgc, sga = _proj(x2d, proj_consts, 512)
    a_m, _, k_m, v_m, lf_m, _, _ = _proj(meta_pad, proj_consts, META_ROWS)

    act = _conv(a.reshape(BATCH, SEQ, CONV_WIDTH), a_m, dw_w[l].astype(F32), row(dw_b[l]),
                row(conv_ln_g[l]), row(conv_ln_b[l])).reshape(N_TOK, CONV_WIDTH)

    o = _attention(q, k, v, lf, k_m, v_m, lf_m)

    wrt = jnp.zeros((D_MODEL, LANES), F32).at[:, :N_GROUPS].set(w_group[l])
    wrt = wrt.at[:, N_GROUPS:N_GROUPS + N_EXPERTS].set(w_router[l]).astype(BF16)
    brt = jnp.zeros((1, LANES), F32).at[0, :N_GROUPS].set(b_group[l])
    brt = brt.at[0, N_GROUPS:N_GROUPS + N_EXPERTS].set(b_router[l])
    tri = jnp.tril(jnp.ones((MERGE_TM, MERGE_TM), F32), -1).astype(BF16)
    merge_consts = (w_conv_out[l].astype(BF16), row(b_conv_out[l]), w_attn_out[l].astype(BF16),
                    w_out[l].astype(BF16), row(norm2_g[l]), wrt, brt, tri)
    h1, hn, route, route_t, cnt = _merge(act, o, sgc, sga, x2d, merge_consts)

    rt = route_t.astype(jnp.int32)
    code0 = rt[0] * (1 << RANK_BITS) + rt[4]
    code1 = rt[1] * (1 << RANK_BITS) + rt[5]
    counts = cnt[0, :N_EXPERTS].astype(jnp.int32)
    padded = (counts + ROW_BLOCK - 1) // ROW_BLOCK * ROW_BLOCK
    pend = jnp.cumsum(padded).astype(jnp.int32)
    pstart = pend - padded
    blk_row = jnp.arange(N_BLOCKS, dtype=jnp.int32) * ROW_BLOCK
    blk_e = jnp.minimum(jnp.sum((pend[None, :] <= blk_row[:, None]).astype(jnp.int32), axis=1), N_EXPERTS - 1)
    n_used = pend[-1:] // ROW_BLOCK

    xin = _dispatch(code0, code1, pstart, pend, hn)
    yb = _experts(blk_e, n_used, xin, w_gate[l], w_up[l], w_down[l])
    out = _combine(code0, code1, pstart, h1, route, yb)
    return out.reshape(BATCH, SEQ, D_MODEL)
```

```python
import functools

import jax
import jax.numpy as jnp
from jax import lax
from jax.experimental import pallas as pl
from jax.experimental.pallas import tpu as pltpu

D_MODEL = 1024
BATCH = 8
SEQ = 2048
N_META = 16
CONV_WIDTH = 512
CONV_K = 31
N_HEADS = 8
HEAD_DIM = 64
ATTN_WIDTH = N_HEADS * HEAD_DIM
N_GROUPS = 8
EXPERTS_PER_GROUP = 8
N_EXPERTS = N_GROUPS * EXPERTS_PER_GROUP
D_EXPERT = 256
RMS_EPS = 1e-6
LN_EPS = 1e-5

N_TOK = BATCH * SEQ
LANES = 128
META_ROWS = 128
ROW_BLOCK = 512
N_ASSIGN = 2 * N_TOK
N_BLOCKS = N_ASSIGN // ROW_BLOCK + N_EXPERTS
P_ROWS = N_BLOCKS * ROW_BLOCK
TOK_SUB = D_MODEL // LANES
RANK_BITS = 16
VMEM_LIMIT = 56 * 1024 * 1024

F32 = jnp.float32
BF16 = jnp.bfloat16
NEG_INF = float("-inf")


def _dot(a, b):
    return jnp.dot(a, b, preferred_element_type=F32)


def _dot_nt(a, b):
    return lax.dot_general(a, b, (((1,), (1,)), ((), ())), preferred_element_type=F32)


def _sigmoid(x):
    return 1.0 / (1.0 + jnp.exp(-x))


def _const_spec(shape):
    nd = len(shape)
    return pl.BlockSpec(shape, lambda *_: (0,) * nd)


def _head_rms(t, hsum_ref, hexp_ref, gain):
    ss = _dot((t * t).astype(BF16), hsum_ref[...])
    inv = lax.rsqrt(ss * (1.0 / HEAD_DIM) + RMS_EPS)
    hi = inv.astype(BF16)
    lo = (inv - hi.astype(F32)).astype(BF16)
    invb = _dot(jnp.concatenate([hi, lo], axis=1), hexp_ref[...])
    return t * invb * gain


def _proj_kernel(x_ref, g1_ref, wu_ref, bu_ref, wqkv_ref, bqkv_ref, wft_ref, bft_ref, wgc_ref, bgc_ref,
                 wga_ref, bga_ref, qg_ref, kg_ref, hsum_ref, hexp_ref,
                 a_ref, q_ref, k_ref, v_ref, lf_ref, sgc_ref, sga_ref):
    x = x_ref[...]
    ms = jnp.mean(x * x, axis=-1, keepdims=True)
    xn = (x * lax.rsqrt(ms + RMS_EPS) * g1_ref[...]).astype(BF16)

    u = _dot(xn, wu_ref[...]) + bu_ref[...]
    a_ref[...] = (u[:, :CONV_WIDTH] * _sigmoid(u[:, CONV_WIDTH:])).astype(BF16)

    qkv = _dot(xn, wqkv_ref[...]) + bqkv_ref[...]
    q_ref[...] = _head_rms(qkv[:, :ATTN_WIDTH], hsum_ref, hexp_ref, qg_ref[...]).astype(BF16)
    k_ref[...] = _head_rms(qkv[:, ATTN_WIDTH:2 * ATTN_WIDTH], hsum_ref, hexp_ref, kg_ref[...]).astype(BF16)
    v_ref[...] = qkv[:, 2 * ATTN_WIDTH:].astype(BF16)

    f = _dot_nt(wft_ref[...], xn) + bft_ref[...]
    lf_ref[...] = jnp.minimum(f, 0.0) - jnp.log(1.0 + jnp.exp(-jnp.abs(f)))

    sgc_ref[...] = _sigmoid(_dot(xn, wgc_ref[...]) + bgc_ref[...]).astype(BF16)
    sga_ref[...] = _sigmoid(_dot(xn, wga_ref[...]) + bga_ref[...]).astype(BF16)


def _proj(x2d, consts, tm):
    n = x2d.shape[0]
    row = lambda w: pl.BlockSpec((tm, w), lambda i: (i, 0))
    in_specs = [row(D_MODEL)] + [_const_spec(c.shape) for c in consts]
    out_shape = [
        jax.ShapeDtypeStruct((n, CONV_WIDTH), BF16),
        jax.ShapeDtypeStruct((n, ATTN_WIDTH), BF16),
        jax.ShapeDtypeStruct((n, ATTN_WIDTH), BF16),
        jax.ShapeDtypeStruct((n, ATTN_WIDTH), BF16),
        jax.ShapeDtypeStruct((16, n), F32),
        jax.ShapeDtypeStruct((n, D_MODEL), BF16),
        jax.ShapeDtypeStruct((n, D_MODEL), BF16),
    ]
    out_specs = [row(CONV_WIDTH), row(ATTN_WIDTH), row(ATTN_WIDTH), row(ATTN_WIDTH),
                 pl.BlockSpec((16, tm), lambda i: (0, i)), row(D_MODEL), row(D_MODEL)]
    return pl.pallas_call(
        _proj_kernel,
        grid=(n // tm,),
        in_specs=in_specs,
        out_specs=out_specs,
        out_shape=out_shape,
        compiler_params=pltpu.CompilerParams(dimension_semantics=("arbitrary",), vmem_limit_bytes=VMEM_LIMIT),
        name="proj",
    )(x2d, *consts)


CONV_CHUNK = 32
CONV_PAD = 32
CONV_WIN = 2 * CONV_CHUNK
CONV_SHROWS = CONV_WIN


def _conv_kernel(a_ref, am_ref, shift_ref, dw_ref, dwb_ref, lng_ref, lnb_ref, o_ref, pad_scr, sha_scr, shb_scr):
    pad_scr[0:N_META, :] = jnp.zeros((N_META, CONV_WIDTH), BF16)
    pad_scr[N_META:CONV_PAD, :] = am_ref[0:N_META, :]
    pad_scr[CONV_PAD:CONV_PAD + SEQ, :] = a_ref[0]

    def shifted_copies(ci):
        r0 = pl.multiple_of(ci * CONV_CHUNK, CONV_CHUNK)
        return _dot(shift_ref[...], pad_scr[pl.ds(r0, CONV_WIN), :])

    n_chunks = SEQ // CONV_CHUNK
    sha_scr[...] = shifted_copies(0)

    def finish(ci, cur_scr):
        r0 = pl.multiple_of(ci * CONV_CHUNK, CONV_CHUNK)
        acc = jnp.zeros((CONV_CHUNK, CONV_WIDTH), F32) + dwb_ref[...]
        for j in range(CONV_K):
            off = j + 2
            b = off % 8
            a0 = b * CONV_SHROWS + (off - b)
            acc = acc + cur_scr[a0:a0 + CONV_CHUNK, :] * dw_ref[j:j + 1, :]
        mu = jnp.mean(acc, axis=-1, keepdims=True)
        d = acc - mu
        var = jnp.mean(d * d, axis=-1, keepdims=True)
        y = d * lax.rsqrt(var + LN_EPS) * lng_ref[...] + lnb_ref[...]
        o_ref[0, pl.ds(r0, CONV_CHUNK), :] = (y * _sigmoid(y)).astype(BF16)

    def chunk_pair(i, carry):
        ci = 2 * i
        shb_scr[...] = shifted_copies(ci + 1)
        finish(ci, sha_scr)
        sha_scr[...] = shifted_copies(jnp.minimum(ci + 2, n_chunks - 1))
        finish(ci + 1, shb_scr)
        return carry

    lax.fori_loop(0, n_chunks // 2, chunk_pair, 0)


def _conv(a3, a_meta, dw_w, dw_b, ln_g, ln_b):
    rr = jnp.arange(8 * CONV_SHROWS)
    shift = ((rr % CONV_SHROWS + rr // CONV_SHROWS)[:, None] == jnp.arange(CONV_WIN)[None, :]).astype(BF16)
    return pl.pallas_call(
        _conv_kernel,
        grid=(BATCH,),
        in_specs=[pl.BlockSpec((1, SEQ, CONV_WIDTH), lambda b: (b, 0, 0)),
                  _const_spec(a_meta.shape), _const_spec(shift.shape), _const_spec(dw_w.shape),
                  _const_spec(dw_b.shape), _const_spec(ln_g.shape), _const_spec(ln_b.shape)],
        out_specs=pl.BlockSpec((1, SEQ, CONV_WIDTH), lambda b: (b, 0, 0)),
        out_shape=jax.ShapeDtypeStruct((BATCH, SEQ, CONV_WIDTH), BF16),
        scratch_shapes=[pltpu.VMEM((CONV_PAD + SEQ, CONV_WIDTH), BF16),
                        pltpu.VMEM((8 * CONV_SHROWS, CONV_WIDTH), F32),
                        pltpu.VMEM((8 * CONV_SHROWS, CONV_WIDTH), F32)],
        compiler_params=pltpu.CompilerParams(dimension_semantics=("arbitrary",), vmem_limit_bytes=VMEM_LIMIT),
        name="conv",
    )(a3, a_meta, shift, dw_w, dw_b, ln_g, ln_b)


ATT_TQ = 512
ATT_TK = 256
N_PAIRS = N_HEADS // 2
assert (ATT_TQ // ATT_TK) % 2 == 0


def _split3(x):
    hi = x.astype(BF16)
    r1 = x - hi.astype(F32)
    mid = r1.astype(BF16)
    lo = (r1 - mid.astype(F32)).astype(BF16)
    return jnp.concatenate([hi, mid, lo], axis=0)


def _attn_kernel(q_ref, k_ref, v_ref, lf_ref, km_ref, vm_ref, lfm_ref, o_ref,
                 c_scr, cm_scr, m_scr, acc_scr, sa_scr, sb_scr):
    p = pl.program_id(1)
    qi = pl.program_id(2)
    nk = SEQ // ATT_TK

    @pl.when((p == 0) & (qi == 0))
    def _():
        r = lax.broadcasted_iota(jnp.int32, (ATT_TK, ATT_TK), 0)
        c = lax.broadcasted_iota(jnp.int32, (ATT_TK, ATT_TK), 1)
        tri = (r <= c).astype(BF16)
        carry = jnp.zeros((N_HEADS, 1), F32)
        for j in range(nk):
            parts = _dot(_split3(lf_ref[0:N_HEADS, j * ATT_TK:(j + 1) * ATT_TK]), tri)
            cj = parts[0:8] + parts[8:16] + parts[16:24] + carry
            c_scr[j] = cj
            carry = cj[:, ATT_TK - 1:ATT_TK]
        rm = lax.broadcasted_iota(jnp.int32, (LANES, LANES), 0)
        cmx = lax.broadcasted_iota(jnp.int32, (LANES, LANES), 1)
        trim = ((rm > cmx) & (rm < N_META)).astype(BF16)
        pm = _dot(_split3(lfm_ref[0:N_HEADS, :]), trim)
        cm_scr[...] = -(pm[0:8] + pm[8:16] + pm[16:24])

    lane = lax.broadcasted_iota(jnp.int32, (ATT_TQ, LANES), 1)
    q_all = q_ref[...]
    zq = jnp.zeros_like(q_all)
    qm = (jnp.where(lane < HEAD_DIM, q_all, zq), jnp.where(lane >= HEAD_DIM, q_all, zq))
    for hl in range(2):
        m_scr[hl] = jnp.full((ATT_TQ, LANES), NEG_INF, F32)
        acc_scr[hl] = jnp.zeros((ATT_TQ, LANES), F32)

    def scores(kb, crows):
        return tuple(_dot_nt(qm[hl], kb) - crows[hl] for hl in range(2))

    def update(s2, vb, mask):
        vlane = lax.broadcasted_iota(jnp.int32, vb.shape, 1)
        ones = jnp.ones_like(vb)
        for hl in range(2):
            s = s2[hl]
            if mask is not None:
                s = jnp.where(mask, s, NEG_INF)
            m_prev = m_scr[hl]
            m_next = jnp.maximum(m_prev, jnp.max(s, axis=1, keepdims=True))
            reps = s.shape[1] // LANES
            m_wide = m_next if reps == 1 else jnp.concatenate([m_next] * reps, axis=1)
            pr = jnp.exp(s - m_wide).astype(BF16)
            alpha = jnp.exp(m_prev - m_next)
            own = (vlane < HEAD_DIM) if hl == 0 else (vlane >= HEAD_DIM)
            acc_scr[hl] = acc_scr[hl] * alpha + _dot(pr, jnp.where(own, vb, ones))
            m_scr[hl] = m_next

    def crows_at(j):
        return (c_scr[j, pl.ds(2 * p, 1), :], c_scr[j, pl.ds(2 * p + 1, 1), :])

    def kblock(j):
        return k_ref[pl.ds(pl.multiple_of(j * ATT_TK, ATT_TK), ATT_TK), :]

    def vblock(j):
        return v_ref[pl.ds(pl.multiple_of(j * ATT_TK, ATT_TK), ATT_TK), :]

    update(scores(km_ref[...], (cm_scr[pl.ds(2 * p, 1), :], cm_scr[pl.ds(2 * p + 1, 1), :])), vm_ref[...],
           lane < N_META)

    nfull = qi * (ATT_TQ // ATT_TK)

    def put(scr, s2):
        scr[0] = s2[0]
        scr[1] = s2[1]

    put(sa_scr, scores(kblock(0), crows_at(0)))

    def full2(i, carry):
        j = 2 * i
        put(sb_scr, scores(kblock(j + 1), crows_at(j + 1)))
        update((sa_scr[0], sa_scr[1]), vblock(j), None)
        put(sa_scr, scores(kblock(j + 2), crows_at(j + 2)))
        update((sb_scr[0], sb_scr[1]), vblock(j + 1), None)
        return carry

    lax.fori_loop(0, nfull // 2, full2, 0)

    rows = lax.broadcasted_iota(jnp.int32, (ATT_TQ, ATT_TK), 0)
    cols = lax.broadcasted_iota(jnp.int32, (ATT_TQ, ATT_TK), 1)
    s_diag = [(sa_scr[0], sa_scr[1])]
    for dj in range(1, ATT_TQ // ATT_TK):
        s_diag.append(scores(kblock(nfull + dj), crows_at(nfull + dj)))
    for dj in range(ATT_TQ // ATT_TK):
        update(s_diag[dj], vblock(nfull + dj), cols + dj * ATT_TK <= rows)

    a0 = acc_scr[0]
    a1 = acc_scr[1]
    o0 = a0 / pltpu.roll(a0, HEAD_DIM, axis=1)
    o1 = a1 / pltpu.roll(a1, HEAD_DIM, axis=1)
    o_ref[...] = jnp.where(lane < HEAD_DIM, o0, o1).astype(BF16)


def _attention(q, k, v, lf, km, vm, lfm):
    nq = SEQ // ATT_TQ
    return pl.pallas_call(
        _attn_kernel,
        grid=(BATCH, N_PAIRS, nq),
        in_specs=[
            pl.BlockSpec((ATT_TQ, LANES), lambda b, p, i: (b * nq + i, p)),
            pl.BlockSpec((SEQ, LANES), lambda b, p, i: (b, p)),
            pl.BlockSpec((SEQ, LANES), lambda b, p, i: (b, p)),
            pl.BlockSpec((16, SEQ), lambda b, p, i: (0, b)),
            pl.BlockSpec((META_ROWS, LANES), lambda b, p, i: (0, p)),
            pl.BlockSpec((META_ROWS, LANES), lambda b, p, i: (0, p)),
            pl.BlockSpec((16, META_ROWS), lambda b, p, i: (0, 0)),
        ],
        out_specs=pl.BlockSpec((ATT_TQ, LANES), lambda b, p, i: (b * nq + i, p)),
        out_shape=jax.ShapeDtypeStruct((N_TOK, ATTN_WIDTH), BF16),
        scratch_shapes=[
            pltpu.VMEM((SEQ // ATT_TK, N_HEADS, ATT_TK), F32),
            pltpu.VMEM((N_HEADS, LANES), F32),
            pltpu.VMEM((2, ATT_TQ, LANES), F32),
            pltpu.VMEM((2, ATT_TQ, LANES), F32),
            pltpu.VMEM((2, ATT_TQ, ATT_TK), F32),
            pltpu.VMEM((2, ATT_TQ, ATT_TK), F32),
        ],
        compiler_params=pltpu.CompilerParams(
            dimension_semantics=("arbitrary", "arbitrary", "arbitrary"), vmem_limit_bytes=VMEM_LIMIT),
        name="attn",
    )(q, k, v, lf, km, vm, lfm)


MERGE_TM = 512


def _merge_kernel(act_ref, o_ref, sgc_ref, sga_ref, x_ref, wpw_ref, bpw_ref, wao_ref, wout_ref, g2_ref,
                  wrt_ref, brt_ref, tri_ref,
                  h1_ref, hn_ref, route_ref, routet_ref, cnt_ref, base_scr):
    i = pl.program_id(0)

    @pl.when(i == 0)
    def _():
        base_scr[...] = jnp.zeros_like(base_scr)

    y_conv = _dot(act_ref[...], wpw_ref[...]) + bpw_ref[...]
    y_attn = _dot(o_ref[...], wao_ref[...])
    merged = sgc_ref[...].astype(F32) * y_conv + sga_ref[...].astype(F32) * y_attn
    h1 = x_ref[...] + _dot(merged.astype(BF16), wout_ref[...])
    h1_ref[...] = h1
    ms = jnp.mean(h1 * h1, axis=-1, keepdims=True)
    hn = (h1 * lax.rsqrt(ms + RMS_EPS) * g2_ref[...]).astype(BF16)
    hn_ref[...] = hn.reshape(hn.shape[0], TOK_SUB, LANES)

    logits = _dot(hn, wrt_ref[...]) + brt_ref[...]
    lane = lax.broadcasted_iota(jnp.int32, logits.shape, 1).astype(F32)
    big = float(LANES)
    is_g = lane < N_GROUPS
    gl = jnp.where(is_g, logits, NEG_INF)
    gmax = jnp.max(gl, axis=1, keepdims=True)
    gidx = jnp.min(jnp.where(gl == gmax, lane, big), axis=1, keepdims=True)
    gsum = jnp.sum(jnp.where(is_g, jnp.exp(gl - gmax), 0.0), axis=1, keepdims=True)
    g_w = 1.0 / gsum
    lo = N_GROUPS + EXPERTS_PER_GROUP * gidx
    el = jnp.where((lane >= lo) & (lane < lo + EXPERTS_PER_GROUP), logits, NEG_INF)
    t1 = jnp.max(el, axis=1, keepdims=True)
    i1 = jnp.min(jnp.where(el == t1, lane, big), axis=1, keepdims=True)
    el2 = jnp.where(lane == i1, NEG_INF, el)
    t2 = jnp.max(el2, axis=1, keepdims=True)
    i2 = jnp.min(jnp.where(el2 == t2, lane, big), axis=1, keepdims=True)
    e2 = jnp.exp(t2 - t1)
    den = 1.0 + e2
    w1 = (1.0 / den) * g_w
    w2 = (e2 / den) * g_w
    ea = i1 - N_GROUPS
    eb = i2 - N_GROUPS

    oha = (lane == ea).astype(F32)
    ohb = (lane == eb).astype(F32)
    ohs = oha + ohb
    before = _dot(tri_ref[...], ohs.astype(BF16)) + base_scr[0:1, :]
    ra = jnp.sum(before * oha, axis=1, keepdims=True)
    rb = jnp.sum(before * ohb, axis=1, keepdims=True)
    base_scr[...] = base_scr[...] + jnp.sum(ohs, axis=0, keepdims=True)
    cnt_ref[...] = base_scr[...]

    route = jnp.where(lane == 0, ea, 0.0)
    route = jnp.where(lane == 1, eb, route)
    route = jnp.where(lane == 2, w1, route)
    route = jnp.where(lane == 3, w2, route)
    route = jnp.where(lane == 4, ra, route)
    route = jnp.where(lane == 5, rb, route)
    route_ref[...] = route
    routet_ref[...] = route.T[0:8]


def _merge(act, o, sgc, sga, x2d, consts):
    tm = MERGE_TM
    row = lambda w: pl.BlockSpec((tm, w), lambda i: (i, 0))
    return pl.pallas_call(
        _merge_kernel,
        grid=(N_TOK // tm,),
        in_specs=[row(CONV_WIDTH), row(ATTN_WIDTH), row(D_MODEL), row(D_MODEL), row(D_MODEL)]
        + [_const_spec(c.shape) for c in consts],
        out_specs=[row(D_MODEL), pl.BlockSpec((tm, TOK_SUB, LANES), lambda i: (i, 0, 0)), row(LANES),
                   pl.BlockSpec((8, tm), lambda i: (0, i)), _const_spec((8, LANES))],
        out_shape=[jax.ShapeDtypeStruct((N_TOK, D_MODEL), F32),
                   jax.ShapeDtypeStruct((N_TOK, TOK_SUB, LANES), BF16),
                   jax.ShapeDtypeStruct((N_TOK, LANES), F32),
                   jax.ShapeDtypeStruct((8, N_TOK), F32),
                   jax.ShapeDtypeStruct((8, LANES), F32)],
        scratch_shapes=[pltpu.VMEM((8, LANES), F32)],
        compiler_params=pltpu.CompilerParams(dimension_semantics=("arbitrary",), vmem_limit_bytes=VMEM_LIMIT),
        name="merge",
    )(act, o, sgc, sga, x2d, *consts)


DISP_TM = 1024
DMA_UNROLL = 8


def _dest_row(code_ref, pstart_ref, idx):
    c = code_ref[idx]
    return pstart_ref[lax.shift_right_logical(c, RANK_BITS)] + (c & ((1 << RANK_BITS) - 1))


def _dispatch_kernel(c0_ref, c1_ref, pstart_ref, pend_ref, hn_ref, xin_ref, zero_scr, sem, zsem):
    i = pl.program_id(0)
    base = i * DISP_TM

    @pl.when(i == 0)
    def _():
        zero_scr[...] = jnp.zeros_like(zero_scr)

        def zcopy(e):
            start = pl.multiple_of(pend_ref[e] - ROW_BLOCK, ROW_BLOCK)
            return pltpu.make_async_copy(zero_scr, xin_ref.at[pl.ds(start, ROW_BLOCK)], zsem.at[0])

        def zstart(e, carry):
            @pl.when(pend_ref[e] > pstart_ref[e])
            def _():
                zcopy(e).start()
            return carry

        def zwait(e, carry):
            @pl.when(pend_ref[e] > pstart_ref[e])
            def _():
                zcopy(e).wait()
            return carry

        n_used = pend_ref[N_EXPERTS - 1] // ROW_BLOCK

        def tcopy(b):
            return pltpu.make_async_copy(
                zero_scr, xin_ref.at[pl.ds(pl.multiple_of(b * ROW_BLOCK, ROW_BLOCK), ROW_BLOCK)], zsem.at[0])

        def tstart(b, carry):
            tcopy(b).start()
            return carry

        def twait(b, carry):
            tcopy(b).wait()
            return carry

        lax.fori_loop(0, N_EXPERTS, zstart, 0)
        lax.fori_loop(n_used, N_BLOCKS, tstart, 0)
        lax.fori_loop(0, N_EXPERTS, zwait, 0)
        lax.fori_loop(n_used, N_BLOCKS, twait, 0)

    def start(r, carry):
        pltpu.make_async_copy(hn_ref.at[r], xin_ref.at[_dest_row(c0_ref, pstart_ref, base + r)], sem.at[0]).start()
        pltpu.make_async_copy(hn_ref.at[r], xin_ref.at[_dest_row(c1_ref, pstart_ref, base + r)], sem.at[1]).start()
        return carry

    lax.fori_loop(0, DISP_TM, start, 0, unroll=DMA_UNROLL)
    pltpu.make_async_copy(hn_ref, xin_ref.at[pl.ds(0, DISP_TM)], sem.at[0]).wait()
    pltpu.make_async_copy(hn_ref, xin_ref.at[pl.ds(0, DISP_TM)], sem.at[1]).wait()


def _dispatch(code0, code1, pstart, pend, hn):
    grid_spec = pltpu.PrefetchScalarGridSpec(
        num_scalar_prefetch=4,
        grid=(N_TOK // DISP_TM,),
        in_specs=[pl.BlockSpec((DISP_TM, TOK_SUB, LANES), lambda i, *_: (i, 0, 0))],
        out_specs=pl.BlockSpec(memory_space=pl.ANY),
        scratch_shapes=[pltpu.VMEM((ROW_BLOCK, TOK_SUB, LANES), BF16),
                        pltpu.SemaphoreType.DMA((2,)), pltpu.SemaphoreType.DMA((1,))],
    )
    return pl.pallas_call(
        _dispatch_kernel,
        grid_spec=grid_spec,
        out_shape=jax.ShapeDtypeStruct((P_ROWS, TOK_SUB, LANES), BF16),
        compiler_params=pltpu.CompilerParams(dimension_semantics=("arbitrary",), vmem_limit_bytes=VMEM_LIMIT),
        name="dispatch",
    )(code0, code1, pstart, pend, hn)


def _expert_kernel(be_ref, nu_ref, x_ref, wg_ref, wu_ref, wd_ref, y_ref):
    i = pl.program_id(0)

    @pl.when(i < nu_ref[0])
    def _():
        x = x_ref[...].reshape(ROW_BLOCK, D_MODEL)
        g = _dot(x, wg_ref[0].astype(BF16))
        u = _dot(x, wu_ref[0].astype(BF16))
        hmid = (g * _sigmoid(g) * u).astype(BF16)
        y = _dot(hmid, wd_ref[0].astype(BF16)).astype(BF16)
        y_ref[...] = y.reshape(ROW_BLOCK, TOK_SUB, LANES)

    @pl.when(i >= nu_ref[0])
    def _():
        y_ref[...] = jnp.zeros_like(y_ref)


def _experts(blk_e, n_used, xin, w_gate, w_up, w_down):
    def xmap(i, be, nu):
        return (jnp.minimum(i, nu[0] - 1), 0, 0)

    def wmap(i, be, nu):
        return (be[i], 0, 0)

    grid_spec = pltpu.PrefetchScalarGridSpec(
        num_scalar_prefetch=2,
        grid=(N_BLOCKS,),
        in_specs=[pl.BlockSpec((ROW_BLOCK, TOK_SUB, LANES), xmap),
                  pl.BlockSpec((1, D_MODEL, D_EXPERT), wmap),
                  pl.BlockSpec((1, D_MODEL, D_EXPERT), wmap),
                  pl.BlockSpec((1, D_EXPERT, D_MODEL), wmap)],
        out_specs=pl.BlockSpec((ROW_BLOCK, TOK_SUB, LANES), lambda i, be, nu: (i, 0, 0)),
    )
    return pl.pallas_call(
        _expert_kernel,
        grid_spec=grid_spec,
        out_shape=jax.ShapeDtypeStruct((P_ROWS, TOK_SUB, LANES), BF16),
        compiler_params=pltpu.CompilerParams(dimension_semantics=("arbitrary",), vmem_limit_bytes=VMEM_LIMIT),
        name="experts",
    )(blk_e, n_used, xin, w_gate, w_up, w_down)


COMB_TM = 256


def _combine_kernel(c0_ref, c1_ref, pstart_ref, h1_ref, route_ref, yb_ref, out_ref, g_scr, sem):
    i = pl.program_id(0)
    n = pl.num_programs(0)

    def issue(tile, slot):
        base = tile * COMB_TM

        def start(r, carry):
            pltpu.make_async_copy(yb_ref.at[_dest_row(c0_ref, pstart_ref, base + r)], g_scr.at[slot, 0, r],
                                  sem.at[slot, 0]).start()
            pltpu.make_async_copy(yb_ref.at[_dest_row(c1_ref, pstart_ref, base + r)], g_scr.at[slot, 1, r],
                                  sem.at[slot, 1]).start()
            return carry

        lax.fori_loop(0, COMB_TM, start, 0, unroll=DMA_UNROLL)

    @pl.when(i == 0)
    def _():
        issue(0, 0)

    slot = i % 2

    @pl.when(i + 1 < n)
    def _():
        issue(i + 1, 1 - slot)

    for k in range(2):
        pltpu.make_async_copy(yb_ref.at[pl.ds(0, COMB_TM)], g_scr.at[slot, k], sem.at[slot, k]).wait()

    route = route_ref[...]
    w0 = route[:, 2:3]
    w1 = route[:, 3:4]
    g0 = g_scr[slot, 0].reshape(COMB_TM, D_MODEL).astype(F32)
    g1 = g_scr[slot, 1].reshape(COMB_TM, D_MODEL).astype(F32)
    out_ref[...] = h1_ref[...] + (g0 * w0 + g1 * w1)


def _combine(code0, code1, pstart, h1, route, yb):
    grid_spec = pltpu.PrefetchScalarGridSpec(
        num_scalar_prefetch=3,
        grid=(N_TOK // COMB_TM,),
        in_specs=[pl.BlockSpec((COMB_TM, D_MODEL), lambda i, *_: (i, 0)),
                  pl.BlockSpec((COMB_TM, LANES), lambda i, *_: (i, 0)),
                  pl.BlockSpec(memory_space=pl.ANY)],
        out_specs=pl.BlockSpec((COMB_TM, D_MODEL), lambda i, *_: (i, 0)),
        scratch_shapes=[pltpu.VMEM((2, 2, COMB_TM, TOK_SUB, LANES), BF16),
                        pltpu.SemaphoreType.DMA((2, 2))],
    )
    return pl.pallas_call(
        _combine_kernel,
        grid_spec=grid_spec,
        out_shape=jax.ShapeDtypeStruct((N_TOK, D_MODEL), F32),
        compiler_params=pltpu.CompilerParams(dimension_semantics=("arbitrary",), vmem_limit_bytes=VMEM_LIMIT),
        name="combine",
    )(code0, code1, pstart, h1, route, yb)


def kernel(x, meta, norm1_g, w_in, b_in, dw_w, dw_b, conv_ln_g, conv_ln_b, w_conv_out, b_conv_out,
           q_norm_g, k_norm_g, b_forget, w_attn_out, w_out, norm2_g, w_group, b_group, w_router,
           b_router, w_gate, w_up, w_down):
    l = 0
    row = lambda v: v.reshape(1, -1).astype(F32)

    c_u, c_q, c_f = 2 * CONV_WIDTH, 2 * CONV_WIDTH, 2 * CONV_WIDTH + 3 * ATTN_WIDTH
    c_gc = c_f + N_HEADS
    c_ga = c_gc + D_MODEL
    wi, bi = w_in[l], b_in[l]
    wu, bu = wi[:, :c_u].astype(BF16), row(bi[:c_u])
    wqkv, bqkv = wi[:, c_q:c_f].astype(BF16), row(bi[c_q:c_f])
    wft = jnp.zeros((16, D_MODEL), F32).at[:N_HEADS].set(wi[:, c_f:c_gc].T).astype(BF16)
    bft = jnp.zeros((16, 1), F32).at[:N_HEADS, 0].set(bi[c_f:c_gc] + b_forget[l])
    wgc, bgc = wi[:, c_gc:c_ga].astype(BF16), row(bi[c_gc:c_ga])
    wga, bga = wi[:, c_ga:].astype(BF16), row(bi[c_ga:])
    qg = row(jnp.tile(q_norm_g[l], N_HEADS) * (HEAD_DIM ** -0.5))
    kg = row(jnp.tile(k_norm_g[l], N_HEADS))
    head_of_col = jnp.arange(ATTN_WIDTH) // HEAD_DIM
    hsum = (head_of_col[:, None] == jnp.arange(LANES)[None, :]).astype(BF16)
    hexp_half = (jnp.arange(LANES)[:, None] == head_of_col[None, :]).astype(BF16)
    hexp = jnp.concatenate([hexp_half, hexp_half], axis=0)
    proj_consts = (row(norm1_g[l]), wu, bu, wqkv, bqkv, wft, bft, wgc, bgc, wga, bga, qg, kg, hsum, hexp)

    x2d = x.reshape(N_TOK, D_MODEL)
    meta_pad = jnp.zeros((META_ROWS, D_MODEL), F32).at[:N_META].set(meta.astype(F32))
    a, q, k, v, lf, sgc, sga = _proj(x2d, proj_consts, 512)
    a_m, _, k_m, v_m, lf_m, _, _ = _proj(meta_pad, proj_consts, META_ROWS)

    act = _conv(a.reshape(BATCH, SEQ, CONV_WIDTH), a_m, dw_w[l].astype(F32), row(dw_b[l]),
                row(conv_ln_g[l]), row(conv_ln_b[l])).reshape(N_TOK, CONV_WIDTH)

    o = _attention(q, k, v, lf, k_m, v_m, lf_m)

    wrt = jnp.zeros((D_MODEL, LANES), F32).at[:, :N_GROUPS].set(w_group[l])
    wrt = wrt.at[:, N_GROUPS:N_GROUPS + N_EXPERTS].set(w_router[l]).astype(BF16)
    brt = jnp.zeros((1, LANES), F32).at[0, :N_GROUPS].set(b_group[l])
    brt = brt.at[0, N_GROUPS:N_GROUPS + N_EXPERTS].set(b_router[l])
    tri = jnp.tril(jnp.ones((MERGE_TM, MERGE_TM), F32), -1).astype(BF16)
    merge_consts = (w_conv_out[l].astype(BF16), row(b_conv_out[l]), w_attn_out[l].astype(BF16),
                    w_out[l].astype(BF16), row(norm2_g[l]), wrt, brt, tri)
    h1, hn, route, route_t, cnt = _merge(act, o, sgc, sga, x2d, merge_consts)

    rt = route_t.astype(jnp.int32)
    code0 = rt[0] * (1 << RANK_BITS) + rt[4]
    code1 = rt[1] * (1 << RANK_BITS) + rt[5]
    counts = cnt[0, :N_EXPERTS].astype(jnp.int32)
    padded = (counts + ROW_BLOCK - 1) // ROW_BLOCK * ROW_BLOCK
    pend = jnp.cumsum(padded).astype(jnp.int32)
    pstart = pend - padded
    blk_row = jnp.arange(N_BLOCKS, dtype=jnp.int32) * ROW_BLOCK
    blk_e = jnp.minimum(jnp.sum((pend[None, :] <= blk_row[:, None]).astype(jnp.int32), axis=1), N_EXPERTS - 1)
    n_used = pend[-1:] // ROW_BLOCK

    xin = _dispatch(code0, code1, pstart, pend, hn)
    yb = _experts(blk_e, n_used, xin, w_gate[l], w_up[l], w_down[l])
    out = _combine(code0, code1, pstart, h1, route, yb)
    return out.reshape(BATCH, SEQ, D_MODEL)
```

```python
import functools

import jax
import jax.numpy as jnp
from jax import lax
from jax.experimental import pallas as pl
from jax.experimental.pallas import tpu as pltpu

D_MODEL = 1024
BATCH = 8
SEQ = 2048
N_META = 16
CONV_WIDTH = 512
CONV_K = 31
N_HEADS = 8
HEAD_DIM = 64
ATTN_WIDTH = N_HEADS * HEAD_DIM
N_GROUPS = 8
EXPERTS_PER_GROUP = 8
N_EXPERTS = N_GROUPS * EXPERTS_PER_GROUP
D_EXPERT = 256
RMS_EPS = 1e-6
LN_EPS = 1e-5

N_TOK = BATCH * SEQ
LANES = 128
META_ROWS = 128
ROW_BLOCK = 512
N_ASSIGN = 2 * N_TOK
N_BLOCKS = N_ASSIGN // ROW_BLOCK + N_EXPERTS
P_ROWS = N_BLOCKS * ROW_BLOCK
TOK_SUB = D_MODEL // LANES
RANK_BITS = 16
ATT_TQ = 512
ATT_TK = 256
N_PAIRS = N_HEADS // 2
assert (ATT_TQ // ATT_TK) % 2 == 0
VMEM_LIMIT = 56 * 1024 * 1024

F32 = jnp.float32
BF16 = jnp.bfloat16
NEG_INF = float("-inf")
LOG2E = 1.4426950408889634


def _dot(a, b):
    return jnp.dot(a, b, preferred_element_type=F32)


def _dot_nt(a, b):
    return lax.dot_general(a, b, (((1,), (1,)), ((), ())), preferred_element_type=F32)


def _sigmoid(x):
    return 1.0 / (1.0 + jnp.exp(-x))


def _const_spec(shape):
    nd = len(shape)
    return pl.BlockSpec(shape, lambda *_: (0,) * nd)


def _head_rms(t, hsum_ref, hexp_ref, gain):
    ss = _dot((t * t).astype(BF16), hsum_ref[...])
    inv = lax.rsqrt(ss * (1.0 / HEAD_DIM) + RMS_EPS)
    hi = inv.astype(BF16)
    lo = (inv - hi.astype(F32)).astype(BF16)
    invb = _dot(jnp.concatenate([hi, lo], axis=1), hexp_ref[...])
    return t * invb * gain


def _proj_kernel(x_ref, g1_ref, wu_ref, bu_ref, wqkv_ref, bqkv_ref, wft_ref, bft_ref, wgc_ref, bgc_ref,
                 wga_ref, bga_ref, qg_ref, kg_ref, hsum_ref, hexp_ref,
                 a_ref, q_ref, k_ref, v_ref, lf_ref, sgc_ref, sga_ref):
    x = x_ref[...]
    ms = jnp.mean(x * x, axis=-1, keepdims=True)
    xn = (x * lax.rsqrt(ms + RMS_EPS) * g1_ref[...]).astype(BF16)

    u = _dot(xn, wu_ref[...]) + bu_ref[...]
    a_ref[...] = (u[:, :CONV_WIDTH] * _sigmoid(u[:, CONV_WIDTH:])).astype(BF16)

    qkv = _dot(xn, wqkv_ref[...]) + bqkv_ref[...]
    q_ref[...] = _head_rms(qkv[:, :ATTN_WIDTH], hsum_ref, hexp_ref, qg_ref[...]).astype(BF16)
    k_ref[...] = _head_rms(qkv[:, ATTN_WIDTH:2 * ATTN_WIDTH], hsum_ref, hexp_ref, kg_ref[...]).astype(BF16)
    v = qkv[:, 2 * ATTN_WIDTH:]
    vb = v_ref.shape[2]
    for c in range(v_ref.shape[0]):
        v_ref[c] = v[c * vb:(c + 1) * vb, :].T.astype(BF16)

    f = _dot_nt(wft_ref[...], xn) + bft_ref[...]
    lf_ref[...] = jnp.minimum(f, 0.0) - jnp.log(1.0 + jnp.exp(-jnp.abs(f)))

    sgc_ref[...] = _sigmoid(_dot(xn, wgc_ref[...]) + bgc_ref[...]).astype(BF16)
    sga_ref[...] = _sigmoid(_dot(xn, wga_ref[...]) + bga_ref[...]).astype(BF16)


def _proj(x2d, consts, tm):
    n = x2d.shape[0]
    row = lambda w: pl.BlockSpec((tm, w), lambda i: (i, 0))
    in_specs = [row(D_MODEL)] + [_const_spec(c.shape) for c in consts]
    vb = min(tm, ATT_TK)
    out_shape = [
        jax.ShapeDtypeStruct((n, CONV_WIDTH), BF16),
        jax.ShapeDtypeStruct((n, ATTN_WIDTH), BF16),
        jax.ShapeDtypeStruct((n, ATTN_WIDTH), BF16),
        jax.ShapeDtypeStruct((n // vb, ATTN_WIDTH, vb), BF16),
        jax.ShapeDtypeStruct((16, n), F32),
        jax.ShapeDtypeStruct((n, D_MODEL), BF16),
        jax.ShapeDtypeStruct((n, D_MODEL), BF16),
    ]
    out_specs = [row(CONV_WIDTH), row(ATTN_WIDTH), row(ATTN_WIDTH),
                 pl.BlockSpec((tm // vb, ATTN_WIDTH, vb), lambda i: (i, 0, 0)),
                 pl.BlockSpec((16, tm), lambda i: (0, i)), row(D_MODEL), row(D_MODEL)]
    return pl.pallas_call(
        _proj_kernel,
        grid=(n // tm,),
        in_specs=in_specs,
        out_specs=out_specs,
        out_shape=out_shape,
        compiler_params=pltpu.CompilerParams(dimension_semantics=("arbitrary",), vmem_limit_bytes=VMEM_LIMIT),
        name="proj",
    )(x2d, *consts)


CONV_CHUNK = 32
CONV_PAD = 32
CONV_WIN = 2 * CONV_CHUNK
CONV_SHROWS = CONV_WIN


def _conv_kernel(a_ref, am_ref, shift_ref, dw_ref, dwb_ref, lng_ref, lnb_ref, o_ref, pad_scr, sha_scr, shb_scr):
    pad_scr[0:N_META, :] = jnp.zeros((N_META, CONV_WIDTH), BF16)
    pad_scr[N_META:CONV_PAD, :] = am_ref[0:N_META, :]
    pad_scr[CONV_PAD:CONV_PAD + SEQ, :] = a_ref[0]

    def shifted_copies(ci):
        r0 = pl.multiple_of(ci * CONV_CHUNK, CONV_CHUNK)
        return _dot(shift_ref[...], pad_scr[pl.ds(r0, CONV_WIN), :])

    n_chunks = SEQ // CONV_CHUNK
    sha_scr[...] = shifted_copies(0)

    def finish(ci, cur_scr):
        r0 = pl.multiple_of(ci * CONV_CHUNK, CONV_CHUNK)
        acc = jnp.zeros((CONV_CHUNK, CONV_WIDTH), F32) + dwb_ref[...]
        for j in range(CONV_K):
            off = j + 2
            b = off % 8
            a0 = b * CONV_SHROWS + (off - b)
            acc = acc + cur_scr[a0:a0 + CONV_CHUNK, :] * dw_ref[j:j + 1, :]
        mu = jnp.mean(acc, axis=-1, keepdims=True)
        d = acc - mu
        var = jnp.mean(d * d, axis=-1, keepdims=True)
        y = d * lax.rsqrt(var + LN_EPS) * lng_ref[...] + lnb_ref[...]
        o_ref[0, pl.ds(r0, CONV_CHUNK), :] = (y * _sigmoid(y)).astype(BF16)

    def chunk_pair(i, carry):
        ci = 2 * i
        shb_scr[...] = shifted_copies(ci + 1)
        finish(ci, sha_scr)
        sha_scr[...] = shifted_copies(jnp.minimum(ci + 2, n_chunks - 1))
        finish(ci + 1, shb_scr)
        return carry

    lax.fori_loop(0, n_chunks // 2, chunk_pair, 0)


def _conv(a3, a_meta, dw_w, dw_b, ln_g, ln_b):
    rr = jnp.arange(8 * CONV_SHROWS)
    shift = ((rr % CONV_SHROWS + rr // CONV_SHROWS)[:, None] == jnp.arange(CONV_WIN)[None, :]).astype(BF16)
    return pl.pallas_call(
        _conv_kernel,
        grid=(BATCH,),
        in_specs=[pl.BlockSpec((1, SEQ, CONV_WIDTH), lambda b: (b, 0, 0)),
                  _const_spec(a_meta.shape), _const_spec(shift.shape), _const_spec(dw_w.shape),
                  _const_spec(dw_b.shape), _const_spec(ln_g.shape), _const_spec(ln_b.shape)],
        out_specs=pl.BlockSpec((1, SEQ, CONV_WIDTH), lambda b: (b, 0, 0)),
        out_shape=jax.ShapeDtypeStruct((BATCH, SEQ, CONV_WIDTH), BF16),
        scratch_shapes=[pltpu.VMEM((CONV_PAD + SEQ, CONV_WIDTH), BF16),
                        pltpu.VMEM((8 * CONV_SHROWS, CONV_WIDTH), F32),
                        pltpu.VMEM((8 * CONV_SHROWS, CONV_WIDTH), F32)],
        compiler_params=pltpu.CompilerParams(dimension_semantics=("arbitrary",), vmem_limit_bytes=VMEM_LIMIT),
        name="conv",
    )(a3, a_meta, shift, dw_w, dw_b, ln_g, ln_b)


def _split3(x):
    hi = x.astype(BF16)
    r1 = x - hi.astype(F32)
    mid = r1.astype(BF16)
    lo = (r1 - mid.astype(F32)).astype(BF16)
    return jnp.concatenate([hi, mid, lo], axis=0)


def _attn_kernel(q_ref, k_ref, v_ref, lf_ref, km_ref, vm_ref, lfm_ref, o_ref,
                 c_scr, cm_scr, m_scr, acc_scr, sa_scr, sb_scr):
    p = pl.program_id(1)
    qi = pl.program_id(2)
    nk = SEQ // ATT_TK

    @pl.when((p == 0) & (qi == 0))
    def _():
        r = lax.broadcasted_iota(jnp.int32, (ATT_TK, ATT_TK), 0)
        c = lax.broadcasted_iota(jnp.int32, (ATT_TK, ATT_TK), 1)
        tri = (r <= c).astype(BF16)
        carry = jnp.zeros((N_HEADS, 1), F32)
        for j in range(nk):
            parts = _dot(_split3(lf_ref[0:N_HEADS, j * ATT_TK:(j + 1) * ATT_TK]), tri)
            cj = parts[0:8] + parts[8:16] + parts[16:24] + carry
            c_scr[j] = cj
            carry = cj[:, ATT_TK - 1:ATT_TK]
        rm = lax.broadcasted_iota(jnp.int32, (LANES, LANES), 0)
        cmx = lax.broadcasted_iota(jnp.int32, (LANES, LANES), 1)
        trim = ((rm > cmx) & (rm < N_META)).astype(BF16)
        pm = _dot(_split3(lfm_ref[0:N_HEADS, :]), trim)
        cm_scr[...] = -(pm[0:8] + pm[8:16] + pm[16:24])

    lane = lax.broadcasted_iota(jnp.int32, (ATT_TQ, LANES), 1)
    q_all = q_ref[...]
    zq = jnp.zeros_like(q_all)
    qm = (jnp.where(lane < HEAD_DIM, q_all, zq), jnp.where(lane >= HEAD_DIM, q_all, zq))
    for hl in range(2):
        m_scr[hl] = jnp.full((ATT_TQ, LANES), NEG_INF, F32)
        acc_scr[hl] = jnp.zeros((ATT_TQ, LANES), F32)

    def scores(kb, crows):
        return tuple(_dot_nt(qm[hl], kb) - crows[hl] for hl in range(2))

    def update(s2, vb, mask):
        vlane = lax.broadcasted_iota(jnp.int32, vb.shape, 1)
        ones = jnp.ones_like(vb)
        for hl in range(2):
            s = s2[hl]
            if mask is not None:
                s = jnp.where(mask, s, NEG_INF)
            m_prev = m_scr[hl]
            m_next = jnp.maximum(m_prev, jnp.max(s, axis=1, keepdims=True))
            reps = s.shape[1] // LANES
            m_wide = m_next if reps == 1 else jnp.concatenate([m_next] * reps, axis=1)
            pr = jnp.exp(s - m_wide).astype(BF16)
            alpha = jnp.exp(m_prev - m_next)
            own = (vlane < HEAD_DIM) if hl == 0 else (vlane >= HEAD_DIM)
            acc_scr[hl] = acc_scr[hl] * alpha + _dot(pr, jnp.where(own, vb, ones))
            m_scr[hl] = m_next

    def crows_at(j):
        return (c_scr[j, pl.ds(2 * p, 1), :], c_scr[j, pl.ds(2 * p + 1, 1), :])

    def kblock(j):
        return k_ref[pl.ds(pl.multiple_of(j * ATT_TK, ATT_TK), ATT_TK), :]

    def vblock(j):
        return v_ref[pl.ds(pl.multiple_of(j * ATT_TK, ATT_TK), ATT_TK), :]

    update(scores(km_ref[...], (cm_scr[pl.ds(2 * p, 1), :], cm_scr[pl.ds(2 * p + 1, 1), :])), vm_ref[...],
           lane < N_META)

    nfull = qi * (ATT_TQ // ATT_TK)

    def put(scr, s2):
        scr[0] = s2[0]
        scr[1] = s2[1]

    put(sa_scr, scores(kblock(0), crows_at(0)))

    def full2(i, carry):
        j = 2 * i
        put(sb_scr, scores(kblock(j + 1), crows_at(j + 1)))
        update((sa_scr[0], sa_scr[1]), vblock(j), None)
        put(sa_scr, scores(kblock(j + 2), crows_at(j + 2)))
        update((sb_scr[0], sb_scr[1]), vblock(j + 1), None)
        return carry

    lax.fori_loop(0, nfull // 2, full2, 0)

    rows = lax.broadcasted_iota(jnp.int32, (ATT_TQ, ATT_TK), 0)
    cols = lax.broadcasted_iota(jnp.int32, (ATT_TQ, ATT_TK), 1)
    s_diag = [(sa_scr[0], sa_scr[1])]
    for dj in range(1, ATT_TQ // ATT_TK):
        s_diag.append(scores(kblock(nfull + dj), crows_at(nfull + dj)))
    for dj in range(ATT_TQ // ATT_TK):
        update(s_diag[dj], vblock(nfull + dj), cols + dj * ATT_TK <= rows)

    a0 = acc_scr[0]
    a1 = acc_scr[1]
    o0 = a0 / pltpu.roll(a0, HEAD_DIM, axis=1)
    o1 = a1 / pltpu.roll(a1, HEAD_DIM, axis=1)
    o_ref[...] = jnp.where(lane < HEAD_DIM, o0, o1).astype(BF16)


def _attention(q, k, v, lf, km, vm, lfm):
    nq = SEQ // ATT_TQ
    return pl.pallas_call(
        _attn_kernel,
        grid=(BATCH, N_PAIRS, nq),
        in_specs=[
            pl.BlockSpec((ATT_TQ, LANES), lambda b, p, i: (b * nq + i, p)),
            pl.BlockSpec((SEQ, LANES), lambda b, p, i: (b, p)),
            pl.BlockSpec((SEQ, LANES), lambda b, p, i: (b, p)),
            pl.BlockSpec((16, SEQ), lambda b, p, i: (0, b)),
            pl.BlockSpec((META_ROWS, LANES), lambda b, p, i: (0, p)),
            pl.BlockSpec((META_ROWS, LANES), lambda b, p, i: (0, p)),
            pl.BlockSpec((16, META_ROWS), lambda b, p, i: (0, 0)),
        ],
        out_specs=pl.BlockSpec((ATT_TQ, LANES), lambda b, p, i: (b * nq + i, p)),
        out_shape=jax.ShapeDtypeStruct((N_TOK, ATTN_WIDTH), BF16),
        scratch_shapes=[
            pltpu.VMEM((SEQ // ATT_TK, N_HEADS, ATT_TK), F32),
            pltpu.VMEM((N_HEADS, LANES), F32),
            pltpu.VMEM((2, ATT_TQ, LANES), F32),
            pltpu.VMEM((2, ATT_TQ, LANES), F32),
            pltpu.VMEM((2, ATT_TQ, ATT_TK), F32),
            pltpu.VMEM((2, ATT_TQ, ATT_TK), F32),
        ],
        compiler_params=pltpu.CompilerParams(
            dimension_semantics=("arbitrary", "arbitrary", "arbitrary"), vmem_limit_bytes=VMEM_LIMIT),
        name="attn",
    )(q, k, v, lf, km, vm, lfm)


AUG_TERMS = 3


def _split_terms(x):
    hi = x.astype(BF16)
    r1 = x - hi.astype(F32)
    mid = r1.astype(BF16)
    lo = (r1 - mid.astype(F32)).astype(BF16)
    return hi, mid, lo


def _attn_t_kernel(q_ref, k_ref, vt_ref, lf_ref, km_ref, vm_ref, lfm_ref, place_ref, o_ref,
                   aug_scr, augm_scr, m_scr, acc_scr, sa_scr, sb_scr):
    p = pl.program_id(1)
    qi = pl.program_id(2)
    nk = SEQ // ATT_TK

    def bias_lanes(colsum):
        out = None
        for t, term in enumerate(_split_terms(colsum)):
            d = _dot(term, place_ref[t])
            out = d if out is None else out + d
        return out.astype(BF16)

    @pl.when((p == 0) & (qi == 0))
    def _():
        r = lax.broadcasted_iota(jnp.int32, (ATT_TK, ATT_TK), 0)
        c = lax.broadcasted_iota(jnp.int32, (ATT_TK, ATT_TK), 1)
        tri = (c <= r).astype(BF16)
        carry = jnp.zeros((1, 16), F32)
        for j in range(nk):
            lf = lf_ref[:, j * ATT_TK:(j + 1) * ATT_TK]
            csum = carry
            for term in _split_terms(lf):
                csum = csum + _dot_nt(tri, term)
            aug_scr[j * ATT_TK:(j + 1) * ATT_TK, :] = bias_lanes(csum * (-LOG2E))
            carry = csum[ATT_TK - 1:ATT_TK, :]
        rm = lax.broadcasted_iota(jnp.int32, (META_ROWS, META_ROWS), 0)
        cm = lax.broadcasted_iota(jnp.int32, (META_ROWS, META_ROWS), 1)
        trim = ((cm > rm) & (cm < N_META)).astype(BF16)
        msum = jnp.zeros((META_ROWS, 16), F32)
        for term in _split_terms(lfm_ref[...]):
            msum = msum + _dot_nt(trim, term)
        augm_scr[...] = bias_lanes(msum * LOG2E)

    lane = lax.broadcasted_iota(jnp.int32, (ATT_TQ, LANES), 1)
    q_all = q_ref[...]
    zq = jnp.zeros_like(q_all)
    q_cat = []
    for hl in range(2):
        own = (lane < HEAD_DIM) if hl == 0 else (lane >= HEAD_DIM)
        h = 2 * p + hl
        bias_sel = (lane >= AUG_TERMS * h) & (lane < AUG_TERMS * h + AUG_TERMS)
        bias_ones = jnp.where(bias_sel, 1.0, 0.0).astype(BF16)
        q_cat.append(jnp.concatenate([jnp.where(own, q_all, zq), bias_ones], axis=1))
        m_scr[hl] = jnp.full((1, ATT_TQ), NEG_INF, F32)
        acc_scr[hl] = jnp.zeros((LANES, ATT_TQ), F32)

    def scores(kb, augb):
        k_cat = jnp.concatenate([kb, augb], axis=1)
        return tuple(_dot_nt(k_cat, q_cat[hl]) for hl in range(2))

    def update(s2, vtb, mask):
        vrow = lax.broadcasted_iota(jnp.int32, vtb.shape, 0)
        ones = jnp.ones_like(vtb)
        for hl in range(2):
            s = s2[hl]
            if mask is not None:
                s = jnp.where(mask, s, NEG_INF)
            m_prev = m_scr[hl]
            m_next = jnp.maximum(m_prev, jnp.max(s, axis=0, keepdims=True))
            pr = jnp.exp2(s - m_next).astype(BF16)
            alpha = jnp.exp2(m_prev - m_next)
            own = (vrow < HEAD_DIM) if hl == 0 else (vrow >= HEAD_DIM)
            acc_scr[hl] = acc_scr[hl] * alpha + _dot(jnp.where(own, vtb, ones), pr)
            m_scr[hl] = m_next

    def kblock(j):
        return k_ref[pl.ds(pl.multiple_of(j * ATT_TK, ATT_TK), ATT_TK), :]

    def augblock(j):
        return aug_scr[pl.ds(pl.multiple_of(j * ATT_TK, ATT_TK), ATT_TK), :]

    mrow = lax.broadcasted_iota(jnp.int32, (META_ROWS, ATT_TQ), 0)
    update(scores(km_ref[...], augm_scr[...]), vm_ref[0], mrow < N_META)

    nfull = qi * (ATT_TQ // ATT_TK)

    def put(scr, s2):
        scr[0] = s2[0]
        scr[1] = s2[1]

    put(sa_scr, scores(kblock(0), augblock(0)))

    def full2(i, carry):
        j = 2 * i
        put(sb_scr, scores(kblock(j + 1), augblock(j + 1)))
        update((sa_scr[0], sa_scr[1]), vt_ref[j], None)
        put(sa_scr, scores(kblock(j + 2), augblock(j + 2)))
        update((sb_scr[0], sb_scr[1]), vt_ref[j + 1], None)
        return carry

    lax.fori_loop(0, nfull // 2, full2, 0)

    srow = lax.broadcasted_iota(jnp.int32, (ATT_TK, ATT_TQ), 0)
    tcol = lax.broadcasted_iota(jnp.int32, (ATT_TK, ATT_TQ), 1)
    s_diag = [(sa_scr[0], sa_scr[1])]
    for dj in range(1, ATT_TQ // ATT_TK):
        s_diag.append(scores(kblock(nfull + dj), augblock(nfull + dj)))
    for dj in range(ATT_TQ // ATT_TK):
        update(s_diag[dj], vt_ref[nfull + dj], srow + dj * ATT_TK <= tcol)

    a0 = acc_scr[0]
    a1 = acc_scr[1]
    ot = jnp.concatenate([a0[0:HEAD_DIM] / a0[HEAD_DIM:], a1[HEAD_DIM:] / a1[0:HEAD_DIM]], axis=0)
    o_ref[...] = ot.T.astype(BF16)


def _attention_t(q, k, vt, lf, km, vm, lfm):
    nq = SEQ // ATT_TQ
    nk = SEQ // ATT_TK
    hh = jnp.arange(16)[:, None]
    ll = jnp.arange(LANES)[None, :]
    place = jnp.stack([((ll == AUG_TERMS * hh + t) & (hh < N_HEADS)).astype(BF16) for t in range(AUG_TERMS)])
    return pl.pallas_call(
        _attn_t_kernel,
        grid=(BATCH, N_PAIRS, nq),
        in_specs=[
            pl.BlockSpec((ATT_TQ, LANES), lambda b, p, i: (b * nq + i, p)),
            pl.BlockSpec((SEQ, LANES), lambda b, p, i: (b, p)),
            pl.BlockSpec((nk, LANES, ATT_TK), lambda b, p, i: (b, p, 0)),
            pl.BlockSpec((16, SEQ), lambda b, p, i: (0, b)),
            pl.BlockSpec((META_ROWS, LANES), lambda b, p, i: (0, p)),
            pl.BlockSpec((1, LANES, META_ROWS), lambda b, p, i: (0, p, 0)),
            pl.BlockSpec((16, META_ROWS), lambda b, p, i: (0, 0)),
            _const_spec(place.shape),
        ],
        out_specs=pl.BlockSpec((ATT_TQ, LANES), lambda b, p, i: (b * nq + i, p)),
        out_shape=jax.ShapeDtypeStruct((N_TOK, ATTN_WIDTH), BF16),
        scratch_shapes=[
            pltpu.VMEM((SEQ, LANES), BF16),
            pltpu.VMEM((META_ROWS, LANES), BF16),
            pltpu.VMEM((2, 1, ATT_TQ), F32),
            pltpu.VMEM((2, LANES, ATT_TQ), F32),
            pltpu.VMEM((2, ATT_TK, ATT_TQ), F32),
            pltpu.VMEM((2, ATT_TK, ATT_TQ), F32),
        ],
        compiler_params=pltpu.CompilerParams(
            dimension_semantics=("arbitrary", "arbitrary", "arbitrary"), vmem_limit_bytes=VMEM_LIMIT),
        name="attn",
    )(q, k, vt, lf, km, vm, lfm, place)


MERGE_TM = 512


def _merge_kernel(act_ref, o_ref, sgc_ref, sga_ref, x_ref, wpw_ref, bpw_ref, wao_ref, wout_ref, g2_ref,
                  wrt_ref, brt_ref, tri_ref,
                  h1_ref, hn_ref, route_ref, routet_ref, cnt_ref, base_scr):
    i = pl.program_id(0)

    @pl.when(i == 0)
    def _():
        base_scr[...] = jnp.zeros_like(base_scr)

    y_conv = _dot(act_ref[...], wpw_ref[...]) + bpw_ref[...]
    y_attn = _dot(o_ref[...], wao_ref[...])
    merged = sgc_ref[...].astype(F32) * y_conv + sga_ref[...].astype(F32) * y_attn
    h1 = x_ref[...] + _dot(merged.astype(BF16), wout_ref[...])
    h1_ref[...] = h1
    ms = jnp.mean(h1 * h1, axis=-1, keepdims=True)
    hn = (h1 * lax.rsqrt(ms + RMS_EPS) * g2_ref[...]).astype(BF16)
    hn_ref[...] = hn.reshape(hn.shape[0], TOK_SUB, LANES)

    logits = _dot(hn, wrt_ref[...]) + brt_ref[...]
    lane = lax.broadcasted_iota(jnp.int32, logits.shape, 1).astype(F32)
    big = float(LANES)
    is_g = lane < N_GROUPS
    gl = jnp.where(is_g, logits, NEG_INF)
    gmax = jnp.max(gl, axis=1, keepdims=True)
    gidx = jnp.min(jnp.where(gl == gmax, lane, big), axis=1, keepdims=True)
    gsum = jnp.sum(jnp.where(is_g, jnp.exp(gl - gmax), 0.0), axis=1, keepdims=True)
    g_w = 1.0 / gsum
    lo = N_GROUPS + EXPERTS_PER_GROUP * gidx
    el = jnp.where((lane >= lo) & (lane < lo + EXPERTS_PER_GROUP), logits, NEG_INF)
    t1 = jnp.max(el, axis=1, keepdims=True)
    i1 = jnp.min(jnp.where(el == t1, lane, big), axis=1, keepdims=True)
    el2 = jnp.where(lane == i1, NEG_INF, el)
    t2 = jnp.max(el2, axis=1, keepdims=True)
    i2 = jnp.min(jnp.where(el2 == t2, lane, big), axis=1, keepdims=True)
    e2 = jnp.exp(t2 - t1)
    den = 1.0 + e2
    w1 = (1.0 / den) * g_w
    w2 = (e2 / den) * g_w
    ea = i1 - N_GROUPS
    eb = i2 - N_GROUPS

    oha = (lane == ea).astype(F32)
    ohb = (lane == eb).astype(F32)
    ohs = oha + ohb
    before = _dot(tri_ref[...], ohs.astype(BF16)) + base_scr[0:1, :]
    ra = jnp.sum(before * oha, axis=1, keepdims=True)
    rb = jnp.sum(before * ohb, axis=1, keepdims=True)
    base_scr[...] = base_scr[...] + jnp.sum(ohs, axis=0, keepdims=True)
    cnt_ref[...] = base_scr[...]

    route = jnp.where(lane == 0, ea, 0.0)
    route = jnp.where(lane == 1, eb, route)
    route = jnp.where(lane == 2, w1, route)
    route = jnp.where(lane == 3, w2, route)
    route = jnp.where(lane == 4, ra, route)
    route = jnp.where(lane == 5, rb, route)
    route_ref[...] = route
    routet_ref[...] = route.T[0:8]


def _merge(act, o, sgc, sga, x2d, consts):
    tm = MERGE_TM
    row = lambda w: pl.BlockSpec((tm, w), lambda i: (i, 0))
    return pl.pallas_call(
        _merge_kernel,
        grid=(N_TOK // tm,),
        in_specs=[row(CONV_WIDTH), row(ATTN_WIDTH), row(D_MODEL), row(D_MODEL), row(D_MODEL)]
        + [_const_spec(c.shape) for c in consts],
        out_specs=[row(D_MODEL), pl.BlockSpec((tm, TOK_SUB, LANES), lambda i: (i, 0, 0)), row(LANES),
                   pl.BlockSpec((8, tm), lambda i: (0, i)), _const_spec((8, LANES))],
        out_shape=[jax.ShapeDtypeStruct((N_TOK, D_MODEL), F32),
                   jax.ShapeDtypeStruct((N_TOK, TOK_SUB, LANES), BF16),
                   jax.ShapeDtypeStruct((N_TOK, LANES), F32),
                   jax.ShapeDtypeStruct((8, N_TOK), F32),
                   jax.ShapeDtypeStruct((8, LANES), F32)],
        scratch_shapes=[pltpu.VMEM((8, LANES), F32)],
        compiler_params=pltpu.CompilerParams(dimension_semantics=("arbitrary",), vmem_limit_bytes=VMEM_LIMIT),
        name="merge",
    )(act, o, sgc, sga, x2d, *consts)


DISP_TM = 1024
DMA_UNROLL = 8


def _dest_row(code_ref, pstart_ref, idx):
    c = code_ref[idx]
    return pstart_ref[lax.shift_right_logical(c, RANK_BITS)] + (c & ((1 << RANK_BITS) - 1))


def _dispatch_kernel(c0_ref, c1_ref, pstart_ref, pend_ref, hn_ref, xin_ref, zero_scr, sem, zsem):
    i = pl.program_id(0)
    base = i * DISP_TM

    @pl.when(i == 0)
    def _():
        zero_scr[...] = jnp.zeros_like(zero_scr)

        def zcopy(e):
            start = pl.multiple_of(pend_ref[e] - ROW_BLOCK, ROW_BLOCK)
            return pltpu.make_async_copy(zero_scr, xin_ref.at[pl.ds(start, ROW_BLOCK)], zsem.at[0])

        def zstart(e, carry):
            @pl.when(pend_ref[e] > pstart_ref[e])
            def _():
                zcopy(e).start()
            return carry

        def zwait(e, carry):
            @pl.when(pend_ref[e] > pstart_ref[e])
            def _():
                zcopy(e).wait()
            return carry

        n_used = pend_ref[N_EXPERTS - 1] // ROW_BLOCK

        def tcopy(b):
            return pltpu.make_async_copy(
                zero_scr, xin_ref.at[pl.ds(pl.multiple_of(b * ROW_BLOCK, ROW_BLOCK), ROW_BLOCK)], zsem.at[0])

        def tstart(b, carry):
            tcopy(b).start()
            return carry

        def twait(b, carry):
            tcopy(b).wait()
            return carry

        lax.fori_loop(0, N_EXPERTS, zstart, 0)
        lax.fori_loop(n_used, N_BLOCKS, tstart, 0)
        lax.fori_loop(0, N_EXPERTS, zwait, 0)
        lax.fori_loop(n_used, N_BLOCKS, twait, 0)

    def start(r, carry):
        pltpu.make_async_copy(hn_ref.at[r], xin_ref.at[_dest_row(c0_ref, pstart_ref, base + r)], sem.at[0]).start()
        pltpu.make_async_copy(hn_ref.at[r], xin_ref.at[_dest_row(c1_ref, pstart_ref, base + r)], sem.at[1]).start()
        return carry

    lax.fori_loop(0, DISP_TM, start, 0, unroll=DMA_UNROLL)
    pltpu.make_async_copy(hn_ref, xin_ref.at[pl.ds(0, DISP_TM)], sem.at[0]).wait()
    pltpu.make_async_copy(hn_ref, xin_ref.at[pl.ds(0, DISP_TM)], sem.at[1]).wait()


def _dispatch(code0, code1, pstart, pend, hn):
    grid_spec = pltpu.PrefetchScalarGridSpec(
        num_scalar_prefetch=4,
        grid=(N_TOK // DISP_TM,),
        in_specs=[pl.BlockSpec((DISP_TM, TOK_SUB, LANES), lambda i, *_: (i, 0, 0))],
        out_specs=pl.BlockSpec(memory_space=pl.ANY),
        scratch_shapes=[pltpu.VMEM((ROW_BLOCK, TOK_SUB, LANES), BF16),
                        pltpu.SemaphoreType.DMA((2,)), pltpu.SemaphoreType.DMA((1,))],
    )
    return pl.pallas_call(
        _dispatch_kernel,
        grid_spec=grid_spec,
        out_shape=jax.ShapeDtypeStruct((P_ROWS, TOK_SUB, LANES), BF16),
        compiler_params=pltpu.CompilerParams(dimension_semantics=("arbitrary",), vmem_limit_bytes=VMEM_LIMIT),
        name="dispatch",
    )(code0, code1, pstart, pend, hn)


def _expert_kernel(be_ref, nu_ref, x_ref, wg_ref, wu_ref, wd_ref, y_ref):
    i = pl.program_id(0)

    @pl.when(i < nu_ref[0])
    def _():
        x = x_ref[...].reshape(ROW_BLOCK, D_MODEL)
        g = _dot(x, wg_ref[0].astype(BF16))
        u = _dot(x, wu_ref[0].astype(BF16))
        hmid = (g * _sigmoid(g) * u).astype(BF16)
        y = _dot(hmid, wd_ref[0].astype(BF16)).astype(BF16)
        y_ref[...] = y.reshape(ROW_BLOCK, TOK_SUB, LANES)

    @pl.when(i >= nu_ref[0])
    def _():
        y_ref[...] = jnp.zeros_like(y_ref)


def _experts(blk_e, n_used, xin, w_gate, w_up, w_down):
    def xmap(i, be, nu):
        return (jnp.minimum(i, nu[0] - 1), 0, 0)

    def wmap(i, be, nu):
        return (be[i], 0, 0)

    grid_spec = pltpu.PrefetchScalarGridSpec(
        num_scalar_prefetch=2,
        grid=(N_BLOCKS,),
        in_specs=[pl.BlockSpec((ROW_BLOCK, TOK_SUB, LANES), xmap),
                  pl.BlockSpec((1, D_MODEL, D_EXPERT), wmap),
                  pl.BlockSpec((1, D_MODEL, D_EXPERT), wmap),
                  pl.BlockSpec((1, D_EXPERT, D_MODEL), wmap)],
        out_specs=pl.BlockSpec((ROW_BLOCK, TOK_SUB, LANES), lambda i, be, nu: (i, 0, 0)),
    )
    return pl.pallas_call(
        _expert_kernel,
        grid_spec=grid_spec,
        out_shape=jax.ShapeDtypeStruct((P_ROWS, TOK_SUB, LANES), BF16),
        compiler_params=pltpu.CompilerParams(dimension_semantics=("arbitrary",), vmem_limit_bytes=VMEM_LIMIT),
        name="experts",
    )(blk_e, n_used, xin, w_gate, w_up, w_down)


COMB_TM = 256


def _combine_kernel(c0_ref, c1_ref, pstart_ref, h1_ref, route_ref, yb_ref, out_ref, g_scr, sem):
    i = pl.program_id(0)
    n = pl.num_programs(0)

    def issue(tile, slot):
        base = tile * COMB_TM

        def start(r, carry):
            pltpu.make_async_copy(yb_ref.at[_dest_row(c0_ref, pstart_ref, base + r)], g_scr.at[slot, 0, r],
                                  sem.at[slot, 0]).start()
            pltpu.make_async_copy(yb_ref.at[_dest_row(c1_ref, pstart_ref, base + r)], g_scr.at[slot, 1, r],
                                  sem.at[slot, 1]).start()
            return carry

        lax.fori_loop(0, COMB_TM, start, 0, unroll=DMA_UNROLL)

    @pl.when(i == 0)
    def _():
        issue(0, 0)

    slot = i % 2

    @pl.when(i + 1 < n)
    def _():
        issue(i + 1, 1 - slot)

    for k in range(2):
        pltpu.make_async_copy(yb_ref.at[pl.ds(0, COMB_TM)], g_scr.at[slot, k], sem.at[slot, k]).wait()

    route = route_ref[...]
    w0 = route[:, 2:3]
    w1 = route[:, 3:4]
    g0 = g_scr[slot, 0].reshape(COMB_TM, D_MODEL).astype(F32)
    g1 = g_scr[slot, 1].reshape(COMB_TM, D_MODEL).astype(F32)
    out_ref[...] = h1_ref[...] + (g0 * w0 + g1 * w1)


def _combine(code0, code1, pstart, h1, route, yb):
    grid_spec = pltpu.PrefetchScalarGridSpec(
        num_scalar_prefetch=3,
        grid=(N_TOK // COMB_TM,),
        in_specs=[pl.BlockSpec((COMB_TM, D_MODEL), lambda i, *_: (i, 0)),
                  pl.BlockSpec((COMB_TM, LANES), lambda i, *_: (i, 0)),
                  pl.BlockSpec(memory_space=pl.ANY)],
        out_specs=pl.BlockSpec((COMB_TM, D_MODEL), lambda i, *_: (i, 0)),
        scratch_shapes=[pltpu.VMEM((2, 2, COMB_TM, TOK_SUB, LANES), BF16),
                        pltpu.SemaphoreType.DMA((2, 2))],
    )
    return pl.pallas_call(
        _combine_kernel,
        grid_spec=grid_spec,
        out_shape=jax.ShapeDtypeStruct((N_TOK, D_MODEL), F32),
        compiler_params=pltpu.CompilerParams(dimension_semantics=("arbitrary",), vmem_limit_bytes=VMEM_LIMIT),
        name="combine",
    )(code0, code1, pstart, h1, route, yb)


def kernel(x, meta, norm1_g, w_in, b_in, dw_w, dw_b, conv_ln_g, conv_ln_b, w_conv_out, b_conv_out,
           q_norm_g, k_norm_g, b_forget, w_attn_out, w_out, norm2_g, w_group, b_group, w_router,
           b_router, w_gate, w_up, w_down):
    l = 0
    row = lambda v: v.reshape(1, -1).astype(F32)

    c_u, c_q, c_f = 2 * CONV_WIDTH, 2 * CONV_WIDTH, 2 * CONV_WIDTH + 3 * ATTN_WIDTH
    c_gc = c_f + N_HEADS
    c_ga = c_gc + D_MODEL
    wi, bi = w_in[l], b_in[l]
    wu, bu = wi[:, :c_u].astype(BF16), row(bi[:c_u])
    wqkv, bqkv = wi[:, c_q:c_f].astype(BF16), row(bi[c_q:c_f])
    wft = jnp.zeros((16, D_MODEL), F32).at[:N_HEADS].set(wi[:, c_f:c_gc].T).astype(BF16)
    bft = jnp.zeros((16, 1), F32).at[:N_HEADS, 0].set(bi[c_f:c_gc] + b_forget[l])
    wgc, bgc = wi[:, c_gc:c_ga].astype(BF16), row(bi[c_gc:c_ga])
    wga, bga = wi[:, c_ga:].astype(BF16), row(bi[c_ga:])
    qg = row(jnp.tile(q_norm_g[l], N_HEADS) * (HEAD_DIM ** -0.5 * LOG2E))
    kg = row(jnp.tile(k_norm_g[l], N_HEADS))
    head_of_col = jnp.arange(ATTN_WIDTH) // HEAD_DIM
    hsum = (head_of_col[:, None] == jnp.arange(LANES)[None, :]).astype(BF16)
    hexp_half = (jnp.arange(LANES)[:, None] == head_of_col[None, :]).astype(BF16)
    hexp = jnp.concatenate([hexp_half, hexp_half], axis=0)
    proj_consts = (row(norm1_g[l]), wu, bu, wqkv, bqkv, wft, bft, wgc, bgc, wga, bga, qg, kg, hsum, hexp)

    x2d = x.reshape(N_TOK, D_MODEL)
    meta_pad = jnp.zeros((META_ROWS, D_MODEL), F32).at[:N_META].set(meta.astype(F32))
    a, q, k, v, lf, sgc, sga = _proj(x2d, proj_consts, 512)
    a_m, _, k_m, v_m, lf_m, _, _ = _proj(meta_pad, proj_consts, META_ROWS)

    act = _conv(a.reshape(BATCH, SEQ, CONV_WIDTH), a_m, dw_w[l].astype(F32), row(dw_b[l]),
                row(conv_ln_g[l]), row(conv_ln_b[l])).reshape(N_TOK, CONV_WIDTH)

    o = _attention_t(q, k, v, lf, k_m, v_m, lf_m)

    wrt = jnp.zeros((D_MODEL, LANES), F32).at[:, :N_GROUPS].set(w_group[l])
    wrt = wrt.at[:, N_GROUPS:N_GROUPS + N_EXPERTS].set(w_router[l]).astype(BF16)
    brt = jnp.zeros((1, LANES), F32).at[0, :N_GROUPS].set(b_group[l])
    brt = brt.at[0, N_GROUPS:N_GROUPS + N_EXPERTS].set(b_router[l])
    tri = jnp.tril(jnp.ones((MERGE_TM, MERGE_TM), F32), -1).astype(BF16)
    merge_consts = (w_conv_out[l].astype(BF16), row(b_conv_out[l]), w_attn_out[l].astype(BF16),
                    w_out[l].astype(BF16), row(norm2_g[l]), wrt, brt, tri)
    h1, hn, route, route_t, cnt = _merge(act, o, sgc, sga, x2d, merge_consts)

    rt = route_t.astype(jnp.int32)
    code0 = rt[0] * (1 << RANK_BITS) + rt[4]
    code1 = rt[1] * (1 << RANK_BITS) + rt[5]
    counts = cnt[0, :N_EXPERTS].astype(jnp.int32)
    padded = (counts + ROW_BLOCK - 1) // ROW_BLOCK * ROW_BLOCK
    pend = jnp.cumsum(padded).astype(jnp.int32)
    pstart = pend - padded
    blk_row = jnp.arange(N_BLOCKS, dtype=jnp.int32) * ROW_BLOCK
    blk_e = jnp.minimum(jnp.sum((pend[None, :] <= blk_row[:, None]).astype(jnp.int32), axis=1), N_EXPERTS - 1)
    n_used = pend[-1:] // ROW_BLOCK

    xin = _dispatch(code0, code1, pstart, pend, hn)
    yb = _experts(blk_e, n_used, xin, w_gate[l], w_up[l], w_down[l])
    out = _combine(code0, code1, pstart, h1, route, yb)
    return out.reshape(BATCH, SEQ, D_MODEL)
```

```python
import functools

import jax
import jax.numpy as jnp
from jax import lax
from jax.experimental import pallas as pl
from jax.experimental.pallas import tpu as pltpu

D_MODEL = 1024
BATCH = 8
SEQ = 2048
N_META = 16
CONV_WIDTH = 512
CONV_K = 31
N_HEADS = 8
HEAD_DIM = 64
ATTN_WIDTH = N_HEADS * HEAD_DIM
N_GROUPS = 8
EXPERTS_PER_GROUP = 8
N_EXPERTS = N_GROUPS * EXPERTS_PER_GROUP
D_EXPERT = 256
RMS_EPS = 1e-6
LN_EPS = 1e-5

N_TOK = BATCH * SEQ
LANES = 128
META_ROWS = 128
ROW_BLOCK = 512
N_ASSIGN = 2 * N_TOK
N_BLOCKS = N_ASSIGN // ROW_BLOCK + N_EXPERTS
P_ROWS = N_BLOCKS * ROW_BLOCK
TOK_SUB = D_MODEL // LANES
RANK_BITS = 16
ATT_TQ = 512
ATT_TK = 256
N_PAIRS = N_HEADS // 2
assert (ATT_TQ // ATT_TK) % 2 == 0
VMEM_LIMIT = 56 * 1024 * 1024

F32 = jnp.float32
BF16 = jnp.bfloat16
NEG_INF = float("-inf")
LOG2E = 1.4426950408889634


def _dot(a, b):
    return jnp.dot(a, b, preferred_element_type=F32)


def _dot_nt(a, b):
    return lax.dot_general(a, b, (((1,), (1,)), ((), ())), preferred_element_type=F32)


def _sigmoid(x):
    return 1.0 / (1.0 + jnp.exp(-x))


def _const_spec(shape):
    nd = len(shape)
    return pl.BlockSpec(shape, lambda *_: (0,) * nd)


def _head_rms(t, hsum_ref, hexp_ref, gain):
    ss = _dot((t * t).astype(BF16), hsum_ref[...])
    inv = lax.rsqrt(ss * (1.0 / HEAD_DIM) + RMS_EPS)
    hi = inv.astype(BF16)
    lo = (inv - hi.astype(F32)).astype(BF16)
    invb = _dot(jnp.concatenate([hi, lo], axis=1), hexp_ref[...])
    return t * invb * gain


def _proj_kernel(x_ref, g1_ref, wu_ref, bu_ref, wqkv_ref, bqkv_ref, wft_ref, bft_ref, wgc_ref, bgc_ref,
                 wga_ref, bga_ref, qg_ref, kg_ref, hsum_ref, hexp_ref,
                 a_ref, q_ref, k_ref, v_ref, lf_ref, sgc_ref, sga_ref):
    x = x_ref[...]
    ms = jnp.mean(x * x, axis=-1, keepdims=True)
    xn = (x * lax.rsqrt(ms + RMS_EPS) * g1_ref[...]).astype(BF16)

    u = _dot(xn, wu_ref[...]) + bu_ref[...]
    a_ref[...] = (u[:, :CONV_WIDTH] * _sigmoid(u[:, CONV_WIDTH:])).astype(BF16)

    qkv = _dot(xn, wqkv_ref[...]) + bqkv_ref[...]
    q_ref[...] = _head_rms(qkv[:, :ATTN_WIDTH], hsum_ref, hexp_ref, qg_ref[...]).astype(BF16)
    k_ref[...] = _head_rms(qkv[:, ATTN_WIDTH:2 * ATTN_WIDTH], hsum_ref, hexp_ref, kg_ref[...]).astype(BF16)
    v = qkv[:, 2 * ATTN_WIDTH:]
    vb = v_ref.shape[2]
    for c in range(v_ref.shape[0]):
        v_ref[c] = v[c * vb:(c + 1) * vb, :].T.astype(BF16)

    f = _dot_nt(wft_ref[...], xn) + bft_ref[...]
    lf_ref[...] = jnp.minimum(f, 0.0) - jnp.log(1.0 + jnp.exp(-jnp.abs(f)))

    sgc_ref[...] = _sigmoid(_dot(xn, wgc_ref[...]) + bgc_ref[...]).astype(BF16)
    sga_ref[...] = _sigmoid(_dot(xn, wga_ref[...]) + bga_ref[...]).astype(BF16)


def _proj(x2d, consts, tm):
    n = x2d.shape[0]
    row = lambda w: pl.BlockSpec((tm, w), lambda i: (i, 0))
    in_specs = [row(D_MODEL)] + [_const_spec(c.shape) for c in consts]
    vb = min(tm, ATT_TK)
    out_shape = [
        jax.ShapeDtypeStruct((n, CONV_WIDTH), BF16),
        jax.ShapeDtypeStruct((n, ATTN_WIDTH), BF16),
        jax.ShapeDtypeStruct((n, ATTN_WIDTH), BF16),
        jax.ShapeDtypeStruct((n // vb, ATTN_WIDTH, vb), BF16),
        jax.ShapeDtypeStruct((16, n), F32),
        jax.ShapeDtypeStruct((n, D_MODEL), BF16),
        jax.ShapeDtypeStruct((n, D_MODEL), BF16),
    ]
    out_specs = [row(CONV_WIDTH), row(ATTN_WIDTH), row(ATTN_WIDTH),
                 pl.BlockSpec((tm // vb, ATTN_WIDTH, vb), lambda i: (i, 0, 0)),
                 pl.BlockSpec((16, tm), lambda i: (0, i)), row(D_MODEL), row(D_MODEL)]
    return pl.pallas_call(
        _proj_kernel,
        grid=(n // tm,),
        in_specs=in_specs,
        out_specs=out_specs,
        out_shape=out_shape,
        compiler_params=pltpu.CompilerParams(dimension_semantics=("arbitrary",), vmem_limit_bytes=VMEM_LIMIT),
        name="proj",
    )(x2d, *consts)


CONV_CHUNK = 64
CONV_PAD = 32
CONV_WIN = CONV_CHUNK + 32
CONV_SHROWS = CONV_WIN


def _conv_kernel(a_ref, am_ref, shift_ref, dw_ref, dwb_ref, lng_ref, lnb_ref, o_ref, pad_scr, sha_scr, shb_scr):
    pad_scr[0:N_META, :] = jnp.zeros((N_META, CONV_WIDTH), BF16)
    pad_scr[N_META:CONV_PAD, :] = am_ref[0:N_META, :]
    pad_scr[CONV_PAD:CONV_PAD + SEQ, :] = a_ref[0]

    def shifted_copies(ci):
        r0 = pl.multiple_of(ci * CONV_CHUNK, CONV_CHUNK)
        return _dot(shift_ref[...], pad_scr[pl.ds(r0, CONV_WIN), :])

    n_chunks = SEQ // CONV_CHUNK
    sha_scr[...] = shifted_copies(0)

    def finish(ci, cur_scr):
        r0 = pl.multiple_of(ci * CONV_CHUNK, CONV_CHUNK)
        acc = jnp.zeros((CONV_CHUNK, CONV_WIDTH), F32) + dwb_ref[...]
        for j in range(CONV_K):
            off = j + 2
            b = off % 8
            a0 = b * CONV_SHROWS + (off - b)
            acc = acc + cur_scr[a0:a0 + CONV_CHUNK, :] * dw_ref[j:j + 1, :]
        mu = jnp.mean(acc, axis=-1, keepdims=True)
        d = acc - mu
        var = jnp.mean(d * d, axis=-1, keepdims=True)
        y = d * lax.rsqrt(var + LN_EPS) * lng_ref[...] + lnb_ref[...]
        o_ref[0, pl.ds(r0, CONV_CHUNK), :] = (y * _sigmoid(y)).astype(BF16)

    def chunk_pair(i, carry):
        ci = 2 * i
        shb_scr[...] = shifted_copies(ci + 1)
        finish(ci, sha_scr)
        sha_scr[...] = shifted_copies(jnp.minimum(ci + 2, n_chunks - 1))
        finish(ci + 1, shb_scr)
        return carry

    lax.fori_loop(0, n_chunks // 2, chunk_pair, 0)


def _conv(a3, a_meta, dw_w, dw_b, ln_g, ln_b):
    rr = jnp.arange(8 * CONV_SHROWS)
    shift = ((rr % CONV_SHROWS + rr // CONV_SHROWS)[:, None] == jnp.arange(CONV_WIN)[None, :]).astype(BF16)
    return pl.pallas_call(
        _conv_kernel,
        grid=(BATCH,),
        in_specs=[pl.BlockSpec((1, SEQ, CONV_WIDTH), lambda b: (b, 0, 0)),
                  _const_spec(a_meta.shape), _const_spec(shift.shape), _const_spec(dw_w.shape),
                  _const_spec(dw_b.shape), _const_spec(ln_g.shape), _const_spec(ln_b.shape)],
        out_specs=pl.BlockSpec((1, SEQ, CONV_WIDTH), lambda b: (b, 0, 0)),
        out_shape=jax.ShapeDtypeStruct((BATCH, SEQ, CONV_WIDTH), BF16),
        scratch_shapes=[pltpu.VMEM((CONV_PAD + SEQ, CONV_WIDTH), BF16),
                        pltpu.VMEM((8 * CONV_SHROWS, CONV_WIDTH), F32),
                        pltpu.VMEM((8 * CONV_SHROWS, CONV_WIDTH), F32)],
        compiler_params=pltpu.CompilerParams(dimension_semantics=("arbitrary",), vmem_limit_bytes=VMEM_LIMIT),
        name="conv",
    )(a3, a_meta, shift, dw_w, dw_b, ln_g, ln_b)


def _split3(x):
    hi = x.astype(BF16)
    r1 = x - hi.astype(F32)
    mid = r1.astype(BF16)
    lo = (r1 - mid.astype(F32)).astype(BF16)
    return jnp.concatenate([hi, mid, lo], axis=0)


def _attn_kernel(q_ref, k_ref, v_ref, lf_ref, km_ref, vm_ref, lfm_ref, o_ref,
                 c_scr, cm_scr, m_scr, acc_scr, sa_scr, sb_scr):
    p = pl.program_id(1)
    qi = pl.program_id(2)
    nk = SEQ // ATT_TK

    @pl.when((p == 0) & (qi == 0))
    def _():
        r = lax.broadcasted_iota(jnp.int32, (ATT_TK, ATT_TK), 0)
        c = lax.broadcasted_iota(jnp.int32, (ATT_TK, ATT_TK), 1)
        tri = (r <= c).astype(BF16)
        carry = jnp.zeros((N_HEADS, 1), F32)
        for j in range(nk):
            parts = _dot(_split3(lf_ref[0:N_HEADS, j * ATT_TK:(j + 1) * ATT_TK]), tri)
            cj = parts[0:8] + parts[8:16] + parts[16:24] + carry
            c_scr[j] = cj
            carry = cj[:, ATT_TK - 1:ATT_TK]
        rm = lax.broadcasted_iota(jnp.int32, (LANES, LANES), 0)
        cmx = lax.broadcasted_iota(jnp.int32, (LANES, LANES), 1)
        trim = ((rm > cmx) & (rm < N_META)).astype(BF16)
        pm = _dot(_split3(lfm_ref[0:N_HEADS, :]), trim)
        cm_scr[...] = -(pm[0:8] + pm[8:16] + pm[16:24])

    lane = lax.broadcasted_iota(jnp.int32, (ATT_TQ, LANES), 1)
    q_all = q_ref[...]
    zq = jnp.zeros_like(q_all)
    qm = (jnp.where(lane < HEAD_DIM, q_all, zq), jnp.where(lane >= HEAD_DIM, q_all, zq))
    for hl in range(2):
        m_scr[hl] = jnp.full((ATT_TQ, LANES), NEG_INF, F32)
        acc_scr[hl] = jnp.zeros((ATT_TQ, LANES), F32)

    def scores(kb, crows):
        return tuple(_dot_nt(qm[hl], kb) - crows[hl] for hl in range(2))

    def update(s2, vb, mask):
        vlane = lax.broadcasted_iota(jnp.int32, vb.shape, 1)
        ones = jnp.ones_like(vb)
        for hl in range(2):
            s = s2[hl]
            if mask is not None:
                s = jnp.where(mask, s, NEG_INF)
            m_prev = m_scr[hl]
            m_next = jnp.maximum(m_prev, jnp.max(s, axis=1, keepdims=True))
            reps = s.shape[1] // LANES
            m_wide = m_next if reps == 1 else jnp.concatenate([m_next] * reps, axis=1)
            pr = jnp.exp(s - m_wide).astype(BF16)
            alpha = jnp.exp(m_prev - m_next)
            own = (vlane < HEAD_DIM) if hl == 0 else (vlane >= HEAD_DIM)
            acc_scr[hl] = acc_scr[hl] * alpha + _dot(pr, jnp.where(own, vb, ones))
            m_scr[hl] = m_next

    def crows_at(j):
        return (c_scr[j, pl.ds(2 * p, 1), :], c_scr[j, pl.ds(2 * p + 1, 1), :])

    def kblock(j):
        return k_ref[pl.ds(pl.multiple_of(j * ATT_TK, ATT_TK), ATT_TK), :]

    def vblock(j):
        return v_ref[pl.ds(pl.multiple_of(j * ATT_TK, ATT_TK), ATT_TK), :]

    update(scores(km_ref[...], (cm_scr[pl.ds(2 * p, 1), :], cm_scr[pl.ds(2 * p + 1, 1), :])), vm_ref[...],
           lane < N_META)

    nfull = qi * (ATT_TQ // ATT_TK)

    def put(scr, s2):
        scr[0] = s2[0]
        scr[1] = s2[1]

    put(sa_scr, scores(kblock(0), crows_at(0)))

    def full2(i, carry):
        j = 2 * i
        put(sb_scr, scores(kblock(j + 1), crows_at(j + 1)))
        update((sa_scr[0], sa_scr[1]), vblock(j), None)
        put(sa_scr, scores(kblock(j + 2), crows_at(j + 2)))
        update((sb_scr[0], sb_scr[1]), vblock(j + 1), None)
        return carry

    lax.fori_loop(0, nfull // 2, full2, 0)

    rows = lax.broadcasted_iota(jnp.int32, (ATT_TQ, ATT_TK), 0)
    cols = lax.broadcasted_iota(jnp.int32, (ATT_TQ, ATT_TK), 1)
    s_diag = [(sa_scr[0], sa_scr[1])]
    for dj in range(1, ATT_TQ // ATT_TK):
        s_diag.append(scores(kblock(nfull + dj), crows_at(nfull + dj)))
    for dj in range(ATT_TQ // ATT_TK):
        update(s_diag[dj], vblock(nfull + dj), cols + dj * ATT_TK <= rows)

    a0 = acc_scr[0]
    a1 = acc_scr[1]
    o0 = a0 / pltpu.roll(a0, HEAD_DIM, axis=1)
    o1 = a1 / pltpu.roll(a1, HEAD_DIM, axis=1)
    o_ref[...] = jnp.where(lane < HEAD_DIM, o0, o1).astype(BF16)


def _attention(q, k, v, lf, km, vm, lfm):
    nq = SEQ // ATT_TQ
    return pl.pallas_call(
        _attn_kernel,
        grid=(BATCH, N_PAIRS, nq),
        in_specs=[
            pl.BlockSpec((ATT_TQ, LANES), lambda b, p, i: (b * nq + i, p)),
            pl.BlockSpec((SEQ, LANES), lambda b, p, i: (b, p)),
            pl.BlockSpec((SEQ, LANES), lambda b, p, i: (b, p)),
            pl.BlockSpec((16, SEQ), lambda b, p, i: (0, b)),
            pl.BlockSpec((META_ROWS, LANES), lambda b, p, i: (0, p)),
            pl.BlockSpec((META_ROWS, LANES), lambda b, p, i: (0, p)),
            pl.BlockSpec((16, META_ROWS), lambda b, p, i: (0, 0)),
        ],
        out_specs=pl.BlockSpec((ATT_TQ, LANES), lambda b, p, i: (b * nq + i, p)),
        out_shape=jax.ShapeDtypeStruct((N_TOK, ATTN_WIDTH), BF16),
        scratch_shapes=[
            pltpu.VMEM((SEQ // ATT_TK, N_HEADS, ATT_TK), F32),
            pltpu.VMEM((N_HEADS, LANES), F32),
            pltpu.VMEM((2, ATT_TQ, LANES), F32),
            pltpu.VMEM((2, ATT_TQ, LANES), F32),
            pltpu.VMEM((2, ATT_TQ, ATT_TK), F32),
            pltpu.VMEM((2, ATT_TQ, ATT_TK), F32),
        ],
        compiler_params=pltpu.CompilerParams(
            dimension_semantics=("arbitrary", "arbitrary", "arbitrary"), vmem_limit_bytes=VMEM_LIMIT),
        name="attn",
    )(q, k, v, lf, km, vm, lfm)


AUG_TERMS = 3


def _split_terms(x):
    hi = x.astype(BF16)
    r1 = x - hi.astype(F32)
    mid = r1.astype(BF16)
    lo = (r1 - mid.astype(F32)).astype(BF16)
    return hi, mid, lo


def _attn_t_kernel(q_ref, k_ref, vt_ref, lf_ref, km_ref, vm_ref, lfm_ref, place_ref, o_ref,
                   aug_scr, augm_scr, m_scr, acc_scr, sa_scr, sb_scr):
    p = pl.program_id(1)
    qi = pl.program_id(2)
    nk = SEQ // ATT_TK

    def bias_lanes(colsum):
        out = None
        for t, term in enumerate(_split_terms(colsum)):
            d = _dot(term, place_ref[t])
            out = d if out is None else out + d
        return out.astype(BF16)

    @pl.when((p == 0) & (qi == 0))
    def _():
        r = lax.broadcasted_iota(jnp.int32, (ATT_TK, ATT_TK), 0)
        c = lax.broadcasted_iota(jnp.int32, (ATT_TK, ATT_TK), 1)
        tri = (c <= r).astype(BF16)
        carry = jnp.zeros((1, 16), F32)
        for j in range(nk):
            lf = lf_ref[:, j * ATT_TK:(j + 1) * ATT_TK]
            csum = carry
            for term in _split_terms(lf):
                csum = csum + _dot_nt(tri, term)
            aug_scr[j * ATT_TK:(j + 1) * ATT_TK, :] = bias_lanes(csum * (-LOG2E))
            carry = csum[ATT_TK - 1:ATT_TK, :]
        rm = lax.broadcasted_iota(jnp.int32, (META_ROWS, META_ROWS), 0)
        cm = lax.broadcasted_iota(jnp.int32, (META_ROWS, META_ROWS), 1)
        trim = ((cm > rm) & (cm < N_META)).astype(BF16)
        msum = jnp.zeros((META_ROWS, 16), F32)
        for term in _split_terms(lfm_ref[...]):
            msum = msum + _dot_nt(trim, term)
        augm_scr[...] = bias_lanes(msum * LOG2E)

    lane = lax.broadcasted_iota(jnp.int32, (ATT_TQ, LANES), 1)
    q_all = q_ref[...]
    zq = jnp.zeros_like(q_all)
    q_cat = []
    for hl in range(2):
        own = (lane < HEAD_DIM) if hl == 0 else (lane >= HEAD_DIM)
        h = 2 * p + hl
        bias_sel = (lane >= AUG_TERMS * h) & (lane < AUG_TERMS * h + AUG_TERMS)
        bias_ones = jnp.where(bias_sel, 1.0, 0.0).astype(BF16)
        q_cat.append(jnp.concatenate([jnp.where(own, q_all, zq), bias_ones], axis=1))
        m_scr[hl] = jnp.full((1, ATT_TQ), NEG_INF, F32)
        acc_scr[hl] = jnp.zeros((LANES, ATT_TQ), F32)

    def scores(kb, augb, lo=0):
        k_cat = jnp.concatenate([kb, augb], axis=1)
        return tuple(_dot_nt(k_cat, q_cat[hl][lo:]) for hl in range(2))

    def update(s2, vtb, mask, lo=0):
        vrow = lax.broadcasted_iota(jnp.int32, vtb.shape, 0)
        ones = jnp.ones_like(vtb)
        for hl in range(2):
            s = s2[hl]
            if mask is not None:
                s = jnp.where(mask, s, NEG_INF)
            m_prev = m_scr[hl, :, lo:]
            m_next = jnp.maximum(m_prev, jnp.max(s, axis=0, keepdims=True))
            pr = jnp.exp2(s - m_next).astype(BF16)
            alpha = jnp.exp2(m_prev - m_next)
            own = (vrow < HEAD_DIM) if hl == 0 else (vrow >= HEAD_DIM)
            acc_scr[hl, :, lo:] = acc_scr[hl, :, lo:] * alpha + _dot(jnp.where(own, vtb, ones), pr)
            m_scr[hl, :, lo:] = m_next

    def kblock(j):
        return k_ref[pl.ds(pl.multiple_of(j * ATT_TK, ATT_TK), ATT_TK), :]

    def augblock(j):
        return aug_scr[pl.ds(pl.multiple_of(j * ATT_TK, ATT_TK), ATT_TK), :]

    nfull = qi * (ATT_TQ // ATT_TK)

    def put(scr, s2):
        scr[0] = s2[0]
        scr[1] = s2[1]

    s_meta = scores(km_ref[0:N_META, :], augm_scr[0:N_META, :])
    put(sa_scr, scores(kblock(0), augblock(0)))
    update(s_meta, vm_ref[0][:, 0:N_META], None)

    def full2(i, carry):
        j = 2 * i
        put(sb_scr, scores(kblock(j + 1), augblock(j + 1)))
        update((sa_scr[0], sa_scr[1]), vt_ref[j], None)
        put(sa_scr, scores(kblock(j + 2), augblock(j + 2)))
        update((sb_scr[0], sb_scr[1]), vt_ref[j + 1], None)
        return carry

    lax.fori_loop(0, nfull // 2, full2, 0)

    srow = lax.broadcasted_iota(jnp.int32, (ATT_TK, ATT_TQ), 0)
    tcol = lax.broadcasted_iota(jnp.int32, (ATT_TK, ATT_TQ), 1)
    s_diag = [(sa_scr[0], sa_scr[1])]
    for dj in range(1, ATT_TQ // ATT_TK):
        s_diag.append(scores(kblock(nfull + dj), augblock(nfull + dj), lo=dj * ATT_TK))
    for dj in range(ATT_TQ // ATT_TK):
        lo = dj * ATT_TK
        update(s_diag[dj], vt_ref[nfull + dj], (srow <= tcol)[:, :ATT_TQ - lo], lo=lo)

    a0 = acc_scr[0]
    a1 = acc_scr[1]
    ot = jnp.concatenate([a0[0:HEAD_DIM] / a0[HEAD_DIM:], a1[HEAD_DIM:] / a1[0:HEAD_DIM]], axis=0)
    o_ref[...] = ot.T.astype(BF16)


def _attention_t(q, k, vt, lf, km, vm, lfm):
    nq = SEQ // ATT_TQ
    nk = SEQ // ATT_TK
    hh = jnp.arange(16)[:, None]
    ll = jnp.arange(LANES)[None, :]
    place = jnp.stack([((ll == AUG_TERMS * hh + t) & (hh < N_HEADS)).astype(BF16) for t in range(AUG_TERMS)])
    return pl.pallas_call(
        _attn_t_kernel,
        grid=(BATCH, N_PAIRS, nq),
        in_specs=[
            pl.BlockSpec((ATT_TQ, LANES), lambda b, p, i: (b * nq + i, p)),
            pl.BlockSpec((SEQ, LANES), lambda b, p, i: (b, p)),
            pl.BlockSpec((nk, LANES, ATT_TK), lambda b, p, i: (b, p, 0)),
            pl.BlockSpec((16, SEQ), lambda b, p, i: (0, b)),
            pl.BlockSpec((META_ROWS, LANES), lambda b, p, i: (0, p)),
            pl.BlockSpec((1, LANES, META_ROWS), lambda b, p, i: (0, p, 0)),
            pl.BlockSpec((16, META_ROWS), lambda b, p, i: (0, 0)),
            _const_spec(place.shape),
        ],
        out_specs=pl.BlockSpec((ATT_TQ, LANES), lambda b, p, i: (b * nq + i, p)),
        out_shape=jax.ShapeDtypeStruct((N_TOK, ATTN_WIDTH), BF16),
        scratch_shapes=[
            pltpu.VMEM((SEQ, LANES), BF16),
            pltpu.VMEM((META_ROWS, LANES), BF16),
            pltpu.VMEM((2, 1, ATT_TQ), F32),
            pltpu.VMEM((2, LANES, ATT_TQ), F32),
            pltpu.VMEM((2, ATT_TK, ATT_TQ), F32),
            pltpu.VMEM((2, ATT_TK, ATT_TQ), F32),
        ],
        compiler_params=pltpu.CompilerParams(
            dimension_semantics=("arbitrary", "arbitrary", "arbitrary"), vmem_limit_bytes=VMEM_LIMIT),
        name="attn",
    )(q, k, vt, lf, km, vm, lfm, place)


MERGE_TM = 512


def _merge_kernel(act_ref, o_ref, sgc_ref, sga_ref, x_ref, wpw_ref, bpw_ref, wao_ref, wout_ref, g2_ref,
                  wrt_ref, brt_ref, tri_ref,
                  h1_ref, hn_ref, route_ref, routet_ref, cnt_ref, base_scr):
    i = pl.program_id(0)

    @pl.when(i == 0)
    def _():
        base_scr[...] = jnp.zeros_like(base_scr)

    y_conv = _dot(act_ref[...], wpw_ref[...]) + bpw_ref[...]
    y_attn = _dot(o_ref[...], wao_ref[...])
    merged = sgc_ref[...].astype(F32) * y_conv + sga_ref[...].astype(F32) * y_attn
    h1 = x_ref[...] + _dot(merged.astype(BF16), wout_ref[...])
    h1_ref[...] = h1
    ms = jnp.mean(h1 * h1, axis=-1, keepdims=True)
    hn = (h1 * lax.rsqrt(ms + RMS_EPS) * g2_ref[...]).astype(BF16)
    hn_ref[...] = hn.reshape(hn.shape[0], TOK_SUB, LANES)

    logits = _dot(hn, wrt_ref[...]) + brt_ref[...]
    lane = lax.broadcasted_iota(jnp.int32, logits.shape, 1).astype(F32)
    big = float(LANES)
    is_g = lane < N_GROUPS
    gl = jnp.where(is_g, logits, NEG_INF)
    gmax = jnp.max(gl, axis=1, keepdims=True)
    gidx = jnp.min(jnp.where(gl == gmax, lane, big), axis=1, keepdims=True)
    gsum = jnp.sum(jnp.where(is_g, jnp.exp(gl - gmax), 0.0), axis=1, keepdims=True)
    g_w = 1.0 / gsum
    lo = N_GROUPS + EXPERTS_PER_GROUP * gidx
    el = jnp.where((lane >= lo) & (lane < lo + EXPERTS_PER_GROUP), logits, NEG_INF)
    t1 = jnp.max(el, axis=1, keepdims=True)
    i1 = jnp.min(jnp.where(el == t1, lane, big), axis=1, keepdims=True)
    el2 = jnp.where(lane == i1, NEG_INF, el)
    t2 = jnp.max(el2, axis=1, keepdims=True)
    i2 = jnp.min(jnp.where(el2 == t2, lane, big), axis=1, keepdims=True)
    e2 = jnp.exp(t2 - t1)
    den = 1.0 + e2
    w1 = (1.0 / den) * g_w
    w2 = (e2 / den) * g_w
    ea = i1 - N_GROUPS
    eb = i2 - N_GROUPS

    oha = (lane == ea).astype(F32)
    ohb = (lane == eb).astype(F32)
    ohs = oha + ohb
    before = _dot(tri_ref[...], ohs.astype(BF16)) + base_scr[0:1, :]
    ra = jnp.sum(before * oha, axis=1, keepdims=True)
    rb = jnp.sum(before * ohb, axis=1, keepdims=True)
    base_scr[...] = base_scr[...] + jnp.sum(ohs, axis=0, keepdims=True)
    cnt_ref[...] = base_scr[...]

    route = jnp.where(lane == 0, ea, 0.0)
    route = jnp.where(lane == 1, eb, route)
    route = jnp.where(lane == 2, w1, route)
    route = jnp.where(lane == 3, w2, route)
    route = jnp.where(lane == 4, ra, route)
    route = jnp.where(lane == 5, rb, route)
    route_ref[...] = route
    routet_ref[...] = route.T[0:8]


def _merge(act, o, sgc, sga, x2d, consts):
    tm = MERGE_TM
    row = lambda w: pl.BlockSpec((tm, w), lambda i: (i, 0))
    return pl.pallas_call(
        _merge_kernel,
        grid=(N_TOK // tm,),
        in_specs=[row(CONV_WIDTH), row(ATTN_WIDTH), row(D_MODEL), row(D_MODEL), row(D_MODEL)]
        + [_const_spec(c.shape) for c in consts],
        out_specs=[row(D_MODEL), pl.BlockSpec((tm, TOK_SUB, LANES), lambda i: (i, 0, 0)), row(LANES),
                   pl.BlockSpec((8, tm), lambda i: (0, i)), _const_spec((8, LANES))],
        out_shape=[jax.ShapeDtypeStruct((N_TOK, D_MODEL), F32),
                   jax.ShapeDtypeStruct((N_TOK, TOK_SUB, LANES), BF16),
                   jax.ShapeDtypeStruct((N_TOK, LANES), F32),
                   jax.ShapeDtypeStruct((8, N_TOK), F32),
                   jax.ShapeDtypeStruct((8, LANES), F32)],
        scratch_shapes=[pltpu.VMEM((8, LANES), F32)],
        compiler_params=pltpu.CompilerParams(dimension_semantics=("arbitrary",), vmem_limit_bytes=VMEM_LIMIT),
        name="merge",
    )(act, o, sgc, sga, x2d, *consts)


DISP_TM = 1024
DMA_UNROLL = 8


def _dest_row(code_ref, pstart_ref, idx):
    c = code_ref[idx]
    return pstart_ref[lax.shift_right_logical(c, RANK_BITS)] + (c & ((1 << RANK_BITS) - 1))


def _dispatch_kernel(c0_ref, c1_ref, pstart_ref, pend_ref, hn_ref, xin_ref, zero_scr, sem, zsem):
    i = pl.program_id(0)
    base = i * DISP_TM

    @pl.when(i == 0)
    def _():
        zero_scr[...] = jnp.zeros_like(zero_scr)

        def zcopy(e):
            start = pl.multiple_of(pend_ref[e] - ROW_BLOCK, ROW_BLOCK)
            return pltpu.make_async_copy(zero_scr, xin_ref.at[pl.ds(start, ROW_BLOCK)], zsem.at[0])

        def zstart(e, carry):
            @pl.when(pend_ref[e] > pstart_ref[e])
            def _():
                zcopy(e).start()
            return carry

        def zwait(e, carry):
            @pl.when(pend_ref[e] > pstart_ref[e])
            def _():
                zcopy(e).wait()
            return carry

        n_used = pend_ref[N_EXPERTS - 1] // ROW_BLOCK

        def tcopy(b):
            return pltpu.make_async_copy(
                zero_scr, xin_ref.at[pl.ds(pl.multiple_of(b * ROW_BLOCK, ROW_BLOCK), ROW_BLOCK)], zsem.at[0])

        def tstart(b, carry):
            tcopy(b).start()
            return carry

        def twait(b, carry):
            tcopy(b).wait()
            return carry

        lax.fori_loop(0, N_EXPERTS, zstart, 0)
        lax.fori_loop(n_used, N_BLOCKS, tstart, 0)
        lax.fori_loop(0, N_EXPERTS, zwait, 0)
        lax.fori_loop(n_used, N_BLOCKS, twait, 0)

    def start(r, carry):
        pltpu.make_async_copy(hn_ref.at[r], xin_ref.at[_dest_row(c0_ref, pstart_ref, base + r)], sem.at[0]).start()
        pltpu.make_async_copy(hn_ref.at[r], xin_ref.at[_dest_row(c1_ref, pstart_ref, base + r)], sem.at[1]).start()
        return carry

    lax.fori_loop(0, DISP_TM, start, 0, unroll=DMA_UNROLL)
    pltpu.make_async_copy(hn_ref, xin_ref.at[pl.ds(0, DISP_TM)], sem.at[0]).wait()
    pltpu.make_async_copy(hn_ref, xin_ref.at[pl.ds(0, DISP_TM)], sem.at[1]).wait()


def _dispatch(code0, code1, pstart, pend, hn):
    grid_spec = pltpu.PrefetchScalarGridSpec(
        num_scalar_prefetch=4,
        grid=(N_TOK // DISP_TM,),
        in_specs=[pl.BlockSpec((DISP_TM, TOK_SUB, LANES), lambda i, *_: (i, 0, 0))],
        out_specs=pl.BlockSpec(memory_space=pl.ANY),
        scratch_shapes=[pltpu.VMEM((ROW_BLOCK, TOK_SUB, LANES), BF16),
                        pltpu.SemaphoreType.DMA((2,)), pltpu.SemaphoreType.DMA((1,))],
    )
    return pl.pallas_call(
        _dispatch_kernel,
        grid_spec=grid_spec,
        out_shape=jax.ShapeDtypeStruct((P_ROWS, TOK_SUB, LANES), BF16),
        compiler_params=pltpu.CompilerParams(dimension_semantics=("arbitrary",), vmem_limit_bytes=VMEM_LIMIT),
        name="dispatch",
    )(code0, code1, pstart, pend, hn)


W_SLOTS = 3


def _expert_kernel(nu_ref, first_ref, ord_ref, used_ref, nue_ref, x_ref, wg_hbm, wu_hbm, wd_hbm, y_ref,
                   wg_f, wu_f, wd_f, wg_b, wu_b, wd_b, sem):
    i = pl.program_id(0)

    def fetch(k):
        e = used_ref[k]
        slot = k % W_SLOTS
        return (pltpu.make_async_copy(wg_hbm.at[e], wg_f.at[slot], sem.at[slot, 0]),
                pltpu.make_async_copy(wu_hbm.at[e], wu_f.at[slot], sem.at[slot, 1]),
                pltpu.make_async_copy(wd_hbm.at[e], wd_f.at[slot], sem.at[slot, 2]))

    @pl.when(i == 0)
    def _():
        for k in range(W_SLOTS - 1):
            @pl.when(k < nue_ref[0])
            def _():
                for c in fetch(k):
                    c.start()

    @pl.when(i < nu_ref[0])
    def _():
        @pl.when(first_ref[i] == 1)
        def _():
            k = ord_ref[i]
            for c in fetch(k):
                c.wait()

            @pl.when(k + (W_SLOTS - 1) < nue_ref[0])
            def _():
                for c in fetch(k + (W_SLOTS - 1)):
                    c.start()

            slot = k % W_SLOTS
            wg_b[...] = wg_f[slot].astype(BF16)
            wu_b[...] = wu_f[slot].astype(BF16)
            wd_b[...] = wd_f[slot].astype(BF16)

        x = x_ref[...].reshape(ROW_BLOCK, D_MODEL)
        g = _dot(x, wg_b[...])
        u = _dot(x, wu_b[...])
        hmid = (g * _sigmoid(g) * u).astype(BF16)
        y = _dot(hmid, wd_b[...]).astype(BF16)
        y_ref[...] = y.reshape(ROW_BLOCK, TOK_SUB, LANES)

    @pl.when(i >= nu_ref[0])
    def _():
        y_ref[...] = jnp.zeros_like(y_ref)


def _experts(n_used, blk_first, blk_ord, used_e, n_used_e, xin, w_gate, w_up, w_down):
    def xmap(i, nu, *_):
        return (jnp.minimum(i, nu[0] - 1), 0, 0)

    grid_spec = pltpu.PrefetchScalarGridSpec(
        num_scalar_prefetch=5,
        grid=(N_BLOCKS,),
        in_specs=[pl.BlockSpec((ROW_BLOCK, TOK_SUB, LANES), xmap),
                  pl.BlockSpec(memory_space=pl.ANY),
                  pl.BlockSpec(memory_space=pl.ANY),
                  pl.BlockSpec(memory_space=pl.ANY)],
        out_specs=pl.BlockSpec((ROW_BLOCK, TOK_SUB, LANES), lambda i, *_: (i, 0, 0)),
        scratch_shapes=[pltpu.VMEM((W_SLOTS, D_MODEL, D_EXPERT), F32),
                        pltpu.VMEM((W_SLOTS, D_MODEL, D_EXPERT), F32),
                        pltpu.VMEM((W_SLOTS, D_EXPERT, D_MODEL), F32),
                        pltpu.VMEM((D_MODEL, D_EXPERT), BF16),
                        pltpu.VMEM((D_MODEL, D_EXPERT), BF16),
                        pltpu.VMEM((D_EXPERT, D_MODEL), BF16),
                        pltpu.SemaphoreType.DMA((W_SLOTS, 3))],
    )
    return pl.pallas_call(
        _expert_kernel,
        grid_spec=grid_spec,
        out_shape=jax.ShapeDtypeStruct((P_ROWS, TOK_SUB, LANES), BF16),
        compiler_params=pltpu.CompilerParams(dimension_semantics=("arbitrary",), vmem_limit_bytes=VMEM_LIMIT),
        name="experts",
    )(n_used, blk_first, blk_ord, used_e, n_used_e, xin, w_gate, w_up, w_down)


COMB_TM = 256


def _combine_kernel(c0_ref, c1_ref, pstart_ref, h1_ref, route_ref, yb_ref, out_ref, g_scr, sem):
    i = pl.program_id(0)
    n = pl.num_programs(0)

    def issue(tile, slot):
        base = tile * COMB_TM

        def start(r, carry):
            pltpu.make_async_copy(yb_ref.at[_dest_row(c0_ref, pstart_ref, base + r)], g_scr.at[slot, 0, r],
                                  sem.at[slot, 0]).start()
            pltpu.make_async_copy(yb_ref.at[_dest_row(c1_ref, pstart_ref, base + r)], g_scr.at[slot, 1, r],
                                  sem.at[slot, 1]).start()
            return carry

        lax.fori_loop(0, COMB_TM, start, 0, unroll=DMA_UNROLL)

    @pl.when(i == 0)
    def _():
        issue(0, 0)

    slot = i % 2

    @pl.when(i + 1 < n)
    def _():
        issue(i + 1, 1 - slot)

    for k in range(2):
        pltpu.make_async_copy(yb_ref.at[pl.ds(0, COMB_TM)], g_scr.at[slot, k], sem.at[slot, k]).wait()

    route = route_ref[...]
    w0 = route[:, 2:3]
    w1 = route[:, 3:4]
    g0 = g_scr[slot, 0].reshape(COMB_TM, D_MODEL).astype(F32)
    g1 = g_scr[slot, 1].reshape(COMB_TM, D_MODEL).astype(F32)
    out_ref[...] = h1_ref[...] + (g0 * w0 + g1 * w1)


def _combine(code0, code1, pstart, h1, route, yb):
    grid_spec = pltpu.PrefetchScalarGridSpec(
        num_scalar_prefetch=3,
        grid=(N_TOK // COMB_TM,),
        in_specs=[pl.BlockSpec((COMB_TM, D_MODEL), lambda i, *_: (i, 0)),
                  pl.BlockSpec((COMB_TM, LANES), lambda i, *_: (i, 0)),
                  pl.BlockSpec(memory_space=pl.ANY)],
        out_specs=pl.BlockSpec((COMB_TM, D_MODEL), lambda i, *_: (i, 0)),
        scratch_shapes=[pltpu.VMEM((2, 2, COMB_TM, TOK_SUB, LANES), BF16),
                        pltpu.SemaphoreType.DMA((2, 2))],
    )
    return pl.pallas_call(
        _combine_kernel,
        grid_spec=grid_spec,
        out_shape=jax.ShapeDtypeStruct((N_TOK, D_MODEL), F32),
        compiler_params=pltpu.CompilerParams(dimension_semantics=("arbitrary",), vmem_limit_bytes=VMEM_LIMIT),
        name="combine",
    )(code0, code1, pstart, h1, route, yb)


def kernel(x, meta, norm1_g, w_in, b_in, dw_w, dw_b, conv_ln_g, conv_ln_b, w_conv_out, b_conv_out,
           q_norm_g, k_norm_g, b_forget, w_attn_out, w_out, norm2_g, w_group, b_group, w_router,
           b_router, w_gate, w_up, w_down):
    l = 0
    row = lambda v: v.reshape(1, -1).astype(F32)

    c_u, c_q, c_f = 2 * CONV_WIDTH, 2 * CONV_WIDTH, 2 * CONV_WIDTH + 3 * ATTN_WIDTH
    c_gc = c_f + N_HEADS
    c_ga = c_gc + D_MODEL
    wi, bi = w_in[l], b_in[l]
    wu, bu = wi[:, :c_u].astype(BF16), row(bi[:c_u])
    wqkv, bqkv = wi[:, c_q:c_f].astype(BF16), row(bi[c_q:c_f])
    wft = jnp.zeros((16, D_MODEL), F32).at[:N_HEADS].set(wi[:, c_f:c_gc].T).astype(BF16)
    bft = jnp.zeros((16, 1), F32).at[:N_HEADS, 0].set(bi[c_f:c_gc] + b_forget[l])
    wgc, bgc = wi[:, c_gc:c_ga].astype(BF16), row(bi[c_gc:c_ga])
    wga, bga = wi[:, c_ga:].astype(BF16), row(bi[c_ga:])
    qg = row(jnp.tile(q_norm_g[l], N_HEADS) * (HEAD_DIM ** -0.5 * LOG2E))
    kg = row(jnp.tile(k_norm_g[l], N_HEADS))
    head_of_col = jnp.arange(ATTN_WIDTH) // HEAD_DIM
    hsum = (head_of_col[:, None] == jnp.arange(LANES)[None, :]).astype(BF16)
    hexp_half = (jnp.arange(LANES)[:, None] == head_of_col[None, :]).astype(BF16)
    hexp = jnp.concatenate([hexp_half, hexp_half], axis=0)
    proj_consts = (row(norm1_g[l]), wu, bu, wqkv, bqkv, wft, bft, wgc, bgc, wga, bga, qg, kg, hsum, hexp)

    x2d = x.reshape(N_TOK, D_MODEL)
    meta_pad = jnp.zeros((META_ROWS, D_MODEL), F32).at[:N_META].set(meta.astype(F32))
    a, q, k, v, lf, sgc, sga = _proj(x2d, proj_consts, 512)
    a_m, _, k_m, v_m, lf_m, _, _ = _proj(meta_pad, proj_consts, META_ROWS)

    act = _conv(a.reshape(BATCH, SEQ, CONV_WIDTH), a_m, dw_w[l].astype(F32), row(dw_b[l]),
                row(conv_ln_g[l]), row(conv_ln_b[l])).reshape(N_TOK, CONV_WIDTH)

    o = _attention_t(q, k, v, lf, k_m, v_m, lf_m)

    wrt = jnp.zeros((D_MODEL, LANES), F32).at[:, :N_GROUPS].set(w_group[l])
    wrt = wrt.at[:, N_GROUPS:N_GROUPS + N_EXPERTS].set(w_router[l]).astype(BF16)
    brt = jnp.zeros((1, LANES), F32).at[0, :N_GROUPS].set(b_group[l])
    brt = brt.at[0, N_GROUPS:N_GROUPS + N_EXPERTS].set(b_router[l])
    tri = jnp.tril(jnp.ones((MERGE_TM, MERGE_TM), F32), -1).astype(BF16)
    merge_consts = (w_conv_out[l].astype(BF16), row(b_conv_out[l]), w_attn_out[l].astype(BF16),
                    w_out[l].astype(BF16), row(norm2_g[l]), wrt, brt, tri)
    h1, hn, route, route_t, cnt = _merge(act, o, sgc, sga, x2d, merge_consts)

    rt = route_t.astype(jnp.int32)
    code0 = rt[0] * (1 << RANK_BITS) + rt[4]
    code1 = rt[1] * (1 << RANK_BITS) + rt[5]
    counts = cnt[0, :N_EXPERTS].astype(jnp.int32)
    padded = (counts + ROW_BLOCK - 1) // ROW_BLOCK * ROW_BLOCK
    pend = jnp.cumsum(padded).astype(jnp.int32)
    pstart = pend - padded
    blk_row = jnp.arange(N_BLOCKS, dtype=jnp.int32) * ROW_BLOCK
    n_used = pend[-1:] // ROW_BLOCK
    nz = counts > 0
    before = (pend[None, :] <= blk_row[:, None]) & nz[None, :]
    blk_ord = jnp.sum(before.astype(jnp.int32), axis=1)
    blk_first = jnp.any((pstart[None, :] == blk_row[:, None]) & nz[None, :], axis=1).astype(jnp.int32)
    e_ord = jnp.cumsum(nz.astype(jnp.int32)) - 1
    e_ids = jnp.arange(N_EXPERTS, dtype=jnp.int32)
    used_e = jnp.sum(jnp.where((e_ord[None, :] == e_ids[:, None]) & nz[None, :], e_ids[None, :], 0), axis=1)
    n_used_e = jnp.sum(nz.astype(jnp.int32)).reshape(1)

    xin = _dispatch(code0, code1, pstart, pend, hn)
    yb = _experts(n_used, blk_first, blk_ord, used_e.astype(jnp.int32), n_used_e, xin,
                  w_gate[l], w_up[l], w_down[l])
    out = _combine(code0, code1, pstart, h1, route, yb)
    return out.reshape(BATCH, SEQ, D_MODEL)
```

```python
import functools

import jax
import jax.numpy as jnp
from jax import lax
from jax.experimental import pallas as pl
from jax.experimental.pallas import tpu as pltpu

D_MODEL = 1024
BATCH = 8
SEQ = 2048
N_META = 16
CONV_WIDTH = 512
CONV_K = 31
N_HEADS = 8
HEAD_DIM = 64
ATTN_WIDTH = N_HEADS * HEAD_DIM
N_GROUPS = 8
EXPERTS_PER_GROUP = 8
N_EXPERTS = N_GROUPS * EXPERTS_PER_GROUP
D_EXPERT = 256
RMS_EPS = 1e-6
LN_EPS = 1e-5

N_TOK = BATCH * SEQ
LANES = 128
META_ROWS = 128
ROW_BLOCK = 512
N_ASSIGN = 2 * N_TOK
N_BLOCKS = N_ASSIGN // ROW_BLOCK + N_EXPERTS
P_ROWS = N_BLOCKS * ROW_BLOCK
TOK_SUB = D_MODEL // LANES
RANK_BITS = 16
ATT_TQ = 512
ATT_TK = 256
N_PAIRS = N_HEADS // 2
assert (ATT_TQ // ATT_TK) % 2 == 0
VMEM_LIMIT = 56 * 1024 * 1024

F32 = jnp.float32
BF16 = jnp.bfloat16
NEG_INF = float("-inf")
LOG2E = 1.4426950408889634


def _dot(a, b):
    return jnp.dot(a, b, preferred_element_type=F32)


def _dot_nt(a, b):
    return lax.dot_general(a, b, (((1,), (1,)), ((), ())), preferred_element_type=F32)


def _sigmoid(x):
    return 1.0 / (1.0 + jnp.exp(-x))


def _const_spec(shape):
    nd = len(shape)
    return pl.BlockSpec(shape, lambda *_: (0,) * nd)


def _head_rms(t, hsum_ref, hexp_ref, gain):
    ss = _dot((t * t).astype(BF16), hsum_ref[...])
    inv = lax.rsqrt(ss * (1.0 / HEAD_DIM) + RMS_EPS)
    hi = inv.astype(BF16)
    lo = (inv - hi.astype(F32)).astype(BF16)
    invb = _dot(jnp.concatenate([hi, lo], axis=1), hexp_ref[...])
    return t * invb * gain


def _proj_kernel(x_ref, g1_ref, wu_ref, bu_ref, wqkv_ref, bqkv_ref, wft_ref, bft_ref, wgc_ref, bgc_ref,
                 wga_ref, bga_ref, qg_ref, kg_ref, hsum_ref, hexp_ref,
                 a_ref, q_ref, k_ref, v_ref, lf_ref, sgc_ref, sga_ref):
    x = x_ref[...]
    ms = jnp.mean(x * x, axis=-1, keepdims=True)
    xn = (x * lax.rsqrt(ms + RMS_EPS) * g1_ref[...]).astype(BF16)

    u = _dot(xn, wu_ref[...]) + bu_ref[...]
    a_ref[...] = (u[:, :CONV_WIDTH] * _sigmoid(u[:, CONV_WIDTH:])).astype(BF16)

    qkv = _dot(xn, wqkv_ref[...]) + bqkv_ref[...]
    q_ref[...] = _head_rms(qkv[:, :ATTN_WIDTH], hsum_ref, hexp_ref, qg_ref[...]).astype(BF16)
    k_ref[...] = _head_rms(qkv[:, ATTN_WIDTH:2 * ATTN_WIDTH], hsum_ref, hexp_ref, kg_ref[...]).astype(BF16)
    v = qkv[:, 2 * ATTN_WIDTH:]
    vb = v_ref.shape[2]
    for c in range(v_ref.shape[0]):
        v_ref[c] = v[c * vb:(c + 1) * vb, :].T.astype(BF16)

    f = _dot_nt(wft_ref[...], xn) + bft_ref[...]
    lf_ref[...] = jnp.minimum(f, 0.0) - jnp.log(1.0 + jnp.exp(-jnp.abs(f)))

    sgc_ref[...] = _sigmoid(_dot(xn, wgc_ref[...]) + bgc_ref[...]).astype(BF16)
    sga_ref[...] = _sigmoid(_dot(xn, wga_ref[...]) + bga_ref[...]).astype(BF16)


def _proj(x2d, consts, tm):
    n = x2d.shape[0]
    row = lambda w: pl.BlockSpec((tm, w), lambda i: (i, 0))
    in_specs = [row(D_MODEL)] + [_const_spec(c.shape) for c in consts]
    vb = min(tm, ATT_TK)
    out_shape = [
        jax.ShapeDtypeStruct((n, CONV_WIDTH), BF16),
        jax.ShapeDtypeStruct((n, ATTN_WIDTH), BF16),
        jax.ShapeDtypeStruct((n, ATTN_WIDTH), BF16),
        jax.ShapeDtypeStruct((n // vb, ATTN_WIDTH, vb), BF16),
        jax.ShapeDtypeStruct((16, n), F32),
        jax.ShapeDtypeStruct((n, D_MODEL), BF16),
        jax.ShapeDtypeStruct((n, D_MODEL), BF16),
    ]
    out_specs = [row(CONV_WIDTH), row(ATTN_WIDTH), row(ATTN_WIDTH),
                 pl.BlockSpec((tm // vb, ATTN_WIDTH, vb), lambda i: (i, 0, 0)),
                 pl.BlockSpec((16, tm), lambda i: (0, i)), row(D_MODEL), row(D_MODEL)]
    return pl.pallas_call(
        _proj_kernel,
        grid=(n // tm,),
        in_specs=in_specs,
        out_specs=out_specs,
        out_shape=out_shape,
        compiler_params=pltpu.CompilerParams(dimension_semantics=("arbitrary",), vmem_limit_bytes=VMEM_LIMIT),
        name="proj",
    )(x2d, *consts)


CONV_CHUNK = 64
CONV_PAD = 32
CONV_WIN = CONV_CHUNK + 32
CONV_SHROWS = CONV_WIN


def _conv_kernel(a_ref, am_ref, shift_ref, dw_ref, dwb_ref, lng_ref, lnb_ref, o_ref, pad_scr, sha_scr, shb_scr):
    pad_scr[0:N_META, :] = jnp.zeros((N_META, CONV_WIDTH), BF16)
    pad_scr[N_META:CONV_PAD, :] = am_ref[0:N_META, :]
    pad_scr[CONV_PAD:CONV_PAD + SEQ, :] = a_ref[0]

    def shifted_copies(ci):
        r0 = pl.multiple_of(ci * CONV_CHUNK, CONV_CHUNK)
        return _dot(shift_ref[...], pad_scr[pl.ds(r0, CONV_WIN), :])

    n_chunks = SEQ // CONV_CHUNK
    sha_scr[...] = shifted_copies(0)

    def finish(ci, cur_scr):
        r0 = pl.multiple_of(ci * CONV_CHUNK, CONV_CHUNK)
        acc = jnp.zeros((CONV_CHUNK, CONV_WIDTH), F32) + dwb_ref[...]
        for j in range(CONV_K):
            off = j + 2
            b = off % 8
            a0 = b * CONV_SHROWS + (off - b)
            acc = acc + cur_scr[a0:a0 + CONV_CHUNK, :] * dw_ref[j:j + 1, :]
        mu = jnp.mean(acc, axis=-1, keepdims=True)
        d = acc - mu
        var = jnp.mean(d * d, axis=-1, keepdims=True)
        y = d * lax.rsqrt(var + LN_EPS) * lng_ref[...] + lnb_ref[...]
        o_ref[0, pl.ds(r0, CONV_CHUNK), :] = (y * _sigmoid(y)).astype(BF16)

    def chunk_pair(i, carry):
        ci = 2 * i
        shb_scr[...] = shifted_copies(ci + 1)
        finish(ci, sha_scr)
        sha_scr[...] = shifted_copies(jnp.minimum(ci + 2, n_chunks - 1))
        finish(ci + 1, shb_scr)
        return carry

    lax.fori_loop(0, n_chunks // 2, chunk_pair, 0)


def _conv(a3, a_meta, dw_w, dw_b, ln_g, ln_b):
    rr = jnp.arange(8 * CONV_SHROWS)
    shift = ((rr % CONV_SHROWS + rr // CONV_SHROWS)[:, None] == jnp.arange(CONV_WIN)[None, :]).astype(BF16)
    return pl.pallas_call(
        _conv_kernel,
        grid=(BATCH,),
        in_specs=[pl.BlockSpec((1, SEQ, CONV_WIDTH), lambda b: (b, 0, 0)),
                  _const_spec(a_meta.shape), _const_spec(shift.shape), _const_spec(dw_w.shape),
                  _const_spec(dw_b.shape), _const_spec(ln_g.shape), _const_spec(ln_b.shape)],
        out_specs=pl.BlockSpec((1, SEQ, CONV_WIDTH), lambda b: (b, 0, 0)),
        out_shape=jax.ShapeDtypeStruct((BATCH, SEQ, CONV_WIDTH), BF16),
        scratch_shapes=[pltpu.VMEM((CONV_PAD + SEQ, CONV_WIDTH), BF16),
                        pltpu.VMEM((8 * CONV_SHROWS, CONV_WIDTH), F32),
                        pltpu.VMEM((8 * CONV_SHROWS, CONV_WIDTH), F32)],
        compiler_params=pltpu.CompilerParams(dimension_semantics=("arbitrary",), vmem_limit_bytes=VMEM_LIMIT),
        name="conv",
    )(a3, a_meta, shift, dw_w, dw_b, ln_g, ln_b)


def _split3(x):
    hi = x.astype(BF16)
    r1 = x - hi.astype(F32)
    mid = r1.astype(BF16)
    lo = (r1 - mid.astype(F32)).astype(BF16)
    return jnp.concatenate([hi, mid, lo], axis=0)


def _attn_kernel(q_ref, k_ref, v_ref, lf_ref, km_ref, vm_ref, lfm_ref, o_ref,
                 c_scr, cm_scr, m_scr, acc_scr, sa_scr, sb_scr):
    p = pl.program_id(1)
    qi = pl.program_id(2)
    nk = SEQ // ATT_TK

    @pl.when((p == 0) & (qi == 0))
    def _():
        r = lax.broadcasted_iota(jnp.int32, (ATT_TK, ATT_TK), 0)
        c = lax.broadcasted_iota(jnp.int32, (ATT_TK, ATT_TK), 1)
        tri = (r <= c).astype(BF16)
        carry = jnp.zeros((N_HEADS, 1), F32)
        for j in range(nk):
            parts = _dot(_split3(lf_ref[0:N_HEADS, j * ATT_TK:(j + 1) * ATT_TK]), tri)
            cj = parts[0:8] + parts[8:16] + parts[16:24] + carry
            c_scr[j] = cj
            carry = cj[:, ATT_TK - 1:ATT_TK]
        rm = lax.broadcasted_iota(jnp.int32, (LANES, LANES), 0)
        cmx = lax.broadcasted_iota(jnp.int32, (LANES, LANES), 1)
        trim = ((rm > cmx) & (rm < N_META)).astype(BF16)
        pm = _dot(_split3(lfm_ref[0:N_HEADS, :]), trim)
        cm_scr[...] = -(pm[0:8] + pm[8:16] + pm[16:24])

    lane = lax.broadcasted_iota(jnp.int32, (ATT_TQ, LANES), 1)
    q_all = q_ref[...]
    zq = jnp.zeros_like(q_all)
    qm = (jnp.where(lane < HEAD_DIM, q_all, zq), jnp.where(lane >= HEAD_DIM, q_all, zq))
    for hl in range(2):
        m_scr[hl] = jnp.full((ATT_TQ, LANES), NEG_INF, F32)
        acc_scr[hl] = jnp.zeros((ATT_TQ, LANES), F32)

    def scores(kb, crows):
        return tuple(_dot_nt(qm[hl], kb) - crows[hl] for hl in range(2))

    def update(s2, vb, mask):
        vlane = lax.broadcasted_iota(jnp.int32, vb.shape, 1)
        ones = jnp.ones_like(vb)
        for hl in range(2):
            s = s2[hl]
            if mask is not None:
                s = jnp.where(mask, s, NEG_INF)
            m_prev = m_scr[hl]
            m_next = jnp.maximum(m_prev, jnp.max(s, axis=1, keepdims=True))
            reps = s.shape[1] // LANES
            m_wide = m_next if reps == 1 else jnp.concatenate([m_next] * reps, axis=1)
            pr = jnp.exp(s - m_wide).astype(BF16)
            alpha = jnp.exp(m_prev - m_next)
            own = (vlane < HEAD_DIM) if hl == 0 else (vlane >= HEAD_DIM)
            acc_scr[hl] = acc_scr[hl] * alpha + _dot(pr, jnp.where(own, vb, ones))
            m_scr[hl] = m_next

    def crows_at(j):
        return (c_scr[j, pl.ds(2 * p, 1), :], c_scr[j, pl.ds(2 * p + 1, 1), :])

    def kblock(j):
        return k_ref[pl.ds(pl.multiple_of(j * ATT_TK, ATT_TK), ATT_TK), :]

    def vblock(j):
        return v_ref[pl.ds(pl.multiple_of(j * ATT_TK, ATT_TK), ATT_TK), :]

    update(scores(km_ref[...], (cm_scr[pl.ds(2 * p, 1), :], cm_scr[pl.ds(2 * p + 1, 1), :])), vm_ref[...],
           lane < N_META)

    nfull = qi * (ATT_TQ // ATT_TK)

    def put(scr, s2):
        scr[0] = s2[0]
        scr[1] = s2[1]

    put(sa_scr, scores(kblock(0), crows_at(0)))

    def full2(i, carry):
        j = 2 * i
        put(sb_scr, scores(kblock(j + 1), crows_at(j + 1)))
        update((sa_scr[0], sa_scr[1]), vblock(j), None)
        put(sa_scr, scores(kblock(j + 2), crows_at(j + 2)))
        update((sb_scr[0], sb_scr[1]), vblock(j + 1), None)
        return carry

    lax.fori_loop(0, nfull // 2, full2, 0)

    rows = lax.broadcasted_iota(jnp.int32, (ATT_TQ, ATT_TK), 0)
    cols = lax.broadcasted_iota(jnp.int32, (ATT_TQ, ATT_TK), 1)
    s_diag = [(sa_scr[0], sa_scr[1])]
    for dj in range(1, ATT_TQ // ATT_TK):
        s_diag.append(scores(kblock(nfull + dj), crows_at(nfull + dj)))
    for dj in range(ATT_TQ // ATT_TK):
        update(s_diag[dj], vblock(nfull + dj), cols + dj * ATT_TK <= rows)

    a0 = acc_scr[0]
    a1 = acc_scr[1]
    o0 = a0 / pltpu.roll(a0, HEAD_DIM, axis=1)
    o1 = a1 / pltpu.roll(a1, HEAD_DIM, axis=1)
    o_ref[...] = jnp.where(lane < HEAD_DIM, o0, o1).astype(BF16)


def _attention(q, k, v, lf, km, vm, lfm):
    nq = SEQ // ATT_TQ
    return pl.pallas_call(
        _attn_kernel,
        grid=(BATCH, N_PAIRS, nq),
        in_specs=[
            pl.BlockSpec((ATT_TQ, LANES), lambda b, p, i: (b * nq + i, p)),
            pl.BlockSpec((SEQ, LANES), lambda b, p, i: (b, p)),
            pl.BlockSpec((SEQ, LANES), lambda b, p, i: (b, p)),
            pl.BlockSpec((16, SEQ), lambda b, p, i: (0, b)),
            pl.BlockSpec((META_ROWS, LANES), lambda b, p, i: (0, p)),
            pl.BlockSpec((META_ROWS, LANES), lambda b, p, i: (0, p)),
            pl.BlockSpec((16, META_ROWS), lambda b, p, i: (0, 0)),
        ],
        out_specs=pl.BlockSpec((ATT_TQ, LANES), lambda b, p, i: (b * nq + i, p)),
        out_shape=jax.ShapeDtypeStruct((N_TOK, ATTN_WIDTH), BF16),
        scratch_shapes=[
            pltpu.VMEM((SEQ // ATT_TK, N_HEADS, ATT_TK), F32),
            pltpu.VMEM((N_HEADS, LANES), F32),
            pltpu.VMEM((2, ATT_TQ, LANES), F32),
            pltpu.VMEM((2, ATT_TQ, LANES), F32),
            pltpu.VMEM((2, ATT_TQ, ATT_TK), F32),
            pltpu.VMEM((2, ATT_TQ, ATT_TK), F32),
        ],
        compiler_params=pltpu.CompilerParams(
            dimension_semantics=("arbitrary", "arbitrary", "arbitrary"), vmem_limit_bytes=VMEM_LIMIT),
        name="attn",
    )(q, k, v, lf, km, vm, lfm)


AUG_TERMS = 3


def _split_terms(x):
    hi = x.astype(BF16)
    r1 = x - hi.astype(F32)
    mid = r1.astype(BF16)
    lo = (r1 - mid.astype(F32)).astype(BF16)
    return hi, mid, lo


def _attn_t_kernel(q_ref, k_ref, vt_ref, lf_ref, km_ref, vm_ref, lfm_ref, place_ref, o_ref,
                   aug_scr, augm_scr, m_scr, acc_scr, sa_scr, sb_scr):
    p = pl.program_id(1)
    qi = pl.program_id(2)
    nk = SEQ // ATT_TK

    def bias_lanes(colsum):
        out = None
        for t, term in enumerate(_split_terms(colsum)):
            d = _dot(term, place_ref[t])
            out = d if out is None else out + d
        return out.astype(BF16)

    @pl.when((p == 0) & (qi == 0))
    def _():
        r = lax.broadcasted_iota(jnp.int32, (ATT_TK, ATT_TK), 0)
        c = lax.broadcasted_iota(jnp.int32, (ATT_TK, ATT_TK), 1)
        tri = (c <= r).astype(BF16)
        carry = jnp.zeros((1, 16), F32)
        for j in range(nk):
            lf = lf_ref[:, j * ATT_TK:(j + 1) * ATT_TK]
            csum = carry
            for term in _split_terms(lf):
                csum = csum + _dot_nt(tri, term)
            aug_scr[j * ATT_TK:(j + 1) * ATT_TK, :] = bias_lanes(csum * (-LOG2E))
            carry = csum[ATT_TK - 1:ATT_TK, :]
        rm = lax.broadcasted_iota(jnp.int32, (META_ROWS, META_ROWS), 0)
        cm = lax.broadcasted_iota(jnp.int32, (META_ROWS, META_ROWS), 1)
        trim = ((cm > rm) & (cm < N_META)).astype(BF16)
        msum = jnp.zeros((META_ROWS, 16), F32)
        for term in _split_terms(lfm_ref[...]):
            msum = msum + _dot_nt(trim, term)
        augm_scr[...] = bias_lanes(msum * LOG2E)

    lane = lax.broadcasted_iota(jnp.int32, (ATT_TQ, LANES), 1)
    q_all = q_ref[...]
    zq = jnp.zeros_like(q_all)
    q_cat = []
    for hl in range(2):
        own = (lane < HEAD_DIM) if hl == 0 else (lane >= HEAD_DIM)
        h = 2 * p + hl
        bias_sel = (lane >= AUG_TERMS * h) & (lane < AUG_TERMS * h + AUG_TERMS)
        bias_ones = jnp.where(bias_sel, 1.0, 0.0).astype(BF16)
        q_cat.append(jnp.concatenate([jnp.where(own, q_all, zq), bias_ones], axis=1))
        m_scr[hl] = jnp.full((1, ATT_TQ), NEG_INF, F32)
        acc_scr[hl] = jnp.zeros((LANES, ATT_TQ), F32)

    def scores(kb, augb, lo=0):
        k_cat = jnp.concatenate([kb, augb], axis=1)
        return tuple(_dot_nt(k_cat, q_cat[hl][lo:]) for hl in range(2))

    def update(s2, vtb, mask, lo=0):
        vrow = lax.broadcasted_iota(jnp.int32, vtb.shape, 0)
        ones = jnp.ones_like(vtb)
        for hl in range(2):
            s = s2[hl]
            if mask is not None:
                s = jnp.where(mask, s, NEG_INF)
            m_prev = m_scr[hl, :, lo:]
            m_next = jnp.maximum(m_prev, jnp.max(s, axis=0, keepdims=True))
            pr = jnp.exp2(s - m_next).astype(BF16)
            alpha = jnp.exp2(m_prev - m_next)
            own = (vrow < HEAD_DIM) if hl == 0 else (vrow >= HEAD_DIM)
            acc_scr[hl, :, lo:] = acc_scr[hl, :, lo:] * alpha + _dot(jnp.where(own, vtb, ones), pr)
            m_scr[hl, :, lo:] = m_next

    def kblock(j):
        return k_ref[pl.ds(pl.multiple_of(j * ATT_TK, ATT_TK), ATT_TK), :]

    def augblock(j):
        return aug_scr[pl.ds(pl.multiple_of(j * ATT_TK, ATT_TK), ATT_TK), :]

    nfull = qi * (ATT_TQ // ATT_TK)

    def put(scr, s2):
        scr[0] = s2[0]
        scr[1] = s2[1]

    s_meta = scores(km_ref[0:N_META, :], augm_scr[0:N_META, :])
    put(sa_scr, scores(kblock(0), augblock(0)))
    update(s_meta, vm_ref[0][:, 0:N_META], None)

    def full2(i, carry):
        j = 2 * i
        put(sb_scr, scores(kblock(j + 1), augblock(j + 1)))
        update((sa_scr[0], sa_scr[1]), vt_ref[j], None)
        put(sa_scr, scores(kblock(j + 2), augblock(j + 2)))
        update((sb_scr[0], sb_scr[1]), vt_ref[j + 1], None)
        return carry

    lax.fori_loop(0, nfull // 2, full2, 0)

    srow = lax.broadcasted_iota(jnp.int32, (ATT_TK, ATT_TQ), 0)
    tcol = lax.broadcasted_iota(jnp.int32, (ATT_TK, ATT_TQ), 1)
    s_diag = [(sa_scr[0], sa_scr[1])]
    for dj in range(1, ATT_TQ // ATT_TK):
        s_diag.append(scores(kblock(nfull + dj), augblock(nfull + dj), lo=dj * ATT_TK))
    for dj in range(ATT_TQ // ATT_TK):
        lo = dj * ATT_TK
        update(s_diag[dj], vt_ref[nfull + dj], (srow <= tcol)[:, :ATT_TQ - lo], lo=lo)

    a0 = acc_scr[0]
    a1 = acc_scr[1]
    ot = jnp.concatenate([a0[0:HEAD_DIM] / a0[HEAD_DIM:], a1[HEAD_DIM:] / a1[0:HEAD_DIM]], axis=0)
    o_ref[...] = ot.T.astype(BF16)


def _attention_t(q, k, vt, lf, km, vm, lfm):
    nq = SEQ // ATT_TQ
    nk = SEQ // ATT_TK
    hh = jnp.arange(16)[:, None]
    ll = jnp.arange(LANES)[None, :]
    place = jnp.stack([((ll == AUG_TERMS * hh + t) & (hh < N_HEADS)).astype(BF16) for t in range(AUG_TERMS)])
    return pl.pallas_call(
        _attn_t_kernel,
        grid=(BATCH, N_PAIRS, nq),
        in_specs=[
            pl.BlockSpec((ATT_TQ, LANES), lambda b, p, i: (b * nq + i, p)),
            pl.BlockSpec((SEQ, LANES), lambda b, p, i: (b, p)),
            pl.BlockSpec((nk, LANES, ATT_TK), lambda b, p, i: (b, p, 0)),
            pl.BlockSpec((16, SEQ), lambda b, p, i: (0, b)),
            pl.BlockSpec((META_ROWS, LANES), lambda b, p, i: (0, p)),
            pl.BlockSpec((1, LANES, META_ROWS), lambda b, p, i: (0, p, 0)),
            pl.BlockSpec((16, META_ROWS), lambda b, p, i: (0, 0)),
            _const_spec(place.shape),
        ],
        out_specs=pl.BlockSpec((ATT_TQ, LANES), lambda b, p, i: (b * nq + i, p)),
        out_shape=jax.ShapeDtypeStruct((N_TOK, ATTN_WIDTH), BF16),
        scratch_shapes=[
            pltpu.VMEM((SEQ, LANES), BF16),
            pltpu.VMEM((META_ROWS, LANES), BF16),
            pltpu.VMEM((2, 1, ATT_TQ), F32),
            pltpu.VMEM((2, LANES, ATT_TQ), F32),
            pltpu.VMEM((2, ATT_TK, ATT_TQ), F32),
            pltpu.VMEM((2, ATT_TK, ATT_TQ), F32),
        ],
        compiler_params=pltpu.CompilerParams(
            dimension_semantics=("arbitrary", "arbitrary", "arbitrary"), vmem_limit_bytes=VMEM_LIMIT),
        name="attn",
    )(q, k, vt, lf, km, vm, lfm, place)


MERGE_TM = 512


def _merge_kernel(act_ref, o_ref, sgc_ref, sga_ref, x_ref, wpw_ref, bpw_ref, wao_ref, wout_ref, g2_ref,
                  wrt_ref, brt_ref, tri_ref,
                  h1_ref, hn_ref, route_ref, routet_ref, cnt_ref, base_scr):
    i = pl.program_id(0)

    @pl.when(i == 0)
    def _():
        base_scr[...] = jnp.zeros_like(base_scr)

    y_conv = _dot(act_ref[...], wpw_ref[...]) + bpw_ref[...]
    y_attn = _dot(o_ref[...], wao_ref[...])
    merged = sgc_ref[...].astype(F32) * y_conv + sga_ref[...].astype(F32) * y_attn
    h1 = x_ref[...] + _dot(merged.astype(BF16), wout_ref[...])
    h1_ref[...] = h1
    ms = jnp.mean(h1 * h1, axis=-1, keepdims=True)
    hn = (h1 * lax.rsqrt(ms + RMS_EPS) * g2_ref[...]).astype(BF16)
    hn_ref[...] = hn.reshape(hn.shape[0], TOK_SUB, LANES)

    logits = _dot(hn, wrt_ref[...]) + brt_ref[...]
    lane = lax.broadcasted_iota(jnp.int32, logits.shape, 1).astype(F32)
    big = float(LANES)
    is_g = lane < N_GROUPS
    gl = jnp.where(is_g, logits, NEG_INF)
    gmax = jnp.max(gl, axis=1, keepdims=True)
    gidx = jnp.min(jnp.where(gl == gmax, lane, big), axis=1, keepdims=True)
    gsum = jnp.sum(jnp.where(is_g, jnp.exp(gl - gmax), 0.0), axis=1, keepdims=True)
    g_w = 1.0 / gsum
    lo = N_GROUPS + EXPERTS_PER_GROUP * gidx
    el = jnp.where((lane >= lo) & (lane < lo + EXPERTS_PER_GROUP), logits, NEG_INF)
    t1 = jnp.max(el, axis=1, keepdims=True)
    i1 = jnp.min(jnp.where(el == t1, lane, big), axis=1, keepdims=True)
    el2 = jnp.where(lane == i1, NEG_INF, el)
    t2 = jnp.max(el2, axis=1, keepdims=True)
    i2 = jnp.min(jnp.where(el2 == t2, lane, big), axis=1, keepdims=True)
    e2 = jnp.exp(t2 - t1)
    den = 1.0 + e2
    w1 = (1.0 / den) * g_w
    w2 = (e2 / den) * g_w
    ea = i1 - N_GROUPS
    eb = i2 - N_GROUPS

    oha = (lane == ea).astype(F32)
    ohb = (lane == eb).astype(F32)
    ohs = oha + ohb
    before = _dot(tri_ref[...], ohs.astype(BF16)) + base_scr[0:1, :]
    ra = jnp.sum(before * oha, axis=1, keepdims=True)
    rb = jnp.sum(before * ohb, axis=1, keepdims=True)
    base_scr[...] = base_scr[...] + jnp.sum(ohs, axis=0, keepdims=True)
    cnt_ref[...] = base_scr[...]

    route = jnp.where(lane == 0, ea, 0.0)
    route = jnp.where(lane == 1, eb, route)
    route = jnp.where(lane == 2, w1, route)
    route = jnp.where(lane == 3, w2, route)
    route = jnp.where(lane == 4, ra, route)
    route = jnp.where(lane == 5, rb, route)
    route_ref[...] = route
    routet_ref[...] = route.T[0:8]


def _merge(act, o, sgc, sga, x2d, consts):
    tm = MERGE_TM
    row = lambda w: pl.BlockSpec((tm, w), lambda i: (i, 0))
    return pl.pallas_call(
        _merge_kernel,
        grid=(N_TOK // tm,),
        in_specs=[row(CONV_WIDTH), row(ATTN_WIDTH), row(D_MODEL), row(D_MODEL), row(D_MODEL)]
        + [_const_spec(c.shape) for c in consts],
        out_specs=[row(D_MODEL), pl.BlockSpec((tm, TOK_SUB, LANES), lambda i: (i, 0, 0)), row(LANES),
                   pl.BlockSpec((8, tm), lambda i: (0, i)), _const_spec((8, LANES))],
        out_shape=[jax.ShapeDtypeStruct((N_TOK, D_MODEL), F32),
                   jax.ShapeDtypeStruct((N_TOK, TOK_SUB, LANES), BF16),
                   jax.ShapeDtypeStruct((N_TOK, LANES), F32),
                   jax.ShapeDtypeStruct((8, N_TOK), F32),
                   jax.ShapeDtypeStruct((8, LANES), F32)],
        scratch_shapes=[pltpu.VMEM((8, LANES), F32)],
        compiler_params=pltpu.CompilerParams(dimension_semantics=("arbitrary",), vmem_limit_bytes=VMEM_LIMIT),
        name="merge",
    )(act, o, sgc, sga, x2d, *consts)


DISP_TM = 1024
DMA_UNROLL = 8


def _dest_row(code_ref, pstart_ref, idx):
    c = code_ref[idx]
    return pstart_ref[lax.shift_right_logical(c, RANK_BITS)] + (c & ((1 << RANK_BITS) - 1))


def _dispatch_kernel(c0_ref, c1_ref, pstart_ref, pend_ref, hn_ref, xin_ref, zero_scr, sem, zsem):
    i = pl.program_id(0)
    base = i * DISP_TM

    @pl.when(i == 0)
    def _():
        zero_scr[...] = jnp.zeros_like(zero_scr)

        def zcopy(e):
            start = pl.multiple_of(pend_ref[e] - ROW_BLOCK, ROW_BLOCK)
            return pltpu.make_async_copy(zero_scr, xin_ref.at[pl.ds(start, ROW_BLOCK)], zsem.at[0])

        def zstart(e, carry):
            @pl.when(pend_ref[e] > pstart_ref[e])
            def _():
                zcopy(e).start()
            return carry

        def zwait(e, carry):
            @pl.when(pend_ref[e] > pstart_ref[e])
            def _():
                zcopy(e).wait()
            return carry

        n_used = pend_ref[N_EXPERTS - 1] // ROW_BLOCK

        def tcopy(b):
            return pltpu.make_async_copy(
                zero_scr, xin_ref.at[pl.ds(pl.multiple_of(b * ROW_BLOCK, ROW_BLOCK), ROW_BLOCK)], zsem.at[0])

        def tstart(b, carry):
            tcopy(b).start()
            return carry

        def twait(b, carry):
            tcopy(b).wait()
            return carry

        lax.fori_loop(0, N_EXPERTS, zstart, 0)
        lax.fori_loop(n_used, N_BLOCKS, tstart, 0)
        lax.fori_loop(0, N_EXPERTS, zwait, 0)
        lax.fori_loop(n_used, N_BLOCKS, twait, 0)

    def start(r, carry):
        pltpu.make_async_copy(hn_ref.at[r], xin_ref.at[_dest_row(c0_ref, pstart_ref, base + r)], sem.at[0]).start()
        pltpu.make_async_copy(hn_ref.at[r], xin_ref.at[_dest_row(c1_ref, pstart_ref, base + r)], sem.at[1]).start(
            priority=1)
        return carry

    lax.fori_loop(0, DISP_TM, start, 0, unroll=DMA_UNROLL)
    pltpu.make_async_copy(hn_ref, xin_ref.at[pl.ds(0, DISP_TM)], sem.at[0]).wait()
    pltpu.make_async_copy(hn_ref, xin_ref.at[pl.ds(0, DISP_TM)], sem.at[1]).wait()


def _dispatch(code0, code1, pstart, pend, hn):
    grid_spec = pltpu.PrefetchScalarGridSpec(
        num_scalar_prefetch=4,
        grid=(N_TOK // DISP_TM,),
        in_specs=[pl.BlockSpec((DISP_TM, TOK_SUB, LANES), lambda i, *_: (i, 0, 0))],
        out_specs=pl.BlockSpec(memory_space=pl.ANY),
        scratch_shapes=[pltpu.VMEM((ROW_BLOCK, TOK_SUB, LANES), BF16),
                        pltpu.SemaphoreType.DMA((2,)), pltpu.SemaphoreType.DMA((1,))],
    )
    return pl.pallas_call(
        _dispatch_kernel,
        grid_spec=grid_spec,
        out_shape=jax.ShapeDtypeStruct((P_ROWS, TOK_SUB, LANES), BF16),
        compiler_params=pltpu.CompilerParams(dimension_semantics=("arbitrary",), vmem_limit_bytes=VMEM_LIMIT),
        name="dispatch",
    )(code0, code1, pstart, pend, hn)


W_SLOTS = 3


def _expert_kernel(nu_ref, first_ref, ord_ref, used_ref, nue_ref, x_ref, wg_hbm, wu_hbm, wd_hbm, y_ref,
                   wg_f, wu_f, wd_f, wg_b, wu_b, wd_b, sem):
    i = pl.program_id(0)

    def fetch(k):
        e = used_ref[k]
        slot = k % W_SLOTS
        return (pltpu.make_async_copy(wg_hbm.at[e], wg_f.at[slot], sem.at[slot, 0]),
                pltpu.make_async_copy(wu_hbm.at[e], wu_f.at[slot], sem.at[slot, 1]),
                pltpu.make_async_copy(wd_hbm.at[e], wd_f.at[slot], sem.at[slot, 2]))

    @pl.when(i == 0)
    def _():
        for k in range(W_SLOTS - 1):
            @pl.when(k < nue_ref[0])
            def _():
                for c in fetch(k):
                    c.start()

    @pl.when(i < nu_ref[0])
    def _():
        @pl.when(first_ref[i] == 1)
        def _():
            k = ord_ref[i]
            for c in fetch(k):
                c.wait()

            @pl.when(k + (W_SLOTS - 1) < nue_ref[0])
            def _():
                for c in fetch(k + (W_SLOTS - 1)):
                    c.start()

            slot = k % W_SLOTS
            wg_b[...] = wg_f[slot].astype(BF16)
            wu_b[...] = wu_f[slot].astype(BF16)
            wd_b[...] = wd_f[slot].astype(BF16)

        x = x_ref[...].reshape(ROW_BLOCK, D_MODEL)
        g = _dot(x, wg_b[...])
        u = _dot(x, wu_b[...])
        hmid = (g * _sigmoid(g) * u).astype(BF16)
        y = _dot(hmid, wd_b[...]).astype(BF16)
        y_ref[...] = y.reshape(ROW_BLOCK, TOK_SUB, LANES)

    @pl.when(i >= nu_ref[0])
    def _():
        y_ref[...] = jnp.zeros_like(y_ref)


def _experts(n_used, blk_first, blk_ord, used_e, n_used_e, xin, w_gate, w_up, w_down):
    def xmap(i, nu, *_):
        return (jnp.minimum(i, nu[0] - 1), 0, 0)

    grid_spec = pltpu.PrefetchScalarGridSpec(
        num_scalar_prefetch=5,
        grid=(N_BLOCKS,),
        in_specs=[pl.BlockSpec((ROW_BLOCK, TOK_SUB, LANES), xmap),
                  pl.BlockSpec(memory_space=pl.ANY),
                  pl.BlockSpec(memory_space=pl.ANY),
                  pl.BlockSpec(memory_space=pl.ANY)],
        out_specs=pl.BlockSpec((ROW_BLOCK, TOK_SUB, LANES), lambda i, *_: (i, 0, 0)),
        scratch_shapes=[pltpu.VMEM((W_SLOTS, D_MODEL, D_EXPERT), F32),
                        pltpu.VMEM((W_SLOTS, D_MODEL, D_EXPERT), F32),
                        pltpu.VMEM((W_SLOTS, D_EXPERT, D_MODEL), F32),
                        pltpu.VMEM((D_MODEL, D_EXPERT), BF16),
                        pltpu.VMEM((D_MODEL, D_EXPERT), BF16),
                        pltpu.VMEM((D_EXPERT, D_MODEL), BF16),
                        pltpu.SemaphoreType.DMA((W_SLOTS, 3))],
    )
    return pl.pallas_call(
        _expert_kernel,
        grid_spec=grid_spec,
        out_shape=jax.ShapeDtypeStruct((P_ROWS, TOK_SUB, LANES), BF16),
        compiler_params=pltpu.CompilerParams(dimension_semantics=("arbitrary",), vmem_limit_bytes=VMEM_LIMIT),
        name="experts",
    )(n_used, blk_first, blk_ord, used_e, n_used_e, xin, w_gate, w_up, w_down)


COMB_TM = 256


def _combine_kernel(c0_ref, c1_ref, pstart_ref, h1_ref, route_ref, yb_ref, out_ref, g_scr, sem):
    i = pl.program_id(0)
    n = pl.num_programs(0)

    def issue(tile, slot):
        base = tile * COMB_TM

        def start(r, carry):
            pltpu.make_async_copy(yb_ref.at[_dest_row(c0_ref, pstart_ref, base + r)], g_scr.at[slot, 0, r],
                                  sem.at[slot, 0]).start()
            pltpu.make_async_copy(yb_ref.at[_dest_row(c1_ref, pstart_ref, base + r)], g_scr.at[slot, 1, r],
                                  sem.at[slot, 1]).start(priority=1)
            return carry

        lax.fori_loop(0, COMB_TM, start, 0, unroll=DMA_UNROLL)

    @pl.when(i == 0)
    def _():
        issue(0, 0)

    slot = i % 2

    @pl.when(i + 1 < n)
    def _():
        issue(i + 1, 1 - slot)

    for k in range(2):
        pltpu.make_async_copy(yb_ref.at[pl.ds(0, COMB_TM)], g_scr.at[slot, k], sem.at[slot, k]).wait()

    route = route_ref[...]
    w0 = route[:, 2:3]
    w1 = route[:, 3:4]
    g0 = g_scr[slot, 0].reshape(COMB_TM, D_MODEL).astype(F32)
    g1 = g_scr[slot, 1].reshape(COMB_TM, D_MODEL).astype(F32)
    out_ref[...] = h1_ref[...] + (g0 * w0 + g1 * w1)


def _combine(code0, code1, pstart, h1, route, yb):
    grid_spec = pltpu.PrefetchScalarGridSpec(
        num_scalar_prefetch=3,
        grid=(N_TOK // COMB_TM,),
        in_specs=[pl.BlockSpec((COMB_TM, D_MODEL), lambda i, *_: (i, 0)),
                  pl.BlockSpec((COMB_TM, LANES), lambda i, *_: (i, 0)),
                  pl.BlockSpec(memory_space=pl.ANY)],
        out_specs=pl.BlockSpec((COMB_TM, D_MODEL), lambda i, *_: (i, 0)),
        scratch_shapes=[pltpu.VMEM((2, 2, COMB_TM, TOK_SUB, LANES), BF16),
                        pltpu.SemaphoreType.DMA((2, 2))],
    )
    return pl.pallas_call(
        _combine_kernel,
        grid_spec=grid_spec,
        out_shape=jax.ShapeDtypeStruct((N_TOK, D_MODEL), F32),
        compiler_params=pltpu.CompilerParams(dimension_semantics=("arbitrary",), vmem_limit_bytes=VMEM_LIMIT),
        name="combine",
    )(code0, code1, pstart, h1, route, yb)


def kernel(x, meta, norm1_g, w_in, b_in, dw_w, dw_b, conv_ln_g, conv_ln_b, w_conv_out, b_conv_out,
           q_norm_g, k_norm_g, b_forget, w_attn_out, w_out, norm2_g, w_group, b_group, w_router,
           b_router, w_gate, w_up, w_down):
    l = 0
    row = lambda v: v.reshape(1, -1).astype(F32)

    c_u, c_q, c_f = 2 * CONV_WIDTH, 2 * CONV_WIDTH, 2 * CONV_WIDTH + 3 * ATTN_WIDTH
    c_gc = c_f + N_HEADS
    c_ga = c_gc + D_MODEL
    wi, bi = w_in[l], b_in[l]
    wu, bu = wi[:, :c_u].astype(BF16), row(bi[:c_u])
    wqkv, bqkv = wi[:, c_q:c_f].astype(BF16), row(bi[c_q:c_f])
    wft = jnp.zeros((16, D_MODEL), F32).at[:N_HEADS].set(wi[:, c_f:c_gc].T).astype(BF16)
    bft = jnp.zeros((16, 1), F32).at[:N_HEADS, 0].set(bi[c_f:c_gc] + b_forget[l])
    wgc, bgc = wi[:, c_gc:c_ga].astype(BF16), row(bi[c_gc:c_ga])
    wga, bga = wi[:, c_ga:].astype(BF16), row(bi[c_ga:])
    qg = row(jnp.tile(q_norm_g[l], N_HEADS) * (HEAD_DIM ** -0.5 * LOG2E))
    kg = row(jnp.tile(k_norm_g[l], N_HEADS))
    head_of_col = jnp.arange(ATTN_WIDTH) // HEAD_DIM
    hsum = (head_of_col[:, None] == jnp.arange(LANES)[None, :]).astype(BF16)
    hexp_half = (jnp.arange(LANES)[:, None] == head_of_col[None, :]).astype(BF16)
    hexp = jnp.concatenate([hexp_half, hexp_half], axis=0)
    proj_consts = (row(norm1_g[l]), wu, bu, wqkv, bqkv, wft, bft, wgc, bgc, wga, bga, qg, kg, hsum, hexp)

    x2d = x.reshape(N_TOK, D_MODEL)
    meta_pad = jnp.zeros((META_ROWS, D_MODEL), F32).at[:N_META].set(meta.astype(F32))
    a, q, k, v, lf, sgc, sga = _proj(x2d, proj_consts, 512)
    a_m, _, k_m, v_m, lf_m, _, _ = _proj(meta_pad, proj_consts, META_ROWS)

    act = _conv(a.reshape(BATCH, SEQ, CONV_WIDTH), a_m, dw_w[l].astype(F32), row(dw_b[l]),
                row(conv_ln_g[l]), row(conv_ln_b[l])).reshape(N_TOK, CONV_WIDTH)

    o = _attention_t(q, k, v, lf, k_m, v_m, lf_m)

    wrt = jnp.zeros((D_MODEL, LANES), F32).at[:, :N_GROUPS].set(w_group[l])
    wrt = wrt.at[:, N_GROUPS:N_GROUPS + N_EXPERTS].set(w_router[l]).astype(BF16)
    brt = jnp.zeros((1, LANES), F32).at[0, :N_GROUPS].set(b_group[l])
    brt = brt.at[0, N_GROUPS:N_GROUPS + N_EXPERTS].set(b_router[l])
    tri = jnp.tril(jnp.ones((MERGE_TM, MERGE_TM), F32), -1).astype(BF16)
    merge_consts = (w_conv_out[l].astype(BF16), row(b_conv_out[l]), w_attn_out[l].astype(BF16),
                    w_out[l].astype(BF16), row(norm2_g[l]), wrt, brt, tri)
    h1, hn, route, route_t, cnt = _merge(act, o, sgc, sga, x2d, merge_consts)

    rt = route_t.astype(jnp.int32)
    code0 = rt[0] * (1 << RANK_BITS) + rt[4]
    code1 = rt[1] * (1 << RANK_BITS) + rt[5]
    counts = cnt[0, :N_EXPERTS].astype(jnp.int32)
    padded = (counts + ROW_BLOCK - 1) // ROW_BLOCK * ROW_BLOCK
    pend = jnp.cumsum(padded).astype(jnp.int32)
    pstart = pend - padded
    blk_row = jnp.arange(N_BLOCKS, dtype=jnp.int32) * ROW_BLOCK
    n_used = pend[-1:] // ROW_BLOCK
    nz = counts > 0
    before = (pend[None, :] <= blk_row[:, None]) & nz[None, :]
    blk_ord = jnp.sum(before.astype(jnp.int32), axis=1)
    blk_first = jnp.any((pstart[None, :] == blk_row[:, None]) & nz[None, :], axis=1).astype(jnp.int32)
    e_ord = jnp.cumsum(nz.astype(jnp.int32)) - 1
    e_ids = jnp.arange(N_EXPERTS, dtype=jnp.int32)
    used_e = jnp.sum(jnp.where((e_ord[None, :] == e_ids[:, None]) & nz[None, :], e_ids[None, :], 0), axis=1)
    n_used_e = jnp.sum(nz.astype(jnp.int32)).reshape(1)

    xin = _dispatch(code0, code1, pstart, pend, hn)
    yb = _experts(n_used, blk_first, blk_ord, used_e.astype(jnp.int32), n_used_e, xin,
                  w_gate[l], w_up[l], w_down[l])
    out = _combine(code0, code1, pstart, h1, route, yb)
    return out.reshape(BATCH, SEQ, D_MODEL)
```

```python
import functools

import jax
import jax.numpy as jnp
from jax import lax
from jax.experimental import pallas as pl
from jax.experimental.pallas import tpu as pltpu

D_MODEL = 1024
BATCH = 8
SEQ = 2048
N_META = 16
CONV_WIDTH = 512
CONV_K = 31
N_HEADS = 8
HEAD_DIM = 64
ATTN_WIDTH = N_HEADS * HEAD_DIM
N_GROUPS = 8
EXPERTS_PER_GROUP = 8
N_EXPERTS = N_GROUPS * EXPERTS_PER_GROUP
D_EXPERT = 256
RMS_EPS = 1e-6
LN_EPS = 1e-5

N_TOK = BATCH * SEQ
LANES = 128
META_ROWS = 128
ROW_BLOCK = 512
N_ASSIGN = 2 * N_TOK
N_BLOCKS = N_ASSIGN // ROW_BLOCK + N_EXPERTS
P_ROWS = N_BLOCKS * ROW_BLOCK
TOK_SUB = D_MODEL // LANES
RANK_BITS = 16
ATT_TQ = 512
ATT_TK = 256
N_PAIRS = N_HEADS // 2
assert (ATT_TQ // ATT_TK) % 2 == 0
VMEM_LIMIT = 56 * 1024 * 1024

F32 = jnp.float32
BF16 = jnp.bfloat16
NEG_INF = float("-inf")
LOG2E = 1.4426950408889634


def _dot(a, b):
    return jnp.dot(a, b, preferred_element_type=F32)


def _dot_nt(a, b):
    return lax.dot_general(a, b, (((1,), (1,)), ((), ())), preferred_element_type=F32)


def _sigmoid(x):
    return 1.0 / (1.0 + jnp.exp(-x))


def _const_spec(shape):
    nd = len(shape)
    return pl.BlockSpec(shape, lambda *_: (0,) * nd)


def _head_rms(t, hsum_ref, hexp_ref, gain):
    ss = _dot((t * t).astype(BF16), hsum_ref[...])
    inv = lax.rsqrt(ss * (1.0 / HEAD_DIM) + RMS_EPS)
    hi = inv.astype(BF16)
    lo = (inv - hi.astype(F32)).astype(BF16)
    invb = _dot(jnp.concatenate([hi, lo], axis=1), hexp_ref[...])
    return t * invb * gain


def _proj_kernel(x_ref, g1_ref, wu_ref, bu_ref, wqkv_ref, bqkv_ref, wft_ref, bft_ref, wgc_ref, bgc_ref,
                 wga_ref, bga_ref, qg_ref, kg_ref, hsum_ref, hexp_ref,
                 a_ref, q_ref, k_ref, v_ref, lf_ref, sgc_ref, sga_ref):
    x = x_ref[...]
    ms = jnp.mean(x * x, axis=-1, keepdims=True)
    xn = (x * lax.rsqrt(ms + RMS_EPS) * g1_ref[...]).astype(BF16)

    u = _dot(xn, wu_ref[...]) + bu_ref[...]
    a_ref[...] = (u[:, :CONV_WIDTH] * _sigmoid(u[:, CONV_WIDTH:])).astype(BF16)

    qkv = _dot(xn, wqkv_ref[...]) + bqkv_ref[...]
    q_ref[...] = _head_rms(qkv[:, :ATTN_WIDTH], hsum_ref, hexp_ref, qg_ref[...]).astype(BF16)
    k_ref[...] = _head_rms(qkv[:, ATTN_WIDTH:2 * ATTN_WIDTH], hsum_ref, hexp_ref, kg_ref[...]).astype(BF16)
    v = qkv[:, 2 * ATTN_WIDTH:]
    vb = v_ref.shape[2]
    for c in range(v_ref.shape[0]):
        v_ref[c] = v[c * vb:(c + 1) * vb, :].T.astype(BF16)

    f = _dot_nt(wft_ref[...], xn) + bft_ref[...]
    lf_ref[...] = jnp.minimum(f, 0.0) - jnp.log(1.0 + jnp.exp(-jnp.abs(f)))

    sgc_ref[...] = _sigmoid(_dot(xn, wgc_ref[...]) + bgc_ref[...]).astype(BF16)
    sga_ref[...] = _sigmoid(_dot(xn, wga_ref[...]) + bga_ref[...]).astype(BF16)


def _proj(x2d, consts, tm):
    n = x2d.shape[0]
    row = lambda w: pl.BlockSpec((tm, w), lambda i: (i, 0))
    in_specs = [row(D_MODEL)] + [_const_spec(c.shape) for c in consts]
    vb = min(tm, ATT_TK)
    out_shape = [
        jax.ShapeDtypeStruct((n, CONV_WIDTH), BF16),
        jax.ShapeDtypeStruct((n, ATTN_WIDTH), BF16),
        jax.ShapeDtypeStruct((n, ATTN_WIDTH), BF16),
        jax.ShapeDtypeStruct((n // vb, ATTN_WIDTH, vb), BF16),
        jax.ShapeDtypeStruct((16, n), F32),
        jax.ShapeDtypeStruct((n, D_MODEL), BF16),
        jax.ShapeDtypeStruct((n, D_MODEL), BF16),
    ]
    out_specs = [row(CONV_WIDTH), row(ATTN_WIDTH), row(ATTN_WIDTH),
                 pl.BlockSpec((tm // vb, ATTN_WIDTH, vb), lambda i: (i, 0, 0)),
                 pl.BlockSpec((16, tm), lambda i: (0, i)), row(D_MODEL), row(D_MODEL)]
    return pl.pallas_call(
        _proj_kernel,
        grid=(n // tm,),
        in_specs=in_specs,
        out_specs=out_specs,
        out_shape=out_shape,
        compiler_params=pltpu.CompilerParams(dimension_semantics=("arbitrary",), vmem_limit_bytes=VMEM_LIMIT),
        name="proj",
    )(x2d, *consts)


CONV_CHUNK = 64
CONV_PAD = 32
CONV_WIN = CONV_CHUNK + 32
CONV_SHROWS = CONV_WIN


def _conv_kernel(a_ref, am_ref, shift_ref, dw_ref, dwb_ref, lng_ref, lnb_ref, o_ref, pad_scr, sha_scr, shb_scr):
    pad_scr[0:N_META, :] = jnp.zeros((N_META, CONV_WIDTH), BF16)
    pad_scr[N_META:CONV_PAD, :] = am_ref[0:N_META, :]
    pad_scr[CONV_PAD:CONV_PAD + SEQ, :] = a_ref[0]

    def shifted_copies(ci):
        r0 = pl.multiple_of(ci * CONV_CHUNK, CONV_CHUNK)
        return _dot(shift_ref[...], pad_scr[pl.ds(r0, CONV_WIN), :])

    n_chunks = SEQ // CONV_CHUNK
    sha_scr[...] = shifted_copies(0)

    def finish(ci, cur_scr):
        r0 = pl.multiple_of(ci * CONV_CHUNK, CONV_CHUNK)
        acc = jnp.zeros((CONV_CHUNK, CONV_WIDTH), F32) + dwb_ref[...]
        for j in range(CONV_K):
            off = j + 2
            b = off % 8
            a0 = b * CONV_SHROWS + (off - b)
            acc = acc + cur_scr[a0:a0 + CONV_CHUNK, :] * dw_ref[j:j + 1, :]
        mu = jnp.mean(acc, axis=-1, keepdims=True)
        d = acc - mu
        var = jnp.mean(d * d, axis=-1, keepdims=True)
        y = d * lax.rsqrt(var + LN_EPS) * lng_ref[...] + lnb_ref[...]
        o_ref[0, pl.ds(r0, CONV_CHUNK), :] = (y * _sigmoid(y)).astype(BF16)

    def chunk_pair(i, carry):
        ci = 2 * i
        shb_scr[...] = shifted_copies(ci + 1)
        finish(ci, sha_scr)
        sha_scr[...] = shifted_copies(jnp.minimum(ci + 2, n_chunks - 1))
        finish(ci + 1, shb_scr)
        return carry

    lax.fori_loop(0, n_chunks // 2, chunk_pair, 0)


def _conv(a3, a_meta, dw_w, dw_b, ln_g, ln_b):
    rr = jnp.arange(8 * CONV_SHROWS)
    shift = ((rr % CONV_SHROWS + rr // CONV_SHROWS)[:, None] == jnp.arange(CONV_WIN)[None, :]).astype(BF16)
    return pl.pallas_call(
        _conv_kernel,
        grid=(BATCH,),
        in_specs=[pl.BlockSpec((1, SEQ, CONV_WIDTH), lambda b: (b, 0, 0)),
                  _const_spec(a_meta.shape), _const_spec(shift.shape), _const_spec(dw_w.shape),
                  _const_spec(dw_b.shape), _const_spec(ln_g.shape), _const_spec(ln_b.shape)],
        out_specs=pl.BlockSpec((1, SEQ, CONV_WIDTH), lambda b: (b, 0, 0)),
        out_shape=jax.ShapeDtypeStruct((BATCH, SEQ, CONV_WIDTH), BF16),
        scratch_shapes=[pltpu.VMEM((CONV_PAD + SEQ, CONV_WIDTH), BF16),
                        pltpu.VMEM((8 * CONV_SHROWS, CONV_WIDTH), F32),
                        pltpu.VMEM((8 * CONV_SHROWS, CONV_WIDTH), F32)],
        compiler_params=pltpu.CompilerParams(dimension_semantics=("arbitrary",), vmem_limit_bytes=VMEM_LIMIT),
        name="conv",
    )(a3, a_meta, shift, dw_w, dw_b, ln_g, ln_b)


def _split3(x):
    hi = x.astype(BF16)
    r1 = x - hi.astype(F32)
    mid = r1.astype(BF16)
    lo = (r1 - mid.astype(F32)).astype(BF16)
    return jnp.concatenate([hi, mid, lo], axis=0)


def _attn_kernel(q_ref, k_ref, v_ref, lf_ref, km_ref, vm_ref, lfm_ref, o_ref,
                 c_scr, cm_scr, m_scr, acc_scr, sa_scr, sb_scr):
    p = pl.program_id(1)
    qi = pl.program_id(2)
    nk = SEQ // ATT_TK

    @pl.when((p == 0) & (qi == 0))
    def _():
        r = lax.broadcasted_iota(jnp.int32, (ATT_TK, ATT_TK), 0)
        c = lax.broadcasted_iota(jnp.int32, (ATT_TK, ATT_TK), 1)
        tri = (r <= c).astype(BF16)
        carry = jnp.zeros((N_HEADS, 1), F32)
        for j in range(nk):
            parts = _dot(_split3(lf_ref[0:N_HEADS, j * ATT_TK:(j + 1) * ATT_TK]), tri)
            cj = parts[0:8] + parts[8:16] + parts[16:24] + carry
            c_scr[j] = cj
            carry = cj[:, ATT_TK - 1:ATT_TK]
        rm = lax.broadcasted_iota(jnp.int32, (LANES, LANES), 0)
        cmx = lax.broadcasted_iota(jnp.int32, (LANES, LANES), 1)
        trim = ((rm > cmx) & (rm < N_META)).astype(BF16)
        pm = _dot(_split3(lfm_ref[0:N_HEADS, :]), trim)
        cm_scr[...] = -(pm[0:8] + pm[8:16] + pm[16:24])

    lane = lax.broadcasted_iota(jnp.int32, (ATT_TQ, LANES), 1)
    q_all = q_ref[...]
    zq = jnp.zeros_like(q_all)
    qm = (jnp.where(lane < HEAD_DIM, q_all, zq), jnp.where(lane >= HEAD_DIM, q_all, zq))
    for hl in range(2):
        m_scr[hl] = jnp.full((ATT_TQ, LANES), NEG_INF, F32)
        acc_scr[hl] = jnp.zeros((ATT_TQ, LANES), F32)

    def scores(kb, crows):
        return tuple(_dot_nt(qm[hl], kb) - crows[hl] for hl in range(2))

    def update(s2, vb, mask):
        vlane = lax.broadcasted_iota(jnp.int32, vb.shape, 1)
        ones = jnp.ones_like(vb)
        for hl in range(2):
            s = s2[hl]
            if mask is not None:
                s = jnp.where(mask, s, NEG_INF)
            m_prev = m_scr[hl]
            m_next = jnp.maximum(m_prev, jnp.max(s, axis=1, keepdims=True))
            reps = s.shape[1] // LANES
            m_wide = m_next if reps == 1 else jnp.concatenate([m_next] * reps, axis=1)
            pr = jnp.exp(s - m_wide).astype(BF16)
            alpha = jnp.exp(m_prev - m_next)
            own = (vlane < HEAD_DIM) if hl == 0 else (vlane >= HEAD_DIM)
            acc_scr[hl] = acc_scr[hl] * alpha + _dot(pr, jnp.where(own, vb, ones))
            m_scr[hl] = m_next

    def crows_at(j):
        return (c_scr[j, pl.ds(2 * p, 1), :], c_scr[j, pl.ds(2 * p + 1, 1), :])

    def kblock(j):
        return k_ref[pl.ds(pl.multiple_of(j * ATT_TK, ATT_TK), ATT_TK), :]

    def vblock(j):
        return v_ref[pl.ds(pl.multiple_of(j * ATT_TK, ATT_TK), ATT_TK), :]

    update(scores(km_ref[...], (cm_scr[pl.ds(2 * p, 1), :], cm_scr[pl.ds(2 * p + 1, 1), :])), vm_ref[...],
           lane < N_META)

    nfull = qi * (ATT_TQ // ATT_TK)

    def put(scr, s2):
        scr[0] = s2[0]
        scr[1] = s2[1]

    put(sa_scr, scores(kblock(0), crows_at(0)))

    def full2(i, carry):
        j = 2 * i
        put(sb_scr, scores(kblock(j + 1), crows_at(j + 1)))
        update((sa_scr[0], sa_scr[1]), vblock(j), None)
        put(sa_scr, scores(kblock(j + 2), crows_at(j + 2)))
        update((sb_scr[0], sb_scr[1]), vblock(j + 1), None)
        return carry

    lax.fori_loop(0, nfull // 2, full2, 0)

    rows = lax.broadcasted_iota(jnp.int32, (ATT_TQ, ATT_TK), 0)
    cols = lax.broadcasted_iota(jnp.int32, (ATT_TQ, ATT_TK), 1)
    s_diag = [(sa_scr[0], sa_scr[1])]
    for dj in range(1, ATT_TQ // ATT_TK):
        s_diag.append(scores(kblock(nfull + dj), crows_at(nfull + dj)))
    for dj in range(ATT_TQ // ATT_TK):
        update(s_diag[dj], vblock(nfull + dj), cols + dj * ATT_TK <= rows)

    a0 = acc_scr[0]
    a1 = acc_scr[1]
    o0 = a0 / pltpu.roll(a0, HEAD_DIM, axis=1)
    o1 = a1 / pltpu.roll(a1, HEAD_DIM, axis=1)
    o_ref[...] = jnp.where(lane < HEAD_DIM, o0, o1).astype(BF16)


def _attention(q, k, v, lf, km, vm, lfm):
    nq = SEQ // ATT_TQ
    return pl.pallas_call(
        _attn_kernel,
        grid=(BATCH, N_PAIRS, nq),
        in_specs=[
            pl.BlockSpec((ATT_TQ, LANES), lambda b, p, i: (b * nq + i, p)),
            pl.BlockSpec((SEQ, LANES), lambda b, p, i: (b, p)),
            pl.BlockSpec((SEQ, LANES), lambda b, p, i: (b, p)),
            pl.BlockSpec((16, SEQ), lambda b, p, i: (0, b)),
            pl.BlockSpec((META_ROWS, LANES), lambda b, p, i: (0, p)),
            pl.BlockSpec((META_ROWS, LANES), lambda b, p, i: (0, p)),
            pl.BlockSpec((16, META_ROWS), lambda b, p, i: (0, 0)),
        ],
        out_specs=pl.BlockSpec((ATT_TQ, LANES), lambda b, p, i: (b * nq + i, p)),
        out_shape=jax.ShapeDtypeStruct((N_TOK, ATTN_WIDTH), BF16),
        scratch_shapes=[
            pltpu.VMEM((SEQ // ATT_TK, N_HEADS, ATT_TK), F32),
            pltpu.VMEM((N_HEADS, LANES), F32),
            pltpu.VMEM((2, ATT_TQ, LANES), F32),
            pltpu.VMEM((2, ATT_TQ, LANES), F32),
            pltpu.VMEM((2, ATT_TQ, ATT_TK), F32),
            pltpu.VMEM((2, ATT_TQ, ATT_TK), F32),
        ],
        compiler_params=pltpu.CompilerParams(
            dimension_semantics=("arbitrary", "arbitrary", "arbitrary"), vmem_limit_bytes=VMEM_LIMIT),
        name="attn",
    )(q, k, v, lf, km, vm, lfm)


AUG_TERMS = 3


def _split_terms(x):
    hi = x.astype(BF16)
    r1 = x - hi.astype(F32)
    mid = r1.astype(BF16)
    lo = (r1 - mid.astype(F32)).astype(BF16)
    return hi, mid, lo


def _attn_t_kernel(q_ref, k_ref, vt_ref, lf_ref, km_ref, vm_ref, lfm_ref, place_ref, o_ref,
                   aug_scr, augm_scr, m_scr, acc_scr, sa_scr, sb_scr):
    p = pl.program_id(1)
    qi = pl.program_id(2)
    nk = SEQ // ATT_TK

    def bias_lanes(colsum):
        out = None
        for t, term in enumerate(_split_terms(colsum)):
            d = _dot(term, place_ref[t])
            out = d if out is None else out + d
        return out.astype(BF16)

    @pl.when((p == 0) & (qi == 0))
    def _():
        r = lax.broadcasted_iota(jnp.int32, (ATT_TK, ATT_TK), 0)
        c = lax.broadcasted_iota(jnp.int32, (ATT_TK, ATT_TK), 1)
        tri = (c <= r).astype(BF16)
        carry = jnp.zeros((1, 16), F32)
        for j in range(nk):
            lf = lf_ref[:, j * ATT_TK:(j + 1) * ATT_TK]
            csum = carry
            for term in _split_terms(lf):
                csum = csum + _dot_nt(tri, term)
            aug_scr[j * ATT_TK:(j + 1) * ATT_TK, :] = bias_lanes(csum * (-LOG2E))
            carry = csum[ATT_TK - 1:ATT_TK, :]
        rm = lax.broadcasted_iota(jnp.int32, (META_ROWS, META_ROWS), 0)
        cm = lax.broadcasted_iota(jnp.int32, (META_ROWS, META_ROWS), 1)
        trim = ((cm > rm) & (cm < N_META)).astype(BF16)
        msum = jnp.zeros((META_ROWS, 16), F32)
        for term in _split_terms(lfm_ref[...]):
            msum = msum + _dot_nt(trim, term)
        augm_scr[...] = bias_lanes(msum * LOG2E)

    lane = lax.broadcasted_iota(jnp.int32, (ATT_TQ, LANES), 1)
    q_all = q_ref[...]
    zq = jnp.zeros_like(q_all)
    q_cat = []
    for hl in range(2):
        own = (lane < HEAD_DIM) if hl == 0 else (lane >= HEAD_DIM)
        h = 2 * p + hl
        bias_sel = (lane >= AUG_TERMS * h) & (lane < AUG_TERMS * h + AUG_TERMS)
        bias_ones = jnp.where(bias_sel, 1.0, 0.0).astype(BF16)
        q_cat.append(jnp.concatenate([jnp.where(own, q_all, zq), bias_ones], axis=1))
        m_scr[hl] = jnp.full((1, ATT_TQ), NEG_INF, F32)
        acc_scr[hl] = jnp.zeros((LANES, ATT_TQ), F32)

    def scores(kb, augb, lo=0):
        k_cat = jnp.concatenate([kb, augb], axis=1)
        return tuple(_dot_nt(k_cat, q_cat[hl][lo:]) for hl in range(2))

    def update(s2, vtb, mask, lo=0):
        vrow = lax.broadcasted_iota(jnp.int32, vtb.shape, 0)
        ones = jnp.ones_like(vtb)
        for hl in range(2):
            s = s2[hl]
            if mask is not None:
                s = jnp.where(mask, s, NEG_INF)
            m_prev = m_scr[hl, :, lo:]
            m_next = jnp.maximum(m_prev, jnp.max(s, axis=0, keepdims=True))
            pr = jnp.exp2(s - m_next).astype(BF16)
            alpha = jnp.exp2(m_prev - m_next)
            own = (vrow < HEAD_DIM) if hl == 0 else (vrow >= HEAD_DIM)
            acc_scr[hl, :, lo:] = acc_scr[hl, :, lo:] * alpha + _dot(jnp.where(own, vtb, ones), pr)
            m_scr[hl, :, lo:] = m_next

    def kblock(j):
        return k_ref[pl.ds(pl.multiple_of(j * ATT_TK, ATT_TK), ATT_TK), :]

    def augblock(j):
        return aug_scr[pl.ds(pl.multiple_of(j * ATT_TK, ATT_TK), ATT_TK), :]

    nfull = qi * (ATT_TQ // ATT_TK)

    def put(scr, s2):
        scr[0] = s2[0]
        scr[1] = s2[1]

    s_meta = scores(km_ref[0:N_META, :], augm_scr[0:N_META, :])
    put(sa_scr, scores(kblock(0), augblock(0)))
    update(s_meta, vm_ref[0][:, 0:N_META], None)

    def full2(i, carry):
        j = 2 * i
        put(sb_scr, scores(kblock(j + 1), augblock(j + 1)))
        update((sa_scr[0], sa_scr[1]), vt_ref[j], None)
        put(sa_scr, scores(kblock(j + 2), augblock(j + 2)))
        update((sb_scr[0], sb_scr[1]), vt_ref[j + 1], None)
        return carry

    lax.fori_loop(0, nfull // 2, full2, 0)

    srow = lax.broadcasted_iota(jnp.int32, (ATT_TK, ATT_TQ), 0)
    tcol = lax.broadcasted_iota(jnp.int32, (ATT_TK, ATT_TQ), 1)
    s_diag = [(sa_scr[0], sa_scr[1])]
    for dj in range(1, ATT_TQ // ATT_TK):
        s_diag.append(scores(kblock(nfull + dj), augblock(nfull + dj), lo=dj * ATT_TK))
    for dj in range(ATT_TQ // ATT_TK):
        lo = dj * ATT_TK
        update(s_diag[dj], vt_ref[nfull + dj], (srow <= tcol)[:, :ATT_TQ - lo], lo=lo)

    a0 = acc_scr[0]
    a1 = acc_scr[1]
    ot = jnp.concatenate([a0[0:HEAD_DIM] / a0[HEAD_DIM:], a1[HEAD_DIM:] / a1[0:HEAD_DIM]], axis=0)
    o_ref[...] = ot.T.astype(BF16)


def _attention_t(q, k, vt, lf, km, vm, lfm):
    nq = SEQ // ATT_TQ
    nk = SEQ // ATT_TK
    hh = jnp.arange(16)[:, None]
    ll = jnp.arange(LANES)[None, :]
    place = jnp.stack([((ll == AUG_TERMS * hh + t) & (hh < N_HEADS)).astype(BF16) for t in range(AUG_TERMS)])
    return pl.pallas_call(
        _attn_t_kernel,
        grid=(BATCH, N_PAIRS, nq),
        in_specs=[
            pl.BlockSpec((ATT_TQ, LANES), lambda b, p, i: (b * nq + i, p)),
            pl.BlockSpec((SEQ, LANES), lambda b, p, i: (b, p)),
            pl.BlockSpec((nk, LANES, ATT_TK), lambda b, p, i: (b, p, 0)),
            pl.BlockSpec((16, SEQ), lambda b, p, i: (0, b)),
            pl.BlockSpec((META_ROWS, LANES), lambda b, p, i: (0, p)),
            pl.BlockSpec((1, LANES, META_ROWS), lambda b, p, i: (0, p, 0)),
            pl.BlockSpec((16, META_ROWS), lambda b, p, i: (0, 0)),
            _const_spec(place.shape),
        ],
        out_specs=pl.BlockSpec((ATT_TQ, LANES), lambda b, p, i: (b * nq + i, p)),
        out_shape=jax.ShapeDtypeStruct((N_TOK, ATTN_WIDTH), BF16),
        scratch_shapes=[
            pltpu.VMEM((SEQ, LANES), BF16),
            pltpu.VMEM((META_ROWS, LANES), BF16),
            pltpu.VMEM((2, 1, ATT_TQ), F32),
            pltpu.VMEM((2, LANES, ATT_TQ), F32),
            pltpu.VMEM((2, ATT_TK, ATT_TQ), F32),
            pltpu.VMEM((2, ATT_TK, ATT_TQ), F32),
        ],
        compiler_params=pltpu.CompilerParams(
            dimension_semantics=("arbitrary", "arbitrary", "arbitrary"), vmem_limit_bytes=VMEM_LIMIT),
        name="attn",
    )(q, k, vt, lf, km, vm, lfm, place)


MERGE_TM = 512


def _merge_kernel(act_ref, o_ref, sgc_ref, sga_ref, x_ref, wpw_ref, bpw_ref, wao_ref, wout_ref, g2_ref,
                  wrt_ref, brt_ref, tri_ref,
                  h1_ref, hn_ref, route_ref, routet_ref, cnt_ref, base_scr):
    i = pl.program_id(0)

    @pl.when(i == 0)
    def _():
        base_scr[...] = jnp.zeros_like(base_scr)

    y_conv = _dot(act_ref[...], wpw_ref[...]) + bpw_ref[...]
    y_attn = _dot(o_ref[...], wao_ref[...])
    merged = sgc_ref[...].astype(F32) * y_conv + sga_ref[...].astype(F32) * y_attn
    h1 = x_ref[...] + _dot(merged.astype(BF16), wout_ref[...])
    h1_ref[...] = h1
    ms = jnp.mean(h1 * h1, axis=-1, keepdims=True)
    hn = (h1 * lax.rsqrt(ms + RMS_EPS) * g2_ref[...]).astype(BF16)
    hn_ref[...] = hn.reshape(hn.shape[0], TOK_SUB, LANES)

    logits = _dot_nt(wrt_ref[...], hn) + brt_ref[...]
    row = lax.broadcasted_iota(jnp.int32, logits.shape, 0).astype(F32)
    big = float(LANES)
    is_g = row < N_GROUPS
    gl = jnp.where(is_g, logits, NEG_INF)
    gmax = jnp.max(gl, axis=0, keepdims=True)
    gidx = jnp.min(jnp.where(gl == gmax, row, big), axis=0, keepdims=True)
    gsum = jnp.sum(jnp.where(is_g, jnp.exp(gl - gmax), 0.0), axis=0, keepdims=True)
    g_w = 1.0 / gsum
    lo = N_GROUPS + EXPERTS_PER_GROUP * gidx
    el = jnp.where((row >= lo) & (row < lo + EXPERTS_PER_GROUP), logits, NEG_INF)
    t1 = jnp.max(el, axis=0, keepdims=True)
    i1 = jnp.min(jnp.where(el == t1, row, big), axis=0, keepdims=True)
    el2 = jnp.where(row == i1, NEG_INF, el)
    t2 = jnp.max(el2, axis=0, keepdims=True)
    i2 = jnp.min(jnp.where(el2 == t2, row, big), axis=0, keepdims=True)
    e2 = jnp.exp(t2 - t1)
    den = 1.0 + e2
    w1 = (1.0 / den) * g_w
    w2 = (e2 / den) * g_w
    ea = i1 - N_GROUPS
    eb = i2 - N_GROUPS

    oha = (row == ea).astype(F32)
    ohb = (row == eb).astype(F32)
    ohs = oha + ohb
    before = _dot(ohs.astype(BF16), tri_ref[...]) + base_scr[...][:, 0:1]
    ra = jnp.sum(before * oha, axis=0, keepdims=True)
    rb = jnp.sum(before * ohb, axis=0, keepdims=True)
    base_scr[...] = base_scr[...] + jnp.sum(ohs, axis=1, keepdims=True)
    cnt_ref[...] = base_scr[...]

    def rows_of(shape):
        r = lax.broadcasted_iota(jnp.int32, shape, 0)
        out = jnp.where(r == 0, ea, 0.0)
        for k, val in enumerate((eb, w1, w2, ra, rb), start=1):
            out = jnp.where(r == k, val, out)
        return out

    routet_ref[...] = rows_of((8, logits.shape[1]))
    route_ref[...] = rows_of(logits.shape).T


def _merge(act, o, sgc, sga, x2d, consts):
    tm = MERGE_TM
    row = lambda w: pl.BlockSpec((tm, w), lambda i: (i, 0))
    return pl.pallas_call(
        _merge_kernel,
        grid=(N_TOK // tm,),
        in_specs=[row(CONV_WIDTH), row(ATTN_WIDTH), row(D_MODEL), row(D_MODEL), row(D_MODEL)]
        + [_const_spec(c.shape) for c in consts],
        out_specs=[row(D_MODEL), pl.BlockSpec((tm, TOK_SUB, LANES), lambda i: (i, 0, 0)), row(LANES),
                   pl.BlockSpec((8, tm), lambda i: (0, i)), _const_spec((LANES, LANES))],
        out_shape=[jax.ShapeDtypeStruct((N_TOK, D_MODEL), F32),
                   jax.ShapeDtypeStruct((N_TOK, TOK_SUB, LANES), BF16),
                   jax.ShapeDtypeStruct((N_TOK, LANES), F32),
                   jax.ShapeDtypeStruct((8, N_TOK), F32),
                   jax.ShapeDtypeStruct((LANES, LANES), F32)],
        scratch_shapes=[pltpu.VMEM((LANES, LANES), F32)],
        compiler_params=pltpu.CompilerParams(dimension_semantics=("arbitrary",), vmem_limit_bytes=VMEM_LIMIT),
        name="merge",
    )(act, o, sgc, sga, x2d, *consts)


DISP_TM = 1024
DMA_UNROLL = 8


def _dest_kernel(pstart_ref, rt_ref, d_ref):
    rt = rt_ref[...]
    start = jnp.zeros_like(rt)
    for e in range(N_EXPERTS):
        start = jnp.where(rt == float(e), pstart_ref[e].astype(F32), start)
    d_ref[...] = (start + pltpu.roll(rt, 4, axis=0)).astype(jnp.int32)


def _dest(pstart, route_t):
    grid_spec = pltpu.PrefetchScalarGridSpec(
        num_scalar_prefetch=1,
        grid=(1,),
        in_specs=[pl.BlockSpec(route_t.shape, lambda i, *_: (0, 0))],
        out_specs=pl.BlockSpec(route_t.shape, lambda i, *_: (0, 0)),
    )
    return pl.pallas_call(
        _dest_kernel,
        grid_spec=grid_spec,
        out_shape=jax.ShapeDtypeStruct(route_t.shape, jnp.int32),
        compiler_params=pltpu.CompilerParams(dimension_semantics=("arbitrary",), vmem_limit_bytes=VMEM_LIMIT),
        name="dest",
    )(pstart, route_t)


def _dispatch_kernel(d0_ref, d1_ref, pstart_ref, pend_ref, hn_ref, xin_ref, zero_scr, sem, zsem):
    i = pl.program_id(0)
    base = i * DISP_TM

    @pl.when(i == 0)
    def _():
        zero_scr[...] = jnp.zeros_like(zero_scr)

        def zcopy(e):
            start = pl.multiple_of(pend_ref[e] - ROW_BLOCK, ROW_BLOCK)
            return pltpu.make_async_copy(zero_scr, xin_ref.at[pl.ds(start, ROW_BLOCK)], zsem.at[0])

        def zstart(e, carry):
            @pl.when(pend_ref[e] > pstart_ref[e])
            def _():
                zcopy(e).start()
            return carry

        def zwait(e, carry):
            @pl.when(pend_ref[e] > pstart_ref[e])
            def _():
                zcopy(e).wait()
            return carry

        n_used = pend_ref[N_EXPERTS - 1] // ROW_BLOCK

        def tcopy(b):
            return pltpu.make_async_copy(
                zero_scr, xin_ref.at[pl.ds(pl.multiple_of(b * ROW_BLOCK, ROW_BLOCK), ROW_BLOCK)], zsem.at[0])

        def tstart(b, carry):
            tcopy(b).start()
            return carry

        def twait(b, carry):
            tcopy(b).wait()
            return carry

        lax.fori_loop(0, N_EXPERTS, zstart, 0)
        lax.fori_loop(n_used, N_BLOCKS, tstart, 0)
        lax.fori_loop(0, N_EXPERTS, zwait, 0)
        lax.fori_loop(n_used, N_BLOCKS, twait, 0)

    def start(r, carry):
        pltpu.make_async_copy(hn_ref.at[r], xin_ref.at[d0_ref[base + r]], sem.at[0]).start()
        pltpu.make_async_copy(hn_ref.at[r], xin_ref.at[d1_ref[base + r]], sem.at[1]).start(priority=1)
        return carry

    lax.fori_loop(0, DISP_TM, start, 0, unroll=DMA_UNROLL)
    pltpu.make_async_copy(hn_ref, xin_ref.at[pl.ds(0, DISP_TM)], sem.at[0]).wait()
    pltpu.make_async_copy(hn_ref, xin_ref.at[pl.ds(0, DISP_TM)], sem.at[1]).wait()


def _dispatch(d0, d1, pstart, pend, hn):
    grid_spec = pltpu.PrefetchScalarGridSpec(
        num_scalar_prefetch=4,
        grid=(N_TOK // DISP_TM,),
        in_specs=[pl.BlockSpec((DISP_TM, TOK_SUB, LANES), lambda i, *_: (i, 0, 0))],
        out_specs=pl.BlockSpec(memory_space=pl.ANY),
        scratch_shapes=[pltpu.VMEM((ROW_BLOCK, TOK_SUB, LANES), BF16),
                        pltpu.SemaphoreType.DMA((2,)), pltpu.SemaphoreType.DMA((1,))],
    )
    return pl.pallas_call(
        _dispatch_kernel,
        grid_spec=grid_spec,
        out_shape=jax.ShapeDtypeStruct((P_ROWS, TOK_SUB, LANES), BF16),
        compiler_params=pltpu.CompilerParams(dimension_semantics=("arbitrary",), vmem_limit_bytes=VMEM_LIMIT),
        name="dispatch",
    )(d0, d1, pstart, pend, hn)


W_SLOTS = 3


def _expert_kernel(nu_ref, first_ref, ord_ref, used_ref, nue_ref, x_ref, wg_hbm, wu_hbm, wd_hbm, y_ref,
                   wg_f, wu_f, wd_f, wg_b, wu_b, wd_b, sem):
    i = pl.program_id(0)

    def fetch(k):
        e = used_ref[k]
        slot = k % W_SLOTS
        return (pltpu.make_async_copy(wg_hbm.at[e], wg_f.at[slot], sem.at[slot, 0]),
                pltpu.make_async_copy(wu_hbm.at[e], wu_f.at[slot], sem.at[slot, 1]),
                pltpu.make_async_copy(wd_hbm.at[e], wd_f.at[slot], sem.at[slot, 2]))

    @pl.when(i == 0)
    def _():
        for k in range(W_SLOTS - 1):
            @pl.when(k < nue_ref[0])
            def _():
                for c in fetch(k):
                    c.start()

    @pl.when(i < nu_ref[0])
    def _():
        @pl.when(first_ref[i] == 1)
        def _():
            k = ord_ref[i]
            for c in fetch(k):
                c.wait()

            @pl.when(k + (W_SLOTS - 1) < nue_ref[0])
            def _():
                for c in fetch(k + (W_SLOTS - 1)):
                    c.start()

            slot = k % W_SLOTS
            wg_b[...] = wg_f[slot].astype(BF16)
            wu_b[...] = wu_f[slot].astype(BF16)
            wd_b[...] = wd_f[slot].astype(BF16)

        x = x_ref[...].reshape(ROW_BLOCK, D_MODEL)
        g = _dot(x, wg_b[...])
        u = _dot(x, wu_b[...])
        hmid = (g * _sigmoid(g) * u).astype(BF16)
        y = _dot(hmid, wd_b[...]).astype(BF16)
        y_ref[...] = y.reshape(ROW_BLOCK, TOK_SUB, LANES)

    @pl.when(i >= nu_ref[0])
    def _():
        y_ref[...] = jnp.zeros_like(y_ref)


def _experts(n_used, blk_first, blk_ord, used_e, n_used_e, xin, w_gate, w_up, w_down):
    def xmap(i, nu, *_):
        return (jnp.minimum(i, nu[0] - 1), 0, 0)

    grid_spec = pltpu.PrefetchScalarGridSpec(
        num_scalar_prefetch=5,
        grid=(N_BLOCKS,),
        in_specs=[pl.BlockSpec((ROW_BLOCK, TOK_SUB, LANES), xmap),
                  pl.BlockSpec(memory_space=pl.ANY),
                  pl.BlockSpec(memory_space=pl.ANY),
                  pl.BlockSpec(memory_space=pl.ANY)],
        out_specs=pl.BlockSpec((ROW_BLOCK, TOK_SUB, LANES), lambda i, *_: (i, 0, 0)),
        scratch_shapes=[pltpu.VMEM((W_SLOTS, D_MODEL, D_EXPERT), F32),
                        pltpu.VMEM((W_SLOTS, D_MODEL, D_EXPERT), F32),
                        pltpu.VMEM((W_SLOTS, D_EXPERT, D_MODEL), F32),
                        pltpu.VMEM((D_MODEL, D_EXPERT), BF16),
                        pltpu.VMEM((D_MODEL, D_EXPERT), BF16),
                        pltpu.VMEM((D_EXPERT, D_MODEL), BF16),
                        pltpu.SemaphoreType.DMA((W_SLOTS, 3))],
    )
    return pl.pallas_call(
        _expert_kernel,
        grid_spec=grid_spec,
        out_shape=jax.ShapeDtypeStruct((P_ROWS, TOK_SUB, LANES), BF16),
        compiler_params=pltpu.CompilerParams(dimension_semantics=("arbitrary",), vmem_limit_bytes=VMEM_LIMIT),
        name="experts",
    )(n_used, blk_first, blk_ord, used_e, n_used_e, xin, w_gate, w_up, w_down)


COMB_TM = 256


def _combine_kernel(d0_ref, d1_ref, h1_ref, route_ref, yb_ref, out_ref, g_scr, sem):
    i = pl.program_id(0)
    n = pl.num_programs(0)

    def issue(tile, slot):
        base = tile * COMB_TM

        def start(r, carry):
            pltpu.make_async_copy(yb_ref.at[d0_ref[base + r]], g_scr.at[slot, 0, r], sem.at[slot, 0]).start()
            pltpu.make_async_copy(yb_ref.at[d1_ref[base + r]], g_scr.at[slot, 1, r], sem.at[slot, 1]).start(
                priority=1)
            return carry

        lax.fori_loop(0, COMB_TM, start, 0, unroll=DMA_UNROLL)

    @pl.when(i == 0)
    def _():
        issue(0, 0)

    slot = i % 2

    @pl.when(i + 1 < n)
    def _():
        issue(i + 1, 1 - slot)

    for k in range(2):
        pltpu.make_async_copy(yb_ref.at[pl.ds(0, COMB_TM)], g_scr.at[slot, k], sem.at[slot, k]).wait()

    route = route_ref[...]
    w0 = route[:, 2:3]
    w1 = route[:, 3:4]
    g0 = g_scr[slot, 0].reshape(COMB_TM, D_MODEL).astype(F32)
    g1 = g_scr[slot, 1].reshape(COMB_TM, D_MODEL).astype(F32)
    out_ref[...] = h1_ref[...] + (g0 * w0 + g1 * w1)


def _combine(d0, d1, h1, route, yb):
    grid_spec = pltpu.PrefetchScalarGridSpec(
        num_scalar_prefetch=2,
        grid=(N_TOK // COMB_TM,),
        in_specs=[pl.BlockSpec((COMB_TM, D_MODEL), lambda i, *_: (i, 0)),
                  pl.BlockSpec((COMB_TM, LANES), lambda i, *_: (i, 0)),
                  pl.BlockSpec(memory_space=pl.ANY)],
        out_specs=pl.BlockSpec((COMB_TM, D_MODEL), lambda i, *_: (i, 0)),
        scratch_shapes=[pltpu.VMEM((2, 2, COMB_TM, TOK_SUB, LANES), BF16),
                        pltpu.SemaphoreType.DMA((2, 2))],
    )
    return pl.pallas_call(
        _combine_kernel,
        grid_spec=grid_spec,
        out_shape=jax.ShapeDtypeStruct((N_TOK, D_MODEL), F32),
        compiler_params=pltpu.CompilerParams(dimension_semantics=("arbitrary",), vmem_limit_bytes=VMEM_LIMIT),
        name="combine",
    )(d0, d1, h1, route, yb)


def kernel(x, meta, norm1_g, w_in, b_in, dw_w, dw_b, conv_ln_g, conv_ln_b, w_conv_out, b_conv_out,
           q_norm_g, k_norm_g, b_forget, w_attn_out, w_out, norm2_g, w_group, b_group, w_router,
           b_router, w_gate, w_up, w_down):
    l = 0
    row = lambda v: v.reshape(1, -1).astype(F32)

    c_u, c_q, c_f = 2 * CONV_WIDTH, 2 * CONV_WIDTH, 2 * CONV_WIDTH + 3 * ATTN_WIDTH
    c_gc = c_f + N_HEADS
    c_ga = c_gc + D_MODEL
    wi, bi = w_in[l], b_in[l]
    wu, bu = wi[:, :c_u].astype(BF16), row(bi[:c_u])
    wqkv, bqkv = wi[:, c_q:c_f].astype(BF16), row(bi[c_q:c_f])
    wft = jnp.zeros((16, D_MODEL), F32).at[:N_HEADS].set(wi[:, c_f:c_gc].T).astype(BF16)
    bft = jnp.zeros((16, 1), F32).at[:N_HEADS, 0].set(bi[c_f:c_gc] + b_forget[l])
    wgc, bgc = wi[:, c_gc:c_ga].astype(BF16), row(bi[c_gc:c_ga])
    wga, bga = wi[:, c_ga:].astype(BF16), row(bi[c_ga:])
    qg = row(jnp.tile(q_norm_g[l], N_HEADS) * (HEAD_DIM ** -0.5 * LOG2E))
    kg = row(jnp.tile(k_norm_g[l], N_HEADS))
    head_of_col = jnp.arange(ATTN_WIDTH) // HEAD_DIM
    hsum = (head_of_col[:, None] == jnp.arange(LANES)[None, :]).astype(BF16)
    hexp_half = (jnp.arange(LANES)[:, None] == head_of_col[None, :]).astype(BF16)
    hexp = jnp.concatenate([hexp_half, hexp_half], axis=0)
    proj_consts = (row(norm1_g[l]), wu, bu, wqkv, bqkv, wft, bft, wgc, bgc, wga, bga, qg, kg, hsum, hexp)

    x2d = x.reshape(N_TOK, D_MODEL)
    meta_pad = jnp.zeros((META_ROWS, D_MODEL), F32).at[:N_META].set(meta.astype(F32))
    a, q, k, v, lf, sgc, sga = _proj(x2d, proj_consts, 512)
    a_m, _, k_m, v_m, lf_m, _, _ = _proj(meta_pad, proj_consts, META_ROWS)

    act = _conv(a.reshape(BATCH, SEQ, CONV_WIDTH), a_m, dw_w[l].astype(F32), row(dw_b[l]),
                row(conv_ln_g[l]), row(conv_ln_b[l])).reshape(N_TOK, CONV_WIDTH)

    o = _attention_t(q, k, v, lf, k_m, v_m, lf_m)

    wrt = jnp.zeros((LANES, D_MODEL), F32).at[:N_GROUPS].set(w_group[l].T)
    wrt = wrt.at[N_GROUPS:N_GROUPS + N_EXPERTS].set(w_router[l].T).astype(BF16)
    brt = jnp.zeros((LANES, 1), F32).at[:N_GROUPS, 0].set(b_group[l])
    brt = brt.at[N_GROUPS:N_GROUPS + N_EXPERTS, 0].set(b_router[l])
    tri = jnp.triu(jnp.ones((MERGE_TM, MERGE_TM), F32), 1).astype(BF16)
    merge_consts = (w_conv_out[l].astype(BF16), row(b_conv_out[l]), w_attn_out[l].astype(BF16),
                    w_out[l].astype(BF16), row(norm2_g[l]), wrt, brt, tri)
    h1, hn, route, route_t, cnt = _merge(act, o, sgc, sga, x2d, merge_consts)

    counts = cnt[:N_EXPERTS, 0].astype(jnp.int32)
    padded = (counts + ROW_BLOCK - 1) // ROW_BLOCK * ROW_BLOCK
    pend = jnp.cumsum(padded).astype(jnp.int32)
    pstart = pend - padded
    blk_row = jnp.arange(N_BLOCKS, dtype=jnp.int32) * ROW_BLOCK
    n_used = pend[-1:] // ROW_BLOCK
    nz = counts > 0
    before = (pend[None, :] <= blk_row[:, None]) & nz[None, :]
    blk_ord = jnp.sum(before.astype(jnp.int32), axis=1)
    blk_first = jnp.any((pstart[None, :] == blk_row[:, None]) & nz[None, :], axis=1).astype(jnp.int32)
    e_ord = jnp.cumsum(nz.astype(jnp.int32)) - 1
    e_ids = jnp.arange(N_EXPERTS, dtype=jnp.int32)
    used_e = jnp.sum(jnp.where((e_ord[None, :] == e_ids[:, None]) & nz[None, :], e_ids[None, :], 0), axis=1)
    n_used_e = jnp.sum(nz.astype(jnp.int32)).reshape(1)

    dest = _dest(pstart, route_t)
    d0, d1 = dest[0], dest[1]
    xin = _dispatch(d0, d1, pstart, pend, hn)
    yb = _experts(n_used, blk_first, blk_ord, used_e.astype(jnp.int32), n_used_e, xin,
                  w_gate[l], w_up[l], w_down[l])
    out = _combine(d0, d1, h1, route, yb)
    return out.reshape(BATCH, SEQ, D_MODEL)
```

```python
import functools

import jax
import jax.numpy as jnp
from jax import lax
from jax.experimental import pallas as pl
from jax.experimental.pallas import tpu as pltpu

D_MODEL = 1024
BATCH = 8
SEQ = 2048
N_META = 16
CONV_WIDTH = 512
CONV_K = 31
N_HEADS = 8
HEAD_DIM = 64
ATTN_WIDTH = N_HEADS * HEAD_DIM
N_GROUPS = 8
EXPERTS_PER_GROUP = 8
N_EXPERTS = N_GROUPS * EXPERTS_PER_GROUP
D_EXPERT = 256
RMS_EPS = 1e-6
LN_EPS = 1e-5

N_TOK = BATCH * SEQ
LANES = 128
META_ROWS = 128
ROW_BLOCK = 512
N_ASSIGN = 2 * N_TOK
N_BLOCKS = N_ASSIGN // ROW_BLOCK + N_EXPERTS
P_ROWS = N_BLOCKS * ROW_BLOCK
TOK_SUB = D_MODEL // LANES
RANK_BITS = 16
ATT_TQ = 1024
ATT_TK = 256
N_PAIRS = N_HEADS // 2
assert (ATT_TQ // ATT_TK) % 2 == 0
VMEM_LIMIT = 56 * 1024 * 1024

F32 = jnp.float32
BF16 = jnp.bfloat16
NEG_INF = float("-inf")
LOG2E = 1.4426950408889634


def _dot(a, b):
    return jnp.dot(a, b, preferred_element_type=F32)


def _dot_nt(a, b):
    return lax.dot_general(a, b, (((1,), (1,)), ((), ())), preferred_element_type=F32)


def _sigmoid(x):
    return 1.0 / (1.0 + jnp.exp(-x))


def _const_spec(shape):
    nd = len(shape)
    return pl.BlockSpec(shape, lambda *_: (0,) * nd)


def _head_rms(t, hsum_ref, hexp_ref, gain):
    ss = _dot((t * t).astype(BF16), hsum_ref[...])
    inv = lax.rsqrt(ss * (1.0 / HEAD_DIM) + RMS_EPS)
    hi = inv.astype(BF16)
    lo = (inv - hi.astype(F32)).astype(BF16)
    invb = _dot(jnp.concatenate([hi, lo], axis=1), hexp_ref[...])
    return t * invb * gain


def _proj_kernel(x_ref, g1_ref, wu_ref, bu_ref, wqkv_ref, bqkv_ref, wft_ref, bft_ref, wgc_ref, bgc_ref,
                 wga_ref, bga_ref, qg_ref, kg_ref, hsum_ref, hexp_ref,
                 a_ref, q_ref, k_ref, v_ref, lf_ref, sgc_ref, sga_ref):
    x = x_ref[...]
    ms = jnp.mean(x * x, axis=-1, keepdims=True)
    xn = (x * lax.rsqrt(ms + RMS_EPS) * g1_ref[...]).astype(BF16)

    u = _dot(xn, wu_ref[...]) + bu_ref[...]
    a_ref[...] = (u[:, :CONV_WIDTH] * _sigmoid(u[:, CONV_WIDTH:])).astype(BF16)

    qkv = _dot(xn, wqkv_ref[...]) + bqkv_ref[...]
    q_ref[...] = _head_rms(qkv[:, :ATTN_WIDTH], hsum_ref, hexp_ref, qg_ref[...]).astype(BF16)
    k_ref[...] = _head_rms(qkv[:, ATTN_WIDTH:2 * ATTN_WIDTH], hsum_ref, hexp_ref, kg_ref[...]).astype(BF16)
    v = qkv[:, 2 * ATTN_WIDTH:]
    vb = v_ref.shape[2]
    for c in range(v_ref.shape[0]):
        v_ref[c] = v[c * vb:(c + 1) * vb, :].T.astype(BF16)

    f = _dot_nt(wft_ref[...], xn) + bft_ref[...]
    lf_ref[...] = jnp.minimum(f, 0.0) - jnp.log(1.0 + jnp.exp(-jnp.abs(f)))

    sgc_ref[...] = _sigmoid(_dot(xn, wgc_ref[...]) + bgc_ref[...]).astype(BF16)
    sga_ref[...] = _sigmoid(_dot(xn, wga_ref[...]) + bga_ref[...]).astype(BF16)


def _proj(x2d, consts, tm):
    n = x2d.shape[0]
    row = lambda w: pl.BlockSpec((tm, w), lambda i: (i, 0))
    in_specs = [row(D_MODEL)] + [_const_spec(c.shape) for c in consts]
    vb = min(tm, ATT_TK)
    out_shape = [
        jax.ShapeDtypeStruct((n, CONV_WIDTH), BF16),
        jax.ShapeDtypeStruct((n, ATTN_WIDTH), BF16),
        jax.ShapeDtypeStruct((n, ATTN_WIDTH), BF16),
        jax.ShapeDtypeStruct((n // vb, ATTN_WIDTH, vb), BF16),
        jax.ShapeDtypeStruct((16, n), F32),
        jax.ShapeDtypeStruct((n, D_MODEL), BF16),
        jax.ShapeDtypeStruct((n, D_MODEL), BF16),
    ]
    out_specs = [row(CONV_WIDTH), row(ATTN_WIDTH), row(ATTN_WIDTH),
                 pl.BlockSpec((tm // vb, ATTN_WIDTH, vb), lambda i: (i, 0, 0)),
                 pl.BlockSpec((16, tm), lambda i: (0, i)), row(D_MODEL), row(D_MODEL)]
    return pl.pallas_call(
        _proj_kernel,
        grid=(n // tm,),
        in_specs=in_specs,
        out_specs=out_specs,
        out_shape=out_shape,
        compiler_params=pltpu.CompilerParams(dimension_semantics=("arbitrary",), vmem_limit_bytes=VMEM_LIMIT),
        name="proj",
    )(x2d, *consts)


CONV_CHUNK = 64
CONV_PAD = 32
CONV_WIN = CONV_CHUNK + 32
CONV_SHROWS = CONV_WIN


def _conv_kernel(a_ref, am_ref, shift_ref, dw_ref, dwb_ref, lng_ref, lnb_ref, o_ref, pad_scr, sha_scr, shb_scr):
    pad_scr[0:N_META, :] = jnp.zeros((N_META, CONV_WIDTH), BF16)
    pad_scr[N_META:CONV_PAD, :] = am_ref[0:N_META, :]
    pad_scr[CONV_PAD:CONV_PAD + SEQ, :] = a_ref[0]

    def shifted_copies(ci):
        r0 = pl.multiple_of(ci * CONV_CHUNK, CONV_CHUNK)
        return _dot(shift_ref[...], pad_scr[pl.ds(r0, CONV_WIN), :])

    n_chunks = SEQ // CONV_CHUNK
    sha_scr[...] = shifted_copies(0)

    def finish(ci, cur_scr):
        r0 = pl.multiple_of(ci * CONV_CHUNK, CONV_CHUNK)
        acc = jnp.zeros((CONV_CHUNK, CONV_WIDTH), F32) + dwb_ref[...]
        for j in range(CONV_K):
            off = j + 2
            b = off % 8
            a0 = b * CONV_SHROWS + (off - b)
            acc = acc + cur_scr[a0:a0 + CONV_CHUNK, :] * dw_ref[j:j + 1, :]
        mu = jnp.mean(acc, axis=-1, keepdims=True)
        d = acc - mu
        var = jnp.mean(d * d, axis=-1, keepdims=True)
        y = d * lax.rsqrt(var + LN_EPS) * lng_ref[...] + lnb_ref[...]
        o_ref[0, pl.ds(r0, CONV_CHUNK), :] = (y * _sigmoid(y)).astype(BF16)

    def chunk_pair(i, carry):
        ci = 2 * i
        shb_scr[...] = shifted_copies(ci + 1)
        finish(ci, sha_scr)
        sha_scr[...] = shifted_copies(jnp.minimum(ci + 2, n_chunks - 1))
        finish(ci + 1, shb_scr)
        return carry

    lax.fori_loop(0, n_chunks // 2, chunk_pair, 0)


def _conv(a3, a_meta, dw_w, dw_b, ln_g, ln_b):
    rr = jnp.arange(8 * CONV_SHROWS)
    shift = ((rr % CONV_SHROWS + rr // CONV_SHROWS)[:, None] == jnp.arange(CONV_WIN)[None, :]).astype(BF16)
    return pl.pallas_call(
        _conv_kernel,
        grid=(BATCH,),
        in_specs=[pl.BlockSpec((1, SEQ, CONV_WIDTH), lambda b: (b, 0, 0)),
                  _const_spec(a_meta.shape), _const_spec(shift.shape), _const_spec(dw_w.shape),
                  _const_spec(dw_b.shape), _const_spec(ln_g.shape), _const_spec(ln_b.shape)],
        out_specs=pl.BlockSpec((1, SEQ, CONV_WIDTH), lambda b: (b, 0, 0)),
        out_shape=jax.ShapeDtypeStruct((BATCH, SEQ, CONV_WIDTH), BF16),
        scratch_shapes=[pltpu.VMEM((CONV_PAD + SEQ, CONV_WIDTH), BF16),
                        pltpu.VMEM((8 * CONV_SHROWS, CONV_WIDTH), F32),
                        pltpu.VMEM((8 * CONV_SHROWS, CONV_WIDTH), F32)],
        compiler_params=pltpu.CompilerParams(dimension_semantics=("arbitrary",), vmem_limit_bytes=VMEM_LIMIT),
        name="conv",
    )(a3, a_meta, shift, dw_w, dw_b, ln_g, ln_b)


def _split3(x):
    hi = x.astype(BF16)
    r1 = x - hi.astype(F32)
    mid = r1.astype(BF16)
    lo = (r1 - mid.astype(F32)).astype(BF16)
    return jnp.concatenate([hi, mid, lo], axis=0)


def _attn_kernel(q_ref, k_ref, v_ref, lf_ref, km_ref, vm_ref, lfm_ref, o_ref,
                 c_scr, cm_scr, m_scr, acc_scr, sa_scr, sb_scr):
    p = pl.program_id(1)
    qi = pl.program_id(2)
    nk = SEQ // ATT_TK

    @pl.when((p == 0) & (qi == 0))
    def _():
        r = lax.broadcasted_iota(jnp.int32, (ATT_TK, ATT_TK), 0)
        c = lax.broadcasted_iota(jnp.int32, (ATT_TK, ATT_TK), 1)
        tri = (r <= c).astype(BF16)
        carry = jnp.zeros((N_HEADS, 1), F32)
        for j in range(nk):
            parts = _dot(_split3(lf_ref[0:N_HEADS, j * ATT_TK:(j + 1) * ATT_TK]), tri)
            cj = parts[0:8] + parts[8:16] + parts[16:24] + carry
            c_scr[j] = cj
            carry = cj[:, ATT_TK - 1:ATT_TK]
        rm = lax.broadcasted_iota(jnp.int32, (LANES, LANES), 0)
        cmx = lax.broadcasted_iota(jnp.int32, (LANES, LANES), 1)
        trim = ((rm > cmx) & (rm < N_META)).astype(BF16)
        pm = _dot(_split3(lfm_ref[0:N_HEADS, :]), trim)
        cm_scr[...] = -(pm[0:8] + pm[8:16] + pm[16:24])

    lane = lax.broadcasted_iota(jnp.int32, (ATT_TQ, LANES), 1)
    q_all = q_ref[...]
    zq = jnp.zeros_like(q_all)
    qm = (jnp.where(lane < HEAD_DIM, q_all, zq), jnp.where(lane >= HEAD_DIM, q_all, zq))
    for hl in range(2):
        m_scr[hl] = jnp.full((ATT_TQ, LANES), NEG_INF, F32)
        acc_scr[hl] = jnp.zeros((ATT_TQ, LANES), F32)

    def scores(kb, crows):
        return tuple(_dot_nt(qm[hl], kb) - crows[hl] for hl in range(2))

    def update(s2, vb, mask):
        vlane = lax.broadcasted_iota(jnp.int32, vb.shape, 1)
        ones = jnp.ones_like(vb)
        for hl in range(2):
            s = s2[hl]
            if mask is not None:
                s = jnp.where(mask, s, NEG_INF)
            m_prev = m_scr[hl]
            m_next = jnp.maximum(m_prev, jnp.max(s, axis=1, keepdims=True))
            reps = s.shape[1] // LANES
            m_wide = m_next if reps == 1 else jnp.concatenate([m_next] * reps, axis=1)
            pr = jnp.exp(s - m_wide).astype(BF16)
            alpha = jnp.exp(m_prev - m_next)
            own = (vlane < HEAD_DIM) if hl == 0 else (vlane >= HEAD_DIM)
            acc_scr[hl] = acc_scr[hl] * alpha + _dot(pr, jnp.where(own, vb, ones))
            m_scr[hl] = m_next

    def crows_at(j):
        return (c_scr[j, pl.ds(2 * p, 1), :], c_scr[j, pl.ds(2 * p + 1, 1), :])

    def kblock(j):
        return k_ref[pl.ds(pl.multiple_of(j * ATT_TK, ATT_TK), ATT_TK), :]

    def vblock(j):
        return v_ref[pl.ds(pl.multiple_of(j * ATT_TK, ATT_TK), ATT_TK), :]

    update(scores(km_ref[...], (cm_scr[pl.ds(2 * p, 1), :], cm_scr[pl.ds(2 * p + 1, 1), :])), vm_ref[...],
           lane < N_META)

    nfull = qi * (ATT_TQ // ATT_TK)

    def put(scr, s2):
        scr[0] = s2[0]
        scr[1] = s2[1]

    put(sa_scr, scores(kblock(0), crows_at(0)))

    def full2(i, carry):
        j = 2 * i
        put(sb_scr, scores(kblock(j + 1), crows_at(j + 1)))
        update((sa_scr[0], sa_scr[1]), vblock(j), None)
        put(sa_scr, scores(kblock(j + 2), crows_at(j + 2)))
        update((sb_scr[0], sb_scr[1]), vblock(j + 1), None)
        return carry

    lax.fori_loop(0, nfull // 2, full2, 0)

    rows = lax.broadcasted_iota(jnp.int32, (ATT_TQ, ATT_TK), 0)
    cols = lax.broadcasted_iota(jnp.int32, (ATT_TQ, ATT_TK), 1)
    s_diag = [(sa_scr[0], sa_scr[1])]
    for dj in range(1, ATT_TQ // ATT_TK):
        s_diag.append(scores(kblock(nfull + dj), crows_at(nfull + dj)))
    for dj in range(ATT_TQ // ATT_TK):
        update(s_diag[dj], vblock(nfull + dj), cols + dj * ATT_TK <= rows)

    a0 = acc_scr[0]
    a1 = acc_scr[1]
    o0 = a0 / pltpu.roll(a0, HEAD_DIM, axis=1)
    o1 = a1 / pltpu.roll(a1, HEAD_DIM, axis=1)
    o_ref[...] = jnp.where(lane < HEAD_DIM, o0, o1).astype(BF16)


def _attention(q, k, v, lf, km, vm, lfm):
    nq = SEQ // ATT_TQ
    return pl.pallas_call(
        _attn_kernel,
        grid=(BATCH, N_PAIRS, nq),
        in_specs=[
            pl.BlockSpec((ATT_TQ, LANES), lambda b, p, i: (b * nq + i, p)),
            pl.BlockSpec((SEQ, LANES), lambda b, p, i: (b, p)),
            pl.BlockSpec((SEQ, LANES), lambda b, p, i: (b, p)),
            pl.BlockSpec((16, SEQ), lambda b, p, i: (0, b)),
            pl.BlockSpec((META_ROWS, LANES), lambda b, p, i: (0, p)),
            pl.BlockSpec((META_ROWS, LANES), lambda b, p, i: (0, p)),
            pl.BlockSpec((16, META_ROWS), lambda b, p, i: (0, 0)),
        ],
        out_specs=pl.BlockSpec((ATT_TQ, LANES), lambda b, p, i: (b * nq + i, p)),
        out_shape=jax.ShapeDtypeStruct((N_TOK, ATTN_WIDTH), BF16),
        scratch_shapes=[
            pltpu.VMEM((SEQ // ATT_TK, N_HEADS, ATT_TK), F32),
            pltpu.VMEM((N_HEADS, LANES), F32),
            pltpu.VMEM((2, ATT_TQ, LANES), F32),
            pltpu.VMEM((2, ATT_TQ, LANES), F32),
            pltpu.VMEM((2, ATT_TQ, ATT_TK), F32),
            pltpu.VMEM((2, ATT_TQ, ATT_TK), F32),
        ],
        compiler_params=pltpu.CompilerParams(
            dimension_semantics=("arbitrary", "arbitrary", "arbitrary"), vmem_limit_bytes=VMEM_LIMIT),
        name="attn",
    )(q, k, v, lf, km, vm, lfm)


AUG_TERMS = 3


def _split_terms(x):
    hi = x.astype(BF16)
    r1 = x - hi.astype(F32)
    mid = r1.astype(BF16)
    lo = (r1 - mid.astype(F32)).astype(BF16)
    return hi, mid, lo


def _attn_t_kernel(q_ref, k_ref, vt_ref, lf_ref, km_ref, vm_ref, lfm_ref, place_ref, o_ref,
                   aug_scr, augm_scr, m_scr, acc_scr, sa_scr, sb_scr):
    p = pl.program_id(1)
    qi = pl.program_id(2)
    nk = SEQ // ATT_TK

    def bias_lanes(colsum):
        out = None
        for t, term in enumerate(_split_terms(colsum)):
            d = _dot(term, place_ref[t])
            out = d if out is None else out + d
        return out.astype(BF16)

    @pl.when((p == 0) & (qi == 0))
    def _():
        r = lax.broadcasted_iota(jnp.int32, (ATT_TK, ATT_TK), 0)
        c = lax.broadcasted_iota(jnp.int32, (ATT_TK, ATT_TK), 1)
        tri = (c <= r).astype(BF16)
        carry = jnp.zeros((1, 16), F32)
        for j in range(nk):
            lf = lf_ref[:, j * ATT_TK:(j + 1) * ATT_TK]
            csum = carry
            for term in _split_terms(lf):
                csum = csum + _dot_nt(tri, term)
            aug_scr[j * ATT_TK:(j + 1) * ATT_TK, :] = bias_lanes(csum * (-LOG2E))
            carry = csum[ATT_TK - 1:ATT_TK, :]
        rm = lax.broadcasted_iota(jnp.int32, (META_ROWS, META_ROWS), 0)
        cm = lax.broadcasted_iota(jnp.int32, (META_ROWS, META_ROWS), 1)
        trim = ((cm > rm) & (cm < N_META)).astype(BF16)
        msum = jnp.zeros((META_ROWS, 16), F32)
        for term in _split_terms(lfm_ref[...]):
            msum = msum + _dot_nt(trim, term)
        augm_scr[...] = bias_lanes(msum * LOG2E)

    lane = lax.broadcasted_iota(jnp.int32, (ATT_TQ, LANES), 1)
    q_all = q_ref[...]
    zq = jnp.zeros_like(q_all)
    q_cat = []
    for hl in range(2):
        own = (lane < HEAD_DIM) if hl == 0 else (lane >= HEAD_DIM)
        h = 2 * p + hl
        bias_sel = (lane >= AUG_TERMS * h) & (lane < AUG_TERMS * h + AUG_TERMS)
        bias_ones = jnp.where(bias_sel, 1.0, 0.0).astype(BF16)
        q_cat.append(jnp.concatenate([jnp.where(own, q_all, zq), bias_ones], axis=1))
        m_scr[hl] = jnp.full((1, ATT_TQ), NEG_INF, F32)
        acc_scr[hl] = jnp.zeros((LANES, ATT_TQ), F32)

    def scores(kb, augb, lo=0):
        k_cat = jnp.concatenate([kb, augb], axis=1)
        return tuple(_dot_nt(k_cat, q_cat[hl][lo:]) for hl in range(2))

    def update(s2, vtb, mask, lo=0):
        vrow = lax.broadcasted_iota(jnp.int32, vtb.shape, 0)
        ones = jnp.ones_like(vtb)
        for hl in range(2):
            s = s2[hl]
            if mask is not None:
                s = jnp.where(mask, s, NEG_INF)
            m_prev = m_scr[hl, :, lo:]
            m_next = jnp.maximum(m_prev, jnp.max(s, axis=0, keepdims=True))
            pr = jnp.exp2(s - m_next).astype(BF16)
            alpha = jnp.exp2(m_prev - m_next)
            own = (vrow < HEAD_DIM) if hl == 0 else (vrow >= HEAD_DIM)
            acc_scr[hl, :, lo:] = acc_scr[hl, :, lo:] * alpha + _dot(jnp.where(own, vtb, ones), pr)
            m_scr[hl, :, lo:] = m_next

    def kblock(j):
        return k_ref[pl.ds(pl.multiple_of(j * ATT_TK, ATT_TK), ATT_TK), :]

    def augblock(j):
        return aug_scr[pl.ds(pl.multiple_of(j * ATT_TK, ATT_TK), ATT_TK), :]

    nfull = qi * (ATT_TQ // ATT_TK)

    def put(scr, s2):
        scr[0] = s2[0]
        scr[1] = s2[1]

    s_meta = scores(km_ref[0:N_META, :], augm_scr[0:N_META, :])
    put(sa_scr, scores(kblock(0), augblock(0)))
    update(s_meta, vm_ref[0][:, 0:N_META], None)

    def full2(i, carry):
        j = 2 * i
        put(sb_scr, scores(kblock(j + 1), augblock(j + 1)))
        update((sa_scr[0], sa_scr[1]), vt_ref[j], None)
        put(sa_scr, scores(kblock(j + 2), augblock(j + 2)))
        update((sb_scr[0], sb_scr[1]), vt_ref[j + 1], None)
        return carry

    lax.fori_loop(0, nfull // 2, full2, 0)

    srow = lax.broadcasted_iota(jnp.int32, (ATT_TK, ATT_TQ), 0)
    tcol = lax.broadcasted_iota(jnp.int32, (ATT_TK, ATT_TQ), 1)
    s_diag = [(sa_scr[0], sa_scr[1])]
    for dj in range(1, ATT_TQ // ATT_TK):
        s_diag.append(scores(kblock(nfull + dj), augblock(nfull + dj), lo=dj * ATT_TK))
    for dj in range(ATT_TQ // ATT_TK):
        lo = dj * ATT_TK
        update(s_diag[dj], vt_ref[nfull + dj], (srow <= tcol)[:, :ATT_TQ - lo], lo=lo)

    a0 = acc_scr[0]
    a1 = acc_scr[1]
    ot = jnp.concatenate([a0[0:HEAD_DIM] / a0[HEAD_DIM:], a1[HEAD_DIM:] / a1[0:HEAD_DIM]], axis=0)
    o_ref[...] = ot.T.astype(BF16)


def _attention_t(q, k, vt, lf, km, vm, lfm):
    nq = SEQ // ATT_TQ
    nk = SEQ // ATT_TK
    hh = jnp.arange(16)[:, None]
    ll = jnp.arange(LANES)[None, :]
    place = jnp.stack([((ll == AUG_TERMS * hh + t) & (hh < N_HEADS)).astype(BF16) for t in range(AUG_TERMS)])
    return pl.pallas_call(
        _attn_t_kernel,
        grid=(BATCH, N_PAIRS, nq),
        in_specs=[
            pl.BlockSpec((ATT_TQ, LANES), lambda b, p, i: (b * nq + i, p)),
            pl.BlockSpec((SEQ, LANES), lambda b, p, i: (b, p)),
            pl.BlockSpec((nk, LANES, ATT_TK), lambda b, p, i: (b, p, 0)),
            pl.BlockSpec((16, SEQ), lambda b, p, i: (0, b)),
            pl.BlockSpec((META_ROWS, LANES), lambda b, p, i: (0, p)),
            pl.BlockSpec((1, LANES, META_ROWS), lambda b, p, i: (0, p, 0)),
            pl.BlockSpec((16, META_ROWS), lambda b, p, i: (0, 0)),
            _const_spec(place.shape),
        ],
        out_specs=pl.BlockSpec((ATT_TQ, LANES), lambda b, p, i: (b * nq + i, p)),
        out_shape=jax.ShapeDtypeStruct((N_TOK, ATTN_WIDTH), BF16),
        scratch_shapes=[
            pltpu.VMEM((SEQ, LANES), BF16),
            pltpu.VMEM((META_ROWS, LANES), BF16),
            pltpu.VMEM((2, 1, ATT_TQ), F32),
            pltpu.VMEM((2, LANES, ATT_TQ), F32),
            pltpu.VMEM((2, ATT_TK, ATT_TQ), F32),
            pltpu.VMEM((2, ATT_TK, ATT_TQ), F32),
        ],
        compiler_params=pltpu.CompilerParams(
            dimension_semantics=("arbitrary", "arbitrary", "arbitrary"), vmem_limit_bytes=VMEM_LIMIT),
        name="attn",
    )(q, k, vt, lf, km, vm, lfm, place)


MERGE_TM = 512


def _merge_kernel(act_ref, o_ref, sgc_ref, sga_ref, x_ref, wpw_ref, bpw_ref, wao_ref, wout_ref, g2_ref,
                  wrt_ref, brt_ref, tri_ref,
                  h1_ref, hn_ref, route_ref, routet_ref, cnt_ref, base_scr):
    i = pl.program_id(0)

    @pl.when(i == 0)
    def _():
        base_scr[...] = jnp.zeros_like(base_scr)

    y_conv = _dot(act_ref[...], wpw_ref[...]) + bpw_ref[...]
    y_attn = _dot(o_ref[...], wao_ref[...])
    merged = sgc_ref[...].astype(F32) * y_conv + sga_ref[...].astype(F32) * y_attn
    h1 = x_ref[...] + _dot(merged.astype(BF16), wout_ref[...])
    h1_ref[...] = h1
    ms = jnp.mean(h1 * h1, axis=-1, keepdims=True)
    hn = (h1 * lax.rsqrt(ms + RMS_EPS) * g2_ref[...]).astype(BF16)
    hn_ref[...] = hn.reshape(hn.shape[0], TOK_SUB, LANES)

    logits = _dot_nt(wrt_ref[...], hn) + brt_ref[...]
    row = lax.broadcasted_iota(jnp.int32, logits.shape, 0).astype(F32)
    big = float(LANES)
    is_g = row < N_GROUPS
    gl = jnp.where(is_g, logits, NEG_INF)
    gmax = jnp.max(gl, axis=0, keepdims=True)
    gidx = jnp.min(jnp.where(gl == gmax, row, big), axis=0, keepdims=True)
    gsum = jnp.sum(jnp.where(is_g, jnp.exp(gl - gmax), 0.0), axis=0, keepdims=True)
    g_w = 1.0 / gsum
    lo = N_GROUPS + EXPERTS_PER_GROUP * gidx
    el = jnp.where((row >= lo) & (row < lo + EXPERTS_PER_GROUP), logits, NEG_INF)
    t1 = jnp.max(el, axis=0, keepdims=True)
    i1 = jnp.min(jnp.where(el == t1, row, big), axis=0, keepdims=True)
    el2 = jnp.where(row == i1, NEG_INF, el)
    t2 = jnp.max(el2, axis=0, keepdims=True)
    i2 = jnp.min(jnp.where(el2 == t2, row, big), axis=0, keepdims=True)
    e2 = jnp.exp(t2 - t1)
    den = 1.0 + e2
    w1 = (1.0 / den) * g_w
    w2 = (e2 / den) * g_w
    ea = i1 - N_GROUPS
    eb = i2 - N_GROUPS

    oha = (row == ea).astype(F32)
    ohb = (row == eb).astype(F32)
    ohs = oha + ohb
    before = _dot(ohs.astype(BF16), tri_ref[...]) + base_scr[...][:, 0:1]
    ra = jnp.sum(before * oha, axis=0, keepdims=True)
    rb = jnp.sum(before * ohb, axis=0, keepdims=True)
    base_scr[...] = base_scr[...] + jnp.sum(ohs, axis=1, keepdims=True)
    cnt_ref[...] = base_scr[...]

    def rows_of(shape):
        r = lax.broadcasted_iota(jnp.int32, shape, 0)
        out = jnp.where(r == 0, ea, 0.0)
        for k, val in enumerate((eb, w1, w2, ra, rb), start=1):
            out = jnp.where(r == k, val, out)
        return out

    routet_ref[...] = rows_of((8, logits.shape[1]))
    route_ref[...] = rows_of(logits.shape).T


def _merge(act, o, sgc, sga, x2d, consts):
    tm = MERGE_TM
    row = lambda w: pl.BlockSpec((tm, w), lambda i: (i, 0))
    return pl.pallas_call(
        _merge_kernel,
        grid=(N_TOK // tm,),
        in_specs=[row(CONV_WIDTH), row(ATTN_WIDTH), row(D_MODEL), row(D_MODEL), row(D_MODEL)]
        + [_const_spec(c.shape) for c in consts],
        out_specs=[row(D_MODEL), pl.BlockSpec((tm, TOK_SUB, LANES), lambda i: (i, 0, 0)), row(LANES),
                   pl.BlockSpec((8, tm), lambda i: (0, i)), _const_spec((LANES, LANES))],
        out_shape=[jax.ShapeDtypeStruct((N_TOK, D_MODEL), F32),
                   jax.ShapeDtypeStruct((N_TOK, TOK_SUB, LANES), BF16),
                   jax.ShapeDtypeStruct((N_TOK, LANES), F32),
                   jax.ShapeDtypeStruct((8, N_TOK), F32),
                   jax.ShapeDtypeStruct((LANES, LANES), F32)],
        scratch_shapes=[pltpu.VMEM((LANES, LANES), F32)],
        compiler_params=pltpu.CompilerParams(dimension_semantics=("arbitrary",), vmem_limit_bytes=VMEM_LIMIT),
        name="merge",
    )(act, o, sgc, sga, x2d, *consts)


DISP_TM = 1024
DMA_UNROLL = 8


def _dest_kernel(pstart_ref, rt_ref, d_ref):
    rt = rt_ref[...]
    start = jnp.zeros_like(rt)
    for e in range(N_EXPERTS):
        start = jnp.where(rt == float(e), pstart_ref[e].astype(F32), start)
    d_ref[...] = (start + pltpu.roll(rt, 4, axis=0)).astype(jnp.int32)


def _dest(pstart, route_t):
    grid_spec = pltpu.PrefetchScalarGridSpec(
        num_scalar_prefetch=1,
        grid=(1,),
        in_specs=[pl.BlockSpec(route_t.shape, lambda i, *_: (0, 0))],
        out_specs=pl.BlockSpec(route_t.shape, lambda i, *_: (0, 0)),
    )
    return pl.pallas_call(
        _dest_kernel,
        grid_spec=grid_spec,
        out_shape=jax.ShapeDtypeStruct(route_t.shape, jnp.int32),
        compiler_params=pltpu.CompilerParams(dimension_semantics=("arbitrary",), vmem_limit_bytes=VMEM_LIMIT),
        name="dest",
    )(pstart, route_t)


def _dispatch_kernel(d0_ref, d1_ref, pstart_ref, pend_ref, hn_ref, xin_ref, zero_scr, sem, zsem):
    i = pl.program_id(0)
    base = i * DISP_TM

    @pl.when(i == 0)
    def _():
        zero_scr[...] = jnp.zeros_like(zero_scr)

        def zcopy(e):
            start = pl.multiple_of(pend_ref[e] - ROW_BLOCK, ROW_BLOCK)
            return pltpu.make_async_copy(zero_scr, xin_ref.at[pl.ds(start, ROW_BLOCK)], zsem.at[0])

        def zstart(e, carry):
            @pl.when(pend_ref[e] > pstart_ref[e])
            def _():
                zcopy(e).start()
            return carry

        def zwait(e, carry):
            @pl.when(pend_ref[e] > pstart_ref[e])
            def _():
                zcopy(e).wait()
            return carry

        n_used = pend_ref[N_EXPERTS - 1] // ROW_BLOCK

        def tcopy(b):
            return pltpu.make_async_copy(
                zero_scr, xin_ref.at[pl.ds(pl.multiple_of(b * ROW_BLOCK, ROW_BLOCK), ROW_BLOCK)], zsem.at[0])

        def tstart(b, carry):
            tcopy(b).start()
            return carry

        def twait(b, carry):
            tcopy(b).wait()
            return carry

        lax.fori_loop(0, N_EXPERTS, zstart, 0)
        lax.fori_loop(n_used, N_BLOCKS, tstart, 0)
        lax.fori_loop(0, N_EXPERTS, zwait, 0)
        lax.fori_loop(n_used, N_BLOCKS, twait, 0)

    def start(r, carry):
        pltpu.make_async_copy(hn_ref.at[r], xin_ref.at[d0_ref[base + r]], sem.at[0]).start()
        pltpu.make_async_copy(hn_ref.at[r], xin_ref.at[d1_ref[base + r]], sem.at[1]).start(priority=1)
        return carry

    lax.fori_loop(0, DISP_TM, start, 0, unroll=DMA_UNROLL)
    pltpu.make_async_copy(hn_ref, xin_ref.at[pl.ds(0, DISP_TM)], sem.at[0]).wait()
    pltpu.make_async_copy(hn_ref, xin_ref.at[pl.ds(0, DISP_TM)], sem.at[1]).wait()


def _dispatch(d0, d1, pstart, pend, hn):
    grid_spec = pltpu.PrefetchScalarGridSpec(
        num_scalar_prefetch=4,
        grid=(N_TOK // DISP_TM,),
        in_specs=[pl.BlockSpec((DISP_TM, TOK_SUB, LANES), lambda i, *_: (i, 0, 0))],
        out_specs=pl.BlockSpec(memory_space=pl.ANY),
        scratch_shapes=[pltpu.VMEM((ROW_BLOCK, TOK_SUB, LANES), BF16),
                        pltpu.SemaphoreType.DMA((2,)), pltpu.SemaphoreType.DMA((1,))],
    )
    return pl.pallas_call(
        _dispatch_kernel,
        grid_spec=grid_spec,
        out_shape=jax.ShapeDtypeStruct((P_ROWS, TOK_SUB, LANES), BF16),
        compiler_params=pltpu.CompilerParams(dimension_semantics=("arbitrary",), vmem_limit_bytes=VMEM_LIMIT),
        name="dispatch",
    )(d0, d1, pstart, pend, hn)


W_SLOTS = 3


def _expert_kernel(nu_ref, first_ref, ord_ref, used_ref, nue_ref, x_ref, wg_hbm, wu_hbm, wd_hbm, y_ref,
                   wg_f, wu_f, wd_f, wg_b, wu_b, wd_b, sem):
    i = pl.program_id(0)

    def fetch(k):
        e = used_ref[k]
        slot = k % W_SLOTS
        return (pltpu.make_async_copy(wg_hbm.at[e], wg_f.at[slot], sem.at[slot, 0]),
                pltpu.make_async_copy(wu_hbm.at[e], wu_f.at[slot], sem.at[slot, 1]),
                pltpu.make_async_copy(wd_hbm.at[e], wd_f.at[slot], sem.at[slot, 2]))

    @pl.when(i == 0)
    def _():
        for k in range(W_SLOTS - 1):
            @pl.when(k < nue_ref[0])
            def _():
                for c in fetch(k):
                    c.start()

    @pl.when(i < nu_ref[0])
    def _():
        @pl.when(first_ref[i] == 1)
        def _():
            k = ord_ref[i]
            for c in fetch(k):
                c.wait()

            @pl.when(k + (W_SLOTS - 1) < nue_ref[0])
            def _():
                for c in fetch(k + (W_SLOTS - 1)):
                    c.start()

            slot = k % W_SLOTS
            wg_b[...] = wg_f[slot].astype(BF16)
            wu_b[...] = wu_f[slot].astype(BF16)
            wd_b[...] = wd_f[slot].astype(BF16)

        x = x_ref[...].reshape(ROW_BLOCK, D_MODEL)
        g = _dot(x, wg_b[...])
        u = _dot(x, wu_b[...])
        hmid = (g * _sigmoid(g) * u).astype(BF16)
        y = _dot(hmid, wd_b[...]).astype(BF16)
        y_ref[...] = y.reshape(ROW_BLOCK, TOK_SUB, LANES)

    @pl.when(i >= nu_ref[0])
    def _():
        y_ref[...] = jnp.zeros_like(y_ref)


def _experts(n_used, blk_first, blk_ord, used_e, n_used_e, xin, w_gate, w_up, w_down):
    def xmap(i, nu, *_):
        return (jnp.minimum(i, nu[0] - 1), 0, 0)

    grid_spec = pltpu.PrefetchScalarGridSpec(
        num_scalar_prefetch=5,
        grid=(N_BLOCKS,),
        in_specs=[pl.BlockSpec((ROW_BLOCK, TOK_SUB, LANES), xmap),
                  pl.BlockSpec(memory_space=pl.ANY),
                  pl.BlockSpec(memory_space=pl.ANY),
                  pl.BlockSpec(memory_space=pl.ANY)],
        out_specs=pl.BlockSpec((ROW_BLOCK, TOK_SUB, LANES), lambda i, *_: (i, 0, 0)),
        scratch_shapes=[pltpu.VMEM((W_SLOTS, D_MODEL, D_EXPERT), F32),
                        pltpu.VMEM((W_SLOTS, D_MODEL, D_EXPERT), F32),
                        pltpu.VMEM((W_SLOTS, D_EXPERT, D_MODEL), F32),
                        pltpu.VMEM((D_MODEL, D_EXPERT), BF16),
                        pltpu.VMEM((D_MODEL, D_EXPERT), BF16),
                        pltpu.VMEM((D_EXPERT, D_MODEL), BF16),
                        pltpu.SemaphoreType.DMA((W_SLOTS, 3))],
    )
    return pl.pallas_call(
        _expert_kernel,
        grid_spec=grid_spec,
        out_shape=jax.ShapeDtypeStruct((P_ROWS, TOK_SUB, LANES), BF16),
        compiler_params=pltpu.CompilerParams(dimension_semantics=("arbitrary",), vmem_limit_bytes=VMEM_LIMIT),
        name="experts",
    )(n_used, blk_first, blk_ord, used_e, n_used_e, xin, w_gate, w_up, w_down)


COMB_TM = 256


def _combine_kernel(d0_ref, d1_ref, h1_ref, route_ref, yb_ref, out_ref, g_scr, sem):
    i = pl.program_id(0)
    n = pl.num_programs(0)

    def issue(tile, slot):
        base = tile * COMB_TM

        def start(r, carry):
            pltpu.make_async_copy(yb_ref.at[d0_ref[base + r]], g_scr.at[slot, 0, r], sem.at[slot, 0]).start()
            pltpu.make_async_copy(yb_ref.at[d1_ref[base + r]], g_scr.at[slot, 1, r], sem.at[slot, 1]).start(
                priority=1)
            return carry

        lax.fori_loop(0, COMB_TM, start, 0, unroll=DMA_UNROLL)

    @pl.when(i == 0)
    def _():
        issue(0, 0)

    slot = i % 2

    @pl.when(i + 1 < n)
    def _():
        issue(i + 1, 1 - slot)

    for k in range(2):
        pltpu.make_async_copy(yb_ref.at[pl.ds(0, COMB_TM)], g_scr.at[slot, k], sem.at[slot, k]).wait()

    route = route_ref[...]
    w0 = route[:, 2:3]
    w1 = route[:, 3:4]
    g0 = g_scr[slot, 0].reshape(COMB_TM, D_MODEL).astype(F32)
    g1 = g_scr[slot, 1].reshape(COMB_TM, D_MODEL).astype(F32)
    out_ref[...] = h1_ref[...] + (g0 * w0 + g1 * w1)


def _combine(d0, d1, h1, route, yb):
    grid_spec = pltpu.PrefetchScalarGridSpec(
        num_scalar_prefetch=2,
        grid=(N_TOK // COMB_TM,),
        in_specs=[pl.BlockSpec((COMB_TM, D_MODEL), lambda i, *_: (i, 0)),
                  pl.BlockSpec((COMB_TM, LANES), lambda i, *_: (i, 0)),
                  pl.BlockSpec(memory_space=pl.ANY)],
        out_specs=pl.BlockSpec((COMB_TM, D_MODEL), lambda i, *_: (i, 0)),
        scratch_shapes=[pltpu.VMEM((2, 2, COMB_TM, TOK_SUB, LANES), BF16),
                        pltpu.SemaphoreType.DMA((2, 2))],
    )
    return pl.pallas_call(
        _combine_kernel,
        grid_spec=grid_spec,
        out_shape=jax.ShapeDtypeStruct((N_TOK, D_MODEL), F32),
        compiler_params=pltpu.CompilerParams(dimension_semantics=("arbitrary",), vmem_limit_bytes=VMEM_LIMIT),
        name="combine",
    )(d0, d1, h1, route, yb)


def kernel(x, meta, norm1_g, w_in, b_in, dw_w, dw_b, conv_ln_g, conv_ln_b, w_conv_out, b_conv_out,
           q_norm_g, k_norm_g, b_forget, w_attn_out, w_out, norm2_g, w_group, b_group, w_router,
           b_router, w_gate, w_up, w_down):
    l = 0
    row = lambda v: v.reshape(1, -1).astype(F32)

    c_u, c_q, c_f = 2 * CONV_WIDTH, 2 * CONV_WIDTH, 2 * CONV_WIDTH + 3 * ATTN_WIDTH
    c_gc = c_f + N_HEADS
    c_ga = c_gc + D_MODEL
    wi, bi = w_in[l], b_in[l]
    wu, bu = wi[:, :c_u].astype(BF16), row(bi[:c_u])
    wqkv, bqkv = wi[:, c_q:c_f].astype(BF16), row(bi[c_q:c_f])
    wft = jnp.zeros((16, D_MODEL), F32).at[:N_HEADS].set(wi[:, c_f:c_gc].T).astype(BF16)
    bft = jnp.zeros((16, 1), F32).at[:N_HEADS, 0].set(bi[c_f:c_gc] + b_forget[l])
    wgc, bgc = wi[:, c_gc:c_ga].astype(BF16), row(bi[c_gc:c_ga])
    wga, bga = wi[:, c_ga:].astype(BF16), row(bi[c_ga:])
    qg = row(jnp.tile(q_norm_g[l], N_HEADS) * (HEAD_DIM ** -0.5 * LOG2E))
    kg = row(jnp.tile(k_norm_g[l], N_HEADS))
    head_of_col = jnp.arange(ATTN_WIDTH) // HEAD_DIM
    hsum = (head_of_col[:, None] == jnp.arange(LANES)[None, :]).astype(BF16)
    hexp_half = (jnp.arange(LANES)[:, None] == head_of_col[None, :]).astype(BF16)
    hexp = jnp.concatenate([hexp_half, hexp_half], axis=0)
    proj_consts = (row(norm1_g[l]), wu, bu, wqkv, bqkv, wft, bft, wgc, bgc, wga, bga, qg, kg, hsum, hexp)

    x2d = x.reshape(N_TOK, D_MODEL)
    meta_pad = jnp.zeros((META_ROWS, D_MODEL), F32).at[:N_META].set(meta.astype(F32))
    a, q, k, v, lf, sgc, sga = _proj(x2d, proj_consts, 512)
    a_m, _, k_m, v_m, lf_m, _, _ = _proj(meta_pad, proj_consts, META_ROWS)

    act = _conv(a.reshape(BATCH, SEQ, CONV_WIDTH), a_m, dw_w[l].astype(F32), row(dw_b[l]),
                row(conv_ln_g[l]), row(conv_ln_b[l])).reshape(N_TOK, CONV_WIDTH)

    o = _attention_t(q, k, v, lf, k_m, v_m, lf_m)

    wrt = jnp.zeros((LANES, D_MODEL), F32).at[:N_GROUPS].set(w_group[l].T)
    wrt = wrt.at[N_GROUPS:N_GROUPS + N_EXPERTS].set(w_router[l].T).astype(BF16)
    brt = jnp.zeros((LANES, 1), F32).at[:N_GROUPS, 0].set(b_group[l])
    brt = brt.at[N_GROUPS:N_GROUPS + N_EXPERTS, 0].set(b_router[l])
    tri = jnp.triu(jnp.ones((MERGE_TM, MERGE_TM), F32), 1).astype(BF16)
    merge_consts = (w_conv_out[l].astype(BF16), row(b_conv_out[l]), w_attn_out[l].astype(BF16),
                    w_out[l].astype(BF16), row(norm2_g[l]), wrt, brt, tri)
    h1, hn, route, route_t, cnt = _merge(act, o, sgc, sga, x2d, merge_consts)

    counts = cnt[:N_EXPERTS, 0].astype(jnp.int32)
    padded = (counts + ROW_BLOCK - 1) // ROW_BLOCK * ROW_BLOCK
    pend = jnp.cumsum(padded).astype(jnp.int32)
    pstart = pend - padded
    blk_row = jnp.arange(N_BLOCKS, dtype=jnp.int32) * ROW_BLOCK
    n_used = pend[-1:] // ROW_BLOCK
    nz = counts > 0
    before = (pend[None, :] <= blk_row[:, None]) & nz[None, :]
    blk_ord = jnp.sum(before.astype(jnp.int32), axis=1)
    blk_first = jnp.any((pstart[None, :] == blk_row[:, None]) & nz[None, :], axis=1).astype(jnp.int32)
    e_ord = jnp.cumsum(nz.astype(jnp.int32)) - 1
    e_ids = jnp.arange(N_EXPERTS, dtype=jnp.int32)
    used_e = jnp.sum(jnp.where((e_ord[None, :] == e_ids[:, None]) & nz[None, :], e_ids[None, :], 0), axis=1)
    n_used_e = jnp.sum(nz.astype(jnp.int32)).reshape(1)

    dest = _dest(pstart, route_t)
    d0, d1 = dest[0], dest[1]
    xin = _dispatch(d0, d1, pstart, pend, hn)
    yb = _experts(n_used, blk_first, blk_ord, used_e.astype(jnp.int32), n_used_e, xin,
                  w_gate[l], w_up[l], w_down[l])
    out = _combine(d0, d1, h1, route, yb)
    return out.reshape(BATCH, SEQ, D_MODEL)
```

```python
import functools

import jax
import jax.numpy as jnp
from jax import lax
from jax.experimental import pallas as pl
from jax.experimental.pallas import tpu as pltpu

D_MODEL = 1024
BATCH = 8
SEQ = 2048
N_META = 16
CONV_WIDTH = 512
CONV_K = 31
N_HEADS = 8
HEAD_DIM = 64
ATTN_WIDTH = N_HEADS * HEAD_DIM
N_GROUPS = 8
EXPERTS_PER_GROUP = 8
N_EXPERTS = N_GROUPS * EXPERTS_PER_GROUP
D_EXPERT = 256
RMS_EPS = 1e-6
LN_EPS = 1e-5

N_TOK = BATCH * SEQ
LANES = 128
META_ROWS = 128
ROW_BLOCK = 512
N_ASSIGN = 2 * N_TOK
N_BLOCKS = N_ASSIGN // ROW_BLOCK + N_EXPERTS
P_ROWS = N_BLOCKS * ROW_BLOCK
TOK_SUB = D_MODEL // LANES
RANK_BITS = 16
ATT_TQ = 1024
ATT_TK = 256
N_PAIRS = N_HEADS // 2
assert (ATT_TQ // ATT_TK) % 2 == 0
VMEM_LIMIT = 56 * 1024 * 1024

F32 = jnp.float32
BF16 = jnp.bfloat16
NEG_INF = float("-inf")
LOG2E = 1.4426950408889634


def _dot(a, b):
    return jnp.dot(a, b, preferred_element_type=F32)


def _dot_nt(a, b):
    return lax.dot_general(a, b, (((1,), (1,)), ((), ())), preferred_element_type=F32)


def _sigmoid(x):
    return 1.0 / (1.0 + jnp.exp(-x))


def _const_spec(shape):
    nd = len(shape)
    return pl.BlockSpec(shape, lambda *_: (0,) * nd)


def _head_rms(t, hsum_ref, hexp_ref, gain):
    ss = _dot((t * t).astype(BF16), hsum_ref[...])
    inv = lax.rsqrt(ss * (1.0 / HEAD_DIM) + RMS_EPS)
    hi = inv.astype(BF16)
    lo = (inv - hi.astype(F32)).astype(BF16)
    invb = _dot(jnp.concatenate([hi, lo], axis=1), hexp_ref[...])
    return t * invb * gain


def _proj_kernel(x_ref, g1_ref, wu_ref, bu_ref, wqkv_ref, bqkv_ref, wft_ref, bft_ref, wgc_ref, bgc_ref,
                 wga_ref, bga_ref, qg_ref, kg_ref, hsum_ref, hexp_ref,
                 a_ref, q_ref, k_ref, v_ref, lf_ref, sgc_ref, sga_ref):
    x = x_ref[...]
    ms = jnp.mean(x * x, axis=-1, keepdims=True)
    xn = (x * lax.rsqrt(ms + RMS_EPS) * g1_ref[...]).astype(BF16)

    u = _dot(xn, wu_ref[...]) + bu_ref[...]
    a_ref[...] = (u[:, :CONV_WIDTH] * _sigmoid(u[:, CONV_WIDTH:])).astype(BF16)

    qkv = _dot(xn, wqkv_ref[...]) + bqkv_ref[...]
    q_ref[...] = _head_rms(qkv[:, :ATTN_WIDTH], hsum_ref, hexp_ref, qg_ref[...]).astype(BF16)
    k_ref[...] = _head_rms(qkv[:, ATTN_WIDTH:2 * ATTN_WIDTH], hsum_ref, hexp_ref, kg_ref[...]).astype(BF16)
    v = qkv[:, 2 * ATTN_WIDTH:]
    vb = v_ref.shape[2]
    for c in range(v_ref.shape[0]):
        v_ref[c] = v[c * vb:(c + 1) * vb, :].T.astype(BF16)

    f = _dot_nt(wft_ref[...], xn) + bft_ref[...]
    lf_ref[...] = jnp.minimum(f, 0.0) - jnp.log(1.0 + jnp.exp(-jnp.abs(f)))

    sgc_ref[...] = _sigmoid(_dot(xn, wgc_ref[...]) + bgc_ref[...]).astype(BF16)
    sga_ref[...] = _sigmoid(_dot(xn, wga_ref[...]) + bga_ref[...]).astype(BF16)


def _proj(x2d, consts, tm):
    n = x2d.shape[0]
    row = lambda w: pl.BlockSpec((tm, w), lambda i: (i, 0))
    in_specs = [row(D_MODEL)] + [_const_spec(c.shape) for c in consts]
    vb = min(tm, ATT_TK)
    out_shape = [
        jax.ShapeDtypeStruct((n, CONV_WIDTH), BF16),
        jax.ShapeDtypeStruct((n, ATTN_WIDTH), BF16),
        jax.ShapeDtypeStruct((n, ATTN_WIDTH), BF16),
        jax.ShapeDtypeStruct((n // vb, ATTN_WIDTH, vb), BF16),
        jax.ShapeDtypeStruct((16, n), F32),
        jax.ShapeDtypeStruct((n, D_MODEL), BF16),
        jax.ShapeDtypeStruct((n, D_MODEL), BF16),
    ]
    out_specs = [row(CONV_WIDTH), row(ATTN_WIDTH), row(ATTN_WIDTH),
                 pl.BlockSpec((tm // vb, ATTN_WIDTH, vb), lambda i: (i, 0, 0)),
                 pl.BlockSpec((16, tm), lambda i: (0, i)), row(D_MODEL), row(D_MODEL)]
    return pl.pallas_call(
        _proj_kernel,
        grid=(n // tm,),
        in_specs=in_specs,
        out_specs=out_specs,
        out_shape=out_shape,
        compiler_params=pltpu.CompilerParams(dimension_semantics=("arbitrary",), vmem_limit_bytes=VMEM_LIMIT),
        name="proj",
    )(x2d, *consts)


CONV_CHUNK = 64
CONV_PAD = 32
CONV_WIN = CONV_CHUNK + 32
CONV_SHROWS = CONV_WIN
CONV_LANES = 128


def _conv_kernel(a_ref, am_ref, shift_ref, dw_ref, dwb_ref, lng_ref, lnb_ref, o_ref, pad_scr, sha_scr, shb_scr):
    pad_scr[0:N_META, :] = jnp.zeros((N_META, CONV_WIDTH), BF16)
    pad_scr[N_META:CONV_PAD, :] = am_ref[0:N_META, :]
    pad_scr[CONV_PAD:CONV_PAD + SEQ, :] = a_ref[0]

    def shifted_copies(ci):
        r0 = pl.multiple_of(ci * CONV_CHUNK, CONV_CHUNK)
        return _dot(shift_ref[...], pad_scr[pl.ds(r0, CONV_WIN), :])

    n_chunks = SEQ // CONV_CHUNK
    sha_scr[...] = shifted_copies(0)

    def finish(ci, cur_scr):
        r0 = pl.multiple_of(ci * CONV_CHUNK, CONV_CHUNK)
        parts = []
        for c0 in range(0, CONV_WIDTH, CONV_LANES):
            cs = slice(c0, c0 + CONV_LANES)
            part = jnp.zeros((CONV_CHUNK, CONV_LANES), F32) + dwb_ref[:, cs]
            for j in range(CONV_K):
                off = j + 2
                b = off % 8
                a0 = b * CONV_SHROWS + (off - b)
                part = part + cur_scr[a0:a0 + CONV_CHUNK, cs] * dw_ref[j:j + 1, cs]
            parts.append(part)
        acc = jnp.concatenate(parts, axis=1)
        mu = jnp.mean(acc, axis=-1, keepdims=True)
        d = acc - mu
        var = jnp.mean(d * d, axis=-1, keepdims=True)
        y = d * lax.rsqrt(var + LN_EPS) * lng_ref[...] + lnb_ref[...]
        o_ref[0, pl.ds(r0, CONV_CHUNK), :] = (y * _sigmoid(y)).astype(BF16)

    def chunk_pair(i, carry):
        ci = 2 * i
        shb_scr[...] = shifted_copies(ci + 1)
        finish(ci, sha_scr)
        sha_scr[...] = shifted_copies(jnp.minimum(ci + 2, n_chunks - 1))
        finish(ci + 1, shb_scr)
        return carry

    lax.fori_loop(0, n_chunks // 2, chunk_pair, 0)


def _conv(a3, a_meta, dw_w, dw_b, ln_g, ln_b):
    rr = jnp.arange(8 * CONV_SHROWS)
    shift = ((rr % CONV_SHROWS + rr // CONV_SHROWS)[:, None] == jnp.arange(CONV_WIN)[None, :]).astype(BF16)
    return pl.pallas_call(
        _conv_kernel,
        grid=(BATCH,),
        in_specs=[pl.BlockSpec((1, SEQ, CONV_WIDTH), lambda b: (b, 0, 0)),
                  _const_spec(a_meta.shape), _const_spec(shift.shape), _const_spec(dw_w.shape),
                  _const_spec(dw_b.shape), _const_spec(ln_g.shape), _const_spec(ln_b.shape)],
        out_specs=pl.BlockSpec((1, SEQ, CONV_WIDTH), lambda b: (b, 0, 0)),
        out_shape=jax.ShapeDtypeStruct((BATCH, SEQ, CONV_WIDTH), BF16),
        scratch_shapes=[pltpu.VMEM((CONV_PAD + SEQ, CONV_WIDTH), BF16),
                        pltpu.VMEM((8 * CONV_SHROWS, CONV_WIDTH), F32),
                        pltpu.VMEM((8 * CONV_SHROWS, CONV_WIDTH), F32)],
        compiler_params=pltpu.CompilerParams(dimension_semantics=("arbitrary",), vmem_limit_bytes=VMEM_LIMIT),
        name="conv",
    )(a3, a_meta, shift, dw_w, dw_b, ln_g, ln_b)


def _split3(x):
    hi = x.astype(BF16)
    r1 = x - hi.astype(F32)
    mid = r1.astype(BF16)
    lo = (r1 - mid.astype(F32)).astype(BF16)
    return jnp.concatenate([hi, mid, lo], axis=0)


def _attn_kernel(q_ref, k_ref, v_ref, lf_ref, km_ref, vm_ref, lfm_ref, o_ref,
                 c_scr, cm_scr, m_scr, acc_scr, sa_scr, sb_scr):
    p = pl.program_id(1)
    qi = pl.program_id(2)
    nk = SEQ // ATT_TK

    @pl.when((p == 0) & (qi == 0))
    def _():
        r = lax.broadcasted_iota(jnp.int32, (ATT_TK, ATT_TK), 0)
        c = lax.broadcasted_iota(jnp.int32, (ATT_TK, ATT_TK), 1)
        tri = (r <= c).astype(BF16)
        carry = jnp.zeros((N_HEADS, 1), F32)
        for j in range(nk):
            parts = _dot(_split3(lf_ref[0:N_HEADS, j * ATT_TK:(j + 1) * ATT_TK]), tri)
            cj = parts[0:8] + parts[8:16] + parts[16:24] + carry
            c_scr[j] = cj
            carry = cj[:, ATT_TK - 1:ATT_TK]
        rm = lax.broadcasted_iota(jnp.int32, (LANES, LANES), 0)
        cmx = lax.broadcasted_iota(jnp.int32, (LANES, LANES), 1)
        trim = ((rm > cmx) & (rm < N_META)).astype(BF16)
        pm = _dot(_split3(lfm_ref[0:N_HEADS, :]), trim)
        cm_scr[...] = -(pm[0:8] + pm[8:16] + pm[16:24])

    lane = lax.broadcasted_iota(jnp.int32, (ATT_TQ, LANES), 1)
    q_all = q_ref[...]
    zq = jnp.zeros_like(q_all)
    qm = (jnp.where(lane < HEAD_DIM, q_all, zq), jnp.where(lane >= HEAD_DIM, q_all, zq))
    for hl in range(2):
        m_scr[hl] = jnp.full((ATT_TQ, LANES), NEG_INF, F32)
        acc_scr[hl] = jnp.zeros((ATT_TQ, LANES), F32)

    def scores(kb, crows):
        return tuple(_dot_nt(qm[hl], kb) - crows[hl] for hl in range(2))

    def update(s2, vb, mask):
        vlane = lax.broadcasted_iota(jnp.int32, vb.shape, 1)
        ones = jnp.ones_like(vb)
        for hl in range(2):
            s = s2[hl]
            if mask is not None:
                s = jnp.where(mask, s, NEG_INF)
            m_prev = m_scr[hl]
            m_next = jnp.maximum(m_prev, jnp.max(s, axis=1, keepdims=True))
            reps = s.shape[1] // LANES
            m_wide = m_next if reps == 1 else jnp.concatenate([m_next] * reps, axis=1)
            pr = jnp.exp(s - m_wide).astype(BF16)
            alpha = jnp.exp(m_prev - m_next)
            own = (vlane < HEAD_DIM) if hl == 0 else (vlane >= HEAD_DIM)
            acc_scr[hl] = acc_scr[hl] * alpha + _dot(pr, jnp.where(own, vb, ones))
            m_scr[hl] = m_next

    def crows_at(j):
        return (c_scr[j, pl.ds(2 * p, 1), :], c_scr[j, pl.ds(2 * p + 1, 1), :])

    def kblock(j):
        return k_ref[pl.ds(pl.multiple_of(j * ATT_TK, ATT_TK), ATT_TK), :]

    def vblock(j):
        return v_ref[pl.ds(pl.multiple_of(j * ATT_TK, ATT_TK), ATT_TK), :]

    update(scores(km_ref[...], (cm_scr[pl.ds(2 * p, 1), :], cm_scr[pl.ds(2 * p + 1, 1), :])), vm_ref[...],
           lane < N_META)

    nfull = qi * (ATT_TQ // ATT_TK)

    def put(scr, s2):
        scr[0] = s2[0]
        scr[1] = s2[1]

    put(sa_scr, scores(kblock(0), crows_at(0)))

    def full2(i, carry):
        j = 2 * i
        put(sb_scr, scores(kblock(j + 1), crows_at(j + 1)))
        update((sa_scr[0], sa_scr[1]), vblock(j), None)
        put(sa_scr, scores(kblock(j + 2), crows_at(j + 2)))
        update((sb_scr[0], sb_scr[1]), vblock(j + 1), None)
        return carry

    lax.fori_loop(0, nfull // 2, full2, 0)

    rows = lax.broadcasted_iota(jnp.int32, (ATT_TQ, ATT_TK), 0)
    cols = lax.broadcasted_iota(jnp.int32, (ATT_TQ, ATT_TK), 1)
    s_diag = [(sa_scr[0], sa_scr[1])]
    for dj in range(1, ATT_TQ // ATT_TK):
        s_diag.append(scores(kblock(nfull + dj), crows_at(nfull + dj)))
    for dj in range(ATT_TQ // ATT_TK):
        update(s_diag[dj], vblock(nfull + dj), cols + dj * ATT_TK <= rows)

    a0 = acc_scr[0]
    a1 = acc_scr[1]
    o0 = a0 / pltpu.roll(a0, HEAD_DIM, axis=1)
    o1 = a1 / pltpu.roll(a1, HEAD_DIM, axis=1)
    o_ref[...] = jnp.where(lane < HEAD_DIM, o0, o1).astype(BF16)


def _attention(q, k, v, lf, km, vm, lfm):
    nq = SEQ // ATT_TQ
    return pl.pallas_call(
        _attn_kernel,
        grid=(BATCH, N_PAIRS, nq),
        in_specs=[
            pl.BlockSpec((ATT_TQ, LANES), lambda b, p, i: (b * nq + i, p)),
            pl.BlockSpec((SEQ, LANES), lambda b, p, i: (b, p)),
            pl.BlockSpec((SEQ, LANES), lambda b, p, i: (b, p)),
            pl.BlockSpec((16, SEQ), lambda b, p, i: (0, b)),
            pl.BlockSpec((META_ROWS, LANES), lambda b, p, i: (0, p)),
            pl.BlockSpec((META_ROWS, LANES), lambda b, p, i: (0, p)),
            pl.BlockSpec((16, META_ROWS), lambda b, p, i: (0, 0)),
        ],
        out_specs=pl.BlockSpec((ATT_TQ, LANES), lambda b, p, i: (b * nq + i, p)),
        out_shape=jax.ShapeDtypeStruct((N_TOK, ATTN_WIDTH), BF16),
        scratch_shapes=[
            pltpu.VMEM((SEQ // ATT_TK, N_HEADS, ATT_TK), F32),
            pltpu.VMEM((N_HEADS, LANES), F32),
            pltpu.VMEM((2, ATT_TQ, LANES), F32),
            pltpu.VMEM((2, ATT_TQ, LANES), F32),
            pltpu.VMEM((2, ATT_TQ, ATT_TK), F32),
            pltpu.VMEM((2, ATT_TQ, ATT_TK), F32),
        ],
        compiler_params=pltpu.CompilerParams(
            dimension_semantics=("arbitrary", "arbitrary", "arbitrary"), vmem_limit_bytes=VMEM_LIMIT),
        name="attn",
    )(q, k, v, lf, km, vm, lfm)


AUG_TERMS = 3


def _split_terms(x):
    hi = x.astype(BF16)
    r1 = x - hi.astype(F32)
    mid = r1.astype(BF16)
    lo = (r1 - mid.astype(F32)).astype(BF16)
    return hi, mid, lo


def _attn_t_kernel(q_ref, k_ref, vt_ref, lf_ref, km_ref, vm_ref, lfm_ref, place_ref, o_ref,
                   aug_scr, augm_scr, m_scr, acc_scr, sa_scr, sb_scr):
    p = pl.program_id(1)
    qi = pl.program_id(2)
    nk = SEQ // ATT_TK

    def bias_lanes(colsum):
        out = None
        for t, term in enumerate(_split_terms(colsum)):
            d = _dot(term, place_ref[t])
            out = d if out is None else out + d
        return out.astype(BF16)

    @pl.when((p == 0) & (qi == 0))
    def _():
        r = lax.broadcasted_iota(jnp.int32, (ATT_TK, ATT_TK), 0)
        c = lax.broadcasted_iota(jnp.int32, (ATT_TK, ATT_TK), 1)
        tri = (c <= r).astype(BF16)
        local = []
        for j in range(nk):
            lf = lf_ref[:, j * ATT_TK:(j + 1) * ATT_TK]
            csum = None
            for term in _split_terms(lf):
                d = _dot_nt(tri, term)
                csum = d if csum is None else csum + d
            local.append(csum)
        carry = jnp.zeros((1, 16), F32)
        for j in range(nk):
            csum = local[j] + carry
            aug_scr[j * ATT_TK:(j + 1) * ATT_TK, :] = bias_lanes(csum * (-LOG2E))
            carry = csum[ATT_TK - 1:ATT_TK, :]
        rm = lax.broadcasted_iota(jnp.int32, (META_ROWS, META_ROWS), 0)
        cm = lax.broadcasted_iota(jnp.int32, (META_ROWS, META_ROWS), 1)
        trim = ((cm > rm) & (cm < N_META)).astype(BF16)
        msum = jnp.zeros((META_ROWS, 16), F32)
        for term in _split_terms(lfm_ref[...]):
            msum = msum + _dot_nt(trim, term)
        augm_scr[...] = bias_lanes(msum * LOG2E)

    lane = lax.broadcasted_iota(jnp.int32, (ATT_TQ, LANES), 1)
    q_all = q_ref[...]
    zq = jnp.zeros_like(q_all)
    q_cat = []
    for hl in range(2):
        own = (lane < HEAD_DIM) if hl == 0 else (lane >= HEAD_DIM)
        h = 2 * p + hl
        bias_sel = (lane >= AUG_TERMS * h) & (lane < AUG_TERMS * h + AUG_TERMS)
        bias_ones = jnp.where(bias_sel, 1.0, 0.0).astype(BF16)
        q_cat.append(jnp.concatenate([jnp.where(own, q_all, zq), bias_ones], axis=1))
        m_scr[hl] = jnp.full((1, ATT_TQ), NEG_INF, F32)
        acc_scr[hl] = jnp.zeros((LANES, ATT_TQ), F32)

    def scores(kb, augb, lo=0):
        k_cat = jnp.concatenate([kb, augb], axis=1)
        return tuple(_dot_nt(k_cat, q_cat[hl][lo:]) for hl in range(2))

    def update(s2, vtb, mask, lo=0):
        vrow = lax.broadcasted_iota(jnp.int32, vtb.shape, 0)
        ones = jnp.ones_like(vtb)
        for hl in range(2):
            s = s2[hl]
            if mask is not None:
                s = jnp.where(mask, s, NEG_INF)
            m_prev = m_scr[hl, :, lo:]
            m_next = jnp.maximum(m_prev, jnp.max(s, axis=0, keepdims=True))
            pr = jnp.exp2(s - m_next).astype(BF16)
            alpha = jnp.exp2(m_prev - m_next)
            own = (vrow < HEAD_DIM) if hl == 0 else (vrow >= HEAD_DIM)
            acc_scr[hl, :, lo:] = acc_scr[hl, :, lo:] * alpha + _dot(jnp.where(own, vtb, ones), pr)
            m_scr[hl, :, lo:] = m_next

    def kblock(j):
        return k_ref[pl.ds(pl.multiple_of(j * ATT_TK, ATT_TK), ATT_TK), :]

    def augblock(j):
        return aug_scr[pl.ds(pl.multiple_of(j * ATT_TK, ATT_TK), ATT_TK), :]

    nfull = qi * (ATT_TQ // ATT_TK)

    def put(scr, s2):
        scr[0] = s2[0]
        scr[1] = s2[1]

    s_meta = scores(km_ref[0:N_META, :], augm_scr[0:N_META, :])
    put(sa_scr, scores(kblock(0), augblock(0)))
    update(s_meta, vm_ref[0][:, 0:N_META], None)

    def full2(i, carry):
        j = 2 * i
        put(sb_scr, scores(kblock(j + 1), augblock(j + 1)))
        update((sa_scr[0], sa_scr[1]), vt_ref[j], None)
        put(sa_scr, scores(kblock(j + 2), augblock(j + 2)))
        update((sb_scr[0], sb_scr[1]), vt_ref[j + 1], None)
        return carry

    lax.fori_loop(0, nfull // 2, full2, 0)

    srow = lax.broadcasted_iota(jnp.int32, (ATT_TK, ATT_TQ), 0)
    tcol = lax.broadcasted_iota(jnp.int32, (ATT_TK, ATT_TQ), 1)
    s_diag = [(sa_scr[0], sa_scr[1])]
    for dj in range(1, ATT_TQ // ATT_TK):
        s_diag.append(scores(kblock(nfull + dj), augblock(nfull + dj), lo=dj * ATT_TK))
    for dj in range(ATT_TQ // ATT_TK):
        lo = dj * ATT_TK
        update(s_diag[dj], vt_ref[nfull + dj], (srow <= tcol)[:, :ATT_TQ - lo], lo=lo)

    a0 = acc_scr[0]
    a1 = acc_scr[1]
    ot = jnp.concatenate([a0[0:HEAD_DIM] / a0[HEAD_DIM:], a1[HEAD_DIM:] / a1[0:HEAD_DIM]], axis=0)
    o_ref[...] = ot.T.astype(BF16)


def _attention_t(q, k, vt, lf, km, vm, lfm):
    nq = SEQ // ATT_TQ
    nk = SEQ // ATT_TK
    hh = jnp.arange(16)[:, None]
    ll = jnp.arange(LANES)[None, :]
    place = jnp.stack([((ll == AUG_TERMS * hh + t) & (hh < N_HEADS)).astype(BF16) for t in range(AUG_TERMS)])
    return pl.pallas_call(
        _attn_t_kernel,
        grid=(BATCH, N_PAIRS, nq),
        in_specs=[
            pl.BlockSpec((ATT_TQ, LANES), lambda b, p, i: (b * nq + i, p)),
            pl.BlockSpec((SEQ, LANES), lambda b, p, i: (b, p)),
            pl.BlockSpec((nk, LANES, ATT_TK), lambda b, p, i: (b, p, 0)),
            pl.BlockSpec((16, SEQ), lambda b, p, i: (0, b)),
            pl.BlockSpec((META_ROWS, LANES), lambda b, p, i: (0, p)),
            pl.BlockSpec((1, LANES, META_ROWS), lambda b, p, i: (0, p, 0)),
            pl.BlockSpec((16, META_ROWS), lambda b, p, i: (0, 0)),
            _const_spec(place.shape),
        ],
        out_specs=pl.BlockSpec((ATT_TQ, LANES), lambda b, p, i: (b * nq + i, p)),
        out_shape=jax.ShapeDtypeStruct((N_TOK, ATTN_WIDTH), BF16),
        scratch_shapes=[
            pltpu.VMEM((SEQ, LANES), BF16),
            pltpu.VMEM((META_ROWS, LANES), BF16),
            pltpu.VMEM((2, 1, ATT_TQ), F32),
            pltpu.VMEM((2, LANES, ATT_TQ), F32),
            pltpu.VMEM((2, ATT_TK, ATT_TQ), F32),
            pltpu.VMEM((2, ATT_TK, ATT_TQ), F32),
        ],
        compiler_params=pltpu.CompilerParams(
            dimension_semantics=("arbitrary", "arbitrary", "arbitrary"), vmem_limit_bytes=VMEM_LIMIT),
        name="attn",
    )(q, k, vt, lf, km, vm, lfm, place)


MERGE_TM = 512


def _merge_kernel(act_ref, o_ref, sgc_ref, sga_ref, x_ref, wpw_ref, bpw_ref, wao_ref, wout_ref, g2_ref,
                  wrt_ref, brt_ref, tri_ref,
                  h1_ref, hn_ref, route_ref, routet_ref, cnt_ref, base_scr):
    i = pl.program_id(0)

    @pl.when(i == 0)
    def _():
        base_scr[...] = jnp.zeros_like(base_scr)

    y_conv = _dot(act_ref[...], wpw_ref[...]) + bpw_ref[...]
    y_attn = _dot(o_ref[...], wao_ref[...])
    merged = sgc_ref[...].astype(F32) * y_conv + sga_ref[...].astype(F32) * y_attn
    h1 = x_ref[...] + _dot(merged.astype(BF16), wout_ref[...])
    h1_ref[...] = h1
    ms = jnp.mean(h1 * h1, axis=-1, keepdims=True)
    hn = (h1 * lax.rsqrt(ms + RMS_EPS) * g2_ref[...]).astype(BF16)
    hn_ref[...] = hn.reshape(hn.shape[0], TOK_SUB, LANES)

    logits = _dot_nt(wrt_ref[...], hn) + brt_ref[...]
    row = lax.broadcasted_iota(jnp.int32, logits.shape, 0).astype(F32)
    big = float(LANES)
    is_g = row < N_GROUPS
    gl = jnp.where(is_g, logits, NEG_INF)
    gmax = jnp.max(gl, axis=0, keepdims=True)
    gidx = jnp.min(jnp.where(gl == gmax, row, big), axis=0, keepdims=True)
    gsum = jnp.sum(jnp.where(is_g, jnp.exp(gl - gmax), 0.0), axis=0, keepdims=True)
    g_w = 1.0 / gsum
    lo = N_GROUPS + EXPERTS_PER_GROUP * gidx
    el = jnp.where((row >= lo) & (row < lo + EXPERTS_PER_GROUP), logits, NEG_INF)
    t1 = jnp.max(el, axis=0, keepdims=True)
    i1 = jnp.min(jnp.where(el == t1, row, big), axis=0, keepdims=True)
    el2 = jnp.where(row == i1, NEG_INF, el)
    t2 = jnp.max(el2, axis=0, keepdims=True)
    i2 = jnp.min(jnp.where(el2 == t2, row, big), axis=0, keepdims=True)
    e2 = jnp.exp(t2 - t1)
    den = 1.0 + e2
    w1 = (1.0 / den) * g_w
    w2 = (e2 / den) * g_w
    ea = i1 - N_GROUPS
    eb = i2 - N_GROUPS

    oha = (row == ea).astype(F32)
    ohb = (row == eb).astype(F32)
    ohs = oha + ohb
    before = _dot(ohs.astype(BF16), tri_ref[...]) + base_scr[...][:, 0:1]
    ra = jnp.sum(before * oha, axis=0, keepdims=True)
    rb = jnp.sum(before * ohb, axis=0, keepdims=True)
    base_scr[...] = base_scr[...] + jnp.sum(ohs, axis=1, keepdims=True)
    cnt_ref[...] = base_scr[...]

    def rows_of(shape):
        r = lax.broadcasted_iota(jnp.int32, shape, 0)
        out = jnp.where(r == 0, ea, 0.0)
        for k, val in enumerate((eb, w1, w2, ra, rb), start=1):
            out = jnp.where(r == k, val, out)
        return out

    routet_ref[...] = rows_of((8, logits.shape[1]))
    route_ref[...] = rows_of(logits.shape).T


def _merge(act, o, sgc, sga, x2d, consts):
    tm = MERGE_TM
    row = lambda w: pl.BlockSpec((tm, w), lambda i: (i, 0))
    return pl.pallas_call(
        _merge_kernel,
        grid=(N_TOK // tm,),
        in_specs=[row(CONV_WIDTH), row(ATTN_WIDTH), row(D_MODEL), row(D_MODEL), row(D_MODEL)]
        + [_const_spec(c.shape) for c in consts],
        out_specs=[row(D_MODEL), pl.BlockSpec((tm, TOK_SUB, LANES), lambda i: (i, 0, 0)), row(LANES),
                   pl.BlockSpec((8, tm), lambda i: (0, i)), _const_spec((LANES, LANES))],
        out_shape=[jax.ShapeDtypeStruct((N_TOK, D_MODEL), F32),
                   jax.ShapeDtypeStruct((N_TOK, TOK_SUB, LANES), BF16),
                   jax.ShapeDtypeStruct((N_TOK, LANES), F32),
                   jax.ShapeDtypeStruct((8, N_TOK), F32),
                   jax.ShapeDtypeStruct((LANES, LANES), F32)],
        scratch_shapes=[pltpu.VMEM((LANES, LANES), F32)],
        compiler_params=pltpu.CompilerParams(dimension_semantics=("arbitrary",), vmem_limit_bytes=VMEM_LIMIT),
        name="merge",
    )(act, o, sgc, sga, x2d, *consts)


DISP_TM = 1024
DMA_UNROLL = 8


def _dest_kernel(pstart_ref, rt_ref, d_ref):
    rt = rt_ref[...]
    start = jnp.zeros_like(rt)
    for e in range(N_EXPERTS):
        start = jnp.where(rt == float(e), pstart_ref[e].astype(F32), start)
    d_ref[...] = (start + pltpu.roll(rt, 4, axis=0)).astype(jnp.int32)


def _dest(pstart, route_t):
    grid_spec = pltpu.PrefetchScalarGridSpec(
        num_scalar_prefetch=1,
        grid=(1,),
        in_specs=[pl.BlockSpec(route_t.shape, lambda i, *_: (0, 0))],
        out_specs=pl.BlockSpec(route_t.shape, lambda i, *_: (0, 0)),
    )
    return pl.pallas_call(
        _dest_kernel,
        grid_spec=grid_spec,
        out_shape=jax.ShapeDtypeStruct(route_t.shape, jnp.int32),
        compiler_params=pltpu.CompilerParams(dimension_semantics=("arbitrary",), vmem_limit_bytes=VMEM_LIMIT),
        name="dest",
    )(pstart, route_t)


def _dispatch_kernel(d0_ref, d1_ref, pstart_ref, pend_ref, hn_ref, xin_ref, zero_scr, sem, zsem):
    i = pl.program_id(0)
    base = i * DISP_TM

    @pl.when(i == 0)
    def _():
        zero_scr[...] = jnp.zeros_like(zero_scr)

        def zcopy(e):
            start = pl.multiple_of(pend_ref[e] - ROW_BLOCK, ROW_BLOCK)
            return pltpu.make_async_copy(zero_scr, xin_ref.at[pl.ds(start, ROW_BLOCK)], zsem.at[0])

        def zstart(e, carry):
            @pl.when(pend_ref[e] > pstart_ref[e])
            def _():
                zcopy(e).start()
            return carry

        def zwait(e, carry):
            @pl.when(pend_ref[e] > pstart_ref[e])
            def _():
                zcopy(e).wait()
            return carry

        n_used = pend_ref[N_EXPERTS - 1] // ROW_BLOCK

        def tcopy(b):
            return pltpu.make_async_copy(
                zero_scr, xin_ref.at[pl.ds(pl.multiple_of(b * ROW_BLOCK, ROW_BLOCK), ROW_BLOCK)], zsem.at[0])

        def tstart(b, carry):
            tcopy(b).start()
            return carry

        def twait(b, carry):
            tcopy(b).wait()
            return carry

        lax.fori_loop(0, N_EXPERTS, zstart, 0)
        lax.fori_loop(n_used, N_BLOCKS, tstart, 0)
        lax.fori_loop(0, N_EXPERTS, zwait, 0)
        lax.fori_loop(n_used, N_BLOCKS, twait, 0)

    def start(r, carry):
        pltpu.make_async_copy(hn_ref.at[r], xin_ref.at[d0_ref[base + r]], sem.at[0]).start()
        pltpu.make_async_copy(hn_ref.at[r], xin_ref.at[d1_ref[base + r]], sem.at[1]).start(priority=1)
        return carry

    lax.fori_loop(0, DISP_TM, start, 0, unroll=DMA_UNROLL)
    pltpu.make_async_copy(hn_ref, xin_ref.at[pl.ds(0, DISP_TM)], sem.at[0]).wait()
    pltpu.make_async_copy(hn_ref, xin_ref.at[pl.ds(0, DISP_TM)], sem.at[1]).wait()


def _dispatch(d0, d1, pstart, pend, hn):
    grid_spec = pltpu.PrefetchScalarGridSpec(
        num_scalar_prefetch=4,
        grid=(N_TOK // DISP_TM,),
        in_specs=[pl.BlockSpec((DISP_TM, TOK_SUB, LANES), lambda i, *_: (i, 0, 0))],
        out_specs=pl.BlockSpec(memory_space=pl.ANY),
        scratch_shapes=[pltpu.VMEM((ROW_BLOCK, TOK_SUB, LANES), BF16),
                        pltpu.SemaphoreType.DMA((2,)), pltpu.SemaphoreType.DMA((1,))],
    )
    return pl.pallas_call(
        _dispatch_kernel,
        grid_spec=grid_spec,
        out_shape=jax.ShapeDtypeStruct((P_ROWS, TOK_SUB, LANES), BF16),
        compiler_params=pltpu.CompilerParams(dimension_semantics=("arbitrary",), vmem_limit_bytes=VMEM_LIMIT),
        name="dispatch",
    )(d0, d1, pstart, pend, hn)


W_SLOTS = 3


def _expert_kernel(nu_ref, first_ref, ord_ref, used_ref, nue_ref, x_hbm, wg_hbm, wu_hbm, wd_hbm, y_hbm,
                   x_buf, y_buf, wg_f, wu_f, wd_f, wg_b, wu_b, wd_b, xsem, ysem, wsem):
    n = nu_ref[0]

    def rows(i):
        return pl.ds(pl.multiple_of(i * ROW_BLOCK, ROW_BLOCK), ROW_BLOCK)

    def xcopy(i, slot):
        return pltpu.make_async_copy(x_hbm.at[rows(i)], x_buf.at[slot], xsem.at[slot])

    def ycopy(i, slot):
        return pltpu.make_async_copy(y_buf.at[slot], y_hbm.at[rows(i)], ysem.at[slot])

    def fetch(k):
        e = used_ref[k]
        slot = k % W_SLOTS
        return (pltpu.make_async_copy(wg_hbm.at[e], wg_f.at[slot], wsem.at[slot, 0]),
                pltpu.make_async_copy(wu_hbm.at[e], wu_f.at[slot], wsem.at[slot, 1]),
                pltpu.make_async_copy(wd_hbm.at[e], wd_f.at[slot], wsem.at[slot, 2]))

    for k in range(W_SLOTS - 1):
        @pl.when(k < nue_ref[0])
        def _():
            for c in fetch(k):
                c.start()
    xcopy(0, 0).start()

    def block(i, carry):
        slot = i % 2
        xcopy(i, slot).wait()

        @pl.when(i + 1 < n)
        def _():
            xcopy(i + 1, 1 - slot).start()

        @pl.when(first_ref[i] == 1)
        def _():
            k = ord_ref[i]
            for c in fetch(k):
                c.wait()

            @pl.when(k + (W_SLOTS - 1) < nue_ref[0])
            def _():
                for c in fetch(k + (W_SLOTS - 1)):
                    c.start()

            wslot = k % W_SLOTS
            wg_b[...] = wg_f[wslot].astype(BF16)
            wu_b[...] = wu_f[wslot].astype(BF16)
            wd_b[...] = wd_f[wslot].astype(BF16)

        @pl.when(i >= 2)
        def _():
            ycopy(i - 2, slot).wait()

        x = x_buf[slot].reshape(ROW_BLOCK, D_MODEL)
        g = _dot(x, wg_b[...])
        u = _dot(x, wu_b[...])
        hmid = (g * _sigmoid(g) * u).astype(BF16)
        y = _dot(hmid, wd_b[...]).astype(BF16)
        y_buf[slot] = y.reshape(ROW_BLOCK, TOK_SUB, LANES)
        ycopy(i, slot).start()
        return carry

    lax.fori_loop(0, n, block, 0)

    @pl.when(n >= 2)
    def _():
        ycopy(n - 2, n % 2).wait()
    ycopy(n - 1, (n - 1) % 2).wait()

    y_buf[0] = jnp.zeros((ROW_BLOCK, TOK_SUB, LANES), BF16)

    def zcopy(b):
        return pltpu.make_async_copy(y_buf.at[0], y_hbm.at[rows(b)], ysem.at[0])

    def zstart(b, carry):
        zcopy(b).start()
        return carry

    def zwait(b, carry):
        zcopy(b).wait()
        return carry

    lax.fori_loop(n, N_BLOCKS, zstart, 0)
    lax.fori_loop(n, N_BLOCKS, zwait, 0)


def _experts(n_used, blk_first, blk_ord, used_e, n_used_e, xin, w_gate, w_up, w_down):
    grid_spec = pltpu.PrefetchScalarGridSpec(
        num_scalar_prefetch=5,
        grid=(1,),
        in_specs=[pl.BlockSpec(memory_space=pl.ANY),
                  pl.BlockSpec(memory_space=pl.ANY),
                  pl.BlockSpec(memory_space=pl.ANY),
                  pl.BlockSpec(memory_space=pl.ANY)],
        out_specs=pl.BlockSpec(memory_space=pl.ANY),
        scratch_shapes=[pltpu.VMEM((2, ROW_BLOCK, TOK_SUB, LANES), BF16),
                        pltpu.VMEM((2, ROW_BLOCK, TOK_SUB, LANES), BF16),
                        pltpu.VMEM((W_SLOTS, D_MODEL, D_EXPERT), F32),
                        pltpu.VMEM((W_SLOTS, D_MODEL, D_EXPERT), F32),
                        pltpu.VMEM((W_SLOTS, D_EXPERT, D_MODEL), F32),
                        pltpu.VMEM((D_MODEL, D_EXPERT), BF16),
                        pltpu.VMEM((D_MODEL, D_EXPERT), BF16),
                        pltpu.VMEM((D_EXPERT, D_MODEL), BF16),
                        pltpu.SemaphoreType.DMA((2,)), pltpu.SemaphoreType.DMA((2,)),
                        pltpu.SemaphoreType.DMA((W_SLOTS, 3))],
    )
    return pl.pallas_call(
        _expert_kernel,
        grid_spec=grid_spec,
        out_shape=jax.ShapeDtypeStruct((P_ROWS, TOK_SUB, LANES), BF16),
        compiler_params=pltpu.CompilerParams(dimension_semantics=("arbitrary",), vmem_limit_bytes=VMEM_LIMIT),
        name="experts",
    )(n_used, blk_first, blk_ord, used_e, n_used_e, xin, w_gate, w_up, w_down)


COMB_TM = 512


def _combine_kernel(d0_ref, d1_ref, h1_ref, route_ref, yb_ref, out_ref, g_scr, sem):
    i = pl.program_id(0)
    n = pl.num_programs(0)

    def issue(tile, slot):
        base = tile * COMB_TM

        def start(r, carry):
            pltpu.make_async_copy(yb_ref.at[d0_ref[base + r]], g_scr.at[slot, 0, r], sem.at[slot, 0]).start()
            pltpu.make_async_copy(yb_ref.at[d1_ref[base + r]], g_scr.at[slot, 1, r], sem.at[slot, 1]).start(
                priority=1)
            return carry

        lax.fori_loop(0, COMB_TM, start, 0, unroll=DMA_UNROLL)

    @pl.when(i == 0)
    def _():
        issue(0, 0)

    slot = i % 2

    @pl.when(i + 1 < n)
    def _():
        issue(i + 1, 1 - slot)

    for k in range(2):
        pltpu.make_async_copy(yb_ref.at[pl.ds(0, COMB_TM)], g_scr.at[slot, k], sem.at[slot, k]).wait()

    route = route_ref[...]
    w0 = route[:, 2:3]
    w1 = route[:, 3:4]
    g0 = g_scr[slot, 0].reshape(COMB_TM, D_MODEL).astype(F32)
    g1 = g_scr[slot, 1].reshape(COMB_TM, D_MODEL).astype(F32)
    out_ref[...] = h1_ref[...] + (g0 * w0 + g1 * w1)


def _combine(d0, d1, h1, route, yb):
    grid_spec = pltpu.PrefetchScalarGridSpec(
        num_scalar_prefetch=2,
        grid=(N_TOK // COMB_TM,),
        in_specs=[pl.BlockSpec((COMB_TM, D_MODEL), lambda i, *_: (i, 0)),
                  pl.BlockSpec((COMB_TM, LANES), lambda i, *_: (i, 0)),
                  pl.BlockSpec(memory_space=pl.ANY)],
        out_specs=pl.BlockSpec((COMB_TM, D_MODEL), lambda i, *_: (i, 0)),
        scratch_shapes=[pltpu.VMEM((2, 2, COMB_TM, TOK_SUB, LANES), BF16),
                        pltpu.SemaphoreType.DMA((2, 2))],
    )
    return pl.pallas_call(
        _combine_kernel,
        grid_spec=grid_spec,
        out_shape=jax.ShapeDtypeStruct((N_TOK, D_MODEL), F32),
        compiler_params=pltpu.CompilerParams(dimension_semantics=("arbitrary",), vmem_limit_bytes=VMEM_LIMIT),
        name="combine",
    )(d0, d1, h1, route, yb)


def kernel(x, meta, norm1_g, w_in, b_in, dw_w, dw_b, conv_ln_g, conv_ln_b, w_conv_out, b_conv_out,
           q_norm_g, k_norm_g, b_forget, w_attn_out, w_out, norm2_g, w_group, b_group, w_router,
           b_router, w_gate, w_up, w_down):
    l = 0
    row = lambda v: v.reshape(1, -1).astype(F32)

    c_u, c_q, c_f = 2 * CONV_WIDTH, 2 * CONV_WIDTH, 2 * CONV_WIDTH + 3 * ATTN_WIDTH
    c_gc = c_f + N_HEADS
    c_ga = c_gc + D_MODEL
    wi, bi = w_in[l], b_in[l]
    wu, bu = wi[:, :c_u].astype(BF16), row(bi[:c_u])
    wqkv, bqkv = wi[:, c_q:c_f].astype(BF16), row(bi[c_q:c_f])
    wft = jnp.zeros((16, D_MODEL), F32).at[:N_HEADS].set(wi[:, c_f:c_gc].T).astype(BF16)
    bft = jnp.zeros((16, 1), F32).at[:N_HEADS, 0].set(bi[c_f:c_gc] + b_forget[l])
    wgc, bgc = wi[:, c_gc:c_ga].astype(BF16), row(bi[c_gc:c_ga])
    wga, bga = wi[:, c_ga:].astype(BF16), row(bi[c_ga:])
    qg = row(jnp.tile(q_norm_g[l], N_HEADS) * (HEAD_DIM ** -0.5 * LOG2E))
    kg = row(jnp.tile(k_norm_g[l], N_HEADS))
    head_of_col = jnp.arange(ATTN_WIDTH) // HEAD_DIM
    hsum = (head_of_col[:, None] == jnp.arange(LANES)[None, :]).astype(BF16)
    hexp_half = (jnp.arange(LANES)[:, None] == head_of_col[None, :]).astype(BF16)
    hexp = jnp.concatenate([hexp_half, hexp_half], axis=0)
    proj_consts = (row(norm1_g[l]), wu, bu, wqkv, bqkv, wft, bft, wgc, bgc, wga, bga, qg, kg, hsum, hexp)

    x2d = x.reshape(N_TOK, D_MODEL)
    meta_pad = jnp.zeros((META_ROWS, D_MODEL), F32).at[:N_META].set(meta.astype(F32))
    a, q, k, v, lf, sgc, sga = _proj(x2d, proj_consts, 512)
    a_m, _, k_m, v_m, lf_m, _, _ = _proj(meta_pad, proj_consts, META_ROWS)

    act = _conv(a.reshape(BATCH, SEQ, CONV_WIDTH), a_m, dw_w[l].astype(F32), row(dw_b[l]),
                row(conv_ln_g[l]), row(conv_ln_b[l])).reshape(N_TOK, CONV_WIDTH)

    o = _attention_t(q, k, v, lf, k_m, v_m, lf_m)

    wrt = jnp.zeros((LANES, D_MODEL), F32).at[:N_GROUPS].set(w_group[l].T)
    wrt = wrt.at[N_GROUPS:N_GROUPS + N_EXPERTS].set(w_router[l].T).astype(BF16)
    brt = jnp.zeros((LANES, 1), F32).at[:N_GROUPS, 0].set(b_group[l])
    brt = brt.at[N_GROUPS:N_GROUPS + N_EXPERTS, 0].set(b_router[l])
    tri = jnp.triu(jnp.ones((MERGE_TM, MERGE_TM), F32), 1).astype(BF16)
    merge_consts = (w_conv_out[l].astype(BF16), row(b_conv_out[l]), w_attn_out[l].astype(BF16),
                    w_out[l].astype(BF16), row(norm2_g[l]), wrt, brt, tri)
    h1, hn, route, route_t, cnt = _merge(act, o, sgc, sga, x2d, merge_consts)

    counts = cnt[:N_EXPERTS, 0].astype(jnp.int32)
    padded = (counts + ROW_BLOCK - 1) // ROW_BLOCK * ROW_BLOCK
    pend = jnp.cumsum(padded).astype(jnp.int32)
    pstart = pend - padded
    blk_row = jnp.arange(N_BLOCKS, dtype=jnp.int32) * ROW_BLOCK
    n_used = pend[-1:] // ROW_BLOCK
    nz = counts > 0
    before = (pend[None, :] <= blk_row[:, None]) & nz[None, :]
    blk_ord = jnp.sum(before.astype(jnp.int32), axis=1)
    blk_first = jnp.any((pstart[None, :] == blk_row[:, None]) & nz[None, :], axis=1).astype(jnp.int32)
    e_ord = jnp.cumsum(nz.astype(jnp.int32)) - 1
    e_ids = jnp.arange(N_EXPERTS, dtype=jnp.int32)
    used_e = jnp.sum(jnp.where((e_ord[None, :] == e_ids[:, None]) & nz[None, :], e_ids[None, :], 0), axis=1)
    n_used_e = jnp.sum(nz.astype(jnp.int32)).reshape(1)

    dest = _dest(pstart, route_t)
    d0, d1 = dest[0], dest[1]
    xin = _dispatch(d0, d1, pstart, pend, hn)
    yb = _experts(n_used, blk_first, blk_ord, used_e.astype(jnp.int32), n_used_e, xin,
                  w_gate[l], w_up[l], w_down[l])
    out = _combine(d0, d1, h1, route, yb)
    return out.reshape(BATCH, SEQ, D_MODEL)
```

```python
import jax
import jax.numpy as jnp
from jax import lax
from jax.experimental import pallas as pl
from jax.experimental.pallas import tpu as pltpu

D_MODEL = 1024
BATCH = 8
SEQ = 2048
N_META = 16
CONV_WIDTH = 512
CONV_K = 31
N_HEADS = 8
HEAD_DIM = 64
ATTN_WIDTH = N_HEADS * HEAD_DIM
N_GROUPS = 8
EXPERTS_PER_GROUP = 8
N_EXPERTS = N_GROUPS * EXPERTS_PER_GROUP
D_EXPERT = 256
RMS_EPS = 1e-6
LN_EPS = 1e-5

N_TOK = BATCH * SEQ
LANES = 128
META_ROWS = 128
ROW_BLOCK = 512
N_ASSIGN = 2 * N_TOK
N_BLOCKS = N_ASSIGN // ROW_BLOCK + N_EXPERTS
P_ROWS = N_BLOCKS * ROW_BLOCK
TOK_SUB = D_MODEL // LANES
PROJ_TM = 1024
ATT_TQ = 1024
ATT_TK = 256
N_PAIRS = N_HEADS // 2
assert (ATT_TQ // ATT_TK) % 2 == 0
VMEM_LIMIT = 56 * 1024 * 1024

F32 = jnp.float32
BF16 = jnp.bfloat16
NEG_INF = float("-inf")
LOG2E = 1.4426950408889634


def _dot(a, b):
    return jnp.dot(a, b, preferred_element_type=F32)


def _dot_nt(a, b):
    return lax.dot_general(a, b, (((1,), (1,)), ((), ())), preferred_element_type=F32)


def _sigmoid(x):
    return 1.0 / (1.0 + jnp.exp(-x))


def _const_spec(shape):
    nd = len(shape)
    return pl.BlockSpec(shape, lambda *_: (0,) * nd)


def _head_rms(t, hsum_ref, hexp_ref, gain):
    ss = _dot((t * t).astype(BF16), hsum_ref[...])
    inv = lax.rsqrt(ss * (1.0 / HEAD_DIM) + RMS_EPS)
    hi = inv.astype(BF16)
    lo = (inv - hi.astype(F32)).astype(BF16)
    invb = _dot(jnp.concatenate([hi, lo], axis=1), hexp_ref[...])
    return t * invb * gain


def _proj_kernel(x_ref, g1_ref, wu_ref, bu_ref, wqkv_ref, bqkv_ref, wft_ref, bft_ref, wgc_ref, bgc_ref,
                 wga_ref, bga_ref, qg_ref, kg_ref, hsum_ref, hexp_ref,
                 a_ref, q_ref, k_ref, v_ref, lf_ref, sgc_ref, sga_ref):
    x = x_ref[...]
    ms = jnp.mean(x * x, axis=-1, keepdims=True)
    xn = (x * lax.rsqrt(ms + RMS_EPS) * g1_ref[...]).astype(BF16)

    u = _dot(xn, wu_ref[...]) + bu_ref[...]
    a_ref[...] = (u[:, :CONV_WIDTH] * _sigmoid(u[:, CONV_WIDTH:])).astype(BF16)

    qkv = _dot(xn, wqkv_ref[...]) + bqkv_ref[...]
    q_ref[...] = _head_rms(qkv[:, :ATTN_WIDTH], hsum_ref, hexp_ref, qg_ref[...]).astype(BF16)
    k_ref[...] = _head_rms(qkv[:, ATTN_WIDTH:2 * ATTN_WIDTH], hsum_ref, hexp_ref, kg_ref[...]).astype(BF16)
    v = qkv[:, 2 * ATTN_WIDTH:]
    vb = v_ref.shape[2]
    for c in range(v_ref.shape[0]):
        v_ref[c] = v[c * vb:(c + 1) * vb, :].T.astype(BF16)

    f = _dot_nt(wft_ref[...], xn) + bft_ref[...]
    lf_ref[...] = jnp.minimum(f, 0.0) - jnp.log(1.0 + jnp.exp(-jnp.abs(f)))

    sgc_ref[...] = _sigmoid(_dot(xn, wgc_ref[...]) + bgc_ref[...]).astype(BF16)
    sga_ref[...] = _sigmoid(_dot(xn, wga_ref[...]) + bga_ref[...]).astype(BF16)


def _proj(x2d, consts, tm):
    n = x2d.shape[0]
    row = lambda w: pl.BlockSpec((tm, w), lambda i: (i, 0))
    in_specs = [row(D_MODEL)] + [_const_spec(c.shape) for c in consts]
    vb = min(tm, ATT_TK)
    out_shape = [
        jax.ShapeDtypeStruct((n, CONV_WIDTH), BF16),
        jax.ShapeDtypeStruct((n, ATTN_WIDTH), BF16),
        jax.ShapeDtypeStruct((n, ATTN_WIDTH), BF16),
        jax.ShapeDtypeStruct((n // vb, ATTN_WIDTH, vb), BF16),
        jax.ShapeDtypeStruct((16, n), F32),
        jax.ShapeDtypeStruct((n, D_MODEL), BF16),
        jax.ShapeDtypeStruct((n, D_MODEL), BF16),
    ]
    out_specs = [row(CONV_WIDTH), row(ATTN_WIDTH), row(ATTN_WIDTH),
                 pl.BlockSpec((tm // vb, ATTN_WIDTH, vb), lambda i: (i, 0, 0)),
                 pl.BlockSpec((16, tm), lambda i: (0, i)), row(D_MODEL), row(D_MODEL)]
    return pl.pallas_call(
        _proj_kernel,
        grid=(n // tm,),
        in_specs=in_specs,
        out_specs=out_specs,
        out_shape=out_shape,
        compiler_params=pltpu.CompilerParams(dimension_semantics=("arbitrary",), vmem_limit_bytes=VMEM_LIMIT),
        name="proj",
    )(x2d, *consts)


CONV_CHUNK = 64
CONV_PAD = 32
CONV_WIN = CONV_CHUNK + 32
CONV_SHROWS = CONV_WIN
CONV_LANES = 128


def _conv_kernel(a_ref, am_ref, shift_ref, dw_ref, dwb_ref, lng_ref, lnb_ref, o_ref, pad_scr, sha_scr, shb_scr):
    pad_scr[0:N_META, :] = jnp.zeros((N_META, CONV_WIDTH), BF16)
    pad_scr[N_META:CONV_PAD, :] = am_ref[0:N_META, :]
    pad_scr[CONV_PAD:CONV_PAD + SEQ, :] = a_ref[0]

    def shifted_copies(ci):
        r0 = pl.multiple_of(ci * CONV_CHUNK, CONV_CHUNK)
        return _dot(shift_ref[...], pad_scr[pl.ds(r0, CONV_WIN), :])

    n_chunks = SEQ // CONV_CHUNK
    sha_scr[...] = shifted_copies(0)

    def finish(ci, cur_scr):
        r0 = pl.multiple_of(ci * CONV_CHUNK, CONV_CHUNK)
        parts = []
        for c0 in range(0, CONV_WIDTH, CONV_LANES):
            cs = slice(c0, c0 + CONV_LANES)
            part = jnp.zeros((CONV_CHUNK, CONV_LANES), F32) + dwb_ref[:, cs]
            for j in range(CONV_K):
                off = j + 2
                b = off % 8
                a0 = b * CONV_SHROWS + (off - b)
                part = part + cur_scr[a0:a0 + CONV_CHUNK, cs] * dw_ref[j:j + 1, cs]
            parts.append(part)
        acc = jnp.concatenate(parts, axis=1)
        mu = jnp.mean(acc, axis=-1, keepdims=True)
        d = acc - mu
        var = jnp.mean(d * d, axis=-1, keepdims=True)
        y = d * lax.rsqrt(var + LN_EPS) * lng_ref[...] + lnb_ref[...]
        o_ref[0, pl.ds(r0, CONV_CHUNK), :] = (y * _sigmoid(y)).astype(BF16)

    def chunk_pair(i, carry):
        ci = 2 * i
        shb_scr[...] = shifted_copies(ci + 1)
        finish(ci, sha_scr)
        sha_scr[...] = shifted_copies(jnp.minimum(ci + 2, n_chunks - 1))
        finish(ci + 1, shb_scr)
        return carry

    lax.fori_loop(0, n_chunks // 2, chunk_pair, 0)


def _conv(a3, a_meta, dw_w, dw_b, ln_g, ln_b):
    rr = jnp.arange(8 * CONV_SHROWS)
    shift = ((rr % CONV_SHROWS + rr // CONV_SHROWS)[:, None] == jnp.arange(CONV_WIN)[None, :]).astype(BF16)
    return pl.pallas_call(
        _conv_kernel,
        grid=(BATCH,),
        in_specs=[pl.BlockSpec((1, SEQ, CONV_WIDTH), lambda b: (b, 0, 0)),
                  _const_spec(a_meta.shape), _const_spec(shift.shape), _const_spec(dw_w.shape),
                  _const_spec(dw_b.shape), _const_spec(ln_g.shape), _const_spec(ln_b.shape)],
        out_specs=pl.BlockSpec((1, SEQ, CONV_WIDTH), lambda b: (b, 0, 0)),
        out_shape=jax.ShapeDtypeStruct((BATCH, SEQ, CONV_WIDTH), BF16),
        scratch_shapes=[pltpu.VMEM((CONV_PAD + SEQ, CONV_WIDTH), BF16),
                        pltpu.VMEM((8 * CONV_SHROWS, CONV_WIDTH), F32),
                        pltpu.VMEM((8 * CONV_SHROWS, CONV_WIDTH), F32)],
        compiler_params=pltpu.CompilerParams(dimension_semantics=("arbitrary",), vmem_limit_bytes=VMEM_LIMIT),
        name="conv",
    )(a3, a_meta, shift, dw_w, dw_b, ln_g, ln_b)


AUG_TERMS = 3


def _split_terms(x):
    hi = x.astype(BF16)
    r1 = x - hi.astype(F32)
    mid = r1.astype(BF16)
    lo = (r1 - mid.astype(F32)).astype(BF16)
    return hi, mid, lo


def _attn_t_kernel(q_ref, k_ref, vt_ref, lf_ref, km_ref, vm_ref, lfm_ref, place_ref, o_ref,
                   aug_scr, augm_scr, m_scr, acc_scr, sa_scr, sb_scr):
    p = pl.program_id(1)
    qi = pl.program_id(2)
    nk = SEQ // ATT_TK

    def bias_lanes(colsum):
        out = None
        for t, term in enumerate(_split_terms(colsum)):
            d = _dot(term, place_ref[t])
            out = d if out is None else out + d
        return out.astype(BF16)

    @pl.when((p == 0) & (qi == 0))
    def _():
        r = lax.broadcasted_iota(jnp.int32, (ATT_TK, ATT_TK), 0)
        c = lax.broadcasted_iota(jnp.int32, (ATT_TK, ATT_TK), 1)
        tri = (c <= r).astype(BF16)
        local = []
        for j in range(nk):
            lf = lf_ref[:, j * ATT_TK:(j + 1) * ATT_TK]
            csum = None
            for term in _split_terms(lf):
                d = _dot_nt(tri, term)
                csum = d if csum is None else csum + d
            local.append(csum)
        carry = jnp.zeros((1, 16), F32)
        for j in range(nk):
            csum = local[j] + carry
            aug_scr[j * ATT_TK:(j + 1) * ATT_TK, :] = bias_lanes(csum * (-LOG2E))
            carry = csum[ATT_TK - 1:ATT_TK, :]
        rm = lax.broadcasted_iota(jnp.int32, (META_ROWS, META_ROWS), 0)
        cm = lax.broadcasted_iota(jnp.int32, (META_ROWS, META_ROWS), 1)
        trim = ((cm > rm) & (cm < N_META)).astype(BF16)
        msum = jnp.zeros((META_ROWS, 16), F32)
        for term in _split_terms(lfm_ref[...]):
            msum = msum + _dot_nt(trim, term)
        augm_scr[...] = bias_lanes(msum * LOG2E)

    lane = lax.broadcasted_iota(jnp.int32, (ATT_TQ, LANES), 1)
    q_all = q_ref[...]
    zq = jnp.zeros_like(q_all)
    q_cat = []
    for hl in range(2):
        own = (lane < HEAD_DIM) if hl == 0 else (lane >= HEAD_DIM)
        h = 2 * p + hl
        bias_sel = (lane >= AUG_TERMS * h) & (lane < AUG_TERMS * h + AUG_TERMS)
        bias_ones = jnp.where(bias_sel, 1.0, 0.0).astype(BF16)
        q_cat.append(jnp.concatenate([jnp.where(own, q_all, zq), bias_ones], axis=1))
        m_scr[hl] = jnp.full((1, ATT_TQ), NEG_INF, F32)
        acc_scr[hl] = jnp.zeros((LANES, ATT_TQ), F32)

    def scores(kb, augb, lo=0):
        k_cat = jnp.concatenate([kb, augb], axis=1)
        return tuple(_dot_nt(k_cat, q_cat[hl][lo:]) for hl in range(2))

    def update(s2, vtb, mask, lo=0):
        vrow = lax.broadcasted_iota(jnp.int32, vtb.shape, 0)
        ones = jnp.ones_like(vtb)
        for hl in range(2):
            s = s2[hl]
            if mask is not None:
                s = jnp.where(mask, s, NEG_INF)
            m_prev = m_scr[hl, :, lo:]
            m_next = jnp.maximum(m_prev, jnp.max(s, axis=0, keepdims=True))
            pr = jnp.exp2(s - m_next).astype(BF16)
            alpha = jnp.exp2(m_prev - m_next)
            own = (vrow < HEAD_DIM) if hl == 0 else (vrow >= HEAD_DIM)
            acc_scr[hl, :, lo:] = acc_scr[hl, :, lo:] * alpha + _dot(jnp.where(own, vtb, ones), pr)
            m_scr[hl, :, lo:] = m_next

    def kblock(j):
        return k_ref[pl.ds(pl.multiple_of(j * ATT_TK, ATT_TK), ATT_TK), :]

    def augblock(j):
        return aug_scr[pl.ds(pl.multiple_of(j * ATT_TK, ATT_TK), ATT_TK), :]

    nfull = qi * (ATT_TQ // ATT_TK)

    def put(scr, s2):
        scr[0] = s2[0]
        scr[1] = s2[1]

    s_meta = scores(km_ref[0:N_META, :], augm_scr[0:N_META, :])
    put(sa_scr, scores(kblock(0), augblock(0)))
    update(s_meta, vm_ref[0][:, 0:N_META], None)

    def full2(i, carry):
        j = 2 * i
        put(sb_scr, scores(kblock(j + 1), augblock(j + 1)))
        update((sa_scr[0], sa_scr[1]), vt_ref[j], None)
        put(sa_scr, scores(kblock(j + 2), augblock(j + 2)))
        update((sb_scr[0], sb_scr[1]), vt_ref[j + 1], None)
        return carry

    lax.fori_loop(0, nfull // 2, full2, 0)

    srow = lax.broadcasted_iota(jnp.int32, (ATT_TK, ATT_TQ), 0)
    tcol = lax.broadcasted_iota(jnp.int32, (ATT_TK, ATT_TQ), 1)
    s_diag = [(sa_scr[0], sa_scr[1])]
    for dj in range(1, ATT_TQ // ATT_TK):
        s_diag.append(scores(kblock(nfull + dj), augblock(nfull + dj), lo=dj * ATT_TK))
    for dj in range(ATT_TQ // ATT_TK):
        lo = dj * ATT_TK
        update(s_diag[dj], vt_ref[nfull + dj], (srow <= tcol)[:, :ATT_TQ - lo], lo=lo)

    a0 = acc_scr[0]
    a1 = acc_scr[1]
    ot = jnp.concatenate([a0[0:HEAD_DIM] / a0[HEAD_DIM:], a1[HEAD_DIM:] / a1[0:HEAD_DIM]], axis=0)
    o_ref[...] = ot.T.astype(BF16)


def _attention_t(q, k, vt, lf, km, vm, lfm):
    nq = SEQ // ATT_TQ
    nk = SEQ // ATT_TK
    hh = jnp.arange(16)[:, None]
    ll = jnp.arange(LANES)[None, :]
    place = jnp.stack([((ll == AUG_TERMS * hh + t) & (hh < N_HEADS)).astype(BF16) for t in range(AUG_TERMS)])
    return pl.pallas_call(
        _attn_t_kernel,
        grid=(BATCH, N_PAIRS, nq),
        in_specs=[
            pl.BlockSpec((ATT_TQ, LANES), lambda b, p, i: (b * nq + i, p)),
            pl.BlockSpec((SEQ, LANES), lambda b, p, i: (b, p)),
            pl.BlockSpec((nk, LANES, ATT_TK), lambda b, p, i: (b, p, 0)),
            pl.BlockSpec((16, SEQ), lambda b, p, i: (0, b)),
            pl.BlockSpec((META_ROWS, LANES), lambda b, p, i: (0, p)),
            pl.BlockSpec((1, LANES, META_ROWS), lambda b, p, i: (0, p, 0)),
            pl.BlockSpec((16, META_ROWS), lambda b, p, i: (0, 0)),
            _const_spec(place.shape),
        ],
        out_specs=pl.BlockSpec((ATT_TQ, LANES), lambda b, p, i: (b * nq + i, p)),
        out_shape=jax.ShapeDtypeStruct((N_TOK, ATTN_WIDTH), BF16),
        scratch_shapes=[
            pltpu.VMEM((SEQ, LANES), BF16),
            pltpu.VMEM((META_ROWS, LANES), BF16),
            pltpu.VMEM((2, 1, ATT_TQ), F32),
            pltpu.VMEM((2, LANES, ATT_TQ), F32),
            pltpu.VMEM((2, ATT_TK, ATT_TQ), F32),
            pltpu.VMEM((2, ATT_TK, ATT_TQ), F32),
        ],
        compiler_params=pltpu.CompilerParams(
            dimension_semantics=("arbitrary", "arbitrary", "arbitrary"), vmem_limit_bytes=VMEM_LIMIT),
        name="attn",
    )(q, k, vt, lf, km, vm, lfm, place)


MERGE_TM = 1024


def _merge_kernel(act_ref, o_ref, sgc_ref, sga_ref, x_ref, wpw_ref, bpw_ref, wao_ref, wout_ref, g2_ref,
                  wrt_ref, brt_ref, tri_ref,
                  h1_ref, hn_ref, route_ref, routet_ref, cnt_ref, base_scr):
    i = pl.program_id(0)

    @pl.when(i == 0)
    def _():
        base_scr[...] = jnp.zeros_like(base_scr)

    y_conv = _dot(act_ref[...], wpw_ref[...]) + bpw_ref[...]
    y_attn = _dot(o_ref[...], wao_ref[...])
    merged = sgc_ref[...].astype(F32) * y_conv + sga_ref[...].astype(F32) * y_attn
    h1 = x_ref[...] + _dot(merged.astype(BF16), wout_ref[...])
    h1_ref[...] = h1
    ms = jnp.mean(h1 * h1, axis=-1, keepdims=True)
    hn = (h1 * lax.rsqrt(ms + RMS_EPS) * g2_ref[...]).astype(BF16)
    hn_ref[...] = hn.reshape(hn.shape[0], TOK_SUB, LANES)

    logits = _dot_nt(wrt_ref[...], hn) + brt_ref[...]
    row = lax.broadcasted_iota(jnp.int32, logits.shape, 0).astype(F32)
    big = float(LANES)
    is_g = row < N_GROUPS
    gl = jnp.where(is_g, logits, NEG_INF)
    gmax = jnp.max(gl, axis=0, keepdims=True)
    gidx = jnp.min(jnp.where(gl == gmax, row, big), axis=0, keepdims=True)
    gsum = jnp.sum(jnp.where(is_g, jnp.exp(gl - gmax), 0.0), axis=0, keepdims=True)
    g_w = 1.0 / gsum
    lo = N_GROUPS + EXPERTS_PER_GROUP * gidx
    el = jnp.where((row >= lo) & (row < lo + EXPERTS_PER_GROUP), logits, NEG_INF)
    t1 = jnp.max(el, axis=0, keepdims=True)
    i1 = jnp.min(jnp.where(el == t1, row, big), axis=0, keepdims=True)
    el2 = jnp.where(row == i1, NEG_INF, el)
    t2 = jnp.max(el2, axis=0, keepdims=True)
    i2 = jnp.min(jnp.where(el2 == t2, row, big), axis=0, keepdims=True)
    e2 = jnp.exp(t2 - t1)
    den = 1.0 + e2
    w1 = (1.0 / den) * g_w
    w2 = (e2 / den) * g_w
    ea = i1 - N_GROUPS
    eb = i2 - N_GROUPS

    oha = (row == ea).astype(F32)
    ohb = (row == eb).astype(F32)
    ohs = oha + ohb
    before = _dot(ohs.astype(BF16), tri_ref[...]) + base_scr[...][:, 0:1]
    ra = jnp.sum(before * oha, axis=0, keepdims=True)
    rb = jnp.sum(before * ohb, axis=0, keepdims=True)
    base_scr[...] = base_scr[...] + jnp.sum(ohs, axis=1, keepdims=True)
    cnt_ref[...] = base_scr[...]

    def rows_of(shape):
        r = lax.broadcasted_iota(jnp.int32, shape, 0)
        out = jnp.where(r == 0, ea, 0.0)
        for k, val in enumerate((eb, w1, w2, ra, rb), start=1):
            out = jnp.where(r == k, val, out)
        return out

    routet_ref[...] = rows_of((8, logits.shape[1]))
    route_ref[...] = rows_of(logits.shape).T


def _merge(act, o, sgc, sga, x2d, consts):
    tm = MERGE_TM
    row = lambda w: pl.BlockSpec((tm, w), lambda i: (i, 0))
    return pl.pallas_call(
        _merge_kernel,
        grid=(N_TOK // tm,),
        in_specs=[row(CONV_WIDTH), row(ATTN_WIDTH), row(D_MODEL), row(D_MODEL), row(D_MODEL)]
        + [_const_spec(c.shape) for c in consts],
        out_specs=[row(D_MODEL), pl.BlockSpec((tm, TOK_SUB, LANES), lambda i: (i, 0, 0)), row(LANES),
                   pl.BlockSpec((8, tm), lambda i: (0, i)), _const_spec((LANES, LANES))],
        out_shape=[jax.ShapeDtypeStruct((N_TOK, D_MODEL), F32),
                   jax.ShapeDtypeStruct((N_TOK, TOK_SUB, LANES), BF16),
                   jax.ShapeDtypeStruct((N_TOK, LANES), F32),
                   jax.ShapeDtypeStruct((8, N_TOK), F32),
                   jax.ShapeDtypeStruct((LANES, LANES), F32)],
        scratch_shapes=[pltpu.VMEM((LANES, LANES), F32)],
        compiler_params=pltpu.CompilerParams(dimension_semantics=("arbitrary",), vmem_limit_bytes=VMEM_LIMIT),
        name="merge",
    )(act, o, sgc, sga, x2d, *consts)


DISP_TM = 1024
DMA_UNROLL = 8


def _dest_kernel(pstart_ref, rt_ref, d_ref):
    rt = rt_ref[...]
    start = jnp.zeros_like(rt)
    for e in range(N_EXPERTS):
        start = jnp.where(rt == float(e), pstart_ref[e].astype(F32), start)
    d_ref[...] = (start + pltpu.roll(rt, 4, axis=0)).astype(jnp.int32)


def _dest(pstart, route_t):
    grid_spec = pltpu.PrefetchScalarGridSpec(
        num_scalar_prefetch=1,
        grid=(1,),
        in_specs=[pl.BlockSpec(route_t.shape, lambda i, *_: (0, 0))],
        out_specs=pl.BlockSpec(route_t.shape, lambda i, *_: (0, 0)),
    )
    return pl.pallas_call(
        _dest_kernel,
        grid_spec=grid_spec,
        out_shape=jax.ShapeDtypeStruct(route_t.shape, jnp.int32),
        compiler_params=pltpu.CompilerParams(dimension_semantics=("arbitrary",), vmem_limit_bytes=VMEM_LIMIT),
        name="dest",
    )(pstart, route_t)


def _dispatch_kernel(d0_ref, d1_ref, pstart_ref, pend_ref, hn_ref, xin_ref, zero_scr, sem, zsem):
    i = pl.program_id(0)
    base = i * DISP_TM

    @pl.when(i == 0)
    def _():
        zero_scr[...] = jnp.zeros_like(zero_scr)

        def zcopy(e):
            start = pl.multiple_of(pend_ref[e] - ROW_BLOCK, ROW_BLOCK)
            return pltpu.make_async_copy(zero_scr, xin_ref.at[pl.ds(start, ROW_BLOCK)], zsem.at[0])

        def zstart(e, carry):
            @pl.when(pend_ref[e] > pstart_ref[e])
            def _():
                zcopy(e).start()
            return carry

        def zwait(e, carry):
            @pl.when(pend_ref[e] > pstart_ref[e])
            def _():
                zcopy(e).wait()
            return carry

        n_used = pend_ref[N_EXPERTS - 1] // ROW_BLOCK

        def tcopy(b):
            return pltpu.make_async_copy(
                zero_scr, xin_ref.at[pl.ds(pl.multiple_of(b * ROW_BLOCK, ROW_BLOCK), ROW_BLOCK)], zsem.at[0])

        def tstart(b, carry):
            tcopy(b).start()
            return carry

        def twait(b, carry):
            tcopy(b).wait()
            return carry

        lax.fori_loop(0, N_EXPERTS, zstart, 0)
        lax.fori_loop(n_used, N_BLOCKS, tstart, 0)
        lax.fori_loop(0, N_EXPERTS, zwait, 0)
        lax.fori_loop(n_used, N_BLOCKS, twait, 0)

    def start(r, carry):
        pltpu.make_async_copy(hn_ref.at[r], xin_ref.at[d0_ref[base + r]], sem.at[0]).start()
        pltpu.make_async_copy(hn_ref.at[r], xin_ref.at[d1_ref[base + r]], sem.at[1]).start(priority=1)
        return carry

    lax.fori_loop(0, DISP_TM, start, 0, unroll=DMA_UNROLL)
    pltpu.make_async_copy(hn_ref, xin_ref.at[pl.ds(0, DISP_TM)], sem.at[0]).wait()
    pltpu.make_async_copy(hn_ref, xin_ref.at[pl.ds(0, DISP_TM)], sem.at[1]).wait()


def _dispatch(d0, d1, pstart, pend, hn):
    grid_spec = pltpu.PrefetchScalarGridSpec(
        num_scalar_prefetch=4,
        grid=(N_TOK // DISP_TM,),
        in_specs=[pl.BlockSpec((DISP_TM, TOK_SUB, LANES), lambda i, *_: (i, 0, 0))],
        out_specs=pl.BlockSpec(memory_space=pl.ANY),
        scratch_shapes=[pltpu.VMEM((ROW_BLOCK, TOK_SUB, LANES), BF16),
                        pltpu.SemaphoreType.DMA((2,)), pltpu.SemaphoreType.DMA((1,))],
    )
    return pl.pallas_call(
        _dispatch_kernel,
        grid_spec=grid_spec,
        out_shape=jax.ShapeDtypeStruct((P_ROWS, TOK_SUB, LANES), BF16),
        compiler_params=pltpu.CompilerParams(dimension_semantics=("arbitrary",), vmem_limit_bytes=VMEM_LIMIT),
        name="dispatch",
    )(d0, d1, pstart, pend, hn)


W_SLOTS = 3


def _expert_kernel(nu_ref, first_ref, ord_ref, used_ref, nue_ref, x_hbm, wg_hbm, wu_hbm, wd_hbm, y_hbm,
                   x_buf, y_buf, wg_f, wu_f, wd_f, wg_b, wu_b, wd_b, xsem, ysem, wsem):
    n = nu_ref[0]

    def rows(i):
        return pl.ds(pl.multiple_of(i * ROW_BLOCK, ROW_BLOCK), ROW_BLOCK)

    def xcopy(i, slot):
        return pltpu.make_async_copy(x_hbm.at[rows(i)], x_buf.at[slot], xsem.at[slot])

    def ycopy(i, slot):
        return pltpu.make_async_copy(y_buf.at[slot], y_hbm.at[rows(i)], ysem.at[slot])

    def fetch(k):
        e = used_ref[k]
        slot = k % W_SLOTS
        return (pltpu.make_async_copy(wg_hbm.at[e], wg_f.at[slot], wsem.at[slot, 0]),
                pltpu.make_async_copy(wu_hbm.at[e], wu_f.at[slot], wsem.at[slot, 1]),
                pltpu.make_async_copy(wd_hbm.at[e], wd_f.at[slot], wsem.at[slot, 2]))

    for k in range(W_SLOTS - 1):
        @pl.when(k < nue_ref[0])
        def _():
            for c in fetch(k):
                c.start()
    xcopy(0, 0).start()

    def block(i, carry):
        slot = i % 2
        xcopy(i, slot).wait()

        @pl.when(i + 1 < n)
        def _():
            xcopy(i + 1, 1 - slot).start()

        @pl.when(first_ref[i] == 1)
        def _():
            k = ord_ref[i]
            for c in fetch(k):
                c.wait()

            @pl.when(k + (W_SLOTS - 1) < nue_ref[0])
            def _():
                for c in fetch(k + (W_SLOTS - 1)):
                    c.start()

            wslot = k % W_SLOTS
            wg_b[...] = wg_f[wslot].astype(BF16)
            wu_b[...] = wu_f[wslot].astype(BF16)
            wd_b[...] = wd_f[wslot].astype(BF16)

        @pl.when(i >= 2)
        def _():
            ycopy(i - 2, slot).wait()

        x = x_buf[slot].reshape(ROW_BLOCK, D_MODEL)
        g = _dot(x, wg_b[...])
        u = _dot(x, wu_b[...])
        hmid = (g * _sigmoid(g) * u).astype(BF16)
        y = _dot(hmid, wd_b[...]).astype(BF16)
        y_buf[slot] = y.reshape(ROW_BLOCK, TOK_SUB, LANES)
        ycopy(i, slot).start()
        return carry

    lax.fori_loop(0, n, block, 0)

    @pl.when(n >= 2)
    def _():
        ycopy(n - 2, n % 2).wait()
    ycopy(n - 1, (n - 1) % 2).wait()

    y_buf[0] = jnp.zeros((ROW_BLOCK, TOK_SUB, LANES), BF16)

    def zcopy(b):
        return pltpu.make_async_copy(y_buf.at[0], y_hbm.at[rows(b)], ysem.at[0])

    def zstart(b, carry):
        zcopy(b).start()
        return carry

    def zwait(b, carry):
        zcopy(b).wait()
        return carry

    lax.fori_loop(n, N_BLOCKS, zstart, 0)
    lax.fori_loop(n, N_BLOCKS, zwait, 0)


def _experts(n_used, blk_first, blk_ord, used_e, n_used_e, xin, w_gate, w_up, w_down):
    grid_spec = pltpu.PrefetchScalarGridSpec(
        num_scalar_prefetch=5,
        grid=(1,),
        in_specs=[pl.BlockSpec(memory_space=pl.ANY),
                  pl.BlockSpec(memory_space=pl.ANY),
                  pl.BlockSpec(memory_space=pl.ANY),
                  pl.BlockSpec(memory_space=pl.ANY)],
        out_specs=pl.BlockSpec(memory_space=pl.ANY),
        scratch_shapes=[pltpu.VMEM((2, ROW_BLOCK, TOK_SUB, LANES), BF16),
                        pltpu.VMEM((2, ROW_BLOCK, TOK_SUB, LANES), BF16),
                        pltpu.VMEM((W_SLOTS, D_MODEL, D_EXPERT), F32),
                        pltpu.VMEM((W_SLOTS, D_MODEL, D_EXPERT), F32),
                        pltpu.VMEM((W_SLOTS, D_EXPERT, D_MODEL), F32),
                        pltpu.VMEM((D_MODEL, D_EXPERT), BF16),
                        pltpu.VMEM((D_MODEL, D_EXPERT), BF16),
                        pltpu.VMEM((D_EXPERT, D_MODEL), BF16),
                        pltpu.SemaphoreType.DMA((2,)), pltpu.SemaphoreType.DMA((2,)),
                        pltpu.SemaphoreType.DMA((W_SLOTS, 3))],
    )
    return pl.pallas_call(
        _expert_kernel,
        grid_spec=grid_spec,
        out_shape=jax.ShapeDtypeStruct((P_ROWS, TOK_SUB, LANES), BF16),
        compiler_params=pltpu.CompilerParams(dimension_semantics=("arbitrary",), vmem_limit_bytes=VMEM_LIMIT),
        name="experts",
    )(n_used, blk_first, blk_ord, used_e, n_used_e, xin, w_gate, w_up, w_down)


COMB_TM = 512


def _combine_kernel(d0_ref, d1_ref, h1_ref, route_ref, yb_ref, out_ref, g_scr, sem):
    i = pl.program_id(0)
    n = pl.num_programs(0)

    def issue(tile, slot):
        base = tile * COMB_TM

        def start(r, carry):
            pltpu.make_async_copy(yb_ref.at[d0_ref[base + r]], g_scr.at[slot, 0, r], sem.at[slot, 0]).start()
            pltpu.make_async_copy(yb_ref.at[d1_ref[base + r]], g_scr.at[slot, 1, r], sem.at[slot, 1]).start(
                priority=1)
            return carry

        lax.fori_loop(0, COMB_TM, start, 0, unroll=DMA_UNROLL)

    @pl.when(i == 0)
    def _():
        issue(0, 0)

    slot = i % 2

    @pl.when(i + 1 < n)
    def _():
        issue(i + 1, 1 - slot)

    for k in range(2):
        pltpu.make_async_copy(yb_ref.at[pl.ds(0, COMB_TM)], g_scr.at[slot, k], sem.at[slot, k]).wait()

    route = route_ref[...]
    w0 = route[:, 2:3]
    w1 = route[:, 3:4]
    g0 = g_scr[slot, 0].reshape(COMB_TM, D_MODEL).astype(F32)
    g1 = g_scr[slot, 1].reshape(COMB_TM, D_MODEL).astype(F32)
    out_ref[...] = h1_ref[...] + (g0 * w0 + g1 * w1)


def _combine(d0, d1, h1, route, yb):
    grid_spec = pltpu.PrefetchScalarGridSpec(
        num_scalar_prefetch=2,
        grid=(N_TOK // COMB_TM,),
        in_specs=[pl.BlockSpec((COMB_TM, D_MODEL), lambda i, *_: (i, 0)),
                  pl.BlockSpec((COMB_TM, LANES), lambda i, *_: (i, 0)),
                  pl.BlockSpec(memory_space=pl.ANY)],
        out_specs=pl.BlockSpec((COMB_TM, D_MODEL), lambda i, *_: (i, 0)),
        scratch_shapes=[pltpu.VMEM((2, 2, COMB_TM, TOK_SUB, LANES), BF16),
                        pltpu.SemaphoreType.DMA((2, 2))],
    )
    return pl.pallas_call(
        _combine_kernel,
        grid_spec=grid_spec,
        out_shape=jax.ShapeDtypeStruct((N_TOK, D_MODEL), F32),
        compiler_params=pltpu.CompilerParams(dimension_semantics=("arbitrary",), vmem_limit_bytes=VMEM_LIMIT),
        name="combine",
    )(d0, d1, h1, route, yb)


def kernel(x, meta, norm1_g, w_in, b_in, dw_w, dw_b, conv_ln_g, conv_ln_b, w_conv_out, b_conv_out,
           q_norm_g, k_norm_g, b_forget, w_attn_out, w_out, norm2_g, w_group, b_group, w_router,
           b_router, w_gate, w_up, w_down):
    l = 0
    row = lambda v: v.reshape(1, -1).astype(F32)

    c_u, c_q, c_f = 2 * CONV_WIDTH, 2 * CONV_WIDTH, 2 * CONV_WIDTH + 3 * ATTN_WIDTH
    c_gc = c_f + N_HEADS
    c_ga = c_gc + D_MODEL
    wi, bi = w_in[l], b_in[l]
    wu, bu = wi[:, :c_u].astype(BF16), row(bi[:c_u])
    wqkv, bqkv = wi[:, c_q:c_f].astype(BF16), row(bi[c_q:c_f])
    wft = jnp.zeros((16, D_MODEL), F32).at[:N_HEADS].set(wi[:, c_f:c_gc].T).astype(BF16)
    bft = jnp.zeros((16, 1), F32).at[:N_HEADS, 0].set(bi[c_f:c_gc] + b_forget[l])
    wgc, bgc = wi[:, c_gc:c_ga].astype(BF16), row(bi[c_gc:c_ga])
    wga, bga = wi[:, c_ga:].astype(BF16), row(bi[c_ga:])
    qg = row(jnp.tile(q_norm_g[l], N_HEADS) * (HEAD_DIM ** -0.5 * LOG2E))
    kg = row(jnp.tile(k_norm_g[l], N_HEADS))
    head_of_col = jnp.arange(ATTN_WIDTH) // HEAD_DIM
    hsum = (head_of_col[:, None] == jnp.arange(LANES)[None, :]).astype(BF16)
    hexp_half = (jnp.arange(LANES)[:, None] == head_of_col[None, :]).astype(BF16)
    hexp = jnp.concatenate([hexp_half, hexp_half], axis=0)
    proj_consts = (row(norm1_g[l]), wu, bu, wqkv, bqkv, wft, bft, wgc, bgc, wga, bga, qg, kg, hsum, hexp)

    x2d = x.reshape(N_TOK, D_MODEL)
    meta_pad = jnp.zeros((META_ROWS, D_MODEL), F32).at[:N_META].set(meta.astype(F32))
    a, q, k, v, lf, sgc, sga = _proj(x2d, proj_consts, PROJ_TM)
    a_m, _, k_m, v_m, lf_m, _, _ = _proj(meta_pad, proj_consts, META_ROWS)

    act = _conv(a.reshape(BATCH, SEQ, CONV_WIDTH), a_m, dw_w[l].astype(F32), row(dw_b[l]),
                row(conv_ln_g[l]), row(conv_ln_b[l])).reshape(N_TOK, CONV_WIDTH)

    o = _attention_t(q, k, v, lf, k_m, v_m, lf_m)

    wrt = jnp.zeros((LANES, D_MODEL), F32).at[:N_GROUPS].set(w_group[l].T)
    wrt = wrt.at[N_GROUPS:N_GROUPS + N_EXPERTS].set(w_router[l].T).astype(BF16)
    brt = jnp.zeros((LANES, 1), F32).at[:N_GROUPS, 0].set(b_group[l])
    brt = brt.at[N_GROUPS:N_GROUPS + N_EXPERTS, 0].set(b_router[l])
    tri = jnp.triu(jnp.ones((MERGE_TM, MERGE_TM), F32), 1).astype(BF16)
    merge_consts = (w_conv_out[l].astype(BF16), row(b_conv_out[l]), w_attn_out[l].astype(BF16),
                    w_out[l].astype(BF16), row(norm2_g[l]), wrt, brt, tri)
    h1, hn, route, route_t, cnt = _merge(act, o, sgc, sga, x2d, merge_consts)

    counts = cnt[:N_EXPERTS, 0].astype(jnp.int32)
    padded = (counts + ROW_BLOCK - 1) // ROW_BLOCK * ROW_BLOCK
    pend = jnp.cumsum(padded).astype(jnp.int32)
    pstart = pend - padded
    blk_row = jnp.arange(N_BLOCKS, dtype=jnp.int32) * ROW_BLOCK
    n_used = pend[-1:] // ROW_BLOCK
    nz = counts > 0
    before = (pend[None, :] <= blk_row[:, None]) & nz[None, :]
    blk_ord = jnp.sum(before.astype(jnp.int32), axis=1)
    blk_first = jnp.any((pstart[None, :] == blk_row[:, None]) & nz[None, :], axis=1).astype(jnp.int32)
    e_ord = jnp.cumsum(nz.astype(jnp.int32)) - 1
    e_ids = jnp.arange(N_EXPERTS, dtype=jnp.int32)
    used_e = jnp.sum(jnp.where((e_ord[None, :] == e_ids[:, None]) & nz[None, :], e_ids[None, :], 0), axis=1)
    n_used_e = jnp.sum(nz.astype(jnp.int32)).reshape(1)

    dest = _dest(pstart, route_t)
    d0, d1 = dest[0], dest[1]
    xin = _dispatch(d0, d1, pstart, pend, hn)
    yb = _experts(n_used, blk_first, blk_ord, used_e.astype(jnp.int32), n_used_e, xin,
                  w_gate[l], w_up[l], w_down[l])
    out = _combine(d0, d1, h1, route, yb)
    return out.reshape(BATCH, SEQ, D_MODEL)
```

```python
import jax
import jax.numpy as jnp
from jax import lax
from jax.experimental import pallas as pl
from jax.experimental.pallas import tpu as pltpu

D_MODEL = 1024
BATCH = 8
SEQ = 2048
N_META = 16
CONV_WIDTH = 512
CONV_K = 31
N_HEADS = 8
HEAD_DIM = 64
ATTN_WIDTH = N_HEADS * HEAD_DIM
N_GROUPS = 8
EXPERTS_PER_GROUP = 8
N_EXPERTS = N_GROUPS * EXPERTS_PER_GROUP
D_EXPERT = 256
RMS_EPS = 1e-6
LN_EPS = 1e-5

N_TOK = BATCH * SEQ
LANES = 128
META_ROWS = 128
ROW_BLOCK = 512
N_ASSIGN = 2 * N_TOK
N_BLOCKS = N_ASSIGN // ROW_BLOCK + N_EXPERTS
P_ROWS = N_BLOCKS * ROW_BLOCK
TOK_SUB = D_MODEL // LANES
PROJ_TM = 1024
ATT_TQ = 1024
ATT_TK = 256
N_PAIRS = N_HEADS // 2
assert (ATT_TQ // ATT_TK) % 2 == 0
VMEM_LIMIT = 56 * 1024 * 1024

F32 = jnp.float32
BF16 = jnp.bfloat16
NEG_INF = float("-inf")
LOG2E = 1.4426950408889634


def _dot(a, b):
    return jnp.dot(a, b, preferred_element_type=F32)


def _dot_nt(a, b):
    return lax.dot_general(a, b, (((1,), (1,)), ((), ())), preferred_element_type=F32)


def _sigmoid(x):
    return 1.0 / (1.0 + jnp.exp(-x))


def _const_spec(shape):
    nd = len(shape)
    return pl.BlockSpec(shape, lambda *_: (0,) * nd)


def _head_rms(t, hsum_ref, hexp_ref, gain):
    ss = _dot((t * t).astype(BF16), hsum_ref[...])
    inv = lax.rsqrt(ss * (1.0 / HEAD_DIM) + RMS_EPS)
    hi = inv.astype(BF16)
    lo = (inv - hi.astype(F32)).astype(BF16)
    invb = _dot(jnp.concatenate([hi, lo], axis=1), hexp_ref[...])
    return t * invb * gain


def _proj_stages(x_ref, g1_ref, wu_ref, bu_ref, wqkv_ref, bqkv_ref, wft_ref, bft_ref, wgc_ref, bgc_ref,
                 wga_ref, bga_ref, qg_ref, kg_ref, hsum_ref, hexp_ref,
                 q_ref, k_ref, v_ref, lf_ref, sgc_ref, sga_ref):
    x = x_ref[...]
    ms = jnp.mean(x * x, axis=-1, keepdims=True)
    xn = (x * lax.rsqrt(ms + RMS_EPS) * g1_ref[...]).astype(BF16)

    def glu():
        u = _dot(xn, wu_ref[...]) + bu_ref[...]
        return (u[:, :CONV_WIDTH] * _sigmoid(u[:, CONV_WIDTH:])).astype(BF16)

    def qkv_stage():
        qkv = _dot(xn, wqkv_ref[...]) + bqkv_ref[...]
        q_ref[...] = _head_rms(qkv[:, :ATTN_WIDTH], hsum_ref, hexp_ref, qg_ref[...]).astype(BF16)
        k_ref[...] = _head_rms(qkv[:, ATTN_WIDTH:2 * ATTN_WIDTH], hsum_ref, hexp_ref, kg_ref[...]).astype(BF16)
        v = qkv[:, 2 * ATTN_WIDTH:]
        vb = v_ref.shape[2]
        for c in range(v_ref.shape[0]):
            v_ref[c] = v[c * vb:(c + 1) * vb, :].T.astype(BF16)

    def gate_stage_c():
        f = _dot_nt(wft_ref[...], xn) + bft_ref[...]
        lf_ref[...] = jnp.minimum(f, 0.0) - jnp.log(1.0 + jnp.exp(-jnp.abs(f)))
        sgc_ref[...] = _sigmoid(_dot(xn, wgc_ref[...]) + bgc_ref[...]).astype(BF16)

    def gate_stage_a():
        sga_ref[...] = _sigmoid(_dot(xn, wga_ref[...]) + bga_ref[...]).astype(BF16)

    return glu, (qkv_stage, gate_stage_c, gate_stage_a)


def _proj_kernel(*refs):
    ins, a_ref, outs = refs[:16], refs[16], refs[17:]
    glu, stages = _proj_stages(*ins, *outs)
    a_ref[...] = glu()
    for stage in stages:
        stage()


def _proj_conv_kernel(*refs):
    ins = refs[:16]
    am_ref, shift_ref, dw_ref, dwb_ref, lng_ref, lnb_ref = refs[16:22]
    act_ref, outs = refs[22], refs[23:29]
    win_scr, sh_scr = refs[29:]
    tm = act_ref.shape[0]
    i = pl.program_id(0)
    first = (i % (SEQ // tm)) == 0

    @pl.when(first)
    def _():
        win_scr[0:N_META, :] = jnp.zeros((N_META, CONV_WIDTH), BF16)
        win_scr[N_META:CONV_PAD, :] = am_ref[0:N_META, :]

    @pl.when(jnp.logical_not(first))
    def _():
        win_scr[0:CONV_PAD, :] = win_scr[tm:tm + CONV_PAD, :]

    glu, stages = _proj_stages(*ins, *outs)
    win_scr[CONV_PAD:CONV_PAD + tm, :] = glu()

    def conv_chunk(ci):
        r0 = ci * CONV_CHUNK
        slot = ci % 2
        sh_scr[slot] = _dot(shift_ref[...], win_scr[r0:r0 + CONV_WIN, :])
        parts = []
        for c0 in range(0, CONV_WIDTH, CONV_LANES):
            cs = slice(c0, c0 + CONV_LANES)
            part = jnp.zeros((CONV_CHUNK, CONV_LANES), F32) + dwb_ref[:, cs]
            for j in range(CONV_K):
                off = j + 2
                b = off % 8
                a0 = b * CONV_SHROWS + (off - b)
                part = part + sh_scr[slot, a0:a0 + CONV_CHUNK, cs] * dw_ref[j:j + 1, cs]
            parts.append(part)
        acc = jnp.concatenate(parts, axis=1)
        mu = jnp.mean(acc, axis=-1, keepdims=True)
        d = acc - mu
        var = jnp.mean(d * d, axis=-1, keepdims=True)
        y = d * lax.rsqrt(var + LN_EPS) * lng_ref[...] + lnb_ref[...]
        act_ref[r0:r0 + CONV_CHUNK, :] = (y * _sigmoid(y)).astype(BF16)

    n_chunks = tm // CONV_CHUNK
    per_stage = -(-n_chunks // (len(stages) + 1))
    ci = 0
    for stage in (None,) + stages:
        if stage is not None:
            stage()
        for _ in range(per_stage):
            if ci < n_chunks:
                conv_chunk(ci)
                ci += 1


def _proj(x2d, consts, tm, conv_consts=None):
    n = x2d.shape[0]
    row = lambda w: pl.BlockSpec((tm, w), lambda i: (i, 0))
    in_specs = [row(D_MODEL)] + [_const_spec(c.shape) for c in consts]
    fused = conv_consts is not None
    extra = tuple(conv_consts) if fused else ()
    in_specs += [_const_spec(c.shape) for c in extra]
    scratch = [pltpu.VMEM((CONV_PAD + tm, CONV_WIDTH), BF16),
               pltpu.VMEM((2, 8 * CONV_SHROWS, CONV_WIDTH), F32)] if fused else []
    vb = min(tm, ATT_TK)
    out_shape = [
        jax.ShapeDtypeStruct((n, CONV_WIDTH), BF16),
        jax.ShapeDtypeStruct((n, ATTN_WIDTH), BF16),
        jax.ShapeDtypeStruct((n, ATTN_WIDTH), BF16),
        jax.ShapeDtypeStruct((n // vb, ATTN_WIDTH, vb), BF16),
        jax.ShapeDtypeStruct((16, n), F32),
        jax.ShapeDtypeStruct((n, D_MODEL), BF16),
        jax.ShapeDtypeStruct((n, D_MODEL), BF16),
    ]
    out_specs = [row(CONV_WIDTH), row(ATTN_WIDTH), row(ATTN_WIDTH),
                 pl.BlockSpec((tm // vb, ATTN_WIDTH, vb), lambda i: (i, 0, 0)),
                 pl.BlockSpec((16, tm), lambda i: (0, i)), row(D_MODEL), row(D_MODEL)]
    return pl.pallas_call(
        _proj_conv_kernel if fused else _proj_kernel,
        grid=(n // tm,),
        in_specs=in_specs,
        out_specs=out_specs,
        out_shape=out_shape,
        scratch_shapes=scratch,
        compiler_params=pltpu.CompilerParams(dimension_semantics=("arbitrary",), vmem_limit_bytes=VMEM_LIMIT),
        name="proj_conv" if fused else "proj",
    )(x2d, *consts, *extra)


CONV_CHUNK = 64
CONV_PAD = 32
CONV_WIN = CONV_CHUNK + 32
CONV_SHROWS = CONV_WIN
CONV_LANES = 128


def _conv_kernel(a_ref, am_ref, shift_ref, dw_ref, dwb_ref, lng_ref, lnb_ref, o_ref, pad_scr, sha_scr, shb_scr):
    pad_scr[0:N_META, :] = jnp.zeros((N_META, CONV_WIDTH), BF16)
    pad_scr[N_META:CONV_PAD, :] = am_ref[0:N_META, :]
    pad_scr[CONV_PAD:CONV_PAD + SEQ, :] = a_ref[0]

    def shifted_copies(ci):
        r0 = pl.multiple_of(ci * CONV_CHUNK, CONV_CHUNK)
        return _dot(shift_ref[...], pad_scr[pl.ds(r0, CONV_WIN), :])

    n_chunks = SEQ // CONV_CHUNK
    sha_scr[...] = shifted_copies(0)

    def finish(ci, cur_scr):
        r0 = pl.multiple_of(ci * CONV_CHUNK, CONV_CHUNK)
        parts = []
        for c0 in range(0, CONV_WIDTH, CONV_LANES):
            cs = slice(c0, c0 + CONV_LANES)
            part = jnp.zeros((CONV_CHUNK, CONV_LANES), F32) + dwb_ref[:, cs]
            for j in range(CONV_K):
                off = j + 2
                b = off % 8
                a0 = b * CONV_SHROWS + (off - b)
                part = part + cur_scr[a0:a0 + CONV_CHUNK, cs] * dw_ref[j:j + 1, cs]
            parts.append(part)
        acc = jnp.concatenate(parts, axis=1)
        mu = jnp.mean(acc, axis=-1, keepdims=True)
        d = acc - mu
        var = jnp.mean(d * d, axis=-1, keepdims=True)
        y = d * lax.rsqrt(var + LN_EPS) * lng_ref[...] + lnb_ref[...]
        o_ref[0, pl.ds(r0, CONV_CHUNK), :] = (y * _sigmoid(y)).astype(BF16)

    def chunk_pair(i, carry):
        ci = 2 * i
        shb_scr[...] = shifted_copies(ci + 1)
        finish(ci, sha_scr)
        sha_scr[...] = shifted_copies(jnp.minimum(ci + 2, n_chunks - 1))
        finish(ci + 1, shb_scr)
        return carry

    lax.fori_loop(0, n_chunks // 2, chunk_pair, 0)


def _conv_shift_matrix():
    rr = jnp.arange(8 * CONV_SHROWS)
    return ((rr % CONV_SHROWS + rr // CONV_SHROWS)[:, None] == jnp.arange(CONV_WIN)[None, :]).astype(BF16)


def _conv(a3, a_meta, dw_w, dw_b, ln_g, ln_b):
    shift = _conv_shift_matrix()
    return pl.pallas_call(
        _conv_kernel,
        grid=(BATCH,),
        in_specs=[pl.BlockSpec((1, SEQ, CONV_WIDTH), lambda b: (b, 0, 0)),
                  _const_spec(a_meta.shape), _const_spec(shift.shape), _const_spec(dw_w.shape),
                  _const_spec(dw_b.shape), _const_spec(ln_g.shape), _const_spec(ln_b.shape)],
        out_specs=pl.BlockSpec((1, SEQ, CONV_WIDTH), lambda b: (b, 0, 0)),
        out_shape=jax.ShapeDtypeStruct((BATCH, SEQ, CONV_WIDTH), BF16),
        scratch_shapes=[pltpu.VMEM((CONV_PAD + SEQ, CONV_WIDTH), BF16),
                        pltpu.VMEM((8 * CONV_SHROWS, CONV_WIDTH), F32),
                        pltpu.VMEM((8 * CONV_SHROWS, CONV_WIDTH), F32)],
        compiler_params=pltpu.CompilerParams(dimension_semantics=("arbitrary",), vmem_limit_bytes=VMEM_LIMIT),
        name="conv",
    )(a3, a_meta, shift, dw_w, dw_b, ln_g, ln_b)


AUG_TERMS = 3


def _split_terms(x):
    hi = x.astype(BF16)
    r1 = x - hi.astype(F32)
    mid = r1.astype(BF16)
    lo = (r1 - mid.astype(F32)).astype(BF16)
    return hi, mid, lo


def _attn_t_kernel(q_ref, k_ref, vt_ref, lf_ref, km_ref, vm_ref, lfm_ref, place_ref, o_ref,
                   aug_scr, augm_scr, m_scr, acc_scr, sa_scr, sb_scr):
    p = pl.program_id(1)
    qi = pl.program_id(2)
    nk = SEQ // ATT_TK

    def bias_lanes(colsum):
        out = None
        for t, term in enumerate(_split_terms(colsum)):
            d = _dot(term, place_ref[t])
            out = d if out is None else out + d
        return out.astype(BF16)

    @pl.when((p == 0) & (qi == 0))
    def _():
        r = lax.broadcasted_iota(jnp.int32, (ATT_TK, ATT_TK), 0)
        c = lax.broadcasted_iota(jnp.int32, (ATT_TK, ATT_TK), 1)
        tri = (c <= r).astype(BF16)
        local = []
        for j in range(nk):
            lf = lf_ref[:, j * ATT_TK:(j + 1) * ATT_TK]
            csum = None
            for term in _split_terms(lf):
                d = _dot_nt(tri, term)
                csum = d if csum is None else csum + d
            local.append(csum)
        carry = jnp.zeros((1, 16), F32)
        for j in range(nk):
            csum = local[j] + carry
            aug_scr[j * ATT_TK:(j + 1) * ATT_TK, :] = bias_lanes(csum * (-LOG2E))
            carry = csum[ATT_TK - 1:ATT_TK, :]
        rm = lax.broadcasted_iota(jnp.int32, (META_ROWS, META_ROWS), 0)
        cm = lax.broadcasted_iota(jnp.int32, (META_ROWS, META_ROWS), 1)
        trim = ((cm > rm) & (cm < N_META)).astype(BF16)
        msum = jnp.zeros((META_ROWS, 16), F32)
        for term in _split_terms(lfm_ref[...]):
            msum = msum + _dot_nt(trim, term)
        augm_scr[...] = bias_lanes(msum * LOG2E)

    lane = lax.broadcasted_iota(jnp.int32, (ATT_TQ, LANES), 1)
    q_all = q_ref[...]
    zq = jnp.zeros_like(q_all)
    q_cat = []
    for hl in range(2):
        own = (lane < HEAD_DIM) if hl == 0 else (lane >= HEAD_DIM)
        h = 2 * p + hl
        bias_sel = (lane >= AUG_TERMS * h) & (lane < AUG_TERMS * h + AUG_TERMS)
        bias_ones = jnp.where(bias_sel, 1.0, 0.0).astype(BF16)
        q_cat.append(jnp.concatenate([jnp.where(own, q_all, zq), bias_ones], axis=1))
        m_scr[hl] = jnp.full((1, ATT_TQ), NEG_INF, F32)
        acc_scr[hl] = jnp.zeros((LANES, ATT_TQ), F32)

    def scores(kb, augb, lo=0):
        k_cat = jnp.concatenate([kb, augb], axis=1)
        return tuple(_dot_nt(k_cat, q_cat[hl][lo:]) for hl in range(2))

    def update(s2, vtb, mask, lo=0):
        vrow = lax.broadcasted_iota(jnp.int32, vtb.shape, 0)
        ones = jnp.ones_like(vtb)
        for hl in range(2):
            s = s2[hl]
            if mask is not None:
                s = jnp.where(mask, s, NEG_INF)
            m_prev = m_scr[hl, :, lo:]
            m_next = jnp.maximum(m_prev, jnp.max(s, axis=0, keepdims=True))
            pr = jnp.exp2(s - m_next).astype(BF16)
            alpha = jnp.exp2(m_prev - m_next)
            own = (vrow < HEAD_DIM) if hl == 0 else (vrow >= HEAD_DIM)
            acc_scr[hl, :, lo:] = acc_scr[hl, :, lo:] * alpha + _dot(jnp.where(own, vtb, ones), pr)
            m_scr[hl, :, lo:] = m_next

    def kblock(j):
        return k_ref[pl.ds(pl.multiple_of(j * ATT_TK, ATT_TK), ATT_TK), :]

    def augblock(j):
        return aug_scr[pl.ds(pl.multiple_of(j * ATT_TK, ATT_TK), ATT_TK), :]

    nfull = qi * (ATT_TQ // ATT_TK)

    def put(scr, s2):
        scr[0] = s2[0]
        scr[1] = s2[1]

    s_meta = scores(km_ref[0:N_META, :], augm_scr[0:N_META, :])
    put(sa_scr, scores(kblock(0), augblock(0)))
    update(s_meta, vm_ref[0][:, 0:N_META], None)

    def full2(i, carry):
        j = 2 * i
        put(sb_scr, scores(kblock(j + 1), augblock(j + 1)))
        update((sa_scr[0], sa_scr[1]), vt_ref[j], None)
        put(sa_scr, scores(kblock(j + 2), augblock(j + 2)))
        update((sb_scr[0], sb_scr[1]), vt_ref[j + 1], None)
        return carry

    lax.fori_loop(0, nfull // 2, full2, 0)

    srow = lax.broadcasted_iota(jnp.int32, (ATT_TK, ATT_TQ), 0)
    tcol = lax.broadcasted_iota(jnp.int32, (ATT_TK, ATT_TQ), 1)
    s_diag = [(sa_scr[0], sa_scr[1])]
    for dj in range(1, ATT_TQ // ATT_TK):
        s_diag.append(scores(kblock(nfull + dj), augblock(nfull + dj), lo=dj * ATT_TK))
    for dj in range(ATT_TQ // ATT_TK):
        lo = dj * ATT_TK
        update(s_diag[dj], vt_ref[nfull + dj], (srow <= tcol)[:, :ATT_TQ - lo], lo=lo)

    a0 = acc_scr[0]
    a1 = acc_scr[1]
    ot = jnp.concatenate([a0[0:HEAD_DIM] / a0[HEAD_DIM:], a1[HEAD_DIM:] / a1[0:HEAD_DIM]], axis=0)
    o_ref[...] = ot.T.astype(BF16)


def _attention_t(q, k, vt, lf, km, vm, lfm):
    nq = SEQ // ATT_TQ
    nk = SEQ // ATT_TK
    hh = jnp.arange(16)[:, None]
    ll = jnp.arange(LANES)[None, :]
    place = jnp.stack([((ll == AUG_TERMS * hh + t) & (hh < N_HEADS)).astype(BF16) for t in range(AUG_TERMS)])
    return pl.pallas_call(
        _attn_t_kernel,
        grid=(BATCH, N_PAIRS, nq),
        in_specs=[
            pl.BlockSpec((ATT_TQ, LANES), lambda b, p, i: (b * nq + i, p)),
            pl.BlockSpec((SEQ, LANES), lambda b, p, i: (b, p)),
            pl.BlockSpec((nk, LANES, ATT_TK), lambda b, p, i: (b, p, 0)),
            pl.BlockSpec((16, SEQ), lambda b, p, i: (0, b)),
            pl.BlockSpec((META_ROWS, LANES), lambda b, p, i: (0, p)),
            pl.BlockSpec((1, LANES, META_ROWS), lambda b, p, i: (0, p, 0)),
            pl.BlockSpec((16, META_ROWS), lambda b, p, i: (0, 0)),
            _const_spec(place.shape),
        ],
        out_specs=pl.BlockSpec((ATT_TQ, LANES), lambda b, p, i: (b * nq + i, p)),
        out_shape=jax.ShapeDtypeStruct((N_TOK, ATTN_WIDTH), BF16),
        scratch_shapes=[
            pltpu.VMEM((SEQ, LANES), BF16),
            pltpu.VMEM((META_ROWS, LANES), BF16),
            pltpu.VMEM((2, 1, ATT_TQ), F32),
            pltpu.VMEM((2, LANES, ATT_TQ), F32),
            pltpu.VMEM((2, ATT_TK, ATT_TQ), F32),
            pltpu.VMEM((2, ATT_TK, ATT_TQ), F32),
        ],
        compiler_params=pltpu.CompilerParams(
            dimension_semantics=("arbitrary", "arbitrary", "arbitrary"), vmem_limit_bytes=VMEM_LIMIT),
        name="attn",
    )(q, k, vt, lf, km, vm, lfm, place)


MERGE_TM = 1024


def _merge_kernel(act_ref, o_ref, sgc_ref, sga_ref, x_ref, wpw_ref, bpw_ref, wao_ref, wout_ref, g2_ref,
                  wrt_ref, brt_ref, tri_ref,
                  h1_ref, hn_ref, route_ref, routet_ref, cnt_ref, base_scr):
    i = pl.program_id(0)

    @pl.when(i == 0)
    def _():
        base_scr[...] = jnp.zeros_like(base_scr)

    y_conv = _dot(act_ref[...], wpw_ref[...]) + bpw_ref[...]
    y_attn = _dot(o_ref[...], wao_ref[...])
    merged = sgc_ref[...].astype(F32) * y_conv + sga_ref[...].astype(F32) * y_attn
    h1 = x_ref[...] + _dot(merged.astype(BF16), wout_ref[...])
    h1_ref[...] = h1
    ms = jnp.mean(h1 * h1, axis=-1, keepdims=True)
    hn = (h1 * lax.rsqrt(ms + RMS_EPS) * g2_ref[...]).astype(BF16)
    hn_ref[...] = hn.reshape(hn.shape[0], TOK_SUB, LANES)

    logits = _dot_nt(wrt_ref[...], hn) + brt_ref[...]
    row = lax.broadcasted_iota(jnp.int32, logits.shape, 0).astype(F32)
    big = float(LANES)
    is_g = row < N_GROUPS
    gl = jnp.where(is_g, logits, NEG_INF)
    gmax = jnp.max(gl, axis=0, keepdims=True)
    gidx = jnp.min(jnp.where(gl == gmax, row, big), axis=0, keepdims=True)
    gsum = jnp.sum(jnp.where(is_g, jnp.exp(gl - gmax), 0.0), axis=0, keepdims=True)
    g_w = 1.0 / gsum
    lo = N_GROUPS + EXPERTS_PER_GROUP * gidx
    el = jnp.where((row >= lo) & (row < lo + EXPERTS_PER_GROUP), logits, NEG_INF)
    t1 = jnp.max(el, axis=0, keepdims=True)
    i1 = jnp.min(jnp.where(el == t1, row, big), axis=0, keepdims=True)
    el2 = jnp.where(row == i1, NEG_INF, el)
    t2 = jnp.max(el2, axis=0, keepdims=True)
    i2 = jnp.min(jnp.where(el2 == t2, row, big), axis=0, keepdims=True)
    e2 = jnp.exp(t2 - t1)
    den = 1.0 + e2
    w1 = (1.0 / den) * g_w
    w2 = (e2 / den) * g_w
    ea = i1 - N_GROUPS
    eb = i2 - N_GROUPS

    oha = (row == ea).astype(F32)
    ohb = (row == eb).astype(F32)
    ohs = oha + ohb
    before = _dot(ohs.astype(BF16), tri_ref[...]) + base_scr[...][:, 0:1]
    ra = jnp.sum(before * oha, axis=0, keepdims=True)
    rb = jnp.sum(before * ohb, axis=0, keepdims=True)
    base_scr[...] = base_scr[...] + jnp.sum(ohs, axis=1, keepdims=True)
    cnt_ref[...] = base_scr[...]

    def rows_of(shape):
        r = lax.broadcasted_iota(jnp.int32, shape, 0)
        out = jnp.where(r == 0, ea, 0.0)
        for k, val in enumerate((eb, w1, w2, ra, rb), start=1):
            out = jnp.where(r == k, val, out)
        return out

    routet_ref[...] = rows_of((8, logits.shape[1]))
    route_ref[...] = rows_of(logits.shape).T


def _merge(act, o, sgc, sga, x2d, consts):
    tm = MERGE_TM
    row = lambda w: pl.BlockSpec((tm, w), lambda i: (i, 0))
    return pl.pallas_call(
        _merge_kernel,
        grid=(N_TOK // tm,),
        in_specs=[row(CONV_WIDTH), row(ATTN_WIDTH), row(D_MODEL), row(D_MODEL), row(D_MODEL)]
        + [_const_spec(c.shape) for c in consts],
        out_specs=[row(D_MODEL), pl.BlockSpec((tm, TOK_SUB, LANES), lambda i: (i, 0, 0)), row(LANES),
                   pl.BlockSpec((8, tm), lambda i: (0, i)), _const_spec((LANES, LANES))],
        out_shape=[jax.ShapeDtypeStruct((N_TOK, D_MODEL), F32),
                   jax.ShapeDtypeStruct((N_TOK, TOK_SUB, LANES), BF16),
                   jax.ShapeDtypeStruct((N_TOK, LANES), F32),
                   jax.ShapeDtypeStruct((8, N_TOK), F32),
                   jax.ShapeDtypeStruct((LANES, LANES), F32)],
        scratch_shapes=[pltpu.VMEM((LANES, LANES), F32)],
        compiler_params=pltpu.CompilerParams(dimension_semantics=("arbitrary",), vmem_limit_bytes=VMEM_LIMIT),
        name="merge",
    )(act, o, sgc, sga, x2d, *consts)


DISP_TM = 1024
DMA_UNROLL = 8


def _dest_kernel(pstart_ref, rt_ref, d_ref):
    rt = rt_ref[...]
    start = jnp.zeros_like(rt)
    for e in range(N_EXPERTS):
        start = jnp.where(rt == float(e), pstart_ref[e].astype(F32), start)
    d_ref[...] = (start + pltpu.roll(rt, 4, axis=0)).astype(jnp.int32)


def _dest(pstart, route_t):
    grid_spec = pltpu.PrefetchScalarGridSpec(
        num_scalar_prefetch=1,
        grid=(1,),
        in_specs=[pl.BlockSpec(route_t.shape, lambda i, *_: (0, 0))],
        out_specs=pl.BlockSpec(route_t.shape, lambda i, *_: (0, 0)),
    )
    return pl.pallas_call(
        _dest_kernel,
        grid_spec=grid_spec,
        out_shape=jax.ShapeDtypeStruct(route_t.shape, jnp.int32),
        compiler_params=pltpu.CompilerParams(dimension_semantics=("arbitrary",), vmem_limit_bytes=VMEM_LIMIT),
        name="dest",
    )(pstart, route_t)


def _dispatch_kernel(d0_ref, d1_ref, pstart_ref, pend_ref, hn_ref, xin_ref, zero_scr, sem, zsem):
    i = pl.program_id(0)
    base = i * DISP_TM

    @pl.when(i == 0)
    def _():
        zero_scr[...] = jnp.zeros_like(zero_scr)

        def zcopy(e):
            start = pl.multiple_of(pend_ref[e] - ROW_BLOCK, ROW_BLOCK)
            return pltpu.make_async_copy(zero_scr, xin_ref.at[pl.ds(start, ROW_BLOCK)], zsem.at[0])

        def zstart(e, carry):
            @pl.when(pend_ref[e] > pstart_ref[e])
            def _():
                zcopy(e).start()
            return carry

        def zwait(e, carry):
            @pl.when(pend_ref[e] > pstart_ref[e])
            def _():
                zcopy(e).wait()
            return carry

        n_used = pend_ref[N_EXPERTS - 1] // ROW_BLOCK

        def tcopy(b):
            return pltpu.make_async_copy(
                zero_scr, xin_ref.at[pl.ds(pl.multiple_of(b * ROW_BLOCK, ROW_BLOCK), ROW_BLOCK)], zsem.at[0])

        def tstart(b, carry):
            tcopy(b).start()
            return carry

        def twait(b, carry):
            tcopy(b).wait()
            return carry

        lax.fori_loop(0, N_EXPERTS, zstart, 0)
        lax.fori_loop(n_used, N_BLOCKS, tstart, 0)
        lax.fori_loop(0, N_EXPERTS, zwait, 0)
        lax.fori_loop(n_used, N_BLOCKS, twait, 0)

    def start(r, carry):
        pltpu.make_async_copy(hn_ref.at[r], xin_ref.at[d0_ref[base + r]], sem.at[0]).start()
        pltpu.make_async_copy(hn_ref.at[r], xin_ref.at[d1_ref[base + r]], sem.at[1]).start(priority=1)
        return carry

    lax.fori_loop(0, DISP_TM, start, 0, unroll=DMA_UNROLL)
    pltpu.make_async_copy(hn_ref, xin_ref.at[pl.ds(0, DISP_TM)], sem.at[0]).wait()
    pltpu.make_async_copy(hn_ref, xin_ref.at[pl.ds(0, DISP_TM)], sem.at[1]).wait()


def _dispatch(d0, d1, pstart, pend, hn):
    grid_spec = pltpu.PrefetchScalarGridSpec(
        num_scalar_prefetch=4,
        grid=(N_TOK // DISP_TM,),
        in_specs=[pl.BlockSpec((DISP_TM, TOK_SUB, LANES), lambda i, *_: (i, 0, 0))],
        out_specs=pl.BlockSpec(memory_space=pl.ANY),
        scratch_shapes=[pltpu.VMEM((ROW_BLOCK, TOK_SUB, LANES), BF16),
                        pltpu.SemaphoreType.DMA((2,)), pltpu.SemaphoreType.DMA((1,))],
    )
    return pl.pallas_call(
        _dispatch_kernel,
        grid_spec=grid_spec,
        out_shape=jax.ShapeDtypeStruct((P_ROWS, TOK_SUB, LANES), BF16),
        compiler_params=pltpu.CompilerParams(dimension_semantics=("arbitrary",), vmem_limit_bytes=VMEM_LIMIT),
        name="dispatch",
    )(d0, d1, pstart, pend, hn)


W_SLOTS = 3


def _expert_kernel(nu_ref, first_ref, ord_ref, used_ref, nue_ref, x_hbm, wg_hbm, wu_hbm, wd_hbm, y_hbm,
                   x_buf, y_buf, wg_f, wu_f, wd_f, wg_b, wu_b, wd_b, xsem, ysem, wsem):
    n = nu_ref[0]

    def rows(i):
        return pl.ds(pl.multiple_of(i * ROW_BLOCK, ROW_BLOCK), ROW_BLOCK)

    def xcopy(i, slot):
        return pltpu.make_async_copy(x_hbm.at[rows(i)], x_buf.at[slot], xsem.at[slot])

    def ycopy(i, slot):
        return pltpu.make_async_copy(y_buf.at[slot], y_hbm.at[rows(i)], ysem.at[slot])

    def fetch(k):
        e = used_ref[k]
        slot = k % W_SLOTS
        return (pltpu.make_async_copy(wg_hbm.at[e], wg_f.at[slot], wsem.at[slot, 0]),
                pltpu.make_async_copy(wu_hbm.at[e], wu_f.at[slot], wsem.at[slot, 1]),
                pltpu.make_async_copy(wd_hbm.at[e], wd_f.at[slot], wsem.at[slot, 2]))

    for k in range(W_SLOTS - 1):
        @pl.when(k < nue_ref[0])
        def _():
            for c in fetch(k):
                c.start()
    xcopy(0, 0).start()

    def block(i, carry):
        slot = i % 2
        xcopy(i, slot).wait()

        @pl.when(i + 1 < n)
        def _():
            xcopy(i + 1, 1 - slot).start()

        @pl.when(first_ref[i] == 1)
        def _():
            k = ord_ref[i]
            for c in fetch(k):
                c.wait()

            @pl.when(k + (W_SLOTS - 1) < nue_ref[0])
            def _():
                for c in fetch(k + (W_SLOTS - 1)):
                    c.start()

            wslot = k % W_SLOTS
            wg_b[...] = wg_f[wslot].astype(BF16)
            wu_b[...] = wu_f[wslot].astype(BF16)
            wd_b[...] = wd_f[wslot].astype(BF16)

        @pl.when(i >= 2)
        def _():
            ycopy(i - 2, slot).wait()

        x = x_buf[slot].reshape(ROW_BLOCK, D_MODEL)
        g = _dot(x, wg_b[...])
        u = _dot(x, wu_b[...])
        hmid = (g * _sigmoid(g) * u).astype(BF16)
        y = _dot(hmid, wd_b[...]).astype(BF16)
        y_buf[slot] = y.reshape(ROW_BLOCK, TOK_SUB, LANES)
        ycopy(i, slot).start()
        return carry

    lax.fori_loop(0, n, block, 0)

    @pl.when(n >= 2)
    def _():
        ycopy(n - 2, n % 2).wait()
    ycopy(n - 1, (n - 1) % 2).wait()

    y_buf[0] = jnp.zeros((ROW_BLOCK, TOK_SUB, LANES), BF16)

    def zcopy(b):
        return pltpu.make_async_copy(y_buf.at[0], y_hbm.at[rows(b)], ysem.at[0])

    def zstart(b, carry):
        zcopy(b).start()
        return carry

    def zwait(b, carry):
        zcopy(b).wait()
        return carry

    lax.fori_loop(n, N_BLOCKS, zstart, 0)
    lax.fori_loop(n, N_BLOCKS, zwait, 0)


def _experts(n_used, blk_first, blk_ord, used_e, n_used_e, xin, w_gate, w_up, w_down):
    grid_spec = pltpu.PrefetchScalarGridSpec(
        num_scalar_prefetch=5,
        grid=(1,),
        in_specs=[pl.BlockSpec(memory_space=pl.ANY),
                  pl.BlockSpec(memory_space=pl.ANY),
                  pl.BlockSpec(memory_space=pl.ANY),
                  pl.BlockSpec(memory_space=pl.ANY)],
        out_specs=pl.BlockSpec(memory_space=pl.ANY),
        scratch_shapes=[pltpu.VMEM((2, ROW_BLOCK, TOK_SUB, LANES), BF16),
                        pltpu.VMEM((2, ROW_BLOCK, TOK_SUB, LANES), BF16),
                        pltpu.VMEM((W_SLOTS, D_MODEL, D_EXPERT), F32),
                        pltpu.VMEM((W_SLOTS, D_MODEL, D_EXPERT), F32),
                        pltpu.VMEM((W_SLOTS, D_EXPERT, D_MODEL), F32),
                        pltpu.VMEM((D_MODEL, D_EXPERT), BF16),
                        pltpu.VMEM((D_MODEL, D_EXPERT), BF16),
                        pltpu.VMEM((D_EXPERT, D_MODEL), BF16),
                        pltpu.SemaphoreType.DMA((2,)), pltpu.SemaphoreType.DMA((2,)),
                        pltpu.SemaphoreType.DMA((W_SLOTS, 3))],
    )
    return pl.pallas_call(
        _expert_kernel,
        grid_spec=grid_spec,
        out_shape=jax.ShapeDtypeStruct((P_ROWS, TOK_SUB, LANES), BF16),
        compiler_params=pltpu.CompilerParams(dimension_semantics=("arbitrary",), vmem_limit_bytes=VMEM_LIMIT),
        name="experts",
    )(n_used, blk_first, blk_ord, used_e, n_used_e, xin, w_gate, w_up, w_down)


COMB_TM = 512


def _combine_kernel(d0_ref, d1_ref, h1_ref, route_ref, yb_ref, out_ref, g_scr, sem):
    i = pl.program_id(0)
    n = pl.num_programs(0)

    def issue(tile, slot):
        base = tile * COMB_TM

        def start(r, carry):
            pltpu.make_async_copy(yb_ref.at[d0_ref[base + r]], g_scr.at[slot, 0, r], sem.at[slot, 0]).start()
            pltpu.make_async_copy(yb_ref.at[d1_ref[base + r]], g_scr.at[slot, 1, r], sem.at[slot, 1]).start(
                priority=1)
            return carry

        lax.fori_loop(0, COMB_TM, start, 0, unroll=DMA_UNROLL)

    @pl.when(i == 0)
    def _():
        issue(0, 0)

    slot = i % 2

    @pl.when(i + 1 < n)
    def _():
        issue(i + 1, 1 - slot)

    for k in range(2):
        pltpu.make_async_copy(yb_ref.at[pl.ds(0, COMB_TM)], g_scr.at[slot, k], sem.at[slot, k]).wait()

    route = route_ref[...]
    w0 = route[:, 2:3]
    w1 = route[:, 3:4]
    g0 = g_scr[slot, 0].reshape(COMB_TM, D_MODEL).astype(F32)
    g1 = g_scr[slot, 1].reshape(COMB_TM, D_MODEL).astype(F32)
    out_ref[...] = h1_ref[...] + (g0 * w0 + g1 * w1)


def _combine(d0, d1, h1, route, yb):
    grid_spec = pltpu.PrefetchScalarGridSpec(
        num_scalar_prefetch=2,
        grid=(N_TOK // COMB_TM,),
        in_specs=[pl.BlockSpec((COMB_TM, D_MODEL), lambda i, *_: (i, 0)),
                  pl.BlockSpec((COMB_TM, LANES), lambda i, *_: (i, 0)),
                  pl.BlockSpec(memory_space=pl.ANY)],
        out_specs=pl.BlockSpec((COMB_TM, D_MODEL), lambda i, *_: (i, 0)),
        scratch_shapes=[pltpu.VMEM((2, 2, COMB_TM, TOK_SUB, LANES), BF16),
                        pltpu.SemaphoreType.DMA((2, 2))],
    )
    return pl.pallas_call(
        _combine_kernel,
        grid_spec=grid_spec,
        out_shape=jax.ShapeDtypeStruct((N_TOK, D_MODEL), F32),
        compiler_params=pltpu.CompilerParams(dimension_semantics=("arbitrary",), vmem_limit_bytes=VMEM_LIMIT),
        name="combine",
    )(d0, d1, h1, route, yb)


def kernel(x, meta, norm1_g, w_in, b_in, dw_w, dw_b, conv_ln_g, conv_ln_b, w_conv_out, b_conv_out,
           q_norm_g, k_norm_g, b_forget, w_attn_out, w_out, norm2_g, w_group, b_group, w_router,
           b_router, w_gate, w_up, w_down):
    l = 0
    row = lambda v: v.reshape(1, -1).astype(F32)

    c_u, c_q, c_f = 2 * CONV_WIDTH, 2 * CONV_WIDTH, 2 * CONV_WIDTH + 3 * ATTN_WIDTH
    c_gc = c_f + N_HEADS
    c_ga = c_gc + D_MODEL
    wi, bi = w_in[l], b_in[l]
    wu, bu = wi[:, :c_u].astype(BF16), row(bi[:c_u])
    wqkv, bqkv = wi[:, c_q:c_f].astype(BF16), row(bi[c_q:c_f])
    wft = jnp.zeros((16, D_MODEL), F32).at[:N_HEADS].set(wi[:, c_f:c_gc].T).astype(BF16)
    bft = jnp.zeros((16, 1), F32).at[:N_HEADS, 0].set(bi[c_f:c_gc] + b_forget[l])
    wgc, bgc = wi[:, c_gc:c_ga].astype(BF16), row(bi[c_gc:c_ga])
    wga, bga = wi[:, c_ga:].astype(BF16), row(bi[c_ga:])
    qg = row(jnp.tile(q_norm_g[l], N_HEADS) * (HEAD_DIM ** -0.5 * LOG2E))
    kg = row(jnp.tile(k_norm_g[l], N_HEADS))
    head_of_col = jnp.arange(ATTN_WIDTH) // HEAD_DIM
    hsum = (head_of_col[:, None] == jnp.arange(LANES)[None, :]).astype(BF16)
    hexp_half = (jnp.arange(LANES)[:, None] == head_of_col[None, :]).astype(BF16)
    hexp = jnp.concatenate([hexp_half, hexp_half], axis=0)
    proj_consts = (row(norm1_g[l]), wu, bu, wqkv, bqkv, wft, bft, wgc, bgc, wga, bga, qg, kg, hsum, hexp)

    x2d = x.reshape(N_TOK, D_MODEL)
    meta_pad = jnp.zeros((META_ROWS, D_MODEL), F32).at[:N_META].set(meta.astype(F32))
    a_m, _, k_m, v_m, lf_m, _, _ = _proj(meta_pad, proj_consts, META_ROWS)
    conv_consts = (a_m, _conv_shift_matrix(), dw_w[l].astype(F32), row(dw_b[l]),
                   row(conv_ln_g[l]), row(conv_ln_b[l]))
    act, q, k, v, lf, sgc, sga = _proj(x2d, proj_consts, PROJ_TM, conv_consts)

    o = _attention_t(q, k, v, lf, k_m, v_m, lf_m)

    wrt = jnp.zeros((LANES, D_MODEL), F32).at[:N_GROUPS].set(w_group[l].T)
    wrt = wrt.at[N_GROUPS:N_GROUPS + N_EXPERTS].set(w_router[l].T).astype(BF16)
    brt = jnp.zeros((LANES, 1), F32).at[:N_GROUPS, 0].set(b_group[l])
    brt = brt.at[N_GROUPS:N_GROUPS + N_EXPERTS, 0].set(b_router[l])
    tri = jnp.triu(jnp.ones((MERGE_TM, MERGE_TM), F32), 1).astype(BF16)
    merge_consts = (w_conv_out[l].astype(BF16), row(b_conv_out[l]), w_attn_out[l].astype(BF16),
                    w_out[l].astype(BF16), row(norm2_g[l]), wrt, brt, tri)
    h1, hn, route, route_t, cnt = _merge(act, o, sgc, sga, x2d, merge_consts)

    counts = cnt[:N_EXPERTS, 0].astype(jnp.int32)
    padded = (counts + ROW_BLOCK - 1) // ROW_BLOCK * ROW_BLOCK
    pend = jnp.cumsum(padded).astype(jnp.int32)
    pstart = pend - padded
    blk_row = jnp.arange(N_BLOCKS, dtype=jnp.int32) * ROW_BLOCK
    n_used = pend[-1:] // ROW_BLOCK
    nz = counts > 0
    before = (pend[None, :] <= blk_row[:, None]) & nz[None, :]
    blk_ord = jnp.sum(before.astype(jnp.int32), axis=1)
    blk_first = jnp.any((pstart[None, :] == blk_row[:, None]) & nz[None, :], axis=1).astype(jnp.int32)
    e_ord = jnp.cumsum(nz.astype(jnp.int32)) - 1
    e_ids = jnp.arange(N_EXPERTS, dtype=jnp.int32)
    used_e = jnp.sum(jnp.where((e_ord[None, :] == e_ids[:, None]) & nz[None, :], e_ids[None, :], 0), axis=1)
    n_used_e = jnp.sum(nz.astype(jnp.int32)).reshape(1)

    dest = _dest(pstart, route_t)
    d0, d1 = dest[0], dest[1]
    xin = _dispatch(d0, d1, pstart, pend, hn)
    yb = _experts(n_used, blk_first, blk_ord, used_e.astype(jnp.int32), n_used_e, xin,
                  w_gate[l], w_up[l], w_down[l])
    out = _combine(d0, d1, h1, route, yb)
    return out.reshape(BATCH, SEQ, D_MODEL)
```

```python
import jax
import jax.numpy as jnp
from jax import lax
from jax.experimental import pallas as pl
from jax.experimental.pallas import tpu as pltpu

D_MODEL = 1024
BATCH = 8
SEQ = 2048
N_META = 16
CONV_WIDTH = 512
CONV_K = 31
N_HEADS = 8
HEAD_DIM = 64
ATTN_WIDTH = N_HEADS * HEAD_DIM
N_GROUPS = 8
EXPERTS_PER_GROUP = 8
N_EXPERTS = N_GROUPS * EXPERTS_PER_GROUP
D_EXPERT = 256
RMS_EPS = 1e-6
LN_EPS = 1e-5

N_TOK = BATCH * SEQ
LANES = 128
META_ROWS = 128
ROW_BLOCK = 512
N_ASSIGN = 2 * N_TOK
N_BLOCKS = N_ASSIGN // ROW_BLOCK + N_EXPERTS
P_ROWS = N_BLOCKS * ROW_BLOCK
TOK_SUB = D_MODEL // LANES
PROJ_TM = 1024
CONV_CHUNK = 64
CONV_PAD = 32
CONV_WIN = CONV_CHUNK + 32
CONV_SHROWS = CONV_CHUNK + 24
CONV_COPIES = 7
CONV_LANES = 128
ATT_TQ = 1024
ATT_TK = 256
N_PAIRS = N_HEADS // 2
assert (ATT_TQ // ATT_TK) % 2 == 0
VMEM_LIMIT = 56 * 1024 * 1024

F32 = jnp.float32
BF16 = jnp.bfloat16
NEG_INF = float("-inf")
LOG2E = 1.4426950408889634


def _dot(a, b):
    return jnp.dot(a, b, preferred_element_type=F32)


def _dot_nt(a, b):
    return lax.dot_general(a, b, (((1,), (1,)), ((), ())), preferred_element_type=F32)


def _sigmoid(x):
    return 1.0 / (1.0 + jnp.exp(-x))


def _const_spec(shape):
    nd = len(shape)
    return pl.BlockSpec(shape, lambda *_: (0,) * nd)


def _head_rms(t, hsum_ref, hexp_ref, gain):
    ss = _dot((t * t).astype(BF16), hsum_ref[...])
    inv = lax.rsqrt(ss * (1.0 / HEAD_DIM) + RMS_EPS)
    hi = inv.astype(BF16)
    lo = (inv - hi.astype(F32)).astype(BF16)
    invb = _dot(jnp.concatenate([hi, lo], axis=1), hexp_ref[...])
    return t * invb * gain


def _proj_stages(x_ref, g1_ref, wu_ref, bu_ref, wqkv_ref, bqkv_ref, wft_ref, bft_ref, wgc_ref, bgc_ref,
                 wga_ref, bga_ref, qg_ref, kg_ref, hsum_ref, hexp_ref,
                 q_ref, k_ref, v_ref, lf_ref, sgc_ref, sga_ref):
    x = x_ref[...]
    ms = jnp.mean(x * x, axis=-1, keepdims=True)
    xn = (x * lax.rsqrt(ms + RMS_EPS) * g1_ref[...]).astype(BF16)

    def glu():
        u = _dot(xn, wu_ref[...]) + bu_ref[...]
        return (u[:, :CONV_WIDTH] * _sigmoid(u[:, CONV_WIDTH:])).astype(BF16)

    def qkv_stage():
        qkv = _dot(xn, wqkv_ref[...]) + bqkv_ref[...]
        q_ref[...] = _head_rms(qkv[:, :ATTN_WIDTH], hsum_ref, hexp_ref, qg_ref[...]).astype(BF16)
        k_ref[...] = _head_rms(qkv[:, ATTN_WIDTH:2 * ATTN_WIDTH], hsum_ref, hexp_ref, kg_ref[...]).astype(BF16)
        v = qkv[:, 2 * ATTN_WIDTH:]
        vb = v_ref.shape[2]
        for c in range(v_ref.shape[0]):
            v_ref[c] = v[c * vb:(c + 1) * vb, :].T.astype(BF16)

    def gate_stage_c():
        f = _dot_nt(wft_ref[...], xn) + bft_ref[...]
        lf_ref[...] = jnp.minimum(f, 0.0) - jnp.log(1.0 + jnp.exp(-jnp.abs(f)))
        sgc_ref[...] = _sigmoid(_dot(xn, wgc_ref[...]) + bgc_ref[...]).astype(BF16)

    def gate_stage_a():
        sga_ref[...] = _sigmoid(_dot(xn, wga_ref[...]) + bga_ref[...]).astype(BF16)

    return glu, (qkv_stage, gate_stage_c, gate_stage_a)


def _proj_kernel(*refs):
    ins, a_ref, outs = refs[:16], refs[16], refs[17:]
    glu, stages = _proj_stages(*ins, *outs)
    a_ref[...] = glu()
    for stage in stages:
        stage()


def _proj_conv_kernel(*refs):
    ins = refs[:16]
    am_ref, shift_ref, dw_ref, dwb_ref, lng_ref, lnb_ref = refs[16:22]
    act_ref, outs = refs[22], refs[23:29]
    win_scr, sh_scr = refs[29:]
    tm = act_ref.shape[0]
    i = pl.program_id(0)
    first = (i % (SEQ // tm)) == 0

    @pl.when(first)
    def _():
        win_scr[0:N_META, :] = jnp.zeros((N_META, CONV_WIDTH), BF16)
        win_scr[N_META:CONV_PAD, :] = am_ref[0:N_META, :]

    @pl.when(jnp.logical_not(first))
    def _():
        win_scr[0:CONV_PAD, :] = win_scr[tm:tm + CONV_PAD, :]

    glu, stages = _proj_stages(*ins, *outs)
    win_scr[CONV_PAD:CONV_PAD + tm, :] = glu()

    def conv_chunk(ci):
        r0 = ci * CONV_CHUNK
        slot = ci % 2
        sh_scr[slot] = _dot(shift_ref[...], win_scr[r0:r0 + CONV_WIN, :])
        parts = []
        for c0 in range(0, CONV_WIDTH, CONV_LANES):
            cs = slice(c0, c0 + CONV_LANES)
            part = jnp.zeros((CONV_CHUNK, CONV_LANES), F32) + dwb_ref[:, cs]
            for j in range(CONV_K):
                off = j + 2
                b = off % 8
                if b == 0:
                    tap = win_scr[r0 + off:r0 + off + CONV_CHUNK, cs].astype(F32)
                else:
                    a0 = (b - 1) * CONV_SHROWS + (off - b)
                    tap = sh_scr[slot, a0:a0 + CONV_CHUNK, cs]
                part = part + tap * dw_ref[j:j + 1, cs]
            parts.append(part)
        acc = jnp.concatenate(parts, axis=1)
        mu = jnp.mean(acc, axis=-1, keepdims=True)
        d = acc - mu
        var = jnp.mean(d * d, axis=-1, keepdims=True)
        y = d * lax.rsqrt(var + LN_EPS) * lng_ref[...] + lnb_ref[...]
        act_ref[r0:r0 + CONV_CHUNK, :] = (y * _sigmoid(y)).astype(BF16)

    n_chunks = tm // CONV_CHUNK
    per_stage = -(-n_chunks // (len(stages) + 1))
    ci = 0
    for stage in (None,) + stages:
        if stage is not None:
            stage()
        for _ in range(per_stage):
            if ci < n_chunks:
                conv_chunk(ci)
                ci += 1


def _proj(x2d, consts, tm, conv_consts=None):
    n = x2d.shape[0]
    row = lambda w: pl.BlockSpec((tm, w), lambda i: (i, 0))
    in_specs = [row(D_MODEL)] + [_const_spec(c.shape) for c in consts]
    fused = conv_consts is not None
    extra = tuple(conv_consts) if fused else ()
    in_specs += [_const_spec(c.shape) for c in extra]
    scratch = [pltpu.VMEM((CONV_PAD + tm, CONV_WIDTH), BF16),
               pltpu.VMEM((2, CONV_COPIES * CONV_SHROWS, CONV_WIDTH), F32)] if fused else []
    vb = min(tm, ATT_TK)
    out_shape = [
        jax.ShapeDtypeStruct((n, CONV_WIDTH), BF16),
        jax.ShapeDtypeStruct((n, ATTN_WIDTH), BF16),
        jax.ShapeDtypeStruct((n, ATTN_WIDTH), BF16),
        jax.ShapeDtypeStruct((n // vb, ATTN_WIDTH, vb), BF16),
        jax.ShapeDtypeStruct((16, n), F32),
        jax.ShapeDtypeStruct((n, D_MODEL), BF16),
        jax.ShapeDtypeStruct((n, D_MODEL), BF16),
    ]
    out_specs = [row(CONV_WIDTH), row(ATTN_WIDTH), row(ATTN_WIDTH),
                 pl.BlockSpec((tm // vb, ATTN_WIDTH, vb), lambda i: (i, 0, 0)),
                 pl.BlockSpec((16, tm), lambda i: (0, i)), row(D_MODEL), row(D_MODEL)]
    return pl.pallas_call(
        _proj_conv_kernel if fused else _proj_kernel,
        grid=(n // tm,),
        in_specs=in_specs,
        out_specs=out_specs,
        out_shape=out_shape,
        scratch_shapes=scratch,
        compiler_params=pltpu.CompilerParams(dimension_semantics=("arbitrary",), vmem_limit_bytes=VMEM_LIMIT),
        name="proj_conv" if fused else "proj",
    )(x2d, *consts, *extra)


def _conv_shift_matrix():
    rr = jnp.arange(CONV_COPIES * CONV_SHROWS)
    src = rr % CONV_SHROWS + rr // CONV_SHROWS + 1
    return (src[:, None] == jnp.arange(CONV_WIN)[None, :]).astype(BF16)


AUG_TERMS = 3


def _split_terms(x):
    hi = x.astype(BF16)
    r1 = x - hi.astype(F32)
    mid = r1.astype(BF16)
    lo = (r1 - mid.astype(F32)).astype(BF16)
    return hi, mid, lo


def _attn_t_kernel(q_ref, k_ref, vt_ref, lf_ref, km_ref, vm_ref, lfm_ref, place_ref, o_ref,
                   aug_scr, augm_scr, m_scr, acc_scr, sa_scr, sb_scr):
    p = pl.program_id(1)
    qi = pl.program_id(2)
    nk = SEQ // ATT_TK

    def bias_lanes(colsum):
        out = None
        for t, term in enumerate(_split_terms(colsum)):
            d = _dot(term, place_ref[t])
            out = d if out is None else out + d
        return out.astype(BF16)

    @pl.when((p == 0) & (qi == 0))
    def _():
        r = lax.broadcasted_iota(jnp.int32, (ATT_TK, ATT_TK), 0)
        c = lax.broadcasted_iota(jnp.int32, (ATT_TK, ATT_TK), 1)
        tri = (c <= r).astype(BF16)
        local = []
        for j in range(nk):
            lf = lf_ref[:, j * ATT_TK:(j + 1) * ATT_TK]
            csum = None
            for term in _split_terms(lf):
                d = _dot_nt(tri, term)
                csum = d if csum is None else csum + d
            local.append(csum)
        carry = jnp.zeros((1, 16), F32)
        for j in range(nk):
            csum = local[j] + carry
            aug_scr[j * ATT_TK:(j + 1) * ATT_TK, :] = bias_lanes(csum * (-LOG2E))
            carry = csum[ATT_TK - 1:ATT_TK, :]
        rm = lax.broadcasted_iota(jnp.int32, (META_ROWS, META_ROWS), 0)
        cm = lax.broadcasted_iota(jnp.int32, (META_ROWS, META_ROWS), 1)
        trim = ((cm > rm) & (cm < N_META)).astype(BF16)
        msum = jnp.zeros((META_ROWS, 16), F32)
        for term in _split_terms(lfm_ref[...]):
            msum = msum + _dot_nt(trim, term)
        augm_scr[...] = bias_lanes(msum * LOG2E)

    lane = lax.broadcasted_iota(jnp.int32, (ATT_TQ, LANES), 1)
    q_all = q_ref[...]
    zq = jnp.zeros_like(q_all)
    q_cat = []
    for hl in range(2):
        own = (lane < HEAD_DIM) if hl == 0 else (lane >= HEAD_DIM)
        h = 2 * p + hl
        bias_sel = (lane >= AUG_TERMS * h) & (lane < AUG_TERMS * h + AUG_TERMS)
        bias_ones = jnp.where(bias_sel, 1.0, 0.0).astype(BF16)
        q_cat.append(jnp.concatenate([jnp.where(own, q_all, zq), bias_ones], axis=1))
        m_scr[hl] = jnp.full((1, ATT_TQ), NEG_INF, F32)
        acc_scr[hl] = jnp.zeros((LANES, ATT_TQ), F32)

    def scores(kb, augb, lo=0):
        k_cat = jnp.concatenate([kb, augb], axis=1)
        return tuple(_dot_nt(k_cat, q_cat[hl][lo:]) for hl in range(2))

    def update(s2, vtb, mask, lo=0):
        vrow = lax.broadcasted_iota(jnp.int32, vtb.shape, 0)
        ones = jnp.ones_like(vtb)
        for hl in range(2):
            s = s2[hl]
            if mask is not None:
                s = jnp.where(mask, s, NEG_INF)
            m_prev = m_scr[hl, :, lo:]
            m_next = jnp.maximum(m_prev, jnp.max(s, axis=0, keepdims=True))
            pr = jnp.exp2(s - m_next).astype(BF16)
            alpha = jnp.exp2(m_prev - m_next)
            own = (vrow < HEAD_DIM) if hl == 0 else (vrow >= HEAD_DIM)
            acc_scr[hl, :, lo:] = acc_scr[hl, :, lo:] * alpha + _dot(jnp.where(own, vtb, ones), pr)
            m_scr[hl, :, lo:] = m_next

    def kblock(j):
        return k_ref[pl.ds(pl.multiple_of(j * ATT_TK, ATT_TK), ATT_TK), :]

    def augblock(j):
        return aug_scr[pl.ds(pl.multiple_of(j * ATT_TK, ATT_TK), ATT_TK), :]

    nfull = qi * (ATT_TQ // ATT_TK)

    def put(scr, s2):
        scr[0] = s2[0]
        scr[1] = s2[1]

    s_meta = scores(km_ref[0:N_META, :], augm_scr[0:N_META, :])
    put(sa_scr, scores(kblock(0), augblock(0)))
    update(s_meta, vm_ref[0][:, 0:N_META], None)

    def full2(i, carry):
        j = 2 * i
        put(sb_scr, scores(kblock(j + 1), augblock(j + 1)))
        update((sa_scr[0], sa_scr[1]), vt_ref[j], None)
        put(sa_scr, scores(kblock(j + 2), augblock(j + 2)))
        update((sb_scr[0], sb_scr[1]), vt_ref[j + 1], None)
        return carry

    lax.fori_loop(0, nfull // 2, full2, 0)

    srow = lax.broadcasted_iota(jnp.int32, (ATT_TK, ATT_TQ), 0)
    tcol = lax.broadcasted_iota(jnp.int32, (ATT_TK, ATT_TQ), 1)
    s_diag = [(sa_scr[0], sa_scr[1])]
    for dj in range(1, ATT_TQ // ATT_TK):
        s_diag.append(scores(kblock(nfull + dj), augblock(nfull + dj), lo=dj * ATT_TK))
    for dj in range(ATT_TQ // ATT_TK):
        lo = dj * ATT_TK
        update(s_diag[dj], vt_ref[nfull + dj], (srow <= tcol)[:, :ATT_TQ - lo], lo=lo)

    a0 = acc_scr[0]
    a1 = acc_scr[1]
    ot = jnp.concatenate([a0[0:HEAD_DIM] / a0[HEAD_DIM:], a1[HEAD_DIM:] / a1[0:HEAD_DIM]], axis=0)
    o_ref[...] = ot.T.astype(BF16)


def _attention_t(q, k, vt, lf, km, vm, lfm):
    nq = SEQ // ATT_TQ
    nk = SEQ // ATT_TK
    hh = jnp.arange(16)[:, None]
    ll = jnp.arange(LANES)[None, :]
    place = jnp.stack([((ll == AUG_TERMS * hh + t) & (hh < N_HEADS)).astype(BF16) for t in range(AUG_TERMS)])
    return pl.pallas_call(
        _attn_t_kernel,
        grid=(BATCH, N_PAIRS, nq),
        in_specs=[
            pl.BlockSpec((ATT_TQ, LANES), lambda b, p, i: (b * nq + i, p)),
            pl.BlockSpec((SEQ, LANES), lambda b, p, i: (b, p)),
            pl.BlockSpec((nk, LANES, ATT_TK), lambda b, p, i: (b, p, 0)),
            pl.BlockSpec((16, SEQ), lambda b, p, i: (0, b)),
            pl.BlockSpec((META_ROWS, LANES), lambda b, p, i: (0, p)),
            pl.BlockSpec((1, LANES, META_ROWS), lambda b, p, i: (0, p, 0)),
            pl.BlockSpec((16, META_ROWS), lambda b, p, i: (0, 0)),
            _const_spec(place.shape),
        ],
        out_specs=pl.BlockSpec((ATT_TQ, LANES), lambda b, p, i: (b * nq + i, p)),
        out_shape=jax.ShapeDtypeStruct((N_TOK, ATTN_WIDTH), BF16),
        scratch_shapes=[
            pltpu.VMEM((SEQ, LANES), BF16),
            pltpu.VMEM((META_ROWS, LANES), BF16),
            pltpu.VMEM((2, 1, ATT_TQ), F32),
            pltpu.VMEM((2, LANES, ATT_TQ), F32),
            pltpu.VMEM((2, ATT_TK, ATT_TQ), F32),
            pltpu.VMEM((2, ATT_TK, ATT_TQ), F32),
        ],
        compiler_params=pltpu.CompilerParams(
            dimension_semantics=("arbitrary", "arbitrary", "arbitrary"), vmem_limit_bytes=VMEM_LIMIT),
        name="attn",
    )(q, k, vt, lf, km, vm, lfm, place)


MERGE_TM = 1024


def _merge_kernel(act_ref, o_ref, sgc_ref, sga_ref, x_ref, wpw_ref, bpw_ref, wao_ref, wout_ref, g2_ref,
                  wrt_ref, brt_ref, tri_ref,
                  h1_ref, hn_ref, route_ref, routet_ref, cnt_ref, base_scr):
    i = pl.program_id(0)

    @pl.when(i == 0)
    def _():
        base_scr[...] = jnp.zeros_like(base_scr)

    y_conv = _dot(act_ref[...], wpw_ref[...]) + bpw_ref[...]
    y_attn = _dot(o_ref[...], wao_ref[...])
    merged = sgc_ref[...].astype(F32) * y_conv + sga_ref[...].astype(F32) * y_attn
    h1 = x_ref[...] + _dot(merged.astype(BF16), wout_ref[...])
    h1_ref[...] = h1
    ms = jnp.mean(h1 * h1, axis=-1, keepdims=True)
    hn = (h1 * lax.rsqrt(ms + RMS_EPS) * g2_ref[...]).astype(BF16)
    hn_ref[...] = hn.reshape(hn.shape[0], TOK_SUB, LANES)

    logits = _dot_nt(wrt_ref[...], hn) + brt_ref[...]
    row = lax.broadcasted_iota(jnp.int32, logits.shape, 0).astype(F32)
    big = float(LANES)
    is_g = row < N_GROUPS
    gl = jnp.where(is_g, logits, NEG_INF)
    gmax = jnp.max(gl, axis=0, keepdims=True)
    gidx = jnp.min(jnp.where(gl == gmax, row, big), axis=0, keepdims=True)
    gsum = jnp.sum(jnp.where(is_g, jnp.exp(gl - gmax), 0.0), axis=0, keepdims=True)
    g_w = 1.0 / gsum
    lo = N_GROUPS + EXPERTS_PER_GROUP * gidx
    el = jnp.where((row >= lo) & (row < lo + EXPERTS_PER_GROUP), logits, NEG_INF)
    t1 = jnp.max(el, axis=0, keepdims=True)
    i1 = jnp.min(jnp.where(el == t1, row, big), axis=0, keepdims=True)
    el2 = jnp.where(row == i1, NEG_INF, el)
    t2 = jnp.max(el2, axis=0, keepdims=True)
    i2 = jnp.min(jnp.where(el2 == t2, row, big), axis=0, keepdims=True)
    e2 = jnp.exp(t2 - t1)
    den = 1.0 + e2
    w1 = (1.0 / den) * g_w
    w2 = (e2 / den) * g_w
    ea = i1 - N_GROUPS
    eb = i2 - N_GROUPS

    oha = (row == ea).astype(F32)
    ohb = (row == eb).astype(F32)
    ohs = oha + ohb
    before = _dot(ohs.astype(BF16), tri_ref[...]) + base_scr[...][:, 0:1]
    ra = jnp.sum(before * oha, axis=0, keepdims=True)
    rb = jnp.sum(before * ohb, axis=0, keepdims=True)
    base_scr[...] = base_scr[...] + jnp.sum(ohs, axis=1, keepdims=True)
    cnt_ref[...] = base_scr[...]

    def rows_of(shape):
        r = lax.broadcasted_iota(jnp.int32, shape, 0)
        out = jnp.where(r == 0, ea, 0.0)
        for k, val in enumerate((eb, w1, w2, ra, rb), start=1):
            out = jnp.where(r == k, val, out)
        return out

    routet_ref[...] = rows_of((8, logits.shape[1]))
    route_ref[...] = rows_of(logits.shape).T


def _merge(act, o, sgc, sga, x2d, consts):
    tm = MERGE_TM
    row = lambda w: pl.BlockSpec((tm, w), lambda i: (i, 0))
    return pl.pallas_call(
        _merge_kernel,
        grid=(N_TOK // tm,),
        in_specs=[row(CONV_WIDTH), row(ATTN_WIDTH), row(D_MODEL), row(D_MODEL), row(D_MODEL)]
        + [_const_spec(c.shape) for c in consts],
        out_specs=[row(D_MODEL), pl.BlockSpec((tm, TOK_SUB, LANES), lambda i: (i, 0, 0)), row(LANES),
                   pl.BlockSpec((8, tm), lambda i: (0, i)), _const_spec((LANES, LANES))],
        out_shape=[jax.ShapeDtypeStruct((N_TOK, D_MODEL), F32),
                   jax.ShapeDtypeStruct((N_TOK, TOK_SUB, LANES), BF16),
                   jax.ShapeDtypeStruct((N_TOK, LANES), F32),
                   jax.ShapeDtypeStruct((8, N_TOK), F32),
                   jax.ShapeDtypeStruct((LANES, LANES), F32)],
        scratch_shapes=[pltpu.VMEM((LANES, LANES), F32)],
        compiler_params=pltpu.CompilerParams(dimension_semantics=("arbitrary",), vmem_limit_bytes=VMEM_LIMIT),
        name="merge",
    )(act, o, sgc, sga, x2d, *consts)


DISP_TM = 1024
DMA_UNROLL = 8


def _dest_kernel(pstart_ref, rt_ref, d_ref):
    rt = rt_ref[...]
    start = jnp.zeros_like(rt)
    for e in range(N_EXPERTS):
        start = jnp.where(rt == float(e), pstart_ref[e].astype(F32), start)
    d_ref[...] = (start + pltpu.roll(rt, 4, axis=0)).astype(jnp.int32)


def _dest(pstart, route_t):
    grid_spec = pltpu.PrefetchScalarGridSpec(
        num_scalar_prefetch=1,
        grid=(1,),
        in_specs=[pl.BlockSpec(route_t.shape, lambda i, *_: (0, 0))],
        out_specs=pl.BlockSpec(route_t.shape, lambda i, *_: (0, 0)),
    )
    return pl.pallas_call(
        _dest_kernel,
        grid_spec=grid_spec,
        out_shape=jax.ShapeDtypeStruct(route_t.shape, jnp.int32),
        compiler_params=pltpu.CompilerParams(dimension_semantics=("arbitrary",), vmem_limit_bytes=VMEM_LIMIT),
        name="dest",
    )(pstart, route_t)


ZERO_CHUNK = 64


def _dispatch_kernel(d0_ref, d1_ref, pfill_ref, pend_ref, hn_ref, xin_ref, zero_scr, sem, zsem):
    i = pl.program_id(0)
    base = i * DISP_TM

    @pl.when(i == 0)
    def _():
        zero_scr[...] = jnp.zeros_like(zero_scr)

        def pad_chunks(e, fn):
            first = pfill_ref[e] // ZERO_CHUNK * ZERO_CHUNK

            def one(c, carry):
                start = pl.multiple_of(first + c * ZERO_CHUNK, ZERO_CHUNK)
                fn(pltpu.make_async_copy(zero_scr.at[pl.ds(0, ZERO_CHUNK)], xin_ref.at[pl.ds(start, ZERO_CHUNK)],
                                         zsem.at[0]))
                return carry

            lax.fori_loop(0, (pend_ref[e] - first) // ZERO_CHUNK, one, 0)

        def zstart(e, carry):
            pad_chunks(e, lambda c: c.start())
            return carry

        def zwait(e, carry):
            pad_chunks(e, lambda c: c.wait())
            return carry

        n_used = pend_ref[N_EXPERTS - 1] // ROW_BLOCK

        def tcopy(b):
            return pltpu.make_async_copy(
                zero_scr, xin_ref.at[pl.ds(pl.multiple_of(b * ROW_BLOCK, ROW_BLOCK), ROW_BLOCK)], zsem.at[0])

        def tstart(b, carry):
            tcopy(b).start()
            return carry

        def twait(b, carry):
            tcopy(b).wait()
            return carry

        lax.fori_loop(0, N_EXPERTS, zstart, 0)
        lax.fori_loop(n_used, N_BLOCKS, tstart, 0)
        lax.fori_loop(0, N_EXPERTS, zwait, 0)
        lax.fori_loop(n_used, N_BLOCKS, twait, 0)

    def start(r, carry):
        pltpu.make_async_copy(hn_ref.at[r], xin_ref.at[d0_ref[base + r]], sem.at[0]).start()
        pltpu.make_async_copy(hn_ref.at[r], xin_ref.at[d1_ref[base + r]], sem.at[1]).start(priority=1)
        return carry

    lax.fori_loop(0, DISP_TM, start, 0, unroll=DMA_UNROLL)
    pltpu.make_async_copy(hn_ref, xin_ref.at[pl.ds(0, DISP_TM)], sem.at[0]).wait()
    pltpu.make_async_copy(hn_ref, xin_ref.at[pl.ds(0, DISP_TM)], sem.at[1]).wait()


def _dispatch(d0, d1, pfill, pend, hn):
    grid_spec = pltpu.PrefetchScalarGridSpec(
        num_scalar_prefetch=4,
        grid=(N_TOK // DISP_TM,),
        in_specs=[pl.BlockSpec((DISP_TM, TOK_SUB, LANES), lambda i, *_: (i, 0, 0))],
        out_specs=pl.BlockSpec(memory_space=pl.ANY),
        scratch_shapes=[pltpu.VMEM((ROW_BLOCK, TOK_SUB, LANES), BF16),
                        pltpu.SemaphoreType.DMA((2,)), pltpu.SemaphoreType.DMA((1,))],
    )
    return pl.pallas_call(
        _dispatch_kernel,
        grid_spec=grid_spec,
        out_shape=jax.ShapeDtypeStruct((P_ROWS, TOK_SUB, LANES), BF16),
        compiler_params=pltpu.CompilerParams(dimension_semantics=("arbitrary",), vmem_limit_bytes=VMEM_LIMIT),
        name="dispatch",
    )(d0, d1, pfill, pend, hn)


W_SLOTS = 3


def _expert_kernel(nu_ref, first_ref, ord_ref, used_ref, nue_ref, x_hbm, wg_hbm, wu_hbm, wd_hbm, y_hbm,
                   x_buf, y_buf, wg_f, wu_f, wd_f, wg_b, wu_b, wd_b, xsem, ysem, wsem):
    n = nu_ref[0]

    def rows(i):
        return pl.ds(pl.multiple_of(i * ROW_BLOCK, ROW_BLOCK), ROW_BLOCK)

    def xcopy(i, slot):
        return pltpu.make_async_copy(x_hbm.at[rows(i)], x_buf.at[slot], xsem.at[slot])

    def ycopy(i, slot):
        return pltpu.make_async_copy(y_buf.at[slot], y_hbm.at[rows(i)], ysem.at[slot])

    def fetch(k):
        e = used_ref[k]
        slot = k % W_SLOTS
        return (pltpu.make_async_copy(wg_hbm.at[e], wg_f.at[slot], wsem.at[slot, 0]),
                pltpu.make_async_copy(wu_hbm.at[e], wu_f.at[slot], wsem.at[slot, 1]),
                pltpu.make_async_copy(wd_hbm.at[e], wd_f.at[slot], wsem.at[slot, 2]))

    for k in range(W_SLOTS - 1):
        @pl.when(k < nue_ref[0])
        def _():
            for c in fetch(k):
                c.start()
    xcopy(0, 0).start()

    def block(i, carry):
        slot = i % 2
        xcopy(i, slot).wait()

        @pl.when(i + 1 < n)
        def _():
            xcopy(i + 1, 1 - slot).start()

        @pl.when(first_ref[i] == 1)
        def _():
            k = ord_ref[i]
            for c in fetch(k):
                c.wait()

            @pl.when(k + (W_SLOTS - 1) < nue_ref[0])
            def _():
                for c in fetch(k + (W_SLOTS - 1)):
                    c.start()

            wslot = k % W_SLOTS
            wg_b[...] = wg_f[wslot].astype(BF16)
            wu_b[...] = wu_f[wslot].astype(BF16)
            wd_b[...] = wd_f[wslot].astype(BF16)

        @pl.when(i >= 2)
        def _():
            ycopy(i - 2, slot).wait()

        x = x_buf[slot].reshape(ROW_BLOCK, D_MODEL)
        g = _dot(x, wg_b[...])
        u = _dot(x, wu_b[...])
        hmid = (g * _sigmoid(g) * u).astype(BF16)
        y = _dot(hmid, wd_b[...]).astype(BF16)
        y_buf[slot] = y.reshape(ROW_BLOCK, TOK_SUB, LANES)
        ycopy(i, slot).start()
        return carry

    lax.fori_loop(0, n, block, 0)

    @pl.when(n >= 2)
    def _():
        ycopy(n - 2, n % 2).wait()
    ycopy(n - 1, (n - 1) % 2).wait()

    y_buf[0] = jnp.zeros((ROW_BLOCK, TOK_SUB, LANES), BF16)

    def zcopy(b):
        return pltpu.make_async_copy(y_buf.at[0], y_hbm.at[rows(b)], ysem.at[0])

    def zstart(b, carry):
        zcopy(b).start()
        return carry

    def zwait(b, carry):
        zcopy(b).wait()
        return carry

    lax.fori_loop(n, N_BLOCKS, zstart, 0)
    lax.fori_loop(n, N_BLOCKS, zwait, 0)


def _experts(n_used, blk_first, blk_ord, used_e, n_used_e, xin, w_gate, w_up, w_down):
    grid_spec = pltpu.PrefetchScalarGridSpec(
        num_scalar_prefetch=5,
        grid=(1,),
        in_specs=[pl.BlockSpec(memory_space=pl.ANY),
                  pl.BlockSpec(memory_space=pl.ANY),
                  pl.BlockSpec(memory_space=pl.ANY),
                  pl.BlockSpec(memory_space=pl.ANY)],
        out_specs=pl.BlockSpec(memory_space=pl.ANY),
        scratch_shapes=[pltpu.VMEM((2, ROW_BLOCK, TOK_SUB, LANES), BF16),
                        pltpu.VMEM((2, ROW_BLOCK, TOK_SUB, LANES), BF16),
                        pltpu.VMEM((W_SLOTS, D_MODEL, D_EXPERT), F32),
                        pltpu.VMEM((W_SLOTS, D_MODEL, D_EXPERT), F32),
                        pltpu.VMEM((W_SLOTS, D_EXPERT, D_MODEL), F32),
                        pltpu.VMEM((D_MODEL, D_EXPERT), BF16),
                        pltpu.VMEM((D_MODEL, D_EXPERT), BF16),
                        pltpu.VMEM((D_EXPERT, D_MODEL), BF16),
                        pltpu.SemaphoreType.DMA((2,)), pltpu.SemaphoreType.DMA((2,)),
                        pltpu.SemaphoreType.DMA((W_SLOTS, 3))],
    )
    return pl.pallas_call(
        _expert_kernel,
        grid_spec=grid_spec,
        out_shape=jax.ShapeDtypeStruct((P_ROWS, TOK_SUB, LANES), BF16),
        compiler_params=pltpu.CompilerParams(dimension_semantics=("arbitrary",), vmem_limit_bytes=VMEM_LIMIT),
        name="experts",
    )(n_used, blk_first, blk_ord, used_e, n_used_e, xin, w_gate, w_up, w_down)


COMB_TM = 512


def _combine_kernel(d0_ref, d1_ref, h1_ref, route_ref, yb_ref, out_ref, g_scr, sem):
    i = pl.program_id(0)
    n = pl.num_programs(0)

    def issue(tile, slot):
        base = tile * COMB_TM

        def start(r, carry):
            pltpu.make_async_copy(yb_ref.at[d0_ref[base + r]], g_scr.at[slot, 0, r], sem.at[slot, 0]).start()
            pltpu.make_async_copy(yb_ref.at[d1_ref[base + r]], g_scr.at[slot, 1, r], sem.at[slot, 1]).start(
                priority=1)
            return carry

        lax.fori_loop(0, COMB_TM, start, 0, unroll=DMA_UNROLL)

    @pl.when(i == 0)
    def _():
        issue(0, 0)

    slot = i % 2

    @pl.when(i + 1 < n)
    def _():
        issue(i + 1, 1 - slot)

    for k in range(2):
        pltpu.make_async_copy(yb_ref.at[pl.ds(0, COMB_TM)], g_scr.at[slot, k], sem.at[slot, k]).wait()

    route = route_ref[...]
    w0 = route[:, 2:3]
    w1 = route[:, 3:4]
    g0 = g_scr[slot, 0].reshape(COMB_TM, D_MODEL).astype(F32)
    g1 = g_scr[slot, 1].reshape(COMB_TM, D_MODEL).astype(F32)
    out_ref[...] = h1_ref[...] + (g0 * w0 + g1 * w1)


def _combine(d0, d1, h1, route, yb):
    grid_spec = pltpu.PrefetchScalarGridSpec(
        num_scalar_prefetch=2,
        grid=(N_TOK // COMB_TM,),
        in_specs=[pl.BlockSpec((COMB_TM, D_MODEL), lambda i, *_: (i, 0)),
                  pl.BlockSpec((COMB_TM, LANES), lambda i, *_: (i, 0)),
                  pl.BlockSpec(memory_space=pl.ANY)],
        out_specs=pl.BlockSpec((COMB_TM, D_MODEL), lambda i, *_: (i, 0)),
        scratch_shapes=[pltpu.VMEM((2, 2, COMB_TM, TOK_SUB, LANES), BF16),
                        pltpu.SemaphoreType.DMA((2, 2))],
    )
    return pl.pallas_call(
        _combine_kernel,
        grid_spec=grid_spec,
        out_shape=jax.ShapeDtypeStruct((N_TOK, D_MODEL), F32),
        compiler_params=pltpu.CompilerParams(dimension_semantics=("arbitrary",), vmem_limit_bytes=VMEM_LIMIT),
        name="combine",
    )(d0, d1, h1, route, yb)


def kernel(x, meta, norm1_g, w_in, b_in, dw_w, dw_b, conv_ln_g, conv_ln_b, w_conv_out, b_conv_out,
           q_norm_g, k_norm_g, b_forget, w_attn_out, w_out, norm2_g, w_group, b_group, w_router,
           b_router, w_gate, w_up, w_down):
    l = 0
    row = lambda v: v.reshape(1, -1).astype(F32)

    c_u, c_q, c_f = 2 * CONV_WIDTH, 2 * CONV_WIDTH, 2 * CONV_WIDTH + 3 * ATTN_WIDTH
    c_gc = c_f + N_HEADS
    c_ga = c_gc + D_MODEL
    wi, bi = w_in[l], b_in[l]
    wu, bu = wi[:, :c_u].astype(BF16), row(bi[:c_u])
    wqkv, bqkv = wi[:, c_q:c_f].astype(BF16), row(bi[c_q:c_f])
    wft = jnp.zeros((16, D_MODEL), F32).at[:N_HEADS].set(wi[:, c_f:c_gc].T).astype(BF16)
    bft = jnp.zeros((16, 1), F32).at[:N_HEADS, 0].set(bi[c_f:c_gc] + b_forget[l])
    wgc, bgc = wi[:, c_gc:c_ga].astype(BF16), row(bi[c_gc:c_ga])
    wga, bga = wi[:, c_ga:].astype(BF16), row(bi[c_ga:])
    qg = row(jnp.tile(q_norm_g[l], N_HEADS) * (HEAD_DIM ** -0.5 * LOG2E))
    kg = row(jnp.tile(k_norm_g[l], N_HEADS))
    head_of_col = jnp.arange(ATTN_WIDTH) // HEAD_DIM
    hsum = (head_of_col[:, None] == jnp.arange(LANES)[None, :]).astype(BF16)
    hexp_half = (jnp.arange(LANES)[:, None] == head_of_col[None, :]).astype(BF16)
    hexp = jnp.concatenate([hexp_half, hexp_half], axis=0)
    proj_consts = (row(norm1_g[l]), wu, bu, wqkv, bqkv, wft, bft, wgc, bgc, wga, bga, qg, kg, hsum, hexp)

    x2d = x.reshape(N_TOK, D_MODEL)
    meta_pad = jnp.zeros((META_ROWS, D_MODEL), F32).at[:N_META].set(meta.astype(F32))
    a_m, _, k_m, v_m, lf_m, _, _ = _proj(meta_pad, proj_consts, META_ROWS)
    conv_consts = (a_m, _conv_shift_matrix(), dw_w[l].astype(F32), row(dw_b[l]),
                   row(conv_ln_g[l]), row(conv_ln_b[l]))
    act, q, k, v, lf, sgc, sga = _proj(x2d, proj_consts, PROJ_TM, conv_consts)

    o = _attention_t(q, k, v, lf, k_m, v_m, lf_m)

    wrt = jnp.zeros((LANES, D_MODEL), F32).at[:N_GROUPS].set(w_group[l].T)
    wrt = wrt.at[N_GROUPS:N_GROUPS + N_EXPERTS].set(w_router[l].T).astype(BF16)
    brt = jnp.zeros((LANES, 1), F32).at[:N_GROUPS, 0].set(b_group[l])
    brt = brt.at[N_GROUPS:N_GROUPS + N_EXPERTS, 0].set(b_router[l])
    tri = jnp.triu(jnp.ones((MERGE_TM, MERGE_TM), F32), 1).astype(BF16)
    merge_consts = (w_conv_out[l].astype(BF16), row(b_conv_out[l]), w_attn_out[l].astype(BF16),
                    w_out[l].astype(BF16), row(norm2_g[l]), wrt, brt, tri)
    h1, hn, route, route_t, cnt = _merge(act, o, sgc, sga, x2d, merge_consts)

    counts = cnt[:N_EXPERTS, 0].astype(jnp.int32)
    padded = (counts + ROW_BLOCK - 1) // ROW_BLOCK * ROW_BLOCK
    pend = jnp.cumsum(padded).astype(jnp.int32)
    pstart = pend - padded
    blk_row = jnp.arange(N_BLOCKS, dtype=jnp.int32) * ROW_BLOCK
    n_used = pend[-1:] // ROW_BLOCK
    nz = counts > 0
    before = (pend[None, :] <= blk_row[:, None]) & nz[None, :]
    blk_ord = jnp.sum(before.astype(jnp.int32), axis=1)
    blk_first = jnp.any((pstart[None, :] == blk_row[:, None]) & nz[None, :], axis=1).astype(jnp.int32)
    e_ord = jnp.cumsum(nz.astype(jnp.int32)) - 1
    e_ids = jnp.arange(N_EXPERTS, dtype=jnp.int32)
    used_e = jnp.sum(jnp.where((e_ord[None, :] == e_ids[:, None]) & nz[None, :], e_ids[None, :], 0), axis=1)
    n_used_e = jnp.sum(nz.astype(jnp.int32)).reshape(1)

    dest = _dest(pstart, route_t)
    d0, d1 = dest[0], dest[1]
    xin = _dispatch(d0, d1, pstart + counts, pend, hn)
    yb = _experts(n_used, blk_first, blk_ord, used_e.astype(jnp.int32), n_used_e, xin,
                  w_gate[l], w_up[l], w_down[l])
    out = _combine(d0, d1, h1, route, yb)
    return out.reshape(BATCH, SEQ, D_MODEL)
```

```python
import jax
import jax.numpy as jnp
from jax import lax
from jax.experimental import pallas as pl
from jax.experimental.pallas import tpu as pltpu

D_MODEL = 1024
BATCH = 8
SEQ = 2048
N_META = 16
CONV_WIDTH = 512
CONV_K = 31
N_HEADS = 8
HEAD_DIM = 64
ATTN_WIDTH = N_HEADS * HEAD_DIM
N_GROUPS = 8
EXPERTS_PER_GROUP = 8
N_EXPERTS = N_GROUPS * EXPERTS_PER_GROUP
D_EXPERT = 256
RMS_EPS = 1e-6
LN_EPS = 1e-5

N_TOK = BATCH * SEQ
LANES = 128
META_ROWS = 128
ROW_BLOCK = 512
N_ASSIGN = 2 * N_TOK
N_BLOCKS = N_ASSIGN // ROW_BLOCK + N_EXPERTS
P_ROWS = N_BLOCKS * ROW_BLOCK
TOK_SUB = D_MODEL // LANES
PROJ_TM = 1024
CONV_CHUNK = 64
CONV_PAD = 32
CONV_WIN = CONV_CHUNK + 32
CONV_SHROWS = CONV_CHUNK + 24
CONV_COPIES = 7
CONV_LANES = 128
ATT_TQ = 1024
ATT_TK = 256
N_PAIRS = N_HEADS // 2
assert (ATT_TQ // ATT_TK) % 2 == 0
VMEM_LIMIT = 56 * 1024 * 1024

F32 = jnp.float32
BF16 = jnp.bfloat16
NEG_INF = float("-inf")
LOG2E = 1.4426950408889634


def _dot(a, b):
    return jnp.dot(a, b, preferred_element_type=F32)


def _dot_nt(a, b):
    return lax.dot_general(a, b, (((1,), (1,)), ((), ())), preferred_element_type=F32)


def _sigmoid(x):
    return 1.0 / (1.0 + jnp.exp(-x))


def _const_spec(shape):
    nd = len(shape)
    return pl.BlockSpec(shape, lambda *_: (0,) * nd)


def _head_rms(t, hsum_ref, hexp_ref, gain):
    ss = _dot((t * t).astype(BF16), hsum_ref[...])
    inv = lax.rsqrt(ss * (1.0 / HEAD_DIM) + RMS_EPS)
    hi = inv.astype(BF16)
    lo = (inv - hi.astype(F32)).astype(BF16)
    invb = _dot(jnp.concatenate([hi, lo], axis=1), hexp_ref[...])
    return t * invb * gain


def _proj_stages(x_ref, g1_ref, wu_ref, bu_ref, wqkv_ref, bqkv_ref, wft_ref, bft_ref, wgc_ref, bgc_ref,
                 wga_ref, bga_ref, qg_ref, kg_ref, hsum_ref, hexp_ref,
                 q_ref, k_ref, v_ref, lf_ref, sgc_ref, sga_ref):
    x = x_ref[...]
    ms = jnp.mean(x * x, axis=-1, keepdims=True)
    xn = (x * lax.rsqrt(ms + RMS_EPS) * g1_ref[...]).astype(BF16)

    def glu():
        u = _dot(xn, wu_ref[...]) + bu_ref[...]
        return (u[:, :CONV_WIDTH] * _sigmoid(u[:, CONV_WIDTH:])).astype(BF16)

    def qkv_stage():
        qkv = _dot(xn, wqkv_ref[...]) + bqkv_ref[...]
        q_ref[...] = _head_rms(qkv[:, :ATTN_WIDTH], hsum_ref, hexp_ref, qg_ref[...]).astype(BF16)
        k_ref[...] = _head_rms(qkv[:, ATTN_WIDTH:2 * ATTN_WIDTH], hsum_ref, hexp_ref, kg_ref[...]).astype(BF16)
        v = qkv[:, 2 * ATTN_WIDTH:]
        vb = v_ref.shape[2]
        for c in range(v_ref.shape[0]):
            v_ref[c] = v[c * vb:(c + 1) * vb, :].T.astype(BF16)

    def gate_stage_c():
        f = _dot_nt(wft_ref[...], xn) + bft_ref[...]
        lf_ref[...] = jnp.minimum(f, 0.0) - jnp.log(1.0 + jnp.exp(-jnp.abs(f)))
        sgc_ref[...] = _sigmoid(_dot(xn, wgc_ref[...]) + bgc_ref[...]).astype(BF16)

    def gate_stage_a():
        sga_ref[...] = _sigmoid(_dot(xn, wga_ref[...]) + bga_ref[...]).astype(BF16)

    return glu, (qkv_stage, gate_stage_c, gate_stage_a)


def _proj_kernel(*refs):
    ins, a_ref, outs = refs[:16], refs[16], refs[17:]
    glu, stages = _proj_stages(*ins, *outs)
    a_ref[...] = glu()
    for stage in stages:
        stage()


def _proj_conv_kernel(*refs):
    ins = refs[:16]
    am_ref, shift_ref, dw_ref, dwb_ref, lng_ref, lnb_ref = refs[16:22]
    act_ref, outs = refs[22], refs[23:29]
    win_scr, sh_scr = refs[29:]
    tm = act_ref.shape[0]
    i = pl.program_id(0)
    first = (i % (SEQ // tm)) == 0

    @pl.when(first)
    def _():
        win_scr[0:N_META, :] = jnp.zeros((N_META, CONV_WIDTH), BF16)
        win_scr[N_META:CONV_PAD, :] = am_ref[0:N_META, :]

    @pl.when(jnp.logical_not(first))
    def _():
        win_scr[0:CONV_PAD, :] = win_scr[tm:tm + CONV_PAD, :]

    glu, stages = _proj_stages(*ins, *outs)
    win_scr[CONV_PAD:CONV_PAD + tm, :] = glu()

    def conv_chunk(ci):
        r0 = ci * CONV_CHUNK
        slot = ci % 2
        sh_scr[slot] = _dot(shift_ref[...], win_scr[r0:r0 + CONV_WIN, :])
        parts = []
        for c0 in range(0, CONV_WIDTH, CONV_LANES):
            cs = slice(c0, c0 + CONV_LANES)
            part = jnp.zeros((CONV_CHUNK, CONV_LANES), F32) + dwb_ref[:, cs]
            for j in range(CONV_K):
                off = j + 2
                b = off % 8
                if b == 0:
                    tap = win_scr[r0 + off:r0 + off + CONV_CHUNK, cs].astype(F32)
                else:
                    a0 = (b - 1) * CONV_SHROWS + (off - b)
                    tap = sh_scr[slot, a0:a0 + CONV_CHUNK, cs]
                part = part + tap * dw_ref[j:j + 1, cs]
            parts.append(part)
        acc = jnp.concatenate(parts, axis=1)
        mu = jnp.mean(acc, axis=-1, keepdims=True)
        d = acc - mu
        var = jnp.mean(d * d, axis=-1, keepdims=True)
        y = d * lax.rsqrt(var + LN_EPS) * lng_ref[...] + lnb_ref[...]
        act_ref[r0:r0 + CONV_CHUNK, :] = (y * _sigmoid(y)).astype(BF16)

    n_chunks = tm // CONV_CHUNK
    per_stage = -(-n_chunks // (len(stages) + 1))
    ci = 0
    for stage in (None,) + stages:
        if stage is not None:
            stage()
        for _ in range(per_stage):
            if ci < n_chunks:
                conv_chunk(ci)
                ci += 1


def _proj(x2d, consts, tm, conv_consts=None):
    n = x2d.shape[0]
    row = lambda w: pl.BlockSpec((tm, w), lambda i: (i, 0))
    in_specs = [row(D_MODEL)] + [_const_spec(c.shape) for c in consts]
    fused = conv_consts is not None
    extra = tuple(conv_consts) if fused else ()
    in_specs += [_const_spec(c.shape) for c in extra]
    scratch = [pltpu.VMEM((CONV_PAD + tm, CONV_WIDTH), BF16),
               pltpu.VMEM((2, CONV_COPIES * CONV_SHROWS, CONV_WIDTH), F32)] if fused else []
    vb = min(tm, ATT_TK)
    out_shape = [
        jax.ShapeDtypeStruct((n, CONV_WIDTH), BF16),
        jax.ShapeDtypeStruct((n, ATTN_WIDTH), BF16),
        jax.ShapeDtypeStruct((n, ATTN_WIDTH), BF16),
        jax.ShapeDtypeStruct((n // vb, ATTN_WIDTH, vb), BF16),
        jax.ShapeDtypeStruct((16, n), F32),
        jax.ShapeDtypeStruct((n, D_MODEL), BF16),
        jax.ShapeDtypeStruct((n, D_MODEL), BF16),
    ]
    out_specs = [row(CONV_WIDTH), row(ATTN_WIDTH), row(ATTN_WIDTH),
                 pl.BlockSpec((tm // vb, ATTN_WIDTH, vb), lambda i: (i, 0, 0)),
                 pl.BlockSpec((16, tm), lambda i: (0, i)), row(D_MODEL), row(D_MODEL)]
    return pl.pallas_call(
        _proj_conv_kernel if fused else _proj_kernel,
        grid=(n // tm,),
        in_specs=in_specs,
        out_specs=out_specs,
        out_shape=out_shape,
        scratch_shapes=scratch,
        compiler_params=pltpu.CompilerParams(dimension_semantics=("arbitrary",), vmem_limit_bytes=VMEM_LIMIT),
        name="proj_conv" if fused else "proj",
    )(x2d, *consts, *extra)


def _conv_shift_matrix():
    rr = jnp.arange(CONV_COPIES * CONV_SHROWS)
    src = rr % CONV_SHROWS + rr // CONV_SHROWS + 1
    return (src[:, None] == jnp.arange(CONV_WIN)[None, :]).astype(BF16)


AUG_TERMS = 3
ACC_ONES = 16


def _split_terms(x):
    hi = x.astype(BF16)
    r1 = x - hi.astype(F32)
    mid = r1.astype(BF16)
    lo = (r1 - mid.astype(F32)).astype(BF16)
    return hi, mid, lo


def _attn_t_kernel(q_ref, k_ref, vt_ref, lf_ref, km_ref, vm_ref, lfm_ref, place_ref, o_ref,
                   aug_scr, augm_scr, m_scr, acc_scr, sa_scr, sb_scr):
    p = pl.program_id(1)
    qi = pl.program_id(2)
    nk = SEQ // ATT_TK

    def bias_lanes(colsum):
        out = None
        for t, term in enumerate(_split_terms(colsum)):
            d = _dot(term, place_ref[t])
            out = d if out is None else out + d
        return out.astype(BF16)

    @pl.when((p == 0) & (qi == 0))
    def _():
        r = lax.broadcasted_iota(jnp.int32, (ATT_TK, ATT_TK), 0)
        c = lax.broadcasted_iota(jnp.int32, (ATT_TK, ATT_TK), 1)
        tri = (c <= r).astype(BF16)
        local = []
        for j in range(nk):
            lf = lf_ref[:, j * ATT_TK:(j + 1) * ATT_TK]
            csum = None
            for term in _split_terms(lf):
                d = _dot_nt(tri, term)
                csum = d if csum is None else csum + d
            local.append(csum)
        carry = jnp.zeros((1, 16), F32)
        for j in range(nk):
            csum = local[j] + carry
            aug_scr[j * ATT_TK:(j + 1) * ATT_TK, :] = bias_lanes(csum * (-LOG2E))
            carry = csum[ATT_TK - 1:ATT_TK, :]
        rm = lax.broadcasted_iota(jnp.int32, (META_ROWS, META_ROWS), 0)
        cm = lax.broadcasted_iota(jnp.int32, (META_ROWS, META_ROWS), 1)
        trim = ((cm > rm) & (cm < N_META)).astype(BF16)
        msum = jnp.zeros((META_ROWS, 16), F32)
        for term in _split_terms(lfm_ref[...]):
            msum = msum + _dot_nt(trim, term)
        augm_scr[...] = bias_lanes(msum * LOG2E)

    lane = lax.broadcasted_iota(jnp.int32, (ATT_TQ, LANES), 1)
    q_all = q_ref[...]
    zq = jnp.zeros_like(q_all)
    q_cat = []
    for hl in range(2):
        own = (lane < HEAD_DIM) if hl == 0 else (lane >= HEAD_DIM)
        h = 2 * p + hl
        bias_sel = (lane >= AUG_TERMS * h) & (lane < AUG_TERMS * h + AUG_TERMS)
        bias_ones = jnp.where(bias_sel, 1.0, 0.0).astype(BF16)
        q_cat.append(jnp.concatenate([jnp.where(own, q_all, zq), bias_ones], axis=1))
        m_scr[hl] = jnp.full((1, ATT_TQ), NEG_INF, F32)
        acc_scr[hl] = jnp.zeros((HEAD_DIM + ACC_ONES, ATT_TQ), F32)

    def scores(kb, augb, lo=0):
        k_cat = jnp.concatenate([kb, augb], axis=1)
        return tuple(_dot_nt(k_cat, q_cat[hl][lo:]) for hl in range(2))

    def update(s2, vtb, mask, lo=0):
        ones = jnp.ones((ACC_ONES, vtb.shape[1]), BF16)
        for hl in range(2):
            s = s2[hl]
            if mask is not None:
                s = jnp.where(mask, s, NEG_INF)
            m_prev = m_scr[hl, :, lo:]
            m_next = jnp.maximum(m_prev, jnp.max(s, axis=0, keepdims=True))
            pr = jnp.exp2(s - m_next).astype(BF16)
            alpha = jnp.exp2(m_prev - m_next)
            v_own = jnp.concatenate([vtb[hl * HEAD_DIM:(hl + 1) * HEAD_DIM], ones], axis=0)
            acc_scr[hl, :, lo:] = acc_scr[hl, :, lo:] * alpha + _dot(v_own, pr)
            m_scr[hl, :, lo:] = m_next

    def kblock(j):
        return k_ref[pl.ds(pl.multiple_of(j * ATT_TK, ATT_TK), ATT_TK), :]

    def augblock(j):
        return aug_scr[pl.ds(pl.multiple_of(j * ATT_TK, ATT_TK), ATT_TK), :]

    nfull = qi * (ATT_TQ // ATT_TK)

    def put(scr, s2):
        scr[0] = s2[0]
        scr[1] = s2[1]

    s_meta = scores(km_ref[0:N_META, :], augm_scr[0:N_META, :])
    put(sa_scr, scores(kblock(0), augblock(0)))
    update(s_meta, vm_ref[0][:, 0:N_META], None)

    def full2(i, carry):
        j = 2 * i
        put(sb_scr, scores(kblock(j + 1), augblock(j + 1)))
        update((sa_scr[0], sa_scr[1]), vt_ref[j], None)
        put(sa_scr, scores(kblock(j + 2), augblock(j + 2)))
        update((sb_scr[0], sb_scr[1]), vt_ref[j + 1], None)
        return carry

    lax.fori_loop(0, nfull // 2, full2, 0)

    srow = lax.broadcasted_iota(jnp.int32, (ATT_TK, ATT_TQ), 0)
    tcol = lax.broadcasted_iota(jnp.int32, (ATT_TK, ATT_TQ), 1)
    s_diag = [(sa_scr[0], sa_scr[1])]
    for dj in range(1, ATT_TQ // ATT_TK):
        s_diag.append(scores(kblock(nfull + dj), augblock(nfull + dj), lo=dj * ATT_TK))
    for dj in range(ATT_TQ // ATT_TK):
        lo = dj * ATT_TK
        update(s_diag[dj], vt_ref[nfull + dj], (srow <= tcol)[:, :ATT_TQ - lo], lo=lo)

    a0 = acc_scr[0]
    a1 = acc_scr[1]
    ot = jnp.concatenate([a0[0:HEAD_DIM] / a0[HEAD_DIM:HEAD_DIM + 1], a1[0:HEAD_DIM] / a1[HEAD_DIM:HEAD_DIM + 1]],
                         axis=0)
    o_ref[...] = ot.T.astype(BF16)


def _attention_t(q, k, vt, lf, km, vm, lfm):
    nq = SEQ // ATT_TQ
    nk = SEQ // ATT_TK
    hh = jnp.arange(16)[:, None]
    ll = jnp.arange(LANES)[None, :]
    place = jnp.stack([((ll == AUG_TERMS * hh + t) & (hh < N_HEADS)).astype(BF16) for t in range(AUG_TERMS)])
    return pl.pallas_call(
        _attn_t_kernel,
        grid=(BATCH, N_PAIRS, nq),
        in_specs=[
            pl.BlockSpec((ATT_TQ, LANES), lambda b, p, i: (b * nq + i, p)),
            pl.BlockSpec((SEQ, LANES), lambda b, p, i: (b, p)),
            pl.BlockSpec((nk, LANES, ATT_TK), lambda b, p, i: (b, p, 0)),
            pl.BlockSpec((16, SEQ), lambda b, p, i: (0, b)),
            pl.BlockSpec((META_ROWS, LANES), lambda b, p, i: (0, p)),
            pl.BlockSpec((1, LANES, META_ROWS), lambda b, p, i: (0, p, 0)),
            pl.BlockSpec((16, META_ROWS), lambda b, p, i: (0, 0)),
            _const_spec(place.shape),
        ],
        out_specs=pl.BlockSpec((ATT_TQ, LANES), lambda b, p, i: (b * nq + i, p)),
        out_shape=jax.ShapeDtypeStruct((N_TOK, ATTN_WIDTH), BF16),
        scratch_shapes=[
            pltpu.VMEM((SEQ, LANES), BF16),
            pltpu.VMEM((META_ROWS, LANES), BF16),
            pltpu.VMEM((2, 1, ATT_TQ), F32),
            pltpu.VMEM((2, HEAD_DIM + ACC_ONES, ATT_TQ), F32),
            pltpu.VMEM((2, ATT_TK, ATT_TQ), F32),
            pltpu.VMEM((2, ATT_TK, ATT_TQ), F32),
        ],
        compiler_params=pltpu.CompilerParams(
            dimension_semantics=("arbitrary", "arbitrary", "arbitrary"), vmem_limit_bytes=VMEM_LIMIT),
        name="attn",
    )(q, k, vt, lf, km, vm, lfm, place)


MERGE_TM = 1024


def _merge_kernel(act_ref, o_ref, sgc_ref, sga_ref, x_ref, wpw_ref, bpw_ref, wao_ref, wout_ref, g2_ref,
                  wrt_ref, brt_ref, tri_ref,
                  h1_ref, hn_ref, route_ref, routet_ref, cnt_ref, base_scr):
    i = pl.program_id(0)

    @pl.when(i == 0)
    def _():
        base_scr[...] = jnp.zeros_like(base_scr)

    y_conv = _dot(act_ref[...], wpw_ref[...]) + bpw_ref[...]
    y_attn = _dot(o_ref[...], wao_ref[...])
    merged = sgc_ref[...].astype(F32) * y_conv + sga_ref[...].astype(F32) * y_attn
    h1 = x_ref[...] + _dot(merged.astype(BF16), wout_ref[...])
    h1_ref[...] = h1
    ms = jnp.mean(h1 * h1, axis=-1, keepdims=True)
    hn = (h1 * lax.rsqrt(ms + RMS_EPS) * g2_ref[...]).astype(BF16)
    hn_ref[...] = hn.reshape(hn.shape[0], TOK_SUB, LANES)

    logits = _dot_nt(wrt_ref[...], hn) + brt_ref[...]
    row = lax.broadcasted_iota(jnp.int32, logits.shape, 0).astype(F32)
    big = float(LANES)
    is_g = row < N_GROUPS
    gl = jnp.where(is_g, logits, NEG_INF)
    gmax = jnp.max(gl, axis=0, keepdims=True)
    gidx = jnp.min(jnp.where(gl == gmax, row, big), axis=0, keepdims=True)
    gsum = jnp.sum(jnp.where(is_g, jnp.exp(gl - gmax), 0.0), axis=0, keepdims=True)
    g_w = 1.0 / gsum
    lo = N_GROUPS + EXPERTS_PER_GROUP * gidx
    el = jnp.where((row >= lo) & (row < lo + EXPERTS_PER_GROUP), logits, NEG_INF)
    t1 = jnp.max(el, axis=0, keepdims=True)
    i1 = jnp.min(jnp.where(el == t1, row, big), axis=0, keepdims=True)
    el2 = jnp.where(row == i1, NEG_INF, el)
    t2 = jnp.max(el2, axis=0, keepdims=True)
    i2 = jnp.min(jnp.where(el2 == t2, row, big), axis=0, keepdims=True)
    e2 = jnp.exp(t2 - t1)
    den = 1.0 + e2
    w1 = (1.0 / den) * g_w
    w2 = (e2 / den) * g_w
    ea = i1 - N_GROUPS
    eb = i2 - N_GROUPS

    oha = (row == ea).astype(F32)
    ohb = (row == eb).astype(F32)
    ohs = oha + ohb
    before = _dot(ohs.astype(BF16), tri_ref[...]) + base_scr[...][:, 0:1]
    ra = jnp.sum(before * oha, axis=0, keepdims=True)
    rb = jnp.sum(before * ohb, axis=0, keepdims=True)
    base_scr[...] = base_scr[...] + jnp.sum(ohs, axis=1, keepdims=True)
    cnt_ref[...] = base_scr[...]

    def rows_of(shape):
        r = lax.broadcasted_iota(jnp.int32, shape, 0)
        out = jnp.where(r == 0, ea, 0.0)
        for k, val in enumerate((eb, w1, w2, ra, rb), start=1):
            out = jnp.where(r == k, val, out)
        return out

    routet_ref[...] = rows_of((8, logits.shape[1]))
    route_ref[...] = rows_of(logits.shape).T


def _merge(act, o, sgc, sga, x2d, consts):
    tm = MERGE_TM
    row = lambda w: pl.BlockSpec((tm, w), lambda i: (i, 0))
    return pl.pallas_call(
        _merge_kernel,
        grid=(N_TOK // tm,),
        in_specs=[row(CONV_WIDTH), row(ATTN_WIDTH), row(D_MODEL), row(D_MODEL), row(D_MODEL)]
        + [_const_spec(c.shape) for c in consts],
        out_specs=[row(D_MODEL), pl.BlockSpec((tm, TOK_SUB, LANES), lambda i: (i, 0, 0)), row(LANES),
                   pl.BlockSpec((8, tm), lambda i: (0, i)), _const_spec((LANES, LANES))],
        out_shape=[jax.ShapeDtypeStruct((N_TOK, D_MODEL), F32),
                   jax.ShapeDtypeStruct((N_TOK, TOK_SUB, LANES), BF16),
                   jax.ShapeDtypeStruct((N_TOK, LANES), F32),
                   jax.ShapeDtypeStruct((8, N_TOK), F32),
                   jax.ShapeDtypeStruct((LANES, LANES), F32)],
        scratch_shapes=[pltpu.VMEM((LANES, LANES), F32)],
        compiler_params=pltpu.CompilerParams(dimension_semantics=("arbitrary",), vmem_limit_bytes=VMEM_LIMIT),
        name="merge",
    )(act, o, sgc, sga, x2d, *consts)


DISP_TM = 1024
DMA_UNROLL = 8


def _dest_kernel(pstart_ref, rt_ref, d_ref):
    rt = rt_ref[...]
    start = jnp.zeros_like(rt)
    for e in range(N_EXPERTS):
        start = jnp.where(rt == float(e), pstart_ref[e].astype(F32), start)
    d_ref[...] = (start + pltpu.roll(rt, 4, axis=0)).astype(jnp.int32)


def _dest(pstart, route_t):
    grid_spec = pltpu.PrefetchScalarGridSpec(
        num_scalar_prefetch=1,
        grid=(1,),
        in_specs=[pl.BlockSpec(route_t.shape, lambda i, *_: (0, 0))],
        out_specs=pl.BlockSpec(route_t.shape, lambda i, *_: (0, 0)),
    )
    return pl.pallas_call(
        _dest_kernel,
        grid_spec=grid_spec,
        out_shape=jax.ShapeDtypeStruct(route_t.shape, jnp.int32),
        compiler_params=pltpu.CompilerParams(dimension_semantics=("arbitrary",), vmem_limit_bytes=VMEM_LIMIT),
        name="dest",
    )(pstart, route_t)


ZERO_CHUNK = 64


def _dispatch_kernel(d0_ref, d1_ref, pfill_ref, pend_ref, hn_ref, xin_ref, zero_scr, sem, zsem):
    i = pl.program_id(0)
    base = i * DISP_TM

    @pl.when(i == 0)
    def _():
        zero_scr[...] = jnp.zeros_like(zero_scr)

        def pad_chunks(e, fn):
            first = pfill_ref[e] // ZERO_CHUNK * ZERO_CHUNK

            def one(c, carry):
                start = pl.multiple_of(first + c * ZERO_CHUNK, ZERO_CHUNK)
                fn(pltpu.make_async_copy(zero_scr.at[pl.ds(0, ZERO_CHUNK)], xin_ref.at[pl.ds(start, ZERO_CHUNK)],
                                         zsem.at[0]))
                return carry

            lax.fori_loop(0, (pend_ref[e] - first) // ZERO_CHUNK, one, 0)

        def zstart(e, carry):
            pad_chunks(e, lambda c: c.start())
            return carry

        def zwait(e, carry):
            pad_chunks(e, lambda c: c.wait())
            return carry

        n_used = pend_ref[N_EXPERTS - 1] // ROW_BLOCK

        def tcopy(b):
            return pltpu.make_async_copy(
                zero_scr, xin_ref.at[pl.ds(pl.multiple_of(b * ROW_BLOCK, ROW_BLOCK), ROW_BLOCK)], zsem.at[0])

        def tstart(b, carry):
            tcopy(b).start()
            return carry

        def twait(b, carry):
            tcopy(b).wait()
            return carry

        lax.fori_loop(0, N_EXPERTS, zstart, 0)
        lax.fori_loop(n_used, N_BLOCKS, tstart, 0)
        lax.fori_loop(0, N_EXPERTS, zwait, 0)
        lax.fori_loop(n_used, N_BLOCKS, twait, 0)

    def start(r, carry):
        pltpu.make_async_copy(hn_ref.at[r], xin_ref.at[d0_ref[base + r]], sem.at[0]).start()
        pltpu.make_async_copy(hn_ref.at[r], xin_ref.at[d1_ref[base + r]], sem.at[1]).start(priority=1)
        return carry

    lax.fori_loop(0, DISP_TM, start, 0, unroll=DMA_UNROLL)
    pltpu.make_async_copy(hn_ref, xin_ref.at[pl.ds(0, DISP_TM)], sem.at[0]).wait()
    pltpu.make_async_copy(hn_ref, xin_ref.at[pl.ds(0, DISP_TM)], sem.at[1]).wait()


def _dispatch(d0, d1, pfill, pend, hn):
    grid_spec = pltpu.PrefetchScalarGridSpec(
        num_scalar_prefetch=4,
        grid=(N_TOK // DISP_TM,),
        in_specs=[pl.BlockSpec((DISP_TM, TOK_SUB, LANES), lambda i, *_: (i, 0, 0))],
        out_specs=pl.BlockSpec(memory_space=pl.ANY),
        scratch_shapes=[pltpu.VMEM((ROW_BLOCK, TOK_SUB, LANES), BF16),
                        pltpu.SemaphoreType.DMA((2,)), pltpu.SemaphoreType.DMA((1,))],
    )
    return pl.pallas_call(
        _dispatch_kernel,
        grid_spec=grid_spec,
        out_shape=jax.ShapeDtypeStruct((P_ROWS, TOK_SUB, LANES), BF16),
        compiler_params=pltpu.CompilerParams(dimension_semantics=("arbitrary",), vmem_limit_bytes=VMEM_LIMIT),
        name="dispatch",
    )(d0, d1, pfill, pend, hn)


W_SLOTS = 3


def _expert_kernel(nu_ref, first_ref, ord_ref, used_ref, nue_ref, x_hbm, wg_hbm, wu_hbm, wd_hbm, y_hbm,
                   x_buf, y_buf, wg_f, wu_f, wd_f, wg_b, wu_b, wd_b, xsem, ysem, wsem):
    n = nu_ref[0]

    def rows(i):
        return pl.ds(pl.multiple_of(i * ROW_BLOCK, ROW_BLOCK), ROW_BLOCK)

    def xcopy(i, slot):
        return pltpu.make_async_copy(x_hbm.at[rows(i)], x_buf.at[slot], xsem.at[slot])

    def ycopy(i, slot):
        return pltpu.make_async_copy(y_buf.at[slot], y_hbm.at[rows(i)], ysem.at[slot])

    def fetch(k):
        e = used_ref[k]
        slot = k % W_SLOTS
        return (pltpu.make_async_copy(wg_hbm.at[e], wg_f.at[slot], wsem.at[slot, 0]),
                pltpu.make_async_copy(wu_hbm.at[e], wu_f.at[slot], wsem.at[slot, 1]),
                pltpu.make_async_copy(wd_hbm.at[e], wd_f.at[slot], wsem.at[slot, 2]))

    for k in range(W_SLOTS - 1):
        @pl.when(k < nue_ref[0])
        def _():
            for c in fetch(k):
                c.start()
    xcopy(0, 0).start()

    def block(i, carry):
        slot = i % 2
        xcopy(i, slot).wait()

        @pl.when(i + 1 < n)
        def _():
            xcopy(i + 1, 1 - slot).start()

        @pl.when(first_ref[i] == 1)
        def _():
            k = ord_ref[i]
            for c in fetch(k):
                c.wait()

            @pl.when(k + (W_SLOTS - 1) < nue_ref[0])
            def _():
                for c in fetch(k + (W_SLOTS - 1)):
                    c.start()

            wslot = k % W_SLOTS
            wg_b[...] = wg_f[wslot].astype(BF16)
            wu_b[...] = wu_f[wslot].astype(BF16)
            wd_b[...] = wd_f[wslot].astype(BF16)

        @pl.when(i >= 2)
        def _():
            ycopy(i - 2, slot).wait()

        x = x_buf[slot].reshape(ROW_BLOCK, D_MODEL)
        g = _dot(x, wg_b[...])
        u = _dot(x, wu_b[...])
        hmid = (g * _sigmoid(g) * u).astype(BF16)
        y = _dot(hmid, wd_b[...]).astype(BF16)
        y_buf[slot] = y.reshape(ROW_BLOCK, TOK_SUB, LANES)
        ycopy(i, slot).start()
        return carry

    lax.fori_loop(0, n, block, 0)

    @pl.when(n >= 2)
    def _():
        ycopy(n - 2, n % 2).wait()
    ycopy(n - 1, (n - 1) % 2).wait()

    y_buf[0] = jnp.zeros((ROW_BLOCK, TOK_SUB, LANES), BF16)

    def zcopy(b):
        return pltpu.make_async_copy(y_buf.at[0], y_hbm.at[rows(b)], ysem.at[0])

    def zstart(b, carry):
        zcopy(b).start()
        return carry

    def zwait(b, carry):
        zcopy(b).wait()
        return carry

    lax.fori_loop(n, N_BLOCKS, zstart, 0)
    lax.fori_loop(n, N_BLOCKS, zwait, 0)


def _experts(n_used, blk_first, blk_ord, used_e, n_used_e, xin, w_gate, w_up, w_down):
    grid_spec = pltpu.PrefetchScalarGridSpec(
        num_scalar_prefetch=5,
        grid=(1,),
        in_specs=[pl.BlockSpec(memory_space=pl.ANY),
                  pl.BlockSpec(memory_space=pl.ANY),
                  pl.BlockSpec(memory_space=pl.ANY),
                  pl.BlockSpec(memory_space=pl.ANY)],
        out_specs=pl.BlockSpec(memory_space=pl.ANY),
        scratch_shapes=[pltpu.VMEM((2, ROW_BLOCK, TOK_SUB, LANES), BF16),
                        pltpu.VMEM((2, ROW_BLOCK, TOK_SUB, LANES), BF16),
                        pltpu.VMEM((W_SLOTS, D_MODEL, D_EXPERT), F32),
                        pltpu.VMEM((W_SLOTS, D_MODEL, D_EXPERT), F32),
                        pltpu.VMEM((W_SLOTS, D_EXPERT, D_MODEL), F32),
                        pltpu.VMEM((D_MODEL, D_EXPERT), BF16),
                        pltpu.VMEM((D_MODEL, D_EXPERT), BF16),
                        pltpu.VMEM((D_EXPERT, D_MODEL), BF16),
                        pltpu.SemaphoreType.DMA((2,)), pltpu.SemaphoreType.DMA((2,)),
                        pltpu.SemaphoreType.DMA((W_SLOTS, 3))],
    )
    return pl.pallas_call(
        _expert_kernel,
        grid_spec=grid_spec,
        out_shape=jax.ShapeDtypeStruct((P_ROWS, TOK_SUB, LANES), BF16),
        compiler_params=pltpu.CompilerParams(dimension_semantics=("arbitrary",), vmem_limit_bytes=VMEM_LIMIT),
        name="experts",
    )(n_used, blk_first, blk_ord, used_e, n_used_e, xin, w_gate, w_up, w_down)


COMB_TM = 512


def _combine_kernel(d0_ref, d1_ref, h1_ref, route_ref, yb_ref, out_ref, g_scr, sem):
    i = pl.program_id(0)
    n = pl.num_programs(0)

    def issue(tile, slot):
        base = tile * COMB_TM

        def start(r, carry):
            pltpu.make_async_copy(yb_ref.at[d0_ref[base + r]], g_scr.at[slot, 0, r], sem.at[slot, 0]).start()
            pltpu.make_async_copy(yb_ref.at[d1_ref[base + r]], g_scr.at[slot, 1, r], sem.at[slot, 1]).start(
                priority=1)
            return carry

        lax.fori_loop(0, COMB_TM, start, 0, unroll=DMA_UNROLL)

    @pl.when(i == 0)
    def _():
        issue(0, 0)

    slot = i % 2

    @pl.when(i + 1 < n)
    def _():
        issue(i + 1, 1 - slot)

    for k in range(2):
        pltpu.make_async_copy(yb_ref.at[pl.ds(0, COMB_TM)], g_scr.at[slot, k], sem.at[slot, k]).wait()

    route = route_ref[...]
    w0 = route[:, 2:3]
    w1 = route[:, 3:4]
    g0 = g_scr[slot, 0].reshape(COMB_TM, D_MODEL).astype(F32)
    g1 = g_scr[slot, 1].reshape(COMB_TM, D_MODEL).astype(F32)
    out_ref[...] = h1_ref[...] + (g0 * w0 + g1 * w1)


def _combine(d0, d1, h1, route, yb):
    grid_spec = pltpu.PrefetchScalarGridSpec(
        num_scalar_prefetch=2,
        grid=(N_TOK // COMB_TM,),
        in_specs=[pl.BlockSpec((COMB_TM, D_MODEL), lambda i, *_: (i, 0)),
                  pl.BlockSpec((COMB_TM, LANES), lambda i, *_: (i, 0)),
                  pl.BlockSpec(memory_space=pl.ANY)],
        out_specs=pl.BlockSpec((COMB_TM, D_MODEL), lambda i, *_: (i, 0)),
        scratch_shapes=[pltpu.VMEM((2, 2, COMB_TM, TOK_SUB, LANES), BF16),
                        pltpu.SemaphoreType.DMA((2, 2))],
    )
    return pl.pallas_call(
        _combine_kernel,
        grid_spec=grid_spec,
        out_shape=jax.ShapeDtypeStruct((N_TOK, D_MODEL), F32),
        compiler_params=pltpu.CompilerParams(dimension_semantics=("arbitrary",), vmem_limit_bytes=VMEM_LIMIT),
        name="combine",
    )(d0, d1, h1, route, yb)


def kernel(x, meta, norm1_g, w_in, b_in, dw_w, dw_b, conv_ln_g, conv_ln_b, w_conv_out, b_conv_out,
           q_norm_g, k_norm_g, b_forget, w_attn_out, w_out, norm2_g, w_group, b_group, w_router,
           b_router, w_gate, w_up, w_down):
    l = 0
    row = lambda v: v.reshape(1, -1).astype(F32)

    c_u, c_q, c_f = 2 * CONV_WIDTH, 2 * CONV_WIDTH, 2 * CONV_WIDTH + 3 * ATTN_WIDTH
    c_gc = c_f + N_HEADS
    c_ga = c_gc + D_MODEL
    wi, bi = w_in[l], b_in[l]
    wu, bu = wi[:, :c_u].astype(BF16), row(bi[:c_u])
    wqkv, bqkv = wi[:, c_q:c_f].astype(BF16), row(bi[c_q:c_f])
    wft = jnp.zeros((16, D_MODEL), F32).at[:N_HEADS].set(wi[:, c_f:c_gc].T).astype(BF16)
    bft = jnp.zeros((16, 1), F32).at[:N_HEADS, 0].set(bi[c_f:c_gc] + b_forget[l])
    wgc, bgc = wi[:, c_gc:c_ga].astype(BF16), row(bi[c_gc:c_ga])
    wga, bga = wi[:, c_ga:].astype(BF16), row(bi[c_ga:])
    qg = row(jnp.tile(q_norm_g[l], N_HEADS) * (HEAD_DIM ** -0.5 * LOG2E))
    kg = row(jnp.tile(k_norm_g[l], N_HEADS))
    head_of_col = jnp.arange(ATTN_WIDTH) // HEAD_DIM
    hsum = (head_of_col[:, None] == jnp.arange(LANES)[None, :]).astype(BF16)
    hexp_half = (jnp.arange(LANES)[:, None] == head_of_col[None, :]).astype(BF16)
    hexp = jnp.concatenate([hexp_half, hexp_half], axis=0)
    proj_consts = (row(norm1_g[l]), wu, bu, wqkv, bqkv, wft, bft, wgc, bgc, wga, bga, qg, kg, hsum, hexp)

    x2d = x.reshape(N_TOK, D_MODEL)
    meta_pad = jnp.zeros((META_ROWS, D_MODEL), F32).at[:N_META].set(meta.astype(F32))
    a_m, _, k_m, v_m, lf_m, _, _ = _proj(meta_pad, proj_consts, META_ROWS)
    conv_consts = (a_m, _conv_shift_matrix(), dw_w[l].astype(F32), row(dw_b[l]),
                   row(conv_ln_g[l]), row(conv_ln_b[l]))
    act, q, k, v, lf, sgc, sga = _proj(x2d, proj_consts, PROJ_TM, conv_consts)

    o = _attention_t(q, k, v, lf, k_m, v_m, lf_m)

    wrt = jnp.zeros((LANES, D_MODEL), F32).at[:N_GROUPS].set(w_group[l].T)
    wrt = wrt.at[N_GROUPS:N_GROUPS + N_EXPERTS].set(w_router[l].T).astype(BF16)
    brt = jnp.zeros((LANES, 1), F32).at[:N_GROUPS, 0].set(b_group[l])
    brt = brt.at[N_GROUPS:N_GROUPS + N_EXPERTS, 0].set(b_router[l])
    tri = jnp.triu(jnp.ones((MERGE_TM, MERGE_TM), F32), 1).astype(BF16)
    merge_consts = (w_conv_out[l].astype(BF16), row(b_conv_out[l]), w_attn_out[l].astype(BF16),
                    w_out[l].astype(BF16), row(norm2_g[l]), wrt, brt, tri)
    h1, hn, route, route_t, cnt = _merge(act, o, sgc, sga, x2d, merge_consts)

    counts = cnt[:N_EXPERTS, 0].astype(jnp.int32)
    padded = (counts + ROW_BLOCK - 1) // ROW_BLOCK * ROW_BLOCK
    pend = jnp.cumsum(padded).astype(jnp.int32)
    pstart = pend - padded
    blk_row = jnp.arange(N_BLOCKS, dtype=jnp.int32) * ROW_BLOCK
    n_used = pend[-1:] // ROW_BLOCK
    nz = counts > 0
    before = (pend[None, :] <= blk_row[:, None]) & nz[None, :]
    blk_ord = jnp.sum(before.astype(jnp.int32), axis=1)
    blk_first = jnp.any((pstart[None, :] == blk_row[:, None]) & nz[None, :], axis=1).astype(jnp.int32)
    e_ord = jnp.cumsum(nz.astype(jnp.int32)) - 1
    e_ids = jnp.arange(N_EXPERTS, dtype=jnp.int32)
    used_e = jnp.sum(jnp.where((e_ord[None, :] == e_ids[:, None]) & nz[None, :], e_ids[None, :], 0), axis=1)
    n_used_e = jnp.sum(nz.astype(jnp.int32)).reshape(1)

    dest = _dest(pstart, route_t)
    d0, d1 = dest[0], dest[1]
    xin = _dispatch(d0, d1, pstart + counts, pend, hn)
    yb = _experts(n_used, blk_first, blk_ord, used_e.astype(jnp.int32), n_used_e, xin,
                  w_gate[l], w_up[l], w_down[l])
    out = _combine(d0, d1, h1, route, yb)
    return out.reshape(BATCH, SEQ, D_MODEL)
```

```python
import jax
import jax.numpy as jnp
from jax import lax
from jax.experimental import pallas as pl
from jax.experimental.pallas import tpu as pltpu

D_MODEL = 1024
BATCH = 8
SEQ = 2048
N_META = 16
CONV_WIDTH = 512
CONV_K = 31
N_HEADS = 8
HEAD_DIM = 64
ATTN_WIDTH = N_HEADS * HEAD_DIM
N_GROUPS = 8
EXPERTS_PER_GROUP = 8
N_EXPERTS = N_GROUPS * EXPERTS_PER_GROUP
D_EXPERT = 256
RMS_EPS = 1e-6
LN_EPS = 1e-5

N_TOK = BATCH * SEQ
LANES = 128
META_ROWS = 128
ROW_BLOCK = 512
N_ASSIGN = 2 * N_TOK
N_BLOCKS = N_ASSIGN // ROW_BLOCK + N_EXPERTS
P_ROWS = N_BLOCKS * ROW_BLOCK
TOK_SUB = D_MODEL // LANES
PROJ_TM = 1024
CONV_CHUNK = 64
CONV_PAD = 32
CONV_WIN = CONV_CHUNK + 32
CONV_SHROWS = CONV_CHUNK + 24
CONV_COPIES = 7
CONV_LANES = 128
ATT_TQ = 2048
ATT_TK = 256
N_PAIRS = N_HEADS // 2
assert (ATT_TQ // ATT_TK) % 2 == 0
VMEM_LIMIT = 56 * 1024 * 1024

F32 = jnp.float32
BF16 = jnp.bfloat16
NEG_INF = float("-inf")
LOG2E = 1.4426950408889634


def _dot(a, b):
    return jnp.dot(a, b, preferred_element_type=F32)


def _dot_nt(a, b):
    return lax.dot_general(a, b, (((1,), (1,)), ((), ())), preferred_element_type=F32)


def _sigmoid(x):
    return 1.0 / (1.0 + jnp.exp(-x))


def _const_spec(shape):
    nd = len(shape)
    return pl.BlockSpec(shape, lambda *_: (0,) * nd)


def _head_rms(t, hsum_ref, hexp_ref, gain):
    ss = _dot((t * t).astype(BF16), hsum_ref[...])
    inv = lax.rsqrt(ss * (1.0 / HEAD_DIM) + RMS_EPS)
    hi = inv.astype(BF16)
    lo = (inv - hi.astype(F32)).astype(BF16)
    invb = _dot(jnp.concatenate([hi, lo], axis=1), hexp_ref[...])
    return t * invb * gain


def _proj_stages(x_ref, g1_ref, wu_ref, bu_ref, wqkv_ref, bqkv_ref, wft_ref, bft_ref, wgc_ref, bgc_ref,
                 wga_ref, bga_ref, qg_ref, kg_ref, hsum_ref, hexp_ref,
                 q_ref, k_ref, v_ref, lf_ref, sgc_ref, sga_ref):
    x = x_ref[...]
    ms = jnp.mean(x * x, axis=-1, keepdims=True)
    xn = (x * lax.rsqrt(ms + RMS_EPS) * g1_ref[...]).astype(BF16)

    def glu():
        u = _dot(xn, wu_ref[...]) + bu_ref[...]
        return (u[:, :CONV_WIDTH] * _sigmoid(u[:, CONV_WIDTH:])).astype(BF16)

    def qkv_stage():
        qkv = _dot(xn, wqkv_ref[...]) + bqkv_ref[...]
        q_ref[...] = _head_rms(qkv[:, :ATTN_WIDTH], hsum_ref, hexp_ref, qg_ref[...]).astype(BF16)
        k_ref[...] = _head_rms(qkv[:, ATTN_WIDTH:2 * ATTN_WIDTH], hsum_ref, hexp_ref, kg_ref[...]).astype(BF16)
        v = qkv[:, 2 * ATTN_WIDTH:]
        vb = v_ref.shape[2]
        for c in range(v_ref.shape[0]):
            v_ref[c] = v[c * vb:(c + 1) * vb, :].T.astype(BF16)

    def gate_stage_c():
        f = _dot_nt(wft_ref[...], xn) + bft_ref[...]
        lf_ref[...] = jnp.minimum(f, 0.0) - jnp.log(1.0 + jnp.exp(-jnp.abs(f)))
        sgc_ref[...] = _sigmoid(_dot(xn, wgc_ref[...]) + bgc_ref[...]).astype(BF16)

    def gate_stage_a():
        sga_ref[...] = _sigmoid(_dot(xn, wga_ref[...]) + bga_ref[...]).astype(BF16)

    return glu, (qkv_stage, gate_stage_c, gate_stage_a)


def _proj_kernel(*refs):
    ins, a_ref, outs = refs[:16], refs[16], refs[17:]
    glu, stages = _proj_stages(*ins, *outs)
    a_ref[...] = glu()
    for stage in stages:
        stage()


def _proj_conv_kernel(*refs):
    ins = refs[:16]
    am_ref, shift_ref, dw_ref, dwb_ref, lng_ref, lnb_ref = refs[16:22]
    act_ref, outs = refs[22], refs[23:29]
    win_scr, sh_scr = refs[29:]
    tm = act_ref.shape[0]
    i = pl.program_id(0)
    first = (i % (SEQ // tm)) == 0

    @pl.when(first)
    def _():
        win_scr[0:N_META, :] = jnp.zeros((N_META, CONV_WIDTH), BF16)
        win_scr[N_META:CONV_PAD, :] = am_ref[0:N_META, :]

    @pl.when(jnp.logical_not(first))
    def _():
        win_scr[0:CONV_PAD, :] = win_scr[tm:tm + CONV_PAD, :]

    glu, stages = _proj_stages(*ins, *outs)
    win_scr[CONV_PAD:CONV_PAD + tm, :] = glu()

    def conv_chunk(ci):
        r0 = ci * CONV_CHUNK
        slot = ci % 2
        sh_scr[slot] = _dot(shift_ref[...], win_scr[r0:r0 + CONV_WIN, :])
        parts = []
        for c0 in range(0, CONV_WIDTH, CONV_LANES):
            cs = slice(c0, c0 + CONV_LANES)
            part = jnp.zeros((CONV_CHUNK, CONV_LANES), F32) + dwb_ref[:, cs]
            for j in range(CONV_K):
                off = j + 2
                b = off % 8
                if b == 0:
                    tap = win_scr[r0 + off:r0 + off + CONV_CHUNK, cs].astype(F32)
                else:
                    a0 = (b - 1) * CONV_SHROWS + (off - b)
                    tap = sh_scr[slot, a0:a0 + CONV_CHUNK, cs]
                part = part + tap * dw_ref[j:j + 1, cs]
            parts.append(part)
        acc = jnp.concatenate(parts, axis=1)
        mu = jnp.mean(acc, axis=-1, keepdims=True)
        d = acc - mu
        var = jnp.mean(d * d, axis=-1, keepdims=True)
        y = d * lax.rsqrt(var + LN_EPS) * lng_ref[...] + lnb_ref[...]
        act_ref[r0:r0 + CONV_CHUNK, :] = (y * _sigmoid(y)).astype(BF16)

    n_chunks = tm // CONV_CHUNK
    per_stage = -(-n_chunks // (len(stages) + 1))
    ci = 0
    for stage in (None,) + stages:
        if stage is not None:
            stage()
        for _ in range(per_stage):
            if ci < n_chunks:
                conv_chunk(ci)
                ci += 1


def _proj(x2d, consts, tm, conv_consts=None):
    n = x2d.shape[0]
    row = lambda w: pl.BlockSpec((tm, w), lambda i: (i, 0))
    in_specs = [row(D_MODEL)] + [_const_spec(c.shape) for c in consts]
    fused = conv_consts is not None
    extra = tuple(conv_consts) if fused else ()
    in_specs += [_const_spec(c.shape) for c in extra]
    scratch = [pltpu.VMEM((CONV_PAD + tm, CONV_WIDTH), BF16),
               pltpu.VMEM((2, CONV_COPIES * CONV_SHROWS, CONV_WIDTH), F32)] if fused else []
    vb = min(tm, ATT_TK)
    out_shape = [
        jax.ShapeDtypeStruct((n, CONV_WIDTH), BF16),
        jax.ShapeDtypeStruct((n, ATTN_WIDTH), BF16),
        jax.ShapeDtypeStruct((n, ATTN_WIDTH), BF16),
        jax.ShapeDtypeStruct((n // vb, ATTN_WIDTH, vb), BF16),
        jax.ShapeDtypeStruct((16, n), F32),
        jax.ShapeDtypeStruct((n, D_MODEL), BF16),
        jax.ShapeDtypeStruct((n, D_MODEL), BF16),
    ]
    out_specs = [row(CONV_WIDTH), row(ATTN_WIDTH), row(ATTN_WIDTH),
                 pl.BlockSpec((tm // vb, ATTN_WIDTH, vb), lambda i: (i, 0, 0)),
                 pl.BlockSpec((16, tm), lambda i: (0, i)), row(D_MODEL), row(D_MODEL)]
    return pl.pallas_call(
        _proj_conv_kernel if fused else _proj_kernel,
        grid=(n // tm,),
        in_specs=in_specs,
        out_specs=out_specs,
        out_shape=out_shape,
        scratch_shapes=scratch,
        compiler_params=pltpu.CompilerParams(dimension_semantics=("arbitrary",), vmem_limit_bytes=VMEM_LIMIT),
        name="proj_conv" if fused else "proj",
    )(x2d, *consts, *extra)


def _conv_shift_matrix():
    rr = jnp.arange(CONV_COPIES * CONV_SHROWS)
    src = rr % CONV_SHROWS + rr // CONV_SHROWS + 1
    return (src[:, None] == jnp.arange(CONV_WIN)[None, :]).astype(BF16)


AUG_TERMS = 3
ACC_ONES = 16


def _split_terms(x):
    hi = x.astype(BF16)
    r1 = x - hi.astype(F32)
    mid = r1.astype(BF16)
    lo = (r1 - mid.astype(F32)).astype(BF16)
    return hi, mid, lo


def _attn_t_kernel(q_ref, k_ref, vt_ref, lf_ref, km_ref, vm_ref, lfm_ref, place_ref, o_ref,
                   aug_scr, augm_scr, m_scr, acc_scr, sa_scr, sb_scr):
    p = pl.program_id(1)
    qi = pl.program_id(2)
    nk = SEQ // ATT_TK

    def bias_lanes(colsum):
        out = None
        for t, term in enumerate(_split_terms(colsum)):
            d = _dot(term, place_ref[t])
            out = d if out is None else out + d
        return out.astype(BF16)

    @pl.when((p == 0) & (qi == 0))
    def _():
        r = lax.broadcasted_iota(jnp.int32, (ATT_TK, ATT_TK), 0)
        c = lax.broadcasted_iota(jnp.int32, (ATT_TK, ATT_TK), 1)
        tri = (c <= r).astype(BF16)
        local = []
        for j in range(nk):
            lf = lf_ref[:, j * ATT_TK:(j + 1) * ATT_TK]
            csum = None
            for term in _split_terms(lf):
                d = _dot_nt(tri, term)
                csum = d if csum is None else csum + d
            local.append(csum)
        carry = jnp.zeros((1, 16), F32)
        for j in range(nk):
            csum = local[j] + carry
            aug_scr[j * ATT_TK:(j + 1) * ATT_TK, :] = bias_lanes(csum * (-LOG2E))
            carry = csum[ATT_TK - 1:ATT_TK, :]
        rm = lax.broadcasted_iota(jnp.int32, (META_ROWS, META_ROWS), 0)
        cm = lax.broadcasted_iota(jnp.int32, (META_ROWS, META_ROWS), 1)
        trim = ((cm > rm) & (cm < N_META)).astype(BF16)
        msum = jnp.zeros((META_ROWS, 16), F32)
        for term in _split_terms(lfm_ref[...]):
            msum = msum + _dot_nt(trim, term)
        augm_scr[...] = bias_lanes(msum * LOG2E)

    lane = lax.broadcasted_iota(jnp.int32, (ATT_TQ, LANES), 1)
    q_all = q_ref[...]
    zq = jnp.zeros_like(q_all)
    q_cat = []
    for hl in range(2):
        own = (lane < HEAD_DIM) if hl == 0 else (lane >= HEAD_DIM)
        h = 2 * p + hl
        bias_sel = (lane >= AUG_TERMS * h) & (lane < AUG_TERMS * h + AUG_TERMS)
        bias_ones = jnp.where(bias_sel, 1.0, 0.0).astype(BF16)
        q_cat.append(jnp.concatenate([jnp.where(own, q_all, zq), bias_ones], axis=1))
        m_scr[hl] = jnp.full((1, ATT_TQ), NEG_INF, F32)
        acc_scr[hl] = jnp.zeros((HEAD_DIM + ACC_ONES, ATT_TQ), F32)

    def scores(kb, augb, lo=0):
        k_cat = jnp.concatenate([kb, augb], axis=1)
        return tuple(_dot_nt(k_cat, q_cat[hl][lo:]) for hl in range(2))

    def update(s2, vtb, mask, lo=0):
        ones = jnp.ones((ACC_ONES, vtb.shape[1]), BF16)
        for hl in range(2):
            s = s2[hl]
            if mask is not None:
                s = jnp.where(mask, s, NEG_INF)
            m_prev = m_scr[hl, :, lo:]
            m_next = jnp.maximum(m_prev, jnp.max(s, axis=0, keepdims=True))
            pr = jnp.exp2(s - m_next).astype(BF16)
            alpha = jnp.exp2(m_prev - m_next)
            v_own = jnp.concatenate([vtb[hl * HEAD_DIM:(hl + 1) * HEAD_DIM], ones], axis=0)
            acc_scr[hl, :, lo:] = acc_scr[hl, :, lo:] * alpha + _dot(v_own, pr)
            m_scr[hl, :, lo:] = m_next

    def kblock(j):
        return k_ref[pl.ds(pl.multiple_of(j * ATT_TK, ATT_TK), ATT_TK), :]

    def augblock(j):
        return aug_scr[pl.ds(pl.multiple_of(j * ATT_TK, ATT_TK), ATT_TK), :]

    nfull = qi * (ATT_TQ // ATT_TK)

    def put(scr, s2):
        scr[0] = s2[0]
        scr[1] = s2[1]

    s_meta = scores(km_ref[0:N_META, :], augm_scr[0:N_META, :])
    put(sa_scr, scores(kblock(0), augblock(0)))
    update(s_meta, vm_ref[0][:, 0:N_META], None)

    def full2(i, carry):
        j = 2 * i
        put(sb_scr, scores(kblock(j + 1), augblock(j + 1)))
        update((sa_scr[0], sa_scr[1]), vt_ref[j], None)
        put(sa_scr, scores(kblock(j + 2), augblock(j + 2)))
        update((sb_scr[0], sb_scr[1]), vt_ref[j + 1], None)
        return carry

    lax.fori_loop(0, nfull // 2, full2, 0)

    srow = lax.broadcasted_iota(jnp.int32, (ATT_TK, ATT_TQ), 0)
    tcol = lax.broadcasted_iota(jnp.int32, (ATT_TK, ATT_TQ), 1)
    s_diag = [(sa_scr[0], sa_scr[1])]
    for dj in range(1, ATT_TQ // ATT_TK):
        s_diag.append(scores(kblock(nfull + dj), augblock(nfull + dj), lo=dj * ATT_TK))
    for dj in range(ATT_TQ // ATT_TK):
        lo = dj * ATT_TK
        update(s_diag[dj], vt_ref[nfull + dj], (srow <= tcol)[:, :ATT_TQ - lo], lo=lo)

    a0 = acc_scr[0]
    a1 = acc_scr[1]
    ot = jnp.concatenate([a0[0:HEAD_DIM] / a0[HEAD_DIM:HEAD_DIM + 1], a1[0:HEAD_DIM] / a1[HEAD_DIM:HEAD_DIM + 1]],
                         axis=0)
    o_ref[...] = ot.T.astype(BF16)


def _attention_t(q, k, vt, lf, km, vm, lfm):
    nq = SEQ // ATT_TQ
    nk = SEQ // ATT_TK
    hh = jnp.arange(16)[:, None]
    ll = jnp.arange(LANES)[None, :]
    place = jnp.stack([((ll == AUG_TERMS * hh + t) & (hh < N_HEADS)).astype(BF16) for t in range(AUG_TERMS)])
    return pl.pallas_call(
        _attn_t_kernel,
        grid=(BATCH, N_PAIRS, nq),
        in_specs=[
            pl.BlockSpec((ATT_TQ, LANES), lambda b, p, i: (b * nq + i, p)),
            pl.BlockSpec((SEQ, LANES), lambda b, p, i: (b, p)),
            pl.BlockSpec((nk, LANES, ATT_TK), lambda b, p, i: (b, p, 0)),
            pl.BlockSpec((16, SEQ), lambda b, p, i: (0, b)),
            pl.BlockSpec((META_ROWS, LANES), lambda b, p, i: (0, p)),
            pl.BlockSpec((1, LANES, META_ROWS), lambda b, p, i: (0, p, 0)),
            pl.BlockSpec((16, META_ROWS), lambda b, p, i: (0, 0)),
            _const_spec(place.shape),
        ],
        out_specs=pl.BlockSpec((ATT_TQ, LANES), lambda b, p, i: (b * nq + i, p)),
        out_shape=jax.ShapeDtypeStruct((N_TOK, ATTN_WIDTH), BF16),
        scratch_shapes=[
            pltpu.VMEM((SEQ, LANES), BF16),
            pltpu.VMEM((META_ROWS, LANES), BF16),
            pltpu.VMEM((2, 1, ATT_TQ), F32),
            pltpu.VMEM((2, HEAD_DIM + ACC_ONES, ATT_TQ), F32),
            pltpu.VMEM((2, ATT_TK, ATT_TQ), F32),
            pltpu.VMEM((2, ATT_TK, ATT_TQ), F32),
        ],
        compiler_params=pltpu.CompilerParams(
            dimension_semantics=("arbitrary", "arbitrary", "arbitrary"), vmem_limit_bytes=VMEM_LIMIT),
        name="attn",
    )(q, k, vt, lf, km, vm, lfm, place)


MERGE_TM = 1024


def _merge_kernel(act_ref, o_ref, sgc_ref, sga_ref, x_ref, wpw_ref, bpw_ref, wao_ref, wout_ref, g2_ref,
                  wrt_ref, brt_ref, tri_ref,
                  h1_ref, hn_ref, route_ref, routet_ref, cnt_ref, base_scr):
    i = pl.program_id(0)

    @pl.when(i == 0)
    def _():
        base_scr[...] = jnp.zeros_like(base_scr)

    tm = x_ref.shape[0]
    hn_parts = []
    for rs in (slice(0, tm // 2), slice(tm // 2, tm)):
        y_conv = _dot(act_ref[rs, :], wpw_ref[...]) + bpw_ref[...]
        y_attn = _dot(o_ref[rs, :], wao_ref[...])
        merged = sgc_ref[rs, :].astype(F32) * y_conv + sga_ref[rs, :].astype(F32) * y_attn
        h1 = x_ref[rs, :] + _dot(merged.astype(BF16), wout_ref[...])
        h1_ref[rs, :] = h1
        ms = jnp.mean(h1 * h1, axis=-1, keepdims=True)
        hn_half = (h1 * lax.rsqrt(ms + RMS_EPS) * g2_ref[...]).astype(BF16)
        hn_ref[rs] = hn_half.reshape(tm // 2, TOK_SUB, LANES)
        hn_parts.append(hn_half)
    hn = jnp.concatenate(hn_parts, axis=0)

    logits = _dot_nt(wrt_ref[...], hn) + brt_ref[...]
    row = lax.broadcasted_iota(jnp.int32, logits.shape, 0).astype(F32)
    big = float(LANES)
    is_g = row < N_GROUPS
    gl = jnp.where(is_g, logits, NEG_INF)
    gmax = jnp.max(gl, axis=0, keepdims=True)
    gidx = jnp.min(jnp.where(gl == gmax, row, big), axis=0, keepdims=True)
    gsum = jnp.sum(jnp.where(is_g, jnp.exp(gl - gmax), 0.0), axis=0, keepdims=True)
    g_w = 1.0 / gsum
    lo = N_GROUPS + EXPERTS_PER_GROUP * gidx
    el = jnp.where((row >= lo) & (row < lo + EXPERTS_PER_GROUP), logits, NEG_INF)
    t1 = jnp.max(el, axis=0, keepdims=True)
    i1 = jnp.min(jnp.where(el == t1, row, big), axis=0, keepdims=True)
    el2 = jnp.where(row == i1, NEG_INF, el)
    t2 = jnp.max(el2, axis=0, keepdims=True)
    i2 = jnp.min(jnp.where(el2 == t2, row, big), axis=0, keepdims=True)
    e2 = jnp.exp(t2 - t1)
    den = 1.0 + e2
    w1 = (1.0 / den) * g_w
    w2 = (e2 / den) * g_w
    ea = i1 - N_GROUPS
    eb = i2 - N_GROUPS

    oha = (row == ea).astype(F32)
    ohb = (row == eb).astype(F32)
    ohs = oha + ohb
    before = _dot(ohs.astype(BF16), tri_ref[...]) + base_scr[...][:, 0:1]
    ra = jnp.sum(before * oha, axis=0, keepdims=True)
    rb = jnp.sum(before * ohb, axis=0, keepdims=True)
    base_scr[...] = base_scr[...] + jnp.sum(ohs, axis=1, keepdims=True)
    cnt_ref[...] = base_scr[...]

    def rows_of(shape):
        r = lax.broadcasted_iota(jnp.int32, shape, 0)
        out = jnp.where(r == 0, ea, 0.0)
        for k, val in enumerate((eb, w1, w2, ra, rb), start=1):
            out = jnp.where(r == k, val, out)
        return out

    routet_ref[...] = rows_of((8, logits.shape[1]))
    route_ref[...] = rows_of(logits.shape).T


def _merge(act, o, sgc, sga, x2d, consts):
    tm = MERGE_TM
    row = lambda w: pl.BlockSpec((tm, w), lambda i: (i, 0))
    return pl.pallas_call(
        _merge_kernel,
        grid=(N_TOK // tm,),
        in_specs=[row(CONV_WIDTH), row(ATTN_WIDTH), row(D_MODEL), row(D_MODEL), row(D_MODEL)]
        + [_const_spec(c.shape) for c in consts],
        out_specs=[row(D_MODEL), pl.BlockSpec((tm, TOK_SUB, LANES), lambda i: (i, 0, 0)), row(LANES),
                   pl.BlockSpec((8, tm), lambda i: (0, i)), _const_spec((LANES, LANES))],
        out_shape=[jax.ShapeDtypeStruct((N_TOK, D_MODEL), F32),
                   jax.ShapeDtypeStruct((N_TOK, TOK_SUB, LANES), BF16),
                   jax.ShapeDtypeStruct((N_TOK, LANES), F32),
                   jax.ShapeDtypeStruct((8, N_TOK), F32),
                   jax.ShapeDtypeStruct((LANES, LANES), F32)],
        scratch_shapes=[pltpu.VMEM((LANES, LANES), F32)],
        compiler_params=pltpu.CompilerParams(dimension_semantics=("arbitrary",), vmem_limit_bytes=VMEM_LIMIT),
        name="merge",
    )(act, o, sgc, sga, x2d, *consts)


DISP_TM = 1024
DMA_UNROLL = 8


def _dest_kernel(pstart_ref, rt_ref, d_ref):
    rt = rt_ref[...]
    start = jnp.zeros_like(rt)
    for e in range(N_EXPERTS):
        start = jnp.where(rt == float(e), pstart_ref[e].astype(F32), start)
    d_ref[...] = (start + pltpu.roll(rt, 4, axis=0)).astype(jnp.int32)


def _dest(pstart, route_t):
    grid_spec = pltpu.PrefetchScalarGridSpec(
        num_scalar_prefetch=1,
        grid=(1,),
        in_specs=[pl.BlockSpec(route_t.shape, lambda i, *_: (0, 0))],
        out_specs=pl.BlockSpec(route_t.shape, lambda i, *_: (0, 0)),
    )
    return pl.pallas_call(
        _dest_kernel,
        grid_spec=grid_spec,
        out_shape=jax.ShapeDtypeStruct(route_t.shape, jnp.int32),
        compiler_params=pltpu.CompilerParams(dimension_semantics=("arbitrary",), vmem_limit_bytes=VMEM_LIMIT),
        name="dest",
    )(pstart, route_t)


ZERO_CHUNK = 64


def _dispatch_kernel(d0_ref, d1_ref, pfill_ref, pend_ref, hn_ref, xin_ref, zero_scr, sem, zsem):
    i = pl.program_id(0)
    base = i * DISP_TM

    @pl.when(i == 0)
    def _():
        zero_scr[...] = jnp.zeros_like(zero_scr)

        def pad_chunks(e, fn):
            first = pfill_ref[e] // ZERO_CHUNK * ZERO_CHUNK

            def one(c, carry):
                start = pl.multiple_of(first + c * ZERO_CHUNK, ZERO_CHUNK)
                fn(pltpu.make_async_copy(zero_scr.at[pl.ds(0, ZERO_CHUNK)], xin_ref.at[pl.ds(start, ZERO_CHUNK)],
                                         zsem.at[0]))
                return carry

            lax.fori_loop(0, (pend_ref[e] - first) // ZERO_CHUNK, one, 0)

        def zstart(e, carry):
            pad_chunks(e, lambda c: c.start())
            return carry

        def zwait(e, carry):
            pad_chunks(e, lambda c: c.wait())
            return carry

        n_used = pend_ref[N_EXPERTS - 1] // ROW_BLOCK

        def tcopy(b):
            return pltpu.make_async_copy(
                zero_scr, xin_ref.at[pl.ds(pl.multiple_of(b * ROW_BLOCK, ROW_BLOCK), ROW_BLOCK)], zsem.at[0])

        def tstart(b, carry):
            tcopy(b).start()
            return carry

        def twait(b, carry):
            tcopy(b).wait()
            return carry

        lax.fori_loop(0, N_EXPERTS, zstart, 0)
        lax.fori_loop(n_used, N_BLOCKS, tstart, 0)
        lax.fori_loop(0, N_EXPERTS, zwait, 0)
        lax.fori_loop(n_used, N_BLOCKS, twait, 0)

    def start(r, carry):
        pltpu.make_async_copy(hn_ref.at[r], xin_ref.at[d0_ref[base + r]], sem.at[0]).start()
        pltpu.make_async_copy(hn_ref.at[r], xin_ref.at[d1_ref[base + r]], sem.at[1]).start(priority=1)
        return carry

    lax.fori_loop(0, DISP_TM, start, 0, unroll=DMA_UNROLL)
    pltpu.make_async_copy(hn_ref, xin_ref.at[pl.ds(0, DISP_TM)], sem.at[0]).wait()
    pltpu.make_async_copy(hn_ref, xin_ref.at[pl.ds(0, DISP_TM)], sem.at[1]).wait()


def _dispatch(d0, d1, pfill, pend, hn):
    grid_spec = pltpu.PrefetchScalarGridSpec(
        num_scalar_prefetch=4,
        grid=(N_TOK // DISP_TM,),
        in_specs=[pl.BlockSpec((DISP_TM, TOK_SUB, LANES), lambda i, *_: (i, 0, 0))],
        out_specs=pl.BlockSpec(memory_space=pl.ANY),
        scratch_shapes=[pltpu.VMEM((ROW_BLOCK, TOK_SUB, LANES), BF16),
                        pltpu.SemaphoreType.DMA((2,)), pltpu.SemaphoreType.DMA((1,))],
    )
    return pl.pallas_call(
        _dispatch_kernel,
        grid_spec=grid_spec,
        out_shape=jax.ShapeDtypeStruct((P_ROWS, TOK_SUB, LANES), BF16),
        compiler_params=pltpu.CompilerParams(dimension_semantics=("arbitrary",), vmem_limit_bytes=VMEM_LIMIT),
        name="dispatch",
    )(d0, d1, pfill, pend, hn)


W_SLOTS = 3


def _expert_kernel(nu_ref, first_ref, ord_ref, used_ref, nue_ref, x_hbm, wg_hbm, wu_hbm, wd_hbm, y_hbm,
                   x_buf, y_buf, wg_f, wu_f, wd_f, wg_b, wu_b, wd_b, xsem, ysem, wsem):
    n = nu_ref[0]

    def rows(i):
        return pl.ds(pl.multiple_of(i * ROW_BLOCK, ROW_BLOCK), ROW_BLOCK)

    def xcopy(i, slot):
        return pltpu.make_async_copy(x_hbm.at[rows(i)], x_buf.at[slot], xsem.at[slot])

    def ycopy(i, slot):
        return pltpu.make_async_copy(y_buf.at[slot], y_hbm.at[rows(i)], ysem.at[slot])

    def fetch(k):
        e = used_ref[k]
        slot = k % W_SLOTS
        return (pltpu.make_async_copy(wg_hbm.at[e], wg_f.at[slot], wsem.at[slot, 0]),
                pltpu.make_async_copy(wu_hbm.at[e], wu_f.at[slot], wsem.at[slot, 1]),
                pltpu.make_async_copy(wd_hbm.at[e], wd_f.at[slot], wsem.at[slot, 2]))

    for k in range(W_SLOTS - 1):
        @pl.when(k < nue_ref[0])
        def _():
            for c in fetch(k):
                c.start()
    xcopy(0, 0).start()

    def block(i, carry):
        slot = i % 2
        xcopy(i, slot).wait()

        @pl.when(i + 1 < n)
        def _():
            xcopy(i + 1, 1 - slot).start()

        @pl.when(first_ref[i] == 1)
        def _():
            k = ord_ref[i]
            for c in fetch(k):
                c.wait()

            @pl.when(k + (W_SLOTS - 1) < nue_ref[0])
            def _():
                for c in fetch(k + (W_SLOTS - 1)):
                    c.start()

            wslot = k % W_SLOTS
            wg_b[...] = wg_f[wslot].astype(BF16)
            wu_b[...] = wu_f[wslot].astype(BF16)
            wd_b[...] = wd_f[wslot].astype(BF16)

        @pl.when(i >= 2)
        def _():
            ycopy(i - 2, slot).wait()

        x = x_buf[slot].reshape(ROW_BLOCK, D_MODEL)
        g = _dot(x, wg_b[...])
        u = _dot(x, wu_b[...])
        hmid = (g * _sigmoid(g) * u).astype(BF16)
        y = _dot(hmid, wd_b[...]).astype(BF16)
        y_buf[slot] = y.reshape(ROW_BLOCK, TOK_SUB, LANES)
        ycopy(i, slot).start()
        return carry

    lax.fori_loop(0, n, block, 0)

    @pl.when(n >= 2)
    def _():
        ycopy(n - 2, n % 2).wait()
    ycopy(n - 1, (n - 1) % 2).wait()

    y_buf[0] = jnp.zeros((ROW_BLOCK, TOK_SUB, LANES), BF16)

    def zcopy(b):
        return pltpu.make_async_copy(y_buf.at[0], y_hbm.at[rows(b)], ysem.at[0])

    def zstart(b, carry):
        zcopy(b).start()
        return carry

    def zwait(b, carry):
        zcopy(b).wait()
        return carry

    lax.fori_loop(n, N_BLOCKS, zstart, 0)
    lax.fori_loop(n, N_BLOCKS, zwait, 0)


def _experts(n_used, blk_first, blk_ord, used_e, n_used_e, xin, w_gate, w_up, w_down):
    grid_spec = pltpu.PrefetchScalarGridSpec(
        num_scalar_prefetch=5,
        grid=(1,),
        in_specs=[pl.BlockSpec(memory_space=pl.ANY),
                  pl.BlockSpec(memory_space=pl.ANY),
                  pl.BlockSpec(memory_space=pl.ANY),
                  pl.BlockSpec(memory_space=pl.ANY)],
        out_specs=pl.BlockSpec(memory_space=pl.ANY),
        scratch_shapes=[pltpu.VMEM((2, ROW_BLOCK, TOK_SUB, LANES), BF16),
                        pltpu.VMEM((2, ROW_BLOCK, TOK_SUB, LANES), BF16),
                        pltpu.VMEM((W_SLOTS, D_MODEL, D_EXPERT), F32),
                        pltpu.VMEM((W_SLOTS, D_MODEL, D_EXPERT), F32),
                        pltpu.VMEM((W_SLOTS, D_EXPERT, D_MODEL), F32),
                        pltpu.VMEM((D_MODEL, D_EXPERT), BF16),
                        pltpu.VMEM((D_MODEL, D_EXPERT), BF16),
                        pltpu.VMEM((D_EXPERT, D_MODEL), BF16),
                        pltpu.SemaphoreType.DMA((2,)), pltpu.SemaphoreType.DMA((2,)),
                        pltpu.SemaphoreType.DMA((W_SLOTS, 3))],
    )
    return pl.pallas_call(
        _expert_kernel,
        grid_spec=grid_spec,
        out_shape=jax.ShapeDtypeStruct((P_ROWS, TOK_SUB, LANES), BF16),
        compiler_params=pltpu.CompilerParams(dimension_semantics=("arbitrary",), vmem_limit_bytes=VMEM_LIMIT),
        name="experts",
    )(n_used, blk_first, blk_ord, used_e, n_used_e, xin, w_gate, w_up, w_down)


COMB_TM = 512


def _combine_kernel(d0_ref, d1_ref, h1_ref, route_ref, yb_ref, out_ref, g_scr, sem):
    i = pl.program_id(0)
    n = pl.num_programs(0)

    def issue(tile, slot):
        base = tile * COMB_TM

        def start(r, carry):
            pltpu.make_async_copy(yb_ref.at[d0_ref[base + r]], g_scr.at[slot, 0, r], sem.at[slot, 0]).start()
            pltpu.make_async_copy(yb_ref.at[d1_ref[base + r]], g_scr.at[slot, 1, r], sem.at[slot, 1]).start(
                priority=1)
            return carry

        lax.fori_loop(0, COMB_TM, start, 0, unroll=DMA_UNROLL)

    @pl.when(i == 0)
    def _():
        issue(0, 0)

    slot = i % 2

    @pl.when(i + 1 < n)
    def _():
        issue(i + 1, 1 - slot)

    for k in range(2):
        pltpu.make_async_copy(yb_ref.at[pl.ds(0, COMB_TM)], g_scr.at[slot, k], sem.at[slot, k]).wait()

    route = route_ref[...]
    w0 = route[:, 2:3]
    w1 = route[:, 3:4]
    g0 = g_scr[slot, 0].reshape(COMB_TM, D_MODEL).astype(F32)
    g1 = g_scr[slot, 1].reshape(COMB_TM, D_MODEL).astype(F32)
    out_ref[...] = h1_ref[...] + (g0 * w0 + g1 * w1)


def _combine(d0, d1, h1, route, yb):
    grid_spec = pltpu.PrefetchScalarGridSpec(
        num_scalar_prefetch=2,
        grid=(N_TOK // COMB_TM,),
        in_specs=[pl.BlockSpec((COMB_TM, D_MODEL), lambda i, *_: (i, 0)),
                  pl.BlockSpec((COMB_TM, LANES), lambda i, *_: (i, 0)),
                  pl.BlockSpec(memory_space=pl.ANY)],
        out_specs=pl.BlockSpec((COMB_TM, D_MODEL), lambda i, *_: (i, 0)),
        scratch_shapes=[pltpu.VMEM((2, 2, COMB_TM, TOK_SUB, LANES), BF16),
                        pltpu.SemaphoreType.DMA((2, 2))],
    )
    return pl.pallas_call(
        _combine_kernel,
        grid_spec=grid_spec,
        out_shape=jax.ShapeDtypeStruct((N_TOK, D_MODEL), F32),
        compiler_params=pltpu.CompilerParams(dimension_semantics=("arbitrary",), vmem_limit_bytes=VMEM_LIMIT),
        name="combine",
    )(d0, d1, h1, route, yb)


def kernel(x, meta, norm1_g, w_in, b_in, dw_w, dw_b, conv_ln_g, conv_ln_b, w_conv_out, b_conv_out,
           q_norm_g, k_norm_g, b_forget, w_attn_out, w_out, norm2_g, w_group, b_group, w_router,
           b_router, w_gate, w_up, w_down):
    l = 0
    row = lambda v: v.reshape(1, -1).astype(F32)

    c_u, c_q, c_f = 2 * CONV_WIDTH, 2 * CONV_WIDTH, 2 * CONV_WIDTH + 3 * ATTN_WIDTH
    c_gc = c_f + N_HEADS
    c_ga = c_gc + D_MODEL
    wi, bi = w_in[l], b_in[l]
    wu, bu = wi[:, :c_u].astype(BF16), row(bi[:c_u])
    wqkv, bqkv = wi[:, c_q:c_f].astype(BF16), row(bi[c_q:c_f])
    wft = jnp.zeros((16, D_MODEL), F32).at[:N_HEADS].set(wi[:, c_f:c_gc].T).astype(BF16)
    bft = jnp.zeros((16, 1), F32).at[:N_HEADS, 0].set(bi[c_f:c_gc] + b_forget[l])
    wgc, bgc = wi[:, c_gc:c_ga].astype(BF16), row(bi[c_gc:c_ga])
    wga, bga = wi[:, c_ga:].astype(BF16), row(bi[c_ga:])
    qg = row(jnp.tile(q_norm_g[l], N_HEADS) * (HEAD_DIM ** -0.5 * LOG2E))
    kg = row(jnp.tile(k_norm_g[l], N_HEADS))
    head_of_col = jnp.arange(ATTN_WIDTH) // HEAD_DIM
    hsum = (head_of_col[:, None] == jnp.arange(LANES)[None, :]).astype(BF16)
    hexp_half = (jnp.arange(LANES)[:, None] == head_of_col[None, :]).astype(BF16)
    hexp = jnp.concatenate([hexp_half, hexp_half], axis=0)
    proj_consts = (row(norm1_g[l]), wu, bu, wqkv, bqkv, wft, bft, wgc, bgc, wga, bga, qg, kg, hsum, hexp)

    x2d = x.reshape(N_TOK, D_MODEL)
    meta_pad = jnp.zeros((META_ROWS, D_MODEL), F32).at[:N_META].set(meta.astype(F32))
    a_m, _, k_m, v_m, lf_m, _, _ = _proj(meta_pad, proj_consts, META_ROWS)
    conv_consts = (a_m, _conv_shift_matrix(), dw_w[l].astype(F32), row(dw_b[l]),
                   row(conv_ln_g[l]), row(conv_ln_b[l]))
    act, q, k, v, lf, sgc, sga = _proj(x2d, proj_consts, PROJ_TM, conv_consts)

    o = _attention_t(q, k, v, lf, k_m, v_m, lf_m)

    wrt = jnp.zeros((LANES, D_MODEL), F32).at[:N_GROUPS].set(w_group[l].T)
    wrt = wrt.at[N_GROUPS:N_GROUPS + N_EXPERTS].set(w_router[l].T).astype(BF16)
    brt = jnp.zeros((LANES, 1), F32).at[:N_GROUPS, 0].set(b_group[l])
    brt = brt.at[N_GROUPS:N_GROUPS + N_EXPERTS, 0].set(b_router[l])
    tri = jnp.triu(jnp.ones((MERGE_TM, MERGE_TM), F32), 1).astype(BF16)
    merge_consts = (w_conv_out[l].astype(BF16), row(b_conv_out[l]), w_attn_out[l].astype(BF16),
                    w_out[l].astype(BF16), row(norm2_g[l]), wrt, brt, tri)
    h1, hn, route, route_t, cnt = _merge(act, o, sgc, sga, x2d, merge_consts)

    counts = cnt[:N_EXPERTS, 0].astype(jnp.int32)
    padded = (counts + ROW_BLOCK - 1) // ROW_BLOCK * ROW_BLOCK
    pend = jnp.cumsum(padded).astype(jnp.int32)
    pstart = pend - padded
    blk_row = jnp.arange(N_BLOCKS, dtype=jnp.int32) * ROW_BLOCK
    n_used = pend[-1:] // ROW_BLOCK
    nz = counts > 0
    before = (pend[None, :] <= blk_row[:, None]) & nz[None, :]
    blk_ord = jnp.sum(before.astype(jnp.int32), axis=1)
    blk_first = jnp.any((pstart[None, :] == blk_row[:, None]) & nz[None, :], axis=1).astype(jnp.int32)
    e_ord = jnp.cumsum(nz.astype(jnp.int32)) - 1
    e_ids = jnp.arange(N_EXPERTS, dtype=jnp.int32)
    used_e = jnp.sum(jnp.where((e_ord[None, :] == e_ids[:, None]) & nz[None, :], e_ids[None, :], 0), axis=1)
    n_used_e = jnp.sum(nz.astype(jnp.int32)).reshape(1)

    dest = _dest(pstart, route_t)
    d0, d1 = dest[0], dest[1]
    xin = _dispatch(d0, d1, pstart + counts, pend, hn)
    yb = _experts(n_used, blk_first, blk_ord, used_e.astype(jnp.int32), n_used_e, xin,
                  w_gate[l], w_up[l], w_down[l])
    out = _combine(d0, d1, h1, route, yb)
    return out.reshape(BATCH, SEQ, D_MODEL)
```

```python
import jax
import jax.numpy as jnp
from jax import lax
from jax.experimental import pallas as pl
from jax.experimental.pallas import tpu as pltpu

D_MODEL = 1024
BATCH = 8
SEQ = 2048
N_META = 16
CONV_WIDTH = 512
CONV_K = 31
N_HEADS = 8
HEAD_DIM = 64
ATTN_WIDTH = N_HEADS * HEAD_DIM
N_GROUPS = 8
EXPERTS_PER_GROUP = 8
N_EXPERTS = N_GROUPS * EXPERTS_PER_GROUP
D_EXPERT = 256
RMS_EPS = 1e-6
LN_EPS = 1e-5

N_TOK = BATCH * SEQ
LANES = 128
META_ROWS = 128
ROW_BLOCK = 512
N_ASSIGN = 2 * N_TOK
N_BLOCKS = N_ASSIGN // ROW_BLOCK + N_EXPERTS
P_ROWS = N_BLOCKS * ROW_BLOCK
TOK_SUB = D_MODEL // LANES
PROJ_TM = 1024
CONV_CHUNK = 64
CONV_PAD = 32
CONV_WIN = CONV_CHUNK + 32
CONV_SHROWS = CONV_CHUNK + 24
CONV_COPIES = 7
CONV_LANES = 128
ATT_TQ = SEQ
ATT_TK = 256
N_PAIRS = N_HEADS // 2
VMEM_LIMIT = 56 * 1024 * 1024

F32 = jnp.float32
BF16 = jnp.bfloat16
NEG_INF = float("-inf")
LOG2E = 1.4426950408889634


def _dot(a, b):
    return jnp.dot(a, b, preferred_element_type=F32)


def _dot_nt(a, b):
    return lax.dot_general(a, b, (((1,), (1,)), ((), ())), preferred_element_type=F32)


def _sigmoid(x):
    return 1.0 / (1.0 + jnp.exp(-x))


def _const_spec(shape):
    nd = len(shape)
    return pl.BlockSpec(shape, lambda *_: (0,) * nd)


def _head_rms(t, hsum_ref, hexp_ref, gain):
    ss = _dot((t * t).astype(BF16), hsum_ref[...])
    inv = lax.rsqrt(ss * (1.0 / HEAD_DIM) + RMS_EPS)
    hi = inv.astype(BF16)
    lo = (inv - hi.astype(F32)).astype(BF16)
    invb = _dot(jnp.concatenate([hi, lo], axis=1), hexp_ref[...])
    return t * invb * gain


def _proj_stages(x_ref, g1_ref, wu_ref, bu_ref, wqkv_ref, bqkv_ref, wft_ref, bft_ref, wgc_ref, bgc_ref,
                 wga_ref, bga_ref, qg_ref, kg_ref, hsum_ref, hexp_ref,
                 q_ref, k_ref, v_ref, lf_ref, sgc_ref, sga_ref):
    x = x_ref[...]
    ms = jnp.mean(x * x, axis=-1, keepdims=True)
    xn = (x * lax.rsqrt(ms + RMS_EPS) * g1_ref[...]).astype(BF16)

    def glu():
        u = _dot(xn, wu_ref[...]) + bu_ref[...]
        return (u[:, :CONV_WIDTH] * _sigmoid(u[:, CONV_WIDTH:])).astype(BF16)

    def qkv_stage():
        qkv = _dot(xn, wqkv_ref[...]) + bqkv_ref[...]
        q_ref[...] = _head_rms(qkv[:, :ATTN_WIDTH], hsum_ref, hexp_ref, qg_ref[...]).astype(BF16)
        k_ref[...] = _head_rms(qkv[:, ATTN_WIDTH:2 * ATTN_WIDTH], hsum_ref, hexp_ref, kg_ref[...]).astype(BF16)
        v = qkv[:, 2 * ATTN_WIDTH:]
        vb = v_ref.shape[2]
        for c in range(v_ref.shape[0]):
            v_ref[c] = v[c * vb:(c + 1) * vb, :].T.astype(BF16)

    def gate_stage_c():
        f = _dot_nt(wft_ref[...], xn) + bft_ref[...]
        lf_ref[...] = jnp.minimum(f, 0.0) - jnp.log(1.0 + jnp.exp(-jnp.abs(f)))
        sgc_ref[...] = _sigmoid(_dot(xn, wgc_ref[...]) + bgc_ref[...]).astype(BF16)

    def gate_stage_a():
        sga_ref[...] = _sigmoid(_dot(xn, wga_ref[...]) + bga_ref[...]).astype(BF16)

    return glu, (qkv_stage, gate_stage_c, gate_stage_a)


def _proj_kernel(*refs):
    ins, a_ref, outs = refs[:16], refs[16], refs[17:]
    glu, stages = _proj_stages(*ins, *outs)
    a_ref[...] = glu()
    for stage in stages:
        stage()


def _proj_conv_kernel(*refs):
    ins = refs[:16]
    am_ref, shift_ref, dw_ref, dwb_ref, lng_ref, lnb_ref = refs[16:22]
    act_ref, outs = refs[22], refs[23:29]
    win_scr, sh_scr = refs[29:]
    tm = act_ref.shape[0]
    i = pl.program_id(0)
    first = (i % (SEQ // tm)) == 0

    @pl.when(first)
    def _():
        win_scr[0:N_META, :] = jnp.zeros((N_META, CONV_WIDTH), BF16)
        win_scr[N_META:CONV_PAD, :] = am_ref[0:N_META, :]

    @pl.when(jnp.logical_not(first))
    def _():
        win_scr[0:CONV_PAD, :] = win_scr[tm:tm + CONV_PAD, :]

    glu, stages = _proj_stages(*ins, *outs)
    win_scr[CONV_PAD:CONV_PAD + tm, :] = glu()

    def conv_chunk(ci):
        r0 = ci * CONV_CHUNK
        slot = ci % 2
        sh_scr[slot] = _dot(shift_ref[...], win_scr[r0:r0 + CONV_WIN, :])
        parts = []
        for c0 in range(0, CONV_WIDTH, CONV_LANES):
            cs = slice(c0, c0 + CONV_LANES)
            part = jnp.zeros((CONV_CHUNK, CONV_LANES), F32) + dwb_ref[:, cs]
            for j in range(CONV_K):
                off = j + 2
                b = off % 8
                if b == 0:
                    tap = win_scr[r0 + off:r0 + off + CONV_CHUNK, cs].astype(F32)
                else:
                    a0 = (b - 1) * CONV_SHROWS + (off - b)
                    tap = sh_scr[slot, a0:a0 + CONV_CHUNK, cs]
                part = part + tap * dw_ref[j:j + 1, cs]
            parts.append(part)
        acc = jnp.concatenate(parts, axis=1)
        mu = jnp.mean(acc, axis=-1, keepdims=True)
        d = acc - mu
        var = jnp.mean(d * d, axis=-1, keepdims=True)
        y = d * lax.rsqrt(var + LN_EPS) * lng_ref[...] + lnb_ref[...]
        act_ref[r0:r0 + CONV_CHUNK, :] = (y * _sigmoid(y)).astype(BF16)

    n_chunks = tm // CONV_CHUNK
    per_stage = -(-n_chunks // (len(stages) + 1))
    ci = 0
    for stage in (None,) + stages:
        if stage is not None:
            stage()
        for _ in range(per_stage):
            if ci < n_chunks:
                conv_chunk(ci)
                ci += 1


def _proj(x2d, consts, tm, conv_consts=None):
    n = x2d.shape[0]
    row = lambda w: pl.BlockSpec((tm, w), lambda i: (i, 0))
    in_specs = [row(D_MODEL)] + [_const_spec(c.shape) for c in consts]
    fused = conv_consts is not None
    extra = tuple(conv_consts) if fused else ()
    in_specs += [_const_spec(c.shape) for c in extra]
    scratch = [pltpu.VMEM((CONV_PAD + tm, CONV_WIDTH), BF16),
               pltpu.VMEM((2, CONV_COPIES * CONV_SHROWS, CONV_WIDTH), F32)] if fused else []
    vb = min(tm, ATT_TK)
    out_shape = [
        jax.ShapeDtypeStruct((n, CONV_WIDTH), BF16),
        jax.ShapeDtypeStruct((n, ATTN_WIDTH), BF16),
        jax.ShapeDtypeStruct((n, ATTN_WIDTH), BF16),
        jax.ShapeDtypeStruct((n // vb, ATTN_WIDTH, vb), BF16),
        jax.ShapeDtypeStruct((16, n), F32),
        jax.ShapeDtypeStruct((n, D_MODEL), BF16),
        jax.ShapeDtypeStruct((n, D_MODEL), BF16),
    ]
    out_specs = [row(CONV_WIDTH), row(ATTN_WIDTH), row(ATTN_WIDTH),
                 pl.BlockSpec((tm // vb, ATTN_WIDTH, vb), lambda i: (i, 0, 0)),
                 pl.BlockSpec((16, tm), lambda i: (0, i)), row(D_MODEL), row(D_MODEL)]
    return pl.pallas_call(
        _proj_conv_kernel if fused else _proj_kernel,
        grid=(n // tm,),
        in_specs=in_specs,
        out_specs=out_specs,
        out_shape=out_shape,
        scratch_shapes=scratch,
        compiler_params=pltpu.CompilerParams(dimension_semantics=("arbitrary",), vmem_limit_bytes=VMEM_LIMIT),
        name="proj_conv" if fused else "proj",
    )(x2d, *consts, *extra)


def _conv_shift_matrix():
    rr = jnp.arange(CONV_COPIES * CONV_SHROWS)
    src = rr % CONV_SHROWS + rr // CONV_SHROWS + 1
    return (src[:, None] == jnp.arange(CONV_WIN)[None, :]).astype(BF16)


AUG_TERMS = 3
ACC_ONES = 16
SCORE_LEAD = 8


def _split_terms(x):
    hi = x.astype(BF16)
    r1 = x - hi.astype(F32)
    mid = r1.astype(BF16)
    lo = (r1 - mid.astype(F32)).astype(BF16)
    return hi, mid, lo


def _attn_t_kernel(q_ref, k_ref, vt_ref, lf_ref, km_ref, vm_ref, lfm_ref, place_ref, o_ref,
                   aug_scr, augm_scr, m_scr, acc_scr):
    p = pl.program_id(1)
    nk = SEQ // ATT_TK

    def bias_lanes(colsum):
        out = None
        for t, term in enumerate(_split_terms(colsum)):
            d = _dot(term, place_ref[t])
            out = d if out is None else out + d
        return out.astype(BF16)

    @pl.when(p == 0)
    def _():
        r = lax.broadcasted_iota(jnp.int32, (ATT_TK, ATT_TK), 0)
        c = lax.broadcasted_iota(jnp.int32, (ATT_TK, ATT_TK), 1)
        tri = (c <= r).astype(BF16)
        local = []
        for j in range(nk):
            lf = lf_ref[:, j * ATT_TK:(j + 1) * ATT_TK]
            csum = None
            for term in _split_terms(lf):
                d = _dot_nt(tri, term)
                csum = d if csum is None else csum + d
            local.append(csum)
        carry = jnp.zeros((1, 16), F32)
        for j in range(nk):
            csum = local[j] + carry
            aug_scr[j * ATT_TK:(j + 1) * ATT_TK, :] = bias_lanes(csum * (-LOG2E))
            carry = csum[ATT_TK - 1:ATT_TK, :]
        rm = lax.broadcasted_iota(jnp.int32, (META_ROWS, META_ROWS), 0)
        cm = lax.broadcasted_iota(jnp.int32, (META_ROWS, META_ROWS), 1)
        trim = ((cm > rm) & (cm < N_META)).astype(BF16)
        msum = jnp.zeros((META_ROWS, 16), F32)
        for term in _split_terms(lfm_ref[...]):
            msum = msum + _dot_nt(trim, term)
        augm_scr[...] = bias_lanes(msum * LOG2E)

    lane = lax.broadcasted_iota(jnp.int32, (ATT_TQ, LANES), 1)
    q_all = q_ref[...]
    zq = jnp.zeros_like(q_all)
    q_cat = []
    for hl in range(2):
        own = (lane < HEAD_DIM) if hl == 0 else (lane >= HEAD_DIM)
        h = 2 * p + hl
        bias_sel = (lane >= AUG_TERMS * h) & (lane < AUG_TERMS * h + AUG_TERMS)
        bias_ones = jnp.where(bias_sel, 1.0, 0.0).astype(BF16)
        q_cat.append(jnp.concatenate([jnp.where(own, q_all, zq), bias_ones], axis=1))
        m_scr[hl] = jnp.full((1, ATT_TQ), NEG_INF, F32)
        acc_scr[hl] = jnp.zeros((HEAD_DIM + ACC_ONES, ATT_TQ), F32)

    def scores(kb, augb, lo=0):
        k_cat = jnp.concatenate([kb, augb], axis=1)
        return tuple(_dot_nt(k_cat, q_cat[hl][lo:]) for hl in range(2))

    def update(s2, vtb, lo=0):
        ones = jnp.ones((ACC_ONES, vtb.shape[1]), BF16)
        for hl in range(2):
            s = s2[hl]
            m_prev = m_scr[hl, :, lo:]
            m_next = jnp.maximum(m_prev, jnp.max(s, axis=0, keepdims=True))
            pr = jnp.exp2(s - m_next).astype(BF16)
            alpha = jnp.exp2(m_prev - m_next)
            v_own = jnp.concatenate([vtb[hl * HEAD_DIM:(hl + 1) * HEAD_DIM], ones], axis=0)
            acc_scr[hl, :, lo:] = acc_scr[hl, :, lo:] * alpha + _dot(v_own, pr)
            m_scr[hl, :, lo:] = m_next

    def block_scores(j):
        rs = slice(j * ATT_TK, (j + 1) * ATT_TK)
        return scores(k_ref[rs, :], aug_scr[rs, :], lo=j * ATT_TK)

    s_meta = scores(km_ref[0:N_META, :], augm_scr[0:N_META, :])
    s_blocks = [block_scores(j) for j in range(min(SCORE_LEAD, nk))]
    update(s_meta, vm_ref[0][:, 0:N_META])

    srow = lax.broadcasted_iota(jnp.int32, (ATT_TK, ATT_TK), 0)
    tcol = lax.broadcasted_iota(jnp.int32, (ATT_TK, ATT_TK), 1)
    tri = srow <= tcol
    for j in range(nk):
        if j + SCORE_LEAD < nk:
            s_blocks.append(block_scores(j + SCORE_LEAD))
        s_cur = tuple(jnp.concatenate([jnp.where(tri, s[:, :ATT_TK], NEG_INF), s[:, ATT_TK:]], axis=1)
                      if s.shape[1] > ATT_TK else jnp.where(tri, s, NEG_INF) for s in s_blocks[j])
        update(s_cur, vt_ref[j], lo=j * ATT_TK)

    a0 = acc_scr[0]
    a1 = acc_scr[1]
    ot = jnp.concatenate([a0[0:HEAD_DIM] / a0[HEAD_DIM:HEAD_DIM + 1], a1[0:HEAD_DIM] / a1[HEAD_DIM:HEAD_DIM + 1]],
                         axis=0)
    o_ref[...] = ot.T.astype(BF16)


def _attention_t(q, k, vt, lf, km, vm, lfm):
    nk = SEQ // ATT_TK
    hh = jnp.arange(16)[:, None]
    ll = jnp.arange(LANES)[None, :]
    place = jnp.stack([((ll == AUG_TERMS * hh + t) & (hh < N_HEADS)).astype(BF16) for t in range(AUG_TERMS)])
    return pl.pallas_call(
        _attn_t_kernel,
        grid=(BATCH, N_PAIRS),
        in_specs=[
            pl.BlockSpec((ATT_TQ, LANES), lambda b, p: (b, p)),
            pl.BlockSpec((SEQ, LANES), lambda b, p: (b, p)),
            pl.BlockSpec((nk, LANES, ATT_TK), lambda b, p: (b, p, 0)),
            pl.BlockSpec((16, SEQ), lambda b, p: (0, b)),
            pl.BlockSpec((META_ROWS, LANES), lambda b, p: (0, p)),
            pl.BlockSpec((1, LANES, META_ROWS), lambda b, p: (0, p, 0)),
            pl.BlockSpec((16, META_ROWS), lambda b, p: (0, 0)),
            _const_spec(place.shape),
        ],
        out_specs=pl.BlockSpec((ATT_TQ, LANES), lambda b, p: (b, p)),
        out_shape=jax.ShapeDtypeStruct((N_TOK, ATTN_WIDTH), BF16),
        scratch_shapes=[
            pltpu.VMEM((SEQ, LANES), BF16),
            pltpu.VMEM((META_ROWS, LANES), BF16),
            pltpu.VMEM((2, 1, ATT_TQ), F32),
            pltpu.VMEM((2, HEAD_DIM + ACC_ONES, ATT_TQ), F32),
        ],
        compiler_params=pltpu.CompilerParams(
            dimension_semantics=("arbitrary", "arbitrary"), vmem_limit_bytes=VMEM_LIMIT),
        name="attn",
    )(q, k, vt, lf, km, vm, lfm, place)


MERGE_TM = 1024


def _merge_kernel(act_ref, o_ref, sgc_ref, sga_ref, x_ref, wpw_ref, bpw_ref, wao_ref, wout_ref, g2_ref,
                  wrt_ref, brt_ref, tri_ref,
                  h1_ref, hn_ref, route_ref, routet_ref, cnt_ref, base_scr):
    i = pl.program_id(0)

    @pl.when(i == 0)
    def _():
        base_scr[...] = jnp.zeros_like(base_scr)

    tm = x_ref.shape[0]
    hn_parts = []
    for rs in (slice(0, tm // 2), slice(tm // 2, tm)):
        y_conv = _dot(act_ref[rs, :], wpw_ref[...]) + bpw_ref[...]
        y_attn = _dot(o_ref[rs, :], wao_ref[...])
        merged = sgc_ref[rs, :].astype(F32) * y_conv + sga_ref[rs, :].astype(F32) * y_attn
        h1 = x_ref[rs, :] + _dot(merged.astype(BF16), wout_ref[...])
        h1_ref[rs, :] = h1
        ms = jnp.mean(h1 * h1, axis=-1, keepdims=True)
        hn_half = (h1 * lax.rsqrt(ms + RMS_EPS) * g2_ref[...]).astype(BF16)
        hn_ref[rs] = hn_half.reshape(tm // 2, TOK_SUB, LANES)
        hn_parts.append(hn_half)
    hn = jnp.concatenate(hn_parts, axis=0)

    logits = _dot_nt(wrt_ref[...], hn) + brt_ref[...]
    row = lax.broadcasted_iota(jnp.int32, logits.shape, 0).astype(F32)
    big = float(LANES)
    is_g = row < N_GROUPS
    gl = jnp.where(is_g, logits, NEG_INF)
    gmax = jnp.max(gl, axis=0, keepdims=True)
    gidx = jnp.min(jnp.where(gl == gmax, row, big), axis=0, keepdims=True)
    gsum = jnp.sum(jnp.where(is_g, jnp.exp(gl - gmax), 0.0), axis=0, keepdims=True)
    g_w = 1.0 / gsum
    lo = N_GROUPS + EXPERTS_PER_GROUP * gidx
    el = jnp.where((row >= lo) & (row < lo + EXPERTS_PER_GROUP), logits, NEG_INF)
    t1 = jnp.max(el, axis=0, keepdims=True)
    i1 = jnp.min(jnp.where(el == t1, row, big), axis=0, keepdims=True)
    el2 = jnp.where(row == i1, NEG_INF, el)
    t2 = jnp.max(el2, axis=0, keepdims=True)
    i2 = jnp.min(jnp.where(el2 == t2, row, big), axis=0, keepdims=True)
    e2 = jnp.exp(t2 - t1)
    den = 1.0 + e2
    w1 = (1.0 / den) * g_w
    w2 = (e2 / den) * g_w
    ea = i1 - N_GROUPS
    eb = i2 - N_GROUPS

    oha = (row == ea).astype(F32)
    ohb = (row == eb).astype(F32)
    ohs = oha + ohb
    before = _dot(ohs.astype(BF16), tri_ref[...]) + base_scr[...][:, 0:1]
    ra = jnp.sum(before * oha, axis=0, keepdims=True)
    rb = jnp.sum(before * ohb, axis=0, keepdims=True)
    base_scr[...] = base_scr[...] + jnp.sum(ohs, axis=1, keepdims=True)
    cnt_ref[...] = base_scr[...]

    def rows_of(shape):
        r = lax.broadcasted_iota(jnp.int32, shape, 0)
        out = jnp.where(r == 0, ea, 0.0)
        for k, val in enumerate((eb, w1, w2, ra, rb), start=1):
            out = jnp.where(r == k, val, out)
        return out

    routet_ref[...] = rows_of((8, logits.shape[1]))
    route_ref[...] = rows_of(logits.shape).T


def _merge(act, o, sgc, sga, x2d, consts):
    tm = MERGE_TM
    row = lambda w: pl.BlockSpec((tm, w), lambda i: (i, 0))
    return pl.pallas_call(
        _merge_kernel,
        grid=(N_TOK // tm,),
        in_specs=[row(CONV_WIDTH), row(ATTN_WIDTH), row(D_MODEL), row(D_MODEL), row(D_MODEL)]
        + [_const_spec(c.shape) for c in consts],
        out_specs=[row(D_MODEL), pl.BlockSpec((tm, TOK_SUB, LANES), lambda i: (i, 0, 0)), row(LANES),
                   pl.BlockSpec((8, tm), lambda i: (0, i)), _const_spec((LANES, LANES))],
        out_shape=[jax.ShapeDtypeStruct((N_TOK, D_MODEL), F32),
                   jax.ShapeDtypeStruct((N_TOK, TOK_SUB, LANES), BF16),
                   jax.ShapeDtypeStruct((N_TOK, LANES), F32),
                   jax.ShapeDtypeStruct((8, N_TOK), F32),
                   jax.ShapeDtypeStruct((LANES, LANES), F32)],
        scratch_shapes=[pltpu.VMEM((LANES, LANES), F32)],
        compiler_params=pltpu.CompilerParams(dimension_semantics=("arbitrary",), vmem_limit_bytes=VMEM_LIMIT),
        name="merge",
    )(act, o, sgc, sga, x2d, *consts)


DISP_TM = 1024
DMA_UNROLL = 8


def _dest_kernel(pstart_ref, rt_ref, d_ref):
    rt = rt_ref[...]
    start = jnp.zeros_like(rt)
    for e in range(N_EXPERTS):
        start = jnp.where(rt == float(e), pstart_ref[e].astype(F32), start)
    d_ref[...] = (start + pltpu.roll(rt, 4, axis=0)).astype(jnp.int32)


def _dest(pstart, route_t):
    grid_spec = pltpu.PrefetchScalarGridSpec(
        num_scalar_prefetch=1,
        grid=(1,),
        in_specs=[pl.BlockSpec(route_t.shape, lambda i, *_: (0, 0))],
        out_specs=pl.BlockSpec(route_t.shape, lambda i, *_: (0, 0)),
    )
    return pl.pallas_call(
        _dest_kernel,
        grid_spec=grid_spec,
        out_shape=jax.ShapeDtypeStruct(route_t.shape, jnp.int32),
        compiler_params=pltpu.CompilerParams(dimension_semantics=("arbitrary",), vmem_limit_bytes=VMEM_LIMIT),
        name="dest",
    )(pstart, route_t)


ZERO_CHUNK = 64


def _dispatch_kernel(d0_ref, d1_ref, pfill_ref, pend_ref, hn_ref, xin_ref, zero_scr, sem, zsem):
    i = pl.program_id(0)
    base = i * DISP_TM

    @pl.when(i == 0)
    def _():
        zero_scr[...] = jnp.zeros_like(zero_scr)

        def pad_chunks(e, fn):
            first = pfill_ref[e] // ZERO_CHUNK * ZERO_CHUNK

            def one(c, carry):
                start = pl.multiple_of(first + c * ZERO_CHUNK, ZERO_CHUNK)
                fn(pltpu.make_async_copy(zero_scr.at[pl.ds(0, ZERO_CHUNK)], xin_ref.at[pl.ds(start, ZERO_CHUNK)],
                                         zsem.at[0]))
                return carry

            lax.fori_loop(0, (pend_ref[e] - first) // ZERO_CHUNK, one, 0)

        def zstart(e, carry):
            pad_chunks(e, lambda c: c.start())
            return carry

        def zwait(e, carry):
            pad_chunks(e, lambda c: c.wait())
            return carry

        n_used = pend_ref[N_EXPERTS - 1] // ROW_BLOCK

        def tcopy(b):
            return pltpu.make_async_copy(
                zero_scr, xin_ref.at[pl.ds(pl.multiple_of(b * ROW_BLOCK, ROW_BLOCK), ROW_BLOCK)], zsem.at[0])

        def tstart(b, carry):
            tcopy(b).start()
            return carry

        def twait(b, carry):
            tcopy(b).wait()
            return carry

        lax.fori_loop(0, N_EXPERTS, zstart, 0)
        lax.fori_loop(n_used, N_BLOCKS, tstart, 0)
        lax.fori_loop(0, N_EXPERTS, zwait, 0)
        lax.fori_loop(n_used, N_BLOCKS, twait, 0)

    def start(r, carry):
        pltpu.make_async_copy(hn_ref.at[r], xin_ref.at[d0_ref[base + r]], sem.at[0]).start()
        pltpu.make_async_copy(hn_ref.at[r], xin_ref.at[d1_ref[base + r]], sem.at[1]).start(priority=1)
        return carry

    lax.fori_loop(0, DISP_TM, start, 0, unroll=DMA_UNROLL)
    pltpu.make_async_copy(hn_ref, xin_ref.at[pl.ds(0, DISP_TM)], sem.at[0]).wait()
    pltpu.make_async_copy(hn_ref, xin_ref.at[pl.ds(0, DISP_TM)], sem.at[1]).wait()


def _dispatch(d0, d1, pfill, pend, hn):
    grid_spec = pltpu.PrefetchScalarGridSpec(
        num_scalar_prefetch=4,
        grid=(N_TOK // DISP_TM,),
        in_specs=[pl.BlockSpec((DISP_TM, TOK_SUB, LANES), lambda i, *_: (i, 0, 0))],
        out_specs=pl.BlockSpec(memory_space=pl.ANY),
        scratch_shapes=[pltpu.VMEM((ROW_BLOCK, TOK_SUB, LANES), BF16),
                        pltpu.SemaphoreType.DMA((2,)), pltpu.SemaphoreType.DMA((1,))],
    )
    return pl.pallas_call(
        _dispatch_kernel,
        grid_spec=grid_spec,
        out_shape=jax.ShapeDtypeStruct((P_ROWS, TOK_SUB, LANES), BF16),
        compiler_params=pltpu.CompilerParams(dimension_semantics=("arbitrary",), vmem_limit_bytes=VMEM_LIMIT),
        name="dispatch",
    )(d0, d1, pfill, pend, hn)


W_SLOTS = 3


def _expert_kernel(nu_ref, first_ref, ord_ref, used_ref, nue_ref, x_hbm, wg_hbm, wu_hbm, wd_hbm, y_hbm,
                   x_buf, y_buf, wg_f, wu_f, wd_f, wg_b, wu_b, wd_b, xsem, ysem, wsem):
    n = nu_ref[0]

    def rows(i):
        return pl.ds(pl.multiple_of(i * ROW_BLOCK, ROW_BLOCK), ROW_BLOCK)

    def xcopy(i, slot):
        return pltpu.make_async_copy(x_hbm.at[rows(i)], x_buf.at[slot], xsem.at[slot])

    def ycopy(i, slot):
        return pltpu.make_async_copy(y_buf.at[slot], y_hbm.at[rows(i)], ysem.at[slot])

    def fetch(k):
        e = used_ref[k]
        slot = k % W_SLOTS
        return (pltpu.make_async_copy(wg_hbm.at[e], wg_f.at[slot], wsem.at[slot, 0]),
                pltpu.make_async_copy(wu_hbm.at[e], wu_f.at[slot], wsem.at[slot, 1]),
                pltpu.make_async_copy(wd_hbm.at[e], wd_f.at[slot], wsem.at[slot, 2]))

    for k in range(W_SLOTS - 1):
        @pl.when(k < nue_ref[0])
        def _():
            for c in fetch(k):
                c.start()
    xcopy(0, 0).start()

    def block(i, carry):
        slot = i % 2
        xcopy(i, slot).wait()

        @pl.when(i + 1 < n)
        def _():
            xcopy(i + 1, 1 - slot).start()

        @pl.when(first_ref[i] == 1)
        def _():
            k = ord_ref[i]
            for c in fetch(k):
                c.wait()

            @pl.when(k + (W_SLOTS - 1) < nue_ref[0])
            def _():
                for c in fetch(k + (W_SLOTS - 1)):
                    c.start()

            wslot = k % W_SLOTS
            wg_b[...] = wg_f[wslot].astype(BF16)
            wu_b[...] = wu_f[wslot].astype(BF16)
            wd_b[...] = wd_f[wslot].astype(BF16)

        @pl.when(i >= 2)
        def _():
            ycopy(i - 2, slot).wait()

        x = x_buf[slot].reshape(ROW_BLOCK, D_MODEL)
        g = _dot(x, wg_b[...])
        u = _dot(x, wu_b[...])
        hmid = (g * _sigmoid(g) * u).astype(BF16)
        y = _dot(hmid, wd_b[...]).astype(BF16)
        y_buf[slot] = y.reshape(ROW_BLOCK, TOK_SUB, LANES)
        ycopy(i, slot).start()
        return carry

    lax.fori_loop(0, n, block, 0)

    @pl.when(n >= 2)
    def _():
        ycopy(n - 2, n % 2).wait()
    ycopy(n - 1, (n - 1) % 2).wait()

    y_buf[0] = jnp.zeros((ROW_BLOCK, TOK_SUB, LANES), BF16)

    def zcopy(b):
        return pltpu.make_async_copy(y_buf.at[0], y_hbm.at[rows(b)], ysem.at[0])

    def zstart(b, carry):
        zcopy(b).start()
        return carry

    def zwait(b, carry):
        zcopy(b).wait()
        return carry

    lax.fori_loop(n, N_BLOCKS, zstart, 0)
    lax.fori_loop(n, N_BLOCKS, zwait, 0)


def _experts(n_used, blk_first, blk_ord, used_e, n_used_e, xin, w_gate, w_up, w_down):
    grid_spec = pltpu.PrefetchScalarGridSpec(
        num_scalar_prefetch=5,
        grid=(1,),
        in_specs=[pl.BlockSpec(memory_space=pl.ANY),
                  pl.BlockSpec(memory_space=pl.ANY),
                  pl.BlockSpec(memory_space=pl.ANY),
                  pl.BlockSpec(memory_space=pl.ANY)],
        out_specs=pl.BlockSpec(memory_space=pl.ANY),
        scratch_shapes=[pltpu.VMEM((2, ROW_BLOCK, TOK_SUB, LANES), BF16),
                        pltpu.VMEM((2, ROW_BLOCK, TOK_SUB, LANES), BF16),
                        pltpu.VMEM((W_SLOTS, D_MODEL, D_EXPERT), F32),
                        pltpu.VMEM((W_SLOTS, D_MODEL, D_EXPERT), F32),
                        pltpu.VMEM((W_SLOTS, D_EXPERT, D_MODEL), F32),
                        pltpu.VMEM((D_MODEL, D_EXPERT), BF16),
                        pltpu.VMEM((D_MODEL, D_EXPERT), BF16),
                        pltpu.VMEM((D_EXPERT, D_MODEL), BF16),
                        pltpu.SemaphoreType.DMA((2,)), pltpu.SemaphoreType.DMA((2,)),
                        pltpu.SemaphoreType.DMA((W_SLOTS, 3))],
    )
    return pl.pallas_call(
        _expert_kernel,
        grid_spec=grid_spec,
        out_shape=jax.ShapeDtypeStruct((P_ROWS, TOK_SUB, LANES), BF16),
        compiler_params=pltpu.CompilerParams(dimension_semantics=("arbitrary",), vmem_limit_bytes=VMEM_LIMIT),
        name="experts",
    )(n_used, blk_first, blk_ord, used_e, n_used_e, xin, w_gate, w_up, w_down)


COMB_TM = 512


def _combine_kernel(d0_ref, d1_ref, h1_ref, route_ref, yb_ref, out_ref, g_scr, sem):
    i = pl.program_id(0)
    n = pl.num_programs(0)

    def issue(tile, slot):
        base = tile * COMB_TM

        def start(r, carry):
            pltpu.make_async_copy(yb_ref.at[d0_ref[base + r]], g_scr.at[slot, 0, r], sem.at[slot, 0]).start()
            pltpu.make_async_copy(yb_ref.at[d1_ref[base + r]], g_scr.at[slot, 1, r], sem.at[slot, 1]).start(
                priority=1)
            return carry

        lax.fori_loop(0, COMB_TM, start, 0, unroll=DMA_UNROLL)

    @pl.when(i == 0)
    def _():
        issue(0, 0)

    slot = i % 2

    @pl.when(i + 1 < n)
    def _():
        issue(i + 1, 1 - slot)

    for k in range(2):
        pltpu.make_async_copy(yb_ref.at[pl.ds(0, COMB_TM)], g_scr.at[slot, k], sem.at[slot, k]).wait()

    route = route_ref[...]
    w0 = route[:, 2:3]
    w1 = route[:, 3:4]
    g0 = g_scr[slot, 0].reshape(COMB_TM, D_MODEL).astype(F32)
    g1 = g_scr[slot, 1].reshape(COMB_TM, D_MODEL).astype(F32)
    out_ref[...] = h1_ref[...] + (g0 * w0 + g1 * w1)


def _combine(d0, d1, h1, route, yb):
    grid_spec = pltpu.PrefetchScalarGridSpec(
        num_scalar_prefetch=2,
        grid=(N_TOK // COMB_TM,),
        in_specs=[pl.BlockSpec((COMB_TM, D_MODEL), lambda i, *_: (i, 0)),
                  pl.BlockSpec((COMB_TM, LANES), lambda i, *_: (i, 0)),
                  pl.BlockSpec(memory_space=pl.ANY)],
        out_specs=pl.BlockSpec((COMB_TM, D_MODEL), lambda i, *_: (i, 0)),
        scratch_shapes=[pltpu.VMEM((2, 2, COMB_TM, TOK_SUB, LANES), BF16),
                        pltpu.SemaphoreType.DMA((2, 2))],
    )
    return pl.pallas_call(
        _combine_kernel,
        grid_spec=grid_spec,
        out_shape=jax.ShapeDtypeStruct((N_TOK, D_MODEL), F32),
        compiler_params=pltpu.CompilerParams(dimension_semantics=("arbitrary",), vmem_limit_bytes=VMEM_LIMIT),
        name="combine",
    )(d0, d1, h1, route, yb)


def kernel(x, meta, norm1_g, w_in, b_in, dw_w, dw_b, conv_ln_g, conv_ln_b, w_conv_out, b_conv_out,
           q_norm_g, k_norm_g, b_forget, w_attn_out, w_out, norm2_g, w_group, b_group, w_router,
           b_router, w_gate, w_up, w_down):
    l = 0
    row = lambda v: v.reshape(1, -1).astype(F32)

    c_u, c_q, c_f = 2 * CONV_WIDTH, 2 * CONV_WIDTH, 2 * CONV_WIDTH + 3 * ATTN_WIDTH
    c_gc = c_f + N_HEADS
    c_ga = c_gc + D_MODEL
    wi, bi = w_in[l], b_in[l]
    wu, bu = wi[:, :c_u].astype(BF16), row(bi[:c_u])
    wqkv, bqkv = wi[:, c_q:c_f].astype(BF16), row(bi[c_q:c_f])
    wft = jnp.zeros((16, D_MODEL), F32).at[:N_HEADS].set(wi[:, c_f:c_gc].T).astype(BF16)
    bft = jnp.zeros((16, 1), F32).at[:N_HEADS, 0].set(bi[c_f:c_gc] + b_forget[l])
    wgc, bgc = wi[:, c_gc:c_ga].astype(BF16), row(bi[c_gc:c_ga])
    wga, bga = wi[:, c_ga:].astype(BF16), row(bi[c_ga:])
    qg = row(jnp.tile(q_norm_g[l], N_HEADS) * (HEAD_DIM ** -0.5 * LOG2E))
    kg = row(jnp.tile(k_norm_g[l], N_HEADS))
    head_of_col = jnp.arange(ATTN_WIDTH) // HEAD_DIM
    hsum = (head_of_col[:, None] == jnp.arange(LANES)[None, :]).astype(BF16)
    hexp_half = (jnp.arange(LANES)[:, None] == head_of_col[None, :]).astype(BF16)
    hexp = jnp.concatenate([hexp_half, hexp_half], axis=0)
    proj_consts = (row(norm1_g[l]), wu, bu, wqkv, bqkv, wft, bft, wgc, bgc, wga, bga, qg, kg, hsum, hexp)

    x2d = x.reshape(N_TOK, D_MODEL)
    meta_pad = jnp.zeros((META_ROWS, D_MODEL), F32).at[:N_META].set(meta.astype(F32))
    a_m, _, k_m, v_m, lf_m, _, _ = _proj(meta_pad, proj_consts, META_ROWS)
    conv_consts = (a_m, _conv_shift_matrix(), dw_w[l].astype(F32), row(dw_b[l]),
                   row(conv_ln_g[l]), row(conv_ln_b[l]))
    act, q, k, v, lf, sgc, sga = _proj(x2d, proj_consts, PROJ_TM, conv_consts)

    o = _attention_t(q, k, v, lf, k_m, v_m, lf_m)

    wrt = jnp.zeros((LANES, D_MODEL), F32).at[:N_GROUPS].set(w_group[l].T)
    wrt = wrt.at[N_GROUPS:N_GROUPS + N_EXPERTS].set(w_router[l].T).astype(BF16)
    brt = jnp.zeros((LANES, 1), F32).at[:N_GROUPS, 0].set(b_group[l])
    brt = brt.at[N_GROUPS:N_GROUPS + N_EXPERTS, 0].set(b_router[l])
    tri = jnp.triu(jnp.ones((MERGE_TM, MERGE_TM), F32), 1).astype(BF16)
    merge_consts = (w_conv_out[l].astype(BF16), row(b_conv_out[l]), w_attn_out[l].astype(BF16),
                    w_out[l].astype(BF16), row(norm2_g[l]), wrt, brt, tri)
    h1, hn, route, route_t, cnt = _merge(act, o, sgc, sga, x2d, merge_consts)

    counts = cnt[:N_EXPERTS, 0].astype(jnp.int32)
    padded = (counts + ROW_BLOCK - 1) // ROW_BLOCK * ROW_BLOCK
    pend = jnp.cumsum(padded).astype(jnp.int32)
    pstart = pend - padded
    blk_row = jnp.arange(N_BLOCKS, dtype=jnp.int32) * ROW_BLOCK
    n_used = pend[-1:] // ROW_BLOCK
    nz = counts > 0
    before = (pend[None, :] <= blk_row[:, None]) & nz[None, :]
    blk_ord = jnp.sum(before.astype(jnp.int32), axis=1)
    blk_first = jnp.any((pstart[None, :] == blk_row[:, None]) & nz[None, :], axis=1).astype(jnp.int32)
    e_ord = jnp.cumsum(nz.astype(jnp.int32)) - 1
    e_ids = jnp.arange(N_EXPERTS, dtype=jnp.int32)
    used_e = jnp.sum(jnp.where((e_ord[None, :] == e_ids[:, None]) & nz[None, :], e_ids[None, :], 0), axis=1)
    n_used_e = jnp.sum(nz.astype(jnp.int32)).reshape(1)

    dest = _dest(pstart, route_t)
    d0, d1 = dest[0], dest[1]
    xin = _dispatch(d0, d1, pstart + counts, pend, hn)
    yb = _experts(n_used, blk_first, blk_ord, used_e.astype(jnp.int32), n_used_e, xin,
                  w_gate[l], w_up[l], w_down[l])
    out = _combine(d0, d1, h1, route, yb)
    return out.reshape(BATCH, SEQ, D_MODEL)
```

```python
import jax
import jax.numpy as jnp
from jax import lax
from jax.experimental import pallas as pl
from jax.experimental.pallas import tpu as pltpu

D_MODEL = 1024
BATCH = 8
SEQ = 2048
N_META = 16
CONV_WIDTH = 512
CONV_K = 31
N_HEADS = 8
HEAD_DIM = 64
ATTN_WIDTH = N_HEADS * HEAD_DIM
N_GROUPS = 8
EXPERTS_PER_GROUP = 8
N_EXPERTS = N_GROUPS * EXPERTS_PER_GROUP
D_EXPERT = 256
RMS_EPS = 1e-6
LN_EPS = 1e-5

N_TOK = BATCH * SEQ
LANES = 128
META_ROWS = 128
ROW_BLOCK = 512
N_ASSIGN = 2 * N_TOK
N_BLOCKS = N_ASSIGN // ROW_BLOCK + N_EXPERTS
P_ROWS = N_BLOCKS * ROW_BLOCK
TOK_SUB = D_MODEL // LANES
PROJ_TM = 1024
CONV_CHUNK = 64
CONV_PAD = 32
CONV_WIN = CONV_CHUNK + 32
CONV_SHROWS = CONV_CHUNK + 24
CONV_COPIES = 7
CONV_LANES = 128
ATT_TQ = SEQ
ATT_TK = 256
N_PAIRS = N_HEADS // 2
VMEM_LIMIT = 56 * 1024 * 1024

F32 = jnp.float32
BF16 = jnp.bfloat16
NEG_INF = float("-inf")
LOG2E = 1.4426950408889634


def _dot(a, b):
    return jnp.dot(a, b, preferred_element_type=F32)


def _dot_nt(a, b):
    return lax.dot_general(a, b, (((1,), (1,)), ((), ())), preferred_element_type=F32)


def _sigmoid(x):
    return 1.0 / (1.0 + jnp.exp(-x))


def _const_spec(shape):
    nd = len(shape)
    return pl.BlockSpec(shape, lambda *_: (0,) * nd)


def _head_rms(t, hsum_ref, hexp_ref, gain):
    ss = _dot((t * t).astype(BF16), hsum_ref[...])
    inv = lax.rsqrt(ss * (1.0 / HEAD_DIM) + RMS_EPS)
    hi = inv.astype(BF16)
    lo = (inv - hi.astype(F32)).astype(BF16)
    invb = _dot(jnp.concatenate([hi, lo], axis=1), hexp_ref[...])
    return t * invb * gain


def _proj_stages(x_ref, g1_ref, wu_ref, bu_ref, wqkv_ref, bqkv_ref, wft_ref, bft_ref, wgc_ref, bgc_ref,
                 wga_ref, bga_ref, qg_ref, kg_ref, hsum_ref, hexp_ref,
                 q_ref, k_ref, v_ref, lf_ref, sgc_ref, sga_ref):
    x = x_ref[...]
    ms = jnp.mean(x * x, axis=-1, keepdims=True)
    xn = (x * lax.rsqrt(ms + RMS_EPS) * g1_ref[...]).astype(BF16)

    def glu():
        u = _dot(xn, wu_ref[...]) + bu_ref[...]
        return (u[:, :CONV_WIDTH] * _sigmoid(u[:, CONV_WIDTH:])).astype(BF16)

    def qkv_stage():
        qkv = _dot(xn, wqkv_ref[...]) + bqkv_ref[...]
        q_ref[...] = _head_rms(qkv[:, :ATTN_WIDTH], hsum_ref, hexp_ref, qg_ref[...]).astype(BF16)
        k_ref[...] = _head_rms(qkv[:, ATTN_WIDTH:2 * ATTN_WIDTH], hsum_ref, hexp_ref, kg_ref[...]).astype(BF16)
        v = qkv[:, 2 * ATTN_WIDTH:]
        vb = v_ref.shape[2]
        for c in range(v_ref.shape[0]):
            v_ref[c] = v[c * vb:(c + 1) * vb, :].T.astype(BF16)

    def gate_stage_c():
        f = _dot_nt(wft_ref[...], xn) + bft_ref[...]
        lf_ref[...] = jnp.minimum(f, 0.0) - jnp.log(1.0 + jnp.exp(-jnp.abs(f)))
        sgc_ref[...] = _sigmoid(_dot(xn, wgc_ref[...]) + bgc_ref[...]).astype(BF16)

    def gate_stage_a():
        sga_ref[...] = _sigmoid(_dot(xn, wga_ref[...]) + bga_ref[...]).astype(BF16)

    return glu, (qkv_stage, gate_stage_c, gate_stage_a)


def _proj_kernel(*refs):
    ins, a_ref, outs = refs[:16], refs[16], refs[17:]
    glu, stages = _proj_stages(*ins, *outs)
    a_ref[...] = glu()
    for stage in stages:
        stage()


def _proj_conv_kernel(*refs):
    ins = refs[:16]
    am_ref, shift_ref, dw_ref, dwb_ref, lng_ref, lnb_ref = refs[16:22]
    act_ref, outs = refs[22], refs[23:29]
    win_scr, sh_scr = refs[29:]
    tm = act_ref.shape[0]
    i = pl.program_id(0)
    first = (i % (SEQ // tm)) == 0

    @pl.when(first)
    def _():
        win_scr[0:N_META, :] = jnp.zeros((N_META, CONV_WIDTH), BF16)
        win_scr[N_META:CONV_PAD, :] = am_ref[0:N_META, :]

    @pl.when(jnp.logical_not(first))
    def _():
        win_scr[0:CONV_PAD, :] = win_scr[tm:tm + CONV_PAD, :]

    glu, stages = _proj_stages(*ins, *outs)
    win_scr[CONV_PAD:CONV_PAD + tm, :] = glu()

    def conv_chunk(ci):
        r0 = ci * CONV_CHUNK
        slot = ci % 2
        sh_scr[slot] = _dot(shift_ref[...], win_scr[r0:r0 + CONV_WIN, :])
        parts = []
        for c0 in range(0, CONV_WIDTH, CONV_LANES):
            cs = slice(c0, c0 + CONV_LANES)
            part = jnp.zeros((CONV_CHUNK, CONV_LANES), F32) + dwb_ref[:, cs]
            for b in range(8):
                if b == 0:
                    rows = win_scr[r0:r0 + CONV_WIN, cs].astype(F32)
                else:
                    rows = sh_scr[slot, (b - 1) * CONV_SHROWS:b * CONV_SHROWS, cs]
                for j in range(CONV_K):
                    off = j + 2
                    if off % 8 == b:
                        part = part + rows[off - b:off - b + CONV_CHUNK] * dw_ref[j:j + 1, cs]
            parts.append(part)
        acc = jnp.concatenate(parts, axis=1)
        mu = jnp.mean(acc, axis=-1, keepdims=True)
        d = acc - mu
        var = jnp.mean(d * d, axis=-1, keepdims=True)
        y = d * lax.rsqrt(var + LN_EPS) * lng_ref[...] + lnb_ref[...]
        act_ref[r0:r0 + CONV_CHUNK, :] = (y * _sigmoid(y)).astype(BF16)

    n_chunks = tm // CONV_CHUNK
    per_stage = -(-n_chunks // (len(stages) + 1))
    ci = 0
    for stage in (None,) + stages:
        if stage is not None:
            stage()
        for _ in range(per_stage):
            if ci < n_chunks:
                conv_chunk(ci)
                ci += 1


def _proj(x2d, consts, tm, conv_consts=None):
    n = x2d.shape[0]
    row = lambda w: pl.BlockSpec((tm, w), lambda i: (i, 0))
    in_specs = [row(D_MODEL)] + [_const_spec(c.shape) for c in consts]
    fused = conv_consts is not None
    extra = tuple(conv_consts) if fused else ()
    in_specs += [_const_spec(c.shape) for c in extra]
    scratch = [pltpu.VMEM((CONV_PAD + tm, CONV_WIDTH), BF16),
               pltpu.VMEM((2, CONV_COPIES * CONV_SHROWS, CONV_WIDTH), F32)] if fused else []
    vb = min(tm, ATT_TK)
    out_shape = [
        jax.ShapeDtypeStruct((n, CONV_WIDTH), BF16),
        jax.ShapeDtypeStruct((n, ATTN_WIDTH), BF16),
        jax.ShapeDtypeStruct((n, ATTN_WIDTH), BF16),
        jax.ShapeDtypeStruct((n // vb, ATTN_WIDTH, vb), BF16),
        jax.ShapeDtypeStruct((16, n), F32),
        jax.ShapeDtypeStruct((n, D_MODEL), BF16),
        jax.ShapeDtypeStruct((n, D_MODEL), BF16),
    ]
    out_specs = [row(CONV_WIDTH), row(ATTN_WIDTH), row(ATTN_WIDTH),
                 pl.BlockSpec((tm // vb, ATTN_WIDTH, vb), lambda i: (i, 0, 0)),
                 pl.BlockSpec((16, tm), lambda i: (0, i)), row(D_MODEL), row(D_MODEL)]
    return pl.pallas_call(
        _proj_conv_kernel if fused else _proj_kernel,
        grid=(n // tm,),
        in_specs=in_specs,
        out_specs=out_specs,
        out_shape=out_shape,
        scratch_shapes=scratch,
        compiler_params=pltpu.CompilerParams(dimension_semantics=("arbitrary",), vmem_limit_bytes=VMEM_LIMIT),
        name="proj_conv" if fused else "proj",
    )(x2d, *consts, *extra)


def _conv_shift_matrix():
    rr = jnp.arange(CONV_COPIES * CONV_SHROWS)
    src = rr % CONV_SHROWS + rr // CONV_SHROWS + 1
    return (src[:, None] == jnp.arange(CONV_WIN)[None, :]).astype(BF16)


AUG_TERMS = 3
ACC_ONES = 16
SCORE_LEAD = 8


def _split_terms(x):
    hi = x.astype(BF16)
    r1 = x - hi.astype(F32)
    mid = r1.astype(BF16)
    lo = (r1 - mid.astype(F32)).astype(BF16)
    return hi, mid, lo


def _attn_t_kernel(q_ref, k_ref, vt_ref, lf_ref, km_ref, vm_ref, lfm_ref, place_ref, o_ref,
                   aug_scr, augm_scr, m_scr, acc_scr):
    p = pl.program_id(1)
    nk = SEQ // ATT_TK

    def bias_lanes(colsum):
        out = None
        for t, term in enumerate(_split_terms(colsum)):
            d = _dot(term, place_ref[t])
            out = d if out is None else out + d
        return out.astype(BF16)

    @pl.when(p == 0)
    def _():
        r = lax.broadcasted_iota(jnp.int32, (ATT_TK, ATT_TK), 0)
        c = lax.broadcasted_iota(jnp.int32, (ATT_TK, ATT_TK), 1)
        tri = (c <= r).astype(BF16)
        local = []
        for j in range(nk):
            lf = lf_ref[:, j * ATT_TK:(j + 1) * ATT_TK]
            csum = None
            for term in _split_terms(lf):
                d = _dot_nt(tri, term)
                csum = d if csum is None else csum + d
            local.append(csum)
        carry = jnp.zeros((1, 16), F32)
        for j in range(nk):
            csum = local[j] + carry
            aug_scr[j * ATT_TK:(j + 1) * ATT_TK, :] = bias_lanes(csum * (-LOG2E))
            carry = csum[ATT_TK - 1:ATT_TK, :]
        rm = lax.broadcasted_iota(jnp.int32, (META_ROWS, META_ROWS), 0)
        cm = lax.broadcasted_iota(jnp.int32, (META_ROWS, META_ROWS), 1)
        trim = ((cm > rm) & (cm < N_META)).astype(BF16)
        msum = jnp.zeros((META_ROWS, 16), F32)
        for term in _split_terms(lfm_ref[...]):
            msum = msum + _dot_nt(trim, term)
        augm_scr[...] = bias_lanes(msum * LOG2E)

    lane = lax.broadcasted_iota(jnp.int32, (ATT_TQ, LANES), 1)
    q_all = q_ref[...]
    zq = jnp.zeros_like(q_all)
    q_cat = []
    for hl in range(2):
        own = (lane < HEAD_DIM) if hl == 0 else (lane >= HEAD_DIM)
        h = 2 * p + hl
        bias_sel = (lane >= AUG_TERMS * h) & (lane < AUG_TERMS * h + AUG_TERMS)
        bias_ones = jnp.where(bias_sel, 1.0, 0.0).astype(BF16)
        q_cat.append(jnp.concatenate([jnp.where(own, q_all, zq), bias_ones], axis=1))
        m_scr[hl] = jnp.full((1, ATT_TQ), NEG_INF, F32)
        acc_scr[hl] = jnp.zeros((HEAD_DIM + ACC_ONES, ATT_TQ), F32)

    def scores(kb, augb, lo=0):
        k_cat = jnp.concatenate([kb, augb], axis=1)
        return tuple(_dot_nt(k_cat, q_cat[hl][lo:]) for hl in range(2))

    def update(s2, vtb, lo=0):
        ones = jnp.ones((ACC_ONES, vtb.shape[1]), BF16)
        for hl in range(2):
            s = s2[hl]
            m_prev = m_scr[hl, :, lo:]
            m_next = jnp.maximum(m_prev, jnp.max(s, axis=0, keepdims=True))
            pr = jnp.exp2(s - m_next).astype(BF16)
            alpha = jnp.exp2(m_prev - m_next)
            v_own = jnp.concatenate([vtb[hl * HEAD_DIM:(hl + 1) * HEAD_DIM], ones], axis=0)
            acc_scr[hl, :, lo:] = acc_scr[hl, :, lo:] * alpha + _dot(v_own, pr)
            m_scr[hl, :, lo:] = m_next

    def block_scores(j):
        rs = slice(j * ATT_TK, (j + 1) * ATT_TK)
        return scores(k_ref[rs, :], aug_scr[rs, :], lo=j * ATT_TK)

    s_meta = scores(km_ref[0:N_META, :], augm_scr[0:N_META, :])
    s_blocks = [block_scores(j) for j in range(min(SCORE_LEAD, nk))]
    update(s_meta, vm_ref[0][:, 0:N_META])

    srow = lax.broadcasted_iota(jnp.int32, (ATT_TK, ATT_TK), 0)
    tcol = lax.broadcasted_iota(jnp.int32, (ATT_TK, ATT_TK), 1)
    tri = srow <= tcol
    for j in range(nk):
        if j + SCORE_LEAD < nk:
            s_blocks.append(block_scores(j + SCORE_LEAD))
        s_cur = tuple(jnp.concatenate([jnp.where(tri, s[:, :ATT_TK], NEG_INF), s[:, ATT_TK:]], axis=1)
                      if s.shape[1] > ATT_TK else jnp.where(tri, s, NEG_INF) for s in s_blocks[j])
        update(s_cur, vt_ref[j], lo=j * ATT_TK)

    a0 = acc_scr[0]
    a1 = acc_scr[1]
    ot = jnp.concatenate([a0[0:HEAD_DIM] / a0[HEAD_DIM:HEAD_DIM + 1], a1[0:HEAD_DIM] / a1[HEAD_DIM:HEAD_DIM + 1]],
                         axis=0)
    o_ref[...] = ot.T.astype(BF16)


def _attention_t(q, k, vt, lf, km, vm, lfm):
    nk = SEQ // ATT_TK
    hh = jnp.arange(16)[:, None]
    ll = jnp.arange(LANES)[None, :]
    place = jnp.stack([((ll == AUG_TERMS * hh + t) & (hh < N_HEADS)).astype(BF16) for t in range(AUG_TERMS)])
    return pl.pallas_call(
        _attn_t_kernel,
        grid=(BATCH, N_PAIRS),
        in_specs=[
            pl.BlockSpec((ATT_TQ, LANES), lambda b, p: (b, p)),
            pl.BlockSpec((SEQ, LANES), lambda b, p: (b, p)),
            pl.BlockSpec((nk, LANES, ATT_TK), lambda b, p: (b, p, 0)),
            pl.BlockSpec((16, SEQ), lambda b, p: (0, b)),
            pl.BlockSpec((META_ROWS, LANES), lambda b, p: (0, p)),
            pl.BlockSpec((1, LANES, META_ROWS), lambda b, p: (0, p, 0)),
            pl.BlockSpec((16, META_ROWS), lambda b, p: (0, 0)),
            _const_spec(place.shape),
        ],
        out_specs=pl.BlockSpec((ATT_TQ, LANES), lambda b, p: (b, p)),
        out_shape=jax.ShapeDtypeStruct((N_TOK, ATTN_WIDTH), BF16),
        scratch_shapes=[
            pltpu.VMEM((SEQ, LANES), BF16),
            pltpu.VMEM((META_ROWS, LANES), BF16),
            pltpu.VMEM((2, 1, ATT_TQ), F32),
            pltpu.VMEM((2, HEAD_DIM + ACC_ONES, ATT_TQ), F32),
        ],
        compiler_params=pltpu.CompilerParams(
            dimension_semantics=("arbitrary", "arbitrary"), vmem_limit_bytes=VMEM_LIMIT),
        name="attn",
    )(q, k, vt, lf, km, vm, lfm, place)


MERGE_TM = 1024


def _merge_kernel(act_ref, o_ref, sgc_ref, sga_ref, x_ref, wpw_ref, bpw_ref, wao_ref, wout_ref, g2_ref,
                  wrt_ref, brt_ref, tri_ref,
                  h1_ref, hn_ref, route_ref, routet_ref, cnt_ref, base_scr):
    i = pl.program_id(0)

    @pl.when(i == 0)
    def _():
        base_scr[...] = jnp.zeros_like(base_scr)

    tm = x_ref.shape[0]
    hn_parts = []
    for rs in (slice(0, tm // 2), slice(tm // 2, tm)):
        y_conv = _dot(act_ref[rs, :], wpw_ref[...]) + bpw_ref[...]
        y_attn = _dot(o_ref[rs, :], wao_ref[...])
        merged = sgc_ref[rs, :].astype(F32) * y_conv + sga_ref[rs, :].astype(F32) * y_attn
        h1 = x_ref[rs, :] + _dot(merged.astype(BF16), wout_ref[...])
        h1_ref[rs, :] = h1
        ms = jnp.mean(h1 * h1, axis=-1, keepdims=True)
        hn_half = (h1 * lax.rsqrt(ms + RMS_EPS) * g2_ref[...]).astype(BF16)
        hn_ref[rs] = hn_half.reshape(tm // 2, TOK_SUB, LANES)
        hn_parts.append(hn_half)
    hn = jnp.concatenate(hn_parts, axis=0)

    logits = _dot_nt(wrt_ref[...], hn) + brt_ref[...]
    row = lax.broadcasted_iota(jnp.int32, logits.shape, 0).astype(F32)
    big = float(LANES)
    is_g = row < N_GROUPS
    gl = jnp.where(is_g, logits, NEG_INF)
    gmax = jnp.max(gl, axis=0, keepdims=True)
    gidx = jnp.min(jnp.where(gl == gmax, row, big), axis=0, keepdims=True)
    gsum = jnp.sum(jnp.where(is_g, jnp.exp(gl - gmax), 0.0), axis=0, keepdims=True)
    g_w = 1.0 / gsum
    lo = N_GROUPS + EXPERTS_PER_GROUP * gidx
    el = jnp.where((row >= lo) & (row < lo + EXPERTS_PER_GROUP), logits, NEG_INF)
    t1 = jnp.max(el, axis=0, keepdims=True)
    i1 = jnp.min(jnp.where(el == t1, row, big), axis=0, keepdims=True)
    el2 = jnp.where(row == i1, NEG_INF, el)
    t2 = jnp.max(el2, axis=0, keepdims=True)
    i2 = jnp.min(jnp.where(el2 == t2, row, big), axis=0, keepdims=True)
    e2 = jnp.exp(t2 - t1)
    den = 1.0 + e2
    w1 = (1.0 / den) * g_w
    w2 = (e2 / den) * g_w
    ea = i1 - N_GROUPS
    eb = i2 - N_GROUPS

    oha = (row == ea).astype(F32)
    ohb = (row == eb).astype(F32)
    ohs = oha + ohb
    before = _dot(ohs.astype(BF16), tri_ref[...]) + base_scr[...][:, 0:1]
    ra = jnp.sum(before * oha, axis=0, keepdims=True)
    rb = jnp.sum(before * ohb, axis=0, keepdims=True)
    base_scr[...] = base_scr[...] + jnp.sum(ohs, axis=1, keepdims=True)
    cnt_ref[...] = base_scr[...]

    def rows_of(shape):
        r = lax.broadcasted_iota(jnp.int32, shape, 0)
        out = jnp.where(r == 0, ea, 0.0)
        for k, val in enumerate((eb, w1, w2, ra, rb), start=1):
            out = jnp.where(r == k, val, out)
        return out

    routet_ref[...] = rows_of((8, logits.shape[1]))
    route_ref[...] = rows_of(logits.shape).T


def _merge(act, o, sgc, sga, x2d, consts):
    tm = MERGE_TM
    row = lambda w: pl.BlockSpec((tm, w), lambda i: (i, 0))
    return pl.pallas_call(
        _merge_kernel,
        grid=(N_TOK // tm,),
        in_specs=[row(CONV_WIDTH), row(ATTN_WIDTH), row(D_MODEL), row(D_MODEL), row(D_MODEL)]
        + [_const_spec(c.shape) for c in consts],
        out_specs=[row(D_MODEL), pl.BlockSpec((tm, TOK_SUB, LANES), lambda i: (i, 0, 0)), row(LANES),
                   pl.BlockSpec((8, tm), lambda i: (0, i)), _const_spec((LANES, LANES))],
        out_shape=[jax.ShapeDtypeStruct((N_TOK, D_MODEL), F32),
                   jax.ShapeDtypeStruct((N_TOK, TOK_SUB, LANES), BF16),
                   jax.ShapeDtypeStruct((N_TOK, LANES), F32),
                   jax.ShapeDtypeStruct((8, N_TOK), F32),
                   jax.ShapeDtypeStruct((LANES, LANES), F32)],
        scratch_shapes=[pltpu.VMEM((LANES, LANES), F32)],
        compiler_params=pltpu.CompilerParams(dimension_semantics=("arbitrary",), vmem_limit_bytes=VMEM_LIMIT),
        name="merge",
    )(act, o, sgc, sga, x2d, *consts)


DISP_TM = 1024
DMA_UNROLL = 8


def _dest_kernel(pstart_ref, rt_ref, d_ref):
    rt = rt_ref[...]
    start = jnp.zeros_like(rt)
    for e in range(N_EXPERTS):
        start = jnp.where(rt == float(e), pstart_ref[e].astype(F32), start)
    d_ref[...] = (start + pltpu.roll(rt, 4, axis=0)).astype(jnp.int32)


def _dest(pstart, route_t):
    grid_spec = pltpu.PrefetchScalarGridSpec(
        num_scalar_prefetch=1,
        grid=(1,),
        in_specs=[pl.BlockSpec(route_t.shape, lambda i, *_: (0, 0))],
        out_specs=pl.BlockSpec(route_t.shape, lambda i, *_: (0, 0)),
    )
    return pl.pallas_call(
        _dest_kernel,
        grid_spec=grid_spec,
        out_shape=jax.ShapeDtypeStruct(route_t.shape, jnp.int32),
        compiler_params=pltpu.CompilerParams(dimension_semantics=("arbitrary",), vmem_limit_bytes=VMEM_LIMIT),
        name="dest",
    )(pstart, route_t)


ZERO_CHUNK = 64


def _dispatch_kernel(d0_ref, d1_ref, pfill_ref, pend_ref, hn_ref, xin_ref, zero_scr, sem, zsem):
    i = pl.program_id(0)
    base = i * DISP_TM

    @pl.when(i == 0)
    def _():
        zero_scr[...] = jnp.zeros_like(zero_scr)

        def pad_chunks(e, fn):
            first = pfill_ref[e] // ZERO_CHUNK * ZERO_CHUNK

            def one(c, carry):
                start = pl.multiple_of(first + c * ZERO_CHUNK, ZERO_CHUNK)
                fn(pltpu.make_async_copy(zero_scr.at[pl.ds(0, ZERO_CHUNK)], xin_ref.at[pl.ds(start, ZERO_CHUNK)],
                                         zsem.at[0]))
                return carry

            lax.fori_loop(0, (pend_ref[e] - first) // ZERO_CHUNK, one, 0)

        def zstart(e, carry):
            pad_chunks(e, lambda c: c.start())
            return carry

        def zwait(e, carry):
            pad_chunks(e, lambda c: c.wait())
            return carry

        n_used = pend_ref[N_EXPERTS - 1] // ROW_BLOCK

        def tcopy(b):
            return pltpu.make_async_copy(
                zero_scr, xin_ref.at[pl.ds(pl.multiple_of(b * ROW_BLOCK, ROW_BLOCK), ROW_BLOCK)], zsem.at[0])

        def tstart(b, carry):
            tcopy(b).start()
            return carry

        def twait(b, carry):
            tcopy(b).wait()
            return carry

        lax.fori_loop(0, N_EXPERTS, zstart, 0)
        lax.fori_loop(n_used, N_BLOCKS, tstart, 0)
        lax.fori_loop(0, N_EXPERTS, zwait, 0)
        lax.fori_loop(n_used, N_BLOCKS, twait, 0)

    def start(r, carry):
        pltpu.make_async_copy(hn_ref.at[r], xin_ref.at[d0_ref[base + r]], sem.at[0]).start()
        pltpu.make_async_copy(hn_ref.at[r], xin_ref.at[d1_ref[base + r]], sem.at[1]).start(priority=1)
        return carry

    lax.fori_loop(0, DISP_TM, start, 0, unroll=DMA_UNROLL)
    pltpu.make_async_copy(hn_ref, xin_ref.at[pl.ds(0, DISP_TM)], sem.at[0]).wait()
    pltpu.make_async_copy(hn_ref, xin_ref.at[pl.ds(0, DISP_TM)], sem.at[1]).wait()


def _dispatch(d0, d1, pfill, pend, hn):
    grid_spec = pltpu.PrefetchScalarGridSpec(
        num_scalar_prefetch=4,
        grid=(N_TOK // DISP_TM,),
        in_specs=[pl.BlockSpec((DISP_TM, TOK_SUB, LANES), lambda i, *_: (i, 0, 0))],
        out_specs=pl.BlockSpec(memory_space=pl.ANY),
        scratch_shapes=[pltpu.VMEM((ROW_BLOCK, TOK_SUB, LANES), BF16),
                        pltpu.SemaphoreType.DMA((2,)), pltpu.SemaphoreType.DMA((1,))],
    )
    return pl.pallas_call(
        _dispatch_kernel,
        grid_spec=grid_spec,
        out_shape=jax.ShapeDtypeStruct((P_ROWS, TOK_SUB, LANES), BF16),
        compiler_params=pltpu.CompilerParams(dimension_semantics=("arbitrary",), vmem_limit_bytes=VMEM_LIMIT),
        name="dispatch",
    )(d0, d1, pfill, pend, hn)


W_SLOTS = 3


def _expert_kernel(nu_ref, first_ref, ord_ref, used_ref, nue_ref, x_hbm, wg_hbm, wu_hbm, wd_hbm, y_hbm,
                   x_buf, y_buf, wg_f, wu_f, wd_f, wg_b, wu_b, wd_b, xsem, ysem, wsem):
    n = nu_ref[0]

    def rows(i):
        return pl.ds(pl.multiple_of(i * ROW_BLOCK, ROW_BLOCK), ROW_BLOCK)

    def xcopy(i, slot):
        return pltpu.make_async_copy(x_hbm.at[rows(i)], x_buf.at[slot], xsem.at[slot])

    def ycopy(i, slot):
        return pltpu.make_async_copy(y_buf.at[slot], y_hbm.at[rows(i)], ysem.at[slot])

    def fetch(k):
        e = used_ref[k]
        slot = k % W_SLOTS
        return (pltpu.make_async_copy(wg_hbm.at[e], wg_f.at[slot], wsem.at[slot, 0]),
                pltpu.make_async_copy(wu_hbm.at[e], wu_f.at[slot], wsem.at[slot, 1]),
                pltpu.make_async_copy(wd_hbm.at[e], wd_f.at[slot], wsem.at[slot, 2]))

    for k in range(W_SLOTS - 1):
        @pl.when(k < nue_ref[0])
        def _():
            for c in fetch(k):
                c.start()
    xcopy(0, 0).start()

    def block(i, carry):
        slot = i % 2
        xcopy(i, slot).wait()

        @pl.when(i + 1 < n)
        def _():
            xcopy(i + 1, 1 - slot).start()

        @pl.when(first_ref[i] == 1)
        def _():
            k = ord_ref[i]
            for c in fetch(k):
                c.wait()

            @pl.when(k + (W_SLOTS - 1) < nue_ref[0])
            def _():
                for c in fetch(k + (W_SLOTS - 1)):
                    c.start()

            wslot = k % W_SLOTS
            wg_b[...] = wg_f[wslot].astype(BF16)
            wu_b[...] = wu_f[wslot].astype(BF16)
            wd_b[...] = wd_f[wslot].astype(BF16)

        @pl.when(i >= 2)
        def _():
            ycopy(i - 2, slot).wait()

        x = x_buf[slot].reshape(ROW_BLOCK, D_MODEL)
        g = _dot(x, wg_b[...])
        u = _dot(x, wu_b[...])
        hmid = (g * _sigmoid(g) * u).astype(BF16)
        y = _dot(hmid, wd_b[...]).astype(BF16)
        y_buf[slot] = y.reshape(ROW_BLOCK, TOK_SUB, LANES)
        ycopy(i, slot).start()
        return carry

    lax.fori_loop(0, n, block, 0)

    @pl.when(n >= 2)
    def _():
        ycopy(n - 2, n % 2).wait()
    ycopy(n - 1, (n - 1) % 2).wait()

    y_buf[0] = jnp.zeros((ROW_BLOCK, TOK_SUB, LANES), BF16)

    def zcopy(b):
        return pltpu.make_async_copy(y_buf.at[0], y_hbm.at[rows(b)], ysem.at[0])

    def zstart(b, carry):
        zcopy(b).start()
        return carry

    def zwait(b, carry):
        zcopy(b).wait()
        return carry

    lax.fori_loop(n, N_BLOCKS, zstart, 0)
    lax.fori_loop(n, N_BLOCKS, zwait, 0)


def _experts(n_used, blk_first, blk_ord, used_e, n_used_e, xin, w_gate, w_up, w_down):
    grid_spec = pltpu.PrefetchScalarGridSpec(
        num_scalar_prefetch=5,
        grid=(1,),
        in_specs=[pl.BlockSpec(memory_space=pl.ANY),
                  pl.BlockSpec(memory_space=pl.ANY),
                  pl.BlockSpec(memory_space=pl.ANY),
                  pl.BlockSpec(memory_space=pl.ANY)],
        out_specs=pl.BlockSpec(memory_space=pl.ANY),
        scratch_shapes=[pltpu.VMEM((2, ROW_BLOCK, TOK_SUB, LANES), BF16),
                        pltpu.VMEM((2, ROW_BLOCK, TOK_SUB, LANES), BF16),
                        pltpu.VMEM((W_SLOTS, D_MODEL, D_EXPERT), F32),
                        pltpu.VMEM((W_SLOTS, D_MODEL, D_EXPERT), F32),
                        pltpu.VMEM((W_SLOTS, D_EXPERT, D_MODEL), F32),
                        pltpu.VMEM((D_MODEL, D_EXPERT), BF16),
                        pltpu.VMEM((D_MODEL, D_EXPERT), BF16),
                        pltpu.VMEM((D_EXPERT, D_MODEL), BF16),
                        pltpu.SemaphoreType.DMA((2,)), pltpu.SemaphoreType.DMA((2,)),
                        pltpu.SemaphoreType.DMA((W_SLOTS, 3))],
    )
    return pl.pallas_call(
        _expert_kernel,
        grid_spec=grid_spec,
        out_shape=jax.ShapeDtypeStruct((P_ROWS, TOK_SUB, LANES), BF16),
        compiler_params=pltpu.CompilerParams(dimension_semantics=("arbitrary",), vmem_limit_bytes=VMEM_LIMIT),
        name="experts",
    )(n_used, blk_first, blk_ord, used_e, n_used_e, xin, w_gate, w_up, w_down)


COMB_TM = 512


def _combine_kernel(d0_ref, d1_ref, h1_ref, route_ref, yb_ref, out_ref, g_scr, sem):
    i = pl.program_id(0)
    n = pl.num_programs(0)

    def issue(tile, slot):
        base = tile * COMB_TM

        def start(r, carry):
            pltpu.make_async_copy(yb_ref.at[d0_ref[base + r]], g_scr.at[slot, 0, r], sem.at[slot, 0]).start()
            pltpu.make_async_copy(yb_ref.at[d1_ref[base + r]], g_scr.at[slot, 1, r], sem.at[slot, 1]).start(
                priority=1)
            return carry

        lax.fori_loop(0, COMB_TM, start, 0, unroll=DMA_UNROLL)

    @pl.when(i == 0)
    def _():
        issue(0, 0)

    slot = i % 2

    @pl.when(i + 1 < n)
    def _():
        issue(i + 1, 1 - slot)

    for k in range(2):
        pltpu.make_async_copy(yb_ref.at[pl.ds(0, COMB_TM)], g_scr.at[slot, k], sem.at[slot, k]).wait()

    route = route_ref[...]
    w0 = route[:, 2:3]
    w1 = route[:, 3:4]
    g0 = g_scr[slot, 0].reshape(COMB_TM, D_MODEL).astype(F32)
    g1 = g_scr[slot, 1].reshape(COMB_TM, D_MODEL).astype(F32)
    out_ref[...] = h1_ref[...] + (g0 * w0 + g1 * w1)


def _combine(d0, d1, h1, route, yb):
    grid_spec = pltpu.PrefetchScalarGridSpec(
        num_scalar_prefetch=2,
        grid=(N_TOK // COMB_TM,),
        in_specs=[pl.BlockSpec((COMB_TM, D_MODEL), lambda i, *_: (i, 0)),
                  pl.BlockSpec((COMB_TM, LANES), lambda i, *_: (i, 0)),
                  pl.BlockSpec(memory_space=pl.ANY)],
        out_specs=pl.BlockSpec((COMB_TM, D_MODEL), lambda i, *_: (i, 0)),
        scratch_shapes=[pltpu.VMEM((2, 2, COMB_TM, TOK_SUB, LANES), BF16),
                        pltpu.SemaphoreType.DMA((2, 2))],
    )
    return pl.pallas_call(
        _combine_kernel,
        grid_spec=grid_spec,
        out_shape=jax.ShapeDtypeStruct((N_TOK, D_MODEL), F32),
        compiler_params=pltpu.CompilerParams(dimension_semantics=("arbitrary",), vmem_limit_bytes=VMEM_LIMIT),
        name="combine",
    )(d0, d1, h1, route, yb)


def kernel(x, meta, norm1_g, w_in, b_in, dw_w, dw_b, conv_ln_g, conv_ln_b, w_conv_out, b_conv_out,
           q_norm_g, k_norm_g, b_forget, w_attn_out, w_out, norm2_g, w_group, b_group, w_router,
           b_router, w_gate, w_up, w_down):
    l = 0
    row = lambda v: v.reshape(1, -1).astype(F32)

    c_u, c_q, c_f = 2 * CONV_WIDTH, 2 * CONV_WIDTH, 2 * CONV_WIDTH + 3 * ATTN_WIDTH
    c_gc = c_f + N_HEADS
    c_ga = c_gc + D_MODEL
    wi, bi = w_in[l], b_in[l]
    wu, bu = wi[:, :c_u].astype(BF16), row(bi[:c_u])
    wqkv, bqkv = wi[:, c_q:c_f].astype(BF16), row(bi[c_q:c_f])
    wft = jnp.zeros((16, D_MODEL), F32).at[:N_HEADS].set(wi[:, c_f:c_gc].T).astype(BF16)
    bft = jnp.zeros((16, 1), F32).at[:N_HEADS, 0].set(bi[c_f:c_gc] + b_forget[l])
    wgc, bgc = wi[:, c_gc:c_ga].astype(BF16), row(bi[c_gc:c_ga])
    wga, bga = wi[:, c_ga:].astype(BF16), row(bi[c_ga:])
    qg = row(jnp.tile(q_norm_g[l], N_HEADS) * (HEAD_DIM ** -0.5 * LOG2E))
    kg = row(jnp.tile(k_norm_g[l], N_HEADS))
    head_of_col = jnp.arange(ATTN_WIDTH) // HEAD_DIM
    hsum = (head_of_col[:, None] == jnp.arange(LANES)[None, :]).astype(BF16)
    hexp_half = (jnp.arange(LANES)[:, None] == head_of_col[None, :]).astype(BF16)
    hexp = jnp.concatenate([hexp_half, hexp_half], axis=0)
    proj_consts = (row(norm1_g[l]), wu, bu, wqkv, bqkv, wft, bft, wgc, bgc, wga, bga, qg, kg, hsum, hexp)

    x2d = x.reshape(N_TOK, D_MODEL)
    meta_pad = jnp.zeros((META_ROWS, D_MODEL), F32).at[:N_META].set(meta.astype(F32))
    a_m, _, k_m, v_m, lf_m, _, _ = _proj(meta_pad, proj_consts, META_ROWS)
    conv_consts = (a_m, _conv_shift_matrix(), dw_w[l].astype(F32), row(dw_b[l]),
                   row(conv_ln_g[l]), row(conv_ln_b[l]))
    act, q, k, v, lf, sgc, sga = _proj(x2d, proj_consts, PROJ_TM, conv_consts)

    o = _attention_t(q, k, v, lf, k_m, v_m, lf_m)

    wrt = jnp.zeros((LANES, D_MODEL), F32).at[:N_GROUPS].set(w_group[l].T)
    wrt = wrt.at[N_GROUPS:N_GROUPS + N_EXPERTS].set(w_router[l].T).astype(BF16)
    brt = jnp.zeros((LANES, 1), F32).at[:N_GROUPS, 0].set(b_group[l])
    brt = brt.at[N_GROUPS:N_GROUPS + N_EXPERTS, 0].set(b_router[l])
    tri = jnp.triu(jnp.ones((MERGE_TM, MERGE_TM), F32), 1).astype(BF16)
    merge_consts = (w_conv_out[l].astype(BF16), row(b_conv_out[l]), w_attn_out[l].astype(BF16),
                    w_out[l].astype(BF16), row(norm2_g[l]), wrt, brt, tri)
    h1, hn, route, route_t, cnt = _merge(act, o, sgc, sga, x2d, merge_consts)

    counts = cnt[:N_EXPERTS, 0].astype(jnp.int32)
    padded = (counts + ROW_BLOCK - 1) // ROW_BLOCK * ROW_BLOCK
    pend = jnp.cumsum(padded).astype(jnp.int32)
    pstart = pend - padded
    blk_row = jnp.arange(N_BLOCKS, dtype=jnp.int32) * ROW_BLOCK
    n_used = pend[-1:] // ROW_BLOCK
    nz = counts > 0
    before = (pend[None, :] <= blk_row[:, None]) & nz[None, :]
    blk_ord = jnp.sum(before.astype(jnp.int32), axis=1)
    blk_first = jnp.any((pstart[None, :] == blk_row[:, None]) & nz[None, :], axis=1).astype(jnp.int32)
    e_ord = jnp.cumsum(nz.astype(jnp.int32)) - 1
    e_ids = jnp.arange(N_EXPERTS, dtype=jnp.int32)
    used_e = jnp.sum(jnp.where((e_ord[None, :] == e_ids[:, None]) & nz[None, :], e_ids[None, :], 0), axis=1)
    n_used_e = jnp.sum(nz.astype(jnp.int32)).reshape(1)

    dest = _dest(pstart, route_t)
    d0, d1 = dest[0], dest[1]
    xin = _dispatch(d0, d1, pstart + counts, pend, hn)
    yb = _experts(n_used, blk_first, blk_ord, used_e.astype(jnp.int32), n_used_e, xin,
                  w_gate[l], w_up[l], w_down[l])
    out = _combine(d0, d1, h1, route, yb)
    return out.reshape(BATCH, SEQ, D_MODEL)
```

```python
import jax
import jax.numpy as jnp
from jax import lax
from jax.experimental import pallas as pl
from jax.experimental.pallas import tpu as pltpu

D_MODEL = 1024
BATCH = 8
SEQ = 2048
N_META = 16
CONV_WIDTH = 512
CONV_K = 31
N_HEADS = 8
HEAD_DIM = 64
ATTN_WIDTH = N_HEADS * HEAD_DIM
N_GROUPS = 8
EXPERTS_PER_GROUP = 8
N_EXPERTS = N_GROUPS * EXPERTS_PER_GROUP
D_EXPERT = 256
RMS_EPS = 1e-6
LN_EPS = 1e-5

N_TOK = BATCH * SEQ
LANES = 128
META_ROWS = 128
ROW_BLOCK = 512
N_ASSIGN = 2 * N_TOK
N_BLOCKS = N_ASSIGN // ROW_BLOCK + N_EXPERTS
P_ROWS = N_BLOCKS * ROW_BLOCK
TOK_SUB = D_MODEL // LANES
PROJ_TM = 1024
CONV_CHUNK = 128
CONV_PAD = 32
CONV_WIN = CONV_CHUNK + 32
CONV_SHROWS = CONV_CHUNK + 24
CONV_COPIES = 7
CONV_LANES = 128
ATT_TQ = SEQ
ATT_TK = 256
N_PAIRS = N_HEADS // 2
VMEM_LIMIT = 56 * 1024 * 1024

F32 = jnp.float32
BF16 = jnp.bfloat16
NEG_INF = float("-inf")
LOG2E = 1.4426950408889634


def _dot(a, b):
    return jnp.dot(a, b, preferred_element_type=F32)


def _dot_nt(a, b):
    return lax.dot_general(a, b, (((1,), (1,)), ((), ())), preferred_element_type=F32)


def _sigmoid(x):
    return 1.0 / (1.0 + jnp.exp(-x))


def _const_spec(shape):
    nd = len(shape)
    return pl.BlockSpec(shape, lambda *_: (0,) * nd)


def _head_rms(t, hsum_ref, hexp_ref, gain):
    ss = _dot((t * t).astype(BF16), hsum_ref[...])
    inv = lax.rsqrt(ss * (1.0 / HEAD_DIM) + RMS_EPS)
    hi = inv.astype(BF16)
    lo = (inv - hi.astype(F32)).astype(BF16)
    invb = _dot(jnp.concatenate([hi, lo], axis=1), hexp_ref[...])
    return t * invb * gain


N_PROJ_IN = 16


def _proj_stages(x_ref, g1_ref, wu_ref, bu_ref, wqkv_ref, bqkv_ref, wft_ref, bft_ref, wgc_ref, bgc_ref,
                 wga_ref, bga_ref, qg_ref, kg_ref, hsum_ref, hexp_ref,
                 q_ref, k_ref, v_ref, lf_ref, sgc_ref, sga_ref):
    x = x_ref[...]
    ms = jnp.mean(x * x, axis=-1, keepdims=True)
    xn = (x * lax.rsqrt(ms + RMS_EPS) * g1_ref[...]).astype(BF16)

    def glu():
        u = _dot(xn, wu_ref[...]) + bu_ref[...]
        return (u[:, :CONV_WIDTH] * _sigmoid(u[:, CONV_WIDTH:])).astype(BF16)

    def qkv_stage():
        qkv = _dot(xn, wqkv_ref[...]) + bqkv_ref[...]
        q_ref[...] = _head_rms(qkv[:, :ATTN_WIDTH], hsum_ref, hexp_ref, qg_ref[...]).astype(BF16)
        k_ref[...] = _head_rms(qkv[:, ATTN_WIDTH:2 * ATTN_WIDTH], hsum_ref, hexp_ref, kg_ref[...]).astype(BF16)
        v = qkv[:, 2 * ATTN_WIDTH:]
        vb = v_ref.shape[2]
        for c in range(v_ref.shape[0]):
            v_ref[c] = v[c * vb:(c + 1) * vb, :].T.astype(BF16)

    def gate_stage_c():
        f = _dot_nt(wft_ref[...], xn) + bft_ref[...]
        lf_ref[...] = jnp.minimum(f, 0.0) - jnp.log(1.0 + jnp.exp(-jnp.abs(f)))
        sgc_ref[...] = _sigmoid(_dot(xn, wgc_ref[...]) + bgc_ref[...]).astype(BF16)

    def gate_stage_a():
        sga_ref[...] = _sigmoid(_dot(xn, wga_ref[...]) + bga_ref[...]).astype(BF16)

    return glu, (qkv_stage, gate_stage_c, gate_stage_a)


def _proj_kernel(*refs):
    ins, a_ref, outs = refs[:N_PROJ_IN], refs[N_PROJ_IN], refs[N_PROJ_IN + 1:]
    glu, stages = _proj_stages(*ins, *outs)
    a_ref[...] = glu()
    for stage in stages:
        stage()


def _proj_conv_kernel(*refs):
    ins = refs[:N_PROJ_IN]
    am_ref, shift_ref, dw_ref, dwb_ref, lng_ref, lnb_ref = refs[N_PROJ_IN:N_PROJ_IN + 6]
    act_ref, outs = refs[N_PROJ_IN + 6], refs[N_PROJ_IN + 7:N_PROJ_IN + 13]
    win_scr, sh_scr = refs[N_PROJ_IN + 13:]
    tm = act_ref.shape[0]
    i = pl.program_id(0)
    first = (i % (SEQ // tm)) == 0

    @pl.when(first)
    def _():
        win_scr[0:N_META, :] = jnp.zeros((N_META, CONV_WIDTH), BF16)
        win_scr[N_META:CONV_PAD, :] = am_ref[0:N_META, :]

    @pl.when(jnp.logical_not(first))
    def _():
        win_scr[0:CONV_PAD, :] = win_scr[tm:tm + CONV_PAD, :]

    glu, stages = _proj_stages(*ins, *outs)
    win_scr[CONV_PAD:CONV_PAD + tm, :] = glu()

    def conv_chunk(ci):
        r0 = ci * CONV_CHUNK
        slot = ci % 2
        sh_scr[slot] = _dot(shift_ref[...], win_scr[r0:r0 + CONV_WIN, :])
        parts = []
        for c0 in range(0, CONV_WIDTH, CONV_LANES):
            cs = slice(c0, c0 + CONV_LANES)
            part = jnp.zeros((CONV_CHUNK, CONV_LANES), F32) + dwb_ref[:, cs]
            for b in range(8):
                if b == 0:
                    rows = win_scr[r0:r0 + CONV_WIN, cs].astype(F32)
                else:
                    rows = sh_scr[slot, (b - 1) * CONV_SHROWS:b * CONV_SHROWS, cs]
                for j in range(CONV_K):
                    off = j + 2
                    if off % 8 == b:
                        part = part + rows[off - b:off - b + CONV_CHUNK] * dw_ref[j:j + 1, cs]
            parts.append(part)
        acc = jnp.concatenate(parts, axis=1)
        mu = jnp.mean(acc, axis=-1, keepdims=True)
        d = acc - mu
        var = jnp.mean(d * d, axis=-1, keepdims=True)
        y = d * lax.rsqrt(var + LN_EPS) * lng_ref[...] + lnb_ref[...]
        act_ref[r0:r0 + CONV_CHUNK, :] = (y * _sigmoid(y)).astype(BF16)

    n_chunks = tm // CONV_CHUNK
    per_stage = -(-n_chunks // (len(stages) + 1))
    ci = 0
    for stage in (None,) + stages:
        if stage is not None:
            stage()
        for _ in range(per_stage):
            if ci < n_chunks:
                conv_chunk(ci)
                ci += 1


def _proj(x2d, consts, tm, conv_consts=None):
    n = x2d.shape[0]
    row = lambda w: pl.BlockSpec((tm, w), lambda i: (i, 0))
    in_specs = [row(D_MODEL)] + [_const_spec(c.shape) for c in consts]
    fused = conv_consts is not None
    extra = tuple(conv_consts) if fused else ()
    in_specs += [_const_spec(c.shape) for c in extra]
    scratch = [pltpu.VMEM((CONV_PAD + tm, CONV_WIDTH), BF16),
               pltpu.VMEM((2, CONV_COPIES * CONV_SHROWS, CONV_WIDTH), F32)] if fused else []
    vb = min(tm, ATT_TK)
    out_shape = [
        jax.ShapeDtypeStruct((n, CONV_WIDTH), BF16),
        jax.ShapeDtypeStruct((n, ATTN_WIDTH), BF16),
        jax.ShapeDtypeStruct((n, ATTN_WIDTH), BF16),
        jax.ShapeDtypeStruct((n // vb, ATTN_WIDTH, vb), BF16),
        jax.ShapeDtypeStruct((16, n), F32),
        jax.ShapeDtypeStruct((n, D_MODEL), BF16),
        jax.ShapeDtypeStruct((n, D_MODEL), BF16),
    ]
    out_specs = [row(CONV_WIDTH), row(ATTN_WIDTH), row(ATTN_WIDTH),
                 pl.BlockSpec((tm // vb, ATTN_WIDTH, vb), lambda i: (i, 0, 0)),
                 pl.BlockSpec((16, tm), lambda i: (0, i)), row(D_MODEL), row(D_MODEL)]
    return pl.pallas_call(
        _proj_conv_kernel if fused else _proj_kernel,
        grid=(n // tm,),
        in_specs=in_specs,
        out_specs=out_specs,
        out_shape=out_shape,
        scratch_shapes=scratch,
        compiler_params=pltpu.CompilerParams(dimension_semantics=("arbitrary",), vmem_limit_bytes=VMEM_LIMIT),
        name="proj_conv" if fused else "proj",
    )(x2d, *consts, *extra)


def _conv_shift_matrix():
    rr = jnp.arange(CONV_COPIES * CONV_SHROWS)
    src = rr % CONV_SHROWS + rr // CONV_SHROWS + 1
    return (src[:, None] == jnp.arange(CONV_WIN)[None, :]).astype(BF16)


AUG_TERMS = 3
ACC_ONES = 16
SCORE_LEAD = 8


def _split_terms(x):
    hi = x.astype(BF16)
    r1 = x - hi.astype(F32)
    mid = r1.astype(BF16)
    lo = (r1 - mid.astype(F32)).astype(BF16)
    return hi, mid, lo


def _attn_t_kernel(q_ref, k_ref, vt_ref, lf_ref, km_ref, vm_ref, lfm_ref, place_ref, o_ref,
                   aug_scr, augm_scr, m_scr, acc_scr):
    p = pl.program_id(1)
    nk = SEQ // ATT_TK

    def bias_lanes(colsum):
        out = None
        for t, term in enumerate(_split_terms(colsum)):
            d = _dot(term, place_ref[t])
            out = d if out is None else out + d
        return out.astype(BF16)

    @pl.when(p == 0)
    def _():
        r = lax.broadcasted_iota(jnp.int32, (ATT_TK, ATT_TK), 0)
        c = lax.broadcasted_iota(jnp.int32, (ATT_TK, ATT_TK), 1)
        tri = (c <= r).astype(BF16)
        local = []
        for j in range(nk):
            lf = lf_ref[:, j * ATT_TK:(j + 1) * ATT_TK]
            csum = None
            for term in _split_terms(lf):
                d = _dot_nt(tri, term)
                csum = d if csum is None else csum + d
            local.append(csum)
        carry = jnp.zeros((1, 16), F32)
        for j in range(nk):
            csum = local[j] + carry
            aug_scr[j * ATT_TK:(j + 1) * ATT_TK, :] = bias_lanes(csum * (-LOG2E))
            carry = csum[ATT_TK - 1:ATT_TK, :]
        rm = lax.broadcasted_iota(jnp.int32, (META_ROWS, META_ROWS), 0)
        cm = lax.broadcasted_iota(jnp.int32, (META_ROWS, META_ROWS), 1)
        trim = ((cm > rm) & (cm < N_META)).astype(BF16)
        msum = jnp.zeros((META_ROWS, 16), F32)
        for term in _split_terms(lfm_ref[...]):
            msum = msum + _dot_nt(trim, term)
        augm_scr[...] = bias_lanes(msum * LOG2E)

    lane = lax.broadcasted_iota(jnp.int32, (ATT_TQ, LANES), 1)
    q_all = q_ref[...]
    zq = jnp.zeros_like(q_all)
    q_cat = []
    for hl in range(2):
        own = (lane < HEAD_DIM) if hl == 0 else (lane >= HEAD_DIM)
        h = 2 * p + hl
        bias_sel = (lane >= AUG_TERMS * h) & (lane < AUG_TERMS * h + AUG_TERMS)
        bias_ones = jnp.where(bias_sel, 1.0, 0.0).astype(BF16)
        q_cat.append(jnp.concatenate([jnp.where(own, q_all, zq), bias_ones], axis=1))
        m_scr[hl] = jnp.full((1, ATT_TQ), NEG_INF, F32)
        acc_scr[hl] = jnp.zeros((HEAD_DIM + ACC_ONES, ATT_TQ), F32)

    def scores(kb, augb, lo=0):
        k_cat = jnp.concatenate([kb, augb], axis=1)
        return tuple(_dot_nt(k_cat, q_cat[hl][lo:]) for hl in range(2))

    def update(s2, vtb, lo=0):
        ones = jnp.ones((ACC_ONES, vtb.shape[1]), BF16)
        for hl in range(2):
            s = s2[hl]
            m_prev = m_scr[hl, :, lo:]
            m_next = jnp.maximum(m_prev, jnp.max(s, axis=0, keepdims=True))
            pr = jnp.exp2(s - m_next).astype(BF16)
            alpha = jnp.exp2(m_prev - m_next)
            v_own = jnp.concatenate([vtb[hl * HEAD_DIM:(hl + 1) * HEAD_DIM], ones], axis=0)
            acc_scr[hl, :, lo:] = acc_scr[hl, :, lo:] * alpha + _dot(v_own, pr)
            m_scr[hl, :, lo:] = m_next

    def block_scores(j):
        rs = slice(j * ATT_TK, (j + 1) * ATT_TK)
        return scores(k_ref[rs, :], aug_scr[rs, :], lo=j * ATT_TK)

    s_meta = scores(km_ref[0:N_META, :], augm_scr[0:N_META, :])
    s_blocks = [block_scores(j) for j in range(min(SCORE_LEAD, nk))]
    update(s_meta, vm_ref[0][:, 0:N_META])

    srow = lax.broadcasted_iota(jnp.int32, (ATT_TK, ATT_TK), 0)
    tcol = lax.broadcasted_iota(jnp.int32, (ATT_TK, ATT_TK), 1)
    tri = srow <= tcol
    for j in range(nk):
        if j + SCORE_LEAD < nk:
            s_blocks.append(block_scores(j + SCORE_LEAD))
        s_cur = tuple(jnp.concatenate([jnp.where(tri, s[:, :ATT_TK], NEG_INF), s[:, ATT_TK:]], axis=1)
                      if s.shape[1] > ATT_TK else jnp.where(tri, s, NEG_INF) for s in s_blocks[j])
        update(s_cur, vt_ref[j], lo=j * ATT_TK)

    a0 = acc_scr[0]
    a1 = acc_scr[1]
    ot = jnp.concatenate([a0[0:HEAD_DIM] / a0[HEAD_DIM:HEAD_DIM + 1], a1[0:HEAD_DIM] / a1[HEAD_DIM:HEAD_DIM + 1]],
                         axis=0)
    o_ref[...] = ot.T.astype(BF16)


def _attention_t(q, k, vt, lf, km, vm, lfm):
    nk = SEQ // ATT_TK
    hh = jnp.arange(16)[:, None]
    ll = jnp.arange(LANES)[None, :]
    place = jnp.stack([((ll == AUG_TERMS * hh + t) & (hh < N_HEADS)).astype(BF16) for t in range(AUG_TERMS)])
    return pl.pallas_call(
        _attn_t_kernel,
        grid=(BATCH, N_PAIRS),
        in_specs=[
            pl.BlockSpec((ATT_TQ, LANES), lambda b, p: (b, p)),
            pl.BlockSpec((SEQ, LANES), lambda b, p: (b, p)),
            pl.BlockSpec((nk, LANES, ATT_TK), lambda b, p: (b, p, 0)),
            pl.BlockSpec((16, SEQ), lambda b, p: (0, b)),
            pl.BlockSpec((META_ROWS, LANES), lambda b, p: (0, p)),
            pl.BlockSpec((1, LANES, META_ROWS), lambda b, p: (0, p, 0)),
            pl.BlockSpec((16, META_ROWS), lambda b, p: (0, 0)),
            _const_spec(place.shape),
        ],
        out_specs=pl.BlockSpec((ATT_TQ, LANES), lambda b, p: (b, p)),
        out_shape=jax.ShapeDtypeStruct((N_TOK, ATTN_WIDTH), BF16),
        scratch_shapes=[
            pltpu.VMEM((SEQ, LANES), BF16),
            pltpu.VMEM((META_ROWS, LANES), BF16),
            pltpu.VMEM((2, 1, ATT_TQ), F32),
            pltpu.VMEM((2, HEAD_DIM + ACC_ONES, ATT_TQ), F32),
        ],
        compiler_params=pltpu.CompilerParams(
            dimension_semantics=("arbitrary", "arbitrary"), vmem_limit_bytes=VMEM_LIMIT),
        name="attn",
    )(q, k, vt, lf, km, vm, lfm, place)


MERGE_TM = 1024


def _merge_kernel(act_ref, o_ref, sgc_ref, sga_ref, x_ref, wpw_ref, bpw_ref, wao_ref, wout_ref, g2_ref,
                  wrt_ref, brt_ref, tri_ref,
                  h1_ref, hn_ref, route_ref, routet_ref, cnt_ref, base_scr):
    i = pl.program_id(0)

    @pl.when(i == 0)
    def _():
        base_scr[...] = jnp.zeros_like(base_scr)

    tm = x_ref.shape[0]
    hn_parts = []
    for rs in (slice(0, tm // 2), slice(tm // 2, tm)):
        y_conv = _dot(act_ref[rs, :], wpw_ref[...]) + bpw_ref[...]
        y_attn = _dot(o_ref[rs, :], wao_ref[...])
        merged = sgc_ref[rs, :].astype(F32) * y_conv + sga_ref[rs, :].astype(F32) * y_attn
        h1 = x_ref[rs, :] + _dot(merged.astype(BF16), wout_ref[...])
        h1_ref[rs, :] = h1
        ms = jnp.mean(h1 * h1, axis=-1, keepdims=True)
        hn_half = (h1 * lax.rsqrt(ms + RMS_EPS) * g2_ref[...]).astype(BF16)
        hn_ref[rs] = hn_half.reshape(tm // 2, TOK_SUB, LANES)
        hn_parts.append(hn_half)
    hn = jnp.concatenate(hn_parts, axis=0)

    logits = _dot_nt(wrt_ref[...], hn) + brt_ref[...]
    row = lax.broadcasted_iota(jnp.int32, logits.shape, 0).astype(F32)
    big = float(LANES)
    is_g = row < N_GROUPS
    gl = jnp.where(is_g, logits, NEG_INF)
    gmax = jnp.max(gl, axis=0, keepdims=True)
    gidx = jnp.min(jnp.where(gl == gmax, row, big), axis=0, keepdims=True)
    gsum = jnp.sum(jnp.where(is_g, jnp.exp(gl - gmax), 0.0), axis=0, keepdims=True)
    g_w = 1.0 / gsum
    lo = N_GROUPS + EXPERTS_PER_GROUP * gidx
    el = jnp.where((row >= lo) & (row < lo + EXPERTS_PER_GROUP), logits, NEG_INF)
    t1 = jnp.max(el, axis=0, keepdims=True)
    i1 = jnp.min(jnp.where(el == t1, row, big), axis=0, keepdims=True)
    el2 = jnp.where(row == i1, NEG_INF, el)
    t2 = jnp.max(el2, axis=0, keepdims=True)
    i2 = jnp.min(jnp.where(el2 == t2, row, big), axis=0, keepdims=True)
    e2 = jnp.exp(t2 - t1)
    den = 1.0 + e2
    w1 = (1.0 / den) * g_w
    w2 = (e2 / den) * g_w
    ea = i1 - N_GROUPS
    eb = i2 - N_GROUPS

    oha = (row == ea).astype(F32)
    ohb = (row == eb).astype(F32)
    ohs = oha + ohb
    before = _dot(ohs.astype(BF16), tri_ref[...]) + base_scr[...][:, 0:1]
    ra = jnp.sum(before * oha, axis=0, keepdims=True)
    rb = jnp.sum(before * ohb, axis=0, keepdims=True)
    base_scr[...] = base_scr[...] + jnp.sum(ohs, axis=1, keepdims=True)
    cnt_ref[...] = base_scr[...]

    def rows_of(shape):
        r = lax.broadcasted_iota(jnp.int32, shape, 0)
        out = jnp.where(r == 0, ea, 0.0)
        for k, val in enumerate((eb, w1, w2, ra, rb), start=1):
            out = jnp.where(r == k, val, out)
        return out

    routet_ref[...] = rows_of((8, logits.shape[1]))
    route_ref[...] = rows_of(logits.shape).T


def _merge(act, o, sgc, sga, x2d, consts):
    tm = MERGE_TM
    row = lambda w: pl.BlockSpec((tm, w), lambda i: (i, 0))
    return pl.pallas_call(
        _merge_kernel,
        grid=(N_TOK // tm,),
        in_specs=[row(CONV_WIDTH), row(ATTN_WIDTH), row(D_MODEL), row(D_MODEL), row(D_MODEL)]
        + [_const_spec(c.shape) for c in consts],
        out_specs=[row(D_MODEL), pl.BlockSpec((tm, TOK_SUB, LANES), lambda i: (i, 0, 0)), row(LANES),
                   pl.BlockSpec((8, tm), lambda i: (0, i)), _const_spec((LANES, LANES))],
        out_shape=[jax.ShapeDtypeStruct((N_TOK, D_MODEL), F32),
                   jax.ShapeDtypeStruct((N_TOK, TOK_SUB, LANES), BF16),
                   jax.ShapeDtypeStruct((N_TOK, LANES), F32),
                   jax.ShapeDtypeStruct((8, N_TOK), F32),
                   jax.ShapeDtypeStruct((LANES, LANES), F32)],
        scratch_shapes=[pltpu.VMEM((LANES, LANES), F32)],
        compiler_params=pltpu.CompilerParams(dimension_semantics=("arbitrary",), vmem_limit_bytes=VMEM_LIMIT),
        name="merge",
    )(act, o, sgc, sga, x2d, *consts)


DISP_TM = 2048
DMA_UNROLL = 8


def _dest_kernel(pstart_ref, rt_ref, d_ref):
    rt = rt_ref[...]
    start = jnp.zeros_like(rt)
    for e in range(N_EXPERTS):
        start = jnp.where(rt == float(e), pstart_ref[e].astype(F32), start)
    d_ref[...] = (start + pltpu.roll(rt, 4, axis=0)).astype(jnp.int32)


def _dest(pstart, route_t):
    grid_spec = pltpu.PrefetchScalarGridSpec(
        num_scalar_prefetch=1,
        grid=(1,),
        in_specs=[pl.BlockSpec(route_t.shape, lambda i, *_: (0, 0))],
        out_specs=pl.BlockSpec(route_t.shape, lambda i, *_: (0, 0)),
    )
    return pl.pallas_call(
        _dest_kernel,
        grid_spec=grid_spec,
        out_shape=jax.ShapeDtypeStruct(route_t.shape, jnp.int32),
        compiler_params=pltpu.CompilerParams(dimension_semantics=("arbitrary",), vmem_limit_bytes=VMEM_LIMIT),
        name="dest",
    )(pstart, route_t)


ZERO_CHUNK = 64


def _dispatch_kernel(d0_ref, d1_ref, pfill_ref, pend_ref, hn_ref, xin_ref, zero_scr, sem, zsem):
    i = pl.program_id(0)
    base = i * DISP_TM

    @pl.when(i == 0)
    def _():
        zero_scr[...] = jnp.zeros_like(zero_scr)

        def pad_chunks(e, fn):
            first = pfill_ref[e] // ZERO_CHUNK * ZERO_CHUNK

            def one(c, carry):
                start = pl.multiple_of(first + c * ZERO_CHUNK, ZERO_CHUNK)
                fn(pltpu.make_async_copy(zero_scr.at[pl.ds(0, ZERO_CHUNK)], xin_ref.at[pl.ds(start, ZERO_CHUNK)],
                                         zsem.at[0]))
                return carry

            lax.fori_loop(0, (pend_ref[e] - first) // ZERO_CHUNK, one, 0)

        def zstart(e, carry):
            pad_chunks(e, lambda c: c.start())
            return carry

        def zwait(e, carry):
            pad_chunks(e, lambda c: c.wait())
            return carry

        n_used = pend_ref[N_EXPERTS - 1] // ROW_BLOCK

        def tcopy(b):
            return pltpu.make_async_copy(
                zero_scr, xin_ref.at[pl.ds(pl.multiple_of(b * ROW_BLOCK, ROW_BLOCK), ROW_BLOCK)], zsem.at[0])

        def tstart(b, carry):
            tcopy(b).start()
            return carry

        def twait(b, carry):
            tcopy(b).wait()
            return carry

        lax.fori_loop(0, N_EXPERTS, zstart, 0)
        lax.fori_loop(n_used, N_BLOCKS, tstart, 0)
        lax.fori_loop(0, N_EXPERTS, zwait, 0)
        lax.fori_loop(n_used, N_BLOCKS, twait, 0)

    def start(r, carry):
        pltpu.make_async_copy(hn_ref.at[r], xin_ref.at[d0_ref[base + r]], sem.at[0]).start()
        pltpu.make_async_copy(hn_ref.at[r], xin_ref.at[d1_ref[base + r]], sem.at[1]).start(priority=1)
        return carry

    lax.fori_loop(0, DISP_TM, start, 0, unroll=DMA_UNROLL)
    pltpu.make_async_copy(hn_ref, xin_ref.at[pl.ds(0, DISP_TM)], sem.at[0]).wait()
    pltpu.make_async_copy(hn_ref, xin_ref.at[pl.ds(0, DISP_TM)], sem.at[1]).wait()


def _dispatch(d0, d1, pfill, pend, hn):
    grid_spec = pltpu.PrefetchScalarGridSpec(
        num_scalar_prefetch=4,
        grid=(N_TOK // DISP_TM,),
        in_specs=[pl.BlockSpec((DISP_TM, TOK_SUB, LANES), lambda i, *_: (i, 0, 0))],
        out_specs=pl.BlockSpec(memory_space=pl.ANY),
        scratch_shapes=[pltpu.VMEM((ROW_BLOCK, TOK_SUB, LANES), BF16),
                        pltpu.SemaphoreType.DMA((2,)), pltpu.SemaphoreType.DMA((1,))],
    )
    return pl.pallas_call(
        _dispatch_kernel,
        grid_spec=grid_spec,
        out_shape=jax.ShapeDtypeStruct((P_ROWS, TOK_SUB, LANES), BF16),
        compiler_params=pltpu.CompilerParams(dimension_semantics=("arbitrary",), vmem_limit_bytes=VMEM_LIMIT),
        name="dispatch",
    )(d0, d1, pfill, pend, hn)


W_SLOTS = 3


def _expert_kernel(nu_ref, first_ref, ord_ref, used_ref, nue_ref, x_hbm, wg_hbm, wu_hbm, wd_hbm, y_hbm,
                   x_buf, y_buf, wg_f, wu_f, wd_f, wg_b, wu_b, wd_b, xsem, ysem, wsem):
    n = nu_ref[0]

    def rows(i):
        return pl.ds(pl.multiple_of(i * ROW_BLOCK, ROW_BLOCK), ROW_BLOCK)

    def xcopy(i, slot):
        return pltpu.make_async_copy(x_hbm.at[rows(i)], x_buf.at[slot], xsem.at[slot])

    def ycopy(i, slot):
        return pltpu.make_async_copy(y_buf.at[slot], y_hbm.at[rows(i)], ysem.at[slot])

    def fetch(k):
        e = used_ref[k]
        slot = k % W_SLOTS
        return (pltpu.make_async_copy(wg_hbm.at[e], wg_f.at[slot], wsem.at[slot, 0]),
                pltpu.make_async_copy(wu_hbm.at[e], wu_f.at[slot], wsem.at[slot, 1]),
                pltpu.make_async_copy(wd_hbm.at[e], wd_f.at[slot], wsem.at[slot, 2]))

    for k in range(W_SLOTS - 1):
        @pl.when(k < nue_ref[0])
        def _():
            for c in fetch(k):
                c.start()
    xcopy(0, 0).start()

    def block(i, carry):
        slot = i % 2
        xcopy(i, slot).wait()

        @pl.when(i + 1 < n)
        def _():
            xcopy(i + 1, 1 - slot).start()

        @pl.when(first_ref[i] == 1)
        def _():
            k = ord_ref[i]
            for c in fetch(k):
                c.wait()

            @pl.when(k + (W_SLOTS - 1) < nue_ref[0])
            def _():
                for c in fetch(k + (W_SLOTS - 1)):
                    c.start()

            wslot = k % W_SLOTS
            wg_b[...] = wg_f[wslot].astype(BF16)
            wu_b[...] = wu_f[wslot].astype(BF16)
            wd_b[...] = wd_f[wslot].astype(BF16)

        @pl.when(i >= 2)
        def _():
            ycopy(i - 2, slot).wait()

        x = x_buf[slot].reshape(ROW_BLOCK, D_MODEL)
        g = _dot(x, wg_b[...])
        u = _dot(x, wu_b[...])
        hmid = (g * _sigmoid(g) * u).astype(BF16)
        y = _dot(hmid, wd_b[...]).astype(BF16)
        y_buf[slot] = y.reshape(ROW_BLOCK, TOK_SUB, LANES)
        ycopy(i, slot).start()
        return carry

    lax.fori_loop(0, n, block, 0)

    @pl.when(n >= 2)
    def _():
        ycopy(n - 2, n % 2).wait()
    ycopy(n - 1, (n - 1) % 2).wait()

    y_buf[0] = jnp.zeros((ROW_BLOCK, TOK_SUB, LANES), BF16)

    def zcopy(b):
        return pltpu.make_async_copy(y_buf.at[0], y_hbm.at[rows(b)], ysem.at[0])

    def zstart(b, carry):
        zcopy(b).start()
        return carry

    def zwait(b, carry):
        zcopy(b).wait()
        return carry

    lax.fori_loop(n, N_BLOCKS, zstart, 0)
    lax.fori_loop(n, N_BLOCKS, zwait, 0)


def _experts(n_used, blk_first, blk_ord, used_e, n_used_e, xin, w_gate, w_up, w_down):
    grid_spec = pltpu.PrefetchScalarGridSpec(
        num_scalar_prefetch=5,
        grid=(1,),
        in_specs=[pl.BlockSpec(memory_space=pl.ANY),
                  pl.BlockSpec(memory_space=pl.ANY),
                  pl.BlockSpec(memory_space=pl.ANY),
                  pl.BlockSpec(memory_space=pl.ANY)],
        out_specs=pl.BlockSpec(memory_space=pl.ANY),
        scratch_shapes=[pltpu.VMEM((2, ROW_BLOCK, TOK_SUB, LANES), BF16),
                        pltpu.VMEM((2, ROW_BLOCK, TOK_SUB, LANES), BF16),
                        pltpu.VMEM((W_SLOTS, D_MODEL, D_EXPERT), F32),
                        pltpu.VMEM((W_SLOTS, D_MODEL, D_EXPERT), F32),
                        pltpu.VMEM((W_SLOTS, D_EXPERT, D_MODEL), F32),
                        pltpu.VMEM((D_MODEL, D_EXPERT), BF16),
                        pltpu.VMEM((D_MODEL, D_EXPERT), BF16),
                        pltpu.VMEM((D_EXPERT, D_MODEL), BF16),
                        pltpu.SemaphoreType.DMA((2,)), pltpu.SemaphoreType.DMA((2,)),
                        pltpu.SemaphoreType.DMA((W_SLOTS, 3))],
    )
    return pl.pallas_call(
        _expert_kernel,
        grid_spec=grid_spec,
        out_shape=jax.ShapeDtypeStruct((P_ROWS, TOK_SUB, LANES), BF16),
        compiler_params=pltpu.CompilerParams(dimension_semantics=("arbitrary",), vmem_limit_bytes=VMEM_LIMIT),
        name="experts",
    )(n_used, blk_first, blk_ord, used_e, n_used_e, xin, w_gate, w_up, w_down)


COMB_TM = 1024


def _combine_kernel(d0_ref, d1_ref, h1_ref, route_ref, yb_ref, out_ref, g_scr, sem):
    i = pl.program_id(0)
    n = pl.num_programs(0)

    def issue(tile, slot):
        base = tile * COMB_TM

        def start(r, carry):
            pltpu.make_async_copy(yb_ref.at[d0_ref[base + r]], g_scr.at[slot, 0, r], sem.at[slot, 0]).start()
            pltpu.make_async_copy(yb_ref.at[d1_ref[base + r]], g_scr.at[slot, 1, r], sem.at[slot, 1]).start(
                priority=1)
            return carry

        lax.fori_loop(0, COMB_TM, start, 0, unroll=DMA_UNROLL)

    @pl.when(i == 0)
    def _():
        issue(0, 0)

    slot = i % 2

    @pl.when(i + 1 < n)
    def _():
        issue(i + 1, 1 - slot)

    for k in range(2):
        pltpu.make_async_copy(yb_ref.at[pl.ds(0, COMB_TM)], g_scr.at[slot, k], sem.at[slot, k]).wait()

    route = route_ref[...]
    w0 = route[:, 2:3]
    w1 = route[:, 3:4]
    g0 = g_scr[slot, 0].reshape(COMB_TM, D_MODEL).astype(F32)
    g1 = g_scr[slot, 1].reshape(COMB_TM, D_MODEL).astype(F32)
    out_ref[...] = h1_ref[...] + (g0 * w0 + g1 * w1)


def _combine(d0, d1, h1, route, yb):
    grid_spec = pltpu.PrefetchScalarGridSpec(
        num_scalar_prefetch=2,
        grid=(N_TOK // COMB_TM,),
        in_specs=[pl.BlockSpec((COMB_TM, D_MODEL), lambda i, *_: (i, 0)),
                  pl.BlockSpec((COMB_TM, LANES), lambda i, *_: (i, 0)),
                  pl.BlockSpec(memory_space=pl.ANY)],
        out_specs=pl.BlockSpec((COMB_TM, D_MODEL), lambda i, *_: (i, 0)),
        scratch_shapes=[pltpu.VMEM((2, 2, COMB_TM, TOK_SUB, LANES), BF16),
                        pltpu.SemaphoreType.DMA((2, 2))],
    )
    return pl.pallas_call(
        _combine_kernel,
        grid_spec=grid_spec,
        out_shape=jax.ShapeDtypeStruct((N_TOK, D_MODEL), F32),
        compiler_params=pltpu.CompilerParams(dimension_semantics=("arbitrary",), vmem_limit_bytes=VMEM_LIMIT),
        name="combine",
    )(d0, d1, h1, route, yb)


def kernel(x, meta, norm1_g, w_in, b_in, dw_w, dw_b, conv_ln_g, conv_ln_b, w_conv_out, b_conv_out,
           q_norm_g, k_norm_g, b_forget, w_attn_out, w_out, norm2_g, w_group, b_group, w_router,
           b_router, w_gate, w_up, w_down):
    l = 0
    row = lambda v: v.reshape(1, -1).astype(F32)

    c_u, c_q, c_f = 2 * CONV_WIDTH, 2 * CONV_WIDTH, 2 * CONV_WIDTH + 3 * ATTN_WIDTH
    c_gc = c_f + N_HEADS
    c_ga = c_gc + D_MODEL
    wi, bi = w_in[l], b_in[l]
    wu, bu = wi[:, :c_u].astype(BF16), row(bi[:c_u])
    wqkv, bqkv = wi[:, c_q:c_f].astype(BF16), row(bi[c_q:c_f])
    wft = jnp.zeros((16, D_MODEL), F32).at[:N_HEADS].set(wi[:, c_f:c_gc].T).astype(BF16)
    bft = jnp.zeros((16, 1), F32).at[:N_HEADS, 0].set(bi[c_f:c_gc] + b_forget[l])
    wgc, bgc = wi[:, c_gc:c_ga].astype(BF16), row(bi[c_gc:c_ga])
    wga, bga = wi[:, c_ga:].astype(BF16), row(bi[c_ga:])
    qg = row(jnp.tile(q_norm_g[l], N_HEADS) * (HEAD_DIM ** -0.5 * LOG2E))
    kg = row(jnp.tile(k_norm_g[l], N_HEADS))
    head_of_col = jnp.arange(ATTN_WIDTH) // HEAD_DIM
    hsum = (head_of_col[:, None] == jnp.arange(LANES)[None, :]).astype(BF16)
    hexp_half = (jnp.arange(LANES)[:, None] == head_of_col[None, :]).astype(BF16)
    hexp = jnp.concatenate([hexp_half, hexp_half], axis=0)
    proj_consts = (row(norm1_g[l]), wu, bu, wqkv, bqkv, wft, bft, wgc, bgc, wga, bga, qg, kg, hsum, hexp)

    x2d = x.reshape(N_TOK, D_MODEL)
    meta_pad = jnp.zeros((META_ROWS, D_MODEL), F32).at[:N_META].set(meta.astype(F32))
    a_m, _, k_m, v_m, lf_m, _, _ = _proj(meta_pad, proj_consts, META_ROWS)
    conv_consts = (a_m, _conv_shift_matrix(), dw_w[l].astype(F32), row(dw_b[l]),
                   row(conv_ln_g[l]), row(conv_ln_b[l]))
    act, q, k, v, lf, sgc, sga = _proj(x2d, proj_consts, PROJ_TM, conv_consts)

    o = _attention_t(q, k, v, lf, k_m, v_m, lf_m)

    wrt = jnp.zeros((LANES, D_MODEL), F32).at[:N_GROUPS].set(w_group[l].T)
    wrt = wrt.at[N_GROUPS:N_GROUPS + N_EXPERTS].set(w_router[l].T).astype(BF16)
    brt = jnp.zeros((LANES, 1), F32).at[:N_GROUPS, 0].set(b_group[l])
    brt = brt.at[N_GROUPS:N_GROUPS + N_EXPERTS, 0].set(b_router[l])
    tri = jnp.triu(jnp.ones((MERGE_TM, MERGE_TM), F32), 1).astype(BF16)
    merge_consts = (w_conv_out[l].astype(BF16), row(b_conv_out[l]), w_attn_out[l].astype(BF16),
                    w_out[l].astype(BF16), row(norm2_g[l]), wrt, brt, tri)
    h1, hn, route, route_t, cnt = _merge(act, o, sgc, sga, x2d, merge_consts)

    counts = cnt[:N_EXPERTS, 0].astype(jnp.int32)
    padded = (counts + ROW_BLOCK - 1) // ROW_BLOCK * ROW_BLOCK
    pend = jnp.cumsum(padded).astype(jnp.int32)
    pstart = pend - padded
    blk_row = jnp.arange(N_BLOCKS, dtype=jnp.int32) * ROW_BLOCK
    n_used = pend[-1:] // ROW_BLOCK
    nz = counts > 0
    before = (pend[None, :] <= blk_row[:, None]) & nz[None, :]
    blk_ord = jnp.sum(before.astype(jnp.int32), axis=1)
    blk_first = jnp.any((pstart[None, :] == blk_row[:, None]) & nz[None, :], axis=1).astype(jnp.int32)
    e_ord = jnp.cumsum(nz.astype(jnp.int32)) - 1
    e_ids = jnp.arange(N_EXPERTS, dtype=jnp.int32)
    used_e = jnp.sum(jnp.where((e_ord[None, :] == e_ids[:, None]) & nz[None, :], e_ids[None, :], 0), axis=1)
    n_used_e = jnp.sum(nz.astype(jnp.int32)).reshape(1)

    dest = _dest(pstart, route_t)
    d0, d1 = dest[0], dest[1]
    xin = _dispatch(d0, d1, pstart + counts, pend, hn)
    yb = _experts(n_used, blk_first, blk_ord, used_e.astype(jnp.int32), n_used_e, xin,
                  w_gate[l], w_up[l], w_down[l])
    out = _combine(d0, d1, h1, route, yb)
    return out.reshape(BATCH, SEQ, D_MODEL)
```

```python
import jax
import jax.numpy as jnp
from jax import lax
from jax.experimental import pallas as pl
from jax.experimental.pallas import tpu as pltpu

D_MODEL = 1024
BATCH = 8
SEQ = 2048
N_META = 16
CONV_WIDTH = 512
CONV_K = 31
N_HEADS = 8
HEAD_DIM = 64
ATTN_WIDTH = N_HEADS * HEAD_DIM
N_GROUPS = 8
EXPERTS_PER_GROUP = 8
N_EXPERTS = N_GROUPS * EXPERTS_PER_GROUP
D_EXPERT = 256
RMS_EPS = 1e-6
LN_EPS = 1e-5

N_TOK = BATCH * SEQ
LANES = 128
META_ROWS = 128
ROW_BLOCK = 512
N_ASSIGN = 2 * N_TOK
N_BLOCKS = N_ASSIGN // ROW_BLOCK + N_EXPERTS
P_ROWS = N_BLOCKS * ROW_BLOCK
TOK_SUB = D_MODEL // LANES
PROJ_TM = 1024
CONV_CHUNK = 128
CONV_PAD = 32
CONV_WIN = CONV_CHUNK + 32
CONV_SHROWS = CONV_CHUNK + 24
CONV_COPIES = 7
CONV_LANES = 128
ATT_TQ = SEQ
ATT_TK = 256
N_PAIRS = N_HEADS // 2
VMEM_LIMIT = 56 * 1024 * 1024

F32 = jnp.float32
BF16 = jnp.bfloat16
NEG_INF = float("-inf")
LOG2E = 1.4426950408889634


def _dot(a, b):
    return jnp.dot(a, b, preferred_element_type=F32)


def _dot_nt(a, b):
    return lax.dot_general(a, b, (((1,), (1,)), ((), ())), preferred_element_type=F32)


def _sigmoid(x):
    return 1.0 / (1.0 + jnp.exp(-x))


def _const_spec(shape):
    nd = len(shape)
    return pl.BlockSpec(shape, lambda *_: (0,) * nd)


def _head_rms(t, hsum_ref, hexp_ref, gain):
    ss = _dot((t * t).astype(BF16), hsum_ref[...])
    inv = lax.rsqrt(ss * (1.0 / HEAD_DIM) + RMS_EPS)
    hi = inv.astype(BF16)
    lo = (inv - hi.astype(F32)).astype(BF16)
    invb = _dot(jnp.concatenate([hi, lo], axis=1), hexp_ref[...])
    return t * invb * gain


N_PROJ_IN = 16


def _proj_stages(x_ref, g1_ref, wu_ref, bu_ref, wqkv_ref, bqkv_ref, wft_ref, bft_ref, wgc_ref, bgc_ref,
                 wga_ref, bga_ref, qg_ref, kg_ref, hsum_ref, hexp_ref,
                 q_ref, k_ref, v_ref, lf_ref, sgc_ref, sga_ref):
    x = x_ref[...]
    ms = jnp.mean(x * x, axis=-1, keepdims=True)
    xn = (x * lax.rsqrt(ms + RMS_EPS) * g1_ref[...]).astype(BF16)

    def glu():
        u = _dot(xn, wu_ref[...]) + bu_ref[...]
        return (u[:, :CONV_WIDTH] * _sigmoid(u[:, CONV_WIDTH:])).astype(BF16)

    def qkv_stage():
        qkv = _dot(xn, wqkv_ref[...]) + bqkv_ref[...]
        q_ref[...] = _head_rms(qkv[:, :ATTN_WIDTH], hsum_ref, hexp_ref, qg_ref[...]).astype(BF16)
        k_ref[...] = _head_rms(qkv[:, ATTN_WIDTH:2 * ATTN_WIDTH], hsum_ref, hexp_ref, kg_ref[...]).astype(BF16)
        v = qkv[:, 2 * ATTN_WIDTH:]
        vb = v_ref.shape[2]
        for c in range(v_ref.shape[0]):
            v_ref[c] = v[c * vb:(c + 1) * vb, :].T.astype(BF16)

    def gate_stage_c():
        f = _dot_nt(wft_ref[...], xn) + bft_ref[...]
        lf_ref[...] = jnp.minimum(f, 0.0) - jnp.log(1.0 + jnp.exp(-jnp.abs(f)))
        sgc_ref[...] = _sigmoid(_dot(xn, wgc_ref[...]) + bgc_ref[...]).astype(BF16)

    def gate_stage_a():
        sga_ref[...] = _sigmoid(_dot(xn, wga_ref[...]) + bga_ref[...]).astype(BF16)

    return glu, (qkv_stage, gate_stage_c, gate_stage_a)


def _proj_kernel(*refs):
    ins, a_ref, outs = refs[:N_PROJ_IN], refs[N_PROJ_IN], refs[N_PROJ_IN + 1:]
    glu, stages = _proj_stages(*ins, *outs)
    a_ref[...] = glu()
    for stage in stages:
        stage()


def _proj_conv_kernel(*refs):
    ins = refs[:N_PROJ_IN]
    am_ref, shift_ref, dw_ref, dwb_ref, lng_ref, lnb_ref = refs[N_PROJ_IN:N_PROJ_IN + 6]
    act_ref, outs = refs[N_PROJ_IN + 6], refs[N_PROJ_IN + 7:N_PROJ_IN + 13]
    win_scr, sh_scr = refs[N_PROJ_IN + 13:]
    tm = act_ref.shape[0]
    i = pl.program_id(0)
    first = (i % (SEQ // tm)) == 0

    @pl.when(first)
    def _():
        win_scr[0:N_META, :] = jnp.zeros((N_META, CONV_WIDTH), BF16)
        win_scr[N_META:CONV_PAD, :] = am_ref[0:N_META, :]

    @pl.when(jnp.logical_not(first))
    def _():
        win_scr[0:CONV_PAD, :] = win_scr[tm:tm + CONV_PAD, :]

    glu, stages = _proj_stages(*ins, *outs)
    win_scr[CONV_PAD:CONV_PAD + tm, :] = glu()

    def conv_chunk(ci):
        r0 = ci * CONV_CHUNK
        slot = ci % 2
        sh_scr[slot] = _dot(shift_ref[...], win_scr[r0:r0 + CONV_WIN, :])
        parts = []
        for c0 in range(0, CONV_WIDTH, CONV_LANES):
            cs = slice(c0, c0 + CONV_LANES)
            part = jnp.zeros((CONV_CHUNK, CONV_LANES), F32) + dwb_ref[:, cs]
            for b in range(8):
                if b == 0:
                    rows = win_scr[r0:r0 + CONV_WIN, cs].astype(F32)
                else:
                    rows = sh_scr[slot, (b - 1) * CONV_SHROWS:b * CONV_SHROWS, cs]
                for j in range(CONV_K):
                    off = j + 2
                    if off % 8 == b:
                        part = part + rows[off - b:off - b + CONV_CHUNK] * dw_ref[j:j + 1, cs]
            parts.append(part)
        acc = jnp.concatenate(parts, axis=1)
        mu = jnp.mean(acc, axis=-1, keepdims=True)
        d = acc - mu
        var = jnp.mean(d * d, axis=-1, keepdims=True)
        y = d * lax.rsqrt(var + LN_EPS) * lng_ref[...] + lnb_ref[...]
        act_ref[r0:r0 + CONV_CHUNK, :] = (y * _sigmoid(y)).astype(BF16)

    n_chunks = tm // CONV_CHUNK
    per_stage = -(-n_chunks // (len(stages) + 1))
    ci = 0
    for stage in (None,) + stages:
        if stage is not None:
            stage()
        for _ in range(per_stage):
            if ci < n_chunks:
                conv_chunk(ci)
                ci += 1


def _proj(x2d, consts, tm, conv_consts=None):
    n = x2d.shape[0]
    row = lambda w: pl.BlockSpec((tm, w), lambda i: (i, 0))
    in_specs = [row(D_MODEL)] + [_const_spec(c.shape) for c in consts]
    fused = conv_consts is not None
    extra = tuple(conv_consts) if fused else ()
    in_specs += [_const_spec(c.shape) for c in extra]
    scratch = [pltpu.VMEM((CONV_PAD + tm, CONV_WIDTH), BF16),
               pltpu.VMEM((2, CONV_COPIES * CONV_SHROWS, CONV_WIDTH), F32)] if fused else []
    vb = min(tm, ATT_TK)
    out_shape = [
        jax.ShapeDtypeStruct((n, CONV_WIDTH), BF16),
        jax.ShapeDtypeStruct((n, ATTN_WIDTH), BF16),
        jax.ShapeDtypeStruct((n, ATTN_WIDTH), BF16),
        jax.ShapeDtypeStruct((n // vb, ATTN_WIDTH, vb), BF16),
        jax.ShapeDtypeStruct((16, n), F32),
        jax.ShapeDtypeStruct((n, D_MODEL), BF16),
        jax.ShapeDtypeStruct((n, D_MODEL), BF16),
    ]
    out_specs = [row(CONV_WIDTH), row(ATTN_WIDTH), row(ATTN_WIDTH),
                 pl.BlockSpec((tm // vb, ATTN_WIDTH, vb), lambda i: (i, 0, 0)),
                 pl.BlockSpec((16, tm), lambda i: (0, i)), row(D_MODEL), row(D_MODEL)]
    return pl.pallas_call(
        _proj_conv_kernel if fused else _proj_kernel,
        grid=(n // tm,),
        in_specs=in_specs,
        out_specs=out_specs,
        out_shape=out_shape,
        scratch_shapes=scratch,
        compiler_params=pltpu.CompilerParams(dimension_semantics=("arbitrary",), vmem_limit_bytes=VMEM_LIMIT),
        name="proj_conv" if fused else "proj",
    )(x2d, *consts, *extra)


def _conv_shift_matrix():
    rr = jnp.arange(CONV_COPIES * CONV_SHROWS)
    src = rr % CONV_SHROWS + rr // CONV_SHROWS + 1
    return (src[:, None] == jnp.arange(CONV_WIN)[None, :]).astype(BF16)


AUG_TERMS = 3
ACC_ONES = 16
SCORE_LEAD = 8


def _split_terms(x):
    hi = x.astype(BF16)
    r1 = x - hi.astype(F32)
    mid = r1.astype(BF16)
    lo = (r1 - mid.astype(F32)).astype(BF16)
    return hi, mid, lo


def _attn_t_kernel(q_ref, k_ref, vt_ref, lf_ref, km_ref, vm_ref, lfm_ref, place_ref, o_ref,
                   aug_scr, augm_scr, m_scr, acc_scr):
    p = pl.program_id(1)
    nk = SEQ // ATT_TK

    def bias_lanes(colsum):
        out = None
        for t, term in enumerate(_split_terms(colsum)):
            d = _dot(term, place_ref[t])
            out = d if out is None else out + d
        return out.astype(BF16)

    @pl.when(p == 0)
    def _():
        r = lax.broadcasted_iota(jnp.int32, (ATT_TK, ATT_TK), 0)
        c = lax.broadcasted_iota(jnp.int32, (ATT_TK, ATT_TK), 1)
        tri = (c <= r).astype(BF16)
        local = []
        for j in range(nk):
            lf = lf_ref[:, j * ATT_TK:(j + 1) * ATT_TK]
            csum = None
            for term in _split_terms(lf):
                d = _dot_nt(tri, term)
                csum = d if csum is None else csum + d
            local.append(csum)
        carry = jnp.zeros((1, 16), F32)
        for j in range(nk):
            csum = local[j] + carry
            aug_scr[j * ATT_TK:(j + 1) * ATT_TK, :] = bias_lanes(csum * (-LOG2E))
            carry = csum[ATT_TK - 1:ATT_TK, :]
        rm = lax.broadcasted_iota(jnp.int32, (META_ROWS, META_ROWS), 0)
        cm = lax.broadcasted_iota(jnp.int32, (META_ROWS, META_ROWS), 1)
        trim = ((cm > rm) & (cm < N_META)).astype(BF16)
        msum = jnp.zeros((META_ROWS, 16), F32)
        for term in _split_terms(lfm_ref[...]):
            msum = msum + _dot_nt(trim, term)
        augm_scr[...] = bias_lanes(msum * LOG2E)

    lane = lax.broadcasted_iota(jnp.int32, (ATT_TQ, LANES), 1)
    q_all = q_ref[...]
    zq = jnp.zeros_like(q_all)
    q_cat = []
    for hl in range(2):
        own = (lane < HEAD_DIM) if hl == 0 else (lane >= HEAD_DIM)
        h = 2 * p + hl
        bias_sel = (lane >= AUG_TERMS * h) & (lane < AUG_TERMS * h + AUG_TERMS)
        bias_ones = jnp.where(bias_sel, 1.0, 0.0).astype(BF16)
        q_cat.append(jnp.concatenate([jnp.where(own, q_all, zq), bias_ones], axis=1))
        m_scr[hl] = jnp.full((1, ATT_TQ), NEG_INF, F32)
        acc_scr[hl] = jnp.zeros((HEAD_DIM + ACC_ONES, ATT_TQ), F32)

    def scores(kb, augb, lo=0):
        k_cat = jnp.concatenate([kb, augb], axis=1)
        return tuple(_dot_nt(k_cat, q_cat[hl][lo:]) for hl in range(2))

    def update(s2, vtb, lo=0):
        ones = jnp.ones((ACC_ONES, vtb.shape[1]), BF16)
        for hl in range(2):
            s = s2[hl]
            m_prev = m_scr[hl, :, lo:]
            m_next = jnp.maximum(m_prev, jnp.max(s, axis=0, keepdims=True))
            pr = jnp.exp2(s - m_next).astype(BF16)
            alpha = jnp.exp2(m_prev - m_next)
            v_own = jnp.concatenate([vtb[hl * HEAD_DIM:(hl + 1) * HEAD_DIM], ones], axis=0)
            acc_scr[hl, :, lo:] = acc_scr[hl, :, lo:] * alpha + _dot(v_own, pr)
            m_scr[hl, :, lo:] = m_next

    def block_scores(j):
        rs = slice(j * ATT_TK, (j + 1) * ATT_TK)
        return scores(k_ref[rs, :], aug_scr[rs, :], lo=j * ATT_TK)

    s_meta = scores(km_ref[0:N_META, :], augm_scr[0:N_META, :])
    s_blocks = [block_scores(j) for j in range(min(SCORE_LEAD, nk))]
    update(s_meta, vm_ref[0][:, 0:N_META])

    srow = lax.broadcasted_iota(jnp.int32, (ATT_TK, ATT_TK), 0)
    tcol = lax.broadcasted_iota(jnp.int32, (ATT_TK, ATT_TK), 1)
    tri = srow <= tcol
    for j in range(nk):
        if j + SCORE_LEAD < nk:
            s_blocks.append(block_scores(j + SCORE_LEAD))
        s_cur = tuple(jnp.concatenate([jnp.where(tri, s[:, :ATT_TK], NEG_INF), s[:, ATT_TK:]], axis=1)
                      if s.shape[1] > ATT_TK else jnp.where(tri, s, NEG_INF) for s in s_blocks[j])
        update(s_cur, vt_ref[j], lo=j * ATT_TK)

    a0 = acc_scr[0]
    a1 = acc_scr[1]
    ot = jnp.concatenate([a0[0:HEAD_DIM] / a0[HEAD_DIM:HEAD_DIM + 1], a1[0:HEAD_DIM] / a1[HEAD_DIM:HEAD_DIM + 1]],
                         axis=0)
    o_ref[...] = ot.T.astype(BF16)


def _attention_t(q, k, vt, lf, km, vm, lfm):
    nk = SEQ // ATT_TK
    hh = jnp.arange(16)[:, None]
    ll = jnp.arange(LANES)[None, :]
    place = jnp.stack([((ll == AUG_TERMS * hh + t) & (hh < N_HEADS)).astype(BF16) for t in range(AUG_TERMS)])
    return pl.pallas_call(
        _attn_t_kernel,
        grid=(BATCH, N_PAIRS),
        in_specs=[
            pl.BlockSpec((ATT_TQ, LANES), lambda b, p: (b, p)),
            pl.BlockSpec((SEQ, LANES), lambda b, p: (b, p)),
            pl.BlockSpec((nk, LANES, ATT_TK), lambda b, p: (b, p, 0)),
            pl.BlockSpec((16, SEQ), lambda b, p: (0, b)),
            pl.BlockSpec((META_ROWS, LANES), lambda b, p: (0, p)),
            pl.BlockSpec((1, LANES, META_ROWS), lambda b, p: (0, p, 0)),
            pl.BlockSpec((16, META_ROWS), lambda b, p: (0, 0)),
            _const_spec(place.shape),
        ],
        out_specs=pl.BlockSpec((ATT_TQ, LANES), lambda b, p: (b, p)),
        out_shape=jax.ShapeDtypeStruct((N_TOK, ATTN_WIDTH), BF16),
        scratch_shapes=[
            pltpu.VMEM((SEQ, LANES), BF16),
            pltpu.VMEM((META_ROWS, LANES), BF16),
            pltpu.VMEM((2, 1, ATT_TQ), F32),
            pltpu.VMEM((2, HEAD_DIM + ACC_ONES, ATT_TQ), F32),
        ],
        compiler_params=pltpu.CompilerParams(
            dimension_semantics=("arbitrary", "arbitrary"), vmem_limit_bytes=VMEM_LIMIT),
        name="attn",
    )(q, k, vt, lf, km, vm, lfm, place)


MERGE_TM = 1024


def _merge_kernel(act_ref, o_ref, sgc_ref, sga_ref, x_ref, wpw_ref, bpw_ref, wao_ref, wout_ref, g2_ref,
                  wrt_ref, brt_ref, tri_ref,
                  h1_ref, hn_ref, route_ref, routet_ref, cnt_ref, base_scr):
    i = pl.program_id(0)

    @pl.when(i == 0)
    def _():
        base_scr[...] = jnp.zeros_like(base_scr)

    tm = x_ref.shape[0]
    hn_parts = []
    for rs in (slice(0, tm // 2), slice(tm // 2, tm)):
        y_conv = _dot(act_ref[rs, :], wpw_ref[...]) + bpw_ref[...]
        y_attn = _dot(o_ref[rs, :], wao_ref[...])
        merged = sgc_ref[rs, :].astype(F32) * y_conv + sga_ref[rs, :].astype(F32) * y_attn
        h1 = x_ref[rs, :] + _dot(merged.astype(BF16), wout_ref[...])
        h1_ref[rs, :] = h1
        ms = jnp.mean(h1 * h1, axis=-1, keepdims=True)
        hn_half = (h1 * lax.rsqrt(ms + RMS_EPS) * g2_ref[...]).astype(BF16)
        hn_ref[rs] = hn_half.reshape(tm // 2, TOK_SUB, LANES)
        hn_parts.append(hn_half)
    hn = jnp.concatenate(hn_parts, axis=0)

    logits = _dot_nt(wrt_ref[...], hn) + brt_ref[...]
    row = lax.broadcasted_iota(jnp.int32, logits.shape, 0).astype(F32)
    big = float(LANES)
    is_g = row < N_GROUPS
    gl = jnp.where(is_g, logits, NEG_INF)
    gmax = jnp.max(gl, axis=0, keepdims=True)
    gidx = jnp.min(jnp.where(gl == gmax, row, big), axis=0, keepdims=True)
    gsum = jnp.sum(jnp.where(is_g, jnp.exp(gl - gmax), 0.0), axis=0, keepdims=True)
    g_w = 1.0 / gsum
    lo = N_GROUPS + EXPERTS_PER_GROUP * gidx
    el = jnp.where((row >= lo) & (row < lo + EXPERTS_PER_GROUP), logits, NEG_INF)
    t1 = jnp.max(el, axis=0, keepdims=True)
    i1 = jnp.min(jnp.where(el == t1, row, big), axis=0, keepdims=True)
    el2 = jnp.where(row == i1, NEG_INF, el)
    t2 = jnp.max(el2, axis=0, keepdims=True)
    i2 = jnp.min(jnp.where(el2 == t2, row, big), axis=0, keepdims=True)
    e2 = jnp.exp(t2 - t1)
    den = 1.0 + e2
    w1 = (1.0 / den) * g_w
    w2 = (e2 / den) * g_w
    ea = i1 - N_GROUPS
    eb = i2 - N_GROUPS

    oha = (row == ea).astype(F32)
    ohb = (row == eb).astype(F32)
    ohs = oha + ohb
    before = _dot(ohs.astype(BF16), tri_ref[...]) + base_scr[...][:, 0:1]
    ra = jnp.sum(before * oha, axis=0, keepdims=True)
    rb = jnp.sum(before * ohb, axis=0, keepdims=True)
    base_scr[...] = base_scr[...] + jnp.sum(ohs, axis=1, keepdims=True)
    cnt_ref[...] = base_scr[...]

    def rows_of(shape):
        r = lax.broadcasted_iota(jnp.int32, shape, 0)
        out = jnp.where(r == 0, ea, 0.0)
        for k, val in enumerate((eb, w1, w2, ra, rb), start=1):
            out = jnp.where(r == k, val, out)
        return out

    routet_ref[...] = rows_of((8, logits.shape[1]))
    route_ref[...] = rows_of(logits.shape).T


def _merge(act, o, sgc, sga, x2d, consts):
    tm = MERGE_TM
    row = lambda w: pl.BlockSpec((tm, w), lambda i: (i, 0))
    return pl.pallas_call(
        _merge_kernel,
        grid=(N_TOK // tm,),
        in_specs=[row(CONV_WIDTH), row(ATTN_WIDTH), row(D_MODEL), row(D_MODEL), row(D_MODEL)]
        + [_const_spec(c.shape) for c in consts],
        out_specs=[row(D_MODEL), pl.BlockSpec((tm, TOK_SUB, LANES), lambda i: (i, 0, 0)), row(LANES),
                   pl.BlockSpec((8, tm), lambda i: (0, i)), _const_spec((LANES, LANES))],
        out_shape=[jax.ShapeDtypeStruct((N_TOK, D_MODEL), F32),
                   jax.ShapeDtypeStruct((N_TOK, TOK_SUB, LANES), BF16),
                   jax.ShapeDtypeStruct((N_TOK, LANES), F32),
                   jax.ShapeDtypeStruct((8, N_TOK), F32),
                   jax.ShapeDtypeStruct((LANES, LANES), F32)],
        scratch_shapes=[pltpu.VMEM((LANES, LANES), F32)],
        compiler_params=pltpu.CompilerParams(dimension_semantics=("arbitrary",), vmem_limit_bytes=VMEM_LIMIT),
        name="merge",
    )(act, o, sgc, sga, x2d, *consts)


DISP_TM = 2048
DMA_UNROLL = 8


def _dest_kernel(pstart_ref, rt_ref, d_ref):
    rt = rt_ref[...]
    start = jnp.zeros_like(rt)
    for e in range(N_EXPERTS):
        start = jnp.where(rt == float(e), pstart_ref[e].astype(F32), start)
    d_ref[...] = (start + pltpu.roll(rt, 4, axis=0)).astype(jnp.int32)


def _dest(pstart, route_t):
    grid_spec = pltpu.PrefetchScalarGridSpec(
        num_scalar_prefetch=1,
        grid=(1,),
        in_specs=[pl.BlockSpec(route_t.shape, lambda i, *_: (0, 0))],
        out_specs=pl.BlockSpec(route_t.shape, lambda i, *_: (0, 0)),
    )
    return pl.pallas_call(
        _dest_kernel,
        grid_spec=grid_spec,
        out_shape=jax.ShapeDtypeStruct(route_t.shape, jnp.int32),
        compiler_params=pltpu.CompilerParams(dimension_semantics=("arbitrary",), vmem_limit_bytes=VMEM_LIMIT),
        name="dest",
    )(pstart, route_t)


ZERO_CHUNK = 64


def _dispatch_kernel(d0_ref, d1_ref, pfill_ref, pend_ref, hn_ref, xin_ref, zero_scr, sem, zsem):
    i = pl.program_id(0)
    base = i * DISP_TM

    @pl.when(i == 0)
    def _():
        zero_scr[...] = jnp.zeros_like(zero_scr)

        def pad_chunks(e, fn):
            first = pfill_ref[e] // ZERO_CHUNK * ZERO_CHUNK

            def one(c, carry):
                start = pl.multiple_of(first + c * ZERO_CHUNK, ZERO_CHUNK)
                fn(pltpu.make_async_copy(zero_scr.at[pl.ds(0, ZERO_CHUNK)], xin_ref.at[pl.ds(start, ZERO_CHUNK)],
                                         zsem.at[0]))
                return carry

            lax.fori_loop(0, (pend_ref[e] - first) // ZERO_CHUNK, one, 0)

        def zstart(e, carry):
            pad_chunks(e, lambda c: c.start())
            return carry

        def zwait(e, carry):
            pad_chunks(e, lambda c: c.wait())
            return carry

        n_used = pend_ref[N_EXPERTS - 1] // ROW_BLOCK

        def tcopy(b):
            return pltpu.make_async_copy(
                zero_scr, xin_ref.at[pl.ds(pl.multiple_of(b * ROW_BLOCK, ROW_BLOCK), ROW_BLOCK)], zsem.at[0])

        def tstart(b, carry):
            tcopy(b).start()
            return carry

        def twait(b, carry):
            tcopy(b).wait()
            return carry

        lax.fori_loop(0, N_EXPERTS, zstart, 0)
        lax.fori_loop(n_used, N_BLOCKS, tstart, 0)
        lax.fori_loop(0, N_EXPERTS, zwait, 0)
        lax.fori_loop(n_used, N_BLOCKS, twait, 0)

    def start(r, carry):
        pltpu.make_async_copy(hn_ref.at[r], xin_ref.at[d0_ref[base + r]], sem.at[0]).start()
        pltpu.make_async_copy(hn_ref.at[r], xin_ref.at[d1_ref[base + r]], sem.at[1]).start(priority=1)
        return carry

    lax.fori_loop(0, DISP_TM, start, 0, unroll=DMA_UNROLL)
    pltpu.make_async_copy(hn_ref, xin_ref.at[pl.ds(0, DISP_TM)], sem.at[0]).wait()
    pltpu.make_async_copy(hn_ref, xin_ref.at[pl.ds(0, DISP_TM)], sem.at[1]).wait()


def _dispatch(d0, d1, pfill, pend, hn):
    grid_spec = pltpu.PrefetchScalarGridSpec(
        num_scalar_prefetch=4,
        grid=(N_TOK // DISP_TM,),
        in_specs=[pl.BlockSpec((DISP_TM, TOK_SUB, LANES), lambda i, *_: (i, 0, 0))],
        out_specs=pl.BlockSpec(memory_space=pl.ANY),
        scratch_shapes=[pltpu.VMEM((ROW_BLOCK, TOK_SUB, LANES), BF16),
                        pltpu.SemaphoreType.DMA((2,)), pltpu.SemaphoreType.DMA((1,))],
    )
    return pl.pallas_call(
        _dispatch_kernel,
        grid_spec=grid_spec,
        out_shape=jax.ShapeDtypeStruct((P_ROWS, TOK_SUB, LANES), BF16),
        compiler_params=pltpu.CompilerParams(dimension_semantics=("arbitrary",), vmem_limit_bytes=VMEM_LIMIT),
        name="dispatch",
    )(d0, d1, pfill, pend, hn)


W_SLOTS = 4


def _expert_kernel(nu_ref, first_ref, ord_ref, used_ref, nue_ref, x_hbm, wg_hbm, wu_hbm, wd_hbm, y_hbm,
                   x_buf, y_buf, wg_f, wu_f, wd_f, wg_b, wu_b, wd_b, xsem, ysem, wsem):
    n = nu_ref[0]

    def rows(i):
        return pl.ds(pl.multiple_of(i * ROW_BLOCK, ROW_BLOCK), ROW_BLOCK)

    def xcopy(i, slot):
        return pltpu.make_async_copy(x_hbm.at[rows(i)], x_buf.at[slot], xsem.at[slot])

    def ycopy(i, slot):
        return pltpu.make_async_copy(y_buf.at[slot], y_hbm.at[rows(i)], ysem.at[slot])

    def fetch(k):
        e = used_ref[k]
        slot = k % W_SLOTS
        return (pltpu.make_async_copy(wg_hbm.at[e], wg_f.at[slot], wsem.at[slot, 0]),
                pltpu.make_async_copy(wu_hbm.at[e], wu_f.at[slot], wsem.at[slot, 1]),
                pltpu.make_async_copy(wd_hbm.at[e], wd_f.at[slot], wsem.at[slot, 2]))

    for k in range(W_SLOTS - 1):
        @pl.when(k < nue_ref[0])
        def _():
            for c in fetch(k):
                c.start()
    xcopy(0, 0).start()

    def block(i, carry):
        slot = i % 2
        xcopy(i, slot).wait()

        @pl.when(i + 1 < n)
        def _():
            xcopy(i + 1, 1 - slot).start()

        @pl.when(first_ref[i] == 1)
        def _():
            k = ord_ref[i]
            for c in fetch(k):
                c.wait()

            @pl.when(k + (W_SLOTS - 1) < nue_ref[0])
            def _():
                for c in fetch(k + (W_SLOTS - 1)):
                    c.start()

            wslot = k % W_SLOTS
            wg_b[...] = wg_f[wslot].astype(BF16)
            wu_b[...] = wu_f[wslot].astype(BF16)
            wd_b[...] = wd_f[wslot].astype(BF16)

        @pl.when(i >= 2)
        def _():
            ycopy(i - 2, slot).wait()

        x = x_buf[slot].reshape(ROW_BLOCK, D_MODEL)
        g = _dot(x, wg_b[...])
        u = _dot(x, wu_b[...])
        hmid = (g * _sigmoid(g) * u).astype(BF16)
        y = _dot(hmid, wd_b[...]).astype(BF16)
        y_buf[slot] = y.reshape(ROW_BLOCK, TOK_SUB, LANES)
        ycopy(i, slot).start()
        return carry

    lax.fori_loop(0, n, block, 0)

    @pl.when(n >= 2)
    def _():
        ycopy(n - 2, n % 2).wait()
    ycopy(n - 1, (n - 1) % 2).wait()

    y_buf[0] = jnp.zeros((ROW_BLOCK, TOK_SUB, LANES), BF16)

    def zcopy(b):
        return pltpu.make_async_copy(y_buf.at[0], y_hbm.at[rows(b)], ysem.at[0])

    def zstart(b, carry):
        zcopy(b).start()
        return carry

    def zwait(b, carry):
        zcopy(b).wait()
        return carry

    lax.fori_loop(n, N_BLOCKS, zstart, 0)
    lax.fori_loop(n, N_BLOCKS, zwait, 0)


def _experts(n_used, blk_first, blk_ord, used_e, n_used_e, xin, w_gate, w_up, w_down):
    grid_spec = pltpu.PrefetchScalarGridSpec(
        num_scalar_prefetch=5,
        grid=(1,),
        in_specs=[pl.BlockSpec(memory_space=pl.ANY),
                  pl.BlockSpec(memory_space=pl.ANY),
                  pl.BlockSpec(memory_space=pl.ANY),
                  pl.BlockSpec(memory_space=pl.ANY)],
        out_specs=pl.BlockSpec(memory_space=pl.ANY),
        scratch_shapes=[pltpu.VMEM((2, ROW_BLOCK, TOK_SUB, LANES), BF16),
                        pltpu.VMEM((2, ROW_BLOCK, TOK_SUB, LANES), BF16),
                        pltpu.VMEM((W_SLOTS, D_MODEL, D_EXPERT), F32),
                        pltpu.VMEM((W_SLOTS, D_MODEL, D_EXPERT), F32),
                        pltpu.VMEM((W_SLOTS, D_EXPERT, D_MODEL), F32),
                        pltpu.VMEM((D_MODEL, D_EXPERT), BF16),
                        pltpu.VMEM((D_MODEL, D_EXPERT), BF16),
                        pltpu.VMEM((D_EXPERT, D_MODEL), BF16),
                        pltpu.SemaphoreType.DMA((2,)), pltpu.SemaphoreType.DMA((2,)),
                        pltpu.SemaphoreType.DMA((W_SLOTS, 3))],
    )
    return pl.pallas_call(
        _expert_kernel,
        grid_spec=grid_spec,
        out_shape=jax.ShapeDtypeStruct((P_ROWS, TOK_SUB, LANES), BF16),
        compiler_params=pltpu.CompilerParams(dimension_semantics=("arbitrary",), vmem_limit_bytes=VMEM_LIMIT),
        name="experts",
    )(n_used, blk_first, blk_ord, used_e, n_used_e, xin, w_gate, w_up, w_down)


COMB_TM = 512


def _combine_kernel(d0_ref, d1_ref, h1_ref, route_ref, yb_ref, out_ref, g_scr, sem):
    i = pl.program_id(0)
    n = pl.num_programs(0)

    def issue(tile, slot):
        base = tile * COMB_TM

        def start(r, carry):
            pltpu.make_async_copy(yb_ref.at[d0_ref[base + r]], g_scr.at[slot, 0, r], sem.at[slot, 0]).start()
            pltpu.make_async_copy(yb_ref.at[d1_ref[base + r]], g_scr.at[slot, 1, r], sem.at[slot, 1]).start(
                priority=1)
            return carry

        lax.fori_loop(0, COMB_TM, start, 0, unroll=DMA_UNROLL)

    @pl.when(i == 0)
    def _():
        issue(0, 0)

    slot = i % 2

    @pl.when(i + 1 < n)
    def _():
        issue(i + 1, 1 - slot)

    for k in range(2):
        pltpu.make_async_copy(yb_ref.at[pl.ds(0, COMB_TM)], g_scr.at[slot, k], sem.at[slot, k]).wait()

    route = route_ref[...]
    w0 = route[:, 2:3]
    w1 = route[:, 3:4]
    g0 = g_scr[slot, 0].reshape(COMB_TM, D_MODEL).astype(F32)
    g1 = g_scr[slot, 1].reshape(COMB_TM, D_MODEL).astype(F32)
    out_ref[...] = h1_ref[...] + (g0 * w0 + g1 * w1)


def _combine(d0, d1, h1, route, yb):
    grid_spec = pltpu.PrefetchScalarGridSpec(
        num_scalar_prefetch=2,
        grid=(N_TOK // COMB_TM,),
        in_specs=[pl.BlockSpec((COMB_TM, D_MODEL), lambda i, *_: (i, 0)),
                  pl.BlockSpec((COMB_TM, LANES), lambda i, *_: (i, 0)),
                  pl.BlockSpec(memory_space=pl.ANY)],
        out_specs=pl.BlockSpec((COMB_TM, D_MODEL), lambda i, *_: (i, 0)),
        scratch_shapes=[pltpu.VMEM((2, 2, COMB_TM, TOK_SUB, LANES), BF16),
                        pltpu.SemaphoreType.DMA((2, 2))],
    )
    return pl.pallas_call(
        _combine_kernel,
        grid_spec=grid_spec,
        out_shape=jax.ShapeDtypeStruct((N_TOK, D_MODEL), F32),
        compiler_params=pltpu.CompilerParams(dimension_semantics=("arbitrary",), vmem_limit_bytes=VMEM_LIMIT),
        name="combine",
    )(d0, d1, h1, route, yb)


def kernel(x, meta, norm1_g, w_in, b_in, dw_w, dw_b, conv_ln_g, conv_ln_b, w_conv_out, b_conv_out,
           q_norm_g, k_norm_g, b_forget, w_attn_out, w_out, norm2_g, w_group, b_group, w_router,
           b_router, w_gate, w_up, w_down):
    l = 0
    row = lambda v: v.reshape(1, -1).astype(F32)

    c_u, c_q, c_f = 2 * CONV_WIDTH, 2 * CONV_WIDTH, 2 * CONV_WIDTH + 3 * ATTN_WIDTH
    c_gc = c_f + N_HEADS
    c_ga = c_gc + D_MODEL
    wi, bi = w_in[l], b_in[l]
    wu, bu = wi[:, :c_u].astype(BF16), row(bi[:c_u])
    wqkv, bqkv = wi[:, c_q:c_f].astype(BF16), row(bi[c_q:c_f])
    wft = jnp.zeros((16, D_MODEL), F32).at[:N_HEADS].set(wi[:, c_f:c_gc].T).astype(BF16)
    bft = jnp.zeros((16, 1), F32).at[:N_HEADS, 0].set(bi[c_f:c_gc] + b_forget[l])
    wgc, bgc = wi[:, c_gc:c_ga].astype(BF16), row(bi[c_gc:c_ga])
    wga, bga = wi[:, c_ga:].astype(BF16), row(bi[c_ga:])
    qg = row(jnp.tile(q_norm_g[l], N_HEADS) * (HEAD_DIM ** -0.5 * LOG2E))
    kg = row(jnp.tile(k_norm_g[l], N_HEADS))
    head_of_col = jnp.arange(ATTN_WIDTH) // HEAD_DIM
    hsum = (head_of_col[:, None] == jnp.arange(LANES)[None, :]).astype(BF16)
    hexp_half = (jnp.arange(LANES)[:, None] == head_of_col[None, :]).astype(BF16)
    hexp = jnp.concatenate([hexp_half, hexp_half], axis=0)
    proj_consts = (row(norm1_g[l]), wu, bu, wqkv, bqkv, wft, bft, wgc, bgc, wga, bga, qg, kg, hsum, hexp)

    x2d = x.reshape(N_TOK, D_MODEL)
    meta_pad = jnp.zeros((META_ROWS, D_MODEL), F32).at[:N_META].set(meta.astype(F32))
    a_m, _, k_m, v_m, lf_m, _, _ = _proj(meta_pad, proj_consts, META_ROWS)
    conv_consts = (a_m, _conv_shift_matrix(), dw_w[l].astype(F32), row(dw_b[l]),
                   row(conv_ln_g[l]), row(conv_ln_b[l]))
    act, q, k, v, lf, sgc, sga = _proj(x2d, proj_consts, PROJ_TM, conv_consts)

    o = _attention_t(q, k, v, lf, k_m, v_m, lf_m)

    wrt = jnp.zeros((LANES, D_MODEL), F32).at[:N_GROUPS].set(w_group[l].T)
    wrt = wrt.at[N_GROUPS:N_GROUPS + N_EXPERTS].set(w_router[l].T).astype(BF16)
    brt = jnp.zeros((LANES, 1), F32).at[:N_GROUPS, 0].set(b_group[l])
    brt = brt.at[N_GROUPS:N_GROUPS + N_EXPERTS, 0].set(b_router[l])
    tri = jnp.triu(jnp.ones((MERGE_TM, MERGE_TM), F32), 1).astype(BF16)
    merge_consts = (w_conv_out[l].astype(BF16), row(b_conv_out[l]), w_attn_out[l].astype(BF16),
                    w_out[l].astype(BF16), row(norm2_g[l]), wrt, brt, tri)
    h1, hn, route, route_t, cnt = _merge(act, o, sgc, sga, x2d, merge_consts)

    counts = cnt[:N_EXPERTS, 0].astype(jnp.int32)
    padded = (counts + ROW_BLOCK - 1) // ROW_BLOCK * ROW_BLOCK
    pend = jnp.cumsum(padded).astype(jnp.int32)
    pstart = pend - padded
    blk_row = jnp.arange(N_BLOCKS, dtype=jnp.int32) * ROW_BLOCK
    n_used = pend[-1:] // ROW_BLOCK
    nz = counts > 0
    before = (pend[None, :] <= blk_row[:, None]) & nz[None, :]
    blk_ord = jnp.sum(before.astype(jnp.int32), axis=1)
    blk_first = jnp.any((pstart[None, :] == blk_row[:, None]) & nz[None, :], axis=1).astype(jnp.int32)
    e_ord = jnp.cumsum(nz.astype(jnp.int32)) - 1
    e_ids = jnp.arange(N_EXPERTS, dtype=jnp.int32)
    used_e = jnp.sum(jnp.where((e_ord[None, :] == e_ids[:, None]) & nz[None, :], e_ids[None, :], 0), axis=1)
    n_used_e = jnp.sum(nz.astype(jnp.int32)).reshape(1)

    dest = _dest(pstart, route_t)
    d0, d1 = dest[0], dest[1]
    xin = _dispatch(d0, d1, pstart + counts, pend, hn)
    yb = _experts(n_used, blk_first, blk_ord, used_e.astype(jnp.int32), n_used_e, xin,
                  w_gate[l], w_up[l], w_down[l])
    out = _combine(d0, d1, h1, route, yb)
    return out.reshape(BATCH, SEQ, D_MODEL)
```

```python
import jax
import jax.numpy as jnp
from jax import lax
from jax.experimental import pallas as pl
from jax.experimental.pallas import tpu as pltpu

D_MODEL = 1024
BATCH = 8
SEQ = 2048
N_META = 16
CONV_WIDTH = 512
CONV_K = 31
N_HEADS = 8
HEAD_DIM = 64
ATTN_WIDTH = N_HEADS * HEAD_DIM
N_GROUPS = 8
EXPERTS_PER_GROUP = 8
N_EXPERTS = N_GROUPS * EXPERTS_PER_GROUP
D_EXPERT = 256
RMS_EPS = 1e-6
LN_EPS = 1e-5

N_TOK = BATCH * SEQ
LANES = 128
META_ROWS = 128
ROW_BLOCK = 512
N_ASSIGN = 2 * N_TOK
N_BLOCKS = N_ASSIGN // ROW_BLOCK + N_EXPERTS
P_ROWS = N_BLOCKS * ROW_BLOCK
TOK_SUB = D_MODEL // LANES
PROJ_TM = 1024
CONV_CHUNK = 128
CONV_PAD = 32
CONV_WIN = CONV_CHUNK + 32
CONV_SHROWS = CONV_CHUNK + 24
CONV_COPIES = 7
CONV_LANES = 128
ATT_TQ = SEQ
ATT_TK = 256
N_PAIRS = N_HEADS // 2
VMEM_LIMIT = 56 * 1024 * 1024

F32 = jnp.float32
BF16 = jnp.bfloat16
NEG_INF = float("-inf")
LOG2E = 1.4426950408889634


def _dot(a, b):
    return jnp.dot(a, b, preferred_element_type=F32)


def _dot_nt(a, b):
    return lax.dot_general(a, b, (((1,), (1,)), ((), ())), preferred_element_type=F32)


def _sigmoid(x):
    return 1.0 / (1.0 + jnp.exp(-x))


def _const_spec(shape):
    nd = len(shape)
    return pl.BlockSpec(shape, lambda *_: (0,) * nd)


def _head_rms(t, hsum_ref, hexp_ref, gain):
    ss = _dot((t * t).astype(BF16), hsum_ref[...])
    inv = lax.rsqrt(ss * (1.0 / HEAD_DIM) + RMS_EPS)
    hi = inv.astype(BF16)
    lo = (inv - hi.astype(F32)).astype(BF16)
    invb = _dot(jnp.concatenate([hi, lo], axis=1), hexp_ref[...])
    return t * invb * gain


N_PROJ_IN = 16


def _proj_stages(x_ref, g1_ref, wu_ref, bu_ref, wqkv_ref, bqkv_ref, wft_ref, bft_ref, wgc_ref, bgc_ref,
                 wga_ref, bga_ref, qg_ref, kg_ref, hsum_ref, hexp_ref,
                 q_ref, k_ref, v_ref, lf_ref, sgc_ref, sga_ref):
    x = x_ref[...]
    ms = jnp.mean(x * x, axis=-1, keepdims=True)
    xn = (x * lax.rsqrt(ms + RMS_EPS) * g1_ref[...]).astype(BF16)

    def glu():
        u = _dot(xn, wu_ref[...]) + bu_ref[...]
        return (u[:, :CONV_WIDTH] * _sigmoid(u[:, CONV_WIDTH:])).astype(BF16)

    def qkv_stage():
        qkv = _dot(xn, wqkv_ref[...]) + bqkv_ref[...]
        q_ref[...] = _head_rms(qkv[:, :ATTN_WIDTH], hsum_ref, hexp_ref, qg_ref[...]).astype(BF16)
        k_ref[...] = _head_rms(qkv[:, ATTN_WIDTH:2 * ATTN_WIDTH], hsum_ref, hexp_ref, kg_ref[...]).astype(BF16)
        v = qkv[:, 2 * ATTN_WIDTH:]
        vb = v_ref.shape[2]
        for c in range(v_ref.shape[0]):
            v_ref[c] = v[c * vb:(c + 1) * vb, :].T.astype(BF16)

    def gate_stage_c():
        f = _dot_nt(wft_ref[...], xn) + bft_ref[...]
        lf_ref[...] = jnp.minimum(f, 0.0) - jnp.log(1.0 + jnp.exp(-jnp.abs(f)))
        sgc_ref[...] = _sigmoid(_dot(xn, wgc_ref[...]) + bgc_ref[...]).astype(BF16)

    def gate_stage_a():
        sga_ref[...] = _sigmoid(_dot(xn, wga_ref[...]) + bga_ref[...]).astype(BF16)

    return glu, (qkv_stage, gate_stage_c, gate_stage_a)


def _proj_kernel(*refs):
    ins, a_ref, outs = refs[:N_PROJ_IN], refs[N_PROJ_IN], refs[N_PROJ_IN + 1:]
    glu, stages = _proj_stages(*ins, *outs)
    a_ref[...] = glu()
    for stage in stages:
        stage()


def _proj_conv_kernel(*refs):
    ins = refs[:N_PROJ_IN]
    am_ref, shift_ref, dw_ref, dwb_ref, lng_ref, lnb_ref = refs[N_PROJ_IN:N_PROJ_IN + 6]
    act_ref, outs = refs[N_PROJ_IN + 6], refs[N_PROJ_IN + 7:N_PROJ_IN + 13]
    win_scr, sh_scr = refs[N_PROJ_IN + 13:]
    tm = act_ref.shape[0]
    i = pl.program_id(0)
    first = (i % (SEQ // tm)) == 0

    @pl.when(first)
    def _():
        win_scr[0:N_META, :] = jnp.zeros((N_META, CONV_WIDTH), BF16)
        win_scr[N_META:CONV_PAD, :] = am_ref[0:N_META, :]

    @pl.when(jnp.logical_not(first))
    def _():
        win_scr[0:CONV_PAD, :] = win_scr[tm:tm + CONV_PAD, :]

    glu, stages = _proj_stages(*ins, *outs)
    win_scr[CONV_PAD:CONV_PAD + tm, :] = glu()

    def conv_chunk(ci):
        r0 = ci * CONV_CHUNK
        slot = ci % 2
        sh_scr[slot] = _dot(shift_ref[...], win_scr[r0:r0 + CONV_WIN, :])
        parts = []
        for c0 in range(0, CONV_WIDTH, CONV_LANES):
            cs = slice(c0, c0 + CONV_LANES)
            part = jnp.zeros((CONV_CHUNK, CONV_LANES), F32) + dwb_ref[:, cs]
            for b in range(8):
                if b == 0:
                    rows = win_scr[r0:r0 + CONV_WIN, cs].astype(F32)
                else:
                    rows = sh_scr[slot, (b - 1) * CONV_SHROWS:b * CONV_SHROWS, cs]
                for j in range(CONV_K):
                    off = j + 2
                    if off % 8 == b:
                        part = part + rows[off - b:off - b + CONV_CHUNK] * dw_ref[j:j + 1, cs]
            parts.append(part)
        acc = jnp.concatenate(parts, axis=1)
        mu = jnp.mean(acc, axis=-1, keepdims=True)
        d = acc - mu
        var = jnp.mean(d * d, axis=-1, keepdims=True)
        y = d * lax.rsqrt(var + LN_EPS) * lng_ref[...] + lnb_ref[...]
        act_ref[r0:r0 + CONV_CHUNK, :] = (y * _sigmoid(y)).astype(BF16)

    n_chunks = tm // CONV_CHUNK
    per_stage = -(-n_chunks // (len(stages) + 1))
    ci = 0
    for stage in (None,) + stages:
        if stage is not None:
            stage()
        for _ in range(per_stage):
            if ci < n_chunks:
                conv_chunk(ci)
                ci += 1


def _proj(x2d, consts, tm, conv_consts=None):
    n = x2d.shape[0]
    row = lambda w: pl.BlockSpec((tm, w), lambda i: (i, 0))
    in_specs = [row(D_MODEL)] + [_const_spec(c.shape) for c in consts]
    fused = conv_consts is not None
    extra = tuple(conv_consts) if fused else ()
    in_specs += [_const_spec(c.shape) for c in extra]
    scratch = [pltpu.VMEM((CONV_PAD + tm, CONV_WIDTH), BF16),
               pltpu.VMEM((2, CONV_COPIES * CONV_SHROWS, CONV_WIDTH), F32)] if fused else []
    vb = min(tm, ATT_TK)
    out_shape = [
        jax.ShapeDtypeStruct((n, CONV_WIDTH), BF16),
        jax.ShapeDtypeStruct((n, ATTN_WIDTH), BF16),
        jax.ShapeDtypeStruct((n, ATTN_WIDTH), BF16),
        jax.ShapeDtypeStruct((n // vb, ATTN_WIDTH, vb), BF16),
        jax.ShapeDtypeStruct((16, n), F32),
        jax.ShapeDtypeStruct((n, D_MODEL), BF16),
        jax.ShapeDtypeStruct((n, D_MODEL), BF16),
    ]
    out_specs = [row(CONV_WIDTH), row(ATTN_WIDTH), row(ATTN_WIDTH),
                 pl.BlockSpec((tm // vb, ATTN_WIDTH, vb), lambda i: (i, 0, 0)),
                 pl.BlockSpec((16, tm), lambda i: (0, i)), row(D_MODEL), row(D_MODEL)]
    return pl.pallas_call(
        _proj_conv_kernel if fused else _proj_kernel,
        grid=(n // tm,),
        in_specs=in_specs,
        out_specs=out_specs,
        out_shape=out_shape,
        scratch_shapes=scratch,
        compiler_params=pltpu.CompilerParams(dimension_semantics=("arbitrary",), vmem_limit_bytes=VMEM_LIMIT),
        name="proj_conv" if fused else "proj",
    )(x2d, *consts, *extra)


def _conv_shift_matrix():
    rr = jnp.arange(CONV_COPIES * CONV_SHROWS)
    src = rr % CONV_SHROWS + rr // CONV_SHROWS + 1
    return (src[:, None] == jnp.arange(CONV_WIN)[None, :]).astype(BF16)


AUG_TERMS = 3
ACC_ONES = 16
SCORE_LEAD = 8


def _split_terms(x):
    hi = x.astype(BF16)
    r1 = x - hi.astype(F32)
    mid = r1.astype(BF16)
    lo = (r1 - mid.astype(F32)).astype(BF16)
    return hi, mid, lo


def _attn_t_kernel(q_ref, k_ref, vt_ref, lf_ref, km_ref, vm_ref, lfm_ref, place_ref, o_ref,
                   aug_scr, augm_scr, m_scr, acc_scr):
    p = pl.program_id(1)
    nk = SEQ // ATT_TK

    def bias_lanes(colsum):
        out = None
        for t, term in enumerate(_split_terms(colsum)):
            d = _dot(term, place_ref[t])
            out = d if out is None else out + d
        return out.astype(BF16)

    @pl.when(p == 0)
    def _():
        r = lax.broadcasted_iota(jnp.int32, (ATT_TK, ATT_TK), 0)
        c = lax.broadcasted_iota(jnp.int32, (ATT_TK, ATT_TK), 1)
        tri = (c <= r).astype(BF16)
        local = []
        for j in range(nk):
            lf = lf_ref[:, j * ATT_TK:(j + 1) * ATT_TK]
            csum = None
            for term in _split_terms(lf):
                d = _dot_nt(tri, term)
                csum = d if csum is None else csum + d
            local.append(csum)
        carry = jnp.zeros((1, 16), F32)
        for j in range(nk):
            csum = local[j] + carry
            aug_scr[j * ATT_TK:(j + 1) * ATT_TK, :] = bias_lanes(csum * (-LOG2E))
            carry = csum[ATT_TK - 1:ATT_TK, :]
        rm = lax.broadcasted_iota(jnp.int32, (META_ROWS, META_ROWS), 0)
        cm = lax.broadcasted_iota(jnp.int32, (META_ROWS, META_ROWS), 1)
        trim = ((cm > rm) & (cm < N_META)).astype(BF16)
        msum = jnp.zeros((META_ROWS, 16), F32)
        for term in _split_terms(lfm_ref[...]):
            msum = msum + _dot_nt(trim, term)
        augm_scr[...] = bias_lanes(msum * LOG2E)

    lane = lax.broadcasted_iota(jnp.int32, (ATT_TQ, LANES), 1)
    q_all = q_ref[...]
    zq = jnp.zeros_like(q_all)
    q_cat = []
    for hl in range(2):
        own = (lane < HEAD_DIM) if hl == 0 else (lane >= HEAD_DIM)
        h = 2 * p + hl
        bias_sel = (lane >= AUG_TERMS * h) & (lane < AUG_TERMS * h + AUG_TERMS)
        bias_ones = jnp.where(bias_sel, 1.0, 0.0).astype(BF16)
        q_cat.append(jnp.concatenate([jnp.where(own, q_all, zq), bias_ones], axis=1))
        m_scr[hl] = jnp.full((1, ATT_TQ), NEG_INF, F32)
        acc_scr[hl] = jnp.zeros((HEAD_DIM + ACC_ONES, ATT_TQ), F32)

    def scores(kb, augb, lo=0):
        k_cat = jnp.concatenate([kb, augb], axis=1)
        return tuple(_dot_nt(k_cat, q_cat[hl][lo:]) for hl in range(2))

    def block_softmax(s2, vtb):
        ones = jnp.ones((ACC_ONES, vtb.shape[1]), BF16)
        out = []
        for hl in range(2):
            s = s2[hl]
            m_blk = jnp.max(s, axis=0, keepdims=True)
            pr = jnp.exp2(s - m_blk).astype(BF16)
            v_own = jnp.concatenate([vtb[hl * HEAD_DIM:(hl + 1) * HEAD_DIM], ones], axis=0)
            out.append((m_blk, _dot(v_own, pr)))
        return out

    def block_scores(j):
        rs = slice(j * ATT_TK, (j + 1) * ATT_TK)
        return scores(k_ref[rs, :], aug_scr[rs, :], lo=j * ATT_TK)

    s_meta = scores(km_ref[0:N_META, :], augm_scr[0:N_META, :])
    s_blocks = [block_scores(j) for j in range(min(SCORE_LEAD, nk))]
    partials = [(0, block_softmax(s_meta, vm_ref[0][:, 0:N_META]))]

    srow = lax.broadcasted_iota(jnp.int32, (ATT_TK, ATT_TK), 0)
    tcol = lax.broadcasted_iota(jnp.int32, (ATT_TK, ATT_TK), 1)
    tri = srow <= tcol
    for j in range(nk):
        if j + SCORE_LEAD < nk:
            s_blocks.append(block_scores(j + SCORE_LEAD))
        s_cur = tuple(jnp.concatenate([jnp.where(tri, s[:, :ATT_TK], NEG_INF), s[:, ATT_TK:]], axis=1)
                      if s.shape[1] > ATT_TK else jnp.where(tri, s, NEG_INF) for s in s_blocks[j])
        partials.append((j * ATT_TK, block_softmax(s_cur, vt_ref[j])))

    for hl in range(2):
        for lo, part in partials:
            m_scr[hl, :, lo:] = jnp.maximum(m_scr[hl, :, lo:], part[hl][0])
        for lo, part in partials:
            m_blk, pv = part[hl]
            acc_scr[hl, :, lo:] = acc_scr[hl, :, lo:] + pv * jnp.exp2(m_blk - m_scr[hl, :, lo:])

    a0 = acc_scr[0]
    a1 = acc_scr[1]
    ot = jnp.concatenate([a0[0:HEAD_DIM] / a0[HEAD_DIM:HEAD_DIM + 1], a1[0:HEAD_DIM] / a1[HEAD_DIM:HEAD_DIM + 1]],
                         axis=0)
    o_ref[...] = ot.T.astype(BF16)


def _attention_t(q, k, vt, lf, km, vm, lfm):
    nk = SEQ // ATT_TK
    hh = jnp.arange(16)[:, None]
    ll = jnp.arange(LANES)[None, :]
    place = jnp.stack([((ll == AUG_TERMS * hh + t) & (hh < N_HEADS)).astype(BF16) for t in range(AUG_TERMS)])
    return pl.pallas_call(
        _attn_t_kernel,
        grid=(BATCH, N_PAIRS),
        in_specs=[
            pl.BlockSpec((ATT_TQ, LANES), lambda b, p: (b, p)),
            pl.BlockSpec((SEQ, LANES), lambda b, p: (b, p)),
            pl.BlockSpec((nk, LANES, ATT_TK), lambda b, p: (b, p, 0)),
            pl.BlockSpec((16, SEQ), lambda b, p: (0, b)),
            pl.BlockSpec((META_ROWS, LANES), lambda b, p: (0, p)),
            pl.BlockSpec((1, LANES, META_ROWS), lambda b, p: (0, p, 0)),
            pl.BlockSpec((16, META_ROWS), lambda b, p: (0, 0)),
            _const_spec(place.shape),
        ],
        out_specs=pl.BlockSpec((ATT_TQ, LANES), lambda b, p: (b, p)),
        out_shape=jax.ShapeDtypeStruct((N_TOK, ATTN_WIDTH), BF16),
        scratch_shapes=[
            pltpu.VMEM((SEQ, LANES), BF16),
            pltpu.VMEM((META_ROWS, LANES), BF16),
            pltpu.VMEM((2, 1, ATT_TQ), F32),
            pltpu.VMEM((2, HEAD_DIM + ACC_ONES, ATT_TQ), F32),
        ],
        compiler_params=pltpu.CompilerParams(
            dimension_semantics=("arbitrary", "arbitrary"), vmem_limit_bytes=VMEM_LIMIT),
        name="attn",
    )(q, k, vt, lf, km, vm, lfm, place)


MERGE_TM = 1024


def _merge_kernel(act_ref, o_ref, sgc_ref, sga_ref, x_ref, wpw_ref, bpw_ref, wao_ref, wout_ref, g2_ref,
                  wrt_ref, brt_ref, tri_ref,
                  h1_ref, hn_ref, route_ref, routet_ref, cnt_ref, base_scr):
    i = pl.program_id(0)

    @pl.when(i == 0)
    def _():
        base_scr[...] = jnp.zeros_like(base_scr)

    tm = x_ref.shape[0]
    hn_parts = []
    for rs in (slice(0, tm // 2), slice(tm // 2, tm)):
        y_conv = _dot(act_ref[rs, :], wpw_ref[...]) + bpw_ref[...]
        y_attn = _dot(o_ref[rs, :], wao_ref[...])
        merged = sgc_ref[rs, :].astype(F32) * y_conv + sga_ref[rs, :].astype(F32) * y_attn
        h1 = x_ref[rs, :] + _dot(merged.astype(BF16), wout_ref[...])
        h1_ref[rs, :] = h1
        ms = jnp.mean(h1 * h1, axis=-1, keepdims=True)
        hn_half = (h1 * lax.rsqrt(ms + RMS_EPS) * g2_ref[...]).astype(BF16)
        hn_ref[rs] = hn_half.reshape(tm // 2, TOK_SUB, LANES)
        hn_parts.append(hn_half)
    hn = jnp.concatenate(hn_parts, axis=0)

    logits = _dot_nt(wrt_ref[...], hn) + brt_ref[...]
    row = lax.broadcasted_iota(jnp.int32, logits.shape, 0).astype(F32)
    big = float(LANES)
    is_g = row < N_GROUPS
    gl = jnp.where(is_g, logits, NEG_INF)
    gmax = jnp.max(gl, axis=0, keepdims=True)
    gidx = jnp.min(jnp.where(gl == gmax, row, big), axis=0, keepdims=True)
    gsum = jnp.sum(jnp.where(is_g, jnp.exp(gl - gmax), 0.0), axis=0, keepdims=True)
    g_w = 1.0 / gsum
    lo = N_GROUPS + EXPERTS_PER_GROUP * gidx
    el = jnp.where((row >= lo) & (row < lo + EXPERTS_PER_GROUP), logits, NEG_INF)
    t1 = jnp.max(el, axis=0, keepdims=True)
    i1 = jnp.min(jnp.where(el == t1, row, big), axis=0, keepdims=True)
    el2 = jnp.where(row == i1, NEG_INF, el)
    t2 = jnp.max(el2, axis=0, keepdims=True)
    i2 = jnp.min(jnp.where(el2 == t2, row, big), axis=0, keepdims=True)
    e2 = jnp.exp(t2 - t1)
    den = 1.0 + e2
    w1 = (1.0 / den) * g_w
    w2 = (e2 / den) * g_w
    ea = i1 - N_GROUPS
    eb = i2 - N_GROUPS

    oha = (row == ea).astype(F32)
    ohb = (row == eb).astype(F32)
    ohs = oha + ohb
    before = _dot(ohs.astype(BF16), tri_ref[...]) + base_scr[...][:, 0:1]
    ra = jnp.sum(before * oha, axis=0, keepdims=True)
    rb = jnp.sum(before * ohb, axis=0, keepdims=True)
    base_scr[...] = base_scr[...] + jnp.sum(ohs, axis=1, keepdims=True)
    cnt_ref[...] = base_scr[...]

    def rows_of(shape):
        r = lax.broadcasted_iota(jnp.int32, shape, 0)
        out = jnp.where(r == 0, ea, 0.0)
        for k, val in enumerate((eb, w1, w2, ra, rb), start=1):
            out = jnp.where(r == k, val, out)
        return out

    routet_ref[...] = rows_of((8, logits.shape[1]))
    route_ref[...] = rows_of(logits.shape).T


def _merge(act, o, sgc, sga, x2d, consts):
    tm = MERGE_TM
    row = lambda w: pl.BlockSpec((tm, w), lambda i: (i, 0))
    return pl.pallas_call(
        _merge_kernel,
        grid=(N_TOK // tm,),
        in_specs=[row(CONV_WIDTH), row(ATTN_WIDTH), row(D_MODEL), row(D_MODEL), row(D_MODEL)]
        + [_const_spec(c.shape) for c in consts],
        out_specs=[row(D_MODEL), pl.BlockSpec((tm, TOK_SUB, LANES), lambda i: (i, 0, 0)), row(LANES),
                   pl.BlockSpec((8, tm), lambda i: (0, i)), _const_spec((LANES, LANES))],
        out_shape=[jax.ShapeDtypeStruct((N_TOK, D_MODEL), F32),
                   jax.ShapeDtypeStruct((N_TOK, TOK_SUB, LANES), BF16),
                   jax.ShapeDtypeStruct((N_TOK, LANES), F32),
                   jax.ShapeDtypeStruct((8, N_TOK), F32),
                   jax.ShapeDtypeStruct((LANES, LANES), F32)],
        scratch_shapes=[pltpu.VMEM((LANES, LANES), F32)],
        compiler_params=pltpu.CompilerParams(dimension_semantics=("arbitrary",), vmem_limit_bytes=VMEM_LIMIT),
        name="merge",
    )(act, o, sgc, sga, x2d, *consts)


DISP_TM = 2048
DMA_UNROLL = 8


def _dest_kernel(pstart_ref, rt_ref, d_ref):
    rt = rt_ref[...]
    start = jnp.zeros_like(rt)
    for e in range(N_EXPERTS):
        start = jnp.where(rt == float(e), pstart_ref[e].astype(F32), start)
    d_ref[...] = (start + pltpu.roll(rt, 4, axis=0)).astype(jnp.int32)


def _dest(pstart, route_t):
    grid_spec = pltpu.PrefetchScalarGridSpec(
        num_scalar_prefetch=1,
        grid=(1,),
        in_specs=[pl.BlockSpec(route_t.shape, lambda i, *_: (0, 0))],
        out_specs=pl.BlockSpec(route_t.shape, lambda i, *_: (0, 0)),
    )
    return pl.pallas_call(
        _dest_kernel,
        grid_spec=grid_spec,
        out_shape=jax.ShapeDtypeStruct(route_t.shape, jnp.int32),
        compiler_params=pltpu.CompilerParams(dimension_semantics=("arbitrary",), vmem_limit_bytes=VMEM_LIMIT),
        name="dest",
    )(pstart, route_t)


ZERO_CHUNK = 64


def _dispatch_kernel(d0_ref, d1_ref, pfill_ref, pend_ref, hn_ref, xin_ref, zero_scr, sem, zsem):
    i = pl.program_id(0)
    base = i * DISP_TM

    @pl.when(i == 0)
    def _():
        zero_scr[...] = jnp.zeros_like(zero_scr)

        def pad_chunks(e, fn):
            first = pfill_ref[e] // ZERO_CHUNK * ZERO_CHUNK

            def one(c, carry):
                start = pl.multiple_of(first + c * ZERO_CHUNK, ZERO_CHUNK)
                fn(pltpu.make_async_copy(zero_scr.at[pl.ds(0, ZERO_CHUNK)], xin_ref.at[pl.ds(start, ZERO_CHUNK)],
                                         zsem.at[0]))
                return carry

            lax.fori_loop(0, (pend_ref[e] - first) // ZERO_CHUNK, one, 0)

        def zstart(e, carry):
            pad_chunks(e, lambda c: c.start())
            return carry

        def zwait(e, carry):
            pad_chunks(e, lambda c: c.wait())
            return carry

        n_used = pend_ref[N_EXPERTS - 1] // ROW_BLOCK

        def tcopy(b):
            return pltpu.make_async_copy(
                zero_scr, xin_ref.at[pl.ds(pl.multiple_of(b * ROW_BLOCK, ROW_BLOCK), ROW_BLOCK)], zsem.at[0])

        def tstart(b, carry):
            tcopy(b).start()
            return carry

        def twait(b, carry):
            tcopy(b).wait()
            return carry

        lax.fori_loop(0, N_EXPERTS, zstart, 0)
        lax.fori_loop(n_used, N_BLOCKS, tstart, 0)
        lax.fori_loop(0, N_EXPERTS, zwait, 0)
        lax.fori_loop(n_used, N_BLOCKS, twait, 0)

    def start(r, carry):
        pltpu.make_async_copy(hn_ref.at[r], xin_ref.at[d0_ref[base + r]], sem.at[0]).start()
        pltpu.make_async_copy(hn_ref.at[r], xin_ref.at[d1_ref[base + r]], sem.at[1]).start(priority=1)
        return carry

    lax.fori_loop(0, DISP_TM, start, 0, unroll=DMA_UNROLL)
    pltpu.make_async_copy(hn_ref, xin_ref.at[pl.ds(0, DISP_TM)], sem.at[0]).wait()
    pltpu.make_async_copy(hn_ref, xin_ref.at[pl.ds(0, DISP_TM)], sem.at[1]).wait()


def _dispatch(d0, d1, pfill, pend, hn):
    grid_spec = pltpu.PrefetchScalarGridSpec(
        num_scalar_prefetch=4,
        grid=(N_TOK // DISP_TM,),
        in_specs=[pl.BlockSpec((DISP_TM, TOK_SUB, LANES), lambda i, *_: (i, 0, 0))],
        out_specs=pl.BlockSpec(memory_space=pl.ANY),
        scratch_shapes=[pltpu.VMEM((ROW_BLOCK, TOK_SUB, LANES), BF16),
                        pltpu.SemaphoreType.DMA((2,)), pltpu.SemaphoreType.DMA((1,))],
    )
    return pl.pallas_call(
        _dispatch_kernel,
        grid_spec=grid_spec,
        out_shape=jax.ShapeDtypeStruct((P_ROWS, TOK_SUB, LANES), BF16),
        compiler_params=pltpu.CompilerParams(dimension_semantics=("arbitrary",), vmem_limit_bytes=VMEM_LIMIT),
        name="dispatch",
    )(d0, d1, pfill, pend, hn)


W_SLOTS = 3


def _expert_kernel(nu_ref, first_ref, ord_ref, used_ref, nue_ref, x_hbm, wg_hbm, wu_hbm, wd_hbm, y_hbm,
                   x_buf, y_buf, wg_f, wu_f, wd_f, wg_b, wu_b, wd_b, xsem, ysem, wsem):
    n = nu_ref[0]

    def rows(i):
        return pl.ds(pl.multiple_of(i * ROW_BLOCK, ROW_BLOCK), ROW_BLOCK)

    def xcopy(i, slot):
        return pltpu.make_async_copy(x_hbm.at[rows(i)], x_buf.at[slot], xsem.at[slot])

    def ycopy(i, slot):
        return pltpu.make_async_copy(y_buf.at[slot], y_hbm.at[rows(i)], ysem.at[slot])

    def fetch(k):
        e = used_ref[k]
        slot = k % W_SLOTS
        return (pltpu.make_async_copy(wg_hbm.at[e], wg_f.at[slot], wsem.at[slot, 0]),
                pltpu.make_async_copy(wu_hbm.at[e], wu_f.at[slot], wsem.at[slot, 1]),
                pltpu.make_async_copy(wd_hbm.at[e], wd_f.at[slot], wsem.at[slot, 2]))

    for k in range(W_SLOTS - 1):
        @pl.when(k < nue_ref[0])
        def _():
            for c in fetch(k):
                c.start()
    xcopy(0, 0).start()

    def block(i, carry):
        slot = i % 2
        xcopy(i, slot).wait()

        @pl.when(i + 1 < n)
        def _():
            xcopy(i + 1, 1 - slot).start()

        @pl.when(first_ref[i] == 1)
        def _():
            k = ord_ref[i]
            for c in fetch(k):
                c.wait()

            @pl.when(k + (W_SLOTS - 1) < nue_ref[0])
            def _():
                for c in fetch(k + (W_SLOTS - 1)):
                    c.start()

            wslot = k % W_SLOTS
            wg_b[...] = wg_f[wslot].astype(BF16)
            wu_b[...] = wu_f[wslot].astype(BF16)
            wd_b[...] = wd_f[wslot].astype(BF16)

        @pl.when(i >= 2)
        def _():
            ycopy(i - 2, slot).wait()

        x = x_buf[slot].reshape(ROW_BLOCK, D_MODEL)
        g = _dot(x, wg_b[...])
        u = _dot(x, wu_b[...])
        hmid = (g * _sigmoid(g) * u).astype(BF16)
        y = _dot(hmid, wd_b[...]).astype(BF16)
        y_buf[slot] = y.reshape(ROW_BLOCK, TOK_SUB, LANES)
        ycopy(i, slot).start()
        return carry

    lax.fori_loop(0, n, block, 0)

    @pl.when(n >= 2)
    def _():
        ycopy(n - 2, n % 2).wait()
    ycopy(n - 1, (n - 1) % 2).wait()

    y_buf[0] = jnp.zeros((ROW_BLOCK, TOK_SUB, LANES), BF16)

    def zcopy(b):
        return pltpu.make_async_copy(y_buf.at[0], y_hbm.at[rows(b)], ysem.at[0])

    def zstart(b, carry):
        zcopy(b).start()
        return carry

    def zwait(b, carry):
        zcopy(b).wait()
        return carry

    lax.fori_loop(n, N_BLOCKS, zstart, 0)
    lax.fori_loop(n, N_BLOCKS, zwait, 0)


def _experts(n_used, blk_first, blk_ord, used_e, n_used_e, xin, w_gate, w_up, w_down):
    grid_spec = pltpu.PrefetchScalarGridSpec(
        num_scalar_prefetch=5,
        grid=(1,),
        in_specs=[pl.BlockSpec(memory_space=pl.ANY),
                  pl.BlockSpec(memory_space=pl.ANY),
                  pl.BlockSpec(memory_space=pl.ANY),
                  pl.BlockSpec(memory_space=pl.ANY)],
        out_specs=pl.BlockSpec(memory_space=pl.ANY),
        scratch_shapes=[pltpu.VMEM((2, ROW_BLOCK, TOK_SUB, LANES), BF16),
                        pltpu.VMEM((2, ROW_BLOCK, TOK_SUB, LANES), BF16),
                        pltpu.VMEM((W_SLOTS, D_MODEL, D_EXPERT), F32),
                        pltpu.VMEM((W_SLOTS, D_MODEL, D_EXPERT), F32),
                        pltpu.VMEM((W_SLOTS, D_EXPERT, D_MODEL), F32),
                        pltpu.VMEM((D_MODEL, D_EXPERT), BF16),
                        pltpu.VMEM((D_MODEL, D_EXPERT), BF16),
                        pltpu.VMEM((D_EXPERT, D_MODEL), BF16),
                        pltpu.SemaphoreType.DMA((2,)), pltpu.SemaphoreType.DMA((2,)),
                        pltpu.SemaphoreType.DMA((W_SLOTS, 3))],
    )
    return pl.pallas_call(
        _expert_kernel,
        grid_spec=grid_spec,
        out_shape=jax.ShapeDtypeStruct((P_ROWS, TOK_SUB, LANES), BF16),
        compiler_params=pltpu.CompilerParams(dimension_semantics=("arbitrary",), vmem_limit_bytes=VMEM_LIMIT),
        name="experts",
    )(n_used, blk_first, blk_ord, used_e, n_used_e, xin, w_gate, w_up, w_down)


COMB_TM = 512


def _combine_kernel(d0_ref, d1_ref, h1_ref, route_ref, yb_ref, out_ref, g_scr, sem):
    i = pl.program_id(0)
    n = pl.num_programs(0)

    def issue(tile, slot):
        base = tile * COMB_TM

        def start(r, carry):
            pltpu.make_async_copy(yb_ref.at[d0_ref[base + r]], g_scr.at[slot, 0, r], sem.at[slot, 0]).start()
            pltpu.make_async_copy(yb_ref.at[d1_ref[base + r]], g_scr.at[slot, 1, r], sem.at[slot, 1]).start(
                priority=1)
            return carry

        lax.fori_loop(0, COMB_TM, start, 0, unroll=DMA_UNROLL)

    @pl.when(i == 0)
    def _():
        issue(0, 0)

    slot = i % 2

    @pl.when(i + 1 < n)
    def _():
        issue(i + 1, 1 - slot)

    for k in range(2):
        pltpu.make_async_copy(yb_ref.at[pl.ds(0, COMB_TM)], g_scr.at[slot, k], sem.at[slot, k]).wait()

    route = route_ref[...]
    w0 = route[:, 2:3]
    w1 = route[:, 3:4]
    g0 = g_scr[slot, 0].reshape(COMB_TM, D_MODEL).astype(F32)
    g1 = g_scr[slot, 1].reshape(COMB_TM, D_MODEL).astype(F32)
    out_ref[...] = h1_ref[...] + (g0 * w0 + g1 * w1)


def _combine(d0, d1, h1, route, yb):
    grid_spec = pltpu.PrefetchScalarGridSpec(
        num_scalar_prefetch=2,
        grid=(N_TOK // COMB_TM,),
        in_specs=[pl.BlockSpec((COMB_TM, D_MODEL), lambda i, *_: (i, 0)),
                  pl.BlockSpec((COMB_TM, LANES), lambda i, *_: (i, 0)),
                  pl.BlockSpec(memory_space=pl.ANY)],
        out_specs=pl.BlockSpec((COMB_TM, D_MODEL), lambda i, *_: (i, 0)),
        scratch_shapes=[pltpu.VMEM((2, 2, COMB_TM, TOK_SUB, LANES), BF16),
                        pltpu.SemaphoreType.DMA((2, 2))],
    )
    return pl.pallas_call(
        _combine_kernel,
        grid_spec=grid_spec,
        out_shape=jax.ShapeDtypeStruct((N_TOK, D_MODEL), F32),
        compiler_params=pltpu.CompilerParams(dimension_semantics=("arbitrary",), vmem_limit_bytes=VMEM_LIMIT),
        name="combine",
    )(d0, d1, h1, route, yb)


def kernel(x, meta, norm1_g, w_in, b_in, dw_w, dw_b, conv_ln_g, conv_ln_b, w_conv_out, b_conv_out,
           q_norm_g, k_norm_g, b_forget, w_attn_out, w_out, norm2_g, w_group, b_group, w_router,
           b_router, w_gate, w_up, w_down):
    l = 0
    row = lambda v: v.reshape(1, -1).astype(F32)

    c_u, c_q, c_f = 2 * CONV_WIDTH, 2 * CONV_WIDTH, 2 * CONV_WIDTH + 3 * ATTN_WIDTH
    c_gc = c_f + N_HEADS
    c_ga = c_gc + D_MODEL
    wi, bi = w_in[l], b_in[l]
    wu, bu = wi[:, :c_u].astype(BF16), row(bi[:c_u])
    wqkv, bqkv = wi[:, c_q:c_f].astype(BF16), row(bi[c_q:c_f])
    wft = jnp.zeros((16, D_MODEL), F32).at[:N_HEADS].set(wi[:, c_f:c_gc].T).astype(BF16)
    bft = jnp.zeros((16, 1), F32).at[:N_HEADS, 0].set(bi[c_f:c_gc] + b_forget[l])
    wgc, bgc = wi[:, c_gc:c_ga].astype(BF16), row(bi[c_gc:c_ga])
    wga, bga = wi[:, c_ga:].astype(BF16), row(bi[c_ga:])
    qg = row(jnp.tile(q_norm_g[l], N_HEADS) * (HEAD_DIM ** -0.5 * LOG2E))
    kg = row(jnp.tile(k_norm_g[l], N_HEADS))
    head_of_col = jnp.arange(ATTN_WIDTH) // HEAD_DIM
    hsum = (head_of_col[:, None] == jnp.arange(LANES)[None, :]).astype(BF16)
    hexp_half = (jnp.arange(LANES)[:, None] == head_of_col[None, :]).astype(BF16)
    hexp = jnp.concatenate([hexp_half, hexp_half], axis=0)
    proj_consts = (row(norm1_g[l]), wu, bu, wqkv, bqkv, wft, bft, wgc, bgc, wga, bga, qg, kg, hsum, hexp)

    x2d = x.reshape(N_TOK, D_MODEL)
    meta_pad = jnp.zeros((META_ROWS, D_MODEL), F32).at[:N_META].set(meta.astype(F32))
    a_m, _, k_m, v_m, lf_m, _, _ = _proj(meta_pad, proj_consts, META_ROWS)
    conv_consts = (a_m, _conv_shift_matrix(), dw_w[l].astype(F32), row(dw_b[l]),
                   row(conv_ln_g[l]), row(conv_ln_b[l]))
    act, q, k, v, lf, sgc, sga = _proj(x2d, proj_consts, PROJ_TM, conv_consts)

    o = _attention_t(q, k, v, lf, k_m, v_m, lf_m)

    wrt = jnp.zeros((LANES, D_MODEL), F32).at[:N_GROUPS].set(w_group[l].T)
    wrt = wrt.at[N_GROUPS:N_GROUPS + N_EXPERTS].set(w_router[l].T).astype(BF16)
    brt = jnp.zeros((LANES, 1), F32).at[:N_GROUPS, 0].set(b_group[l])
    brt = brt.at[N_GROUPS:N_GROUPS + N_EXPERTS, 0].set(b_router[l])
    tri = jnp.triu(jnp.ones((MERGE_TM, MERGE_TM), F32), 1).astype(BF16)
    merge_consts = (w_conv_out[l].astype(BF16), row(b_conv_out[l]), w_attn_out[l].astype(BF16),
                    w_out[l].astype(BF16), row(norm2_g[l]), wrt, brt, tri)
    h1, hn, route, route_t, cnt = _merge(act, o, sgc, sga, x2d, merge_consts)

    counts = cnt[:N_EXPERTS, 0].astype(jnp.int32)
    padded = (counts + ROW_BLOCK - 1) // ROW_BLOCK * ROW_BLOCK
    pend = jnp.cumsum(padded).astype(jnp.int32)
    pstart = pend - padded
    blk_row = jnp.arange(N_BLOCKS, dtype=jnp.int32) * ROW_BLOCK
    n_used = pend[-1:] // ROW_BLOCK
    nz = counts > 0
    before = (pend[None, :] <= blk_row[:, None]) & nz[None, :]
    blk_ord = jnp.sum(before.astype(jnp.int32), axis=1)
    blk_first = jnp.any((pstart[None, :] == blk_row[:, None]) & nz[None, :], axis=1).astype(jnp.int32)
    e_ord = jnp.cumsum(nz.astype(jnp.int32)) - 1
    e_ids = jnp.arange(N_EXPERTS, dtype=jnp.int32)
    used_e = jnp.sum(jnp.where((e_ord[None, :] == e_ids[:, None]) & nz[None, :], e_ids[None, :], 0), axis=1)
    n_used_e = jnp.sum(nz.astype(jnp.int32)).reshape(1)

    dest = _dest(pstart, route_t)
    d0, d1 = dest[0], dest[1]
    xin = _dispatch(d0, d1, pstart + counts, pend, hn)
    yb = _experts(n_used, blk_first, blk_ord, used_e.astype(jnp.int32), n_used_e, xin,
                  w_gate[l], w_up[l], w_down[l])
    out = _combine(d0, d1, h1, route, yb)
    return out.reshape(BATCH, SEQ, D_MODEL)
```

```python
import jax
import jax.numpy as jnp
from jax import lax
from jax.experimental import pallas as pl
from jax.experimental.pallas import tpu as pltpu

D_MODEL = 1024
BATCH = 8
SEQ = 2048
N_META = 16
CONV_WIDTH = 512
CONV_K = 31
N_HEADS = 8
HEAD_DIM = 64
ATTN_WIDTH = N_HEADS * HEAD_DIM
N_GROUPS = 8
EXPERTS_PER_GROUP = 8
N_EXPERTS = N_GROUPS * EXPERTS_PER_GROUP
D_EXPERT = 256
RMS_EPS = 1e-6
LN_EPS = 1e-5

N_TOK = BATCH * SEQ
LANES = 128
META_ROWS = 128
ROW_BLOCK = 512
N_ASSIGN = 2 * N_TOK
N_BLOCKS = N_ASSIGN // ROW_BLOCK + N_EXPERTS
P_ROWS = N_BLOCKS * ROW_BLOCK
TOK_SUB = D_MODEL // LANES
PROJ_TM = 1024
CONV_CHUNK = 128
CONV_PAD = 32
CONV_WIN = CONV_CHUNK + 32
CONV_SHROWS = CONV_CHUNK + 24
CONV_COPIES = 7
CONV_LANES = 128
ATT_TQ = SEQ
ATT_TK = 256
N_PAIRS = N_HEADS // 2
VMEM_LIMIT = 56 * 1024 * 1024

F32 = jnp.float32
BF16 = jnp.bfloat16
NEG_INF = float("-inf")
LOG2E = 1.4426950408889634


def _dot(a, b):
    return jnp.dot(a, b, preferred_element_type=F32)


def _dot_nt(a, b):
    return lax.dot_general(a, b, (((1,), (1,)), ((), ())), preferred_element_type=F32)


def _sigmoid(x):
    return 1.0 / (1.0 + jnp.exp(-x))


def _const_spec(shape):
    nd = len(shape)
    return pl.BlockSpec(shape, lambda *_: (0,) * nd)


def _head_rms(t, hsum_ref, hexp_ref, gain):
    ss = _dot((t * t).astype(BF16), hsum_ref[...])
    inv = lax.rsqrt(ss * (1.0 / HEAD_DIM) + RMS_EPS)
    hi = inv.astype(BF16)
    lo = (inv - hi.astype(F32)).astype(BF16)
    invb = _dot(jnp.concatenate([hi, lo], axis=1), hexp_ref[...])
    return t * invb * gain


N_PROJ_IN = 16


def _proj_stages(x_ref, g1_ref, wu_ref, bu_ref, wqkv_ref, bqkv_ref, wft_ref, bft_ref, wgc_ref, bgc_ref,
                 wga_ref, bga_ref, qg_ref, kg_ref, hsum_ref, hexp_ref,
                 q_ref, k_ref, v_ref, lf_ref, sgc_ref, sga_ref):
    x = x_ref[...]
    ms = jnp.mean(x * x, axis=-1, keepdims=True)
    xn = (x * lax.rsqrt(ms + RMS_EPS) * g1_ref[...]).astype(BF16)

    def glu():
        u = _dot(xn, wu_ref[...]) + bu_ref[...]
        return (u[:, :CONV_WIDTH] * _sigmoid(u[:, CONV_WIDTH:])).astype(BF16)

    def qkv_stage():
        qkv = _dot(xn, wqkv_ref[...]) + bqkv_ref[...]
        q_ref[...] = _head_rms(qkv[:, :ATTN_WIDTH], hsum_ref, hexp_ref, qg_ref[...]).astype(BF16)
        k_ref[...] = _head_rms(qkv[:, ATTN_WIDTH:2 * ATTN_WIDTH], hsum_ref, hexp_ref, kg_ref[...]).astype(BF16)
        v = qkv[:, 2 * ATTN_WIDTH:]
        vb = v_ref.shape[2]
        for c in range(v_ref.shape[0]):
            v_ref[c] = v[c * vb:(c + 1) * vb, :].T.astype(BF16)

    def gate_stage_c():
        f = _dot_nt(wft_ref[...], xn) + bft_ref[...]
        lf_ref[...] = jnp.minimum(f, 0.0) - jnp.log(1.0 + jnp.exp(-jnp.abs(f)))
        sgc_ref[...] = _sigmoid(_dot(xn, wgc_ref[...]) + bgc_ref[...]).astype(BF16)

    def gate_stage_a():
        sga_ref[...] = _sigmoid(_dot(xn, wga_ref[...]) + bga_ref[...]).astype(BF16)

    return glu, (qkv_stage, gate_stage_c, gate_stage_a)


def _proj_kernel(*refs):
    ins, a_ref, outs = refs[:N_PROJ_IN], refs[N_PROJ_IN], refs[N_PROJ_IN + 1:]
    glu, stages = _proj_stages(*ins, *outs)
    a_ref[...] = glu()
    for stage in stages:
        stage()


def _proj_conv_kernel(*refs):
    ins = refs[:N_PROJ_IN]
    am_ref, shift_ref, dw_ref, dwb_ref, lng_ref, lnb_ref = refs[N_PROJ_IN:N_PROJ_IN + 6]
    act_ref, outs = refs[N_PROJ_IN + 6], refs[N_PROJ_IN + 7:N_PROJ_IN + 13]
    win_scr, sh_scr = refs[N_PROJ_IN + 13:]
    tm = act_ref.shape[0]
    i = pl.program_id(0)
    first = (i % (SEQ // tm)) == 0

    @pl.when(first)
    def _():
        win_scr[0:N_META, :] = jnp.zeros((N_META, CONV_WIDTH), BF16)
        win_scr[N_META:CONV_PAD, :] = am_ref[0:N_META, :]

    @pl.when(jnp.logical_not(first))
    def _():
        win_scr[0:CONV_PAD, :] = win_scr[tm:tm + CONV_PAD, :]

    glu, stages = _proj_stages(*ins, *outs)
    win_scr[CONV_PAD:CONV_PAD + tm, :] = glu()

    def conv_chunk(ci):
        r0 = ci * CONV_CHUNK
        slot = ci % 2
        sh_scr[slot] = _dot(shift_ref[...], win_scr[r0:r0 + CONV_WIN, :])
        parts = []
        for c0 in range(0, CONV_WIDTH, CONV_LANES):
            cs = slice(c0, c0 + CONV_LANES)
            part = jnp.zeros((CONV_CHUNK, CONV_LANES), F32) + dwb_ref[:, cs]
            for b in range(8):
                if b == 0:
                    rows = win_scr[r0:r0 + CONV_WIN, cs].astype(F32)
                else:
                    rows = sh_scr[slot, (b - 1) * CONV_SHROWS:b * CONV_SHROWS, cs]
                for j in range(CONV_K):
                    off = j + 2
                    if off % 8 == b:
                        part = part + rows[off - b:off - b + CONV_CHUNK] * dw_ref[j:j + 1, cs]
            parts.append(part)
        acc = jnp.concatenate(parts, axis=1)
        mu = jnp.mean(acc, axis=-1, keepdims=True)
        d = acc - mu
        var = jnp.mean(d * d, axis=-1, keepdims=True)
        y = d * lax.rsqrt(var + LN_EPS) * lng_ref[...] + lnb_ref[...]
        act_ref[r0:r0 + CONV_CHUNK, :] = (y * _sigmoid(y)).astype(BF16)

    n_chunks = tm // CONV_CHUNK
    per_stage = -(-n_chunks // (len(stages) + 1))
    ci = 0
    for stage in (None,) + stages:
        if stage is not None:
            stage()
        for _ in range(per_stage):
            if ci < n_chunks:
                conv_chunk(ci)
                ci += 1


def _proj(x2d, consts, tm, conv_consts=None):
    n = x2d.shape[0]
    row = lambda w: pl.BlockSpec((tm, w), lambda i: (i, 0))
    in_specs = [row(D_MODEL)] + [_const_spec(c.shape) for c in consts]
    fused = conv_consts is not None
    extra = tuple(conv_consts) if fused else ()
    in_specs += [_const_spec(c.shape) for c in extra]
    scratch = [pltpu.VMEM((CONV_PAD + tm, CONV_WIDTH), BF16),
               pltpu.VMEM((2, CONV_COPIES * CONV_SHROWS, CONV_WIDTH), F32)] if fused else []
    vb = min(tm, ATT_TK)
    out_shape = [
        jax.ShapeDtypeStruct((n, CONV_WIDTH), BF16),
        jax.ShapeDtypeStruct((n, ATTN_WIDTH), BF16),
        jax.ShapeDtypeStruct((n, ATTN_WIDTH), BF16),
        jax.ShapeDtypeStruct((n // vb, ATTN_WIDTH, vb), BF16),
        jax.ShapeDtypeStruct((16, n), F32),
        jax.ShapeDtypeStruct((n, D_MODEL), BF16),
        jax.ShapeDtypeStruct((n, D_MODEL), BF16),
    ]
    out_specs = [row(CONV_WIDTH), row(ATTN_WIDTH), row(ATTN_WIDTH),
                 pl.BlockSpec((tm // vb, ATTN_WIDTH, vb), lambda i: (i, 0, 0)),
                 pl.BlockSpec((16, tm), lambda i: (0, i)), row(D_MODEL), row(D_MODEL)]
    return pl.pallas_call(
        _proj_conv_kernel if fused else _proj_kernel,
        grid=(n // tm,),
        in_specs=in_specs,
        out_specs=out_specs,
        out_shape=out_shape,
        scratch_shapes=scratch,
        compiler_params=pltpu.CompilerParams(dimension_semantics=("arbitrary",), vmem_limit_bytes=VMEM_LIMIT),
        name="proj_conv" if fused else "proj",
    )(x2d, *consts, *extra)


def _conv_shift_matrix():
    rr = jnp.arange(CONV_COPIES * CONV_SHROWS)
    src = rr % CONV_SHROWS + rr // CONV_SHROWS + 1
    return (src[:, None] == jnp.arange(CONV_WIN)[None, :]).astype(BF16)


AUG_TERMS = 3
ACC_ONES = 16


def _split_terms(x):
    hi = x.astype(BF16)
    r1 = x - hi.astype(F32)
    mid = r1.astype(BF16)
    lo = (r1 - mid.astype(F32)).astype(BF16)
    return hi, mid, lo


def _attn_t_kernel(q_ref, k_ref, vt_ref, lf_ref, km_ref, vm_ref, lfm_ref, place_ref, o_ref,
                   aug_scr, augm_scr, m_scr, acc_scr):
    p = pl.program_id(1)
    nk = SEQ // ATT_TK

    def bias_lanes(colsum):
        out = None
        for t, term in enumerate(_split_terms(colsum)):
            d = _dot(term, place_ref[t])
            out = d if out is None else out + d
        return out.astype(BF16)

    @pl.when(p == 0)
    def _():
        r = lax.broadcasted_iota(jnp.int32, (ATT_TK, ATT_TK), 0)
        c = lax.broadcasted_iota(jnp.int32, (ATT_TK, ATT_TK), 1)
        tri = (c <= r).astype(BF16)
        local = []
        for j in range(nk):
            lf = lf_ref[:, j * ATT_TK:(j + 1) * ATT_TK]
            csum = None
            for term in _split_terms(lf):
                d = _dot_nt(tri, term)
                csum = d if csum is None else csum + d
            local.append(csum)
        carry = jnp.zeros((1, 16), F32)
        for j in range(nk):
            csum = local[j] + carry
            aug_scr[j * ATT_TK:(j + 1) * ATT_TK, :] = bias_lanes(csum * (-LOG2E))
            carry = csum[ATT_TK - 1:ATT_TK, :]
        rm = lax.broadcasted_iota(jnp.int32, (META_ROWS, META_ROWS), 0)
        cm = lax.broadcasted_iota(jnp.int32, (META_ROWS, META_ROWS), 1)
        trim = ((cm > rm) & (cm < N_META)).astype(BF16)
        msum = jnp.zeros((META_ROWS, 16), F32)
        for term in _split_terms(lfm_ref[...]):
            msum = msum + _dot_nt(trim, term)
        augm_scr[...] = bias_lanes(msum * LOG2E)

    lane = lax.broadcasted_iota(jnp.int32, (ATT_TQ, LANES), 1)
    q_all = q_ref[...]
    zq = jnp.zeros_like(q_all)
    q_cat = []
    for hl in range(2):
        own = (lane < HEAD_DIM) if hl == 0 else (lane >= HEAD_DIM)
        h = 2 * p + hl
        bias_sel = (lane >= AUG_TERMS * h) & (lane < AUG_TERMS * h + AUG_TERMS)
        bias_ones = jnp.where(bias_sel, 1.0, 0.0).astype(BF16)
        q_cat.append(jnp.concatenate([jnp.where(own, q_all, zq), bias_ones], axis=1))
        m_scr[hl] = jnp.full((1, ATT_TQ), NEG_INF, F32)
        acc_scr[hl] = jnp.zeros((HEAD_DIM + ACC_ONES, ATT_TQ), F32)

    def scores(kb, augb, lo=0):
        k_cat = jnp.concatenate([kb, augb], axis=1)
        return tuple(_dot_nt(k_cat, q_cat[hl][lo:]) for hl in range(2))

    def block_softmax(s2, vtb):
        ones = jnp.ones((ACC_ONES, vtb.shape[1]), BF16)
        out = []
        for hl in range(2):
            s = s2[hl]
            m_blk = jnp.max(s, axis=0, keepdims=True)
            pr = jnp.exp2(s - m_blk).astype(BF16)
            v_own = jnp.concatenate([vtb[hl * HEAD_DIM:(hl + 1) * HEAD_DIM], ones], axis=0)
            out.append((m_blk, _dot(v_own, pr)))
        return out

    def block_scores(j):
        rs = slice(j * ATT_TK, (j + 1) * ATT_TK)
        return scores(k_ref[rs, :], aug_scr[rs, :], lo=j * ATT_TK)

    s_meta = scores(km_ref[0:N_META, :], augm_scr[0:N_META, :])
    s_blocks = [block_scores(j) for j in range(nk)]
    partials = [(0, block_softmax(s_meta, vm_ref[0][:, 0:N_META]))]

    srow = lax.broadcasted_iota(jnp.int32, (ATT_TK, ATT_TK), 0)
    tcol = lax.broadcasted_iota(jnp.int32, (ATT_TK, ATT_TK), 1)
    tri = srow <= tcol
    for j in range(nk):
        s_cur = tuple(jnp.concatenate([jnp.where(tri, s[:, :ATT_TK], NEG_INF), s[:, ATT_TK:]], axis=1)
                      if s.shape[1] > ATT_TK else jnp.where(tri, s, NEG_INF) for s in s_blocks[j])
        partials.append((j * ATT_TK, block_softmax(s_cur, vt_ref[j])))

    for hl in range(2):
        for lo, part in partials:
            m_scr[hl, :, lo:] = jnp.maximum(m_scr[hl, :, lo:], part[hl][0])
        for lo, part in partials:
            m_blk, pv = part[hl]
            acc_scr[hl, :, lo:] = acc_scr[hl, :, lo:] + pv * jnp.exp2(m_blk - m_scr[hl, :, lo:])

    a0 = acc_scr[0]
    a1 = acc_scr[1]
    ot = jnp.concatenate([a0[0:HEAD_DIM] / a0[HEAD_DIM:HEAD_DIM + 1], a1[0:HEAD_DIM] / a1[HEAD_DIM:HEAD_DIM + 1]],
                         axis=0)
    o_ref[...] = ot.T.astype(BF16)


def _attention_t(q, k, vt, lf, km, vm, lfm):
    nk = SEQ // ATT_TK
    hh = jnp.arange(16)[:, None]
    ll = jnp.arange(LANES)[None, :]
    place = jnp.stack([((ll == AUG_TERMS * hh + t) & (hh < N_HEADS)).astype(BF16) for t in range(AUG_TERMS)])
    return pl.pallas_call(
        _attn_t_kernel,
        grid=(BATCH, N_PAIRS),
        in_specs=[
            pl.BlockSpec((ATT_TQ, LANES), lambda b, p: (b, p)),
            pl.BlockSpec((SEQ, LANES), lambda b, p: (b, p)),
            pl.BlockSpec((nk, LANES, ATT_TK), lambda b, p: (b, p, 0)),
            pl.BlockSpec((16, SEQ), lambda b, p: (0, b)),
            pl.BlockSpec((META_ROWS, LANES), lambda b, p: (0, p)),
            pl.BlockSpec((1, LANES, META_ROWS), lambda b, p: (0, p, 0)),
            pl.BlockSpec((16, META_ROWS), lambda b, p: (0, 0)),
            _const_spec(place.shape),
        ],
        out_specs=pl.BlockSpec((ATT_TQ, LANES), lambda b, p: (b, p)),
        out_shape=jax.ShapeDtypeStruct((N_TOK, ATTN_WIDTH), BF16),
        scratch_shapes=[
            pltpu.VMEM((SEQ, LANES), BF16),
            pltpu.VMEM((META_ROWS, LANES), BF16),
            pltpu.VMEM((2, 1, ATT_TQ), F32),
            pltpu.VMEM((2, HEAD_DIM + ACC_ONES, ATT_TQ), F32),
        ],
        compiler_params=pltpu.CompilerParams(
            dimension_semantics=("arbitrary", "arbitrary"), vmem_limit_bytes=VMEM_LIMIT),
        name="attn",
    )(q, k, vt, lf, km, vm, lfm, place)


MERGE_TM = 1024


def _merge_kernel(act_ref, o_ref, sgc_ref, sga_ref, x_ref, wpw_ref, bpw_ref, wao_ref, wout_ref, g2_ref,
                  wrt_ref, brt_ref, tri_ref,
                  h1_ref, hn_ref, route_ref, routet_ref, cnt_ref, base_scr):
    i = pl.program_id(0)

    @pl.when(i == 0)
    def _():
        base_scr[...] = jnp.zeros_like(base_scr)

    tm = x_ref.shape[0]
    hn_parts = []
    for rs in (slice(0, tm // 2), slice(tm // 2, tm)):
        y_conv = _dot(act_ref[rs, :], wpw_ref[...]) + bpw_ref[...]
        y_attn = _dot(o_ref[rs, :], wao_ref[...])
        merged = sgc_ref[rs, :].astype(F32) * y_conv + sga_ref[rs, :].astype(F32) * y_attn
        h1 = x_ref[rs, :] + _dot(merged.astype(BF16), wout_ref[...])
        h1_ref[rs, :] = h1
        ms = jnp.mean(h1 * h1, axis=-1, keepdims=True)
        hn_half = (h1 * lax.rsqrt(ms + RMS_EPS) * g2_ref[...]).astype(BF16)
        hn_ref[rs] = hn_half.reshape(tm // 2, TOK_SUB, LANES)
        hn_parts.append(hn_half)
    hn = jnp.concatenate(hn_parts, axis=0)

    logits = _dot_nt(wrt_ref[...], hn) + brt_ref[...]
    row = lax.broadcasted_iota(jnp.int32, logits.shape, 0).astype(F32)
    big = float(LANES)
    is_g = row < N_GROUPS
    gl = jnp.where(is_g, logits, NEG_INF)
    gmax = jnp.max(gl, axis=0, keepdims=True)
    gidx = jnp.min(jnp.where(gl == gmax, row, big), axis=0, keepdims=True)
    gsum = jnp.sum(jnp.where(is_g, jnp.exp(gl - gmax), 0.0), axis=0, keepdims=True)
    g_w = 1.0 / gsum
    lo = N_GROUPS + EXPERTS_PER_GROUP * gidx
    el = jnp.where((row >= lo) & (row < lo + EXPERTS_PER_GROUP), logits, NEG_INF)
    t1 = jnp.max(el, axis=0, keepdims=True)
    i1 = jnp.min(jnp.where(el == t1, row, big), axis=0, keepdims=True)
    el2 = jnp.where(row == i1, NEG_INF, el)
    t2 = jnp.max(el2, axis=0, keepdims=True)
    i2 = jnp.min(jnp.where(el2 == t2, row, big), axis=0, keepdims=True)
    e2 = jnp.exp(t2 - t1)
    den = 1.0 + e2
    w1 = (1.0 / den) * g_w
    w2 = (e2 / den) * g_w
    ea = i1 - N_GROUPS
    eb = i2 - N_GROUPS

    oha = (row == ea).astype(F32)
    ohb = (row == eb).astype(F32)
    ohs = oha + ohb
    before = _dot(ohs.astype(BF16), tri_ref[...]) + base_scr[...][:, 0:1]
    ra = jnp.sum(before * oha, axis=0, keepdims=True)
    rb = jnp.sum(before * ohb, axis=0, keepdims=True)
    base_scr[...] = base_scr[...] + jnp.sum(ohs, axis=1, keepdims=True)
    cnt_ref[...] = base_scr[...]

    def rows_of(shape):
        r = lax.broadcasted_iota(jnp.int32, shape, 0)
        out = jnp.where(r == 0, ea, 0.0)
        for k, val in enumerate((eb, w1, w2, ra, rb), start=1):
            out = jnp.where(r == k, val, out)
        return out

    routet_ref[...] = rows_of((8, logits.shape[1]))
    route_ref[...] = rows_of(logits.shape).T


def _merge(act, o, sgc, sga, x2d, consts):
    tm = MERGE_TM
    row = lambda w: pl.BlockSpec((tm, w), lambda i: (i, 0))
    return pl.pallas_call(
        _merge_kernel,
        grid=(N_TOK // tm,),
        in_specs=[row(CONV_WIDTH), row(ATTN_WIDTH), row(D_MODEL), row(D_MODEL), row(D_MODEL)]
        + [_const_spec(c.shape) for c in consts],
        out_specs=[row(D_MODEL), pl.BlockSpec((tm, TOK_SUB, LANES), lambda i: (i, 0, 0)), row(LANES),
                   pl.BlockSpec((8, tm), lambda i: (0, i)), _const_spec((LANES, LANES))],
        out_shape=[jax.ShapeDtypeStruct((N_TOK, D_MODEL), F32),
                   jax.ShapeDtypeStruct((N_TOK, TOK_SUB, LANES), BF16),
                   jax.ShapeDtypeStruct((N_TOK, LANES), F32),
                   jax.ShapeDtypeStruct((8, N_TOK), F32),
                   jax.ShapeDtypeStruct((LANES, LANES), F32)],
        scratch_shapes=[pltpu.VMEM((LANES, LANES), F32)],
        compiler_params=pltpu.CompilerParams(dimension_semantics=("arbitrary",), vmem_limit_bytes=VMEM_LIMIT),
        name="merge",
    )(act, o, sgc, sga, x2d, *consts)


DISP_TM = 2048
DMA_UNROLL = 8


def _dest_kernel(pstart_ref, rt_ref, d_ref):
    rt = rt_ref[...]
    start = jnp.zeros_like(rt)
    for e in range(N_EXPERTS):
        start = jnp.where(rt == float(e), pstart_ref[e].astype(F32), start)
    d_ref[...] = (start + pltpu.roll(rt, 4, axis=0)).astype(jnp.int32)


def _dest(pstart, route_t):
    grid_spec = pltpu.PrefetchScalarGridSpec(
        num_scalar_prefetch=1,
        grid=(1,),
        in_specs=[pl.BlockSpec(route_t.shape, lambda i, *_: (0, 0))],
        out_specs=pl.BlockSpec(route_t.shape, lambda i, *_: (0, 0)),
    )
    return pl.pallas_call(
        _dest_kernel,
        grid_spec=grid_spec,
        out_shape=jax.ShapeDtypeStruct(route_t.shape, jnp.int32),
        compiler_params=pltpu.CompilerParams(dimension_semantics=("arbitrary",), vmem_limit_bytes=VMEM_LIMIT),
        name="dest",
    )(pstart, route_t)


ZERO_CHUNK = 64


def _dispatch_kernel(d0_ref, d1_ref, pfill_ref, pend_ref, hn_ref, xin_ref, zero_scr, sem, zsem):
    i = pl.program_id(0)
    base = i * DISP_TM

    @pl.when(i == 0)
    def _():
        zero_scr[...] = jnp.zeros_like(zero_scr)

        def pad_chunks(e, fn):
            first = pfill_ref[e] // ZERO_CHUNK * ZERO_CHUNK

            def one(c, carry):
                start = pl.multiple_of(first + c * ZERO_CHUNK, ZERO_CHUNK)
                fn(pltpu.make_async_copy(zero_scr.at[pl.ds(0, ZERO_CHUNK)], xin_ref.at[pl.ds(start, ZERO_CHUNK)],
                                         zsem.at[0]))
                return carry

            lax.fori_loop(0, (pend_ref[e] - first) // ZERO_CHUNK, one, 0)

        def zstart(e, carry):
            pad_chunks(e, lambda c: c.start())
            return carry

        def zwait(e, carry):
            pad_chunks(e, lambda c: c.wait())
            return carry

        n_used = pend_ref[N_EXPERTS - 1] // ROW_BLOCK

        def tcopy(b):
            return pltpu.make_async_copy(
                zero_scr, xin_ref.at[pl.ds(pl.multiple_of(b * ROW_BLOCK, ROW_BLOCK), ROW_BLOCK)], zsem.at[0])

        def tstart(b, carry):
            tcopy(b).start()
            return carry

        def twait(b, carry):
            tcopy(b).wait()
            return carry

        lax.fori_loop(0, N_EXPERTS, zstart, 0)
        lax.fori_loop(n_used, N_BLOCKS, tstart, 0)
        lax.fori_loop(0, N_EXPERTS, zwait, 0)
        lax.fori_loop(n_used, N_BLOCKS, twait, 0)

    def start(r, carry):
        pltpu.make_async_copy(hn_ref.at[r], xin_ref.at[d0_ref[base + r]], sem.at[0]).start()
        pltpu.make_async_copy(hn_ref.at[r], xin_ref.at[d1_ref[base + r]], sem.at[1]).start(priority=1)
        return carry

    lax.fori_loop(0, DISP_TM, start, 0, unroll=DMA_UNROLL)
    pltpu.make_async_copy(hn_ref, xin_ref.at[pl.ds(0, DISP_TM)], sem.at[0]).wait()
    pltpu.make_async_copy(hn_ref, xin_ref.at[pl.ds(0, DISP_TM)], sem.at[1]).wait()


def _dispatch(d0, d1, pfill, pend, hn):
    grid_spec = pltpu.PrefetchScalarGridSpec(
        num_scalar_prefetch=4,
        grid=(N_TOK // DISP_TM,),
        in_specs=[pl.BlockSpec((DISP_TM, TOK_SUB, LANES), lambda i, *_: (i, 0, 0))],
        out_specs=pl.BlockSpec(memory_space=pl.ANY),
        scratch_shapes=[pltpu.VMEM((ROW_BLOCK, TOK_SUB, LANES), BF16),
                        pltpu.SemaphoreType.DMA((2,)), pltpu.SemaphoreType.DMA((1,))],
    )
    return pl.pallas_call(
        _dispatch_kernel,
        grid_spec=grid_spec,
        out_shape=jax.ShapeDtypeStruct((P_ROWS, TOK_SUB, LANES), BF16),
        compiler_params=pltpu.CompilerParams(dimension_semantics=("arbitrary",), vmem_limit_bytes=VMEM_LIMIT),
        name="dispatch",
    )(d0, d1, pfill, pend, hn)


W_SLOTS = 3


def _expert_kernel(nu_ref, first_ref, ord_ref, used_ref, nue_ref, x_hbm, wg_hbm, wu_hbm, wd_hbm, y_hbm,
                   x_buf, y_buf, wg_f, wu_f, wd_f, wg_b, wu_b, wd_b, xsem, ysem, wsem):
    n = nu_ref[0]

    def rows(i):
        return pl.ds(pl.multiple_of(i * ROW_BLOCK, ROW_BLOCK), ROW_BLOCK)

    def xcopy(i, slot):
        return pltpu.make_async_copy(x_hbm.at[rows(i)], x_buf.at[slot], xsem.at[slot])

    def ycopy(i, slot):
        return pltpu.make_async_copy(y_buf.at[slot], y_hbm.at[rows(i)], ysem.at[slot])

    def fetch(k):
        e = used_ref[k]
        slot = k % W_SLOTS
        return (pltpu.make_async_copy(wg_hbm.at[e], wg_f.at[slot], wsem.at[slot, 0]),
                pltpu.make_async_copy(wu_hbm.at[e], wu_f.at[slot], wsem.at[slot, 1]),
                pltpu.make_async_copy(wd_hbm.at[e], wd_f.at[slot], wsem.at[slot, 2]))

    for k in range(W_SLOTS - 1):
        @pl.when(k < nue_ref[0])
        def _():
            for c in fetch(k):
                c.start()
    xcopy(0, 0).start()

    def block(i, carry):
        slot = i % 2
        xcopy(i, slot).wait()

        @pl.when(i + 1 < n)
        def _():
            xcopy(i + 1, 1 - slot).start()

        @pl.when(first_ref[i] == 1)
        def _():
            k = ord_ref[i]
            for c in fetch(k):
                c.wait()

            @pl.when(k + (W_SLOTS - 1) < nue_ref[0])
            def _():
                for c in fetch(k + (W_SLOTS - 1)):
                    c.start()

            wslot = k % W_SLOTS
            wg_b[...] = wg_f[wslot].astype(BF16)
            wu_b[...] = wu_f[wslot].astype(BF16)
            wd_b[...] = wd_f[wslot].astype(BF16)

        @pl.when(i >= 2)
        def _():
            ycopy(i - 2, slot).wait()

        x = x_buf[slot].reshape(ROW_BLOCK, D_MODEL)
        g = _dot(x, wg_b[...])
        u = _dot(x, wu_b[...])
        hmid = (g * _sigmoid(g) * u).astype(BF16)
        y = _dot(hmid, wd_b[...]).astype(BF16)
        y_buf[slot] = y.reshape(ROW_BLOCK, TOK_SUB, LANES)
        ycopy(i, slot).start()
        return carry

    lax.fori_loop(0, n, block, 0)

    @pl.when(n >= 2)
    def _():
        ycopy(n - 2, n % 2).wait()
    ycopy(n - 1, (n - 1) % 2).wait()

    y_buf[0] = jnp.zeros((ROW_BLOCK, TOK_SUB, LANES), BF16)

    def zcopy(b):
        return pltpu.make_async_copy(y_buf.at[0], y_hbm.at[rows(b)], ysem.at[0])

    def zstart(b, carry):
        zcopy(b).start()
        return carry

    def zwait(b, carry):
        zcopy(b).wait()
        return carry

    lax.fori_loop(n, N_BLOCKS, zstart, 0)
    lax.fori_loop(n, N_BLOCKS, zwait, 0)


def _experts(n_used, blk_first, blk_ord, used_e, n_used_e, xin, w_gate, w_up, w_down):
    grid_spec = pltpu.PrefetchScalarGridSpec(
        num_scalar_prefetch=5,
        grid=(1,),
        in_specs=[pl.BlockSpec(memory_space=pl.ANY),
                  pl.BlockSpec(memory_space=pl.ANY),
                  pl.BlockSpec(memory_space=pl.ANY),
                  pl.BlockSpec(memory_space=pl.ANY)],
        out_specs=pl.BlockSpec(memory_space=pl.ANY),
        scratch_shapes=[pltpu.VMEM((2, ROW_BLOCK, TOK_SUB, LANES), BF16),
                        pltpu.VMEM((2, ROW_BLOCK, TOK_SUB, LANES), BF16),
                        pltpu.VMEM((W_SLOTS, D_MODEL, D_EXPERT), F32),
                        pltpu.VMEM((W_SLOTS, D_MODEL, D_EXPERT), F32),
                        pltpu.VMEM((W_SLOTS, D_EXPERT, D_MODEL), F32),
                        pltpu.VMEM((D_MODEL, D_EXPERT), BF16),
                        pltpu.VMEM((D_MODEL, D_EXPERT), BF16),
                        pltpu.VMEM((D_EXPERT, D_MODEL), BF16),
                        pltpu.SemaphoreType.DMA((2,)), pltpu.SemaphoreType.DMA((2,)),
                        pltpu.SemaphoreType.DMA((W_SLOTS, 3))],
    )
    return pl.pallas_call(
        _expert_kernel,
        grid_spec=grid_spec,
        out_shape=jax.ShapeDtypeStruct((P_ROWS, TOK_SUB, LANES), BF16),
        compiler_params=pltpu.CompilerParams(dimension_semantics=("arbitrary",), vmem_limit_bytes=VMEM_LIMIT),
        name="experts",
    )(n_used, blk_first, blk_ord, used_e, n_used_e, xin, w_gate, w_up, w_down)


COMB_TM = 512


def _combine_kernel(d0_ref, d1_ref, h1_ref, route_ref, yb_ref, out_ref, g_scr, sem):
    i = pl.program_id(0)
    n = pl.num_programs(0)

    def issue(tile, slot):
        base = tile * COMB_TM

        def start(r, carry):
            pltpu.make_async_copy(yb_ref.at[d0_ref[base + r]], g_scr.at[slot, 0, r], sem.at[slot, 0]).start()
            pltpu.make_async_copy(yb_ref.at[d1_ref[base + r]], g_scr.at[slot, 1, r], sem.at[slot, 1]).start(
                priority=1)
            return carry

        lax.fori_loop(0, COMB_TM, start, 0, unroll=DMA_UNROLL)

    @pl.when(i == 0)
    def _():
        issue(0, 0)

    slot = i % 2

    @pl.when(i + 1 < n)
    def _():
        issue(i + 1, 1 - slot)

    for k in range(2):
        pltpu.make_async_copy(yb_ref.at[pl.ds(0, COMB_TM)], g_scr.at[slot, k], sem.at[slot, k]).wait()

    route = route_ref[...]
    w0 = route[:, 2:3]
    w1 = route[:, 3:4]
    g0 = g_scr[slot, 0].reshape(COMB_TM, D_MODEL).astype(F32)
    g1 = g_scr[slot, 1].reshape(COMB_TM, D_MODEL).astype(F32)
    out_ref[...] = h1_ref[...] + (g0 * w0 + g1 * w1)


def _combine(d0, d1, h1, route, yb):
    grid_spec = pltpu.PrefetchScalarGridSpec(
        num_scalar_prefetch=2,
        grid=(N_TOK // COMB_TM,),
        in_specs=[pl.BlockSpec((COMB_TM, D_MODEL), lambda i, *_: (i, 0)),
                  pl.BlockSpec((COMB_TM, LANES), lambda i, *_: (i, 0)),
                  pl.BlockSpec(memory_space=pl.ANY)],
        out_specs=pl.BlockSpec((COMB_TM, D_MODEL), lambda i, *_: (i, 0)),
        scratch_shapes=[pltpu.VMEM((2, 2, COMB_TM, TOK_SUB, LANES), BF16),
                        pltpu.SemaphoreType.DMA((2, 2))],
    )
    return pl.pallas_call(
        _combine_kernel,
        grid_spec=grid_spec,
        out_shape=jax.ShapeDtypeStruct((N_TOK, D_MODEL), F32),
        compiler_params=pltpu.CompilerParams(dimension_semantics=("arbitrary",), vmem_limit_bytes=VMEM_LIMIT),
        name="combine",
    )(d0, d1, h1, route, yb)


def kernel(x, meta, norm1_g, w_in, b_in, dw_w, dw_b, conv_ln_g, conv_ln_b, w_conv_out, b_conv_out,
           q_norm_g, k_norm_g, b_forget, w_attn_out, w_out, norm2_g, w_group, b_group, w_router,
           b_router, w_gate, w_up, w_down):
    l = 0
    row = lambda v: v.reshape(1, -1).astype(F32)

    c_u, c_q, c_f = 2 * CONV_WIDTH, 2 * CONV_WIDTH, 2 * CONV_WIDTH + 3 * ATTN_WIDTH
    c_gc = c_f + N_HEADS
    c_ga = c_gc + D_MODEL
    wi, bi = w_in[l], b_in[l]
    wu, bu = wi[:, :c_u].astype(BF16), row(bi[:c_u])
    wqkv, bqkv = wi[:, c_q:c_f].astype(BF16), row(bi[c_q:c_f])
    wft = jnp.zeros((16, D_MODEL), F32).at[:N_HEADS].set(wi[:, c_f:c_gc].T).astype(BF16)
    bft = jnp.zeros((16, 1), F32).at[:N_HEADS, 0].set(bi[c_f:c_gc] + b_forget[l])
    wgc, bgc = wi[:, c_gc:c_ga].astype(BF16), row(bi[c_gc:c_ga])
    wga, bga = wi[:, c_ga:].astype(BF16), row(bi[c_ga:])
    qg = row(jnp.tile(q_norm_g[l], N_HEADS) * (HEAD_DIM ** -0.5 * LOG2E))
    kg = row(jnp.tile(k_norm_g[l], N_HEADS))
    head_of_col = jnp.arange(ATTN_WIDTH) // HEAD_DIM
    hsum = (head_of_col[:, None] == jnp.arange(LANES)[None, :]).astype(BF16)
    hexp_half = (jnp.arange(LANES)[:, None] == head_of_col[None, :]).astype(BF16)
    hexp = jnp.concatenate([hexp_half, hexp_half], axis=0)
    proj_consts = (row(norm1_g[l]), wu, bu, wqkv, bqkv, wft, bft, wgc, bgc, wga, bga, qg, kg, hsum, hexp)

    x2d = x.reshape(N_TOK, D_MODEL)
    meta_pad = jnp.zeros((META_ROWS, D_MODEL), F32).at[:N_META].set(meta.astype(F32))
    a_m, _, k_m, v_m, lf_m, _, _ = _proj(meta_pad, proj_consts, META_ROWS)
    conv_consts = (a_m, _conv_shift_matrix(), dw_w[l].astype(F32), row(dw_b[l]),
                   row(conv_ln_g[l]), row(conv_ln_b[l]))
    act, q, k, v, lf, sgc, sga = _proj(x2d, proj_consts, PROJ_TM, conv_consts)

    o = _attention_t(q, k, v, lf, k_m, v_m, lf_m)

    wrt = jnp.zeros((LANES, D_MODEL), F32).at[:N_GROUPS].set(w_group[l].T)
    wrt = wrt.at[N_GROUPS:N_GROUPS + N_EXPERTS].set(w_router[l].T).astype(BF16)
    brt = jnp.zeros((LANES, 1), F32).at[:N_GROUPS, 0].set(b_group[l])
    brt = brt.at[N_GROUPS:N_GROUPS + N_EXPERTS, 0].set(b_router[l])
    tri = jnp.triu(jnp.ones((MERGE_TM, MERGE_TM), F32), 1).astype(BF16)
    merge_consts = (w_conv_out[l].astype(BF16), row(b_conv_out[l]), w_attn_out[l].astype(BF16),
                    w_out[l].astype(BF16), row(norm2_g[l]), wrt, brt, tri)
    h1, hn, route, route_t, cnt = _merge(act, o, sgc, sga, x2d, merge_consts)

    counts = cnt[:N_EXPERTS, 0].astype(jnp.int32)
    padded = (counts + ROW_BLOCK - 1) // ROW_BLOCK * ROW_BLOCK
    pend = jnp.cumsum(padded).astype(jnp.int32)
    pstart = pend - padded
    blk_row = jnp.arange(N_BLOCKS, dtype=jnp.int32) * ROW_BLOCK
    n_used = pend[-1:] // ROW_BLOCK
    nz = counts > 0
    before = (pend[None, :] <= blk_row[:, None]) & nz[None, :]
    blk_ord = jnp.sum(before.astype(jnp.int32), axis=1)
    blk_first = jnp.any((pstart[None, :] == blk_row[:, None]) & nz[None, :], axis=1).astype(jnp.int32)
    e_ord = jnp.cumsum(nz.astype(jnp.int32)) - 1
    e_ids = jnp.arange(N_EXPERTS, dtype=jnp.int32)
    used_e = jnp.sum(jnp.where((e_ord[None, :] == e_ids[:, None]) & nz[None, :], e_ids[None, :], 0), axis=1)
    n_used_e = jnp.sum(nz.astype(jnp.int32)).reshape(1)

    dest = _dest(pstart, route_t)
    d0, d1 = dest[0], dest[1]
    xin = _dispatch(d0, d1, pstart + counts, pend, hn)
    yb = _experts(n_used, blk_first, blk_ord, used_e.astype(jnp.int32), n_used_e, xin,
                  w_gate[l], w_up[l], w_down[l])
    out = _combine(d0, d1, h1, route, yb)
    return out.reshape(BATCH, SEQ, D_MODEL)
```

```python
import jax
import jax.numpy as jnp
from jax import lax
from jax.experimental import pallas as pl
from jax.experimental.pallas import tpu as pltpu

D_MODEL = 1024
BATCH = 8
SEQ = 2048
N_META = 16
CONV_WIDTH = 512
CONV_K = 31
N_HEADS = 8
HEAD_DIM = 64
ATTN_WIDTH = N_HEADS * HEAD_DIM
N_GROUPS = 8
EXPERTS_PER_GROUP = 8
N_EXPERTS = N_GROUPS * EXPERTS_PER_GROUP
D_EXPERT = 256
RMS_EPS = 1e-6
LN_EPS = 1e-5

N_TOK = BATCH * SEQ
LANES = 128
META_ROWS = 128
ROW_BLOCK = 512
N_ASSIGN = 2 * N_TOK
N_BLOCKS = N_ASSIGN // ROW_BLOCK + N_EXPERTS
P_ROWS = N_BLOCKS * ROW_BLOCK
TOK_SUB = D_MODEL // LANES
PROJ_TM = 1024
CONV_CHUNK = 128
CONV_PAD = 32
CONV_WIN = CONV_CHUNK + 32
CONV_SHROWS = CONV_CHUNK + 24
CONV_COPIES = 7
CONV_LANES = 128
ATT_TQ = SEQ
ATT_TK = 256
N_PAIRS = N_HEADS // 2
VMEM_LIMIT = 56 * 1024 * 1024

F32 = jnp.float32
BF16 = jnp.bfloat16
NEG_INF = float("-inf")
LOG2E = 1.4426950408889634


def _dot(a, b):
    return jnp.dot(a, b, preferred_element_type=F32)


def _dot_nt(a, b):
    return lax.dot_general(a, b, (((1,), (1,)), ((), ())), preferred_element_type=F32)


def _sigmoid(x):
    return 1.0 / (1.0 + jnp.exp(-x))


def _const_spec(shape):
    nd = len(shape)
    return pl.BlockSpec(shape, lambda *_: (0,) * nd)


def _head_rms(t, hsum_ref, hexp_ref, gain):
    ss = _dot((t * t).astype(BF16), hsum_ref[...])
    inv = lax.rsqrt(ss * (1.0 / HEAD_DIM) + RMS_EPS)
    hi = inv.astype(BF16)
    lo = (inv - hi.astype(F32)).astype(BF16)
    invb = _dot(jnp.concatenate([hi, lo], axis=1), hexp_ref[...])
    return t * invb * gain


N_PROJ_IN = 16


def _proj_stages(x_ref, g1_ref, wu_ref, bu_ref, wqkv_ref, bqkv_ref, wft_ref, bft_ref, wgc_ref, bgc_ref,
                 wga_ref, bga_ref, qg_ref, kg_ref, hsum_ref, hexp_ref,
                 q_ref, k_ref, v_ref, lf_ref, sgc_ref, sga_ref):
    x = x_ref[...]
    ms = jnp.mean(x * x, axis=-1, keepdims=True)
    xn = (x * lax.rsqrt(ms + RMS_EPS) * g1_ref[...]).astype(BF16)

    def glu():
        u = _dot(xn, wu_ref[...]) + bu_ref[...]
        return (u[:, :CONV_WIDTH] * _sigmoid(u[:, CONV_WIDTH:])).astype(BF16)

    def qkv_stage():
        qkv = _dot(xn, wqkv_ref[...]) + bqkv_ref[...]
        q_ref[...] = _head_rms(qkv[:, :ATTN_WIDTH], hsum_ref, hexp_ref, qg_ref[...]).astype(BF16)
        k_ref[...] = _head_rms(qkv[:, ATTN_WIDTH:2 * ATTN_WIDTH], hsum_ref, hexp_ref, kg_ref[...]).astype(BF16)
        v = qkv[:, 2 * ATTN_WIDTH:]
        vb = v_ref.shape[2]
        for c in range(v_ref.shape[0]):
            v_ref[c] = v[c * vb:(c + 1) * vb, :].T.astype(BF16)

    def gate_stage_c():
        f = _dot_nt(wft_ref[...], xn) + bft_ref[...]
        lf_ref[...] = jnp.minimum(f, 0.0) - jnp.log(1.0 + jnp.exp(-jnp.abs(f)))
        sgc_ref[...] = _sigmoid(_dot(xn, wgc_ref[...]) + bgc_ref[...]).astype(BF16)

    def gate_stage_a():
        sga_ref[...] = _sigmoid(_dot(xn, wga_ref[...]) + bga_ref[...]).astype(BF16)

    return glu, (qkv_stage, gate_stage_c, gate_stage_a)


def _proj_kernel(*refs):
    ins, a_ref, outs = refs[:N_PROJ_IN], refs[N_PROJ_IN], refs[N_PROJ_IN + 1:]
    glu, stages = _proj_stages(*ins, *outs)
    a_ref[...] = glu()
    for stage in stages:
        stage()


def _proj_conv_kernel(*refs):
    ins = refs[:N_PROJ_IN]
    am_ref, shift_ref, dw_ref, dwb_ref, lng_ref, lnb_ref = refs[N_PROJ_IN:N_PROJ_IN + 6]
    act_ref, outs = refs[N_PROJ_IN + 6], refs[N_PROJ_IN + 7:N_PROJ_IN + 13]
    win_scr, sh_scr = refs[N_PROJ_IN + 13:]
    tm = act_ref.shape[0]
    i = pl.program_id(0)
    first = (i % (SEQ // tm)) == 0

    @pl.when(first)
    def _():
        win_scr[0:N_META, :] = jnp.zeros((N_META, CONV_WIDTH), BF16)
        win_scr[N_META:CONV_PAD, :] = am_ref[0:N_META, :]

    @pl.when(jnp.logical_not(first))
    def _():
        win_scr[0:CONV_PAD, :] = win_scr[tm:tm + CONV_PAD, :]

    glu, stages = _proj_stages(*ins, *outs)
    win_scr[CONV_PAD:CONV_PAD + tm, :] = glu()

    def conv_chunk(ci):
        r0 = ci * CONV_CHUNK
        slot = ci % 2
        sh_scr[slot] = _dot(shift_ref[...], win_scr[r0:r0 + CONV_WIN, :])
        parts = []
        for c0 in range(0, CONV_WIDTH, CONV_LANES):
            cs = slice(c0, c0 + CONV_LANES)
            part = jnp.zeros((CONV_CHUNK, CONV_LANES), F32) + dwb_ref[:, cs]
            for b in range(8):
                if b == 0:
                    rows = win_scr[r0:r0 + CONV_WIN, cs].astype(F32)
                else:
                    rows = sh_scr[slot, (b - 1) * CONV_SHROWS:b * CONV_SHROWS, cs]
                for j in range(CONV_K):
                    off = j + 2
                    if off % 8 == b:
                        part = part + rows[off - b:off - b + CONV_CHUNK] * dw_ref[j:j + 1, cs]
            parts.append(part)
        acc = jnp.concatenate(parts, axis=1)
        mu = jnp.mean(acc, axis=-1, keepdims=True)
        d = acc - mu
        var = jnp.mean(d * d, axis=-1, keepdims=True)
        y = d * lax.rsqrt(var + LN_EPS) * lng_ref[...] + lnb_ref[...]
        act_ref[r0:r0 + CONV_CHUNK, :] = (y * _sigmoid(y)).astype(BF16)

    n_chunks = tm // CONV_CHUNK
    per_stage = -(-n_chunks // (len(stages) + 1))
    ci = 0
    for stage in (None,) + stages:
        if stage is not None:
            stage()
        for _ in range(per_stage):
            if ci < n_chunks:
                conv_chunk(ci)
                ci += 1


def _proj(x2d, consts, tm, conv_consts=None):
    n = x2d.shape[0]
    row = lambda w: pl.BlockSpec((tm, w), lambda i: (i, 0))
    in_specs = [row(D_MODEL)] + [_const_spec(c.shape) for c in consts]
    fused = conv_consts is not None
    extra = tuple(conv_consts) if fused else ()
    in_specs += [_const_spec(c.shape) for c in extra]
    scratch = [pltpu.VMEM((CONV_PAD + tm, CONV_WIDTH), BF16),
               pltpu.VMEM((2, CONV_COPIES * CONV_SHROWS, CONV_WIDTH), F32)] if fused else []
    vb = min(tm, ATT_TK)
    out_shape = [
        jax.ShapeDtypeStruct((n, CONV_WIDTH), BF16),
        jax.ShapeDtypeStruct((n, ATTN_WIDTH), BF16),
        jax.ShapeDtypeStruct((n, ATTN_WIDTH), BF16),
        jax.ShapeDtypeStruct((n // vb, ATTN_WIDTH, vb), BF16),
        jax.ShapeDtypeStruct((16, n), F32),
        jax.ShapeDtypeStruct((n, D_MODEL), BF16),
        jax.ShapeDtypeStruct((n, D_MODEL), BF16),
    ]
    out_specs = [row(CONV_WIDTH), row(ATTN_WIDTH), row(ATTN_WIDTH),
                 pl.BlockSpec((tm // vb, ATTN_WIDTH, vb), lambda i: (i, 0, 0)),
                 pl.BlockSpec((16, tm), lambda i: (0, i)), row(D_MODEL), row(D_MODEL)]
    return pl.pallas_call(
        _proj_conv_kernel if fused else _proj_kernel,
        grid=(n // tm,),
        in_specs=in_specs,
        out_specs=out_specs,
        out_shape=out_shape,
        scratch_shapes=scratch,
        compiler_params=pltpu.CompilerParams(dimension_semantics=("arbitrary",), vmem_limit_bytes=VMEM_LIMIT),
        name="proj_conv" if fused else "proj",
    )(x2d, *consts, *extra)


def _conv_shift_matrix():
    rr = jnp.arange(CONV_COPIES * CONV_SHROWS)
    src = rr % CONV_SHROWS + rr // CONV_SHROWS + 1
    return (src[:, None] == jnp.arange(CONV_WIN)[None, :]).astype(BF16)


AUG_TERMS = 3
ACC_ONES = 16


def _split_terms(x):
    hi = x.astype(BF16)
    r1 = x - hi.astype(F32)
    mid = r1.astype(BF16)
    lo = (r1 - mid.astype(F32)).astype(BF16)
    return hi, mid, lo


def _attn_t_kernel(q_ref, k_ref, vt_ref, lf_ref, km_ref, vm_ref, lfm_ref, place_ref, o_ref,
                   aug_scr, augm_scr, m_scr, acc_scr):
    p = pl.program_id(1)
    nk = SEQ // ATT_TK

    def bias_lanes(colsum):
        out = None
        for t, term in enumerate(_split_terms(colsum)):
            d = _dot(term, place_ref[t])
            out = d if out is None else out + d
        return out.astype(BF16)

    @pl.when(p == 0)
    def _():
        r = lax.broadcasted_iota(jnp.int32, (ATT_TK, ATT_TK), 0)
        c = lax.broadcasted_iota(jnp.int32, (ATT_TK, ATT_TK), 1)
        tri = (c <= r).astype(BF16)
        local = []
        for j in range(nk):
            lf = lf_ref[:, j * ATT_TK:(j + 1) * ATT_TK]
            csum = None
            for term in _split_terms(lf):
                d = _dot_nt(tri, term)
                csum = d if csum is None else csum + d
            local.append(csum)
        carry = jnp.zeros((1, 16), F32)
        for j in range(nk):
            csum = local[j] + carry
            aug_scr[j * ATT_TK:(j + 1) * ATT_TK, :] = bias_lanes(csum * (-LOG2E))
            carry = csum[ATT_TK - 1:ATT_TK, :]
        rm = lax.broadcasted_iota(jnp.int32, (META_ROWS, META_ROWS), 0)
        cm = lax.broadcasted_iota(jnp.int32, (META_ROWS, META_ROWS), 1)
        trim = ((cm > rm) & (cm < N_META)).astype(BF16)
        msum = jnp.zeros((META_ROWS, 16), F32)
        for term in _split_terms(lfm_ref[...]):
            msum = msum + _dot_nt(trim, term)
        augm_scr[...] = bias_lanes(msum * LOG2E)

    lane = lax.broadcasted_iota(jnp.int32, (ATT_TQ, LANES), 1)
    q_all = q_ref[...]
    zq = jnp.zeros_like(q_all)
    q_cat = []
    for hl in range(2):
        own = (lane < HEAD_DIM) if hl == 0 else (lane >= HEAD_DIM)
        h = 2 * p + hl
        bias_sel = (lane >= AUG_TERMS * h) & (lane < AUG_TERMS * h + AUG_TERMS)
        bias_ones = jnp.where(bias_sel, 1.0, 0.0).astype(BF16)
        q_cat.append(jnp.concatenate([jnp.where(own, q_all, zq), bias_ones], axis=1))
        m_scr[hl] = jnp.full((1, ATT_TQ), NEG_INF, F32)
        acc_scr[hl] = jnp.zeros((HEAD_DIM + ACC_ONES, ATT_TQ), F32)

    def scores(kb, augb, lo=0):
        k_cat = jnp.concatenate([kb, augb], axis=1)
        return tuple(_dot_nt(k_cat, q_cat[hl][lo:]) for hl in range(2))

    def block_softmax(s2, vtb):
        ones = jnp.ones((ACC_ONES, vtb.shape[1]), BF16)
        out = []
        for hl in range(2):
            s = s2[hl]
            m_blk = jnp.max(s, axis=0, keepdims=True)
            pr = jnp.exp2(s - m_blk).astype(BF16)
            v_own = jnp.concatenate([vtb[hl * HEAD_DIM:(hl + 1) * HEAD_DIM], ones], axis=0)
            out.append((m_blk, _dot(v_own, pr)))
        return out

    def block_scores(j):
        rs = slice(j * ATT_TK, (j + 1) * ATT_TK)
        return scores(k_ref[rs, :], aug_scr[rs, :], lo=j * ATT_TK)

    s_meta = scores(km_ref[0:N_META, :], augm_scr[0:N_META, :])
    s_blocks = [block_scores(j) for j in range(nk)]
    partials = [(0, block_softmax(s_meta, vm_ref[0][:, 0:N_META]))]

    srow = lax.broadcasted_iota(jnp.int32, (ATT_TK, ATT_TK), 0)
    tcol = lax.broadcasted_iota(jnp.int32, (ATT_TK, ATT_TK), 1)
    tri = srow <= tcol
    for j in range(nk):
        s_cur = tuple(jnp.concatenate([jnp.where(tri, s[:, :ATT_TK], NEG_INF), s[:, ATT_TK:]], axis=1)
                      if s.shape[1] > ATT_TK else jnp.where(tri, s, NEG_INF) for s in s_blocks[j])
        partials.append((j * ATT_TK, block_softmax(s_cur, vt_ref[j])))

    for hl in range(2):
        for lo, part in partials:
            m_scr[hl, :, lo:] = jnp.maximum(m_scr[hl, :, lo:], part[hl][0])
        for lo, part in partials:
            m_blk, pv = part[hl]
            acc_scr[hl, :, lo:] = acc_scr[hl, :, lo:] + pv * jnp.exp2(m_blk - m_scr[hl, :, lo:])

    a0 = acc_scr[0]
    a1 = acc_scr[1]
    ot = jnp.concatenate([a0[0:HEAD_DIM] / a0[HEAD_DIM:HEAD_DIM + 1], a1[0:HEAD_DIM] / a1[HEAD_DIM:HEAD_DIM + 1]],
                         axis=0)
    o_ref[...] = ot.T.astype(BF16)


def _attention_t(q, k, vt, lf, km, vm, lfm):
    nk = SEQ // ATT_TK
    hh = jnp.arange(16)[:, None]
    ll = jnp.arange(LANES)[None, :]
    place = jnp.stack([((ll == AUG_TERMS * hh + t) & (hh < N_HEADS)).astype(BF16) for t in range(AUG_TERMS)])
    return pl.pallas_call(
        _attn_t_kernel,
        grid=(BATCH, N_PAIRS),
        in_specs=[
            pl.BlockSpec((ATT_TQ, LANES), lambda b, p: (b, p)),
            pl.BlockSpec((SEQ, LANES), lambda b, p: (b, p)),
            pl.BlockSpec((nk, LANES, ATT_TK), lambda b, p: (b, p, 0)),
            pl.BlockSpec((16, SEQ), lambda b, p: (0, b)),
            pl.BlockSpec((META_ROWS, LANES), lambda b, p: (0, p)),
            pl.BlockSpec((1, LANES, META_ROWS), lambda b, p: (0, p, 0)),
            pl.BlockSpec((16, META_ROWS), lambda b, p: (0, 0)),
            _const_spec(place.shape),
        ],
        out_specs=pl.BlockSpec((ATT_TQ, LANES), lambda b, p: (b, p)),
        out_shape=jax.ShapeDtypeStruct((N_TOK, ATTN_WIDTH), BF16),
        scratch_shapes=[
            pltpu.VMEM((SEQ, LANES), BF16),
            pltpu.VMEM((META_ROWS, LANES), BF16),
            pltpu.VMEM((2, 1, ATT_TQ), F32),
            pltpu.VMEM((2, HEAD_DIM + ACC_ONES, ATT_TQ), F32),
        ],
        compiler_params=pltpu.CompilerParams(
            dimension_semantics=("arbitrary", "arbitrary"), vmem_limit_bytes=VMEM_LIMIT),
        name="attn",
    )(q, k, vt, lf, km, vm, lfm, place)


MERGE_TM = 1024


def _merge_kernel(act_ref, o_ref, sgc_ref, sga_ref, x_ref, wpw_ref, bpw_ref, wao_ref, wout_ref, g2_ref,
                  wrt_ref, brt_ref, tri_ref,
                  h1_ref, hn_ref, route_ref, routet_ref, cnt_ref, base_scr):
    i = pl.program_id(0)

    @pl.when(i == 0)
    def _():
        base_scr[...] = jnp.zeros_like(base_scr)

    tm = x_ref.shape[0]
    hn_parts = []
    for rs in (slice(0, tm // 2), slice(tm // 2, tm)):
        y_conv = _dot(act_ref[rs, :], wpw_ref[...]) + bpw_ref[...]
        y_attn = _dot(o_ref[rs, :], wao_ref[...])
        merged = sgc_ref[rs, :].astype(F32) * y_conv + sga_ref[rs, :].astype(F32) * y_attn
        h1 = x_ref[rs, :] + _dot(merged.astype(BF16), wout_ref[...])
        h1_ref[rs, :] = h1
        ms = jnp.mean(h1 * h1, axis=-1, keepdims=True)
        hn_half = (h1 * lax.rsqrt(ms + RMS_EPS) * g2_ref[...]).astype(BF16)
        hn_ref[rs] = hn_half.reshape(tm // 2, TOK_SUB, LANES)
        hn_parts.append(hn_half)
    hn = jnp.concatenate(hn_parts, axis=0)

    logits = _dot_nt(wrt_ref[...], hn) + brt_ref[...]
    row = lax.broadcasted_iota(jnp.int32, logits.shape, 0).astype(F32)
    big = float(LANES)
    is_g = row < N_GROUPS
    gl = jnp.where(is_g, logits, NEG_INF)
    gmax = jnp.max(gl, axis=0, keepdims=True)
    gidx = jnp.min(jnp.where(gl == gmax, row, big), axis=0, keepdims=True)
    gsum = jnp.sum(jnp.where(is_g, jnp.exp(gl - gmax), 0.0), axis=0, keepdims=True)
    g_w = 1.0 / gsum
    lo = N_GROUPS + EXPERTS_PER_GROUP * gidx
    el = jnp.where((row >= lo) & (row < lo + EXPERTS_PER_GROUP), logits, NEG_INF)
    t1 = jnp.max(el, axis=0, keepdims=True)
    i1 = jnp.min(jnp.where(el == t1, row, big), axis=0, keepdims=True)
    el2 = jnp.where(row == i1, NEG_INF, el)
    t2 = jnp.max(el2, axis=0, keepdims=True)
    i2 = jnp.min(jnp.where(el2 == t2, row, big), axis=0, keepdims=True)
    e2 = jnp.exp(t2 - t1)
    den = 1.0 + e2
    w1 = (1.0 / den) * g_w
    w2 = (e2 / den) * g_w
    ea = i1 - N_GROUPS
    eb = i2 - N_GROUPS

    oha = (row == ea).astype(F32)
    ohb = (row == eb).astype(F32)
    ohs = oha + ohb
    before = _dot(ohs.astype(BF16), tri_ref[...]) + base_scr[...][:, 0:1]
    ra = jnp.sum(before * oha, axis=0, keepdims=True)
    rb = jnp.sum(before * ohb, axis=0, keepdims=True)
    base_scr[...] = base_scr[...] + jnp.sum(ohs, axis=1, keepdims=True)
    cnt_ref[...] = base_scr[...]

    def rows_of(shape):
        r = lax.broadcasted_iota(jnp.int32, shape, 0)
        out = jnp.where(r == 0, ea, 0.0)
        for k, val in enumerate((eb, w1, w2, ra, rb), start=1):
            out = jnp.where(r == k, val, out)
        return out

    routet_ref[...] = rows_of((8, logits.shape[1]))
    route_ref[...] = rows_of(logits.shape).T


def _merge(act, o, sgc, sga, x2d, consts):
    tm = MERGE_TM
    row = lambda w: pl.BlockSpec((tm, w), lambda i: (i, 0))
    return pl.pallas_call(
        _merge_kernel,
        grid=(N_TOK // tm,),
        in_specs=[row(CONV_WIDTH), row(ATTN_WIDTH), row(D_MODEL), row(D_MODEL), row(D_MODEL)]
        + [_const_spec(c.shape) for c in consts],
        out_specs=[row(D_MODEL), pl.BlockSpec((tm, TOK_SUB, LANES), lambda i: (i, 0, 0)), row(LANES),
                   pl.BlockSpec((8, tm), lambda i: (0, i)), _const_spec((LANES, LANES))],
        out_shape=[jax.ShapeDtypeStruct((N_TOK, D_MODEL), F32),
                   jax.ShapeDtypeStruct((N_TOK, TOK_SUB, LANES), BF16),
                   jax.ShapeDtypeStruct((N_TOK, LANES), F32),
                   jax.ShapeDtypeStruct((8, N_TOK), F32),
                   jax.ShapeDtypeStruct((LANES, LANES), F32)],
        scratch_shapes=[pltpu.VMEM((LANES, LANES), F32)],
        compiler_params=pltpu.CompilerParams(dimension_semantics=("arbitrary",), vmem_limit_bytes=VMEM_LIMIT),
        name="merge",
    )(act, o, sgc, sga, x2d, *consts)


DISP_TM = 2048
DMA_UNROLL = 8


def _dest_kernel(pstart_ref, rt_ref, d_ref):
    rt = rt_ref[...]
    start = jnp.zeros_like(rt)
    for e in range(N_EXPERTS):
        start = jnp.where(rt == float(e), pstart_ref[e].astype(F32), start)
    d_ref[...] = (start + pltpu.roll(rt, 4, axis=0)).astype(jnp.int32)


def _dest(pstart, route_t):
    grid_spec = pltpu.PrefetchScalarGridSpec(
        num_scalar_prefetch=1,
        grid=(1,),
        in_specs=[pl.BlockSpec(route_t.shape, lambda i, *_: (0, 0))],
        out_specs=pl.BlockSpec(route_t.shape, lambda i, *_: (0, 0)),
    )
    return pl.pallas_call(
        _dest_kernel,
        grid_spec=grid_spec,
        out_shape=jax.ShapeDtypeStruct(route_t.shape, jnp.int32),
        compiler_params=pltpu.CompilerParams(dimension_semantics=("arbitrary",), vmem_limit_bytes=VMEM_LIMIT),
        name="dest",
    )(pstart, route_t)


ZERO_CHUNK = 64


def _dispatch_kernel(d0_ref, d1_ref, pfill_ref, pend_ref, hn_ref, xin_ref, zero_scr, sem, zsem):
    i = pl.program_id(0)
    base = i * DISP_TM

    @pl.when(i == 0)
    def _():
        zero_scr[...] = jnp.zeros_like(zero_scr)

        def pad_chunks(e, fn):
            first = pfill_ref[e] // ZERO_CHUNK * ZERO_CHUNK

            def one(c, carry):
                start = pl.multiple_of(first + c * ZERO_CHUNK, ZERO_CHUNK)
                fn(pltpu.make_async_copy(zero_scr.at[pl.ds(0, ZERO_CHUNK)], xin_ref.at[pl.ds(start, ZERO_CHUNK)],
                                         zsem.at[0]))
                return carry

            lax.fori_loop(0, (pend_ref[e] - first) // ZERO_CHUNK, one, 0)

        def zstart(e, carry):
            pad_chunks(e, lambda c: c.start())
            return carry

        def zwait(e, carry):
            pad_chunks(e, lambda c: c.wait())
            return carry

        n_used = pend_ref[N_EXPERTS - 1] // ROW_BLOCK

        def tcopy(b):
            return pltpu.make_async_copy(
                zero_scr, xin_ref.at[pl.ds(pl.multiple_of(b * ROW_BLOCK, ROW_BLOCK), ROW_BLOCK)], zsem.at[0])

        def tstart(b, carry):
            tcopy(b).start()
            return carry

        def twait(b, carry):
            tcopy(b).wait()
            return carry

        lax.fori_loop(0, N_EXPERTS, zstart, 0)
        lax.fori_loop(n_used, N_BLOCKS, tstart, 0)
        lax.fori_loop(0, N_EXPERTS, zwait, 0)
        lax.fori_loop(n_used, N_BLOCKS, twait, 0)

    def start(r, carry):
        pltpu.make_async_copy(hn_ref.at[r], xin_ref.at[d0_ref[base + r]], sem.at[0]).start()
        pltpu.make_async_copy(hn_ref.at[r], xin_ref.at[d1_ref[base + r]], sem.at[1]).start(priority=1)
        return carry

    lax.fori_loop(0, DISP_TM, start, 0, unroll=DMA_UNROLL)
    pltpu.make_async_copy(hn_ref, xin_ref.at[pl.ds(0, DISP_TM)], sem.at[0]).wait()
    pltpu.make_async_copy(hn_ref, xin_ref.at[pl.ds(0, DISP_TM)], sem.at[1]).wait()


def _dispatch(d0, d1, pfill, pend, hn):
    grid_spec = pltpu.PrefetchScalarGridSpec(
        num_scalar_prefetch=4,
        grid=(N_TOK // DISP_TM,),
        in_specs=[pl.BlockSpec((DISP_TM, TOK_SUB, LANES), lambda i, *_: (i, 0, 0))],
        out_specs=pl.BlockSpec(memory_space=pl.ANY),
        scratch_shapes=[pltpu.VMEM((ROW_BLOCK, TOK_SUB, LANES), BF16),
                        pltpu.SemaphoreType.DMA((2,)), pltpu.SemaphoreType.DMA((1,))],
    )
    return pl.pallas_call(
        _dispatch_kernel,
        grid_spec=grid_spec,
        out_shape=jax.ShapeDtypeStruct((P_ROWS, TOK_SUB, LANES), BF16),
        compiler_params=pltpu.CompilerParams(dimension_semantics=("arbitrary",), vmem_limit_bytes=VMEM_LIMIT),
        name="dispatch",
    )(d0, d1, pfill, pend, hn)


W_SLOTS = 3


def _expert_kernel(nu_ref, first_ref, ord_ref, used_ref, nue_ref, x_hbm, wg_hbm, wu_hbm, wd_hbm, y_hbm,
                   x_buf, y_buf, wg_f, wu_f, wd_f, wg_b, wu_b, wd_b, xsem, ysem, wsem):
    n = nu_ref[0]

    def rows(i):
        return pl.ds(pl.multiple_of(i * ROW_BLOCK, ROW_BLOCK), ROW_BLOCK)

    def xcopy(i, slot):
        return pltpu.make_async_copy(x_hbm.at[rows(i)], x_buf.at[slot], xsem.at[slot])

    def ycopy(i, slot):
        return pltpu.make_async_copy(y_buf.at[slot], y_hbm.at[rows(i)], ysem.at[slot])

    def fetch(k):
        e = used_ref[k]
        slot = k % W_SLOTS
        return (pltpu.make_async_copy(wg_hbm.at[e], wg_f.at[slot], wsem.at[slot, 0]),
                pltpu.make_async_copy(wu_hbm.at[e], wu_f.at[slot], wsem.at[slot, 1]),
                pltpu.make_async_copy(wd_hbm.at[e], wd_f.at[slot], wsem.at[slot, 2]))

    for k in range(W_SLOTS - 1):
        @pl.when(k < nue_ref[0])
        def _():
            for c in fetch(k):
                c.start()
    xcopy(0, 0).start()

    def block(i, carry):
        slot = i % 2
        xcopy(i, slot).wait()

        @pl.when(i + 1 < n)
        def _():
            xcopy(i + 1, 1 - slot).start()

        @pl.when(first_ref[i] == 1)
        def _():
            k = ord_ref[i]
            for c in fetch(k):
                c.wait()

            @pl.when(k + (W_SLOTS - 1) < nue_ref[0])
            def _():
                for c in fetch(k + (W_SLOTS - 1)):
                    c.start()

            wslot = k % W_SLOTS
            wg_b[...] = wg_f[wslot].astype(BF16)
            wu_b[...] = wu_f[wslot].astype(BF16)
            wd_b[...] = wd_f[wslot].astype(BF16)

        @pl.when(i >= 2)
        def _():
            ycopy(i - 2, slot).wait()

        x = x_buf[slot].reshape(ROW_BLOCK, D_MODEL)
        g = _dot(x, wg_b[...])
        u = _dot(x, wu_b[...])
        hmid = (g * _sigmoid(g) * u).astype(BF16)
        y = _dot(hmid, wd_b[...]).astype(BF16)
        y_buf[slot] = y.reshape(ROW_BLOCK, TOK_SUB, LANES)
        ycopy(i, slot).start()
        return carry

    lax.fori_loop(0, n, block, 0)

    @pl.when(n >= 2)
    def _():
        ycopy(n - 2, n % 2).wait()
    ycopy(n - 1, (n - 1) % 2).wait()

    y_buf[0] = jnp.zeros((ROW_BLOCK, TOK_SUB, LANES), BF16)

    def zcopy(b):
        return pltpu.make_async_copy(y_buf.at[0], y_hbm.at[rows(b)], ysem.at[0])

    def zstart(b, carry):
        zcopy(b).start()
        return carry

    def zwait(b, carry):
        zcopy(b).wait()
        return carry

    lax.fori_loop(n, N_BLOCKS, zstart, 0)
    lax.fori_loop(n, N_BLOCKS, zwait, 0)


def _experts(n_used, blk_first, blk_ord, used_e, n_used_e, xin, w_gate, w_up, w_down):
    grid_spec = pltpu.PrefetchScalarGridSpec(
        num_scalar_prefetch=5,
        grid=(1,),
        in_specs=[pl.BlockSpec(memory_space=pl.ANY),
                  pl.BlockSpec(memory_space=pl.ANY),
                  pl.BlockSpec(memory_space=pl.ANY),
                  pl.BlockSpec(memory_space=pl.ANY)],
        out_specs=pl.BlockSpec(memory_space=pl.ANY),
        scratch_shapes=[pltpu.VMEM((2, ROW_BLOCK, TOK_SUB, LANES), BF16),
                        pltpu.VMEM((2, ROW_BLOCK, TOK_SUB, LANES), BF16),
                        pltpu.VMEM((W_SLOTS, D_MODEL, D_EXPERT), F32),
                        pltpu.VMEM((W_SLOTS, D_MODEL, D_EXPERT), F32),
                        pltpu.VMEM((W_SLOTS, D_EXPERT, D_MODEL), F32),
                        pltpu.VMEM((D_MODEL, D_EXPERT), BF16),
                        pltpu.VMEM((D_MODEL, D_EXPERT), BF16),
                        pltpu.VMEM((D_EXPERT, D_MODEL), BF16),
                        pltpu.SemaphoreType.DMA((2,)), pltpu.SemaphoreType.DMA((2,)),
                        pltpu.SemaphoreType.DMA((W_SLOTS, 3))],
    )
    return pl.pallas_call(
        _expert_kernel,
        grid_spec=grid_spec,
        out_shape=jax.ShapeDtypeStruct((P_ROWS, TOK_SUB, LANES), BF16),
        compiler_params=pltpu.CompilerParams(dimension_semantics=("arbitrary",), vmem_limit_bytes=VMEM_LIMIT),
        name="experts",
    )(n_used, blk_first, blk_ord, used_e, n_used_e, xin, w_gate, w_up, w_down)


COMB_TM = 512
H1_SLOTS = 3


def _combine_kernel(d0_ref, d1_ref, h1_hbm, route_ref, yb_ref, out_ref, g_scr, sem, h1_scr, hsem):
    i = pl.program_id(0)
    n = pl.num_programs(0)

    def h1_copy(tile):
        hs = tile % H1_SLOTS
        return pltpu.make_async_copy(h1_hbm.at[pl.ds(tile * COMB_TM, COMB_TM)], h1_scr.at[hs], hsem.at[hs])

    @pl.when(i == 0)
    def _():
        for t in range(H1_SLOTS - 1):
            h1_copy(t).start()

    @pl.when(i + (H1_SLOTS - 1) < n)
    def _():
        h1_copy(i + (H1_SLOTS - 1)).start()

    def issue(tile, slot):
        base = tile * COMB_TM

        def start(r, carry):
            pltpu.make_async_copy(yb_ref.at[d0_ref[base + r]], g_scr.at[slot, 0, r], sem.at[slot, 0]).start()
            pltpu.make_async_copy(yb_ref.at[d1_ref[base + r]], g_scr.at[slot, 1, r], sem.at[slot, 1]).start(
                priority=1)
            return carry

        lax.fori_loop(0, COMB_TM, start, 0, unroll=DMA_UNROLL)

    @pl.when(i == 0)
    def _():
        issue(0, 0)

    slot = i % 2

    @pl.when(i + 1 < n)
    def _():
        issue(i + 1, 1 - slot)

    for k in range(2):
        pltpu.make_async_copy(yb_ref.at[pl.ds(0, COMB_TM)], g_scr.at[slot, k], sem.at[slot, k]).wait()
    h1_copy(i).wait()

    route = route_ref[...]
    w0 = route[:, 2:3]
    w1 = route[:, 3:4]
    g0 = g_scr[slot, 0].reshape(COMB_TM, D_MODEL).astype(F32)
    g1 = g_scr[slot, 1].reshape(COMB_TM, D_MODEL).astype(F32)
    out_ref[...] = h1_scr[i % H1_SLOTS] + (g0 * w0 + g1 * w1)


def _combine(d0, d1, h1, route, yb):
    grid_spec = pltpu.PrefetchScalarGridSpec(
        num_scalar_prefetch=2,
        grid=(N_TOK // COMB_TM,),
        in_specs=[pl.BlockSpec(memory_space=pl.ANY),
                  pl.BlockSpec((COMB_TM, LANES), lambda i, *_: (i, 0)),
                  pl.BlockSpec(memory_space=pl.ANY)],
        out_specs=pl.BlockSpec((COMB_TM, D_MODEL), lambda i, *_: (i, 0)),
        scratch_shapes=[pltpu.VMEM((2, 2, COMB_TM, TOK_SUB, LANES), BF16),
                        pltpu.SemaphoreType.DMA((2, 2)),
                        pltpu.VMEM((H1_SLOTS, COMB_TM, D_MODEL), F32),
                        pltpu.SemaphoreType.DMA((H1_SLOTS,))],
    )
    return pl.pallas_call(
        _combine_kernel,
        grid_spec=grid_spec,
        out_shape=jax.ShapeDtypeStruct((N_TOK, D_MODEL), F32),
        compiler_params=pltpu.CompilerParams(dimension_semantics=("arbitrary",), vmem_limit_bytes=VMEM_LIMIT),
        name="combine",
    )(d0, d1, h1, route, yb)


def kernel(x, meta, norm1_g, w_in, b_in, dw_w, dw_b, conv_ln_g, conv_ln_b, w_conv_out, b_conv_out,
           q_norm_g, k_norm_g, b_forget, w_attn_out, w_out, norm2_g, w_group, b_group, w_router,
           b_router, w_gate, w_up, w_down):
    l = 0
    row = lambda v: v.reshape(1, -1).astype(F32)

    c_u, c_q, c_f = 2 * CONV_WIDTH, 2 * CONV_WIDTH, 2 * CONV_WIDTH + 3 * ATTN_WIDTH
    c_gc = c_f + N_HEADS
    c_ga = c_gc + D_MODEL
    wi, bi = w_in[l], b_in[l]
    wu, bu = wi[:, :c_u].astype(BF16), row(bi[:c_u])
    wqkv, bqkv = wi[:, c_q:c_f].astype(BF16), row(bi[c_q:c_f])
    wft = jnp.zeros((16, D_MODEL), F32).at[:N_HEADS].set(wi[:, c_f:c_gc].T).astype(BF16)
    bft = jnp.zeros((16, 1), F32).at[:N_HEADS, 0].set(bi[c_f:c_gc] + b_forget[l])
    wgc, bgc = wi[:, c_gc:c_ga].astype(BF16), row(bi[c_gc:c_ga])
    wga, bga = wi[:, c_ga:].astype(BF16), row(bi[c_ga:])
    qg = row(jnp.tile(q_norm_g[l], N_HEADS) * (HEAD_DIM ** -0.5 * LOG2E))
    kg = row(jnp.tile(k_norm_g[l], N_HEADS))
    head_of_col = jnp.arange(ATTN_WIDTH) // HEAD_DIM
    hsum = (head_of_col[:, None] == jnp.arange(LANES)[None, :]).astype(BF16)
    hexp_half = (jnp.arange(LANES)[:, None] == head_of_col[None, :]).astype(BF16)
    hexp = jnp.concatenate([hexp_half, hexp_half], axis=0)
    proj_consts = (row(norm1_g[l]), wu, bu, wqkv, bqkv, wft, bft, wgc, bgc, wga, bga, qg, kg, hsum, hexp)

    x2d = x.reshape(N_TOK, D_MODEL)
    meta_pad = jnp.zeros((META_ROWS, D_MODEL), F32).at[:N_META].set(meta.astype(F32))
    a_m, _, k_m, v_m, lf_m, _, _ = _proj(meta_pad, proj_consts, META_ROWS)
    conv_consts = (a_m, _conv_shift_matrix(), dw_w[l].astype(F32), row(dw_b[l]),
                   row(conv_ln_g[l]), row(conv_ln_b[l]))
    act, q, k, v, lf, sgc, sga = _proj(x2d, proj_consts, PROJ_TM, conv_consts)

    o = _attention_t(q, k, v, lf, k_m, v_m, lf_m)

    wrt = jnp.zeros((LANES, D_MODEL), F32).at[:N_GROUPS].set(w_group[l].T)
    wrt = wrt.at[N_GROUPS:N_GROUPS + N_EXPERTS].set(w_router[l].T).astype(BF16)
    brt = jnp.zeros((LANES, 1), F32).at[:N_GROUPS, 0].set(b_group[l])
    brt = brt.at[N_GROUPS:N_GROUPS + N_EXPERTS, 0].set(b_router[l])
    tri = jnp.triu(jnp.ones((MERGE_TM, MERGE_TM), F32), 1).astype(BF16)
    merge_consts = (w_conv_out[l].astype(BF16), row(b_conv_out[l]), w_attn_out[l].astype(BF16),
                    w_out[l].astype(BF16), row(norm2_g[l]), wrt, brt, tri)
    h1, hn, route, route_t, cnt = _merge(act, o, sgc, sga, x2d, merge_consts)

    counts = cnt[:N_EXPERTS, 0].astype(jnp.int32)
    padded = (counts + ROW_BLOCK - 1) // ROW_BLOCK * ROW_BLOCK
    pend = jnp.cumsum(padded).astype(jnp.int32)
    pstart = pend - padded
    blk_row = jnp.arange(N_BLOCKS, dtype=jnp.int32) * ROW_BLOCK
    n_used = pend[-1:] // ROW_BLOCK
    nz = counts > 0
    before = (pend[None, :] <= blk_row[:, None]) & nz[None, :]
    blk_ord = jnp.sum(before.astype(jnp.int32), axis=1)
    blk_first = jnp.any((pstart[None, :] == blk_row[:, None]) & nz[None, :], axis=1).astype(jnp.int32)
    e_ord = jnp.cumsum(nz.astype(jnp.int32)) - 1
    e_ids = jnp.arange(N_EXPERTS, dtype=jnp.int32)
    used_e = jnp.sum(jnp.where((e_ord[None, :] == e_ids[:, None]) & nz[None, :], e_ids[None, :], 0), axis=1)
    n_used_e = jnp.sum(nz.astype(jnp.int32)).reshape(1)

    dest = _dest(pstart, route_t)
    d0, d1 = dest[0], dest[1]
    xin = _dispatch(d0, d1, pstart + counts, pend, hn)
    yb = _experts(n_used, blk_first, blk_ord, used_e.astype(jnp.int32), n_used_e, xin,
                  w_gate[l], w_up[l], w_down[l])
    out = _combine(d0, d1, h1, route, yb)
    return out.reshape(BATCH, SEQ, D_MODEL)
```

```python
import jax
import jax.numpy as jnp
from jax import lax
from jax.experimental import pallas as pl
from jax.experimental.pallas import tpu as pltpu

D_MODEL = 1024
BATCH = 8
SEQ = 2048
N_META = 16
CONV_WIDTH = 512
CONV_K = 31
N_HEADS = 8
HEAD_DIM = 64
ATTN_WIDTH = N_HEADS * HEAD_DIM
N_GROUPS = 8
EXPERTS_PER_GROUP = 8
N_EXPERTS = N_GROUPS * EXPERTS_PER_GROUP
D_EXPERT = 256
RMS_EPS = 1e-6
LN_EPS = 1e-5

N_TOK = BATCH * SEQ
LANES = 128
META_ROWS = 128
ROW_BLOCK = 512
N_ASSIGN = 2 * N_TOK
N_BLOCKS = N_ASSIGN // ROW_BLOCK + N_EXPERTS
P_ROWS = N_BLOCKS * ROW_BLOCK
TOK_SUB = D_MODEL // LANES
PROJ_TM = 1024
CONV_CHUNK = 128
CONV_PAD = 32
CONV_WIN = CONV_CHUNK + 32
CONV_SHROWS = CONV_CHUNK + 24
CONV_COPIES = 7
CONV_LANES = 128
ATT_TQ = SEQ
ATT_TK = 256
N_PAIRS = N_HEADS // 2
VMEM_LIMIT = 56 * 1024 * 1024

F32 = jnp.float32
BF16 = jnp.bfloat16
NEG_INF = float("-inf")
LOG2E = 1.4426950408889634


def _dot(a, b):
    return jnp.dot(a, b, preferred_element_type=F32)


def _dot_nt(a, b):
    return lax.dot_general(a, b, (((1,), (1,)), ((), ())), preferred_element_type=F32)


def _sigmoid(x):
    return 1.0 / (1.0 + jnp.exp(-x))


def _const_spec(shape):
    nd = len(shape)
    return pl.BlockSpec(shape, lambda *_: (0,) * nd)


def _head_rms(t, hsum_ref, hexp_ref, gain):
    ss = _dot((t * t).astype(BF16), hsum_ref[...])
    inv = lax.rsqrt(ss * (1.0 / HEAD_DIM) + RMS_EPS)
    hi = inv.astype(BF16)
    lo = (inv - hi.astype(F32)).astype(BF16)
    invb = _dot(jnp.concatenate([hi, lo], axis=1), hexp_ref[...])
    return t * invb * gain


N_PROJ_IN = 16


def _proj_stages(x_ref, g1_ref, wu_ref, bu_ref, wqkv_ref, bqkv_ref, wft_ref, bft_ref, wgc_ref, bgc_ref,
                 wga_ref, bga_ref, qg_ref, kg_ref, hsum_ref, hexp_ref,
                 q_ref, k_ref, v_ref, lf_ref, sgc_ref, sga_ref):
    x = x_ref[...]
    ms = jnp.mean(x * x, axis=-1, keepdims=True)
    xn = (x * lax.rsqrt(ms + RMS_EPS) * g1_ref[...]).astype(BF16)

    def glu():
        u = _dot(xn, wu_ref[...]) + bu_ref[...]
        return (u[:, :CONV_WIDTH] * _sigmoid(u[:, CONV_WIDTH:])).astype(BF16)

    def qkv_stage():
        qkv = _dot(xn, wqkv_ref[...]) + bqkv_ref[...]
        q_ref[...] = _head_rms(qkv[:, :ATTN_WIDTH], hsum_ref, hexp_ref, qg_ref[...]).astype(BF16)
        k_ref[...] = _head_rms(qkv[:, ATTN_WIDTH:2 * ATTN_WIDTH], hsum_ref, hexp_ref, kg_ref[...]).astype(BF16)
        v = qkv[:, 2 * ATTN_WIDTH:]
        vb = v_ref.shape[2]
        for c in range(v_ref.shape[0]):
            v_ref[c] = v[c * vb:(c + 1) * vb, :].T.astype(BF16)

    def gate_stage_c():
        f = _dot_nt(wft_ref[...], xn) + bft_ref[...]
        lf_ref[...] = jnp.minimum(f, 0.0) - jnp.log(1.0 + jnp.exp(-jnp.abs(f)))
        sgc_ref[...] = _sigmoid(_dot(xn, wgc_ref[...]) + bgc_ref[...]).astype(BF16)

    def gate_stage_a():
        sga_ref[...] = _sigmoid(_dot(xn, wga_ref[...]) + bga_ref[...]).astype(BF16)

    return glu, (qkv_stage, gate_stage_c, gate_stage_a)


def _proj_kernel(*refs):
    ins, a_ref, outs = refs[:N_PROJ_IN], refs[N_PROJ_IN], refs[N_PROJ_IN + 1:]
    glu, stages = _proj_stages(*ins, *outs)
    a_ref[...] = glu()
    for stage in stages:
        stage()


def _proj_conv_kernel(*refs):
    ins = refs[:N_PROJ_IN]
    am_ref, shift_ref, dw_ref, dwb_ref, lng_ref, lnb_ref = refs[N_PROJ_IN:N_PROJ_IN + 6]
    act_ref, outs = refs[N_PROJ_IN + 6], refs[N_PROJ_IN + 7:N_PROJ_IN + 13]
    win_scr, sh_scr = refs[N_PROJ_IN + 13:]
    tm = act_ref.shape[0]
    i = pl.program_id(0)
    first = (i % (SEQ // tm)) == 0

    @pl.when(first)
    def _():
        win_scr[0:N_META, :] = jnp.zeros((N_META, CONV_WIDTH), BF16)
        win_scr[N_META:CONV_PAD, :] = am_ref[0:N_META, :]

    @pl.when(jnp.logical_not(first))
    def _():
        win_scr[0:CONV_PAD, :] = win_scr[tm:tm + CONV_PAD, :]

    glu, stages = _proj_stages(*ins, *outs)
    win_scr[CONV_PAD:CONV_PAD + tm, :] = glu()

    def conv_chunk(ci):
        r0 = ci * CONV_CHUNK
        slot = ci % 2
        sh_scr[slot] = _dot(shift_ref[...], win_scr[r0:r0 + CONV_WIN, :])
        parts = []
        for c0 in range(0, CONV_WIDTH, CONV_LANES):
            cs = slice(c0, c0 + CONV_LANES)
            part = jnp.zeros((CONV_CHUNK, CONV_LANES), F32) + dwb_ref[:, cs]
            for b in range(8):
                if b == 0:
                    rows = win_scr[r0:r0 + CONV_WIN, cs].astype(F32)
                else:
                    rows = sh_scr[slot, (b - 1) * CONV_SHROWS:b * CONV_SHROWS, cs]
                for j in range(CONV_K):
                    off = j + 2
                    if off % 8 == b:
                        part = part + rows[off - b:off - b + CONV_CHUNK] * dw_ref[j:j + 1, cs]
            parts.append(part)
        acc = jnp.concatenate(parts, axis=1)
        mu = jnp.mean(acc, axis=-1, keepdims=True)
        d = acc - mu
        var = jnp.mean(d * d, axis=-1, keepdims=True)
        y = d * lax.rsqrt(var + LN_EPS) * lng_ref[...] + lnb_ref[...]
        act_ref[r0:r0 + CONV_CHUNK, :] = (y * _sigmoid(y)).astype(BF16)

    n_chunks = tm // CONV_CHUNK
    per_stage = -(-n_chunks // (len(stages) + 1))
    ci = 0
    for stage in (None,) + stages:
        if stage is not None:
            stage()
        for _ in range(per_stage):
            if ci < n_chunks:
                conv_chunk(ci)
                ci += 1


def _proj(x2d, consts, tm, conv_consts=None):
    n = x2d.shape[0]
    row = lambda w: pl.BlockSpec((tm, w), lambda i: (i, 0))
    in_specs = [row(D_MODEL)] + [_const_spec(c.shape) for c in consts]
    fused = conv_consts is not None
    extra = tuple(conv_consts) if fused else ()
    in_specs += [_const_spec(c.shape) for c in extra]
    scratch = [pltpu.VMEM((CONV_PAD + tm, CONV_WIDTH), BF16),
               pltpu.VMEM((2, CONV_COPIES * CONV_SHROWS, CONV_WIDTH), F32)] if fused else []
    vb = min(tm, ATT_TK)
    out_shape = [
        jax.ShapeDtypeStruct((n, CONV_WIDTH), BF16),
        jax.ShapeDtypeStruct((n, ATTN_WIDTH), BF16),
        jax.ShapeDtypeStruct((n, ATTN_WIDTH), BF16),
        jax.ShapeDtypeStruct((n // vb, ATTN_WIDTH, vb), BF16),
        jax.ShapeDtypeStruct((16, n), F32),
        jax.ShapeDtypeStruct((n, D_MODEL), BF16),
        jax.ShapeDtypeStruct((n, D_MODEL), BF16),
    ]
    out_specs = [row(CONV_WIDTH), row(ATTN_WIDTH), row(ATTN_WIDTH),
                 pl.BlockSpec((tm // vb, ATTN_WIDTH, vb), lambda i: (i, 0, 0)),
                 pl.BlockSpec((16, tm), lambda i: (0, i)), row(D_MODEL), row(D_MODEL)]
    return pl.pallas_call(
        _proj_conv_kernel if fused else _proj_kernel,
        grid=(n // tm,),
        in_specs=in_specs,
        out_specs=out_specs,
        out_shape=out_shape,
        scratch_shapes=scratch,
        compiler_params=pltpu.CompilerParams(dimension_semantics=("arbitrary",), vmem_limit_bytes=VMEM_LIMIT),
        name="proj_conv" if fused else "proj",
    )(x2d, *consts, *extra)


def _conv_shift_matrix():
    rr = jnp.arange(CONV_COPIES * CONV_SHROWS)
    src = rr % CONV_SHROWS + rr // CONV_SHROWS + 1
    return (src[:, None] == jnp.arange(CONV_WIN)[None, :]).astype(BF16)


AUG_TERMS = 3
ACC_ONES = 16


def _split_terms(x):
    hi = x.astype(BF16)
    r1 = x - hi.astype(F32)
    mid = r1.astype(BF16)
    lo = (r1 - mid.astype(F32)).astype(BF16)
    return hi, mid, lo


def _attn_t_kernel(q_ref, k_ref, vt_ref, lf_ref, km_ref, vm_ref, lfm_ref, place_ref, o_ref,
                   aug_scr, augm_scr, m_scr, acc_scr):
    p = pl.program_id(1)
    nk = SEQ // ATT_TK

    def bias_lanes(colsum):
        out = None
        for t, term in enumerate(_split_terms(colsum)):
            d = _dot(term, place_ref[t])
            out = d if out is None else out + d
        return out.astype(BF16)

    @pl.when(p == 0)
    def _():
        r = lax.broadcasted_iota(jnp.int32, (ATT_TK, ATT_TK), 0)
        c = lax.broadcasted_iota(jnp.int32, (ATT_TK, ATT_TK), 1)
        tri = (c <= r).astype(BF16)
        local = []
        for j in range(nk):
            lf = lf_ref[:, j * ATT_TK:(j + 1) * ATT_TK]
            csum = None
            for term in _split_terms(lf):
                d = _dot_nt(tri, term)
                csum = d if csum is None else csum + d
            local.append(csum)
        carry = jnp.zeros((1, 16), F32)
        for j in range(nk):
            csum = local[j] + carry
            aug_scr[j * ATT_TK:(j + 1) * ATT_TK, :] = bias_lanes(csum * (-LOG2E))
            carry = csum[ATT_TK - 1:ATT_TK, :]
        rm = lax.broadcasted_iota(jnp.int32, (META_ROWS, META_ROWS), 0)
        cm = lax.broadcasted_iota(jnp.int32, (META_ROWS, META_ROWS), 1)
        trim = ((cm > rm) & (cm < N_META)).astype(BF16)
        msum = jnp.zeros((META_ROWS, 16), F32)
        for term in _split_terms(lfm_ref[...]):
            msum = msum + _dot_nt(trim, term)
        augm_scr[...] = bias_lanes(msum * LOG2E)

    lane = lax.broadcasted_iota(jnp.int32, (ATT_TQ, LANES), 1)
    q_all = q_ref[...]
    zq = jnp.zeros_like(q_all)
    q_cat = []
    for hl in range(2):
        own = (lane < HEAD_DIM) if hl == 0 else (lane >= HEAD_DIM)
        h = 2 * p + hl
        bias_sel = (lane >= AUG_TERMS * h) & (lane < AUG_TERMS * h + AUG_TERMS)
        bias_ones = jnp.where(bias_sel, 1.0, 0.0).astype(BF16)
        q_cat.append(jnp.concatenate([jnp.where(own, q_all, zq), bias_ones], axis=1))
        m_scr[hl] = jnp.full((1, ATT_TQ), NEG_INF, F32)
        acc_scr[hl] = jnp.zeros((HEAD_DIM + ACC_ONES, ATT_TQ), F32)

    def scores(kb, augb, lo=0):
        k_cat = jnp.concatenate([kb, augb], axis=1)
        return tuple(_dot_nt(k_cat, q_cat[hl][lo:]) for hl in range(2))

    def block_softmax(s2, vtb):
        ones = jnp.ones((ACC_ONES, vtb.shape[1]), BF16)
        out = []
        for hl in range(2):
            s = s2[hl]
            m_blk = jnp.max(s, axis=0, keepdims=True)
            pr = jnp.exp2(s - m_blk).astype(BF16)
            v_own = jnp.concatenate([vtb[hl * HEAD_DIM:(hl + 1) * HEAD_DIM], ones], axis=0)
            out.append((m_blk, _dot(v_own, pr)))
        return out

    def block_scores(j):
        rs = slice(j * ATT_TK, (j + 1) * ATT_TK)
        return scores(k_ref[rs, :], aug_scr[rs, :], lo=j * ATT_TK)

    s_meta = scores(km_ref[0:N_META, :], augm_scr[0:N_META, :])
    s_blocks = [block_scores(j) for j in range(nk)]
    partials = [(0, block_softmax(s_meta, vm_ref[0][:, 0:N_META]))]

    srow = lax.broadcasted_iota(jnp.int32, (ATT_TK, ATT_TK), 0)
    tcol = lax.broadcasted_iota(jnp.int32, (ATT_TK, ATT_TK), 1)
    tri = srow <= tcol
    for j in range(nk):
        s_cur = tuple(jnp.concatenate([jnp.where(tri, s[:, :ATT_TK], NEG_INF), s[:, ATT_TK:]], axis=1)
                      if s.shape[1] > ATT_TK else jnp.where(tri, s, NEG_INF) for s in s_blocks[j])
        partials.append((j * ATT_TK, block_softmax(s_cur, vt_ref[j])))

    for hl in range(2):
        for lo, part in partials:
            m_scr[hl, :, lo:] = jnp.maximum(m_scr[hl, :, lo:], part[hl][0])
        for lo, part in partials:
            m_blk, pv = part[hl]
            acc_scr[hl, :, lo:] = acc_scr[hl, :, lo:] + pv * jnp.exp2(m_blk - m_scr[hl, :, lo:])

    a0 = acc_scr[0]
    a1 = acc_scr[1]
    ot = jnp.concatenate([a0[0:HEAD_DIM] / a0[HEAD_DIM:HEAD_DIM + 1], a1[0:HEAD_DIM] / a1[HEAD_DIM:HEAD_DIM + 1]],
                         axis=0)
    o_ref[...] = ot.T.astype(BF16)


def _attention_t(q, k, vt, lf, km, vm, lfm):
    nk = SEQ // ATT_TK
    hh = jnp.arange(16)[:, None]
    ll = jnp.arange(LANES)[None, :]
    place = jnp.stack([((ll == AUG_TERMS * hh + t) & (hh < N_HEADS)).astype(BF16) for t in range(AUG_TERMS)])
    return pl.pallas_call(
        _attn_t_kernel,
        grid=(BATCH, N_PAIRS),
        in_specs=[
            pl.BlockSpec((ATT_TQ, LANES), lambda b, p: (b, p)),
            pl.BlockSpec((SEQ, LANES), lambda b, p: (b, p)),
            pl.BlockSpec((nk, LANES, ATT_TK), lambda b, p: (b, p, 0)),
            pl.BlockSpec((16, SEQ), lambda b, p: (0, b)),
            pl.BlockSpec((META_ROWS, LANES), lambda b, p: (0, p)),
            pl.BlockSpec((1, LANES, META_ROWS), lambda b, p: (0, p, 0)),
            pl.BlockSpec((16, META_ROWS), lambda b, p: (0, 0)),
            _const_spec(place.shape),
        ],
        out_specs=pl.BlockSpec((ATT_TQ, LANES), lambda b, p: (b, p)),
        out_shape=jax.ShapeDtypeStruct((N_TOK, ATTN_WIDTH), BF16),
        scratch_shapes=[
            pltpu.VMEM((SEQ, LANES), BF16),
            pltpu.VMEM((META_ROWS, LANES), BF16),
            pltpu.VMEM((2, 1, ATT_TQ), F32),
            pltpu.VMEM((2, HEAD_DIM + ACC_ONES, ATT_TQ), F32),
        ],
        compiler_params=pltpu.CompilerParams(
            dimension_semantics=("arbitrary", "arbitrary"), vmem_limit_bytes=VMEM_LIMIT),
        name="attn",
    )(q, k, vt, lf, km, vm, lfm, place)


MERGE_TM = 1024


def _merge_kernel(act_ref, o_ref, sgc_ref, sga_ref, x_ref, wpw_ref, bpw_ref, wao_ref, wout_ref, g2_ref,
                  wrt_ref, brt_ref, tri_ref,
                  h1_ref, hn_ref, route_ref, routet_ref, cnt_ref, base_scr):
    i = pl.program_id(0)

    @pl.when(i == 0)
    def _():
        base_scr[...] = jnp.zeros_like(base_scr)

    tm = x_ref.shape[0]
    hn_parts = []
    for rs in (slice(0, tm // 2), slice(tm // 2, tm)):
        y_conv = _dot(act_ref[rs, :], wpw_ref[...]) + bpw_ref[...]
        y_attn = _dot(o_ref[rs, :], wao_ref[...])
        merged = sgc_ref[rs, :].astype(F32) * y_conv + sga_ref[rs, :].astype(F32) * y_attn
        h1 = x_ref[rs, :] + _dot(merged.astype(BF16), wout_ref[...])
        h1_ref[rs, :] = h1
        ms = jnp.mean(h1 * h1, axis=-1, keepdims=True)
        hn_half = (h1 * lax.rsqrt(ms + RMS_EPS) * g2_ref[...]).astype(BF16)
        hn_ref[rs] = hn_half.reshape(tm // 2, TOK_SUB, LANES)
        hn_parts.append(hn_half)
    hn = jnp.concatenate(hn_parts, axis=0)

    logits = _dot_nt(wrt_ref[...], hn) + brt_ref[...]
    row = lax.broadcasted_iota(jnp.int32, logits.shape, 0).astype(F32)
    big = float(LANES)
    is_g = row < N_GROUPS
    gl = jnp.where(is_g, logits, NEG_INF)
    gmax = jnp.max(gl, axis=0, keepdims=True)
    gidx = jnp.min(jnp.where(gl == gmax, row, big), axis=0, keepdims=True)
    gsum = jnp.sum(jnp.where(is_g, jnp.exp(gl - gmax), 0.0), axis=0, keepdims=True)
    g_w = 1.0 / gsum
    lo = N_GROUPS + EXPERTS_PER_GROUP * gidx
    el = jnp.where((row >= lo) & (row < lo + EXPERTS_PER_GROUP), logits, NEG_INF)
    t1 = jnp.max(el, axis=0, keepdims=True)
    i1 = jnp.min(jnp.where(el == t1, row, big), axis=0, keepdims=True)
    el2 = jnp.where(row == i1, NEG_INF, el)
    t2 = jnp.max(el2, axis=0, keepdims=True)
    i2 = jnp.min(jnp.where(el2 == t2, row, big), axis=0, keepdims=True)
    e2 = jnp.exp(t2 - t1)
    den = 1.0 + e2
    w1 = (1.0 / den) * g_w
    w2 = (e2 / den) * g_w
    ea = i1 - N_GROUPS
    eb = i2 - N_GROUPS

    oha = (row == ea).astype(F32)
    ohb = (row == eb).astype(F32)
    ohs = oha + ohb
    before = _dot(ohs.astype(BF16), tri_ref[...]) + base_scr[...][:, 0:1]
    ra = jnp.sum(before * oha, axis=0, keepdims=True)
    rb = jnp.sum(before * ohb, axis=0, keepdims=True)
    base_scr[...] = base_scr[...] + jnp.sum(ohs, axis=1, keepdims=True)
    cnt_ref[...] = base_scr[...]

    def rows_of(shape):
        r = lax.broadcasted_iota(jnp.int32, shape, 0)
        out = jnp.where(r == 0, ea, 0.0)
        for k, val in enumerate((eb, w1, w2, ra, rb), start=1):
            out = jnp.where(r == k, val, out)
        return out

    routet_ref[...] = rows_of((8, logits.shape[1]))
    route_ref[...] = rows_of(logits.shape).T


def _merge(act, o, sgc, sga, x2d, consts):
    tm = MERGE_TM
    row = lambda w: pl.BlockSpec((tm, w), lambda i: (i, 0))
    return pl.pallas_call(
        _merge_kernel,
        grid=(N_TOK // tm,),
        in_specs=[row(CONV_WIDTH), row(ATTN_WIDTH), row(D_MODEL), row(D_MODEL), row(D_MODEL)]
        + [_const_spec(c.shape) for c in consts],
        out_specs=[row(D_MODEL), pl.BlockSpec((tm, TOK_SUB, LANES), lambda i: (i, 0, 0)), row(LANES),
                   pl.BlockSpec((8, tm), lambda i: (0, i)), _const_spec((LANES, LANES))],
        out_shape=[jax.ShapeDtypeStruct((N_TOK, D_MODEL), F32),
                   jax.ShapeDtypeStruct((N_TOK, TOK_SUB, LANES), BF16),
                   jax.ShapeDtypeStruct((N_TOK, LANES), F32),
                   jax.ShapeDtypeStruct((8, N_TOK), F32),
                   jax.ShapeDtypeStruct((LANES, LANES), F32)],
        scratch_shapes=[pltpu.VMEM((LANES, LANES), F32)],
        compiler_params=pltpu.CompilerParams(dimension_semantics=("arbitrary",), vmem_limit_bytes=VMEM_LIMIT),
        name="merge",
    )(act, o, sgc, sga, x2d, *consts)


DISP_TM = 2048
DMA_UNROLL = 8


def _dest_kernel(pstart_ref, rt_ref, d_ref):
    rt = rt_ref[...]
    start = jnp.zeros_like(rt)
    for e in range(N_EXPERTS):
        start = jnp.where(rt == float(e), pstart_ref[e].astype(F32), start)
    d_ref[...] = (start + pltpu.roll(rt, 4, axis=0)).astype(jnp.int32)


def _dest(pstart, route_t):
    grid_spec = pltpu.PrefetchScalarGridSpec(
        num_scalar_prefetch=1,
        grid=(1,),
        in_specs=[pl.BlockSpec(route_t.shape, lambda i, *_: (0, 0))],
        out_specs=pl.BlockSpec(route_t.shape, lambda i, *_: (0, 0)),
    )
    return pl.pallas_call(
        _dest_kernel,
        grid_spec=grid_spec,
        out_shape=jax.ShapeDtypeStruct(route_t.shape, jnp.int32),
        compiler_params=pltpu.CompilerParams(dimension_semantics=("arbitrary",), vmem_limit_bytes=VMEM_LIMIT),
        name="dest",
    )(pstart, route_t)


ZERO_CHUNK = 64


def _dispatch_kernel(d0_ref, d1_ref, pfill_ref, pend_ref, hn_ref, xin_ref, zero_scr, sem, zsem):
    i = pl.program_id(0)
    base = i * DISP_TM

    @pl.when(i == 0)
    def _():
        zero_scr[...] = jnp.zeros_like(zero_scr)

        def pad_chunks(e, fn):
            first = pfill_ref[e] // ZERO_CHUNK * ZERO_CHUNK

            def one(c, carry):
                start = pl.multiple_of(first + c * ZERO_CHUNK, ZERO_CHUNK)
                fn(pltpu.make_async_copy(zero_scr.at[pl.ds(0, ZERO_CHUNK)], xin_ref.at[pl.ds(start, ZERO_CHUNK)],
                                         zsem.at[0]))
                return carry

            lax.fori_loop(0, (pend_ref[e] - first) // ZERO_CHUNK, one, 0)

        def zstart(e, carry):
            pad_chunks(e, lambda c: c.start())
            return carry

        def zwait(e, carry):
            pad_chunks(e, lambda c: c.wait())
            return carry

        n_used = pend_ref[N_EXPERTS - 1] // ROW_BLOCK

        def tcopy(b):
            return pltpu.make_async_copy(
                zero_scr, xin_ref.at[pl.ds(pl.multiple_of(b * ROW_BLOCK, ROW_BLOCK), ROW_BLOCK)], zsem.at[0])

        def tstart(b, carry):
            tcopy(b).start()
            return carry

        def twait(b, carry):
            tcopy(b).wait()
            return carry

        lax.fori_loop(0, N_EXPERTS, zstart, 0)
        lax.fori_loop(n_used, N_BLOCKS, tstart, 0)
        lax.fori_loop(0, N_EXPERTS, zwait, 0)
        lax.fori_loop(n_used, N_BLOCKS, twait, 0)

    def start(r, carry):
        pltpu.make_async_copy(hn_ref.at[r], xin_ref.at[d0_ref[base + r]], sem.at[0]).start()
        pltpu.make_async_copy(hn_ref.at[r], xin_ref.at[d1_ref[base + r]], sem.at[1]).start(priority=1)
        return carry

    lax.fori_loop(0, DISP_TM, start, 0, unroll=DMA_UNROLL)
    pltpu.make_async_copy(hn_ref, xin_ref.at[pl.ds(0, DISP_TM)], sem.at[0]).wait()
    pltpu.make_async_copy(hn_ref, xin_ref.at[pl.ds(0, DISP_TM)], sem.at[1]).wait()


def _dispatch(d0, d1, pfill, pend, hn):
    grid_spec = pltpu.PrefetchScalarGridSpec(
        num_scalar_prefetch=4,
        grid=(N_TOK // DISP_TM,),
        in_specs=[pl.BlockSpec((DISP_TM, TOK_SUB, LANES), lambda i, *_: (i, 0, 0))],
        out_specs=pl.BlockSpec(memory_space=pl.ANY),
        scratch_shapes=[pltpu.VMEM((ROW_BLOCK, TOK_SUB, LANES), BF16),
                        pltpu.SemaphoreType.DMA((2,)), pltpu.SemaphoreType.DMA((1,))],
    )
    return pl.pallas_call(
        _dispatch_kernel,
        grid_spec=grid_spec,
        out_shape=jax.ShapeDtypeStruct((P_ROWS, TOK_SUB, LANES), BF16),
        compiler_params=pltpu.CompilerParams(dimension_semantics=("arbitrary",), vmem_limit_bytes=VMEM_LIMIT),
        name="dispatch",
    )(d0, d1, pfill, pend, hn)


W_SLOTS = 3


def _expert_kernel(nu_ref, first_ref, ord_ref, used_ref, nue_ref, x_hbm, wg_hbm, wu_hbm, wd_hbm, y_hbm,
                   x_buf, y_buf, wg_f, wu_f, wd_f, wg_b, wu_b, wd_b, xsem, ysem, wsem):
    n = nu_ref[0]

    def rows(i):
        return pl.ds(pl.multiple_of(i * ROW_BLOCK, ROW_BLOCK), ROW_BLOCK)

    def xcopy(i, slot):
        return pltpu.make_async_copy(x_hbm.at[rows(i)], x_buf.at[slot], xsem.at[slot])

    def ycopy(i, slot):
        return pltpu.make_async_copy(y_buf.at[slot], y_hbm.at[rows(i)], ysem.at[slot])

    def fetch(k):
        e = used_ref[k]
        slot = k % W_SLOTS
        return (pltpu.make_async_copy(wg_hbm.at[e], wg_f.at[slot], wsem.at[slot, 0]),
                pltpu.make_async_copy(wu_hbm.at[e], wu_f.at[slot], wsem.at[slot, 1]),
                pltpu.make_async_copy(wd_hbm.at[e], wd_f.at[slot], wsem.at[slot, 2]))

    for k in range(W_SLOTS - 1):
        @pl.when(k < nue_ref[0])
        def _():
            for c in fetch(k):
                c.start()
    xcopy(0, 0).start()

    def block(i, carry):
        slot = i % 2
        xcopy(i, slot).wait()

        @pl.when(i + 1 < n)
        def _():
            xcopy(i + 1, 1 - slot).start()

        @pl.when(first_ref[i] == 1)
        def _():
            k = ord_ref[i]
            for c in fetch(k):
                c.wait()

            @pl.when(k + (W_SLOTS - 1) < nue_ref[0])
            def _():
                for c in fetch(k + (W_SLOTS - 1)):
                    c.start()

            wslot = k % W_SLOTS
            wg_b[...] = wg_f[wslot].astype(BF16)
            wu_b[...] = wu_f[wslot].astype(BF16)
            wd_b[...] = wd_f[wslot].astype(BF16)

        @pl.when(i >= 2)
        def _():
            ycopy(i - 2, slot).wait()

        x = x_buf[slot].reshape(ROW_BLOCK, D_MODEL)
        g = _dot(x, wg_b[...])
        u = _dot(x, wu_b[...])
        hmid = (g * _sigmoid(g) * u).astype(BF16)
        y = _dot(hmid, wd_b[...]).astype(BF16)
        y_buf[slot] = y.reshape(ROW_BLOCK, TOK_SUB, LANES)
        ycopy(i, slot).start()
        return carry

    lax.fori_loop(0, n, block, 0)

    @pl.when(n >= 2)
    def _():
        ycopy(n - 2, n % 2).wait()
    ycopy(n - 1, (n - 1) % 2).wait()

    y_buf[0] = jnp.zeros((ROW_BLOCK, TOK_SUB, LANES), BF16)

    def zcopy(b):
        return pltpu.make_async_copy(y_buf.at[0], y_hbm.at[rows(b)], ysem.at[0])

    def zstart(b, carry):
        zcopy(b).start()
        return carry

    def zwait(b, carry):
        zcopy(b).wait()
        return carry

    lax.fori_loop(n, N_BLOCKS, zstart, 0)
    lax.fori_loop(n, N_BLOCKS, zwait, 0)


def _experts(n_used, blk_first, blk_ord, used_e, n_used_e, xin, w_gate, w_up, w_down):
    grid_spec = pltpu.PrefetchScalarGridSpec(
        num_scalar_prefetch=5,
        grid=(1,),
        in_specs=[pl.BlockSpec(memory_space=pl.ANY),
                  pl.BlockSpec(memory_space=pl.ANY),
                  pl.BlockSpec(memory_space=pl.ANY),
                  pl.BlockSpec(memory_space=pl.ANY)],
        out_specs=pl.BlockSpec(memory_space=pl.ANY),
        scratch_shapes=[pltpu.VMEM((2, ROW_BLOCK, TOK_SUB, LANES), BF16),
                        pltpu.VMEM((2, ROW_BLOCK, TOK_SUB, LANES), BF16),
                        pltpu.VMEM((W_SLOTS, D_MODEL, D_EXPERT), F32),
                        pltpu.VMEM((W_SLOTS, D_MODEL, D_EXPERT), F32),
                        pltpu.VMEM((W_SLOTS, D_EXPERT, D_MODEL), F32),
                        pltpu.VMEM((D_MODEL, D_EXPERT), BF16),
                        pltpu.VMEM((D_MODEL, D_EXPERT), BF16),
                        pltpu.VMEM((D_EXPERT, D_MODEL), BF16),
                        pltpu.SemaphoreType.DMA((2,)), pltpu.SemaphoreType.DMA((2,)),
                        pltpu.SemaphoreType.DMA((W_SLOTS, 3))],
    )
    return pl.pallas_call(
        _expert_kernel,
        grid_spec=grid_spec,
        out_shape=jax.ShapeDtypeStruct((P_ROWS, TOK_SUB, LANES), BF16),
        compiler_params=pltpu.CompilerParams(dimension_semantics=("arbitrary",), vmem_limit_bytes=VMEM_LIMIT),
        name="experts",
    )(n_used, blk_first, blk_ord, used_e, n_used_e, xin, w_gate, w_up, w_down)


COMB_TM = 512
H1_SLOTS = 3


def _combine_kernel(d0_ref, d1_ref, h1_hbm, route_ref, yb_ref, out_ref, g_scr, sem, h1_scr, hsem):
    i = pl.program_id(0)
    n = pl.num_programs(0)

    def h1_copy(tile):
        hs = tile % H1_SLOTS
        return pltpu.make_async_copy(h1_hbm.at[pl.ds(tile * COMB_TM, COMB_TM)], h1_scr.at[hs], hsem.at[hs])

    @pl.when(i == 0)
    def _():
        for t in range(H1_SLOTS - 1):
            h1_copy(t).start(priority=1)

    @pl.when(i + (H1_SLOTS - 1) < n)
    def _():
        h1_copy(i + (H1_SLOTS - 1)).start(priority=1)

    def issue(tile, slot):
        base = tile * COMB_TM

        def start(r, carry):
            pltpu.make_async_copy(yb_ref.at[d0_ref[base + r]], g_scr.at[slot, 0, r], sem.at[slot, 0]).start()
            pltpu.make_async_copy(yb_ref.at[d1_ref[base + r]], g_scr.at[slot, 1, r], sem.at[slot, 1]).start()
            return carry

        lax.fori_loop(0, COMB_TM, start, 0, unroll=DMA_UNROLL)

    @pl.when(i == 0)
    def _():
        issue(0, 0)

    slot = i % 2

    @pl.when(i + 1 < n)
    def _():
        issue(i + 1, 1 - slot)

    for k in range(2):
        pltpu.make_async_copy(yb_ref.at[pl.ds(0, COMB_TM)], g_scr.at[slot, k], sem.at[slot, k]).wait()
    h1_copy(i).wait()

    route = route_ref[...]
    w0 = route[:, 2:3]
    w1 = route[:, 3:4]
    g0 = g_scr[slot, 0].reshape(COMB_TM, D_MODEL).astype(F32)
    g1 = g_scr[slot, 1].reshape(COMB_TM, D_MODEL).astype(F32)
    out_ref[...] = h1_scr[i % H1_SLOTS] + (g0 * w0 + g1 * w1)


def _combine(d0, d1, h1, route, yb):
    grid_spec = pltpu.PrefetchScalarGridSpec(
        num_scalar_prefetch=2,
        grid=(N_TOK // COMB_TM,),
        in_specs=[pl.BlockSpec(memory_space=pl.ANY),
                  pl.BlockSpec((COMB_TM, LANES), lambda i, *_: (i, 0)),
                  pl.BlockSpec(memory_space=pl.ANY)],
        out_specs=pl.BlockSpec((COMB_TM, D_MODEL), lambda i, *_: (i, 0)),
        scratch_shapes=[pltpu.VMEM((2, 2, COMB_TM, TOK_SUB, LANES), BF16),
                        pltpu.SemaphoreType.DMA((2, 2)),
                        pltpu.VMEM((H1_SLOTS, COMB_TM, D_MODEL), F32),
                        pltpu.SemaphoreType.DMA((H1_SLOTS,))],
    )
    return pl.pallas_call(
        _combine_kernel,
        grid_spec=grid_spec,
        out_shape=jax.ShapeDtypeStruct((N_TOK, D_MODEL), F32),
        compiler_params=pltpu.CompilerParams(dimension_semantics=("arbitrary",), vmem_limit_bytes=VMEM_LIMIT),
        name="combine",
    )(d0, d1, h1, route, yb)


def kernel(x, meta, norm1_g, w_in, b_in, dw_w, dw_b, conv_ln_g, conv_ln_b, w_conv_out, b_conv_out,
           q_norm_g, k_norm_g, b_forget, w_attn_out, w_out, norm2_g, w_group, b_group, w_router,
           b_router, w_gate, w_up, w_down):
    l = 0
    row = lambda v: v.reshape(1, -1).astype(F32)

    c_u, c_q, c_f = 2 * CONV_WIDTH, 2 * CONV_WIDTH, 2 * CONV_WIDTH + 3 * ATTN_WIDTH
    c_gc = c_f + N_HEADS
    c_ga = c_gc + D_MODEL
    wi, bi = w_in[l], b_in[l]
    wu, bu = wi[:, :c_u].astype(BF16), row(bi[:c_u])
    wqkv, bqkv = wi[:, c_q:c_f].astype(BF16), row(bi[c_q:c_f])
    wft = jnp.zeros((16, D_MODEL), F32).at[:N_HEADS].set(wi[:, c_f:c_gc].T).astype(BF16)
    bft = jnp.zeros((16, 1), F32).at[:N_HEADS, 0].set(bi[c_f:c_gc] + b_forget[l])
    wgc, bgc = wi[:, c_gc:c_ga].astype(BF16), row(bi[c_gc:c_ga])
    wga, bga = wi[:, c_ga:].astype(BF16), row(bi[c_ga:])
    qg = row(jnp.tile(q_norm_g[l], N_HEADS) * (HEAD_DIM ** -0.5 * LOG2E))
    kg = row(jnp.tile(k_norm_g[l], N_HEADS))
    head_of_col = jnp.arange(ATTN_WIDTH) // HEAD_DIM
    hsum = (head_of_col[:, None] == jnp.arange(LANES)[None, :]).astype(BF16)
    hexp_half = (jnp.arange(LANES)[:, None] == head_of_col[None, :]).astype(BF16)
    hexp = jnp.concatenate([hexp_half, hexp_half], axis=0)
    proj_consts = (row(norm1_g[l]), wu, bu, wqkv, bqkv, wft, bft, wgc, bgc, wga, bga, qg, kg, hsum, hexp)

    x2d = x.reshape(N_TOK, D_MODEL)
    meta_pad = jnp.zeros((META_ROWS, D_MODEL), F32).at[:N_META].set(meta.astype(F32))
    a_m, _, k_m, v_m, lf_m, _, _ = _proj(meta_pad, proj_consts, META_ROWS)
    conv_consts = (a_m, _conv_shift_matrix(), dw_w[l].astype(F32), row(dw_b[l]),
                   row(conv_ln_g[l]), row(conv_ln_b[l]))
    act, q, k, v, lf, sgc, sga = _proj(x2d, proj_consts, PROJ_TM, conv_consts)

    o = _attention_t(q, k, v, lf, k_m, v_m, lf_m)

    wrt = jnp.zeros((LANES, D_MODEL), F32).at[:N_GROUPS].set(w_group[l].T)
    wrt = wrt.at[N_GROUPS:N_GROUPS + N_EXPERTS].set(w_router[l].T).astype(BF16)
    brt = jnp.zeros((LANES, 1), F32).at[:N_GROUPS, 0].set(b_group[l])
    brt = brt.at[N_GROUPS:N_GROUPS + N_EXPERTS, 0].set(b_router[l])
    tri = jnp.triu(jnp.ones((MERGE_TM, MERGE_TM), F32), 1).astype(BF16)
    merge_consts = (w_conv_out[l].astype(BF16), row(b_conv_out[l]), w_attn_out[l].astype(BF16),
                    w_out[l].astype(BF16), row(norm2_g[l]), wrt, brt, tri)
    h1, hn, route, route_t, cnt = _merge(act, o, sgc, sga, x2d, merge_consts)

    counts = cnt[:N_EXPERTS, 0].astype(jnp.int32)
    padded = (counts + ROW_BLOCK - 1) // ROW_BLOCK * ROW_BLOCK
    pend = jnp.cumsum(padded).astype(jnp.int32)
    pstart = pend - padded
    blk_row = jnp.arange(N_BLOCKS, dtype=jnp.int32) * ROW_BLOCK
    n_used = pend[-1:] // ROW_BLOCK
    nz = counts > 0
    before = (pend[None, :] <= blk_row[:, None]) & nz[None, :]
    blk_ord = jnp.sum(before.astype(jnp.int32), axis=1)
    blk_first = jnp.any((pstart[None, :] == blk_row[:, None]) & nz[None, :], axis=1).astype(jnp.int32)
    e_ord = jnp.cumsum(nz.astype(jnp.int32)) - 1
    e_ids = jnp.arange(N_EXPERTS, dtype=jnp.int32)
    used_e = jnp.sum(jnp.where((e_ord[None, :] == e_ids[:, None]) & nz[None, :], e_ids[None, :], 0), axis=1)
    n_used_e = jnp.sum(nz.astype(jnp.int32)).reshape(1)

    dest = _dest(pstart, route_t)
    d0, d1 = dest[0], dest[1]
    xin = _dispatch(d0, d1, pstart + counts, pend, hn)
    yb = _experts(n_used, blk_first, blk_ord, used_e.astype(jnp.int32), n_used_e, xin,
                  w_gate[l], w_up[l], w_down[l])
    out = _combine(d0, d1, h1, route, yb)
    return out.reshape(BATCH, SEQ, D_MODEL)
```

```python
import jax
import jax.numpy as jnp
from jax import lax
from jax.experimental import pallas as pl
from jax.experimental.pallas import tpu as pltpu

D_MODEL = 1024
BATCH = 8
SEQ = 2048
N_META = 16
CONV_WIDTH = 512
CONV_K = 31
N_HEADS = 8
HEAD_DIM = 64
ATTN_WIDTH = N_HEADS * HEAD_DIM
N_GROUPS = 8
EXPERTS_PER_GROUP = 8
N_EXPERTS = N_GROUPS * EXPERTS_PER_GROUP
D_EXPERT = 256
RMS_EPS = 1e-6
LN_EPS = 1e-5

N_TOK = BATCH * SEQ
LANES = 128
META_ROWS = 128
ROW_BLOCK = 512
N_ASSIGN = 2 * N_TOK
N_BLOCKS = N_ASSIGN // ROW_BLOCK + N_EXPERTS
P_ROWS = N_BLOCKS * ROW_BLOCK
TOK_SUB = D_MODEL // LANES
PROJ_TM = 1024
CONV_CHUNK = 128
CONV_PAD = 32
CONV_WIN = CONV_CHUNK + 32
CONV_SHROWS = CONV_CHUNK + 24
CONV_COPIES = 7
CONV_LANES = 128
ATT_TQ = SEQ
ATT_TK = 256
N_PAIRS = N_HEADS // 2
VMEM_LIMIT = 56 * 1024 * 1024

F32 = jnp.float32
BF16 = jnp.bfloat16
NEG_INF = float("-inf")
LOG2E = 1.4426950408889634


def _dot(a, b):
    return jnp.dot(a, b, preferred_element_type=F32)


def _dot_nt(a, b):
    return lax.dot_general(a, b, (((1,), (1,)), ((), ())), preferred_element_type=F32)


def _sigmoid(x):
    return 1.0 / (1.0 + jnp.exp(-x))


def _const_spec(shape):
    nd = len(shape)
    return pl.BlockSpec(shape, lambda *_: (0,) * nd)


def _head_rms(t, hsum_ref, hexp_ref, gain):
    ss = _dot((t * t).astype(BF16), hsum_ref[...])
    inv = lax.rsqrt(ss * (1.0 / HEAD_DIM) + RMS_EPS)
    hi = inv.astype(BF16)
    lo = (inv - hi.astype(F32)).astype(BF16)
    invb = _dot(jnp.concatenate([hi, lo], axis=1), hexp_ref[...])
    return t * invb * gain


N_PROJ_IN = 16


def _proj_stages(x_ref, g1_ref, wu_ref, bu_ref, wqkv_ref, bqkv_ref, wft_ref, bft_ref, wgc_ref, bgc_ref,
                 wga_ref, bga_ref, qg_ref, kg_ref, hsum_ref, hexp_ref,
                 q_ref, k_ref, v_ref, lf_ref, sgc_ref, sga_ref):
    x = x_ref[...]
    ms = jnp.mean(x * x, axis=-1, keepdims=True)
    xn = (x * lax.rsqrt(ms + RMS_EPS) * g1_ref[...]).astype(BF16)

    def glu():
        u = _dot(xn, wu_ref[...]) + bu_ref[...]
        return (u[:, :CONV_WIDTH] * _sigmoid(u[:, CONV_WIDTH:])).astype(BF16)

    def qkv_stage():
        qkv = _dot(xn, wqkv_ref[...]) + bqkv_ref[...]
        q_ref[...] = _head_rms(qkv[:, :ATTN_WIDTH], hsum_ref, hexp_ref, qg_ref[...]).astype(BF16)
        k_ref[...] = _head_rms(qkv[:, ATTN_WIDTH:2 * ATTN_WIDTH], hsum_ref, hexp_ref, kg_ref[...]).astype(BF16)
        v = qkv[:, 2 * ATTN_WIDTH:]
        vb = v_ref.shape[2]
        for c in range(v_ref.shape[0]):
            v_ref[c] = v[c * vb:(c + 1) * vb, :].T.astype(BF16)

    def gate_stage_c():
        f = _dot_nt(wft_ref[...], xn) + bft_ref[...]
        lf_ref[...] = jnp.minimum(f, 0.0) - jnp.log(1.0 + jnp.exp(-jnp.abs(f)))
        sgc_ref[...] = _sigmoid(_dot(xn, wgc_ref[...]) + bgc_ref[...]).astype(BF16)

    def gate_stage_a():
        sga_ref[...] = _sigmoid(_dot(xn, wga_ref[...]) + bga_ref[...]).astype(BF16)

    return glu, (qkv_stage, gate_stage_c, gate_stage_a)


def _proj_kernel(*refs):
    ins, a_ref, outs = refs[:N_PROJ_IN], refs[N_PROJ_IN], refs[N_PROJ_IN + 1:]
    glu, stages = _proj_stages(*ins, *outs)
    a_ref[...] = glu()
    for stage in stages:
        stage()


def _proj_conv_kernel(*refs):
    ins = refs[:N_PROJ_IN]
    am_ref, shift_ref, dw_ref, dwb_ref, lng_ref, lnb_ref = refs[N_PROJ_IN:N_PROJ_IN + 6]
    act_ref, outs = refs[N_PROJ_IN + 6], refs[N_PROJ_IN + 7:N_PROJ_IN + 13]
    win_scr, sh_scr = refs[N_PROJ_IN + 13:]
    tm = act_ref.shape[0]
    i = pl.program_id(0)
    first = (i % (SEQ // tm)) == 0

    @pl.when(first)
    def _():
        win_scr[0:N_META, :] = jnp.zeros((N_META, CONV_WIDTH), BF16)
        win_scr[N_META:CONV_PAD, :] = am_ref[0:N_META, :]

    @pl.when(jnp.logical_not(first))
    def _():
        win_scr[0:CONV_PAD, :] = win_scr[tm:tm + CONV_PAD, :]

    glu, stages = _proj_stages(*ins, *outs)
    win_scr[CONV_PAD:CONV_PAD + tm, :] = glu()

    def conv_chunk(ci):
        r0 = ci * CONV_CHUNK
        slot = ci % 2
        sh_scr[slot] = _dot(shift_ref[...], win_scr[r0:r0 + CONV_WIN, :])
        parts = []
        for c0 in range(0, CONV_WIDTH, CONV_LANES):
            cs = slice(c0, c0 + CONV_LANES)
            part = jnp.zeros((CONV_CHUNK, CONV_LANES), F32) + dwb_ref[:, cs]
            for b in range(8):
                if b == 0:
                    rows = win_scr[r0:r0 + CONV_WIN, cs].astype(F32)
                else:
                    rows = sh_scr[slot, (b - 1) * CONV_SHROWS:b * CONV_SHROWS, cs]
                for j in range(CONV_K):
                    off = j + 2
                    if off % 8 == b:
                        part = part + rows[off - b:off - b + CONV_CHUNK] * dw_ref[j:j + 1, cs]
            parts.append(part)
        acc = jnp.concatenate(parts, axis=1)
        mu = jnp.mean(acc, axis=-1, keepdims=True)
        d = acc - mu
        var = jnp.mean(d * d, axis=-1, keepdims=True)
        y = d * lax.rsqrt(var + LN_EPS) * lng_ref[...] + lnb_ref[...]
        act_ref[r0:r0 + CONV_CHUNK, :] = (y * _sigmoid(y)).astype(BF16)

    n_chunks = tm // CONV_CHUNK
    per_stage = -(-n_chunks // (len(stages) + 1))
    ci = 0
    for stage in (None,) + stages:
        if stage is not None:
            stage()
        for _ in range(per_stage):
            if ci < n_chunks:
                conv_chunk(ci)
                ci += 1


def _proj(x2d, consts, tm, conv_consts=None):
    n = x2d.shape[0]
    row = lambda w: pl.BlockSpec((tm, w), lambda i: (i, 0))
    in_specs = [row(D_MODEL)] + [_const_spec(c.shape) for c in consts]
    fused = conv_consts is not None
    extra = tuple(conv_consts) if fused else ()
    in_specs += [_const_spec(c.shape) for c in extra]
    scratch = [pltpu.VMEM((CONV_PAD + tm, CONV_WIDTH), BF16),
               pltpu.VMEM((2, CONV_COPIES * CONV_SHROWS, CONV_WIDTH), F32)] if fused else []
    vb = min(tm, ATT_TK)
    out_shape = [
        jax.ShapeDtypeStruct((n, CONV_WIDTH), BF16),
        jax.ShapeDtypeStruct((n, ATTN_WIDTH), BF16),
        jax.ShapeDtypeStruct((n, ATTN_WIDTH), BF16),
        jax.ShapeDtypeStruct((n // vb, ATTN_WIDTH, vb), BF16),
        jax.ShapeDtypeStruct((16, n), F32),
        jax.ShapeDtypeStruct((n, D_MODEL), BF16),
        jax.ShapeDtypeStruct((n, D_MODEL), BF16),
    ]
    out_specs = [row(CONV_WIDTH), row(ATTN_WIDTH), row(ATTN_WIDTH),
                 pl.BlockSpec((tm // vb, ATTN_WIDTH, vb), lambda i: (i, 0, 0)),
                 pl.BlockSpec((16, tm), lambda i: (0, i)), row(D_MODEL), row(D_MODEL)]
    return pl.pallas_call(
        _proj_conv_kernel if fused else _proj_kernel,
        grid=(n // tm,),
        in_specs=in_specs,
        out_specs=out_specs,
        out_shape=out_shape,
        scratch_shapes=scratch,
        compiler_params=pltpu.CompilerParams(dimension_semantics=("arbitrary",), vmem_limit_bytes=VMEM_LIMIT),
        name="proj_conv" if fused else "proj",
    )(x2d, *consts, *extra)


def _conv_shift_matrix():
    rr = jnp.arange(CONV_COPIES * CONV_SHROWS)
    src = rr % CONV_SHROWS + rr // CONV_SHROWS + 1
    return (src[:, None] == jnp.arange(CONV_WIN)[None, :]).astype(BF16)


AUG_TERMS = 3
ACC_ONES = 16


def _split_terms(x):
    hi = x.astype(BF16)
    r1 = x - hi.astype(F32)
    mid = r1.astype(BF16)
    lo = (r1 - mid.astype(F32)).astype(BF16)
    return hi, mid, lo


def _attn_t_kernel(q_ref, k_ref, vt_ref, lf_ref, km_ref, vm_ref, lfm_ref, place_ref, o_ref,
                   aug_scr, augm_scr, m_scr, acc_scr):
    p = pl.program_id(1)
    nk = SEQ // ATT_TK

    def bias_lanes(colsum):
        out = None
        for t, term in enumerate(_split_terms(colsum)):
            d = _dot(term, place_ref[t])
            out = d if out is None else out + d
        return out.astype(BF16)

    @pl.when(p == 0)
    def _():
        r = lax.broadcasted_iota(jnp.int32, (ATT_TK, ATT_TK), 0)
        c = lax.broadcasted_iota(jnp.int32, (ATT_TK, ATT_TK), 1)
        tri = (c <= r).astype(BF16)
        local = []
        for j in range(nk):
            lf = lf_ref[:, j * ATT_TK:(j + 1) * ATT_TK]
            csum = None
            for term in _split_terms(lf):
                d = _dot_nt(tri, term)
                csum = d if csum is None else csum + d
            local.append(csum)
        carry = jnp.zeros((1, 16), F32)
        for j in range(nk):
            csum = local[j] + carry
            aug_scr[j * ATT_TK:(j + 1) * ATT_TK, :] = bias_lanes(csum * (-LOG2E))
            carry = csum[ATT_TK - 1:ATT_TK, :]
        rm = lax.broadcasted_iota(jnp.int32, (META_ROWS, META_ROWS), 0)
        cm = lax.broadcasted_iota(jnp.int32, (META_ROWS, META_ROWS), 1)
        trim = ((cm > rm) & (cm < N_META)).astype(BF16)
        msum = jnp.zeros((META_ROWS, 16), F32)
        for term in _split_terms(lfm_ref[...]):
            msum = msum + _dot_nt(trim, term)
        augm_scr[...] = bias_lanes(msum * LOG2E)

    lane = lax.broadcasted_iota(jnp.int32, (ATT_TQ, LANES), 1)
    q_all = q_ref[...]
    zq = jnp.zeros_like(q_all)
    q_cat = []
    for hl in range(2):
        own = (lane < HEAD_DIM) if hl == 0 else (lane >= HEAD_DIM)
        h = 2 * p + hl
        bias_sel = (lane >= AUG_TERMS * h) & (lane < AUG_TERMS * h + AUG_TERMS)
        bias_ones = jnp.where(bias_sel, 1.0, 0.0).astype(BF16)
        q_cat.append(jnp.concatenate([jnp.where(own, q_all, zq), bias_ones], axis=1))
        m_scr[hl] = jnp.full((1, ATT_TQ), NEG_INF, F32)
        acc_scr[hl] = jnp.zeros((HEAD_DIM + ACC_ONES, ATT_TQ), F32)

    def scores(kb, augb, lo=0):
        k_cat = jnp.concatenate([kb, augb], axis=1)
        return tuple(_dot_nt(k_cat, q_cat[hl][lo:]) for hl in range(2))

    def block_softmax(s2, vtb):
        ones = jnp.ones((ACC_ONES, vtb.shape[1]), BF16)
        out = []
        for hl in range(2):
            s = s2[hl]
            m_blk = jnp.max(s, axis=0, keepdims=True)
            pr = jnp.exp2(s - m_blk).astype(BF16)
            v_own = jnp.concatenate([vtb[hl * HEAD_DIM:(hl + 1) * HEAD_DIM], ones], axis=0)
            out.append((m_blk, _dot(v_own, pr)))
        return out

    def block_scores(j):
        rs = slice(j * ATT_TK, (j + 1) * ATT_TK)
        return scores(k_ref[rs, :], aug_scr[rs, :], lo=j * ATT_TK)

    s_meta = scores(km_ref[0:N_META, :], augm_scr[0:N_META, :])
    s_blocks = [block_scores(j) for j in range(nk)]
    partials = [(0, block_softmax(s_meta, vm_ref[0][:, 0:N_META]))]

    srow = lax.broadcasted_iota(jnp.int32, (ATT_TK, ATT_TK), 0)
    tcol = lax.broadcasted_iota(jnp.int32, (ATT_TK, ATT_TK), 1)
    tri = srow <= tcol
    for j in range(nk):
        s_cur = tuple(jnp.concatenate([jnp.where(tri, s[:, :ATT_TK], NEG_INF), s[:, ATT_TK:]], axis=1)
                      if s.shape[1] > ATT_TK else jnp.where(tri, s, NEG_INF) for s in s_blocks[j])
        partials.append((j * ATT_TK, block_softmax(s_cur, vt_ref[j])))

    for hl in range(2):
        for lo, part in partials:
            m_scr[hl, :, lo:] = jnp.maximum(m_scr[hl, :, lo:], part[hl][0])
        for lo, part in partials:
            m_blk, pv = part[hl]
            acc_scr[hl, :, lo:] = acc_scr[hl, :, lo:] + pv * jnp.exp2(m_blk - m_scr[hl, :, lo:])

    a0 = acc_scr[0]
    a1 = acc_scr[1]
    ot = jnp.concatenate([a0[0:HEAD_DIM] / a0[HEAD_DIM:HEAD_DIM + 1], a1[0:HEAD_DIM] / a1[HEAD_DIM:HEAD_DIM + 1]],
                         axis=0)
    o_ref[...] = ot.T.astype(BF16)


def _attention_t(q, k, vt, lf, km, vm, lfm):
    nk = SEQ // ATT_TK
    hh = jnp.arange(16)[:, None]
    ll = jnp.arange(LANES)[None, :]
    place = jnp.stack([((ll == AUG_TERMS * hh + t) & (hh < N_HEADS)).astype(BF16) for t in range(AUG_TERMS)])
    return pl.pallas_call(
        _attn_t_kernel,
        grid=(BATCH, N_PAIRS),
        in_specs=[
            pl.BlockSpec((ATT_TQ, LANES), lambda b, p: (b, p)),
            pl.BlockSpec((SEQ, LANES), lambda b, p: (b, p)),
            pl.BlockSpec((nk, LANES, ATT_TK), lambda b, p: (b, p, 0)),
            pl.BlockSpec((16, SEQ), lambda b, p: (0, b)),
            pl.BlockSpec((META_ROWS, LANES), lambda b, p: (0, p)),
            pl.BlockSpec((1, LANES, META_ROWS), lambda b, p: (0, p, 0)),
            pl.BlockSpec((16, META_ROWS), lambda b, p: (0, 0)),
            _const_spec(place.shape),
        ],
        out_specs=pl.BlockSpec((ATT_TQ, LANES), lambda b, p: (b, p)),
        out_shape=jax.ShapeDtypeStruct((N_TOK, ATTN_WIDTH), BF16),
        scratch_shapes=[
            pltpu.VMEM((SEQ, LANES), BF16),
            pltpu.VMEM((META_ROWS, LANES), BF16),
            pltpu.VMEM((2, 1, ATT_TQ), F32),
            pltpu.VMEM((2, HEAD_DIM + ACC_ONES, ATT_TQ), F32),
        ],
        compiler_params=pltpu.CompilerParams(
            dimension_semantics=("arbitrary", "arbitrary"), vmem_limit_bytes=VMEM_LIMIT),
        name="attn",
    )(q, k, vt, lf, km, vm, lfm, place)


MERGE_TM = 1024


def _merge_kernel(act_ref, o_ref, sgc_ref, sga_ref, x_ref, wpw_ref, bpw_ref, wao_ref, wout_ref, g2_ref,
                  wrt_ref, brt_ref, tri_ref,
                  h1_ref, hn_ref, route_ref, routet_ref, cnt_ref, base_scr):
    i = pl.program_id(0)

    @pl.when(i == 0)
    def _():
        base_scr[...] = jnp.zeros_like(base_scr)

    tm = x_ref.shape[0]
    hn_parts = []
    for rs in (slice(0, tm // 2), slice(tm // 2, tm)):
        y_conv = _dot(act_ref[rs, :], wpw_ref[...]) + bpw_ref[...]
        y_attn = _dot(o_ref[rs, :], wao_ref[...])
        merged = sgc_ref[rs, :].astype(F32) * y_conv + sga_ref[rs, :].astype(F32) * y_attn
        h1 = x_ref[rs, :] + _dot(merged.astype(BF16), wout_ref[...])
        h1_ref[rs, :] = h1
        ms = jnp.mean(h1 * h1, axis=-1, keepdims=True)
        hn_half = (h1 * lax.rsqrt(ms + RMS_EPS) * g2_ref[...]).astype(BF16)
        hn_ref[rs] = hn_half.reshape(tm // 2, TOK_SUB, LANES)
        hn_parts.append(hn_half)
    hn = jnp.concatenate(hn_parts, axis=0)

    logits = _dot_nt(wrt_ref[...], hn) + brt_ref[...]
    row = lax.broadcasted_iota(jnp.int32, logits.shape, 0).astype(F32)
    big = float(LANES)
    is_g = row < N_GROUPS
    gl = jnp.where(is_g, logits, NEG_INF)
    gmax = jnp.max(gl, axis=0, keepdims=True)
    gidx = jnp.min(jnp.where(gl == gmax, row, big), axis=0, keepdims=True)
    gsum = jnp.sum(jnp.where(is_g, jnp.exp(gl - gmax), 0.0), axis=0, keepdims=True)
    g_w = 1.0 / gsum
    lo = N_GROUPS + EXPERTS_PER_GROUP * gidx
    el = jnp.where((row >= lo) & (row < lo + EXPERTS_PER_GROUP), logits, NEG_INF)
    t1 = jnp.max(el, axis=0, keepdims=True)
    i1 = jnp.min(jnp.where(el == t1, row, big), axis=0, keepdims=True)
    el2 = jnp.where(row == i1, NEG_INF, el)
    t2 = jnp.max(el2, axis=0, keepdims=True)
    i2 = jnp.min(jnp.where(el2 == t2, row, big), axis=0, keepdims=True)
    e2 = jnp.exp(t2 - t1)
    den = 1.0 + e2
    w1 = (1.0 / den) * g_w
    w2 = (e2 / den) * g_w
    ea = i1 - N_GROUPS
    eb = i2 - N_GROUPS

    oha = (row == ea).astype(F32)
    ohb = (row == eb).astype(F32)
    ohs = oha + ohb
    before = _dot(ohs.astype(BF16), tri_ref[...]) + base_scr[...][:, 0:1]
    ra = jnp.sum(before * oha, axis=0, keepdims=True)
    rb = jnp.sum(before * ohb, axis=0, keepdims=True)
    base_scr[...] = base_scr[...] + jnp.sum(ohs, axis=1, keepdims=True)
    cnt_ref[...] = base_scr[...]

    def rows_of(shape):
        r = lax.broadcasted_iota(jnp.int32, shape, 0)
        out = jnp.where(r == 0, ea, 0.0)
        for k, val in enumerate((eb, w1, w2, ra, rb), start=1):
            out = jnp.where(r == k, val, out)
        return out

    routet_ref[...] = rows_of((8, logits.shape[1]))
    route_ref[...] = rows_of(logits.shape).T


def _merge(act, o, sgc, sga, x2d, consts):
    tm = MERGE_TM
    row = lambda w: pl.BlockSpec((tm, w), lambda i: (i, 0))
    return pl.pallas_call(
        _merge_kernel,
        grid=(N_TOK // tm,),
        in_specs=[row(CONV_WIDTH), row(ATTN_WIDTH), row(D_MODEL), row(D_MODEL), row(D_MODEL)]
        + [_const_spec(c.shape) for c in consts],
        out_specs=[row(D_MODEL), pl.BlockSpec((tm, TOK_SUB, LANES), lambda i: (i, 0, 0)), row(LANES),
                   pl.BlockSpec((8, tm), lambda i: (0, i)), _const_spec((LANES, LANES))],
        out_shape=[jax.ShapeDtypeStruct((N_TOK, D_MODEL), F32),
                   jax.ShapeDtypeStruct((N_TOK, TOK_SUB, LANES), BF16),
                   jax.ShapeDtypeStruct((N_TOK, LANES), F32),
                   jax.ShapeDtypeStruct((8, N_TOK), F32),
                   jax.ShapeDtypeStruct((LANES, LANES), F32)],
        scratch_shapes=[pltpu.VMEM((LANES, LANES), F32)],
        compiler_params=pltpu.CompilerParams(dimension_semantics=("arbitrary",), vmem_limit_bytes=VMEM_LIMIT),
        name="merge",
    )(act, o, sgc, sga, x2d, *consts)


DISP_TM = 2048
DMA_UNROLL = 8


def _dest_kernel(pstart_ref, rt_ref, d_ref):
    rt = rt_ref[...]
    start = jnp.zeros_like(rt)
    for e in range(N_EXPERTS):
        start = jnp.where(rt == float(e), pstart_ref[e].astype(F32), start)
    d_ref[...] = (start + pltpu.roll(rt, 4, axis=0)).astype(jnp.int32)


def _dest(pstart, route_t):
    grid_spec = pltpu.PrefetchScalarGridSpec(
        num_scalar_prefetch=1,
        grid=(1,),
        in_specs=[pl.BlockSpec(route_t.shape, lambda i, *_: (0, 0))],
        out_specs=pl.BlockSpec(route_t.shape, lambda i, *_: (0, 0)),
    )
    return pl.pallas_call(
        _dest_kernel,
        grid_spec=grid_spec,
        out_shape=jax.ShapeDtypeStruct(route_t.shape, jnp.int32),
        compiler_params=pltpu.CompilerParams(dimension_semantics=("arbitrary",), vmem_limit_bytes=VMEM_LIMIT),
        name="dest",
    )(pstart, route_t)


ZERO_CHUNK = 64


def _dispatch_kernel(d0_ref, d1_ref, pfill_ref, pend_ref, hn_ref, xin_ref, zero_scr, sem, zsem):
    i = pl.program_id(0)
    base = i * DISP_TM

    @pl.when(i == 0)
    def _():
        zero_scr[...] = jnp.zeros_like(zero_scr)

        def pad_chunks(e, fn):
            first = pfill_ref[e] // ZERO_CHUNK * ZERO_CHUNK

            def one(c, carry):
                start = pl.multiple_of(first + c * ZERO_CHUNK, ZERO_CHUNK)
                fn(pltpu.make_async_copy(zero_scr.at[pl.ds(0, ZERO_CHUNK)], xin_ref.at[pl.ds(start, ZERO_CHUNK)],
                                         zsem.at[0]))
                return carry

            lax.fori_loop(0, (pend_ref[e] - first) // ZERO_CHUNK, one, 0)

        def zstart(e, carry):
            pad_chunks(e, lambda c: c.start())
            return carry

        def zwait(e, carry):
            pad_chunks(e, lambda c: c.wait())
            return carry

        n_used = pend_ref[N_EXPERTS - 1] // ROW_BLOCK

        def tcopy(b):
            return pltpu.make_async_copy(
                zero_scr, xin_ref.at[pl.ds(pl.multiple_of(b * ROW_BLOCK, ROW_BLOCK), ROW_BLOCK)], zsem.at[0])

        def tstart(b, carry):
            tcopy(b).start()
            return carry

        def twait(b, carry):
            tcopy(b).wait()
            return carry

        lax.fori_loop(0, N_EXPERTS, zstart, 0)
        lax.fori_loop(n_used, N_BLOCKS, tstart, 0)
        lax.fori_loop(0, N_EXPERTS, zwait, 0)
        lax.fori_loop(n_used, N_BLOCKS, twait, 0)

    def start(r, carry):
        pltpu.make_async_copy(hn_ref.at[r], xin_ref.at[d0_ref[base + r]], sem.at[0]).start()
        pltpu.make_async_copy(hn_ref.at[r], xin_ref.at[d1_ref[base + r]], sem.at[1]).start(priority=1)
        return carry

    lax.fori_loop(0, DISP_TM, start, 0, unroll=DMA_UNROLL)
    pltpu.make_async_copy(hn_ref, xin_ref.at[pl.ds(0, DISP_TM)], sem.at[0]).wait()
    pltpu.make_async_copy(hn_ref, xin_ref.at[pl.ds(0, DISP_TM)], sem.at[1]).wait()


def _dispatch(d0, d1, pfill, pend, hn):
    grid_spec = pltpu.PrefetchScalarGridSpec(
        num_scalar_prefetch=4,
        grid=(N_TOK // DISP_TM,),
        in_specs=[pl.BlockSpec((DISP_TM, TOK_SUB, LANES), lambda i, *_: (i, 0, 0))],
        out_specs=pl.BlockSpec(memory_space=pl.ANY),
        scratch_shapes=[pltpu.VMEM((ROW_BLOCK, TOK_SUB, LANES), BF16),
                        pltpu.SemaphoreType.DMA((2,)), pltpu.SemaphoreType.DMA((1,))],
    )
    return pl.pallas_call(
        _dispatch_kernel,
        grid_spec=grid_spec,
        out_shape=jax.ShapeDtypeStruct((P_ROWS, TOK_SUB, LANES), BF16),
        compiler_params=pltpu.CompilerParams(dimension_semantics=("arbitrary",), vmem_limit_bytes=VMEM_LIMIT),
        name="dispatch",
    )(d0, d1, pfill, pend, hn)


W_SLOTS = 3


def _expert_kernel(nu_ref, first_ref, ord_ref, used_ref, nue_ref, x_hbm, wg_hbm, wu_hbm, wd_hbm, y_hbm,
                   x_buf, y_buf, wg_f, wu_f, wd_f, wg_b, wu_b, wd_b, xsem, ysem, wsem):
    n = nu_ref[0]

    def rows(i):
        return pl.ds(pl.multiple_of(i * ROW_BLOCK, ROW_BLOCK), ROW_BLOCK)

    def xcopy(i, slot):
        return pltpu.make_async_copy(x_hbm.at[rows(i)], x_buf.at[slot], xsem.at[slot])

    def ycopy(i, slot):
        return pltpu.make_async_copy(y_buf.at[slot], y_hbm.at[rows(i)], ysem.at[slot])

    def fetch(k):
        e = used_ref[k]
        slot = k % W_SLOTS
        return (pltpu.make_async_copy(wg_hbm.at[e], wg_f.at[slot], wsem.at[slot, 0]),
                pltpu.make_async_copy(wu_hbm.at[e], wu_f.at[slot], wsem.at[slot, 1]),
                pltpu.make_async_copy(wd_hbm.at[e], wd_f.at[slot], wsem.at[slot, 2]))

    for k in range(W_SLOTS - 1):
        @pl.when(k < nue_ref[0])
        def _():
            for c in fetch(k):
                c.start()
    xcopy(0, 0).start()

    def block(i, carry):
        slot = i % 2
        xcopy(i, slot).wait()

        @pl.when(i + 1 < n)
        def _():
            xcopy(i + 1, 1 - slot).start()

        @pl.when(first_ref[i] == 1)
        def _():
            k = ord_ref[i]
            for c in fetch(k):
                c.wait()

            @pl.when(k + (W_SLOTS - 1) < nue_ref[0])
            def _():
                for c in fetch(k + (W_SLOTS - 1)):
                    c.start()

            wslot = k % W_SLOTS
            wg_b[...] = wg_f[wslot].astype(BF16)
            wu_b[...] = wu_f[wslot].astype(BF16)
            wd_b[...] = wd_f[wslot].astype(BF16)

        @pl.when(i >= 2)
        def _():
            ycopy(i - 2, slot).wait()

        x = x_buf[slot].reshape(ROW_BLOCK, D_MODEL)
        g = _dot(x, wg_b[...])
        u = _dot(x, wu_b[...])
        hmid = (g * _sigmoid(g) * u).astype(BF16)
        y = _dot(hmid, wd_b[...]).astype(BF16)
        y_buf[slot] = y.reshape(ROW_BLOCK, TOK_SUB, LANES)
        ycopy(i, slot).start()
        return carry

    lax.fori_loop(0, n, block, 0)

    @pl.when(n >= 2)
    def _():
        ycopy(n - 2, n % 2).wait()
    ycopy(n - 1, (n - 1) % 2).wait()

    y_buf[0] = jnp.zeros((ROW_BLOCK, TOK_SUB, LANES), BF16)

    def zcopy(b):
        return pltpu.make_async_copy(y_buf.at[0], y_hbm.at[rows(b)], ysem.at[0])

    def zstart(b, carry):
        zcopy(b).start()
        return carry

    def zwait(b, carry):
        zcopy(b).wait()
        return carry

    lax.fori_loop(n, N_BLOCKS, zstart, 0)
    lax.fori_loop(n, N_BLOCKS, zwait, 0)


def _experts(n_used, blk_first, blk_ord, used_e, n_used_e, xin, w_gate, w_up, w_down):
    grid_spec = pltpu.PrefetchScalarGridSpec(
        num_scalar_prefetch=5,
        grid=(1,),
        in_specs=[pl.BlockSpec(memory_space=pl.ANY),
                  pl.BlockSpec(memory_space=pl.ANY),
                  pl.BlockSpec(memory_space=pl.ANY),
                  pl.BlockSpec(memory_space=pl.ANY)],
        out_specs=pl.BlockSpec(memory_space=pl.ANY),
        scratch_shapes=[pltpu.VMEM((2, ROW_BLOCK, TOK_SUB, LANES), BF16),
                        pltpu.VMEM((2, ROW_BLOCK, TOK_SUB, LANES), BF16),
                        pltpu.VMEM((W_SLOTS, D_MODEL, D_EXPERT), F32),
                        pltpu.VMEM((W_SLOTS, D_MODEL, D_EXPERT), F32),
                        pltpu.VMEM((W_SLOTS, D_EXPERT, D_MODEL), F32),
                        pltpu.VMEM((D_MODEL, D_EXPERT), BF16),
                        pltpu.VMEM((D_MODEL, D_EXPERT), BF16),
                        pltpu.VMEM((D_EXPERT, D_MODEL), BF16),
                        pltpu.SemaphoreType.DMA((2,)), pltpu.SemaphoreType.DMA((2,)),
                        pltpu.SemaphoreType.DMA((W_SLOTS, 3))],
    )
    return pl.pallas_call(
        _expert_kernel,
        grid_spec=grid_spec,
        out_shape=jax.ShapeDtypeStruct((P_ROWS, TOK_SUB, LANES), BF16),
        compiler_params=pltpu.CompilerParams(dimension_semantics=("arbitrary",), vmem_limit_bytes=VMEM_LIMIT),
        name="experts",
    )(n_used, blk_first, blk_ord, used_e, n_used_e, xin, w_gate, w_up, w_down)


COMB_TM = 512
H1_SLOTS = 3


def _combine_kernel(d0_ref, d1_ref, h1_hbm, route_ref, yb_ref, out_ref, g_scr, sem, h1_scr, hsem):
    i = pl.program_id(0)
    n = pl.num_programs(0)

    def h1_copy(tile):
        hs = tile % H1_SLOTS
        return pltpu.make_async_copy(h1_hbm.at[pl.ds(tile * COMB_TM, COMB_TM)], h1_scr.at[hs], hsem.at[hs])

    @pl.when(i == 0)
    def _():
        for t in range(H1_SLOTS - 1):
            h1_copy(t).start(priority=1)

    @pl.when(i + (H1_SLOTS - 1) < n)
    def _():
        h1_copy(i + (H1_SLOTS - 1)).start(priority=1)

    def issue(tile, slot):
        base = tile * COMB_TM

        def start(r, carry):
            pltpu.make_async_copy(yb_ref.at[d0_ref[base + r]], g_scr.at[slot, 0, r], sem.at[slot, 0]).start()
            pltpu.make_async_copy(yb_ref.at[d1_ref[base + r]], g_scr.at[slot, 1, r], sem.at[slot, 1]).start(
                priority=1)
            return carry

        lax.fori_loop(0, COMB_TM, start, 0, unroll=DMA_UNROLL)

    @pl.when(i == 0)
    def _():
        issue(0, 0)

    slot = i % 2

    @pl.when(i + 1 < n)
    def _():
        issue(i + 1, 1 - slot)

    for k in range(2):
        pltpu.make_async_copy(yb_ref.at[pl.ds(0, COMB_TM)], g_scr.at[slot, k], sem.at[slot, k]).wait()
    h1_copy(i).wait()

    route = route_ref[...]
    w0 = route[:, 2:3]
    w1 = route[:, 3:4]
    g0 = g_scr[slot, 0].reshape(COMB_TM, D_MODEL).astype(F32)
    g1 = g_scr[slot, 1].reshape(COMB_TM, D_MODEL).astype(F32)
    out_ref[...] = h1_scr[i % H1_SLOTS] + (g0 * w0 + g1 * w1)


def _combine(d0, d1, h1, route, yb):
    grid_spec = pltpu.PrefetchScalarGridSpec(
        num_scalar_prefetch=2,
        grid=(N_TOK // COMB_TM,),
        in_specs=[pl.BlockSpec(memory_space=pl.ANY),
                  pl.BlockSpec((COMB_TM, LANES), lambda i, *_: (i, 0)),
                  pl.BlockSpec(memory_space=pl.ANY)],
        out_specs=pl.BlockSpec((COMB_TM, D_MODEL), lambda i, *_: (i, 0)),
        scratch_shapes=[pltpu.VMEM((2, 2, COMB_TM, TOK_SUB, LANES), BF16),
                        pltpu.SemaphoreType.DMA((2, 2)),
                        pltpu.VMEM((H1_SLOTS, COMB_TM, D_MODEL), F32),
                        pltpu.SemaphoreType.DMA((H1_SLOTS,))],
    )
    return pl.pallas_call(
        _combine_kernel,
        grid_spec=grid_spec,
        out_shape=jax.ShapeDtypeStruct((N_TOK, D_MODEL), F32),
        compiler_params=pltpu.CompilerParams(dimension_semantics=("arbitrary",), vmem_limit_bytes=VMEM_LIMIT),
        name="combine",
    )(d0, d1, h1, route, yb)


def kernel(x, meta, norm1_g, w_in, b_in, dw_w, dw_b, conv_ln_g, conv_ln_b, w_conv_out, b_conv_out,
           q_norm_g, k_norm_g, b_forget, w_attn_out, w_out, norm2_g, w_group, b_group, w_router,
           b_router, w_gate, w_up, w_down):
    l = 0
    row = lambda v: v.reshape(1, -1).astype(F32)

    c_u, c_q, c_f = 2 * CONV_WIDTH, 2 * CONV_WIDTH, 2 * CONV_WIDTH + 3 * ATTN_WIDTH
    c_gc = c_f + N_HEADS
    c_ga = c_gc + D_MODEL
    wi, bi = w_in[l], b_in[l]
    wu, bu = wi[:, :c_u].astype(BF16), row(bi[:c_u])
    wqkv, bqkv = wi[:, c_q:c_f].astype(BF16), row(bi[c_q:c_f])
    wft = jnp.zeros((16, D_MODEL), F32).at[:N_HEADS].set(wi[:, c_f:c_gc].T).astype(BF16)
    bft = jnp.zeros((16, 1), F32).at[:N_HEADS, 0].set(bi[c_f:c_gc] + b_forget[l])
    wgc, bgc = wi[:, c_gc:c_ga].astype(BF16), row(bi[c_gc:c_ga])
    wga, bga = wi[:, c_ga:].astype(BF16), row(bi[c_ga:])
    qg = row(jnp.tile(q_norm_g[l], N_HEADS) * (HEAD_DIM ** -0.5 * LOG2E))
    kg = row(jnp.tile(k_norm_g[l], N_HEADS))
    head_of_col = jnp.arange(ATTN_WIDTH) // HEAD_DIM
    hsum = (head_of_col[:, None] == jnp.arange(LANES)[None, :]).astype(BF16)
    hexp_half = (jnp.arange(LANES)[:, None] == head_of_col[None, :]).astype(BF16)
    hexp = jnp.concatenate([hexp_half, hexp_half], axis=0)
    proj_consts = (row(norm1_g[l]), wu, bu, wqkv, bqkv, wft, bft, wgc, bgc, wga, bga, qg, kg, hsum, hexp)

    x2d = x.reshape(N_TOK, D_MODEL)
    meta_pad = jnp.zeros((META_ROWS, D_MODEL), F32).at[:N_META].set(meta.astype(F32))
    a_m, _, k_m, v_m, lf_m, _, _ = _proj(meta_pad, proj_consts, META_ROWS)
    conv_consts = (a_m, _conv_shift_matrix(), dw_w[l].astype(F32), row(dw_b[l]),
                   row(conv_ln_g[l]), row(conv_ln_b[l]))
    act, q, k, v, lf, sgc, sga = _proj(x2d, proj_consts, PROJ_TM, conv_consts)

    o = _attention_t(q, k, v, lf, k_m, v_m, lf_m)

    wrt = jnp.zeros((LANES, D_MODEL), F32).at[:N_GROUPS].set(w_group[l].T)
    wrt = wrt.at[N_GROUPS:N_GROUPS + N_EXPERTS].set(w_router[l].T).astype(BF16)
    brt = jnp.zeros((LANES, 1), F32).at[:N_GROUPS, 0].set(b_group[l])
    brt = brt.at[N_GROUPS:N_GROUPS + N_EXPERTS, 0].set(b_router[l])
    tri = jnp.triu(jnp.ones((MERGE_TM, MERGE_TM), F32), 1).astype(BF16)
    merge_consts = (w_conv_out[l].astype(BF16), row(b_conv_out[l]), w_attn_out[l].astype(BF16),
                    w_out[l].astype(BF16), row(norm2_g[l]), wrt, brt, tri)
    h1, hn, route, route_t, cnt = _merge(act, o, sgc, sga, x2d, merge_consts)

    counts = cnt[:N_EXPERTS, 0].astype(jnp.int32)
    padded = (counts + ROW_BLOCK - 1) // ROW_BLOCK * ROW_BLOCK
    pend = jnp.cumsum(padded).astype(jnp.int32)
    pstart = pend - padded
    blk_row = jnp.arange(N_BLOCKS, dtype=jnp.int32) * ROW_BLOCK
    n_used = pend[-1:] // ROW_BLOCK
    nz = counts > 0
    before = (pend[None, :] <= blk_row[:, None]) & nz[None, :]
    blk_ord = jnp.sum(before.astype(jnp.int32), axis=1)
    blk_first = jnp.any((pstart[None, :] == blk_row[:, None]) & nz[None, :], axis=1).astype(jnp.int32)
    e_ord = jnp.cumsum(nz.astype(jnp.int32)) - 1
    e_ids = jnp.arange(N_EXPERTS, dtype=jnp.int32)
    used_e = jnp.sum(jnp.where((e_ord[None, :] == e_ids[:, None]) & nz[None, :], e_ids[None, :], 0), axis=1)
    n_used_e = jnp.sum(nz.astype(jnp.int32)).reshape(1)

    dest = _dest(pstart, route_t)
    d0, d1 = dest[0], dest[1]
    xin = _dispatch(d0, d1, pstart + counts, pend, hn)
    yb = _experts(n_used, blk_first, blk_ord, used_e.astype(jnp.int32), n_used_e, xin,
                  w_gate[l], w_up[l], w_down[l])
    out = _combine(d0, d1, h1, route, yb)
    return out.reshape(BATCH, SEQ, D_MODEL)
```
